```python
import jax, jax.numpy as jnp
from jax import lax
import numpy as np

D_MODEL = 1024
BATCH = 8
SEQ = 4096
DEPTH = 4

CHUNK = 64
Q_BLOCK = 128

D_MIX = D_MODEL
FOX_HEADS = 8
FOX_HEAD_DIM = 64
FOX_W = FOX_HEADS * FOX_HEAD_DIM
MLA_HEADS = 8
MLA_NOPE = 64
MLA_ROPE = 32
MLA_V = 64
MLA_W = MLA_HEADS * MLA_V
Q_LORA = 256
KV_LORA = 128
ROPE_THETA = 10000.0
EPS = 1e-6

IN_SIZES = (FOX_W, FOX_W, FOX_W, FOX_HEADS, FOX_W,
            Q_LORA, KV_LORA, MLA_ROPE, MLA_W)
N_IN = FOX_W * 4 + FOX_HEADS + Q_LORA + KV_LORA + MLA_ROPE + MLA_W

kernel_name = "hybrid_fox_mla_adaln_trunk"


def rms_norm(x, g):
    xf = x.astype(jnp.float32)
    y = xf * lax.rsqrt(jnp.mean(xf * xf, axis=-1, keepdims=True) + EPS) * g.astype(jnp.float32)
    return y.astype(x.dtype)


def to_blocks(a):
    b, s = a.shape[0], a.shape[1]
    return a.reshape((b, s // Q_BLOCK, Q_BLOCK) + a.shape[2:]).swapaxes(0, 1)


def from_blocks(a):
    nb, b, qb = a.shape[0], a.shape[1], a.shape[2]
    return a.swapaxes(0, 1).reshape((b, nb * qb) + a.shape[3:])


def apply_rope(t, cos, sin):
    tf = t.astype(jnp.float32)
    t1, t2 = jnp.split(tf, 2, axis=-1)
    out = jnp.concatenate([t1 * cos - t2 * sin, t2 * cos + t1 * sin], axis=-1)
    return out.astype(t.dtype)


def fox_attention(q, k, v, log_f):
    s_len = q.shape[1]
    cum = jnp.cumsum(log_f, axis=1)
    cum_k = cum.transpose(0, 2, 1)
    key_idx = jnp.arange(s_len)
    scale = FOX_HEAD_DIM ** -0.5

    def block(args):
        qb, cum_qb, start = args
        s = jnp.einsum('bqhd,bkhd->bhqk', qb, k, preferred_element_type=jnp.float32) * scale
        s = s + cum_qb.transpose(0, 2, 1)[..., None] - cum_k[:, :, None, :]
        q_idx = start + jnp.arange(Q_BLOCK)
        mask = key_idx[None, :] <= q_idx[:, None]
        p = jax.nn.softmax(jnp.where(mask, s, -jnp.inf), axis=-1)
        return jnp.einsum('bhqk,bkhd->bqhd', p.astype(v.dtype), v)

    starts = jnp.arange(s_len // Q_BLOCK, dtype=jnp.int32) * Q_BLOCK
    out = lax.map(block, (to_blocks(q), to_blocks(cum), starts))
    return from_blocks(out)


def mla_attention(q_nope, q_rope, k_nope, k_rope, v):
    s_len = q_nope.shape[1]
    key_chunk = jnp.arange(s_len) // CHUNK
    scale = (MLA_NOPE + MLA_ROPE) ** -0.5

    def block(args):
        qn, qr, start = args
        s = jnp.einsum('bqhd,bkhd->bhqk', qn, k_nope, preferred_element_type=jnp.float32)
        s = s + jnp.einsum('bqhr,bkr->bhqk', qr, k_rope, preferred_element_type=jnp.float32)
        q_chunk = (start + jnp.arange(Q_BLOCK)) // CHUNK
        mask = key_chunk[None, :] <= q_chunk[:, None]
        p = jax.nn.softmax(jnp.where(mask, s * scale, -jnp.inf), axis=-1)
        return jnp.einsum('bhqk,bkhd->bqhd', p.astype(v.dtype), v)

    starts = jnp.arange(s_len // Q_BLOCK, dtype=jnp.int32) * Q_BLOCK
    out = lax.map(block, (to_blocks(q_nope), to_blocks(q_rope), starts))
    return from_blocks(out)


def _fwd_setup_inputs(seed: int = 0) -> dict:
    key = jax.random.key(seed)
    ks = jax.random.split(key, 16)
    f32 = jnp.float32
    x = jax.random.normal(ks[0], (BATCH, SEQ, D_MODEL), f32)
    c = jax.random.normal(ks[1], (BATCH, D_MODEL), f32)
    offset = jax.random.randint(ks[2], (BATCH,), 0, 16, dtype=jnp.int32) * CHUNK
    positions = (offset[:, None] + jnp.arange(SEQ, dtype=jnp.int32)[None, :]).astype(jnp.int32)
    norm_g = 1.0 + 0.02 * jax.random.normal(ks[3], (DEPTH, D_MODEL), f32)
    w_ada = 0.5 * jax.random.normal(ks[4], (DEPTH, D_MODEL, 3 * D_MODEL), f32) * D_MODEL ** -0.5
    b_ada = 0.02 * jax.random.normal(ks[5], (DEPTH, 3 * D_MODEL), f32)
    w_in = jax.random.normal(ks[6], (DEPTH, D_MODEL, N_IN), f32) * D_MODEL ** -0.5
    b_f = jax.random.uniform(ks[7], (DEPTH, FOX_HEADS), f32, 1.0, 4.0)
    q_norm_g = 1.0 + 0.02 * jax.random.normal(ks[8], (DEPTH, Q_LORA), f32)
    w_uq = jax.random.normal(ks[9], (DEPTH, Q_LORA, MLA_HEADS * (MLA_NOPE + MLA_ROPE)), f32) * Q_LORA ** -0.5
    kv_norm_g = 1.0 + 0.02 * jax.random.normal(ks[10], (DEPTH, KV_LORA), f32)
    w_ukv = jax.random.normal(ks[11], (DEPTH, KV_LORA, MLA_HEADS * (MLA_NOPE + MLA_V)), f32) * KV_LORA ** -0.5
    w_out = jax.random.normal(ks[12], (DEPTH, D_MIX, D_MODEL), f32) * D_MIX ** -0.5
    final_g = 1.0 + 0.02 * jax.random.normal(ks[13], (D_MODEL,), f32)
    return {"x": x, "c": c, "positions": positions, "norm_g": norm_g, "w_ada": w_ada,
            "b_ada": b_ada, "w_in": w_in, "b_f": b_f, "q_norm_g": q_norm_g, "w_uq": w_uq,
            "kv_norm_g": kv_norm_g, "w_ukv": w_ukv, "w_out": w_out, "final_g": final_g}


def _fwd_reference(x, c, positions, norm_g, w_ada, b_ada, w_in, b_f, q_norm_g, w_uq,
              kv_norm_g, w_ukv, w_out, final_g):
    b, s_len, _ = x.shape
    splits = [int(i) for i in np.cumsum(IN_SIZES)[:-1]]

    inv_freq = 1.0 / (ROPE_THETA ** (jnp.arange(0, MLA_ROPE, 2, dtype=jnp.float32) / MLA_ROPE))
    ang = positions.astype(jnp.float32)[..., None] * inv_freq
    cos, sin = jnp.cos(ang), jnp.sin(ang)

    c_act = jax.nn.silu(c)
    for l in range(DEPTH):
        mod = c_act @ w_ada[l] + b_ada[l]
        shift, scale, gate = jnp.split(mod, 3, axis=-1)
        h = rms_norm(x, norm_g[l]) * (1.0 + scale[:, None, :]) + shift[:, None, :]

        z = h @ w_in[l]
        fq, fk, fv, ff, fg, q_lat, kv_lat, k_r, mg = jnp.split(z, splits, axis=-1)

        log_f = jax.nn.log_sigmoid(ff.astype(jnp.float32) + b_f[l].astype(jnp.float32))
        y_fox = fox_attention(fq.reshape(b, s_len, FOX_HEADS, FOX_HEAD_DIM),
                              fk.reshape(b, s_len, FOX_HEADS, FOX_HEAD_DIM),
                              fv.reshape(b, s_len, FOX_HEADS, FOX_HEAD_DIM), log_f)
        y_fox = y_fox.reshape(b, s_len, FOX_W) * jax.nn.silu(fg)

        q = (rms_norm(q_lat, q_norm_g[l]) @ w_uq[l]).reshape(b, s_len, MLA_HEADS, MLA_NOPE + MLA_ROPE)
        q_nope, q_rope = q[..., :MLA_NOPE], q[..., MLA_NOPE:]
        q_rope = apply_rope(q_rope, cos[:, :, None, :], sin[:, :, None, :])
        kv = (rms_norm(kv_lat, kv_norm_g[l]) @ w_ukv[l]).reshape(b, s_len, MLA_HEADS, MLA_NOPE + MLA_V)
        k_nope, v = kv[..., :MLA_NOPE], kv[..., MLA_NOPE:]
        k_rope = apply_rope(k_r, cos, sin)
        y_mla = mla_attention(q_nope, q_rope, k_nope, k_rope, v)
        y_mla = y_mla.reshape(b, s_len, MLA_W) * jax.nn.silu(mg)

        y = jnp.concatenate([y_fox, y_mla], axis=-1) @ w_out[l]
        x = x + gate[:, None, :] * y

    return rms_norm(x, final_g)


import jax as _jax
import jax.numpy as _jnp

TWIN_FORMAT = 'train_step'
FWD_PARAMS = ['x', 'c', 'positions', 'norm_g', 'w_ada', 'b_ada', 'w_in', 'b_f', 'q_norm_g', 'w_uq', 'kv_norm_g', 'w_ukv', 'w_out', 'final_g']
TWIN_WEIGHTS = ['norm_g', 'w_ada', 'b_ada', 'w_in', 'b_f', 'q_norm_g', 'w_uq', 'kv_norm_g', 'w_ukv', 'w_out', 'final_g']
TWIN_DIFF_INPUT = 'x'
TWIN_INPUTS = ['x', 'c', 'positions', 'norm_g', 'w_ada', 'b_ada', 'w_in', 'b_f', 'q_norm_g', 'w_uq', 'kv_norm_g', 'w_ukv', 'w_out', 'final_g', 'loss_target', 'm_norm_g', 'm_w_ada', 'm_b_ada', 'm_w_in', 'm_b_f', 'm_q_norm_g', 'm_w_uq', 'm_kv_norm_g', 'm_w_ukv', 'm_w_out', 'm_final_g', 'v_norm_g', 'v_w_ada', 'v_b_ada', 'v_w_in', 'v_b_f', 'v_q_norm_g', 'v_w_uq', 'v_kv_norm_g', 'v_w_ukv', 'v_w_out', 'v_final_g']
TWIN_OUTPUTS = ['loss', 'grad_x', 'grad_norm_g', 'grad_w_ada', 'grad_b_ada', 'grad_w_in', 'grad_b_f', 'grad_q_norm_g', 'grad_w_uq', 'grad_kv_norm_g', 'grad_w_ukv', 'grad_w_out', 'grad_final_g', 'delta_norm_g', 'delta_w_ada', 'delta_b_ada', 'delta_w_in', 'delta_b_f', 'delta_q_norm_g', 'delta_w_uq', 'delta_kv_norm_g', 'delta_w_ukv', 'delta_w_out', 'delta_final_g', 'new_m_norm_g', 'new_m_w_ada', 'new_m_b_ada', 'new_m_w_in', 'new_m_b_f', 'new_m_q_norm_g', 'new_m_w_uq', 'new_m_kv_norm_g', 'new_m_w_ukv', 'new_m_w_out', 'new_m_final_g', 'new_v_norm_g', 'new_v_w_ada', 'new_v_b_ada', 'new_v_w_in', 'new_v_b_f', 'new_v_q_norm_g', 'new_v_w_uq', 'new_v_kv_norm_g', 'new_v_w_ukv', 'new_v_w_out', 'new_v_final_g']
TWIN_LEAF_KINDS = {'loss': 'loss', 'grad_x': 'grad_x', 'grad_norm_g': 'grad_w', 'grad_w_ada': 'grad_w', 'grad_b_ada': 'grad_w', 'grad_w_in': 'grad_w', 'grad_b_f': 'grad_w', 'grad_q_norm_g': 'grad_w', 'grad_w_uq': 'grad_w', 'grad_kv_norm_g': 'grad_w', 'grad_w_ukv': 'grad_w', 'grad_w_out': 'grad_w', 'grad_final_g': 'grad_w', 'delta_norm_g': 'delta_w', 'delta_w_ada': 'delta_w', 'delta_b_ada': 'delta_w', 'delta_w_in': 'delta_w', 'delta_b_f': 'delta_w', 'delta_q_norm_g': 'delta_w', 'delta_w_uq': 'delta_w', 'delta_kv_norm_g': 'delta_w', 'delta_w_ukv': 'delta_w', 'delta_w_out': 'delta_w', 'delta_final_g': 'delta_w', 'new_m_norm_g': 'new_m', 'new_m_w_ada': 'new_m', 'new_m_b_ada': 'new_m', 'new_m_w_in': 'new_m', 'new_m_b_f': 'new_m', 'new_m_q_norm_g': 'new_m', 'new_m_w_uq': 'new_m', 'new_m_kv_norm_g': 'new_m', 'new_m_w_ukv': 'new_m', 'new_m_w_out': 'new_m', 'new_m_final_g': 'new_m', 'new_v_norm_g': 'new_v', 'new_v_w_ada': 'new_v', 'new_v_b_ada': 'new_v', 'new_v_w_in': 'new_v', 'new_v_b_f': 'new_v', 'new_v_q_norm_g': 'new_v', 'new_v_w_uq': 'new_v', 'new_v_kv_norm_g': 'new_v', 'new_v_w_ukv': 'new_v', 'new_v_w_out': 'new_v', 'new_v_final_g': 'new_v'}


def _forward(args):
    return _fwd_reference(*[args[k] for k in FWD_PARAMS])


def _output_shape():
    out = _jax.eval_shape(lambda: _forward(_fwd_setup_inputs(0)))
    return out.shape, out.dtype

N_MICROBATCH = 1
ADAM_LR = 0.001
ADAM_B1 = 0.9
ADAM_B2 = 0.999
ADAM_EPS = 1e-08
ADAM_WD = 0.01
ADAM_STEP = 10
PER_EXAMPLE_BATCH_AXIS = {'x': 0, 'c': 0, 'positions': 0, 'loss_target': 0}
SHARED_INPUTS = []
_WEIGHT_DTYPES = {'norm_g': _jnp.float32, 'w_ada': _jnp.float32, 'b_ada': _jnp.float32, 'w_in': _jnp.float32, 'b_f': _jnp.float32, 'q_norm_g': _jnp.float32, 'w_uq': _jnp.float32, 'kv_norm_g': _jnp.float32, 'w_ukv': _jnp.float32, 'w_out': _jnp.float32, 'final_g': _jnp.float32}
MOMENT_SCALE = {'norm_g': 2.482716e-02, 'w_ada': 2.483895e-02, 'b_ada': 4.034147e-02, 'w_in': 1.588845e-02, 'b_f': 7.148931e-02, 'q_norm_g': 7.735949e-03, 'w_uq': 4.374203e-03, 'kv_norm_g': 2.643855e-02, 'w_ukv': 8.715713e-03, 'w_out': 1.612447e-02, 'final_g': 3.199725e+01}


def _to_microbatches(a, axis):
    t = _jnp.moveaxis(a, axis, 0)
    t = t.reshape((N_MICROBATCH, t.shape[0] // N_MICROBATCH) + t.shape[1:])
    return _jnp.moveaxis(t, 1, axis + 1)


def setup_inputs(seed: int = 0) -> dict:
    inp = _fwd_setup_inputs(seed)
    key = _jax.random.fold_in(_jax.random.key(seed), 7919)
    shape, _ = _output_shape()
    out = dict(inp)
    out["loss_target"] = _jax.random.normal(_jax.random.fold_in(key, 0), shape, _jnp.float32)
    for i, name in enumerate(TWIN_WEIGHTS):
        w = inp[name].astype(_jnp.float32)
        if MOMENT_SCALE is None:
            s = _jnp.sqrt(_jnp.mean(_jnp.square(w)) + 1e-30)
        else:
            s = MOMENT_SCALE[name]
        km, kv = _jax.random.split(_jax.random.fold_in(key, i + 1))
        out[name] = w
        out["m_" + name] = s * _jax.random.normal(km, w.shape, _jnp.float32)
        out["v_" + name] = (s * s) * _jax.random.uniform(kv, w.shape, _jnp.float32, 0.5, 1.5)
    if N_MICROBATCH > 1:
        for name, axis in PER_EXAMPLE_BATCH_AXIS.items():
            out[name] = _to_microbatches(out[name], axis)
    return {'x': out['x'], 'c': out['c'], 'positions': out['positions'], 'norm_g': out['norm_g'], 'w_ada': out['w_ada'], 'b_ada': out['b_ada'], 'w_in': out['w_in'], 'b_f': out['b_f'], 'q_norm_g': out['q_norm_g'], 'w_uq': out['w_uq'], 'kv_norm_g': out['kv_norm_g'], 'w_ukv': out['w_ukv'], 'w_out': out['w_out'], 'final_g': out['final_g'], 'loss_target': out['loss_target'], 'm_norm_g': out['m_norm_g'], 'm_w_ada': out['m_w_ada'], 'm_b_ada': out['m_b_ada'], 'm_w_in': out['m_w_in'], 'm_b_f': out['m_b_f'], 'm_q_norm_g': out['m_q_norm_g'], 'm_w_uq': out['m_w_uq'], 'm_kv_norm_g': out['m_kv_norm_g'], 'm_w_ukv': out['m_w_ukv'], 'm_w_out': out['m_w_out'], 'm_final_g': out['m_final_g'], 'v_norm_g': out['v_norm_g'], 'v_w_ada': out['v_w_ada'], 'v_b_ada': out['v_b_ada'], 'v_w_in': out['v_w_in'], 'v_b_f': out['v_b_f'], 'v_q_norm_g': out['v_q_norm_g'], 'v_w_uq': out['v_w_uq'], 'v_kv_norm_g': out['v_kv_norm_g'], 'v_w_ukv': out['v_w_ukv'], 'v_w_out': out['v_w_out'], 'v_final_g': out['v_final_g']}


def _loss(weights, diff, rest, loss_target):
    with _jax.named_scope("forward"):
        args = {**rest, TWIN_DIFF_INPUT: diff, **{k: w.astype(_WEIGHT_DTYPES[k]) for k, w in weights.items()}}
        y = _forward(args)
    with _jax.named_scope("loss_head"):
        err = _jnp.square(y.astype(_jnp.float32) - loss_target)
        return 0.5 * _jnp.sum(_jnp.mean(err, axis=-1)) if err.ndim else 0.5 * err


def _adamw(w, g, m, v):
    m = ADAM_B1 * m + (1.0 - ADAM_B1) * g
    v = ADAM_B2 * v + (1.0 - ADAM_B2) * _jnp.square(g)
    m_hat = m / (1.0 - ADAM_B1 ** ADAM_STEP)
    v_hat = v / (1.0 - ADAM_B2 ** ADAM_STEP)
    delta = -ADAM_LR * (m_hat / (_jnp.sqrt(v_hat) + ADAM_EPS) + ADAM_WD * w)
    return delta, m, v


def reference(x, c, positions, norm_g, w_ada, b_ada, w_in, b_f, q_norm_g, w_uq, kv_norm_g, w_ukv, w_out, final_g, loss_target, m_norm_g, m_w_ada, m_b_ada, m_w_in, m_b_f, m_q_norm_g, m_w_uq, m_kv_norm_g, m_w_ukv, m_w_out, m_final_g, v_norm_g, v_w_ada, v_b_ada, v_w_in, v_b_f, v_q_norm_g, v_w_uq, v_kv_norm_g, v_w_ukv, v_w_out, v_final_g):
    given = dict(x=x, c=c, positions=positions, norm_g=norm_g, w_ada=w_ada, b_ada=b_ada, w_in=w_in, b_f=b_f, q_norm_g=q_norm_g, w_uq=w_uq, kv_norm_g=kv_norm_g, w_ukv=w_ukv, w_out=w_out, final_g=final_g, loss_target=loss_target, m_norm_g=m_norm_g, m_w_ada=m_w_ada, m_b_ada=m_b_ada, m_w_in=m_w_in, m_b_f=m_b_f, m_q_norm_g=m_q_norm_g, m_w_uq=m_w_uq, m_kv_norm_g=m_kv_norm_g, m_w_ukv=m_w_ukv, m_w_out=m_w_out, m_final_g=m_final_g, v_norm_g=v_norm_g, v_w_ada=v_w_ada, v_b_ada=v_b_ada, v_w_in=v_w_in, v_b_f=v_b_f, v_q_norm_g=v_q_norm_g, v_w_uq=v_w_uq, v_kv_norm_g=v_kv_norm_g, v_w_ukv=v_w_ukv, v_w_out=v_w_out, v_final_g=v_final_g)
    weights = {n: given[n] for n in TWIN_WEIGHTS}
    shared = {n: given[n] for n in SHARED_INPUTS}
    per_example = {n: given[n] for n in ['x', 'c', 'positions']}
    grad_fn = _jax.value_and_grad(_loss, argnums=(0, 1))

    def one_microbatch(ex, loss_target):
        ex = dict(ex)
        diff = ex.pop(TWIN_DIFF_INPUT)
        return grad_fn(weights, diff, {**shared, **ex}, loss_target)

    if N_MICROBATCH == 1:
        loss, (grad_w, grad_x) = one_microbatch(per_example, given["loss_target"])
    else:
        def body(carry, xs):
            loss_sum, grad_sum = carry
            l_k, (gw_k, gx_k) = one_microbatch(xs[0], xs[1])
            with _jax.named_scope("update"):
                return (loss_sum + l_k, _jax.tree.map(_jnp.add, grad_sum, gw_k)), gx_k

        init = (_jnp.zeros((), _jnp.float32), _jax.tree.map(_jnp.zeros_like, weights))
        (loss, grad_w), grad_x = _jax.lax.scan(body, init, (per_example, given["loss_target"]))
    with _jax.named_scope("update"):
        delta_w, new_m, new_v = {}, {}, {}
        for n in TWIN_WEIGHTS:
            delta_w[n], new_m[n], new_v[n] = _adamw(weights[n], grad_w[n], given["m_" + n], given["v_" + n])
    return (loss, grad_x, *[grad_w[n] for n in TWIN_WEIGHTS], *[delta_w[n] for n in TWIN_WEIGHTS],
            *[new_m[n] for n in TWIN_WEIGHTS], *[new_v[n] for n in TWIN_WEIGHTS])
```

```python
import functools
import types

import jax
import jax.numpy as jnp
from jax import lax
from jax.experimental import pallas as pl
from jax.experimental.pallas import tpu as pltpu

F32 = jnp.float32
BF16 = jnp.bfloat16
MESH = pl.DeviceIdType.MESH

N_CHIPS = 4
N_DEV = 8
HEAD_DIM = 64
ROPE_DIM = 32
ROPE_THETA = 10000.0
HEAD_PAD = 128
ROPE_LO = 64
ROPE_HALF = 16
LANES = 128
EPS = 1e-6
NEG = -1e30
ADAM_LR = 0.001
ADAM_B1 = 0.9
ADAM_B2 = 0.999
ADAM_EPS = 1e-08
ADAM_WD = 0.01
ADAM_STEP = 10
VMEM_LIMIT = 48 * 1024 * 1024
SLAB_COLS = 1024


def _params(sem=None, vmem=VMEM_LIMIT):
    return pltpu.CompilerParams(dimension_semantics=sem, vmem_limit_bytes=vmem)


def _nn(a, b):
    return jnp.dot(a, b, preferred_element_type=F32)


def _nt(a, b):
    return lax.dot_general(a, b, (((1,), (1,)), ((), ())), preferred_element_type=F32)


def _tn(a, b):
    return lax.dot_general(a, b, (((0,), (0,)), ((), ())), preferred_element_type=F32)


def _sigmoid(x):
    return 1.0 / (1.0 + jnp.exp(-x))


def _lane(shape):
    return lax.broadcasted_iota(jnp.int32, shape, len(shape) - 1)


def _pick(n, cands):
    for c in cands:
        if n % c == 0:
            return c
    return n


def _my_pos():
    return lax.axis_index("x"), lax.axis_index("y"), lax.axis_index("c")


def allgather8(xs, name):
    m_per, n = xs.shape

    def body(x_ref, out_ref, send_sems, recv_sems, local_sem):
        x, y, c = _my_pos()
        me, sibling = (x, y, c), (x, y, 1 - c)
        chips = [(1 - x, y), (x, 1 - y), (1 - x, 1 - y)]

        def rows(px, py, pc):
            return out_ref.at[pl.ds((4 * px + 2 * py + pc) * m_per, m_per), :]

        def copy(k, block, to, src=None):
            return pltpu.make_async_remote_copy(
                src_ref=rows(*block) if src is None else src, dst_ref=rows(*block),
                send_sem=send_sems.at[k], recv_sem=recv_sems.at[k], device_id=to, device_id_type=MESH)

        mine = pltpu.make_async_copy(x_ref, rows(*me), local_sem)
        mine.start()
        first = [copy(0, me, sibling, src=x_ref)]
        first += [copy(1 + j, me, (*chip, c), src=x_ref) for j, chip in enumerate(chips)]
        for cp in first:
            cp.start()
        passed = [copy(4 + j, (*chip, c), sibling) for j, chip in enumerate(chips)]
        for j, chip in enumerate(chips):
            copy(1 + j, (*chip, c), me).wait_recv()
            passed[j].start()
        copy(0, sibling, me).wait_recv()
        for j, chip in enumerate(chips):
            copy(4 + j, (*chip, 1 - c), me).wait_recv()
        for cp in first + passed:
            cp.wait_send()
        mine.wait()

    return pl.pallas_call(
        body, name=name,
        out_shape=jax.ShapeDtypeStruct((N_DEV * m_per, n), xs.dtype),
        in_specs=[pl.BlockSpec(memory_space=pltpu.VMEM)],
        out_specs=pl.BlockSpec(memory_space=pltpu.VMEM),
        scratch_shapes=[pltpu.SemaphoreType.DMA((7,)), pltpu.SemaphoreType.DMA((7,)), pltpu.SemaphoreType.DMA],
    )(xs)


def chip_exchange(xs, bcast, name):
    r, n = xs.shape[-2:]

    def body(x_ref, out_ref, send_sems, recv_sems, local_sem):
        x, y, c = _my_pos()
        s = 2 * x + y
        chips = [(1 - x, y), (x, 1 - y), (1 - x, 1 - y)]

        def src(t):
            return x_ref if bcast else x_ref.at[t]

        local = pltpu.make_async_copy(src(s), out_ref.at[s], local_sem)
        local.start()
        sends = []
        for k, (tx, ty) in enumerate(chips):
            cp = pltpu.make_async_remote_copy(
                src_ref=src(2 * tx + ty), dst_ref=out_ref.at[s], send_sem=send_sems.at[k],
                recv_sem=recv_sems.at[k], device_id=(tx, ty, c), device_id_type=MESH)
            cp.start()
            sends.append(cp)
        for k, (tx, ty) in enumerate(chips):
            t = 2 * tx + ty
            pltpu.make_async_remote_copy(
                src_ref=src(t), dst_ref=out_ref.at[t], send_sem=send_sems.at[k],
                recv_sem=recv_sems.at[k], device_id=(tx, ty, c), device_id_type=MESH).wait_recv()
        for cp in sends:
            cp.wait_send()
        local.wait()

    return pl.pallas_call(
        body, name=name,
        out_shape=jax.ShapeDtypeStruct((N_CHIPS, r, n), xs.dtype),
        in_specs=[pl.BlockSpec(memory_space=pl.ANY)],
        out_specs=pl.BlockSpec(memory_space=pl.ANY),
        scratch_shapes=[pltpu.SemaphoreType.DMA((3,)), pltpu.SemaphoreType.DMA((3,)), pltpu.SemaphoreType.DMA],
    )(xs)


def sibling_swap(xs, name):
    def body(x_ref, out_ref, send_sem, recv_sem):
        x, y, c = _my_pos()
        cp = pltpu.make_async_remote_copy(
            src_ref=x_ref, dst_ref=out_ref, send_sem=send_sem, recv_sem=recv_sem,
            device_id=(x, y, 1 - c), device_id_type=MESH)
        cp.start()
        cp.wait()

    return pl.pallas_call(
        body, name=name,
        out_shape=jax.ShapeDtypeStruct(xs.shape, xs.dtype),
        in_specs=[pl.BlockSpec(memory_space=pl.ANY)],
        out_specs=pl.BlockSpec(memory_space=pl.ANY),
        scratch_shapes=[pltpu.SemaphoreType.DMA, pltpu.SemaphoreType.DMA],
    )(xs)


def sum_leading(xs, name):
    n, r, c = xs.shape

    def body(x_ref, o_ref):
        acc = x_ref[0]
        for i in range(1, n):
            acc = acc + x_ref[i]
        o_ref[...] = acc

    return pl.pallas_call(body, name=name, out_shape=jax.ShapeDtypeStruct((r, c), xs.dtype))(xs)


def add_cast(a, b, out_dtype, name):
    r, c = a.shape
    tr = _pick(r, (512, 256, 128, 64, 32, 16))

    def body(a_ref, b_ref, o_ref):
        o_ref[...] = (a_ref[...] + b_ref[...]).astype(out_dtype)

    spec = pl.BlockSpec((tr, c), lambda i: (i, 0))
    return pl.pallas_call(body, name=name, grid=(r // tr,), in_specs=[spec, spec], out_specs=spec,
                          out_shape=jax.ShapeDtypeStruct((r, c), out_dtype), compiler_params=_params(("parallel",)))(a, b)


def sum_chips(xs, name):
    n, r, c = xs.shape
    tr = _pick(r, (512, 256, 128, 64, 32, 16))

    def body(x_ref, o_ref):
        acc = x_ref[0].astype(F32)
        for i in range(1, n):
            acc = acc + x_ref[i].astype(F32)
        o_ref[...] = acc

    return pl.pallas_call(body, name=name, grid=(r // tr,),
                          in_specs=[pl.BlockSpec((n, tr, c), lambda i: (0, i, 0))],
                          out_specs=pl.BlockSpec((tr, c), lambda i: (i, 0)),
                          out_shape=jax.ShapeDtypeStruct((r, c), F32), compiler_params=_params(("parallel",)))(xs)


def ada_forward(c_all, w_ada):
    nl, d, n = w_ada.shape
    nb = c_all.shape[0]

    def body(c_ref, w_ref, o_ref):
        cv = c_ref[...]
        ca = (cv * _sigmoid(cv)).astype(BF16)
        o_ref[0] = _nn(ca, w_ref[0].astype(BF16))

    return pl.pallas_call(
        body, name="ada_forward", grid=(nl,),
        in_specs=[pl.BlockSpec((nb, d), lambda l: (0, 0)), pl.BlockSpec((1, d, n), lambda l: (l, 0, 0))],
        out_specs=pl.BlockSpec((1, nb, n), lambda l: (l, 0, 0)),
        out_shape=jax.ShapeDtypeStruct((nl, nb, n), F32), compiler_params=_params(("parallel",)))(c_all, w_ada)


def ada_backward(c_all, dmod):
    nl, nb, n = dmod.shape
    d = c_all.shape[1]

    def body(c_ref, g_ref, o_ref):
        cv = c_ref[...]
        ca = (cv * _sigmoid(cv)).astype(BF16)
        o_ref[0] = _tn(ca, g_ref[0].astype(BF16))

    return pl.pallas_call(
        body, name="ada_backward", grid=(nl,),
        in_specs=[pl.BlockSpec((nb, d), lambda l: (0, 0)), pl.BlockSpec((1, nb, n), lambda l: (l, 0, 0))],
        out_specs=pl.BlockSpec((1, d, n), lambda l: (l, 0, 0)),
        out_shape=jax.ShapeDtypeStruct((nl, d, n), F32), compiler_params=_params(("parallel",)))(c_all, dmod)


def matmul_tn(a, b, name):
    k, m = a.shape
    n = b.shape[1]
    tm, tn, tk = _pick(m, (512, 256, 128)), _pick(n, (512, 256, 128)), _pick(k, (512, 256, 128))

    def body(a_ref, b_ref, o_ref):
        @pl.when(pl.program_id(2) == 0)
        def _():
            o_ref[...] = jnp.zeros_like(o_ref)

        o_ref[...] += _tn(a_ref[...], b_ref[...])

    return pl.pallas_call(
        body, name=name, grid=(m // tm, n // tn, k // tk),
        in_specs=[pl.BlockSpec((tk, tm), lambda i, j, kk: (kk, i)), pl.BlockSpec((tk, tn), lambda i, j, kk: (kk, j))],
        out_specs=pl.BlockSpec((tm, tn), lambda i, j, kk: (i, j)),
        out_shape=jax.ShapeDtypeStruct((m, n), F32),
        compiler_params=_params(("parallel", "parallel", "arbitrary")))(a, b)


def adamw(w, g, m, v, name):
    r, c = w.shape
    tr = _pick(r, (512, 256, 128, 64, 32, 16, 8))

    def body(w_ref, g_ref, m_ref, v_ref, d_ref, mo_ref, vo_ref):
        gv = g_ref[...]
        mn = ADAM_B1 * m_ref[...] + (1.0 - ADAM_B1) * gv
        vn = ADAM_B2 * v_ref[...] + (1.0 - ADAM_B2) * (gv * gv)
        m_hat = mn / (1.0 - ADAM_B1 ** ADAM_STEP)
        v_hat = vn / (1.0 - ADAM_B2 ** ADAM_STEP)
        d_ref[...] = -ADAM_LR * (m_hat / (jnp.sqrt(v_hat) + ADAM_EPS) + ADAM_WD * w_ref[...])
        mo_ref[...] = mn
        vo_ref[...] = vn

    spec = pl.BlockSpec((tr, c), lambda i: (i, 0))
    out = jax.ShapeDtypeStruct((r, c), F32)
    return pl.pallas_call(body, name=name, grid=(r // tr,), in_specs=[spec] * 4, out_specs=[spec] * 3,
                          out_shape=[out] * 3, compiler_params=_params(("parallel",)))(w, g, m, v)


def _rope(t, cos_t, sin_t):
    w = t.shape[1]
    lane = _lane(t.shape) & (HEAD_PAD - 1)
    first_half = (lane >= ROPE_LO) & (lane < ROPE_LO + ROPE_HALF)
    partner = jnp.where(first_half, pltpu.roll(t, w - ROPE_HALF, 1), pltpu.roll(t, ROPE_HALF, 1))
    return t * cos_t + partner * sin_t


def _rope_t(dt, cos_t, sin_t):
    w = dt.shape[1]
    lane = _lane(dt.shape) & (HEAD_PAD - 1)
    first_half = (lane >= ROPE_LO) & (lane < ROPE_LO + ROPE_HALF)
    ds = dt * sin_t
    partner = jnp.where(first_half, pltpu.roll(ds, w - ROPE_HALF, 1), pltpu.roll(ds, ROPE_HALF, 1))
    return dt * cos_t + partner


def _rms(xv, g):
    rstd = lax.rsqrt(jnp.mean(xv * xv, axis=-1, keepdims=True) + EPS)
    xh = xv * rstd
    return xh * g, xh, rstd


def _rms_bwd(dy, g, xh, rstd):
    dxh = dy * g
    dx = rstd * (dxh - xh * jnp.mean(dxh * xh, axis=-1, keepdims=True))
    return dx, jnp.sum(dy * xh, axis=0, keepdims=True)


def make_step(cfg):
    S, D, NZ = cfg.S, cfg.D, cfg.NZ
    FW, MW, QL, KVL, FH, MH = cfg.FW, cfg.MW, cfg.QL, cfg.KVL, cfg.FH, cfg.MH
    QW = MH * HEAD_PAD
    TM = _pick(S, (256, 128))
    TQ = TK = _pick(S, (256, 128))
    n_tok = S // TM
    misc_blk = cfg.o_ms // LANES

    def tok(width, col=0):
        return pl.BlockSpec((TM, width), lambda i: (i, col))

    def const(shape):
        return pl.BlockSpec(shape, lambda i: tuple(0 for _ in shape))


    def ln_inproj(x, g, scale, shift, w_in):
        def body(x_ref, g_ref, sc_ref, sh_ref, w_ref, h_ref, z_ref):
            y, _, _ = _rms(x_ref[...], g_ref[...])
            hb = (y * (1.0 + sc_ref[...]) + sh_ref[...]).astype(BF16)
            h_ref[...] = hb
            z_ref[...] = _nn(hb, w_ref[...])

        return pl.pallas_call(
            body, name="ln_inproj", grid=(n_tok,),
            in_specs=[tok(D), const((1, D)), const((1, D)), const((1, D)), const((D, NZ))],
            out_specs=[tok(D), tok(NZ)],
            out_shape=[jax.ShapeDtypeStruct((S, D), BF16), jax.ShapeDtypeStruct((S, NZ), F32)],
            compiler_params=_params(("parallel",)))(x, g, scale, shift, w_in)

    def _log_f_terms(misc, bf):
        lane = _lane(misc.shape)
        a = misc + bf
        e = jnp.exp(-jnp.abs(a))
        logf = jnp.minimum(a, 0.0) - jnp.log(1.0 + e)
        sig_neg = jnp.where(a >= 0, e, 1.0) / (1.0 + e)
        valid = lane < FH
        return jnp.where(valid, logf, 0.0), jnp.where(valid, sig_neg, 0.0)

    def fox_prep(z, bf_pad):
        def body(z_ref, b_ref, o_ref, carry):
            @pl.when(pl.program_id(0) == 0)
            def _():
                carry[...] = jnp.zeros_like(carry)

            logf, _ = _log_f_terms(z_ref[...], b_ref[...])
            row = lax.broadcasted_iota(jnp.int32, (TM, TM), 0)
            col = lax.broadcasted_iota(jnp.int32, (TM, TM), 1)
            tri = (col <= row).astype(F32)
            cum = jnp.dot(tri, logf, precision=lax.Precision.HIGHEST, preferred_element_type=F32) + carry[...]
            o_ref[...] = cum
            carry[...] = cum[TM - 1:TM, :]

        return pl.pallas_call(
            body, name="fox_prep", grid=(n_tok,),
            in_specs=[tok(LANES, misc_blk), const((1, LANES))], out_specs=tok(LANES),
            out_shape=jax.ShapeDtypeStruct((S, LANES), F32),
            scratch_shapes=[pltpu.VMEM((1, LANES), F32)],
            compiler_params=_params(("arbitrary",)))(z, bf_pad)

    def mla_prep(z, gq, gkv, w_uq, w_uk, w_v, cos_t, sin_t):
        def body(ql_ref, kvl_ref, ms_ref, gq_ref, gkv_ref, wq_ref, wk_ref, wv_ref, c_ref, s_ref, q_ref, k_ref, v_ref):
            cos1, sin1 = c_ref[...], s_ref[...]
            qn, _, _ = _rms(ql_ref[...], gq_ref[...])
            q = _nn(qn.astype(BF16), wq_ref[...])
            q_ref[...] = _rope(q, jnp.tile(cos1, (1, MH)), jnp.tile(sin1, (1, MH))).astype(BF16)
            kvn, _, _ = _rms(kvl_ref[...], gkv_ref[...])
            kvb = kvn.astype(BF16)
            lane = _lane((TM, LANES))
            kr = jnp.where((lane >= ROPE_LO) & (lane < ROPE_LO + ROPE_DIM), ms_ref[...], 0.0)
            kr = _rope(kr, cos1, sin1)
            k_ref[...] = (_nn(kvb, wk_ref[...]) + jnp.tile(kr, (1, MH))).astype(BF16)
            v_ref[...] = _nn(kvb, wv_ref[...]).astype(BF16)

        return pl.pallas_call(
            body, name="mla_prep", grid=(n_tok,),
            in_specs=[tok(QL, cfg.o_ql // QL), tok(KVL, cfg.o_kv // KVL), tok(LANES, misc_blk),
                      const((1, QL)), const((1, KVL)), const((QL, QW)), const((KVL, QW)), const((KVL, MW)),
                      tok(LANES), tok(LANES)],
            out_specs=[tok(QW), tok(QW), tok(MW)],
            out_shape=[jax.ShapeDtypeStruct((S, QW), BF16), jax.ShapeDtypeStruct((S, QW), BF16),
                       jax.ShapeDtypeStruct((S, MW), BF16)],
            compiler_params=_params(("parallel",)))(z, z, z, gq, gkv, w_uq, w_uk, w_v, cos_t, sin_t)

    def _allowed(q0, k0, chunked):
        qi = q0 + lax.broadcasted_iota(jnp.int32, (TQ, TK), 0)
        ki = k0 + lax.broadcasted_iota(jnp.int32, (TQ, TK), 1)
        if chunked:
            return (ki >> 6) <= (qi >> 6)
        return ki <= qi

    def _head(ref_or_val, j, packed, rows=None):
        if packed:
            val = ref_or_val[...] if rows is None else ref_or_val[rows, :]
            lane = _lane(val.shape)
            keep = (lane < HEAD_DIM) if j == 0 else (lane >= HEAD_DIM)
            return jnp.where(keep, val, jnp.zeros_like(val)).astype(BF16)
        sl = slice(j * HEAD_PAD, (j + 1) * HEAD_PAD)
        val = ref_or_val[:, sl] if rows is None else ref_or_val[rows, sl]
        return val.astype(BF16)

    def attn_fwd(q, k, v, q_blk0, k_blk0, v_blk0, cumc, cumt, packed, chunked, scale, name):
        has_bias = cumc is not None
        n_pairs = (FH if packed else MH) // 2
        wq = LANES if packed else 2 * HEAD_PAD

        def body(*refs):
            if has_bias:
                q_ref, k_ref, v_ref, cc_ref, ct_ref, o_ref, lse_ref = refs
            else:
                q_ref, k_ref, v_ref, o_ref, lse_ref = refs
            i = pl.program_id(1)
            q0 = i * TQ
            n_kv = ((i + 1) * TQ + TK - 1) // TK
            outs, lses = [], []
            for j in range(2):
                qh = _head(q_ref, j, packed)
                cq = cc_ref[0][:, j:j + 1] if has_bias else None

                def step(kb, carry, j=j, qh=qh, cq=cq):
                    m, l, acc = carry
                    k0 = pl.multiple_of(kb * TK, TK)
                    rows = pl.ds(k0, TK)
                    kh = _head(k_ref, j, packed, rows)
                    s = _nt(qh, kh) * scale
                    if has_bias:
                        s = s + (cq - ct_ref[0, kb][j:j + 1, :])
                    s = jnp.where(_allowed(q0, k0, chunked), s, NEG)
                    m_new = jnp.maximum(m, jnp.max(s, axis=-1, keepdims=True))
                    p = jnp.exp(s - m_new)
                    alpha = jnp.exp(m - m_new)
                    l = alpha * l + jnp.sum(p, axis=-1, keepdims=True)
                    acc = alpha * acc + _nn(p.astype(BF16), v_ref[rows, :].astype(BF16))
                    return m_new, l, acc

                init = (jnp.full((TQ, 1), NEG, F32), jnp.zeros((TQ, 1), F32), jnp.zeros((TQ, LANES), F32))
                m, l, acc = lax.fori_loop(0, n_kv, step, init)
                outs.append(acc / l)
                lses.append(m + jnp.log(l))
            lane = _lane((TQ, LANES))
            o_ref[...] = jnp.where(lane < HEAD_DIM, outs[0], outs[1])
            lse_ref[0] = jnp.where(lane == 0, lses[0], jnp.where(lane == 1, lses[1], 0.0))

        in_specs = [pl.BlockSpec((TQ, wq), lambda p, i: (i, q_blk0 + p)),
                    pl.BlockSpec((S, wq), lambda p, i: (0, k_blk0 + p)),
                    pl.BlockSpec((S, LANES), lambda p, i: (0, v_blk0 + p))]
        args = [q, k, v]
        if has_bias:
            in_specs += [pl.BlockSpec((1, TQ, LANES), lambda p, i: (p, i, 0)),
                         pl.BlockSpec((1, S // TK, 8, TK), lambda p, i: (p, 0, 0, 0))]
            args += [cumc, cumt]
        return pl.pallas_call(
            body, name=name, grid=(n_pairs, S // TQ), in_specs=in_specs,
            out_specs=[pl.BlockSpec((TQ, LANES), lambda p, i: (i, p)),
                       pl.BlockSpec((1, TQ, LANES), lambda p, i: (p, i, 0))],
            out_shape=[jax.ShapeDtypeStruct((S, n_pairs * LANES), F32),
                       jax.ShapeDtypeStruct((n_pairs, S, LANES), F32)],
            compiler_params=_params(("parallel", "parallel")))(*args)

    def gate_outproj(of, om, z, w_out, x, gate):
        def body(of_ref, om_ref, fg_ref, mg_ref, w_ref, x_ref, gt_ref, xn_ref, u_ref, y_ref):
            fg, mg = fg_ref[...], mg_ref[...]
            u = jnp.concatenate([of_ref[...] * fg * _sigmoid(fg), om_ref[...] * mg * _sigmoid(mg)], axis=1).astype(BF16)
            y = _nn(u, w_ref[...])
            u_ref[...] = u
            y_ref[...] = y.astype(BF16)
            xn_ref[...] = x_ref[...] + gt_ref[...] * y

        return pl.pallas_call(
            body, name="gate_outproj", grid=(n_tok,),
            in_specs=[tok(FW), tok(MW), tok(FW, cfg.o_fg // FW), tok(MW, cfg.o_mg // MW),
                      const((FW + MW, D)), tok(D), const((1, D))],
            out_specs=[tok(D), tok(FW + MW), tok(D)],
            out_shape=[jax.ShapeDtypeStruct((S, D), F32), jax.ShapeDtypeStruct((S, FW + MW), BF16),
                       jax.ShapeDtypeStruct((S, D), BF16)],
            compiler_params=_params(("parallel",)))(of, om, z, z, w_out, x, gate)

    def final_loss(x, g, target):
        def body(x_ref, g_ref, t_ref, dx_ref, acc_ref, loss_ref):
            @pl.when(pl.program_id(0) == 0)
            def _():
                acc_ref[...] = jnp.zeros_like(acc_ref)
                loss_ref[...] = jnp.zeros_like(loss_ref)

            gv = g_ref[...]
            y, xh, rstd = _rms(x_ref[...], gv)
            e = y - t_ref[...]
            loss_ref[...] += 0.5 * jnp.sum(jnp.sum(e * e, axis=-1, keepdims=True) / D, axis=0, keepdims=True)
            dx, dg = _rms_bwd(e / D, gv, xh, rstd)
            dx_ref[...] = dx
            acc_ref[0:1, :] += dg

        return pl.pallas_call(
            body, name="final_loss", grid=(n_tok,),
            in_specs=[tok(D), const((1, D)), tok(D)],
            out_specs=[tok(D), const((8, D)), const((1, LANES))],
            out_shape=[jax.ShapeDtypeStruct((S, D), F32), jax.ShapeDtypeStruct((8, D), F32),
                       jax.ShapeDtypeStruct((1, LANES), F32)],
            compiler_params=_params(("arbitrary",)))(x, g, target)


    def bwd_out(dxn, gate, y, w_out, of, om, z):
        def body(dx_ref, gt_ref, y_ref, w_ref, of_ref, om_ref, fg_ref, mg_ref,
                 dof_ref, dom_ref, dfg_ref, dmg_ref, dy_ref, acc_ref):
            @pl.when(pl.program_id(0) == 0)
            def _():
                acc_ref[...] = jnp.zeros_like(acc_ref)

            dxv = dx_ref[...]
            acc_ref[0:1, :] += jnp.sum(dxv * y_ref[...].astype(F32), axis=0, keepdims=True)
            dy = (gt_ref[...] * dxv).astype(BF16)
            dy_ref[...] = dy
            du = _nt(dy, w_ref[...])
            for lo, width, o_ref, g_ref, do_ref, dg_ref in ((0, FW, of_ref, fg_ref, dof_ref, dfg_ref),
                                                            (FW, MW, om_ref, mg_ref, dom_ref, dmg_ref)):
                gv = g_ref[...]
                sg = _sigmoid(gv)
                dup = du[:, lo:lo + width]
                do_ref[...] = dup * gv * sg
                dg_ref[...] = (dup * o_ref[...] * sg * (1.0 + gv * (1.0 - sg))).astype(BF16)

        return pl.pallas_call(
            body, name="bwd_out", grid=(n_tok,),
            in_specs=[tok(D), const((1, D)), tok(D), const((FW + MW, D)), tok(FW), tok(MW),
                      tok(FW, cfg.o_fg // FW), tok(MW, cfg.o_mg // MW)],
            out_specs=[tok(FW), tok(MW), tok(FW), tok(MW), tok(D), const((8, D))],
            out_shape=[jax.ShapeDtypeStruct((S, FW), F32), jax.ShapeDtypeStruct((S, MW), F32),
                       jax.ShapeDtypeStruct((S, FW), BF16), jax.ShapeDtypeStruct((S, MW), BF16),
                       jax.ShapeDtypeStruct((S, D), BF16), jax.ShapeDtypeStruct((8, D), F32)],
            compiler_params=_params(("arbitrary",)))(dxn, gate, y, w_out, of, om, z, z)

    def attn_stats(do, o, lse, name):
        n_pairs = lse.shape[0]

        def body(do_ref, o_ref, lse_ref, st_ref):
            d = do_ref[...].astype(BF16).astype(F32) * o_ref[...]
            lane = _lane(d.shape)
            d0 = jnp.sum(jnp.where(lane < HEAD_DIM, d, 0.0), axis=-1, keepdims=True)
            d1 = jnp.sum(jnp.where(lane >= HEAD_DIM, d, 0.0), axis=-1, keepdims=True)
            st_ref[0] = jnp.where(lane == 2, d0, jnp.where(lane == 3, d1, lse_ref[0]))

        blk = pl.BlockSpec((TM, LANES), lambda p, i: (i, p))
        sblk = pl.BlockSpec((1, TM, LANES), lambda p, i: (p, i, 0))
        return pl.pallas_call(
            body, name=name, grid=(n_pairs, n_tok), in_specs=[blk, blk, sblk], out_specs=sblk,
            out_shape=jax.ShapeDtypeStruct(lse.shape, F32),
            compiler_params=_params(("parallel", "parallel")))(do, o, lse)

    def attn_bwd(q, k, v, do, stats, q_blk0, k_blk0, v_blk0, cumc, cumt, packed, chunked, scale, name):
        has_bias = cumc is not None
        n_pairs = (FH if packed else MH) // 2
        wq = LANES if packed else 2 * HEAD_PAD
        n_q = S // TQ

        def body(*refs):
            if has_bias:
                q_ref, k_ref, v_ref, do_ref, st_ref, cc_ref, ct_ref, dq_ref, dk_ref, dv_ref, dc_ref, dr_ref = refs
            else:
                q_ref, k_ref, v_ref, do_ref, st_ref, dq_ref, dk_ref, dv_ref = refs
            jb = pl.program_id(1)
            k0 = jb * TK

            @pl.when(jb == 0)
            def _():
                dq_ref[...] = jnp.zeros_like(dq_ref)
                if has_bias:
                    dr_ref[...] = jnp.zeros_like(dr_ref)

            dks, dvs, dcs = [], [], []
            for j in range(2):
                kh = _head(k_ref, j, packed)
                vh = _head(v_ref, j, True)
                ck = ct_ref[0, 0][j:j + 1, :] if has_bias else None

                def step(ib, carry, j=j, kh=kh, vh=vh, ck=ck):
                    dk, dv, dc = carry
                    q0 = pl.multiple_of(ib * TQ, TQ)
                    rows = pl.ds(q0, TQ)
                    qh = _head(q_ref, j, packed, rows)
                    doh = _head(do_ref, j, True, rows)
                    st = st_ref[0, rows, :]
                    s = _nt(qh, kh) * scale
                    if has_bias:
                        s = s + (cc_ref[0, rows, :][:, j:j + 1] - ck)
                    s = jnp.where(_allowed(q0, k0, chunked), s, NEG)
                    p = jnp.exp(s - st[:, j:j + 1])
                    dv = dv + _tn(p.astype(BF16), doh)
                    ds = p * (_nt(doh, vh) - st[:, 2 + j:3 + j])
                    dsb = (ds * scale).astype(BF16)
                    dk = dk + _tn(dsb, qh)
                    if packed:
                        dq_ref[rows, :] += _nn(dsb, kh)
                    else:
                        dq_ref[rows, j * HEAD_PAD:(j + 1) * HEAD_PAD] += _nn(dsb, kh)
                    if has_bias:
                        dc = dc - jnp.sum(ds, axis=0, keepdims=True)
                        row_sum = jnp.sum(ds, axis=-1, keepdims=True)
                        dr_ref[0, rows, :] += jnp.where(_lane((TQ, LANES)) == j, row_sum, 0.0)
                    return dk, dv, dc

                init = (jnp.zeros((TK, kh.shape[1]), F32), jnp.zeros((TK, LANES), F32), jnp.zeros((1, TK), F32))
                dk, dv, dc = lax.fori_loop(k0 // TQ, n_q, step, init)
                dks.append(dk)
                dvs.append(dv)
                dcs.append(dc)
            dk_ref[...] = dks[0] + dks[1] if packed else jnp.concatenate(dks, axis=1)
            dv_ref[...] = dvs[0] + dvs[1]
            if has_bias:
                sub = lax.broadcasted_iota(jnp.int32, (8, TK), 0)
                dc_ref[0, 0] = jnp.where(sub == 0, dcs[0], jnp.where(sub == 1, dcs[1], 0.0))

        in_specs = [pl.BlockSpec((S, wq), lambda p, j: (0, q_blk0 + p)),
                    pl.BlockSpec((TK, wq), lambda p, j: (j, k_blk0 + p)),
                    pl.BlockSpec((TK, LANES), lambda p, j: (j, v_blk0 + p)),
                    pl.BlockSpec((S, LANES), lambda p, j: (0, p)),
                    pl.BlockSpec((1, S, LANES), lambda p, j: (p, 0, 0))]
        args = [q, k, v, do, stats]
        out_specs = [pl.BlockSpec((S, wq), lambda p, j: (0, p)),
                     pl.BlockSpec((TK, wq), lambda p, j: (j, p)),
                     pl.BlockSpec((TK, LANES), lambda p, j: (j, p))]
        out_shape = [jax.ShapeDtypeStruct((S, n_pairs * wq), F32), jax.ShapeDtypeStruct((S, n_pairs * wq), F32),
                     jax.ShapeDtypeStruct((S, n_pairs * LANES), F32)]
        if has_bias:
            in_specs += [pl.BlockSpec((1, S, LANES), lambda p, j: (p, 0, 0)),
                         pl.BlockSpec((1, 1, 8, TK), lambda p, j: (p, j, 0, 0))]
            args += [cumc, cumt]
            out_specs += [pl.BlockSpec((1, 1, 8, TK), lambda p, j: (p, j, 0, 0)),
                          pl.BlockSpec((1, S, LANES), lambda p, j: (p, 0, 0))]
            out_shape += [jax.ShapeDtypeStruct((n_pairs, S // TK, 8, TK), F32),
                          jax.ShapeDtypeStruct((n_pairs, S, LANES), F32)]
        return pl.pallas_call(
            body, name=name, grid=(n_pairs, S // TK), in_specs=in_specs, out_specs=out_specs, out_shape=out_shape,
            compiler_params=_params(("parallel", "arbitrary")))(*args)

    def fox_post(dcum, z, bf_pad):
        def rev(i):
            return n_tok - 1 - i

        def body(dc_ref, z_ref, b_ref, dff_ref, acc_ref, carry):
            @pl.when(pl.program_id(0) == 0)
            def _():
                carry[...] = jnp.zeros_like(carry)
                acc_ref[...] = jnp.zeros_like(acc_ref)

            _, sig_neg = _log_f_terms(z_ref[...], b_ref[...])
            row = lax.broadcasted_iota(jnp.int32, (TM, TM), 0)
            col = lax.broadcasted_iota(jnp.int32, (TM, TM), 1)
            tri = (col >= row).astype(F32)
            dlog = jnp.dot(tri, dc_ref[...], precision=lax.Precision.HIGHEST, preferred_element_type=F32) + carry[...]
            carry[...] = dlog[0:1, :]
            dff = dlog * sig_neg
            dff_ref[...] = dff
            acc_ref[0:1, :] += jnp.sum(dff, axis=0, keepdims=True)

        return pl.pallas_call(
            body, name="fox_post", grid=(n_tok,),
            in_specs=[pl.BlockSpec((TM, LANES), lambda i: (rev(i), 0)),
                      pl.BlockSpec((TM, LANES), lambda i: (rev(i), misc_blk)), const((1, LANES))],
            out_specs=[pl.BlockSpec((TM, LANES), lambda i: (rev(i), 0)), const((8, LANES))],
            out_shape=[jax.ShapeDtypeStruct((S, LANES), F32), jax.ShapeDtypeStruct((8, LANES), F32)],
            scratch_shapes=[pltpu.VMEM((1, LANES), F32)],
            compiler_params=_params(("arbitrary",)))(dcum, z, bf_pad)

    def mla_post(dq, dk, dv, dff, z, gq, gkv, w_uq, w_uk, w_v, cos_t, sin_t):
        def body(dq_ref, dk_ref, dv_ref, dff_ref, ql_ref, kvl_ref, gq_ref, gkv_ref, wq_ref, wk_ref, wv_ref,
                 c_ref, s_ref, zq_ref, zkv_ref, zms_ref, dwq_ref, dwk_ref, dwv_ref, dgq_ref, dgkv_ref):
            @pl.when(pl.program_id(0) == 0)
            def _():
                for r in (dwq_ref, dwk_ref, dwv_ref, dgq_ref, dgkv_ref):
                    r[...] = jnp.zeros_like(r)

            cos1, sin1 = c_ref[...], s_ref[...]
            gqv, gkvv = gq_ref[...], gkv_ref[...]
            qn, qxh, qrstd = _rms(ql_ref[...], gqv)
            dq_pre = _rope_t(dq_ref[...], jnp.tile(cos1, (1, MH)), jnp.tile(sin1, (1, MH))).astype(BF16)
            dwq_ref[...] += _tn(qn.astype(BF16), dq_pre)
            dql, dgq = _rms_bwd(_nt(dq_pre, wq_ref[...]), gqv, qxh, qrstd)
            zq_ref[...] = dql.astype(BF16)
            dgq_ref[0:1, :] += dgq

            dkv = dk_ref[...]
            lane = _lane(dkv.shape) & (HEAD_PAD - 1)
            dkn = jnp.where(lane < HEAD_DIM, dkv, 0.0).astype(BF16)
            dkr = dkv[:, 0:HEAD_PAD]
            for hd in range(1, MH):
                dkr = dkr + dkv[:, hd * HEAD_PAD:(hd + 1) * HEAD_PAD]
            lane1 = _lane(dkr.shape)
            dkr = jnp.where((lane1 >= ROPE_LO) & (lane1 < ROPE_LO + ROPE_DIM), dkr, 0.0)
            dkr = _rope_t(dkr, cos1, sin1)
            zms_ref[...] = (dkr + dff_ref[...]).astype(BF16)

            kvn, kxh, krstd = _rms(kvl_ref[...], gkvv)
            kvb = kvn.astype(BF16)
            dvb = dv_ref[...].astype(BF16)
            dwk_ref[...] += _tn(kvb, dkn)
            dwv_ref[...] += _tn(kvb, dvb)
            dkvl, dgkv = _rms_bwd(_nt(dkn, wk_ref[...]) + _nt(dvb, wv_ref[...]), gkvv, kxh, krstd)
            zkv_ref[...] = dkvl.astype(BF16)
            dgkv_ref[0:1, :] += dgkv

        return pl.pallas_call(
            body, name="mla_post", grid=(n_tok,),
            in_specs=[tok(QW), tok(QW), tok(MW), tok(LANES), tok(QL, cfg.o_ql // QL), tok(KVL, cfg.o_kv // KVL),
                      const((1, QL)), const((1, KVL)), const((QL, QW)), const((KVL, QW)), const((KVL, MW)),
                      tok(LANES), tok(LANES)],
            out_specs=[tok(QL), tok(KVL), tok(LANES), const((QL, QW)), const((KVL, QW)), const((KVL, MW)),
                       const((8, QL)), const((8, KVL))],
            out_shape=[jax.ShapeDtypeStruct((S, QL), BF16), jax.ShapeDtypeStruct((S, KVL), BF16),
                       jax.ShapeDtypeStruct((S, LANES), BF16), jax.ShapeDtypeStruct((QL, QW), F32),
                       jax.ShapeDtypeStruct((KVL, QW), F32), jax.ShapeDtypeStruct((KVL, MW), F32),
                       jax.ShapeDtypeStruct((8, QL), F32), jax.ShapeDtypeStruct((8, KVL), F32)],
            compiler_params=_params(("arbitrary",)))(dq, dk, dv, dff, z, z, gq, gkv, w_uq, w_uk, w_v, cos_t, sin_t)

    def bwd_in(dz, w_in, x, dxn, g, scale):
        def body(dz_ref, w_ref, x_ref, dx_ref, g_ref, sc_ref, o_ref, acc_ref):
            @pl.when(pl.program_id(0) == 0)
            def _():
                acc_ref[...] = jnp.zeros_like(acc_ref)

            dh = _nt(dz_ref[...], w_ref[...])
            gv, mod = g_ref[...], 1.0 + sc_ref[...]
            _, xh, rstd = _rms(x_ref[...], gv)
            t = dh * xh
            acc_ref[0:1, :] += jnp.sum(dh, axis=0, keepdims=True)
            acc_ref[1:2, :] += jnp.sum(t * gv, axis=0, keepdims=True)
            acc_ref[2:3, :] += jnp.sum(t * mod, axis=0, keepdims=True)
            dx, _ = _rms_bwd(dh, gv * mod, xh, rstd)
            o_ref[...] = dx_ref[...] + dx

        return pl.pallas_call(
            body, name="bwd_in", grid=(n_tok,),
            in_specs=[tok(NZ), const((D, NZ)), tok(D), tok(D), const((1, D)), const((1, D))],
            out_specs=[tok(D), const((8, D))],
            out_shape=[jax.ShapeDtypeStruct((S, D), F32), jax.ShapeDtypeStruct((8, D), F32)],
            compiler_params=_params(("arbitrary",)))(dz, w_in, x, dxn, g, scale)


    def pair_layouts(cum):
        c8 = cum[:, :FH]
        n_pairs = FH // 2
        cc = jnp.pad(c8.reshape(S, n_pairs, 2).transpose(1, 0, 2), ((0, 0), (0, 0), (0, LANES - 2)))
        ct = jnp.pad(c8.T.reshape(n_pairs, 2, S), ((0, 0), (0, 6), (0, 0)))
        ct = ct.reshape(n_pairs, 8, S // TK, TK).transpose(0, 2, 1, 3)
        return cc, ct

    def bias_grad(dc, dr):
        n_pairs = FH // 2
        d = dc.transpose(0, 2, 1, 3).reshape(n_pairs, 8, S)[:, :2, :].reshape(FH, S).T
        d = d + dr[:, :, :2].transpose(1, 0, 2).reshape(S, FH)
        return jnp.pad(d, ((0, 0), (0, LANES - FH)))

    def layer_forward(x, wl, mod):
        shift, scale, gate = mod
        h, z = ln_inproj(x, wl.norm_g, scale, shift, wl.w_in)
        cum = fox_prep(z, wl.bf_pad)
        cc, ct = pair_layouts(cum)
        of, lse_f = attn_fwd(z, z, z, cfg.o_fq // LANES, cfg.o_fk // LANES, cfg.o_fv // LANES, cc, ct,
                             True, False, HEAD_DIM ** -0.5, "fox_fwd")
        qp, kp, vp = mla_prep(z, wl.gq, wl.gkv, wl.w_uq, wl.w_uk, wl.w_v, cfg.cos_t, cfg.sin_t)
        om, lse_m = attn_fwd(qp, kp, vp, 0, 0, 0, None, None, False, True, (HEAD_DIM + ROPE_DIM) ** -0.5, "mla_fwd")
        xn, u, y = gate_outproj(of, om, z, wl.w_out, x, gate)
        saved = types.SimpleNamespace(x=x, h=h, z=z, cc=cc, ct=ct, of=of, lse_f=lse_f, qp=qp, kp=kp, vp=vp,
                                      om=om, lse_m=lse_m, u=u, y=y)
        return xn, saved

    def layer_backward(dxn, sv, wl, mod):
        shift, scale, gate = mod
        do_f, do_m, dfg, dmg, dy, acc_o = bwd_out(dxn, gate, sv.y, wl.w_out, sv.of, sv.om, sv.z)
        dw_out = matmul_tn(sv.u, dy, "dw_out")
        st_f = attn_stats(do_f, sv.of, sv.lse_f, "fox_stats")
        dfq, dfk, dfv, dck, dcr = attn_bwd(sv.z, sv.z, sv.z, do_f, st_f, cfg.o_fq // LANES, cfg.o_fk // LANES,
                                      cfg.o_fv // LANES, sv.cc, sv.ct, True, False, HEAD_DIM ** -0.5, "fox_bwd")
        dff, acc_f = fox_post(bias_grad(dck, dcr), sv.z, wl.bf_pad)
        st_m = attn_stats(do_m, sv.om, sv.lse_m, "mla_stats")
        dqp, dkp, dvp = attn_bwd(sv.qp, sv.kp, sv.vp, do_m, st_m, 0, 0, 0, None, None, False, True,
                                 (HEAD_DIM + ROPE_DIM) ** -0.5, "mla_bwd")
        zq, zkv, zms, dw_uq, dw_uk, dw_v, dgq, dgkv = mla_post(
            dqp, dkp, dvp, dff, sv.z, wl.gq, wl.gkv, wl.w_uq, wl.w_uk, wl.w_v, cfg.cos_t, cfg.sin_t)
        dz = jnp.concatenate([dfq.astype(BF16), dfk.astype(BF16), dfv.astype(BF16), dfg, dmg, zq, zkv, zms], axis=1)
        dx, acc_i = bwd_in(dz, wl.w_in, sv.x, dxn, wl.norm_g, scale)
        dw_in = matmul_tn(sv.h, dz, "dw_in")
        grads = types.SimpleNamespace(
            w_in=dw_in, w_out=dw_out, w_uq=dw_uq, w_uk=dw_uk, w_v=dw_v, gq=dgq[0], gkv=dgkv[0],
            b_f=acc_f[0, :FH], norm_g=acc_i[2], dmod=jnp.concatenate([acc_i[0], acc_i[1], acc_o[0]]))
        return dx, grads

    return types.SimpleNamespace(layer_forward=layer_forward, layer_backward=layer_backward, final_loss=final_loss)


def _pack_rows(parts, dtype, row_multiple):
    flat = jnp.concatenate([p.reshape(-1).astype(dtype) for p in parts])
    per = SLAB_COLS * row_multiple
    total = -(-flat.shape[0] // per) * per
    return jnp.pad(flat, (0, total - flat.shape[0])).reshape(total // SLAB_COLS, SLAB_COLS)


def _unpack(flat, shapes):
    out, off = [], 0
    for shp in shapes:
        n = 1
        for d in shp:
            n *= d
        out.append(flat[off:off + n].reshape(shp))
        off += n
    return out


def kernel(x, c, positions, norm_g, w_ada, b_ada, w_in, b_f, q_norm_g, w_uq, kv_norm_g, w_ukv, w_out, final_g, loss_target, m_norm_g, m_w_ada, m_b_ada, m_w_in, m_b_f, m_q_norm_g, m_w_uq, m_kv_norm_g, m_w_ukv, m_w_out, m_final_g, v_norm_g, v_w_ada, v_b_ada, v_w_in, v_b_f, v_q_norm_g, v_w_uq, v_kv_norm_g, v_w_ukv, v_w_out, v_final_g):
    S, D = x.shape[1], x.shape[2]
    L = norm_g.shape[0]
    FH = b_f.shape[1]
    QL, KVL = q_norm_g.shape[1], kv_norm_g.shape[1]
    MH = w_ukv.shape[2] * N_CHIPS // (2 * HEAD_DIM)
    FW, MW = FH * HEAD_DIM, MH * HEAD_DIM
    NA = w_ada.shape[2]
    n_in = w_in.shape[2] * N_CHIPS
    cfg = types.SimpleNamespace(S=S, D=D, FW=FW, MW=MW, QL=QL, KVL=KVL, FH=FH, MH=MH)
    cfg.o_fq, cfg.o_fk, cfg.o_fv, cfg.o_fg, cfg.o_mg = 0, FW, 2 * FW, 3 * FW, 4 * FW
    cfg.o_ql = 4 * FW + MW
    cfg.o_kv = cfg.o_ql + QL
    cfg.o_ms = cfg.o_kv + KVL
    cfg.NZ = cfg.o_ms + LANES
    assert FW == MW and FH % 2 == 0 and MH % 2 == 0 and cfg.o_ql % QL == 0 and cfg.o_kv % KVL == 0 and KVL == LANES
    assert n_in == 4 * FW + FH + QL + KVL + ROPE_DIM + MW

    mx, my, mc = _my_pos()
    my_chip = 2 * mx + my
    my_dev = 2 * my_chip + mc

    inv_freq = 1.0 / (ROPE_THETA ** (jnp.arange(0, ROPE_DIM, 2, dtype=F32) / ROPE_DIM))
    ang = positions[0].astype(F32)[:, None] * inv_freq
    cos, sin = jnp.cos(ang), jnp.sin(ang)
    cfg.cos_t = jnp.concatenate([jnp.ones((S, ROPE_LO), F32), cos, cos, jnp.ones((S, HEAD_PAD - ROPE_LO - ROPE_DIM), F32)], axis=1)
    cfg.sin_t = jnp.concatenate([jnp.zeros((S, ROPE_LO), F32), -sin, sin, jnp.zeros((S, HEAD_PAD - ROPE_LO - ROPE_DIM), F32)], axis=1)

    shard_shapes = [w_in.shape, w_uq.shape, w_ukv.shape, w_out.shape]
    slab = _pack_rows([w_in, w_uq, w_ukv, w_out], BF16, 32)
    half = slab.shape[0] // 2
    mine = lax.dynamic_slice_in_dim(slab, mc * half, half, axis=0)
    got = chip_exchange(mine, True, "weights_chips")
    other = sibling_swap(got.reshape(N_CHIPS * half, SLAB_COLS), "weights_sibling").reshape(N_CHIPS, half, SLAB_COLS)
    lo = jnp.where(mc == 0, got, other)
    hi = jnp.where(mc == 0, other, got)
    slabs = jnp.concatenate([lo, hi], axis=1).reshape(N_CHIPS, -1)
    per_chip = [_unpack(slabs[t], shard_shapes) for t in range(N_CHIPS)]
    w_in_f = jnp.concatenate([p[0] for p in per_chip], axis=2)
    w_uq_f = jnp.concatenate([p[1] for p in per_chip], axis=2)
    w_ukv_f = jnp.concatenate([p[2] for p in per_chip], axis=2)
    w_out_f = jnp.concatenate([p[3] for p in per_chip], axis=1)

    sizes = (FW, FW, FW, FH, FW, QL, KVL, ROPE_DIM, MW)
    offs = [0]
    for sz in sizes:
        offs.append(offs[-1] + sz)
    seg = [w_in_f[:, :, offs[i]:offs[i + 1]] for i in range(len(sizes))]
    fq_w, fk_w, fv_w, ff_w, fg_w, ql_w, kvl_w, kr_w, mg_w = seg
    zeros = lambda n: jnp.zeros((L, D, n), BF16)
    w_in_p = jnp.concatenate([fq_w, fk_w, fv_w, fg_w, mg_w, ql_w, kvl_w, ff_w, zeros(ROPE_LO - FH), kr_w,
                              zeros(HEAD_PAD - ROPE_LO - ROPE_DIM)], axis=2)
    w_uq_p = jnp.pad(w_uq_f.reshape(L, QL, MH, HEAD_DIM + ROPE_DIM), ((0, 0), (0, 0), (0, 0), (0, HEAD_PAD - HEAD_DIM - ROPE_DIM)))
    w_uq_p = w_uq_p.reshape(L, QL, MH * HEAD_PAD)
    w_ukv4 = w_ukv_f.reshape(L, KVL, MH, 2 * HEAD_DIM)
    w_uk_p = jnp.pad(w_ukv4[..., :HEAD_DIM], ((0, 0), (0, 0), (0, 0), (0, HEAD_PAD - HEAD_DIM))).reshape(L, KVL, MH * HEAD_PAD)
    w_v_p = w_ukv4[..., HEAD_DIM:].reshape(L, KVL, MW)

    c_all = allgather8(c.reshape(8, D // 8), "gather_c").reshape(N_DEV, D)
    c_pad = jnp.pad(c_all, ((0, 16 - N_DEV), (0, 0)))
    mod_part = ada_forward(c_pad, w_ada)[:, :N_DEV, :]
    mod_all = allgather8(mod_part.reshape(-1, LANES), "gather_mod").reshape(N_CHIPS, 2, L, N_DEV, NA)[:, 0]
    mod_full = mod_all.transpose(1, 2, 0, 3).reshape(L, N_DEV, N_CHIPS * NA) + b_ada[:, None, :]
    mod_mine = lax.dynamic_index_in_dim(mod_full, my_dev, axis=1, keepdims=True)

    step = make_step(cfg)
    bf_pad = jnp.pad(b_f, ((0, 0), (0, LANES - FH)))
    layers, mods = [], []
    for l in range(L):
        layers.append(types.SimpleNamespace(
            norm_g=norm_g[l][None], w_in=w_in_p[l], bf_pad=bf_pad[l][None], gq=q_norm_g[l][None], gkv=kv_norm_g[l][None],
            w_uq=w_uq_p[l], w_uk=w_uk_p[l], w_v=w_v_p[l], w_out=w_out_f[l]))
        mods.append((mod_mine[l, :, :D], mod_mine[l, :, D:2 * D], mod_mine[l, :, 2 * D:]))

    xl = x[0]
    saved = []
    for l in range(L):
        xl, sv = step.layer_forward(xl, layers[l], mods[l])
        saved.append(sv)
    dx, acc_fin, loss_part = step.final_loss(xl, final_g[None], loss_target[0])
    loss = lax.psum(loss_part[0, 0], ("x", "y", "c"))
    gl = [None] * L
    for l in reversed(range(L)):
        dx, gl[l] = step.layer_backward(dx, saved[l], layers[l], mods[l])
    grad_x = dx[None]

    stack = lambda name: jnp.stack([getattr(g, name) for g in gl])
    small_parts = [stack("norm_g"), stack("dmod"), stack("b_f"), stack("gq"), stack("gkv"), acc_fin[0]]
    small_shapes = [p.shape for p in small_parts]
    small = _pack_rows(small_parts, F32, 8).reshape(-1, LANES)
    small_all = allgather8(small, "gather_small").reshape(N_DEV, -1, LANES)
    small_sum = sum_leading(small_all, "sum_small")
    g_norm_g, g_b_ada, g_b_f, g_q_norm_g, g_kv_norm_g, g_final_g = _unpack(small_sum.reshape(-1), small_shapes)

    n_ng = L * D
    dmod_all = small_all.reshape(N_DEV, -1)[:, n_ng:n_ng + L * 3 * D].reshape(N_DEV, L, 3 * D)
    dmod_cols = lax.dynamic_slice_in_dim(dmod_all, my_chip * NA, NA, axis=2).transpose(1, 0, 2)
    g_w_ada = ada_backward(c_pad, jnp.pad(dmod_cols, ((0, 0), (0, 16 - N_DEV), (0, 0))))

    dw_in_p = stack("w_in")
    ms = cfg.o_ms
    dw_in_f = jnp.concatenate([
        dw_in_p[:, :, 0:3 * FW], dw_in_p[:, :, ms:ms + FH], dw_in_p[:, :, cfg.o_fg:cfg.o_fg + FW],
        dw_in_p[:, :, cfg.o_ql:cfg.o_ql + QL + KVL], dw_in_p[:, :, ms + ROPE_LO:ms + ROPE_LO + ROPE_DIM],
        dw_in_p[:, :, cfg.o_mg:cfg.o_mg + MW]], axis=2)
    dw_uq_f = stack("w_uq").reshape(L, QL, MH, HEAD_PAD)[..., :HEAD_DIM + ROPE_DIM].reshape(L, QL, -1)
    dw_ukv_f = jnp.concatenate([stack("w_uk").reshape(L, KVL, MH, HEAD_PAD)[..., :HEAD_DIM],
                                stack("w_v").reshape(L, KVL, MH, HEAD_DIM)], axis=3).reshape(L, KVL, -1)
    dw_out_f = stack("w_out")
    dest = []
    for t in range(N_CHIPS):
        dest.append(_pack_rows([
            dw_in_f[:, :, t * w_in.shape[2]:(t + 1) * w_in.shape[2]], dw_uq_f[:, :, t * w_uq.shape[2]:(t + 1) * w_uq.shape[2]],
            dw_ukv_f[:, :, t * w_ukv.shape[2]:(t + 1) * w_ukv.shape[2]], dw_out_f[:, t * w_out.shape[1]:(t + 1) * w_out.shape[1], :]],
            F32, 32))
    gs = jnp.stack(dest)
    ghalf = gs.shape[1] // 2
    gs = gs.reshape(N_CHIPS, 2, ghalf, SLAB_COLS)
    keep = lax.dynamic_index_in_dim(gs, mc, axis=1, keepdims=False).reshape(N_CHIPS * ghalf, SLAB_COLS)
    give = lax.dynamic_index_in_dim(gs, 1 - mc, axis=1, keepdims=False).reshape(N_CHIPS * ghalf, SLAB_COLS)
    sib = sibling_swap(give, "grads_sibling")
    chip_part = add_cast(keep, sib, BF16, "grads_chip_sum").reshape(N_CHIPS, ghalf, SLAB_COLS)
    parts = chip_exchange(chip_part, False, "grads_chips")
    red_half = sum_chips(parts, "grads_sum")
    red_other = sibling_swap(red_half, "grads_back")
    red = jnp.concatenate([jnp.where(mc == 0, red_half, red_other), jnp.where(mc == 0, red_other, red_half)], axis=0)
    g_w_in, g_w_uq, g_w_ukv, g_w_out = _unpack(red.reshape(-1), shard_shapes)

    def big(w, g, m, v, name):
        shp = w.shape
        two = lambda a: a.reshape(-1, shp[-1])
        return [o.reshape(shp) for o in adamw(two(w), two(g), two(m), two(v), name)]

    names = ["norm_g", "w_ada", "b_ada", "w_in", "b_f", "q_norm_g", "w_uq", "kv_norm_g", "w_ukv", "w_out", "final_g"]
    ws = dict(norm_g=norm_g, w_ada=w_ada, b_ada=b_ada, w_in=w_in, b_f=b_f, q_norm_g=q_norm_g, w_uq=w_uq,
              kv_norm_g=kv_norm_g, w_ukv=w_ukv, w_out=w_out, final_g=final_g)
    msd = dict(norm_g=m_norm_g, w_ada=m_w_ada, b_ada=m_b_ada, w_in=m_w_in, b_f=m_b_f, q_norm_g=m_q_norm_g, w_uq=m_w_uq,
               kv_norm_g=m_kv_norm_g, w_ukv=m_w_ukv, w_out=m_w_out, final_g=m_final_g)
    vsd = dict(norm_g=v_norm_g, w_ada=v_w_ada, b_ada=v_b_ada, w_in=v_w_in, b_f=v_b_f, q_norm_g=v_q_norm_g, w_uq=v_w_uq,
               kv_norm_g=v_kv_norm_g, w_ukv=v_w_ukv, w_out=v_w_out, final_g=v_final_g)
    gsd = dict(norm_g=g_norm_g, w_ada=g_w_ada, b_ada=g_b_ada, w_in=g_w_in, b_f=g_b_f, q_norm_g=g_q_norm_g, w_uq=g_w_uq,
               kv_norm_g=g_kv_norm_g, w_ukv=g_w_ukv, w_out=g_w_out, final_g=g_final_g)
    small_names = ["norm_g", "b_ada", "b_f", "q_norm_g", "kv_norm_g", "final_g"]
    sm_shapes = [ws[n].shape for n in small_names]
    pk = lambda d: _pack_rows([d[n] for n in small_names], F32, 8).reshape(-1, LANES)
    sm_out = adamw(pk(ws), pk(gsd), pk(msd), pk(vsd), "adamw_small")
    sm_d, sm_m, sm_v = [dict(zip(small_names, _unpack(o.reshape(-1), sm_shapes))) for o in sm_out]
    delta, new_m, new_v = dict(sm_d), dict(sm_m), dict(sm_v)
    for n in ["w_ada", "w_in", "w_uq", "w_ukv", "w_out"]:
        delta[n], new_m[n], new_v[n] = big(ws[n], gsd[n], msd[n], vsd[n], "adamw_" + n)

    return (loss, grad_x, *[gsd[n] for n in names], *[delta[n] for n in names],
            *[new_m[n] for n in names], *[new_v[n] for n in names])
```

```python
import functools
import types

import jax
import jax.numpy as jnp
from jax import lax
from jax.experimental import pallas as pl
from jax.experimental.pallas import tpu as pltpu

F32 = jnp.float32
BF16 = jnp.bfloat16
MESH = pl.DeviceIdType.MESH

N_CHIPS = 4
N_DEV = 8
HEAD_DIM = 64
ROPE_DIM = 32
ROPE_THETA = 10000.0
HEAD_PAD = 128
ROPE_LO = 64
ROPE_HALF = 16
LANES = 128
EPS = 1e-6
NEG = -1e30
ADAM_LR = 0.001
ADAM_B1 = 0.9
ADAM_B2 = 0.999
ADAM_EPS = 1e-08
ADAM_WD = 0.01
ADAM_STEP = 10
VMEM_LIMIT = 48 * 1024 * 1024
SLAB_COLS = 1024


def _params(sem=None, vmem=VMEM_LIMIT):
    return pltpu.CompilerParams(dimension_semantics=sem, vmem_limit_bytes=vmem)


def _nn(a, b):
    return jnp.dot(a, b, preferred_element_type=F32)


def _nt(a, b):
    return lax.dot_general(a, b, (((1,), (1,)), ((), ())), preferred_element_type=F32)


def _tn(a, b):
    return lax.dot_general(a, b, (((0,), (0,)), ((), ())), preferred_element_type=F32)


def _sigmoid(x):
    return 1.0 / (1.0 + jnp.exp(-x))


def _lane(shape):
    return lax.broadcasted_iota(jnp.int32, shape, len(shape) - 1)


def _pick(n, cands):
    for c in cands:
        if n % c == 0:
            return c
    return n


def _my_pos():
    return lax.axis_index("x"), lax.axis_index("y"), lax.axis_index("c")


def allgather8(xs, name):
    m_per, n = xs.shape

    def body(x_ref, out_ref, send_sems, recv_sems, local_sem):
        x, y, c = _my_pos()
        me, sibling = (x, y, c), (x, y, 1 - c)
        chips = [(1 - x, y), (x, 1 - y), (1 - x, 1 - y)]

        def rows(px, py, pc):
            return out_ref.at[pl.ds((4 * px + 2 * py + pc) * m_per, m_per), :]

        def copy(k, block, to, src=None):
            return pltpu.make_async_remote_copy(
                src_ref=rows(*block) if src is None else src, dst_ref=rows(*block),
                send_sem=send_sems.at[k], recv_sem=recv_sems.at[k], device_id=to, device_id_type=MESH)

        mine = pltpu.make_async_copy(x_ref, rows(*me), local_sem)
        mine.start()
        first = [copy(0, me, sibling, src=x_ref)]
        first += [copy(1 + j, me, (*chip, c), src=x_ref) for j, chip in enumerate(chips)]
        for cp in first:
            cp.start()
        passed = [copy(4 + j, (*chip, c), sibling) for j, chip in enumerate(chips)]
        for j, chip in enumerate(chips):
            copy(1 + j, (*chip, c), me).wait_recv()
            passed[j].start()
        copy(0, sibling, me).wait_recv()
        for j, chip in enumerate(chips):
            copy(4 + j, (*chip, 1 - c), me).wait_recv()
        for cp in first + passed:
            cp.wait_send()
        mine.wait()

    return pl.pallas_call(
        body, name=name,
        out_shape=jax.ShapeDtypeStruct((N_DEV * m_per, n), xs.dtype),
        in_specs=[pl.BlockSpec(memory_space=pltpu.VMEM)],
        out_specs=pl.BlockSpec(memory_space=pltpu.VMEM),
        scratch_shapes=[pltpu.SemaphoreType.DMA((7,)), pltpu.SemaphoreType.DMA((7,)), pltpu.SemaphoreType.DMA],
    )(xs)


def chip_exchange(xs, bcast, name):
    r, n = xs.shape[-2:]

    def body(x_ref, out_ref, send_sems, recv_sems, local_sem):
        x, y, c = _my_pos()
        s = 2 * x + y
        chips = [(1 - x, y), (x, 1 - y), (1 - x, 1 - y)]

        def src(t):
            return x_ref if bcast else x_ref.at[t]

        local = pltpu.make_async_copy(src(s), out_ref.at[s], local_sem)
        local.start()
        sends = []
        for k, (tx, ty) in enumerate(chips):
            cp = pltpu.make_async_remote_copy(
                src_ref=src(2 * tx + ty), dst_ref=out_ref.at[s], send_sem=send_sems.at[k],
                recv_sem=recv_sems.at[k], device_id=(tx, ty, c), device_id_type=MESH)
            cp.start()
            sends.append(cp)
        for k, (tx, ty) in enumerate(chips):
            t = 2 * tx + ty
            pltpu.make_async_remote_copy(
                src_ref=src(t), dst_ref=out_ref.at[t], send_sem=send_sems.at[k],
                recv_sem=recv_sems.at[k], device_id=(tx, ty, c), device_id_type=MESH).wait_recv()
        for cp in sends:
            cp.wait_send()
        local.wait()

    return pl.pallas_call(
        body, name=name,
        out_shape=jax.ShapeDtypeStruct((N_CHIPS, r, n), xs.dtype),
        in_specs=[pl.BlockSpec(memory_space=pl.ANY)],
        out_specs=pl.BlockSpec(memory_space=pl.ANY),
        scratch_shapes=[pltpu.SemaphoreType.DMA((3,)), pltpu.SemaphoreType.DMA((3,)), pltpu.SemaphoreType.DMA],
    )(xs)


def sibling_swap(xs, name):
    def body(x_ref, out_ref, send_sem, recv_sem):
        x, y, c = _my_pos()
        cp = pltpu.make_async_remote_copy(
            src_ref=x_ref, dst_ref=out_ref, send_sem=send_sem, recv_sem=recv_sem,
            device_id=(x, y, 1 - c), device_id_type=MESH)
        cp.start()
        cp.wait()

    return pl.pallas_call(
        body, name=name,
        out_shape=jax.ShapeDtypeStruct(xs.shape, xs.dtype),
        in_specs=[pl.BlockSpec(memory_space=pl.ANY)],
        out_specs=pl.BlockSpec(memory_space=pl.ANY),
        scratch_shapes=[pltpu.SemaphoreType.DMA, pltpu.SemaphoreType.DMA],
    )(xs)


def sum_leading(xs, name):
    n, r, c = xs.shape

    def body(x_ref, o_ref):
        acc = x_ref[0]
        for i in range(1, n):
            acc = acc + x_ref[i]
        o_ref[...] = acc

    return pl.pallas_call(body, name=name, out_shape=jax.ShapeDtypeStruct((r, c), xs.dtype))(xs)


def add_cast(a, b, out_dtype, name):
    r, c = a.shape
    tr = _pick(r, (512, 256, 128, 64, 32, 16))

    def body(a_ref, b_ref, o_ref):
        o_ref[...] = (a_ref[...] + b_ref[...]).astype(out_dtype)

    spec = pl.BlockSpec((tr, c), lambda i: (i, 0))
    return pl.pallas_call(body, name=name, grid=(r // tr,), in_specs=[spec, spec], out_specs=spec,
                          out_shape=jax.ShapeDtypeStruct((r, c), out_dtype), compiler_params=_params(("parallel",)))(a, b)


def sum_chips(xs, name):
    n, r, c = xs.shape
    tr = _pick(r, (512, 256, 128, 64, 32, 16))

    def body(x_ref, o_ref):
        acc = x_ref[0].astype(F32)
        for i in range(1, n):
            acc = acc + x_ref[i].astype(F32)
        o_ref[...] = acc

    return pl.pallas_call(body, name=name, grid=(r // tr,),
                          in_specs=[pl.BlockSpec((n, tr, c), lambda i: (0, i, 0))],
                          out_specs=pl.BlockSpec((tr, c), lambda i: (i, 0)),
                          out_shape=jax.ShapeDtypeStruct((r, c), F32), compiler_params=_params(("parallel",)))(xs)


def ada_forward(c_all, w_ada):
    nl, d, n = w_ada.shape
    nb = c_all.shape[0]

    def body(c_ref, w_ref, o_ref):
        cv = c_ref[...]
        ca = (cv * _sigmoid(cv)).astype(BF16)
        o_ref[0] = _nn(ca, w_ref[0].astype(BF16))

    return pl.pallas_call(
        body, name="ada_forward", grid=(nl,),
        in_specs=[pl.BlockSpec((nb, d), lambda l: (0, 0)), pl.BlockSpec((1, d, n), lambda l: (l, 0, 0))],
        out_specs=pl.BlockSpec((1, nb, n), lambda l: (l, 0, 0)),
        out_shape=jax.ShapeDtypeStruct((nl, nb, n), F32), compiler_params=_params(("parallel",)))(c_all, w_ada)


def ada_backward(c_all, dmod):
    nl, nb, n = dmod.shape
    d = c_all.shape[1]

    def body(c_ref, g_ref, o_ref):
        cv = c_ref[...]
        ca = (cv * _sigmoid(cv)).astype(BF16)
        o_ref[0] = _tn(ca, g_ref[0].astype(BF16))

    return pl.pallas_call(
        body, name="ada_backward", grid=(nl,),
        in_specs=[pl.BlockSpec((nb, d), lambda l: (0, 0)), pl.BlockSpec((1, nb, n), lambda l: (l, 0, 0))],
        out_specs=pl.BlockSpec((1, d, n), lambda l: (l, 0, 0)),
        out_shape=jax.ShapeDtypeStruct((nl, d, n), F32), compiler_params=_params(("parallel",)))(c_all, dmod)


def matmul_tn(a, b, name):
    k, m = a.shape
    n = b.shape[1]
    tm, tn, tk = _pick(m, (512, 256, 128)), _pick(n, (512, 256, 128)), _pick(k, (512, 256, 128))

    def body(a_ref, b_ref, o_ref):
        @pl.when(pl.program_id(2) == 0)
        def _():
            o_ref[...] = jnp.zeros_like(o_ref)

        o_ref[...] += _tn(a_ref[...], b_ref[...])

    return pl.pallas_call(
        body, name=name, grid=(m // tm, n // tn, k // tk),
        in_specs=[pl.BlockSpec((tk, tm), lambda i, j, kk: (kk, i)), pl.BlockSpec((tk, tn), lambda i, j, kk: (kk, j))],
        out_specs=pl.BlockSpec((tm, tn), lambda i, j, kk: (i, j)),
        out_shape=jax.ShapeDtypeStruct((m, n), F32),
        compiler_params=_params(("parallel", "parallel", "arbitrary")))(a, b)


def adamw(w, g, m, v, name):
    r, c = w.shape
    tr = _pick(r, (512, 256, 128, 64, 32, 16, 8))

    def body(w_ref, g_ref, m_ref, v_ref, d_ref, mo_ref, vo_ref):
        gv = g_ref[...]
        mn = ADAM_B1 * m_ref[...] + (1.0 - ADAM_B1) * gv
        vn = ADAM_B2 * v_ref[...] + (1.0 - ADAM_B2) * (gv * gv)
        m_hat = mn / (1.0 - ADAM_B1 ** ADAM_STEP)
        v_hat = vn / (1.0 - ADAM_B2 ** ADAM_STEP)
        d_ref[...] = -ADAM_LR * (m_hat / (jnp.sqrt(v_hat) + ADAM_EPS) + ADAM_WD * w_ref[...])
        mo_ref[...] = mn
        vo_ref[...] = vn

    spec = pl.BlockSpec((tr, c), lambda i: (i, 0))
    out = jax.ShapeDtypeStruct((r, c), F32)
    return pl.pallas_call(body, name=name, grid=(r // tr,), in_specs=[spec] * 4, out_specs=[spec] * 3,
                          out_shape=[out] * 3, compiler_params=_params(("parallel",)))(w, g, m, v)


def _rope(t, cos_t, sin_t):
    w = t.shape[1]
    lane = _lane(t.shape) & (HEAD_PAD - 1)
    first_half = (lane >= ROPE_LO) & (lane < ROPE_LO + ROPE_HALF)
    partner = jnp.where(first_half, pltpu.roll(t, w - ROPE_HALF, 1), pltpu.roll(t, ROPE_HALF, 1))
    return t * cos_t + partner * sin_t


def _rope_t(dt, cos_t, sin_t):
    w = dt.shape[1]
    lane = _lane(dt.shape) & (HEAD_PAD - 1)
    first_half = (lane >= ROPE_LO) & (lane < ROPE_LO + ROPE_HALF)
    ds = dt * sin_t
    partner = jnp.where(first_half, pltpu.roll(ds, w - ROPE_HALF, 1), pltpu.roll(ds, ROPE_HALF, 1))
    return dt * cos_t + partner


def _rms(xv, g):
    rstd = lax.rsqrt(jnp.mean(xv * xv, axis=-1, keepdims=True) + EPS)
    xh = xv * rstd
    return xh * g, xh, rstd


def _rms_bwd(dy, g, xh, rstd):
    dxh = dy * g
    dx = rstd * (dxh - xh * jnp.mean(dxh * xh, axis=-1, keepdims=True))
    return dx, jnp.sum(dy * xh, axis=0, keepdims=True)


def make_step(cfg):
    S, D, NZ = cfg.S, cfg.D, cfg.NZ
    FW, MW, QL, KVL, FH, MH = cfg.FW, cfg.MW, cfg.QL, cfg.KVL, cfg.FH, cfg.MH
    QW = MH * HEAD_PAD
    TM = _pick(S, (256, 128))
    TQ = TK = _pick(S, (256, 128))
    n_tok = S // TM
    ZO = 3 * FW
    NZR = NZ - ZO
    misc_blk = (cfg.o_ms - ZO) // LANES
    FOX_SCALE = HEAD_DIM ** -0.5
    MLA_SCALE = (HEAD_DIM + ROPE_DIM) ** -0.5

    def tok(width, col=0):
        return pl.BlockSpec((TM, width), lambda i: (i, col))

    def const(shape):
        return pl.BlockSpec(shape, lambda i: tuple(0 for _ in shape))


    def ln_inproj(x, g, scale, shift, w_in):
        def body(x_ref, g_ref, sc_ref, sh_ref, w_ref, h_ref, z_ref, qkv_ref):
            y, _, _ = _rms(x_ref[...], g_ref[...])
            hb = (y * (1.0 + sc_ref[...]) + sh_ref[...]).astype(BF16)
            h_ref[...] = hb
            z = _nn(hb, w_ref[...])
            z_ref[...] = z[:, ZO:]
            qkv_ref[:, :FW] = (z[:, :FW] * FOX_SCALE).astype(BF16)
            qkv_ref[:, FW:] = z[:, FW:ZO].astype(BF16)

        return pl.pallas_call(
            body, name="ln_inproj", grid=(n_tok,),
            in_specs=[tok(D), const((1, D)), const((1, D)), const((1, D)), const((D, NZ))],
            out_specs=[tok(D), tok(NZR), tok(ZO)],
            out_shape=[jax.ShapeDtypeStruct((S, D), BF16), jax.ShapeDtypeStruct((S, NZR), F32),
                       jax.ShapeDtypeStruct((S, ZO), BF16)],
            compiler_params=_params(("parallel",)))(x, g, scale, shift, w_in)

    def _log_f_terms(misc, bf):
        lane = _lane(misc.shape)
        a = misc + bf
        e = jnp.exp(-jnp.abs(a))
        logf = jnp.minimum(a, 0.0) - jnp.log(1.0 + e)
        sig_neg = jnp.where(a >= 0, e, 1.0) / (1.0 + e)
        valid = lane < FH
        return jnp.where(valid, logf, 0.0), jnp.where(valid, sig_neg, 0.0)

    def fox_prep(z, bf_pad):
        def body(z_ref, b_ref, o_ref, carry):
            @pl.when(pl.program_id(0) == 0)
            def _():
                carry[...] = jnp.zeros_like(carry)

            logf, _ = _log_f_terms(z_ref[...], b_ref[...])
            row = lax.broadcasted_iota(jnp.int32, (TM, TM), 0)
            col = lax.broadcasted_iota(jnp.int32, (TM, TM), 1)
            tri = (col <= row).astype(F32)
            cum = jnp.dot(tri, logf, precision=lax.Precision.HIGHEST, preferred_element_type=F32) + carry[...]
            o_ref[...] = cum
            carry[...] = cum[TM - 1:TM, :]

        return pl.pallas_call(
            body, name="fox_prep", grid=(n_tok,),
            in_specs=[tok(LANES, misc_blk), const((1, LANES))], out_specs=tok(LANES),
            out_shape=jax.ShapeDtypeStruct((S, LANES), F32),
            scratch_shapes=[pltpu.VMEM((1, LANES), F32)],
            compiler_params=_params(("arbitrary",)))(z, bf_pad)

    def mla_prep(z, gq, gkv, w_uq, w_uk, w_v, cos_t, sin_t):
        def body(ql_ref, kvl_ref, ms_ref, gq_ref, gkv_ref, wq_ref, wk_ref, wv_ref, c_ref, s_ref, q_ref, k_ref, v_ref):
            cos1, sin1 = c_ref[...], s_ref[...]
            qn, _, _ = _rms(ql_ref[...], gq_ref[...])
            q = _nn(qn.astype(BF16), wq_ref[...])
            q_ref[...] = (_rope(q, jnp.tile(cos1, (1, MH)), jnp.tile(sin1, (1, MH))) * MLA_SCALE).astype(BF16)
            kvn, _, _ = _rms(kvl_ref[...], gkv_ref[...])
            kvb = kvn.astype(BF16)
            lane = _lane((TM, LANES))
            kr = jnp.where((lane >= ROPE_LO) & (lane < ROPE_LO + ROPE_DIM), ms_ref[...], 0.0)
            kr = _rope(kr, cos1, sin1)
            k_ref[...] = (_nn(kvb, wk_ref[...]) + jnp.tile(kr, (1, MH))).astype(BF16)
            v_ref[...] = _nn(kvb, wv_ref[...]).astype(BF16)

        return pl.pallas_call(
            body, name="mla_prep", grid=(n_tok,),
            in_specs=[tok(QL, (cfg.o_ql - ZO) // QL), tok(KVL, (cfg.o_kv - ZO) // KVL), tok(LANES, misc_blk),
                      const((1, QL)), const((1, KVL)), const((QL, QW)), const((KVL, QW)), const((KVL, MW)),
                      tok(LANES), tok(LANES)],
            out_specs=[tok(QW), tok(QW), tok(MW)],
            out_shape=[jax.ShapeDtypeStruct((S, QW), BF16), jax.ShapeDtypeStruct((S, QW), BF16),
                       jax.ShapeDtypeStruct((S, MW), BF16)],
            compiler_params=_params(("parallel",)))(z, z, z, gq, gkv, w_uq, w_uk, w_v, cos_t, sin_t)

    def _allowed(q0, k0, chunked):
        qi = q0 + lax.broadcasted_iota(jnp.int32, (TQ, TK), 0)
        ki = k0 + lax.broadcasted_iota(jnp.int32, (TQ, TK), 1)
        if chunked:
            return (ki >> 6) <= (qi >> 6)
        return ki <= qi

    def _heads(val, packed):
        if packed:
            lane = _lane(val.shape)
            zero = jnp.zeros_like(val)
            return [jnp.where(lane < HEAD_DIM, val, zero), jnp.where(lane >= HEAD_DIM, val, zero)]
        return [val[:, :HEAD_PAD], val[:, HEAD_PAD:]]

    def _merge(a0, a1):
        return jnp.where(_lane(a0.shape) < HEAD_DIM, a0, a1)

    def attn_fwd(q, k, v, q_blk0, k_blk0, v_blk0, cumt, packed, chunked, name):
        has_bias = cumt is not None
        n_pairs = (FH if packed else MH) // 2
        wq = LANES if packed else 2 * HEAD_PAD
        assert TQ == TK

        def body(*refs):
            if has_bias:
                q_ref, k_ref, v_ref, ct_ref, o_ref, lse_ref = refs
            else:
                q_ref, k_ref, v_ref, o_ref, lse_ref = refs
            i = pl.program_id(1)
            q0 = i * TQ
            qh = _heads(q_ref[...], packed)

            def step(kb, carry, masked):
                k0 = pl.multiple_of(kb * TK, TK)
                rows = pl.ds(k0, TK)
                kblk, vblk = k_ref[rows, :], v_ref[rows, :]
                kh = [kblk, kblk] if packed else _heads(kblk, False)
                out = []
                for j in range(2):
                    m, l, acc = carry[3 * j:3 * j + 3]
                    s = _nt(qh[j], kh[j])
                    if has_bias:
                        s = s - ct_ref[0, kb][j:j + 1, :]
                    if masked:
                        s = jnp.where(_allowed(q0, k0, chunked), s, NEG)
                    m_new = jnp.maximum(m, jnp.max(s, axis=-1, keepdims=True))
                    p = jnp.exp(s - m_new)
                    alpha = jnp.exp(m - m_new)
                    l = alpha * l + jnp.sum(p, axis=-1, keepdims=True)
                    acc = alpha * acc + _nn(p.astype(BF16), vblk)
                    out += [m_new, l, acc]
                return tuple(out)

            init = (jnp.full((TQ, 1), NEG, F32), jnp.zeros((TQ, 1), F32), jnp.zeros((TQ, LANES), F32)) * 2
            carry = lax.fori_loop(0, i, lambda kb, cr: step(kb, cr, False), init)
            m0, l0, a0, m1, l1, a1 = step(i, carry, True)
            lane = _lane((TQ, LANES))
            o_ref[...] = _merge(a0 / l0, a1 / l1)
            lse_ref[0] = jnp.where(lane == 0, m0 + jnp.log(l0), jnp.where(lane == 1, m1 + jnp.log(l1), 0.0))

        in_specs = [pl.BlockSpec((TQ, wq), lambda p, i: (i, q_blk0 + p)),
                    pl.BlockSpec((S, wq), lambda p, i: (0, k_blk0 + p)),
                    pl.BlockSpec((S, LANES), lambda p, i: (0, v_blk0 + p))]
        args = [q, k, v]
        if has_bias:
            in_specs += [pl.BlockSpec((1, S // TK, 8, TK), lambda p, i: (p, 0, 0, 0))]
            args += [cumt]
        return pl.pallas_call(
            body, name=name, grid=(n_pairs, S // TQ), in_specs=in_specs,
            out_specs=[pl.BlockSpec((TQ, LANES), lambda p, i: (i, p)),
                       pl.BlockSpec((1, TQ, LANES), lambda p, i: (p, i, 0))],
            out_shape=[jax.ShapeDtypeStruct((S, n_pairs * LANES), F32),
                       jax.ShapeDtypeStruct((n_pairs, S, LANES), F32)],
            compiler_params=_params(("parallel", "parallel")))(*args)

    def gate_outproj(of, om, z, w_out, x, gate):
        def body(of_ref, om_ref, fg_ref, mg_ref, w_ref, x_ref, gt_ref, xn_ref, u_ref, y_ref):
            fg, mg = fg_ref[...], mg_ref[...]
            u = jnp.concatenate([of_ref[...] * fg * _sigmoid(fg), om_ref[...] * mg * _sigmoid(mg)], axis=1).astype(BF16)
            y = _nn(u, w_ref[...])
            u_ref[...] = u
            y_ref[...] = y.astype(BF16)
            xn_ref[...] = x_ref[...] + gt_ref[...] * y

        return pl.pallas_call(
            body, name="gate_outproj", grid=(n_tok,),
            in_specs=[tok(FW), tok(MW), tok(FW, (cfg.o_fg - ZO) // FW), tok(MW, (cfg.o_mg - ZO) // MW),
                      const((FW + MW, D)), tok(D), const((1, D))],
            out_specs=[tok(D), tok(FW + MW), tok(D)],
            out_shape=[jax.ShapeDtypeStruct((S, D), F32), jax.ShapeDtypeStruct((S, FW + MW), BF16),
                       jax.ShapeDtypeStruct((S, D), BF16)],
            compiler_params=_params(("parallel",)))(of, om, z, z, w_out, x, gate)

    def final_loss(x, g, target):
        def body(x_ref, g_ref, t_ref, dx_ref, acc_ref, loss_ref):
            @pl.when(pl.program_id(0) == 0)
            def _():
                acc_ref[...] = jnp.zeros_like(acc_ref)
                loss_ref[...] = jnp.zeros_like(loss_ref)

            gv = g_ref[...]
            y, xh, rstd = _rms(x_ref[...], gv)
            e = y - t_ref[...]
            loss_ref[...] += 0.5 * jnp.sum(jnp.sum(e * e, axis=-1, keepdims=True) / D, axis=0, keepdims=True)
            dx, dg = _rms_bwd(e / D, gv, xh, rstd)
            dx_ref[...] = dx
            acc_ref[0:1, :] += dg

        return pl.pallas_call(
            body, name="final_loss", grid=(n_tok,),
            in_specs=[tok(D), const((1, D)), tok(D)],
            out_specs=[tok(D), const((8, D)), const((1, LANES))],
            out_shape=[jax.ShapeDtypeStruct((S, D), F32), jax.ShapeDtypeStruct((8, D), F32),
                       jax.ShapeDtypeStruct((1, LANES), F32)],
            compiler_params=_params(("arbitrary",)))(x, g, target)


    def bwd_out(dxn, gate, y, w_out, of, om, z):
        def body(dx_ref, gt_ref, y_ref, w_ref, of_ref, om_ref, fg_ref, mg_ref,
                 dof_ref, dom_ref, dfg_ref, dmg_ref, dy_ref, acc_ref):
            @pl.when(pl.program_id(0) == 0)
            def _():
                acc_ref[...] = jnp.zeros_like(acc_ref)

            dxv = dx_ref[...]
            acc_ref[0:1, :] += jnp.sum(dxv * y_ref[...].astype(F32), axis=0, keepdims=True)
            dy = (gt_ref[...] * dxv).astype(BF16)
            dy_ref[...] = dy
            du = _nt(dy, w_ref[...])
            for lo, width, o_ref, g_ref, do_ref, dg_ref in ((0, FW, of_ref, fg_ref, dof_ref, dfg_ref),
                                                            (FW, MW, om_ref, mg_ref, dom_ref, dmg_ref)):
                gv = g_ref[...]
                sg = _sigmoid(gv)
                dup = du[:, lo:lo + width]
                do_ref[...] = (dup * gv * sg).astype(BF16)
                dg_ref[...] = (dup * o_ref[...] * sg * (1.0 + gv * (1.0 - sg))).astype(BF16)

        return pl.pallas_call(
            body, name="bwd_out", grid=(n_tok,),
            in_specs=[tok(D), const((1, D)), tok(D), const((FW + MW, D)), tok(FW), tok(MW),
                      tok(FW, (cfg.o_fg - ZO) // FW), tok(MW, (cfg.o_mg - ZO) // MW)],
            out_specs=[tok(FW), tok(MW), tok(FW), tok(MW), tok(D), const((8, D))],
            out_shape=[jax.ShapeDtypeStruct((S, FW), BF16), jax.ShapeDtypeStruct((S, MW), BF16),
                       jax.ShapeDtypeStruct((S, FW), BF16), jax.ShapeDtypeStruct((S, MW), BF16),
                       jax.ShapeDtypeStruct((S, D), BF16), jax.ShapeDtypeStruct((8, D), F32)],
            compiler_params=_params(("arbitrary",)))(dxn, gate, y, w_out, of, om, z, z)

    def attn_stats(do, o, lse, name):
        n_pairs = lse.shape[0]

        def body(do_ref, o_ref, lse_ref, st_ref):
            d = do_ref[...].astype(F32) * o_ref[...]
            lane = _lane(d.shape)
            d0 = jnp.sum(jnp.where(lane < HEAD_DIM, d, 0.0), axis=-1, keepdims=True)
            d1 = jnp.sum(jnp.where(lane >= HEAD_DIM, d, 0.0), axis=-1, keepdims=True)
            st_ref[0] = jnp.where(lane == 2, d0, jnp.where(lane == 3, d1, lse_ref[0]))

        blk = pl.BlockSpec((TM, LANES), lambda p, i: (i, p))
        sblk = pl.BlockSpec((1, TM, LANES), lambda p, i: (p, i, 0))
        return pl.pallas_call(
            body, name=name, grid=(n_pairs, n_tok), in_specs=[blk, blk, sblk], out_specs=sblk,
            out_shape=jax.ShapeDtypeStruct(lse.shape, F32),
            compiler_params=_params(("parallel", "parallel")))(do, o, lse)

    def attn_bwd(q, k, v, do, stats, q_blk0, k_blk0, v_blk0, cumt, packed, chunked, name):
        has_bias = cumt is not None
        n_pairs = (FH if packed else MH) // 2
        wq = LANES if packed else 2 * HEAD_PAD
        n_q = S // TQ
        assert TQ == TK

        def body(*refs):
            if has_bias:
                q_ref, k_ref, v_ref, do_ref, st_ref, ct_ref, dq_ref, dk_ref, dv_ref, dc_ref, dr_ref = refs
            else:
                q_ref, k_ref, v_ref, do_ref, st_ref, dq_ref, dk_ref, dv_ref = refs
            jb = pl.program_id(1)
            k0 = jb * TK

            @pl.when(jb == 0)
            def _():
                dq_ref[...] = jnp.zeros_like(dq_ref)
                if has_bias:
                    dr_ref[...] = jnp.zeros_like(dr_ref)

            dk_ref[...] = jnp.zeros_like(dk_ref)
            dv_ref[...] = jnp.zeros_like(dv_ref)
            kh = _heads(k_ref[...], packed)
            vh = _heads(v_ref[...], True)
            ck = [ct_ref[0, 0][j:j + 1, :] for j in range(2)] if has_bias else None

            def step(ib, carry, masked):
                q0 = pl.multiple_of(ib * TQ, TQ)
                rows = pl.ds(q0, TQ)
                q2, do2, st = q_ref[rows, :], do_ref[rows, :], st_ref[0, rows, :]
                qh = [q2, q2] if packed else _heads(q2, False)
                dks, dvs, dqs, dcs, drs = [], [], [], [], []
                for j in range(2):
                    s = _nt(qh[j], kh[j])
                    if has_bias:
                        s = s - ck[j]
                    if masked:
                        s = jnp.where(_allowed(q0, k0, chunked), s, NEG)
                    p = jnp.exp(s - st[:, j:j + 1])
                    dvs.append(_tn(p.astype(BF16), do2))
                    ds = p * (_nt(do2, vh[j]) - st[:, 2 + j:3 + j])
                    dsb = ds.astype(BF16)
                    dks.append(_tn(dsb, qh[j]))
                    dqs.append(_nn(dsb, kh[j]))
                    if has_bias:
                        dcs.append(carry[j] - jnp.sum(ds, axis=0, keepdims=True))
                        drs.append(jnp.sum(ds, axis=-1, keepdims=True))
                dv_ref[...] += _merge(dvs[0], dvs[1])
                if packed:
                    dk_ref[...] += _merge(dks[0], dks[1])
                    dq_ref[rows, :] += dqs[0] + dqs[1]
                else:
                    dk_ref[...] += jnp.concatenate(dks, axis=1)
                    dq_ref[rows, :] += jnp.concatenate(dqs, axis=1)
                if has_bias:
                    lane = _lane((TQ, LANES))
                    dr_ref[0, rows, :] += jnp.where(lane == 0, drs[0], jnp.where(lane == 1, drs[1], 0.0))
                    return tuple(dcs)
                return carry

            init = (jnp.zeros((1, TK), F32),) * 2
            carry = step(jb, init, True)
            carry = lax.fori_loop(jb + 1, n_q, lambda ib, cr: step(ib, cr, False), carry)
            if has_bias:
                sub = lax.broadcasted_iota(jnp.int32, (8, TK), 0)
                dc_ref[0, 0] = jnp.where(sub == 0, carry[0], jnp.where(sub == 1, carry[1], 0.0))

        in_specs = [pl.BlockSpec((S, wq), lambda p, j: (0, q_blk0 + p)),
                    pl.BlockSpec((TK, wq), lambda p, j: (j, k_blk0 + p)),
                    pl.BlockSpec((TK, LANES), lambda p, j: (j, v_blk0 + p)),
                    pl.BlockSpec((S, LANES), lambda p, j: (0, p)),
                    pl.BlockSpec((1, S, LANES), lambda p, j: (p, 0, 0))]
        args = [q, k, v, do, stats]
        out_specs = [pl.BlockSpec((S, wq), lambda p, j: (0, p)),
                     pl.BlockSpec((TK, wq), lambda p, j: (j, p)),
                     pl.BlockSpec((TK, LANES), lambda p, j: (j, p))]
        out_shape = [jax.ShapeDtypeStruct((S, n_pairs * wq), F32), jax.ShapeDtypeStruct((S, n_pairs * wq), F32),
                     jax.ShapeDtypeStruct((S, n_pairs * LANES), F32)]
        if has_bias:
            in_specs += [pl.BlockSpec((1, 1, 8, TK), lambda p, j: (p, j, 0, 0))]
            args += [cumt]
            out_specs += [pl.BlockSpec((1, 1, 8, TK), lambda p, j: (p, j, 0, 0)),
                          pl.BlockSpec((1, S, LANES), lambda p, j: (p, 0, 0))]
            out_shape += [jax.ShapeDtypeStruct((n_pairs, S // TK, 8, TK), F32),
                          jax.ShapeDtypeStruct((n_pairs, S, LANES), F32)]
        return pl.pallas_call(
            body, name=name, grid=(n_pairs, S // TK), in_specs=in_specs, out_specs=out_specs, out_shape=out_shape,
            compiler_params=_params(("parallel", "arbitrary")))(*args)

    def fox_post(dcum, z, bf_pad):
        def rev(i):
            return n_tok - 1 - i

        def body(dc_ref, z_ref, b_ref, dff_ref, acc_ref, carry):
            @pl.when(pl.program_id(0) == 0)
            def _():
                carry[...] = jnp.zeros_like(carry)
                acc_ref[...] = jnp.zeros_like(acc_ref)

            _, sig_neg = _log_f_terms(z_ref[...], b_ref[...])
            row = lax.broadcasted_iota(jnp.int32, (TM, TM), 0)
            col = lax.broadcasted_iota(jnp.int32, (TM, TM), 1)
            tri = (col >= row).astype(F32)
            dlog = jnp.dot(tri, dc_ref[...], precision=lax.Precision.HIGHEST, preferred_element_type=F32) + carry[...]
            carry[...] = dlog[0:1, :]
            dff = dlog * sig_neg
            dff_ref[...] = dff
            acc_ref[0:1, :] += jnp.sum(dff, axis=0, keepdims=True)

        return pl.pallas_call(
            body, name="fox_post", grid=(n_tok,),
            in_specs=[pl.BlockSpec((TM, LANES), lambda i: (rev(i), 0)),
                      pl.BlockSpec((TM, LANES), lambda i: (rev(i), misc_blk)), const((1, LANES))],
            out_specs=[pl.BlockSpec((TM, LANES), lambda i: (rev(i), 0)), const((8, LANES))],
            out_shape=[jax.ShapeDtypeStruct((S, LANES), F32), jax.ShapeDtypeStruct((8, LANES), F32)],
            scratch_shapes=[pltpu.VMEM((1, LANES), F32)],
            compiler_params=_params(("arbitrary",)))(dcum, z, bf_pad)

    def mla_post(dq, dk, dv, dff, z, gq, gkv, w_uq, w_uk, w_v, cos_t, sin_t):
        def body(dq_ref, dk_ref, dv_ref, dff_ref, ql_ref, kvl_ref, gq_ref, gkv_ref, wq_ref, wk_ref, wv_ref,
                 c_ref, s_ref, zq_ref, zkv_ref, zms_ref, dwq_ref, dwk_ref, dwv_ref, dgq_ref, dgkv_ref):
            @pl.when(pl.program_id(0) == 0)
            def _():
                for r in (dwq_ref, dwk_ref, dwv_ref, dgq_ref, dgkv_ref):
                    r[...] = jnp.zeros_like(r)

            cos1, sin1 = c_ref[...], s_ref[...]
            gqv, gkvv = gq_ref[...], gkv_ref[...]
            qn, qxh, qrstd = _rms(ql_ref[...], gqv)
            dq_pre = _rope_t(dq_ref[...] * MLA_SCALE, jnp.tile(cos1, (1, MH)), jnp.tile(sin1, (1, MH))).astype(BF16)
            dwq_ref[...] += _tn(qn.astype(BF16), dq_pre)
            dql, dgq = _rms_bwd(_nt(dq_pre, wq_ref[...]), gqv, qxh, qrstd)
            zq_ref[...] = dql.astype(BF16)
            dgq_ref[0:1, :] += dgq

            dkv = dk_ref[...]
            lane = _lane(dkv.shape) & (HEAD_PAD - 1)
            dkn = jnp.where(lane < HEAD_DIM, dkv, 0.0).astype(BF16)
            dkr = dkv[:, 0:HEAD_PAD]
            for hd in range(1, MH):
                dkr = dkr + dkv[:, hd * HEAD_PAD:(hd + 1) * HEAD_PAD]
            lane1 = _lane(dkr.shape)
            dkr = jnp.where((lane1 >= ROPE_LO) & (lane1 < ROPE_LO + ROPE_DIM), dkr, 0.0)
            dkr = _rope_t(dkr, cos1, sin1)
            zms_ref[...] = (dkr + dff_ref[...]).astype(BF16)

            kvn, kxh, krstd = _rms(kvl_ref[...], gkvv)
            kvb = kvn.astype(BF16)
            dvb = dv_ref[...].astype(BF16)
            dwk_ref[...] += _tn(kvb, dkn)
            dwv_ref[...] += _tn(kvb, dvb)
            dkvl, dgkv = _rms_bwd(_nt(dkn, wk_ref[...]) + _nt(dvb, wv_ref[...]), gkvv, kxh, krstd)
            zkv_ref[...] = dkvl.astype(BF16)
            dgkv_ref[0:1, :] += dgkv

        return pl.pallas_call(
            body, name="mla_post", grid=(n_tok,),
            in_specs=[tok(QW), tok(QW), tok(MW), tok(LANES), tok(QL, (cfg.o_ql - ZO) // QL), tok(KVL, (cfg.o_kv - ZO) // KVL),
                      const((1, QL)), const((1, KVL)), const((QL, QW)), const((KVL, QW)), const((KVL, MW)),
                      tok(LANES), tok(LANES)],
            out_specs=[tok(QL), tok(KVL), tok(LANES), const((QL, QW)), const((KVL, QW)), const((KVL, MW)),
                       const((8, QL)), const((8, KVL))],
            out_shape=[jax.ShapeDtypeStruct((S, QL), BF16), jax.ShapeDtypeStruct((S, KVL), BF16),
                       jax.ShapeDtypeStruct((S, LANES), BF16), jax.ShapeDtypeStruct((QL, QW), F32),
                       jax.ShapeDtypeStruct((KVL, QW), F32), jax.ShapeDtypeStruct((KVL, MW), F32),
                       jax.ShapeDtypeStruct((8, QL), F32), jax.ShapeDtypeStruct((8, KVL), F32)],
            compiler_params=_params(("arbitrary",)))(dq, dk, dv, dff, z, z, gq, gkv, w_uq, w_uk, w_v, cos_t, sin_t)

    def bwd_in(dz, w_in, x, dxn, g, scale):
        def body(dz_ref, w_ref, x_ref, dx_ref, g_ref, sc_ref, o_ref, acc_ref):
            @pl.when(pl.program_id(0) == 0)
            def _():
                acc_ref[...] = jnp.zeros_like(acc_ref)

            dh = _nt(dz_ref[...], w_ref[...])
            gv, mod = g_ref[...], 1.0 + sc_ref[...]
            _, xh, rstd = _rms(x_ref[...], gv)
            t = dh * xh
            acc_ref[0:1, :] += jnp.sum(dh, axis=0, keepdims=True)
            acc_ref[1:2, :] += jnp.sum(t * gv, axis=0, keepdims=True)
            acc_ref[2:3, :] += jnp.sum(t * mod, axis=0, keepdims=True)
            dx, _ = _rms_bwd(dh, gv * mod, xh, rstd)
            o_ref[...] = dx_ref[...] + dx

        return pl.pallas_call(
            body, name="bwd_in", grid=(n_tok,),
            in_specs=[tok(NZ), const((D, NZ)), tok(D), tok(D), const((1, D)), const((1, D))],
            out_specs=[tok(D), const((8, D))],
            out_shape=[jax.ShapeDtypeStruct((S, D), F32), jax.ShapeDtypeStruct((8, D), F32)],
            compiler_params=_params(("arbitrary",)))(dz, w_in, x, dxn, g, scale)


    def pair_rows(cum):
        n_pairs = FH // 2
        ct = jnp.pad(cum[:, :FH].T.reshape(n_pairs, 2, S), ((0, 0), (0, 6), (0, 0)))
        return ct.reshape(n_pairs, 8, S // TK, TK).transpose(0, 2, 1, 3)

    def bias_grad(dc, dr):
        n_pairs = FH // 2
        d = dc.transpose(0, 2, 1, 3).reshape(n_pairs, 8, S)[:, :2, :].reshape(FH, S).T
        d = d + dr[:, :, :2].transpose(1, 0, 2).reshape(S, FH)
        return jnp.pad(d, ((0, 0), (0, LANES - FH)))

    def layer_forward(x, wl, mod):
        shift, scale, gate = mod
        h, z, qkv = ln_inproj(x, wl.norm_g, scale, shift, wl.w_in)
        ct = pair_rows(fox_prep(z, wl.bf_pad))
        of, lse_f = attn_fwd(qkv, qkv, qkv, cfg.o_fq // LANES, cfg.o_fk // LANES, cfg.o_fv // LANES, ct,
                             True, False, "fox_fwd")
        qp, kp, vp = mla_prep(z, wl.gq, wl.gkv, wl.w_uq, wl.w_uk, wl.w_v, cfg.cos_t, cfg.sin_t)
        om, lse_m = attn_fwd(qp, kp, vp, 0, 0, 0, None, False, True, "mla_fwd")
        xn, u, y = gate_outproj(of, om, z, wl.w_out, x, gate)
        saved = types.SimpleNamespace(x=x, h=h, z=z, qkv=qkv, ct=ct, of=of, lse_f=lse_f, qp=qp, kp=kp, vp=vp,
                                      om=om, lse_m=lse_m, u=u, y=y)
        return xn, saved

    def layer_backward(dxn, sv, wl, mod):
        shift, scale, gate = mod
        do_f, do_m, dfg, dmg, dy, acc_o = bwd_out(dxn, gate, sv.y, wl.w_out, sv.of, sv.om, sv.z)
        dw_out = matmul_tn(sv.u, dy, "dw_out")
        st_f = attn_stats(do_f, sv.of, sv.lse_f, "fox_stats")
        dfq, dfk, dfv, dck, dcr = attn_bwd(sv.qkv, sv.qkv, sv.qkv, do_f, st_f, cfg.o_fq // LANES, cfg.o_fk // LANES,
                                           cfg.o_fv // LANES, sv.ct, True, False, "fox_bwd")
        dff, acc_f = fox_post(bias_grad(dck, dcr), sv.z, wl.bf_pad)
        st_m = attn_stats(do_m, sv.om, sv.lse_m, "mla_stats")
        dqp, dkp, dvp = attn_bwd(sv.qp, sv.kp, sv.vp, do_m, st_m, 0, 0, 0, None, False, True, "mla_bwd")
        zq, zkv, zms, dw_uq, dw_uk, dw_v, dgq, dgkv = mla_post(
            dqp, dkp, dvp, dff, sv.z, wl.gq, wl.gkv, wl.w_uq, wl.w_uk, wl.w_v, cfg.cos_t, cfg.sin_t)
        dz = jnp.concatenate([(dfq * FOX_SCALE).astype(BF16), dfk.astype(BF16), dfv.astype(BF16), dfg, dmg, zq, zkv, zms], axis=1)
        dx, acc_i = bwd_in(dz, wl.w_in, sv.x, dxn, wl.norm_g, scale)
        dw_in = matmul_tn(sv.h, dz, "dw_in")
        grads = types.SimpleNamespace(
            w_in=dw_in, w_out=dw_out, w_uq=dw_uq, w_uk=dw_uk, w_v=dw_v, gq=dgq[0], gkv=dgkv[0],
            b_f=acc_f[0, :FH], norm_g=acc_i[2], dmod=jnp.concatenate([acc_i[0], acc_i[1], acc_o[0]]))
        return dx, grads

    return types.SimpleNamespace(layer_forward=layer_forward, layer_backward=layer_backward, final_loss=final_loss)


def _pack_rows(parts, dtype, row_multiple):
    flat = jnp.concatenate([p.reshape(-1).astype(dtype) for p in parts])
    per = SLAB_COLS * row_multiple
    total = -(-flat.shape[0] // per) * per
    return jnp.pad(flat, (0, total - flat.shape[0])).reshape(total // SLAB_COLS, SLAB_COLS)


def _unpack(flat, shapes):
    out, off = [], 0
    for shp in shapes:
        n = 1
        for d in shp:
            n *= d
        out.append(flat[off:off + n].reshape(shp))
        off += n
    return out


def kernel(x, c, positions, norm_g, w_ada, b_ada, w_in, b_f, q_norm_g, w_uq, kv_norm_g, w_ukv, w_out, final_g, loss_target, m_norm_g, m_w_ada, m_b_ada, m_w_in, m_b_f, m_q_norm_g, m_w_uq, m_kv_norm_g, m_w_ukv, m_w_out, m_final_g, v_norm_g, v_w_ada, v_b_ada, v_w_in, v_b_f, v_q_norm_g, v_w_uq, v_kv_norm_g, v_w_ukv, v_w_out, v_final_g):
    S, D = x.shape[1], x.shape[2]
    L = norm_g.shape[0]
    FH = b_f.shape[1]
    QL, KVL = q_norm_g.shape[1], kv_norm_g.shape[1]
    MH = w_ukv.shape[2] * N_CHIPS // (2 * HEAD_DIM)
    FW, MW = FH * HEAD_DIM, MH * HEAD_DIM
    NA = w_ada.shape[2]
    n_in = w_in.shape[2] * N_CHIPS
    cfg = types.SimpleNamespace(S=S, D=D, FW=FW, MW=MW, QL=QL, KVL=KVL, FH=FH, MH=MH)
    cfg.o_fq, cfg.o_fk, cfg.o_fv, cfg.o_fg, cfg.o_mg = 0, FW, 2 * FW, 3 * FW, 4 * FW
    cfg.o_ql = 4 * FW + MW
    cfg.o_kv = cfg.o_ql + QL
    cfg.o_ms = cfg.o_kv + KVL
    cfg.NZ = cfg.o_ms + LANES
    assert FW == MW and FH % 2 == 0 and MH % 2 == 0 and cfg.o_ql % QL == 0 and cfg.o_kv % KVL == 0 and KVL == LANES
    assert n_in == 4 * FW + FH + QL + KVL + ROPE_DIM + MW

    mx, my, mc = _my_pos()
    my_chip = 2 * mx + my
    my_dev = 2 * my_chip + mc

    inv_freq = 1.0 / (ROPE_THETA ** (jnp.arange(0, ROPE_DIM, 2, dtype=F32) / ROPE_DIM))
    ang = positions[0].astype(F32)[:, None] * inv_freq
    cos, sin = jnp.cos(ang), jnp.sin(ang)
    cfg.cos_t = jnp.concatenate([jnp.ones((S, ROPE_LO), F32), cos, cos, jnp.ones((S, HEAD_PAD - ROPE_LO - ROPE_DIM), F32)], axis=1)
    cfg.sin_t = jnp.concatenate([jnp.zeros((S, ROPE_LO), F32), -sin, sin, jnp.zeros((S, HEAD_PAD - ROPE_LO - ROPE_DIM), F32)], axis=1)

    shard_shapes = [w_in.shape, w_uq.shape, w_ukv.shape, w_out.shape]
    slab = _pack_rows([w_in, w_uq, w_ukv, w_out], BF16, 32)
    half = slab.shape[0] // 2
    mine = lax.dynamic_slice_in_dim(slab, mc * half, half, axis=0)
    got = chip_exchange(mine, True, "weights_chips")
    other = sibling_swap(got.reshape(N_CHIPS * half, SLAB_COLS), "weights_sibling").reshape(N_CHIPS, half, SLAB_COLS)
    lo = jnp.where(mc == 0, got, other)
    hi = jnp.where(mc == 0, other, got)
    slabs = jnp.concatenate([lo, hi], axis=1).reshape(N_CHIPS, -1)
    per_chip = [_unpack(slabs[t], shard_shapes) for t in range(N_CHIPS)]
    w_in_f = jnp.concatenate([p[0] for p in per_chip], axis=2)
    w_uq_f = jnp.concatenate([p[1] for p in per_chip], axis=2)
    w_ukv_f = jnp.concatenate([p[2] for p in per_chip], axis=2)
    w_out_f = jnp.concatenate([p[3] for p in per_chip], axis=1)

    sizes = (FW, FW, FW, FH, FW, QL, KVL, ROPE_DIM, MW)
    offs = [0]
    for sz in sizes:
        offs.append(offs[-1] + sz)
    seg = [w_in_f[:, :, offs[i]:offs[i + 1]] for i in range(len(sizes))]
    fq_w, fk_w, fv_w, ff_w, fg_w, ql_w, kvl_w, kr_w, mg_w = seg
    zeros = lambda n: jnp.zeros((L, D, n), BF16)
    w_in_p = jnp.concatenate([fq_w, fk_w, fv_w, fg_w, mg_w, ql_w, kvl_w, ff_w, zeros(ROPE_LO - FH), kr_w,
                              zeros(HEAD_PAD - ROPE_LO - ROPE_DIM)], axis=2)
    w_uq_p = jnp.pad(w_uq_f.reshape(L, QL, MH, HEAD_DIM + ROPE_DIM), ((0, 0), (0, 0), (0, 0), (0, HEAD_PAD - HEAD_DIM - ROPE_DIM)))
    w_uq_p = w_uq_p.reshape(L, QL, MH * HEAD_PAD)
    w_ukv4 = w_ukv_f.reshape(L, KVL, MH, 2 * HEAD_DIM)
    w_uk_p = jnp.pad(w_ukv4[..., :HEAD_DIM], ((0, 0), (0, 0), (0, 0), (0, HEAD_PAD - HEAD_DIM))).reshape(L, KVL, MH * HEAD_PAD)
    w_v_p = w_ukv4[..., HEAD_DIM:].reshape(L, KVL, MW)

    c_all = allgather8(c.reshape(8, D // 8), "gather_c").reshape(N_DEV, D)
    c_pad = jnp.pad(c_all, ((0, 16 - N_DEV), (0, 0)))
    mod_part = ada_forward(c_pad, w_ada)[:, :N_DEV, :]
    mod_all = allgather8(mod_part.reshape(-1, LANES), "gather_mod").reshape(N_CHIPS, 2, L, N_DEV, NA)[:, 0]
    mod_full = mod_all.transpose(1, 2, 0, 3).reshape(L, N_DEV, N_CHIPS * NA) + b_ada[:, None, :]
    mod_mine = lax.dynamic_index_in_dim(mod_full, my_dev, axis=1, keepdims=True)

    step = make_step(cfg)
    bf_pad = jnp.pad(b_f, ((0, 0), (0, LANES - FH)))
    layers, mods = [], []
    for l in range(L):
        layers.append(types.SimpleNamespace(
            norm_g=norm_g[l][None], w_in=w_in_p[l], bf_pad=bf_pad[l][None], gq=q_norm_g[l][None], gkv=kv_norm_g[l][None],
            w_uq=w_uq_p[l], w_uk=w_uk_p[l], w_v=w_v_p[l], w_out=w_out_f[l]))
        mods.append((mod_mine[l, :, :D], mod_mine[l, :, D:2 * D], mod_mine[l, :, 2 * D:]))

    xl = x[0]
    saved = []
    for l in range(L):
        xl, sv = step.layer_forward(xl, layers[l], mods[l])
        saved.append(sv)
    dx, acc_fin, loss_part = step.final_loss(xl, final_g[None], loss_target[0])
    loss = lax.psum(loss_part[0, 0], ("x", "y", "c"))
    gl = [None] * L
    for l in reversed(range(L)):
        dx, gl[l] = step.layer_backward(dx, saved[l], layers[l], mods[l])
    grad_x = dx[None]

    stack = lambda name: jnp.stack([getattr(g, name) for g in gl])
    small_parts = [stack("norm_g"), stack("dmod"), stack("b_f"), stack("gq"), stack("gkv"), acc_fin[0]]
    small_shapes = [p.shape for p in small_parts]
    small = _pack_rows(small_parts, F32, 8).reshape(-1, LANES)
    small_all = allgather8(small, "gather_small").reshape(N_DEV, -1, LANES)
    small_sum = sum_leading(small_all, "sum_small")
    g_norm_g, g_b_ada, g_b_f, g_q_norm_g, g_kv_norm_g, g_final_g = _unpack(small_sum.reshape(-1), small_shapes)

    n_ng = L * D
    dmod_all = small_all.reshape(N_DEV, -1)[:, n_ng:n_ng + L * 3 * D].reshape(N_DEV, L, 3 * D)
    dmod_cols = lax.dynamic_slice_in_dim(dmod_all, my_chip * NA, NA, axis=2).transpose(1, 0, 2)
    g_w_ada = ada_backward(c_pad, jnp.pad(dmod_cols, ((0, 0), (0, 16 - N_DEV), (0, 0))))

    dw_in_p = stack("w_in")
    ms = cfg.o_ms
    dw_in_f = jnp.concatenate([
        dw_in_p[:, :, 0:3 * FW], dw_in_p[:, :, ms:ms + FH], dw_in_p[:, :, cfg.o_fg:cfg.o_fg + FW],
        dw_in_p[:, :, cfg.o_ql:cfg.o_ql + QL + KVL], dw_in_p[:, :, ms + ROPE_LO:ms + ROPE_LO + ROPE_DIM],
        dw_in_p[:, :, cfg.o_mg:cfg.o_mg + MW]], axis=2)
    dw_uq_f = stack("w_uq").reshape(L, QL, MH, HEAD_PAD)[..., :HEAD_DIM + ROPE_DIM].reshape(L, QL, -1)
    dw_ukv_f = jnp.concatenate([stack("w_uk").reshape(L, KVL, MH, HEAD_PAD)[..., :HEAD_DIM],
                                stack("w_v").reshape(L, KVL, MH, HEAD_DIM)], axis=3).reshape(L, KVL, -1)
    dw_out_f = stack("w_out")
    dest = []
    for t in range(N_CHIPS):
        dest.append(_pack_rows([
            dw_in_f[:, :, t * w_in.shape[2]:(t + 1) * w_in.shape[2]], dw_uq_f[:, :, t * w_uq.shape[2]:(t + 1) * w_uq.shape[2]],
            dw_ukv_f[:, :, t * w_ukv.shape[2]:(t + 1) * w_ukv.shape[2]], dw_out_f[:, t * w_out.shape[1]:(t + 1) * w_out.shape[1], :]],
            F32, 32))
    gs = jnp.stack(dest)
    ghalf = gs.shape[1] // 2
    gs = gs.reshape(N_CHIPS, 2, ghalf, SLAB_COLS)
    keep = lax.dynamic_index_in_dim(gs, mc, axis=1, keepdims=False).reshape(N_CHIPS * ghalf, SLAB_COLS)
    give = lax.dynamic_index_in_dim(gs, 1 - mc, axis=1, keepdims=False).reshape(N_CHIPS * ghalf, SLAB_COLS)
    sib = sibling_swap(give, "grads_sibling")
    chip_part = add_cast(keep, sib, BF16, "grads_chip_sum").reshape(N_CHIPS, ghalf, SLAB_COLS)
    parts = chip_exchange(chip_part, False, "grads_chips")
    red_half = sum_chips(parts, "grads_sum")
    red_other = sibling_swap(red_half, "grads_back")
    red = jnp.concatenate([jnp.where(mc == 0, red_half, red_other), jnp.where(mc == 0, red_other, red_half)], axis=0)
    g_w_in, g_w_uq, g_w_ukv, g_w_out = _unpack(red.reshape(-1), shard_shapes)

    def big(w, g, m, v, name):
        shp = w.shape
        two = lambda a: a.reshape(-1, shp[-1])
        return [o.reshape(shp) for o in adamw(two(w), two(g), two(m), two(v), name)]

    names = ["norm_g", "w_ada", "b_ada", "w_in", "b_f", "q_norm_g", "w_uq", "kv_norm_g", "w_ukv", "w_out", "final_g"]
    ws = dict(norm_g=norm_g, w_ada=w_ada, b_ada=b_ada, w_in=w_in, b_f=b_f, q_norm_g=q_norm_g, w_uq=w_uq,
              kv_norm_g=kv_norm_g, w_ukv=w_ukv, w_out=w_out, final_g=final_g)
    msd = dict(norm_g=m_norm_g, w_ada=m_w_ada, b_ada=m_b_ada, w_in=m_w_in, b_f=m_b_f, q_norm_g=m_q_norm_g, w_uq=m_w_uq,
               kv_norm_g=m_kv_norm_g, w_ukv=m_w_ukv, w_out=m_w_out, final_g=m_final_g)
    vsd = dict(norm_g=v_norm_g, w_ada=v_w_ada, b_ada=v_b_ada, w_in=v_w_in, b_f=v_b_f, q_norm_g=v_q_norm_g, w_uq=v_w_uq,
               kv_norm_g=v_kv_norm_g, w_ukv=v_w_ukv, w_out=v_w_out, final_g=v_final_g)
    gsd = dict(norm_g=g_norm_g, w_ada=g_w_ada, b_ada=g_b_ada, w_in=g_w_in, b_f=g_b_f, q_norm_g=g_q_norm_g, w_uq=g_w_uq,
               kv_norm_g=g_kv_norm_g, w_ukv=g_w_ukv, w_out=g_w_out, final_g=g_final_g)
    small_names = ["norm_g", "b_ada", "b_f", "q_norm_g", "kv_norm_g", "final_g"]
    sm_shapes = [ws[n].shape for n in small_names]
    pk = lambda d: _pack_rows([d[n] for n in small_names], F32, 8).reshape(-1, LANES)
    sm_out = adamw(pk(ws), pk(gsd), pk(msd), pk(vsd), "adamw_small")
    sm_d, sm_m, sm_v = [dict(zip(small_names, _unpack(o.reshape(-1), sm_shapes))) for o in sm_out]
    delta, new_m, new_v = dict(sm_d), dict(sm_m), dict(sm_v)
    for n in ["w_ada", "w_in", "w_uq", "w_ukv", "w_out"]:
        delta[n], new_m[n], new_v[n] = big(ws[n], gsd[n], msd[n], vsd[n], "adamw_" + n)

    return (loss, grad_x, *[gsd[n] for n in names], *[delta[n] for n in names],
            *[new_m[n] for n in names], *[new_v[n] for n in names])
```

```python
import functools
import types

import jax
import jax.numpy as jnp
from jax import lax
from jax.experimental import pallas as pl
from jax.experimental.pallas import tpu as pltpu

F32 = jnp.float32
BF16 = jnp.bfloat16
MESH = pl.DeviceIdType.MESH

N_CHIPS = 4
N_DEV = 8
HEAD_DIM = 64
ROPE_DIM = 32
ROPE_THETA = 10000.0
HEAD_PAD = 128
ROPE_LO = 64
ROPE_HALF = 16
LANES = 128
EPS = 1e-6
NEG = -1e30
ADAM_LR = 0.001
ADAM_B1 = 0.9
ADAM_B2 = 0.999
ADAM_EPS = 1e-08
ADAM_WD = 0.01
ADAM_STEP = 10
VMEM_LIMIT = 48 * 1024 * 1024
SLAB_COLS = 1024


def _params(sem=None, vmem=VMEM_LIMIT):
    return pltpu.CompilerParams(dimension_semantics=sem, vmem_limit_bytes=vmem)


def _nn(a, b):
    return jnp.dot(a, b, preferred_element_type=F32)


def _nt(a, b):
    return lax.dot_general(a, b, (((1,), (1,)), ((), ())), preferred_element_type=F32)


def _tn(a, b):
    return lax.dot_general(a, b, (((0,), (0,)), ((), ())), preferred_element_type=F32)


def _sigmoid(x):
    return 1.0 / (1.0 + jnp.exp(-x))


def _lane(shape):
    return lax.broadcasted_iota(jnp.int32, shape, len(shape) - 1)


def _pick(n, cands):
    for c in cands:
        if n % c == 0:
            return c
    return n


def _my_pos():
    return lax.axis_index("x"), lax.axis_index("y"), lax.axis_index("c")


def allgather8(xs, name):
    m_per, n = xs.shape

    def body(x_ref, out_ref, send_sems, recv_sems, local_sem):
        x, y, c = _my_pos()
        me, sibling = (x, y, c), (x, y, 1 - c)
        chips = [(1 - x, y), (x, 1 - y), (1 - x, 1 - y)]

        def rows(px, py, pc):
            return out_ref.at[pl.ds((4 * px + 2 * py + pc) * m_per, m_per), :]

        def copy(k, block, to, src=None):
            return pltpu.make_async_remote_copy(
                src_ref=rows(*block) if src is None else src, dst_ref=rows(*block),
                send_sem=send_sems.at[k], recv_sem=recv_sems.at[k], device_id=to, device_id_type=MESH)

        mine = pltpu.make_async_copy(x_ref, rows(*me), local_sem)
        mine.start()
        first = [copy(0, me, sibling, src=x_ref)]
        first += [copy(1 + j, me, (*chip, c), src=x_ref) for j, chip in enumerate(chips)]
        for cp in first:
            cp.start()
        passed = [copy(4 + j, (*chip, c), sibling) for j, chip in enumerate(chips)]
        for j, chip in enumerate(chips):
            copy(1 + j, (*chip, c), me).wait_recv()
            passed[j].start()
        copy(0, sibling, me).wait_recv()
        for j, chip in enumerate(chips):
            copy(4 + j, (*chip, 1 - c), me).wait_recv()
        for cp in first + passed:
            cp.wait_send()
        mine.wait()

    return pl.pallas_call(
        body, name=name,
        out_shape=jax.ShapeDtypeStruct((N_DEV * m_per, n), xs.dtype),
        in_specs=[pl.BlockSpec(memory_space=pltpu.VMEM)],
        out_specs=pl.BlockSpec(memory_space=pltpu.VMEM),
        scratch_shapes=[pltpu.SemaphoreType.DMA((7,)), pltpu.SemaphoreType.DMA((7,)), pltpu.SemaphoreType.DMA],
    )(xs)


def chip_exchange(xs, bcast, name):
    r, n = xs.shape[-2:]

    def body(x_ref, out_ref, send_sems, recv_sems, local_sem):
        x, y, c = _my_pos()
        s = 2 * x + y
        chips = [(1 - x, y), (x, 1 - y), (1 - x, 1 - y)]

        def src(t):
            return x_ref if bcast else x_ref.at[t]

        local = pltpu.make_async_copy(src(s), out_ref.at[s], local_sem)
        local.start()
        sends = []
        for k, (tx, ty) in enumerate(chips):
            cp = pltpu.make_async_remote_copy(
                src_ref=src(2 * tx + ty), dst_ref=out_ref.at[s], send_sem=send_sems.at[k],
                recv_sem=recv_sems.at[k], device_id=(tx, ty, c), device_id_type=MESH)
            cp.start()
            sends.append(cp)
        for k, (tx, ty) in enumerate(chips):
            t = 2 * tx + ty
            pltpu.make_async_remote_copy(
                src_ref=src(t), dst_ref=out_ref.at[t], send_sem=send_sems.at[k],
                recv_sem=recv_sems.at[k], device_id=(tx, ty, c), device_id_type=MESH).wait_recv()
        for cp in sends:
            cp.wait_send()
        local.wait()

    return pl.pallas_call(
        body, name=name,
        out_shape=jax.ShapeDtypeStruct((N_CHIPS, r, n), xs.dtype),
        in_specs=[pl.BlockSpec(memory_space=pl.ANY)],
        out_specs=pl.BlockSpec(memory_space=pl.ANY),
        scratch_shapes=[pltpu.SemaphoreType.DMA((3,)), pltpu.SemaphoreType.DMA((3,)), pltpu.SemaphoreType.DMA],
    )(xs)


def sibling_swap(xs, name):
    def body(x_ref, out_ref, send_sem, recv_sem):
        x, y, c = _my_pos()
        cp = pltpu.make_async_remote_copy(
            src_ref=x_ref, dst_ref=out_ref, send_sem=send_sem, recv_sem=recv_sem,
            device_id=(x, y, 1 - c), device_id_type=MESH)
        cp.start()
        cp.wait()

    return pl.pallas_call(
        body, name=name,
        out_shape=jax.ShapeDtypeStruct(xs.shape, xs.dtype),
        in_specs=[pl.BlockSpec(memory_space=pl.ANY)],
        out_specs=pl.BlockSpec(memory_space=pl.ANY),
        scratch_shapes=[pltpu.SemaphoreType.DMA, pltpu.SemaphoreType.DMA],
    )(xs)


def sum_leading(xs, name):
    n, r, c = xs.shape

    def body(x_ref, o_ref):
        acc = x_ref[0]
        for i in range(1, n):
            acc = acc + x_ref[i]
        o_ref[...] = acc

    return pl.pallas_call(body, name=name, out_shape=jax.ShapeDtypeStruct((r, c), xs.dtype))(xs)


def add_cast(a, b, out_dtype, name):
    r, c = a.shape
    tr = _pick(r, (512, 256, 128, 64, 32, 16))

    def body(a_ref, b_ref, o_ref):
        o_ref[...] = (a_ref[...] + b_ref[...]).astype(out_dtype)

    spec = pl.BlockSpec((tr, c), lambda i: (i, 0))
    return pl.pallas_call(body, name=name, grid=(r // tr,), in_specs=[spec, spec], out_specs=spec,
                          out_shape=jax.ShapeDtypeStruct((r, c), out_dtype), compiler_params=_params(("parallel",)))(a, b)


def sum_chips(xs, name):
    n, r, c = xs.shape
    tr = _pick(r, (512, 256, 128, 64, 32, 16))

    def body(x_ref, o_ref):
        acc = x_ref[0].astype(F32)
        for i in range(1, n):
            acc = acc + x_ref[i].astype(F32)
        o_ref[...] = acc

    return pl.pallas_call(body, name=name, grid=(r // tr,),
                          in_specs=[pl.BlockSpec((n, tr, c), lambda i: (0, i, 0))],
                          out_specs=pl.BlockSpec((tr, c), lambda i: (i, 0)),
                          out_shape=jax.ShapeDtypeStruct((r, c), F32), compiler_params=_params(("parallel",)))(xs)


def ada_forward(c_all, w_ada):
    nl, d, n = w_ada.shape
    nb = c_all.shape[0]

    def body(c_ref, w_ref, o_ref):
        cv = c_ref[...]
        ca = (cv * _sigmoid(cv)).astype(BF16)
        o_ref[0] = _nn(ca, w_ref[0].astype(BF16))

    return pl.pallas_call(
        body, name="ada_forward", grid=(nl,),
        in_specs=[pl.BlockSpec((nb, d), lambda l: (0, 0)), pl.BlockSpec((1, d, n), lambda l: (l, 0, 0))],
        out_specs=pl.BlockSpec((1, nb, n), lambda l: (l, 0, 0)),
        out_shape=jax.ShapeDtypeStruct((nl, nb, n), F32), compiler_params=_params(("parallel",)))(c_all, w_ada)


def ada_backward(c_all, dmod):
    nl, nb, n = dmod.shape
    d = c_all.shape[1]

    def body(c_ref, g_ref, o_ref):
        cv = c_ref[...]
        ca = (cv * _sigmoid(cv)).astype(BF16)
        o_ref[0] = _tn(ca, g_ref[0].astype(BF16))

    return pl.pallas_call(
        body, name="ada_backward", grid=(nl,),
        in_specs=[pl.BlockSpec((nb, d), lambda l: (0, 0)), pl.BlockSpec((1, nb, n), lambda l: (l, 0, 0))],
        out_specs=pl.BlockSpec((1, d, n), lambda l: (l, 0, 0)),
        out_shape=jax.ShapeDtypeStruct((nl, d, n), F32), compiler_params=_params(("parallel",)))(c_all, dmod)


def matmul_tn(a, b, name):
    k, m = a.shape
    n = b.shape[1]
    tm, tn, tk = _pick(m, (512, 256, 128)), _pick(n, (512, 256, 128)), _pick(k, (512, 256, 128))

    def body(a_ref, b_ref, o_ref):
        @pl.when(pl.program_id(2) == 0)
        def _():
            o_ref[...] = jnp.zeros_like(o_ref)

        o_ref[...] += _tn(a_ref[...], b_ref[...])

    return pl.pallas_call(
        body, name=name, grid=(m // tm, n // tn, k // tk),
        in_specs=[pl.BlockSpec((tk, tm), lambda i, j, kk: (kk, i)), pl.BlockSpec((tk, tn), lambda i, j, kk: (kk, j))],
        out_specs=pl.BlockSpec((tm, tn), lambda i, j, kk: (i, j)),
        out_shape=jax.ShapeDtypeStruct((m, n), F32),
        compiler_params=_params(("parallel", "parallel", "arbitrary")))(a, b)


def adamw(w, g, m, v, name):
    r, c = w.shape
    tr = _pick(r, (512, 256, 128, 64, 32, 16, 8))

    def body(w_ref, g_ref, m_ref, v_ref, d_ref, mo_ref, vo_ref):
        gv = g_ref[...]
        mn = ADAM_B1 * m_ref[...] + (1.0 - ADAM_B1) * gv
        vn = ADAM_B2 * v_ref[...] + (1.0 - ADAM_B2) * (gv * gv)
        m_hat = mn / (1.0 - ADAM_B1 ** ADAM_STEP)
        v_hat = vn / (1.0 - ADAM_B2 ** ADAM_STEP)
        d_ref[...] = -ADAM_LR * (m_hat / (jnp.sqrt(v_hat) + ADAM_EPS) + ADAM_WD * w_ref[...])
        mo_ref[...] = mn
        vo_ref[...] = vn

    spec = pl.BlockSpec((tr, c), lambda i: (i, 0))
    out = jax.ShapeDtypeStruct((r, c), F32)
    return pl.pallas_call(body, name=name, grid=(r // tr,), in_specs=[spec] * 4, out_specs=[spec] * 3,
                          out_shape=[out] * 3, compiler_params=_params(("parallel",)))(w, g, m, v)


def _rope(t, cos_t, sin_t):
    w = t.shape[1]
    lane = _lane(t.shape) & (HEAD_PAD - 1)
    first_half = (lane >= ROPE_LO) & (lane < ROPE_LO + ROPE_HALF)
    partner = jnp.where(first_half, pltpu.roll(t, w - ROPE_HALF, 1), pltpu.roll(t, ROPE_HALF, 1))
    return t * cos_t + partner * sin_t


def _rope_t(dt, cos_t, sin_t):
    w = dt.shape[1]
    lane = _lane(dt.shape) & (HEAD_PAD - 1)
    first_half = (lane >= ROPE_LO) & (lane < ROPE_LO + ROPE_HALF)
    ds = dt * sin_t
    partner = jnp.where(first_half, pltpu.roll(ds, w - ROPE_HALF, 1), pltpu.roll(ds, ROPE_HALF, 1))
    return dt * cos_t + partner


def _rms(xv, g):
    rstd = lax.rsqrt(jnp.mean(xv * xv, axis=-1, keepdims=True) + EPS)
    xh = xv * rstd
    return xh * g, xh, rstd


def _rms_bwd(dy, g, xh, rstd):
    dxh = dy * g
    dx = rstd * (dxh - xh * jnp.mean(dxh * xh, axis=-1, keepdims=True))
    return dx, jnp.sum(dy * xh, axis=0, keepdims=True)


def make_step(cfg):
    S, D, NZ = cfg.S, cfg.D, cfg.NZ
    FW, MW, QL, KVL, FH, MH = cfg.FW, cfg.MW, cfg.QL, cfg.KVL, cfg.FH, cfg.MH
    QW = MH * HEAD_PAD
    TM = _pick(S, (256, 128))
    TQ = TK = _pick(S, (256, 128))
    n_tok = S // TM
    ZO = 3 * FW
    NZR = NZ - ZO
    misc_blk = (cfg.o_ms - ZO) // LANES
    FOX_SCALE = HEAD_DIM ** -0.5
    MLA_SCALE = (HEAD_DIM + ROPE_DIM) ** -0.5

    def tok(width, col=0):
        return pl.BlockSpec((TM, width), lambda i: (i, col))

    def const(shape):
        return pl.BlockSpec(shape, lambda i: tuple(0 for _ in shape))


    def ln_inproj(x, g, scale, shift, w_in):
        def body(x_ref, g_ref, sc_ref, sh_ref, w_ref, h_ref, z_ref, qkv_ref):
            y, _, _ = _rms(x_ref[...], g_ref[...])
            hb = (y * (1.0 + sc_ref[...]) + sh_ref[...]).astype(BF16)
            h_ref[...] = hb
            z = _nn(hb, w_ref[...])
            z_ref[...] = z[:, ZO:]
            qkv_ref[:, :FW] = (z[:, :FW] * FOX_SCALE).astype(BF16)
            qkv_ref[:, FW:] = z[:, FW:ZO].astype(BF16)

        return pl.pallas_call(
            body, name="ln_inproj", grid=(n_tok,),
            in_specs=[tok(D), const((1, D)), const((1, D)), const((1, D)), const((D, NZ))],
            out_specs=[tok(D), tok(NZR), tok(ZO)],
            out_shape=[jax.ShapeDtypeStruct((S, D), BF16), jax.ShapeDtypeStruct((S, NZR), F32),
                       jax.ShapeDtypeStruct((S, ZO), BF16)],
            compiler_params=_params(("parallel",)))(x, g, scale, shift, w_in)

    def _log_f_terms(misc, bf):
        lane = _lane(misc.shape)
        a = misc + bf
        e = jnp.exp(-jnp.abs(a))
        logf = jnp.minimum(a, 0.0) - jnp.log(1.0 + e)
        sig_neg = jnp.where(a >= 0, e, 1.0) / (1.0 + e)
        valid = lane < FH
        return jnp.where(valid, logf, 0.0), jnp.where(valid, sig_neg, 0.0)

    def fox_prep(z, bf_pad):
        def body(z_ref, b_ref, o_ref, carry):
            @pl.when(pl.program_id(0) == 0)
            def _():
                carry[...] = jnp.zeros_like(carry)

            logf, _ = _log_f_terms(z_ref[...], b_ref[...])
            row = lax.broadcasted_iota(jnp.int32, (TM, TM), 0)
            col = lax.broadcasted_iota(jnp.int32, (TM, TM), 1)
            tri = (col <= row).astype(F32)
            cum = jnp.dot(tri, logf, precision=lax.Precision.HIGHEST, preferred_element_type=F32) + carry[...]
            o_ref[...] = cum
            carry[...] = cum[TM - 1:TM, :]

        return pl.pallas_call(
            body, name="fox_prep", grid=(n_tok,),
            in_specs=[tok(LANES, misc_blk), const((1, LANES))], out_specs=tok(LANES),
            out_shape=jax.ShapeDtypeStruct((S, LANES), F32),
            scratch_shapes=[pltpu.VMEM((1, LANES), F32)],
            compiler_params=_params(("arbitrary",)))(z, bf_pad)

    def mla_prep(z, gq, gkv, w_uq, w_uk, w_v, cos_t, sin_t):
        def body(ql_ref, kvl_ref, ms_ref, gq_ref, gkv_ref, wq_ref, wk_ref, wv_ref, c_ref, s_ref, q_ref, k_ref, v_ref):
            cos1, sin1 = c_ref[...], s_ref[...]
            qn, _, _ = _rms(ql_ref[...], gq_ref[...])
            q = _nn(qn.astype(BF16), wq_ref[...])
            q_ref[...] = (_rope(q, jnp.tile(cos1, (1, MH)), jnp.tile(sin1, (1, MH))) * MLA_SCALE).astype(BF16)
            kvn, _, _ = _rms(kvl_ref[...], gkv_ref[...])
            kvb = kvn.astype(BF16)
            lane = _lane((TM, LANES))
            kr = jnp.where((lane >= ROPE_LO) & (lane < ROPE_LO + ROPE_DIM), ms_ref[...], 0.0)
            kr = _rope(kr, cos1, sin1)
            k_ref[...] = (_nn(kvb, wk_ref[...]) + jnp.tile(kr, (1, MH))).astype(BF16)
            v_ref[...] = _nn(kvb, wv_ref[...]).astype(BF16)

        return pl.pallas_call(
            body, name="mla_prep", grid=(n_tok,),
            in_specs=[tok(QL, (cfg.o_ql - ZO) // QL), tok(KVL, (cfg.o_kv - ZO) // KVL), tok(LANES, misc_blk),
                      const((1, QL)), const((1, KVL)), const((QL, QW)), const((KVL, QW)), const((KVL, MW)),
                      tok(LANES), tok(LANES)],
            out_specs=[tok(QW), tok(QW), tok(MW)],
            out_shape=[jax.ShapeDtypeStruct((S, QW), BF16), jax.ShapeDtypeStruct((S, QW), BF16),
                       jax.ShapeDtypeStruct((S, MW), BF16)],
            compiler_params=_params(("parallel",)))(z, z, z, gq, gkv, w_uq, w_uk, w_v, cos_t, sin_t)

    def _allowed(q0, k0, chunked):
        qi = q0 + lax.broadcasted_iota(jnp.int32, (TQ, TK), 0)
        ki = k0 + lax.broadcasted_iota(jnp.int32, (TQ, TK), 1)
        if chunked:
            return (ki >> 6) <= (qi >> 6)
        return ki <= qi

    def _heads(val, packed):
        if packed:
            lane = _lane(val.shape)
            zero = jnp.zeros_like(val)
            return [jnp.where(lane < HEAD_DIM, val, zero), jnp.where(lane >= HEAD_DIM, val, zero)]
        return [val[:, :HEAD_PAD], val[:, HEAD_PAD:]]

    def _merge(a0, a1):
        return jnp.where(_lane(a0.shape) < HEAD_DIM, a0, a1)

    def attn_fwd(q, k, v, q_blk0, k_blk0, v_blk0, cumt, packed, chunked, name):
        has_bias = cumt is not None
        n_pairs = (FH if packed else MH) // 2
        wq = LANES if packed else 2 * HEAD_PAD
        assert TQ == TK

        def body(*refs):
            if has_bias:
                q_ref, k_ref, v_ref, ct_ref, o_ref, lse_ref = refs
            else:
                q_ref, k_ref, v_ref, o_ref, lse_ref = refs
            i = pl.program_id(1)
            q0 = i * TQ
            qh = _heads(q_ref[...], packed)
            q_both = jnp.concatenate(qh, axis=0) if packed else None

            def step(kb, carry, masked):
                m, l, acc = carry
                k0 = pl.multiple_of(kb * TK, TK)
                rows = pl.ds(k0, TK)
                kblk, vblk = k_ref[rows, :], v_ref[rows, :]
                if packed:
                    s = _nt(q_both, kblk)
                    s0, s1 = s[:TQ], s[TQ:]
                else:
                    s0, s1 = _nt(qh[0], kblk[:, :HEAD_PAD]), _nt(qh[1], kblk[:, HEAD_PAD:])
                if has_bias:
                    ck = ct_ref[0, kb]
                    s0, s1 = s0 - ck[0:1, :], s1 - ck[1:2, :]
                if masked:
                    allow = _allowed(q0, k0, chunked)
                    s0, s1 = jnp.where(allow, s0, NEG), jnp.where(allow, s1, NEG)
                s = jnp.concatenate([s0, s1], axis=0)
                m_new = jnp.maximum(m, jnp.max(s, axis=-1, keepdims=True))
                p = jnp.exp(s - m_new)
                alpha = jnp.exp(m - m_new)
                l = alpha * l + jnp.sum(p, axis=-1, keepdims=True)
                acc = alpha * acc + _nn(p.astype(BF16), vblk)
                return m_new, l, acc

            init = (jnp.full((2 * TQ, 1), NEG, F32), jnp.zeros((2 * TQ, 1), F32), jnp.zeros((2 * TQ, LANES), F32))
            carry = lax.fori_loop(0, i, lambda kb, cr: step(kb, cr, False), init)
            m, l, acc = step(i, carry, True)
            o = acc / l
            lse = m + jnp.log(l)
            lane = _lane((TQ, LANES))
            o_ref[...] = _merge(o[:TQ], o[TQ:])
            lse_ref[0] = jnp.where(lane == 0, lse[:TQ], jnp.where(lane == 1, lse[TQ:], 0.0))

        in_specs = [pl.BlockSpec((TQ, wq), lambda p, i: (i, q_blk0 + p)),
                    pl.BlockSpec((S, wq), lambda p, i: (0, k_blk0 + p)),
                    pl.BlockSpec((S, LANES), lambda p, i: (0, v_blk0 + p))]
        args = [q, k, v]
        if has_bias:
            in_specs += [pl.BlockSpec((1, S // TK, 8, TK), lambda p, i: (p, 0, 0, 0))]
            args += [cumt]
        return pl.pallas_call(
            body, name=name, grid=(n_pairs, S // TQ), in_specs=in_specs,
            out_specs=[pl.BlockSpec((TQ, LANES), lambda p, i: (i, p)),
                       pl.BlockSpec((1, TQ, LANES), lambda p, i: (p, i, 0))],
            out_shape=[jax.ShapeDtypeStruct((S, n_pairs * LANES), F32),
                       jax.ShapeDtypeStruct((n_pairs, S, LANES), F32)],
            compiler_params=_params(("parallel", "parallel")))(*args)

    def gate_outproj(of, om, z, w_out, x, gate):
        def body(of_ref, om_ref, fg_ref, mg_ref, w_ref, x_ref, gt_ref, xn_ref, u_ref, y_ref):
            fg, mg = fg_ref[...], mg_ref[...]
            u = jnp.concatenate([of_ref[...] * fg * _sigmoid(fg), om_ref[...] * mg * _sigmoid(mg)], axis=1).astype(BF16)
            y = _nn(u, w_ref[...])
            u_ref[...] = u
            y_ref[...] = y.astype(BF16)
            xn_ref[...] = x_ref[...] + gt_ref[...] * y

        return pl.pallas_call(
            body, name="gate_outproj", grid=(n_tok,),
            in_specs=[tok(FW), tok(MW), tok(FW, (cfg.o_fg - ZO) // FW), tok(MW, (cfg.o_mg - ZO) // MW),
                      const((FW + MW, D)), tok(D), const((1, D))],
            out_specs=[tok(D), tok(FW + MW), tok(D)],
            out_shape=[jax.ShapeDtypeStruct((S, D), F32), jax.ShapeDtypeStruct((S, FW + MW), BF16),
                       jax.ShapeDtypeStruct((S, D), BF16)],
            compiler_params=_params(("parallel",)))(of, om, z, z, w_out, x, gate)

    def final_loss(x, g, target):
        def body(x_ref, g_ref, t_ref, dx_ref, acc_ref, loss_ref):
            @pl.when(pl.program_id(0) == 0)
            def _():
                acc_ref[...] = jnp.zeros_like(acc_ref)
                loss_ref[...] = jnp.zeros_like(loss_ref)

            gv = g_ref[...]
            y, xh, rstd = _rms(x_ref[...], gv)
            e = y - t_ref[...]
            loss_ref[...] += 0.5 * jnp.sum(jnp.sum(e * e, axis=-1, keepdims=True) / D, axis=0, keepdims=True)
            dx, dg = _rms_bwd(e / D, gv, xh, rstd)
            dx_ref[...] = dx
            acc_ref[0:1, :] += dg

        return pl.pallas_call(
            body, name="final_loss", grid=(n_tok,),
            in_specs=[tok(D), const((1, D)), tok(D)],
            out_specs=[tok(D), const((8, D)), const((1, LANES))],
            out_shape=[jax.ShapeDtypeStruct((S, D), F32), jax.ShapeDtypeStruct((8, D), F32),
                       jax.ShapeDtypeStruct((1, LANES), F32)],
            compiler_params=_params(("arbitrary",)))(x, g, target)


    def bwd_out(dxn, gate, y, w_out, of, om, z):
        def body(dx_ref, gt_ref, y_ref, w_ref, of_ref, om_ref, fg_ref, mg_ref,
                 dof_ref, dom_ref, dfg_ref, dmg_ref, dy_ref, acc_ref):
            @pl.when(pl.program_id(0) == 0)
            def _():
                acc_ref[...] = jnp.zeros_like(acc_ref)

            dxv = dx_ref[...]
            acc_ref[0:1, :] += jnp.sum(dxv * y_ref[...].astype(F32), axis=0, keepdims=True)
            dy = (gt_ref[...] * dxv).astype(BF16)
            dy_ref[...] = dy
            du = _nt(dy, w_ref[...])
            for lo, width, o_ref, g_ref, do_ref, dg_ref in ((0, FW, of_ref, fg_ref, dof_ref, dfg_ref),
                                                            (FW, MW, om_ref, mg_ref, dom_ref, dmg_ref)):
                gv = g_ref[...]
                sg = _sigmoid(gv)
                dup = du[:, lo:lo + width]
                do_ref[...] = (dup * gv * sg).astype(BF16)
                dg_ref[...] = (dup * o_ref[...] * sg * (1.0 + gv * (1.0 - sg))).astype(BF16)

        return pl.pallas_call(
            body, name="bwd_out", grid=(n_tok,),
            in_specs=[tok(D), const((1, D)), tok(D), const((FW + MW, D)), tok(FW), tok(MW),
                      tok(FW, (cfg.o_fg - ZO) // FW), tok(MW, (cfg.o_mg - ZO) // MW)],
            out_specs=[tok(FW), tok(MW), tok(FW), tok(MW), tok(D), const((8, D))],
            out_shape=[jax.ShapeDtypeStruct((S, FW), BF16), jax.ShapeDtypeStruct((S, MW), BF16),
                       jax.ShapeDtypeStruct((S, FW), BF16), jax.ShapeDtypeStruct((S, MW), BF16),
                       jax.ShapeDtypeStruct((S, D), BF16), jax.ShapeDtypeStruct((8, D), F32)],
            compiler_params=_params(("arbitrary",)))(dxn, gate, y, w_out, of, om, z, z)

    def attn_stats(do, o, lse, name):
        n_pairs = lse.shape[0]

        def body(do_ref, o_ref, lse_ref, st_ref):
            d = do_ref[...].astype(F32) * o_ref[...]
            lane = _lane(d.shape)
            d0 = jnp.sum(jnp.where(lane < HEAD_DIM, d, 0.0), axis=-1, keepdims=True)
            d1 = jnp.sum(jnp.where(lane >= HEAD_DIM, d, 0.0), axis=-1, keepdims=True)
            st_ref[0] = jnp.where(lane == 2, d0, jnp.where(lane == 3, d1, lse_ref[0]))

        blk = pl.BlockSpec((TM, LANES), lambda p, i: (i, p))
        sblk = pl.BlockSpec((1, TM, LANES), lambda p, i: (p, i, 0))
        return pl.pallas_call(
            body, name=name, grid=(n_pairs, n_tok), in_specs=[blk, blk, sblk], out_specs=sblk,
            out_shape=jax.ShapeDtypeStruct(lse.shape, F32),
            compiler_params=_params(("parallel", "parallel")))(do, o, lse)

    def attn_bwd(q, k, v, do, stats, q_blk0, k_blk0, v_blk0, cumt, packed, chunked, name):
        has_bias = cumt is not None
        n_pairs = (FH if packed else MH) // 2
        wq = LANES if packed else 2 * HEAD_PAD
        n_q = S // TQ
        assert TQ == TK

        def body(*refs):
            if has_bias:
                q_ref, k_ref, v_ref, do_ref, st_ref, ct_ref, dq_ref, dk_ref, dv_ref, dc_ref, dr_ref = refs
            else:
                q_ref, k_ref, v_ref, do_ref, st_ref, dq_ref, dk_ref, dv_ref = refs
            jb = pl.program_id(1)
            k0 = jb * TK

            @pl.when(jb == 0)
            def _():
                dq_ref[...] = jnp.zeros_like(dq_ref)
                if has_bias:
                    dr_ref[...] = jnp.zeros_like(dr_ref)

            dk_ref[...] = jnp.zeros_like(dk_ref)
            dv_ref[...] = jnp.zeros_like(dv_ref)
            kh = _heads(k_ref[...], packed)
            k_both = jnp.concatenate(kh, axis=0) if packed else None
            v_both = jnp.concatenate(_heads(v_ref[...], True), axis=0)
            ck = jnp.concatenate([ct_ref[0, 0][0:1, :], ct_ref[0, 0][1:2, :]], axis=1) if has_bias else None

            def step(ib, carry, masked):
                q0 = pl.multiple_of(ib * TQ, TQ)
                rows = pl.ds(q0, TQ)
                q2, do2, st = q_ref[rows, :], do_ref[rows, :], st_ref[0, rows, :]
                if packed:
                    s = _nt(q2, k_both)
                else:
                    qh = _heads(q2, False)
                    s = jnp.concatenate([_nt(qh[0], kh[0]), _nt(qh[1], kh[1])], axis=1)
                if has_bias:
                    s = s - ck
                if masked:
                    allow = _allowed(q0, k0, chunked)
                    s = jnp.where(jnp.concatenate([allow, allow], axis=1), s, NEG)
                p = jnp.concatenate([jnp.exp(s[:, :TK] - st[:, 0:1]), jnp.exp(s[:, TK:] - st[:, 1:2])], axis=1)
                dv2 = _tn(p.astype(BF16), do2)
                dp = _nt(do2, v_both)
                ds = jnp.concatenate([p[:, :TK] * (dp[:, :TK] - st[:, 2:3]), p[:, TK:] * (dp[:, TK:] - st[:, 3:4])], axis=1)
                dsb = ds.astype(BF16)
                dv_ref[...] += _merge(dv2[:TK], dv2[TK:])
                if packed:
                    dk2 = _tn(dsb, q2)
                    dk_ref[...] += _merge(dk2[:TK], dk2[TK:])
                    dq_ref[rows, :] += _nn(dsb, k_both)
                else:
                    dk_ref[...] += jnp.concatenate([_tn(dsb[:, :TK], qh[0]), _tn(dsb[:, TK:], qh[1])], axis=1)
                    dq_ref[rows, :] += jnp.concatenate([_nn(dsb[:, :TK], kh[0]), _nn(dsb[:, TK:], kh[1])], axis=1)
                if has_bias:
                    lane = _lane((TQ, LANES))
                    r0 = jnp.sum(ds[:, :TK], axis=-1, keepdims=True)
                    r1 = jnp.sum(ds[:, TK:], axis=-1, keepdims=True)
                    dr_ref[0, rows, :] += jnp.where(lane == 0, r0, jnp.where(lane == 1, r1, 0.0))
                    return carry - jnp.sum(ds, axis=0, keepdims=True)
                return carry

            carry = step(jb, jnp.zeros((1, 2 * TK), F32), True)
            carry = lax.fori_loop(jb + 1, n_q, lambda ib, cr: step(ib, cr, False), carry)
            if has_bias:
                sub = lax.broadcasted_iota(jnp.int32, (8, TK), 0)
                dc_ref[0, 0] = jnp.where(sub == 0, carry[:, :TK], jnp.where(sub == 1, carry[:, TK:], 0.0))

        in_specs = [pl.BlockSpec((S, wq), lambda p, j: (0, q_blk0 + p)),
                    pl.BlockSpec((TK, wq), lambda p, j: (j, k_blk0 + p)),
                    pl.BlockSpec((TK, LANES), lambda p, j: (j, v_blk0 + p)),
                    pl.BlockSpec((S, LANES), lambda p, j: (0, p)),
                    pl.BlockSpec((1, S, LANES), lambda p, j: (p, 0, 0))]
        args = [q, k, v, do, stats]
        out_specs = [pl.BlockSpec((S, wq), lambda p, j: (0, p)),
                     pl.BlockSpec((TK, wq), lambda p, j: (j, p)),
                     pl.BlockSpec((TK, LANES), lambda p, j: (j, p))]
        out_shape = [jax.ShapeDtypeStruct((S, n_pairs * wq), F32), jax.ShapeDtypeStruct((S, n_pairs * wq), F32),
                     jax.ShapeDtypeStruct((S, n_pairs * LANES), F32)]
        if has_bias:
            in_specs += [pl.BlockSpec((1, 1, 8, TK), lambda p, j: (p, j, 0, 0))]
            args += [cumt]
            out_specs += [pl.BlockSpec((1, 1, 8, TK), lambda p, j: (p, j, 0, 0)),
                          pl.BlockSpec((1, S, LANES), lambda p, j: (p, 0, 0))]
            out_shape += [jax.ShapeDtypeStruct((n_pairs, S // TK, 8, TK), F32),
                          jax.ShapeDtypeStruct((n_pairs, S, LANES), F32)]
        return pl.pallas_call(
            body, name=name, grid=(n_pairs, S // TK), in_specs=in_specs, out_specs=out_specs, out_shape=out_shape,
            compiler_params=_params(("parallel", "arbitrary")))(*args)

    def fox_post(dcum, z, bf_pad):
        def rev(i):
            return n_tok - 1 - i

        def body(dc_ref, z_ref, b_ref, dff_ref, acc_ref, carry):
            @pl.when(pl.program_id(0) == 0)
            def _():
                carry[...] = jnp.zeros_like(carry)
                acc_ref[...] = jnp.zeros_like(acc_ref)

            _, sig_neg = _log_f_terms(z_ref[...], b_ref[...])
            row = lax.broadcasted_iota(jnp.int32, (TM, TM), 0)
            col = lax.broadcasted_iota(jnp.int32, (TM, TM), 1)
            tri = (col >= row).astype(F32)
            dlog = jnp.dot(tri, dc_ref[...], precision=lax.Precision.HIGHEST, preferred_element_type=F32) + carry[...]
            carry[...] = dlog[0:1, :]
            dff = dlog * sig_neg
            dff_ref[...] = dff
            acc_ref[0:1, :] += jnp.sum(dff, axis=0, keepdims=True)

        return pl.pallas_call(
            body, name="fox_post", grid=(n_tok,),
            in_specs=[pl.BlockSpec((TM, LANES), lambda i: (rev(i), 0)),
                      pl.BlockSpec((TM, LANES), lambda i: (rev(i), misc_blk)), const((1, LANES))],
            out_specs=[pl.BlockSpec((TM, LANES), lambda i: (rev(i), 0)), const((8, LANES))],
            out_shape=[jax.ShapeDtypeStruct((S, LANES), F32), jax.ShapeDtypeStruct((8, LANES), F32)],
            scratch_shapes=[pltpu.VMEM((1, LANES), F32)],
            compiler_params=_params(("arbitrary",)))(dcum, z, bf_pad)

    def mla_post(dq, dk, dv, dff, z, gq, gkv, w_uq, w_uk, w_v, cos_t, sin_t):
        def body(dq_ref, dk_ref, dv_ref, dff_ref, ql_ref, kvl_ref, gq_ref, gkv_ref, wq_ref, wk_ref, wv_ref,
                 c_ref, s_ref, zq_ref, zkv_ref, zms_ref, dwq_ref, dwk_ref, dwv_ref, dgq_ref, dgkv_ref):
            @pl.when(pl.program_id(0) == 0)
            def _():
                for r in (dwq_ref, dwk_ref, dwv_ref, dgq_ref, dgkv_ref):
                    r[...] = jnp.zeros_like(r)

            cos1, sin1 = c_ref[...], s_ref[...]
            gqv, gkvv = gq_ref[...], gkv_ref[...]
            qn, qxh, qrstd = _rms(ql_ref[...], gqv)
            dq_pre = _rope_t(dq_ref[...] * MLA_SCALE, jnp.tile(cos1, (1, MH)), jnp.tile(sin1, (1, MH))).astype(BF16)
            dwq_ref[...] += _tn(qn.astype(BF16), dq_pre)
            dql, dgq = _rms_bwd(_nt(dq_pre, wq_ref[...]), gqv, qxh, qrstd)
            zq_ref[...] = dql.astype(BF16)
            dgq_ref[0:1, :] += dgq

            dkv = dk_ref[...]
            lane = _lane(dkv.shape) & (HEAD_PAD - 1)
            dkn = jnp.where(lane < HEAD_DIM, dkv, 0.0).astype(BF16)
            dkr = dkv[:, 0:HEAD_PAD]
            for hd in range(1, MH):
                dkr = dkr + dkv[:, hd * HEAD_PAD:(hd + 1) * HEAD_PAD]
            lane1 = _lane(dkr.shape)
            dkr = jnp.where((lane1 >= ROPE_LO) & (lane1 < ROPE_LO + ROPE_DIM), dkr, 0.0)
            dkr = _rope_t(dkr, cos1, sin1)
            zms_ref[...] = (dkr + dff_ref[...]).astype(BF16)

            kvn, kxh, krstd = _rms(kvl_ref[...], gkvv)
            kvb = kvn.astype(BF16)
            dvb = dv_ref[...].astype(BF16)
            dwk_ref[...] += _tn(kvb, dkn)
            dwv_ref[...] += _tn(kvb, dvb)
            dkvl, dgkv = _rms_bwd(_nt(dkn, wk_ref[...]) + _nt(dvb, wv_ref[...]), gkvv, kxh, krstd)
            zkv_ref[...] = dkvl.astype(BF16)
            dgkv_ref[0:1, :] += dgkv

        return pl.pallas_call(
            body, name="mla_post", grid=(n_tok,),
            in_specs=[tok(QW), tok(QW), tok(MW), tok(LANES), tok(QL, (cfg.o_ql - ZO) // QL), tok(KVL, (cfg.o_kv - ZO) // KVL),
                      const((1, QL)), const((1, KVL)), const((QL, QW)), const((KVL, QW)), const((KVL, MW)),
                      tok(LANES), tok(LANES)],
            out_specs=[tok(QL), tok(KVL), tok(LANES), const((QL, QW)), const((KVL, QW)), const((KVL, MW)),
                       const((8, QL)), const((8, KVL))],
            out_shape=[jax.ShapeDtypeStruct((S, QL), BF16), jax.ShapeDtypeStruct((S, KVL), BF16),
                       jax.ShapeDtypeStruct((S, LANES), BF16), jax.ShapeDtypeStruct((QL, QW), F32),
                       jax.ShapeDtypeStruct((KVL, QW), F32), jax.ShapeDtypeStruct((KVL, MW), F32),
                       jax.ShapeDtypeStruct((8, QL), F32), jax.ShapeDtypeStruct((8, KVL), F32)],
            compiler_params=_params(("arbitrary",)))(dq, dk, dv, dff, z, z, gq, gkv, w_uq, w_uk, w_v, cos_t, sin_t)

    def bwd_in(dz, w_in, x, dxn, g, scale):
        def body(dz_ref, w_ref, x_ref, dx_ref, g_ref, sc_ref, o_ref, acc_ref):
            @pl.when(pl.program_id(0) == 0)
            def _():
                acc_ref[...] = jnp.zeros_like(acc_ref)

            dh = _nt(dz_ref[...], w_ref[...])
            gv, mod = g_ref[...], 1.0 + sc_ref[...]
            _, xh, rstd = _rms(x_ref[...], gv)
            t = dh * xh
            acc_ref[0:1, :] += jnp.sum(dh, axis=0, keepdims=True)
            acc_ref[1:2, :] += jnp.sum(t * gv, axis=0, keepdims=True)
            acc_ref[2:3, :] += jnp.sum(t * mod, axis=0, keepdims=True)
            dx, _ = _rms_bwd(dh, gv * mod, xh, rstd)
            o_ref[...] = dx_ref[...] + dx

        return pl.pallas_call(
            body, name="bwd_in", grid=(n_tok,),
            in_specs=[tok(NZ), const((D, NZ)), tok(D), tok(D), const((1, D)), const((1, D))],
            out_specs=[tok(D), const((8, D))],
            out_shape=[jax.ShapeDtypeStruct((S, D), F32), jax.ShapeDtypeStruct((8, D), F32)],
            compiler_params=_params(("arbitrary",)))(dz, w_in, x, dxn, g, scale)


    def pair_rows(cum):
        n_pairs = FH // 2
        ct = jnp.pad(cum[:, :FH].T.reshape(n_pairs, 2, S), ((0, 0), (0, 6), (0, 0)))
        return ct.reshape(n_pairs, 8, S // TK, TK).transpose(0, 2, 1, 3)

    def bias_grad(dc, dr):
        n_pairs = FH // 2
        d = dc.transpose(0, 2, 1, 3).reshape(n_pairs, 8, S)[:, :2, :].reshape(FH, S).T
        d = d + dr[:, :, :2].transpose(1, 0, 2).reshape(S, FH)
        return jnp.pad(d, ((0, 0), (0, LANES - FH)))

    def layer_forward(x, wl, mod):
        shift, scale, gate = mod
        h, z, qkv = ln_inproj(x, wl.norm_g, scale, shift, wl.w_in)
        ct = pair_rows(fox_prep(z, wl.bf_pad))
        of, lse_f = attn_fwd(qkv, qkv, qkv, cfg.o_fq // LANES, cfg.o_fk // LANES, cfg.o_fv // LANES, ct,
                             True, False, "fox_fwd")
        qp, kp, vp = mla_prep(z, wl.gq, wl.gkv, wl.w_uq, wl.w_uk, wl.w_v, cfg.cos_t, cfg.sin_t)
        om, lse_m = attn_fwd(qp, kp, vp, 0, 0, 0, None, False, True, "mla_fwd")
        xn, u, y = gate_outproj(of, om, z, wl.w_out, x, gate)
        saved = types.SimpleNamespace(x=x, h=h, z=z, qkv=qkv, ct=ct, of=of, lse_f=lse_f, qp=qp, kp=kp, vp=vp,
                                      om=om, lse_m=lse_m, u=u, y=y)
        return xn, saved

    def layer_backward(dxn, sv, wl, mod):
        shift, scale, gate = mod
        do_f, do_m, dfg, dmg, dy, acc_o = bwd_out(dxn, gate, sv.y, wl.w_out, sv.of, sv.om, sv.z)
        dw_out = matmul_tn(sv.u, dy, "dw_out")
        st_f = attn_stats(do_f, sv.of, sv.lse_f, "fox_stats")
        dfq, dfk, dfv, dck, dcr = attn_bwd(sv.qkv, sv.qkv, sv.qkv, do_f, st_f, cfg.o_fq // LANES, cfg.o_fk // LANES,
                                           cfg.o_fv // LANES, sv.ct, True, False, "fox_bwd")
        dff, acc_f = fox_post(bias_grad(dck, dcr), sv.z, wl.bf_pad)
        st_m = attn_stats(do_m, sv.om, sv.lse_m, "mla_stats")
        dqp, dkp, dvp = attn_bwd(sv.qp, sv.kp, sv.vp, do_m, st_m, 0, 0, 0, None, False, True, "mla_bwd")
        zq, zkv, zms, dw_uq, dw_uk, dw_v, dgq, dgkv = mla_post(
            dqp, dkp, dvp, dff, sv.z, wl.gq, wl.gkv, wl.w_uq, wl.w_uk, wl.w_v, cfg.cos_t, cfg.sin_t)
        dz = jnp.concatenate([(dfq * FOX_SCALE).astype(BF16), dfk.astype(BF16), dfv.astype(BF16), dfg, dmg, zq, zkv, zms], axis=1)
        dx, acc_i = bwd_in(dz, wl.w_in, sv.x, dxn, wl.norm_g, scale)
        dw_in = matmul_tn(sv.h, dz, "dw_in")
        grads = types.SimpleNamespace(
            w_in=dw_in, w_out=dw_out, w_uq=dw_uq, w_uk=dw_uk, w_v=dw_v, gq=dgq[0], gkv=dgkv[0],
            b_f=acc_f[0, :FH], norm_g=acc_i[2], dmod=jnp.concatenate([acc_i[0], acc_i[1], acc_o[0]]))
        return dx, grads

    return types.SimpleNamespace(layer_forward=layer_forward, layer_backward=layer_backward, final_loss=final_loss)


def _pack_rows(parts, dtype, row_multiple):
    flat = jnp.concatenate([p.reshape(-1).astype(dtype) for p in parts])
    per = SLAB_COLS * row_multiple
    total = -(-flat.shape[0] // per) * per
    return jnp.pad(flat, (0, total - flat.shape[0])).reshape(total // SLAB_COLS, SLAB_COLS)


def _unpack(flat, shapes):
    out, off = [], 0
    for shp in shapes:
        n = 1
        for d in shp:
            n *= d
        out.append(flat[off:off + n].reshape(shp))
        off += n
    return out


def kernel(x, c, positions, norm_g, w_ada, b_ada, w_in, b_f, q_norm_g, w_uq, kv_norm_g, w_ukv, w_out, final_g, loss_target, m_norm_g, m_w_ada, m_b_ada, m_w_in, m_b_f, m_q_norm_g, m_w_uq, m_kv_norm_g, m_w_ukv, m_w_out, m_final_g, v_norm_g, v_w_ada, v_b_ada, v_w_in, v_b_f, v_q_norm_g, v_w_uq, v_kv_norm_g, v_w_ukv, v_w_out, v_final_g):
    S, D = x.shape[1], x.shape[2]
    L = norm_g.shape[0]
    FH = b_f.shape[1]
    QL, KVL = q_norm_g.shape[1], kv_norm_g.shape[1]
    MH = w_ukv.shape[2] * N_CHIPS // (2 * HEAD_DIM)
    FW, MW = FH * HEAD_DIM, MH * HEAD_DIM
    NA = w_ada.shape[2]
    n_in = w_in.shape[2] * N_CHIPS
    cfg = types.SimpleNamespace(S=S, D=D, FW=FW, MW=MW, QL=QL, KVL=KVL, FH=FH, MH=MH)
    cfg.o_fq, cfg.o_fk, cfg.o_fv, cfg.o_fg, cfg.o_mg = 0, FW, 2 * FW, 3 * FW, 4 * FW
    cfg.o_ql = 4 * FW + MW
    cfg.o_kv = cfg.o_ql + QL
    cfg.o_ms = cfg.o_kv + KVL
    cfg.NZ = cfg.o_ms + LANES
    assert FW == MW and FH % 2 == 0 and MH % 2 == 0 and cfg.o_ql % QL == 0 and cfg.o_kv % KVL == 0 and KVL == LANES
    assert n_in == 4 * FW + FH + QL + KVL + ROPE_DIM + MW

    mx, my, mc = _my_pos()
    my_chip = 2 * mx + my
    my_dev = 2 * my_chip + mc

    inv_freq = 1.0 / (ROPE_THETA ** (jnp.arange(0, ROPE_DIM, 2, dtype=F32) / ROPE_DIM))
    ang = positions[0].astype(F32)[:, None] * inv_freq
    cos, sin = jnp.cos(ang), jnp.sin(ang)
    cfg.cos_t = jnp.concatenate([jnp.ones((S, ROPE_LO), F32), cos, cos, jnp.ones((S, HEAD_PAD - ROPE_LO - ROPE_DIM), F32)], axis=1)
    cfg.sin_t = jnp.concatenate([jnp.zeros((S, ROPE_LO), F32), -sin, sin, jnp.zeros((S, HEAD_PAD - ROPE_LO - ROPE_DIM), F32)], axis=1)

    shard_shapes = [w_in.shape, w_uq.shape, w_ukv.shape, w_out.shape]
    slab = _pack_rows([w_in, w_uq, w_ukv, w_out], BF16, 32)
    half = slab.shape[0] // 2
    mine = lax.dynamic_slice_in_dim(slab, mc * half, half, axis=0)
    got = chip_exchange(mine, True, "weights_chips")
    other = sibling_swap(got.reshape(N_CHIPS * half, SLAB_COLS), "weights_sibling").reshape(N_CHIPS, half, SLAB_COLS)
    lo = jnp.where(mc == 0, got, other)
    hi = jnp.where(mc == 0, other, got)
    slabs = jnp.concatenate([lo, hi], axis=1).reshape(N_CHIPS, -1)
    per_chip = [_unpack(slabs[t], shard_shapes) for t in range(N_CHIPS)]
    w_in_f = jnp.concatenate([p[0] for p in per_chip], axis=2)
    w_uq_f = jnp.concatenate([p[1] for p in per_chip], axis=2)
    w_ukv_f = jnp.concatenate([p[2] for p in per_chip], axis=2)
    w_out_f = jnp.concatenate([p[3] for p in per_chip], axis=1)

    sizes = (FW, FW, FW, FH, FW, QL, KVL, ROPE_DIM, MW)
    offs = [0]
    for sz in sizes:
        offs.append(offs[-1] + sz)
    seg = [w_in_f[:, :, offs[i]:offs[i + 1]] for i in range(len(sizes))]
    fq_w, fk_w, fv_w, ff_w, fg_w, ql_w, kvl_w, kr_w, mg_w = seg
    zeros = lambda n: jnp.zeros((L, D, n), BF16)
    w_in_p = jnp.concatenate([fq_w, fk_w, fv_w, fg_w, mg_w, ql_w, kvl_w, ff_w, zeros(ROPE_LO - FH), kr_w,
                              zeros(HEAD_PAD - ROPE_LO - ROPE_DIM)], axis=2)
    w_uq_p = jnp.pad(w_uq_f.reshape(L, QL, MH, HEAD_DIM + ROPE_DIM), ((0, 0), (0, 0), (0, 0), (0, HEAD_PAD - HEAD_DIM - ROPE_DIM)))
    w_uq_p = w_uq_p.reshape(L, QL, MH * HEAD_PAD)
    w_ukv4 = w_ukv_f.reshape(L, KVL, MH, 2 * HEAD_DIM)
    w_uk_p = jnp.pad(w_ukv4[..., :HEAD_DIM], ((0, 0), (0, 0), (0, 0), (0, HEAD_PAD - HEAD_DIM))).reshape(L, KVL, MH * HEAD_PAD)
    w_v_p = w_ukv4[..., HEAD_DIM:].reshape(L, KVL, MW)

    c_all = allgather8(c.reshape(8, D // 8), "gather_c").reshape(N_DEV, D)
    c_pad = jnp.pad(c_all, ((0, 16 - N_DEV), (0, 0)))
    mod_part = ada_forward(c_pad, w_ada)[:, :N_DEV, :]
    mod_all = allgather8(mod_part.reshape(-1, LANES), "gather_mod").reshape(N_CHIPS, 2, L, N_DEV, NA)[:, 0]
    mod_full = mod_all.transpose(1, 2, 0, 3).reshape(L, N_DEV, N_CHIPS * NA) + b_ada[:, None, :]
    mod_mine = lax.dynamic_index_in_dim(mod_full, my_dev, axis=1, keepdims=True)

    step = make_step(cfg)
    bf_pad = jnp.pad(b_f, ((0, 0), (0, LANES - FH)))
    layers, mods = [], []
    for l in range(L):
        layers.append(types.SimpleNamespace(
            norm_g=norm_g[l][None], w_in=w_in_p[l], bf_pad=bf_pad[l][None], gq=q_norm_g[l][None], gkv=kv_norm_g[l][None],
            w_uq=w_uq_p[l], w_uk=w_uk_p[l], w_v=w_v_p[l], w_out=w_out_f[l]))
        mods.append((mod_mine[l, :, :D], mod_mine[l, :, D:2 * D], mod_mine[l, :, 2 * D:]))

    xl = x[0]
    saved = []
    for l in range(L):
        xl, sv = step.layer_forward(xl, layers[l], mods[l])
        saved.append(sv)
    dx, acc_fin, loss_part = step.final_loss(xl, final_g[None], loss_target[0])
    loss = lax.psum(loss_part[0, 0], ("x", "y", "c"))
    gl = [None] * L
    for l in reversed(range(L)):
        dx, gl[l] = step.layer_backward(dx, saved[l], layers[l], mods[l])
    grad_x = dx[None]

    stack = lambda name: jnp.stack([getattr(g, name) for g in gl])
    small_parts = [stack("norm_g"), stack("dmod"), stack("b_f"), stack("gq"), stack("gkv"), acc_fin[0]]
    small_shapes = [p.shape for p in small_parts]
    small = _pack_rows(small_parts, F32, 8).reshape(-1, LANES)
    small_all = allgather8(small, "gather_small").reshape(N_DEV, -1, LANES)
    small_sum = sum_leading(small_all, "sum_small")
    g_norm_g, g_b_ada, g_b_f, g_q_norm_g, g_kv_norm_g, g_final_g = _unpack(small_sum.reshape(-1), small_shapes)

    n_ng = L * D
    dmod_all = small_all.reshape(N_DEV, -1)[:, n_ng:n_ng + L * 3 * D].reshape(N_DEV, L, 3 * D)
    dmod_cols = lax.dynamic_slice_in_dim(dmod_all, my_chip * NA, NA, axis=2).transpose(1, 0, 2)
    g_w_ada = ada_backward(c_pad, jnp.pad(dmod_cols, ((0, 0), (0, 16 - N_DEV), (0, 0))))

    dw_in_p = stack("w_in")
    ms = cfg.o_ms
    dw_in_f = jnp.concatenate([
        dw_in_p[:, :, 0:3 * FW], dw_in_p[:, :, ms:ms + FH], dw_in_p[:, :, cfg.o_fg:cfg.o_fg + FW],
        dw_in_p[:, :, cfg.o_ql:cfg.o_ql + QL + KVL], dw_in_p[:, :, ms + ROPE_LO:ms + ROPE_LO + ROPE_DIM],
        dw_in_p[:, :, cfg.o_mg:cfg.o_mg + MW]], axis=2)
    dw_uq_f = stack("w_uq").reshape(L, QL, MH, HEAD_PAD)[..., :HEAD_DIM + ROPE_DIM].reshape(L, QL, -1)
    dw_ukv_f = jnp.concatenate([stack("w_uk").reshape(L, KVL, MH, HEAD_PAD)[..., :HEAD_DIM],
                                stack("w_v").reshape(L, KVL, MH, HEAD_DIM)], axis=3).reshape(L, KVL, -1)
    dw_out_f = stack("w_out")
    dest = []
    for t in range(N_CHIPS):
        dest.append(_pack_rows([
            dw_in_f[:, :, t * w_in.shape[2]:(t + 1) * w_in.shape[2]], dw_uq_f[:, :, t * w_uq.shape[2]:(t + 1) * w_uq.shape[2]],
            dw_ukv_f[:, :, t * w_ukv.shape[2]:(t + 1) * w_ukv.shape[2]], dw_out_f[:, t * w_out.shape[1]:(t + 1) * w_out.shape[1], :]],
            F32, 32))
    gs = jnp.stack(dest)
    ghalf = gs.shape[1] // 2
    gs = gs.reshape(N_CHIPS, 2, ghalf, SLAB_COLS)
    keep = lax.dynamic_index_in_dim(gs, mc, axis=1, keepdims=False).reshape(N_CHIPS * ghalf, SLAB_COLS)
    give = lax.dynamic_index_in_dim(gs, 1 - mc, axis=1, keepdims=False).reshape(N_CHIPS * ghalf, SLAB_COLS)
    sib = sibling_swap(give, "grads_sibling")
    chip_part = add_cast(keep, sib, BF16, "grads_chip_sum").reshape(N_CHIPS, ghalf, SLAB_COLS)
    parts = chip_exchange(chip_part, False, "grads_chips")
    red_half = sum_chips(parts, "grads_sum")
    red_other = sibling_swap(red_half, "grads_back")
    red = jnp.concatenate([jnp.where(mc == 0, red_half, red_other), jnp.where(mc == 0, red_other, red_half)], axis=0)
    g_w_in, g_w_uq, g_w_ukv, g_w_out = _unpack(red.reshape(-1), shard_shapes)

    def big(w, g, m, v, name):
        shp = w.shape
        two = lambda a: a.reshape(-1, shp[-1])
        return [o.reshape(shp) for o in adamw(two(w), two(g), two(m), two(v), name)]

    names = ["norm_g", "w_ada", "b_ada", "w_in", "b_f", "q_norm_g", "w_uq", "kv_norm_g", "w_ukv", "w_out", "final_g"]
    ws = dict(norm_g=norm_g, w_ada=w_ada, b_ada=b_ada, w_in=w_in, b_f=b_f, q_norm_g=q_norm_g, w_uq=w_uq,
              kv_norm_g=kv_norm_g, w_ukv=w_ukv, w_out=w_out, final_g=final_g)
    msd = dict(norm_g=m_norm_g, w_ada=m_w_ada, b_ada=m_b_ada, w_in=m_w_in, b_f=m_b_f, q_norm_g=m_q_norm_g, w_uq=m_w_uq,
               kv_norm_g=m_kv_norm_g, w_ukv=m_w_ukv, w_out=m_w_out, final_g=m_final_g)
    vsd = dict(norm_g=v_norm_g, w_ada=v_w_ada, b_ada=v_b_ada, w_in=v_w_in, b_f=v_b_f, q_norm_g=v_q_norm_g, w_uq=v_w_uq,
               kv_norm_g=v_kv_norm_g, w_ukv=v_w_ukv, w_out=v_w_out, final_g=v_final_g)
    gsd = dict(norm_g=g_norm_g, w_ada=g_w_ada, b_ada=g_b_ada, w_in=g_w_in, b_f=g_b_f, q_norm_g=g_q_norm_g, w_uq=g_w_uq,
               kv_norm_g=g_kv_norm_g, w_ukv=g_w_ukv, w_out=g_w_out, final_g=g_final_g)
    small_names = ["norm_g", "b_ada", "b_f", "q_norm_g", "kv_norm_g", "final_g"]
    sm_shapes = [ws[n].shape for n in small_names]
    pk = lambda d: _pack_rows([d[n] for n in small_names], F32, 8).reshape(-1, LANES)
    sm_out = adamw(pk(ws), pk(gsd), pk(msd), pk(vsd), "adamw_small")
    sm_d, sm_m, sm_v = [dict(zip(small_names, _unpack(o.reshape(-1), sm_shapes))) for o in sm_out]
    delta, new_m, new_v = dict(sm_d), dict(sm_m), dict(sm_v)
    for n in ["w_ada", "w_in", "w_uq", "w_ukv", "w_out"]:
        delta[n], new_m[n], new_v[n] = big(ws[n], gsd[n], msd[n], vsd[n], "adamw_" + n)

    return (loss, grad_x, *[gsd[n] for n in names], *[delta[n] for n in names],
            *[new_m[n] for n in names], *[new_v[n] for n in names])
```

```python
import functools
import types

import jax
import jax.numpy as jnp
from jax import lax
from jax.experimental import pallas as pl
from jax.experimental.pallas import tpu as pltpu

F32 = jnp.float32
BF16 = jnp.bfloat16
MESH = pl.DeviceIdType.MESH

N_CHIPS = 4
N_DEV = 8
HEAD_DIM = 64
ROPE_DIM = 32
ROPE_THETA = 10000.0
HEAD_PAD = 128
ROPE_LO = 64
ROPE_HALF = 16
LANES = 128
EPS = 1e-6
NEG = -1e30
ADAM_LR = 0.001
ADAM_B1 = 0.9
ADAM_B2 = 0.999
ADAM_EPS = 1e-08
ADAM_WD = 0.01
ADAM_STEP = 10
VMEM_LIMIT = 48 * 1024 * 1024
SLAB_COLS = 1024


def _params(sem=None, vmem=VMEM_LIMIT):
    return pltpu.CompilerParams(dimension_semantics=sem, vmem_limit_bytes=vmem)


def _nn(a, b):
    return jnp.dot(a, b, preferred_element_type=F32)


def _nt(a, b):
    return lax.dot_general(a, b, (((1,), (1,)), ((), ())), preferred_element_type=F32)


def _tn(a, b):
    return lax.dot_general(a, b, (((0,), (0,)), ((), ())), preferred_element_type=F32)


def _sigmoid(x):
    return 1.0 / (1.0 + jnp.exp(-x))


def _lane(shape):
    return lax.broadcasted_iota(jnp.int32, shape, len(shape) - 1)


def _pick(n, cands):
    for c in cands:
        if n % c == 0:
            return c
    return n


def _my_pos():
    return lax.axis_index("x"), lax.axis_index("y"), lax.axis_index("c")


def allgather8(xs, name):
    m_per, n = xs.shape

    def body(x_ref, out_ref, send_sems, recv_sems, local_sem):
        x, y, c = _my_pos()
        me, sibling = (x, y, c), (x, y, 1 - c)
        chips = [(1 - x, y), (x, 1 - y), (1 - x, 1 - y)]

        def rows(px, py, pc):
            return out_ref.at[pl.ds((4 * px + 2 * py + pc) * m_per, m_per), :]

        def copy(k, block, to, src=None):
            return pltpu.make_async_remote_copy(
                src_ref=rows(*block) if src is None else src, dst_ref=rows(*block),
                send_sem=send_sems.at[k], recv_sem=recv_sems.at[k], device_id=to, device_id_type=MESH)

        mine = pltpu.make_async_copy(x_ref, rows(*me), local_sem)
        mine.start()
        first = [copy(0, me, sibling, src=x_ref)]
        first += [copy(1 + j, me, (*chip, c), src=x_ref) for j, chip in enumerate(chips)]
        for cp in first:
            cp.start()
        passed = [copy(4 + j, (*chip, c), sibling) for j, chip in enumerate(chips)]
        for j, chip in enumerate(chips):
            copy(1 + j, (*chip, c), me).wait_recv()
            passed[j].start()
        copy(0, sibling, me).wait_recv()
        for j, chip in enumerate(chips):
            copy(4 + j, (*chip, 1 - c), me).wait_recv()
        for cp in first + passed:
            cp.wait_send()
        mine.wait()

    return pl.pallas_call(
        body, name=name,
        out_shape=jax.ShapeDtypeStruct((N_DEV * m_per, n), xs.dtype),
        in_specs=[pl.BlockSpec(memory_space=pltpu.VMEM)],
        out_specs=pl.BlockSpec(memory_space=pltpu.VMEM),
        scratch_shapes=[pltpu.SemaphoreType.DMA((7,)), pltpu.SemaphoreType.DMA((7,)), pltpu.SemaphoreType.DMA],
    )(xs)


def chip_exchange(xs, bcast, name):
    r, n = xs.shape[-2:]

    def body(x_ref, out_ref, send_sems, recv_sems, local_sem):
        x, y, c = _my_pos()
        s = 2 * x + y
        chips = [(1 - x, y), (x, 1 - y), (1 - x, 1 - y)]

        def src(t):
            return x_ref if bcast else x_ref.at[t]

        local = pltpu.make_async_copy(src(s), out_ref.at[s], local_sem)
        local.start()
        sends = []
        for k, (tx, ty) in enumerate(chips):
            cp = pltpu.make_async_remote_copy(
                src_ref=src(2 * tx + ty), dst_ref=out_ref.at[s], send_sem=send_sems.at[k],
                recv_sem=recv_sems.at[k], device_id=(tx, ty, c), device_id_type=MESH)
            cp.start()
            sends.append(cp)
        for k, (tx, ty) in enumerate(chips):
            t = 2 * tx + ty
            pltpu.make_async_remote_copy(
                src_ref=src(t), dst_ref=out_ref.at[t], send_sem=send_sems.at[k],
                recv_sem=recv_sems.at[k], device_id=(tx, ty, c), device_id_type=MESH).wait_recv()
        for cp in sends:
            cp.wait_send()
        local.wait()

    return pl.pallas_call(
        body, name=name,
        out_shape=jax.ShapeDtypeStruct((N_CHIPS, r, n), xs.dtype),
        in_specs=[pl.BlockSpec(memory_space=pl.ANY)],
        out_specs=pl.BlockSpec(memory_space=pl.ANY),
        scratch_shapes=[pltpu.SemaphoreType.DMA((3,)), pltpu.SemaphoreType.DMA((3,)), pltpu.SemaphoreType.DMA],
    )(xs)


def sibling_swap(xs, name):
    def body(x_ref, out_ref, send_sem, recv_sem):
        x, y, c = _my_pos()
        cp = pltpu.make_async_remote_copy(
            src_ref=x_ref, dst_ref=out_ref, send_sem=send_sem, recv_sem=recv_sem,
            device_id=(x, y, 1 - c), device_id_type=MESH)
        cp.start()
        cp.wait()

    return pl.pallas_call(
        body, name=name,
        out_shape=jax.ShapeDtypeStruct(xs.shape, xs.dtype),
        in_specs=[pl.BlockSpec(memory_space=pl.ANY)],
        out_specs=pl.BlockSpec(memory_space=pl.ANY),
        scratch_shapes=[pltpu.SemaphoreType.DMA, pltpu.SemaphoreType.DMA],
    )(xs)


def sum_leading(xs, name):
    n, r, c = xs.shape

    def body(x_ref, o_ref):
        acc = x_ref[0]
        for i in range(1, n):
            acc = acc + x_ref[i]
        o_ref[...] = acc

    return pl.pallas_call(body, name=name, out_shape=jax.ShapeDtypeStruct((r, c), xs.dtype))(xs)


def add_cast(a, b, out_dtype, name):
    r, c = a.shape
    tr = _pick(r, (512, 256, 128, 64, 32, 16))

    def body(a_ref, b_ref, o_ref):
        o_ref[...] = (a_ref[...] + b_ref[...]).astype(out_dtype)

    spec = pl.BlockSpec((tr, c), lambda i: (i, 0))
    return pl.pallas_call(body, name=name, grid=(r // tr,), in_specs=[spec, spec], out_specs=spec,
                          out_shape=jax.ShapeDtypeStruct((r, c), out_dtype), compiler_params=_params(("parallel",)))(a, b)


def sum_chips(xs, name):
    n, r, c = xs.shape
    tr = _pick(r, (512, 256, 128, 64, 32, 16))

    def body(x_ref, o_ref):
        acc = x_ref[0].astype(F32)
        for i in range(1, n):
            acc = acc + x_ref[i].astype(F32)
        o_ref[...] = acc

    return pl.pallas_call(body, name=name, grid=(r // tr,),
                          in_specs=[pl.BlockSpec((n, tr, c), lambda i: (0, i, 0))],
                          out_specs=pl.BlockSpec((tr, c), lambda i: (i, 0)),
                          out_shape=jax.ShapeDtypeStruct((r, c), F32), compiler_params=_params(("parallel",)))(xs)


def ada_forward(c_all, w_ada):
    nl, d, n = w_ada.shape
    nb = c_all.shape[0]

    def body(c_ref, w_ref, o_ref):
        cv = c_ref[...]
        ca = (cv * _sigmoid(cv)).astype(BF16)
        o_ref[0] = _nn(ca, w_ref[0].astype(BF16))

    return pl.pallas_call(
        body, name="ada_forward", grid=(nl,),
        in_specs=[pl.BlockSpec((nb, d), lambda l: (0, 0)), pl.BlockSpec((1, d, n), lambda l: (l, 0, 0))],
        out_specs=pl.BlockSpec((1, nb, n), lambda l: (l, 0, 0)),
        out_shape=jax.ShapeDtypeStruct((nl, nb, n), F32), compiler_params=_params(("parallel",)))(c_all, w_ada)


def ada_backward(c_all, dmod):
    nl, nb, n = dmod.shape
    d = c_all.shape[1]

    def body(c_ref, g_ref, o_ref):
        cv = c_ref[...]
        ca = (cv * _sigmoid(cv)).astype(BF16)
        o_ref[0] = _tn(ca, g_ref[0].astype(BF16))

    return pl.pallas_call(
        body, name="ada_backward", grid=(nl,),
        in_specs=[pl.BlockSpec((nb, d), lambda l: (0, 0)), pl.BlockSpec((1, nb, n), lambda l: (l, 0, 0))],
        out_specs=pl.BlockSpec((1, d, n), lambda l: (l, 0, 0)),
        out_shape=jax.ShapeDtypeStruct((nl, d, n), F32), compiler_params=_params(("parallel",)))(c_all, dmod)


def matmul_tn(a, b, name):
    k, m = a.shape
    n = b.shape[1]
    tm, tn, tk = _pick(m, (512, 256, 128)), _pick(n, (512, 256, 128)), _pick(k, (512, 256, 128))

    def body(a_ref, b_ref, o_ref):
        @pl.when(pl.program_id(2) == 0)
        def _():
            o_ref[...] = jnp.zeros_like(o_ref)

        o_ref[...] += _tn(a_ref[...], b_ref[...])

    return pl.pallas_call(
        body, name=name, grid=(m // tm, n // tn, k // tk),
        in_specs=[pl.BlockSpec((tk, tm), lambda i, j, kk: (kk, i)), pl.BlockSpec((tk, tn), lambda i, j, kk: (kk, j))],
        out_specs=pl.BlockSpec((tm, tn), lambda i, j, kk: (i, j)),
        out_shape=jax.ShapeDtypeStruct((m, n), F32),
        compiler_params=_params(("parallel", "parallel", "arbitrary")))(a, b)


def adamw(w, g, m, v, name):
    r, c = w.shape
    tr = _pick(r, (512, 256, 128, 64, 32, 16, 8))

    def body(w_ref, g_ref, m_ref, v_ref, d_ref, mo_ref, vo_ref):
        gv = g_ref[...]
        mn = ADAM_B1 * m_ref[...] + (1.0 - ADAM_B1) * gv
        vn = ADAM_B2 * v_ref[...] + (1.0 - ADAM_B2) * (gv * gv)
        m_hat = mn / (1.0 - ADAM_B1 ** ADAM_STEP)
        v_hat = vn / (1.0 - ADAM_B2 ** ADAM_STEP)
        d_ref[...] = -ADAM_LR * (m_hat / (jnp.sqrt(v_hat) + ADAM_EPS) + ADAM_WD * w_ref[...])
        mo_ref[...] = mn
        vo_ref[...] = vn

    spec = pl.BlockSpec((tr, c), lambda i: (i, 0))
    out = jax.ShapeDtypeStruct((r, c), F32)
    return pl.pallas_call(body, name=name, grid=(r // tr,), in_specs=[spec] * 4, out_specs=[spec] * 3,
                          out_shape=[out] * 3, compiler_params=_params(("parallel",)))(w, g, m, v)


def _rope(t, cos_t, sin_t):
    w = t.shape[1]
    lane = _lane(t.shape) & (HEAD_PAD - 1)
    first_half = (lane >= ROPE_LO) & (lane < ROPE_LO + ROPE_HALF)
    partner = jnp.where(first_half, pltpu.roll(t, w - ROPE_HALF, 1), pltpu.roll(t, ROPE_HALF, 1))
    return t * cos_t + partner * sin_t


def _rope_t(dt, cos_t, sin_t):
    w = dt.shape[1]
    lane = _lane(dt.shape) & (HEAD_PAD - 1)
    first_half = (lane >= ROPE_LO) & (lane < ROPE_LO + ROPE_HALF)
    ds = dt * sin_t
    partner = jnp.where(first_half, pltpu.roll(ds, w - ROPE_HALF, 1), pltpu.roll(ds, ROPE_HALF, 1))
    return dt * cos_t + partner


def _rms(xv, g):
    rstd = lax.rsqrt(jnp.mean(xv * xv, axis=-1, keepdims=True) + EPS)
    xh = xv * rstd
    return xh * g, xh, rstd


def _rms_bwd(dy, g, xh, rstd):
    dxh = dy * g
    dx = rstd * (dxh - xh * jnp.mean(dxh * xh, axis=-1, keepdims=True))
    return dx, jnp.sum(dy * xh, axis=0, keepdims=True)


def make_step(cfg):
    S, D, NZ = cfg.S, cfg.D, cfg.NZ
    FW, MW, QL, KVL, FH, MH = cfg.FW, cfg.MW, cfg.QL, cfg.KVL, cfg.FH, cfg.MH
    QW = MH * HEAD_PAD
    TM = _pick(S, (256, 128))
    TQ = TK = _pick(S, (256, 128))
    n_tok = S // TM
    ZO = 3 * FW
    NZR = NZ - ZO
    misc_blk = (cfg.o_ms - ZO) // LANES
    FOX_SCALE = HEAD_DIM ** -0.5
    MLA_SCALE = (HEAD_DIM + ROPE_DIM) ** -0.5

    def tok(width, col=0):
        return pl.BlockSpec((TM, width), lambda i: (i, col))

    def const(shape):
        return pl.BlockSpec(shape, lambda i: tuple(0 for _ in shape))


    def ln_inproj(x, g, scale, shift, w_in):
        def body(x_ref, g_ref, sc_ref, sh_ref, w_ref, h_ref, z_ref, qkv_ref):
            y, _, _ = _rms(x_ref[...], g_ref[...])
            hb = (y * (1.0 + sc_ref[...]) + sh_ref[...]).astype(BF16)
            h_ref[...] = hb
            z = _nn(hb, w_ref[...])
            z_ref[...] = z[:, ZO:]
            qkv_ref[:, :FW] = (z[:, :FW] * FOX_SCALE).astype(BF16)
            qkv_ref[:, FW:] = z[:, FW:ZO].astype(BF16)

        return pl.pallas_call(
            body, name="ln_inproj", grid=(n_tok,),
            in_specs=[tok(D), const((1, D)), const((1, D)), const((1, D)), const((D, NZ))],
            out_specs=[tok(D), tok(NZR), tok(ZO)],
            out_shape=[jax.ShapeDtypeStruct((S, D), BF16), jax.ShapeDtypeStruct((S, NZR), F32),
                       jax.ShapeDtypeStruct((S, ZO), BF16)],
            compiler_params=_params(("parallel",)))(x, g, scale, shift, w_in)

    def _log_f_terms(misc, bf):
        lane = _lane(misc.shape)
        a = misc + bf
        e = jnp.exp(-jnp.abs(a))
        logf = jnp.minimum(a, 0.0) - jnp.log(1.0 + e)
        sig_neg = jnp.where(a >= 0, e, 1.0) / (1.0 + e)
        valid = lane < FH
        return jnp.where(valid, logf, 0.0), jnp.where(valid, sig_neg, 0.0)

    def fox_prep(z, bf_pad):
        def body(z_ref, b_ref, o_ref, carry):
            @pl.when(pl.program_id(0) == 0)
            def _():
                carry[...] = jnp.zeros_like(carry)

            logf, _ = _log_f_terms(z_ref[...], b_ref[...])
            row = lax.broadcasted_iota(jnp.int32, (TM, TM), 0)
            col = lax.broadcasted_iota(jnp.int32, (TM, TM), 1)
            tri = (col <= row).astype(F32)
            cum = jnp.dot(tri, logf, precision=lax.Precision.HIGHEST, preferred_element_type=F32) + carry[...]
            o_ref[...] = cum
            carry[...] = cum[TM - 1:TM, :]

        return pl.pallas_call(
            body, name="fox_prep", grid=(n_tok,),
            in_specs=[tok(LANES, misc_blk), const((1, LANES))], out_specs=tok(LANES),
            out_shape=jax.ShapeDtypeStruct((S, LANES), F32),
            scratch_shapes=[pltpu.VMEM((1, LANES), F32)],
            compiler_params=_params(("arbitrary",)))(z, bf_pad)

    def mla_prep(z, gq, gkv, w_uq, w_uk, w_v, cos_t, sin_t):
        def body(ql_ref, kvl_ref, ms_ref, gq_ref, gkv_ref, wq_ref, wk_ref, wv_ref, c_ref, s_ref, q_ref, k_ref, v_ref):
            cos1, sin1 = c_ref[...], s_ref[...]
            qn, _, _ = _rms(ql_ref[...], gq_ref[...])
            q = _nn(qn.astype(BF16), wq_ref[...])
            q_ref[...] = (_rope(q, jnp.tile(cos1, (1, MH)), jnp.tile(sin1, (1, MH))) * MLA_SCALE).astype(BF16)
            kvn, _, _ = _rms(kvl_ref[...], gkv_ref[...])
            kvb = kvn.astype(BF16)
            lane = _lane((TM, LANES))
            kr = jnp.where((lane >= ROPE_LO) & (lane < ROPE_LO + ROPE_DIM), ms_ref[...], 0.0)
            kr = _rope(kr, cos1, sin1)
            k_ref[...] = (_nn(kvb, wk_ref[...]) + jnp.tile(kr, (1, MH))).astype(BF16)
            v_ref[...] = _nn(kvb, wv_ref[...]).astype(BF16)

        return pl.pallas_call(
            body, name="mla_prep", grid=(n_tok,),
            in_specs=[tok(QL, (cfg.o_ql - ZO) // QL), tok(KVL, (cfg.o_kv - ZO) // KVL), tok(LANES, misc_blk),
                      const((1, QL)), const((1, KVL)), const((QL, QW)), const((KVL, QW)), const((KVL, MW)),
                      tok(LANES), tok(LANES)],
            out_specs=[tok(QW), tok(QW), tok(MW)],
            out_shape=[jax.ShapeDtypeStruct((S, QW), BF16), jax.ShapeDtypeStruct((S, QW), BF16),
                       jax.ShapeDtypeStruct((S, MW), BF16)],
            compiler_params=_params(("parallel",)))(z, z, z, gq, gkv, w_uq, w_uk, w_v, cos_t, sin_t)

    def _allowed(q0, k0, chunked):
        qi = q0 + lax.broadcasted_iota(jnp.int32, (TQ, TK), 0)
        ki = k0 + lax.broadcasted_iota(jnp.int32, (TQ, TK), 1)
        if chunked:
            return (ki >> 6) <= (qi >> 6)
        return ki <= qi

    def _heads(val, packed):
        if packed:
            lane = _lane(val.shape)
            zero = jnp.zeros_like(val)
            return [jnp.where(lane < HEAD_DIM, val, zero), jnp.where(lane >= HEAD_DIM, val, zero)]
        return [val[:, :HEAD_PAD], val[:, HEAD_PAD:]]

    def _merge(a0, a1):
        return jnp.where(_lane(a0.shape) < HEAD_DIM, a0, a1)

    def attn_fwd(q, k, v, q_blk0, k_blk0, v_blk0, cumt, packed, chunked, name):
        has_bias = cumt is not None
        n_pairs = (FH if packed else MH) // 2
        wq = LANES if packed else 2 * HEAD_PAD
        assert TQ == TK

        def body(*refs):
            if has_bias:
                q_ref, k_ref, v_ref, ct_ref, o_ref, lse_ref = refs
            else:
                q_ref, k_ref, v_ref, o_ref, lse_ref = refs
            i = pl.program_id(1)
            q0 = i * TQ
            qh = _heads(q_ref[...], packed)
            q_both = jnp.concatenate(qh, axis=0) if packed else None

            def scores(kb):
                kblk = k_ref[pl.ds(pl.multiple_of(kb * TK, TK), TK), :]
                if packed:
                    return _nt(q_both, kblk)
                return jnp.concatenate([_nt(qh[0], kblk[:, :HEAD_PAD]), _nt(qh[1], kblk[:, HEAD_PAD:])], axis=0)

            def update(kb, s, carry, masked):
                m, l, acc = carry
                k0 = pl.multiple_of(kb * TK, TK)
                s0, s1 = s[:TQ], s[TQ:]
                if has_bias:
                    ck = ct_ref[0, kb]
                    s0, s1 = s0 - ck[0:1, :], s1 - ck[1:2, :]
                if masked:
                    allow = _allowed(q0, k0, chunked)
                    s0, s1 = jnp.where(allow, s0, NEG), jnp.where(allow, s1, NEG)
                s = jnp.concatenate([s0, s1], axis=0)
                m_new = jnp.maximum(m, jnp.max(s, axis=-1, keepdims=True))
                p = jnp.exp(s - m_new)
                alpha = jnp.exp(m - m_new)
                l = alpha * l + jnp.sum(p, axis=-1, keepdims=True)
                acc = alpha * acc + _nn(p.astype(BF16), v_ref[pl.ds(k0, TK), :])
                return m_new, l, acc

            def two(kb, carry, last_masked):
                s_a, s_b = scores(kb), scores(kb + 1)
                return update(kb + 1, s_b, update(kb, s_a, carry, False), last_masked)

            init = (jnp.full((2 * TQ, 1), NEG, F32), jnp.zeros((2 * TQ, 1), F32), jnp.zeros((2 * TQ, LANES), F32))
            carry = lax.fori_loop(0, i // 2, lambda t, cr: two(2 * t, cr, False), init)
            m, l, acc = lax.cond(i % 2 == 1, lambda cr: two(i - 1, cr, True),
                                 lambda cr: update(i, scores(i), cr, True), carry)
            o = acc / l
            lse = m + jnp.log(l)
            lane = _lane((TQ, LANES))
            o_ref[...] = _merge(o[:TQ], o[TQ:])
            lse_ref[0] = jnp.where(lane == 0, lse[:TQ], jnp.where(lane == 1, lse[TQ:], 0.0))

        in_specs = [pl.BlockSpec((TQ, wq), lambda p, i: (i, q_blk0 + p)),
                    pl.BlockSpec((S, wq), lambda p, i: (0, k_blk0 + p)),
                    pl.BlockSpec((S, LANES), lambda p, i: (0, v_blk0 + p))]
        args = [q, k, v]
        if has_bias:
            in_specs += [pl.BlockSpec((1, S // TK, 8, TK), lambda p, i: (p, 0, 0, 0))]
            args += [cumt]
        return pl.pallas_call(
            body, name=name, grid=(n_pairs, S // TQ), in_specs=in_specs,
            out_specs=[pl.BlockSpec((TQ, LANES), lambda p, i: (i, p)),
                       pl.BlockSpec((1, TQ, LANES), lambda p, i: (p, i, 0))],
            out_shape=[jax.ShapeDtypeStruct((S, n_pairs * LANES), F32),
                       jax.ShapeDtypeStruct((n_pairs, S, LANES), F32)],
            compiler_params=_params(("parallel", "parallel")))(*args)

    def gate_outproj(of, om, z, w_out, x, gate):
        def body(of_ref, om_ref, fg_ref, mg_ref, w_ref, x_ref, gt_ref, xn_ref, u_ref, y_ref):
            fg, mg = fg_ref[...], mg_ref[...]
            u = jnp.concatenate([of_ref[...] * fg * _sigmoid(fg), om_ref[...] * mg * _sigmoid(mg)], axis=1).astype(BF16)
            y = _nn(u, w_ref[...])
            u_ref[...] = u
            y_ref[...] = y.astype(BF16)
            xn_ref[...] = x_ref[...] + gt_ref[...] * y

        return pl.pallas_call(
            body, name="gate_outproj", grid=(n_tok,),
            in_specs=[tok(FW), tok(MW), tok(FW, (cfg.o_fg - ZO) // FW), tok(MW, (cfg.o_mg - ZO) // MW),
                      const((FW + MW, D)), tok(D), const((1, D))],
            out_specs=[tok(D), tok(FW + MW), tok(D)],
            out_shape=[jax.ShapeDtypeStruct((S, D), F32), jax.ShapeDtypeStruct((S, FW + MW), BF16),
                       jax.ShapeDtypeStruct((S, D), BF16)],
            compiler_params=_params(("parallel",)))(of, om, z, z, w_out, x, gate)

    def final_loss(x, g, target):
        def body(x_ref, g_ref, t_ref, dx_ref, acc_ref, loss_ref):
            @pl.when(pl.program_id(0) == 0)
            def _():
                acc_ref[...] = jnp.zeros_like(acc_ref)
                loss_ref[...] = jnp.zeros_like(loss_ref)

            gv = g_ref[...]
            y, xh, rstd = _rms(x_ref[...], gv)
            e = y - t_ref[...]
            loss_ref[...] += 0.5 * jnp.sum(jnp.sum(e * e, axis=-1, keepdims=True) / D, axis=0, keepdims=True)
            dx, dg = _rms_bwd(e / D, gv, xh, rstd)
            dx_ref[...] = dx
            acc_ref[0:1, :] += dg

        return pl.pallas_call(
            body, name="final_loss", grid=(n_tok,),
            in_specs=[tok(D), const((1, D)), tok(D)],
            out_specs=[tok(D), const((8, D)), const((1, LANES))],
            out_shape=[jax.ShapeDtypeStruct((S, D), F32), jax.ShapeDtypeStruct((8, D), F32),
                       jax.ShapeDtypeStruct((1, LANES), F32)],
            compiler_params=_params(("arbitrary",)))(x, g, target)


    def bwd_out(dxn, gate, y, w_out, of, om, z):
        def body(dx_ref, gt_ref, y_ref, w_ref, of_ref, om_ref, fg_ref, mg_ref,
                 dof_ref, dom_ref, dfg_ref, dmg_ref, dy_ref, acc_ref):
            @pl.when(pl.program_id(0) == 0)
            def _():
                acc_ref[...] = jnp.zeros_like(acc_ref)

            dxv = dx_ref[...]
            acc_ref[0:1, :] += jnp.sum(dxv * y_ref[...].astype(F32), axis=0, keepdims=True)
            dy = (gt_ref[...] * dxv).astype(BF16)
            dy_ref[...] = dy
            du = _nt(dy, w_ref[...])
            for lo, width, o_ref, g_ref, do_ref, dg_ref in ((0, FW, of_ref, fg_ref, dof_ref, dfg_ref),
                                                            (FW, MW, om_ref, mg_ref, dom_ref, dmg_ref)):
                gv = g_ref[...]
                sg = _sigmoid(gv)
                dup = du[:, lo:lo + width]
                do_ref[...] = (dup * gv * sg).astype(BF16)
                dg_ref[...] = (dup * o_ref[...] * sg * (1.0 + gv * (1.0 - sg))).astype(BF16)

        return pl.pallas_call(
            body, name="bwd_out", grid=(n_tok,),
            in_specs=[tok(D), const((1, D)), tok(D), const((FW + MW, D)), tok(FW), tok(MW),
                      tok(FW, (cfg.o_fg - ZO) // FW), tok(MW, (cfg.o_mg - ZO) // MW)],
            out_specs=[tok(FW), tok(MW), tok(FW), tok(MW), tok(D), const((8, D))],
            out_shape=[jax.ShapeDtypeStruct((S, FW), BF16), jax.ShapeDtypeStruct((S, MW), BF16),
                       jax.ShapeDtypeStruct((S, FW), BF16), jax.ShapeDtypeStruct((S, MW), BF16),
                       jax.ShapeDtypeStruct((S, D), BF16), jax.ShapeDtypeStruct((8, D), F32)],
            compiler_params=_params(("arbitrary",)))(dxn, gate, y, w_out, of, om, z, z)

    def attn_stats(do, o, lse, name):
        n_pairs = lse.shape[0]

        def body(do_ref, o_ref, lse_ref, st_ref):
            d = do_ref[...].astype(F32) * o_ref[...]
            lane = _lane(d.shape)
            d0 = jnp.sum(jnp.where(lane < HEAD_DIM, d, 0.0), axis=-1, keepdims=True)
            d1 = jnp.sum(jnp.where(lane >= HEAD_DIM, d, 0.0), axis=-1, keepdims=True)
            st_ref[0] = jnp.where(lane == 2, d0, jnp.where(lane == 3, d1, lse_ref[0]))

        blk = pl.BlockSpec((TM, LANES), lambda p, i: (i, p))
        sblk = pl.BlockSpec((1, TM, LANES), lambda p, i: (p, i, 0))
        return pl.pallas_call(
            body, name=name, grid=(n_pairs, n_tok), in_specs=[blk, blk, sblk], out_specs=sblk,
            out_shape=jax.ShapeDtypeStruct(lse.shape, F32),
            compiler_params=_params(("parallel", "parallel")))(do, o, lse)

    def attn_bwd(q, k, v, do, stats, q_blk0, k_blk0, v_blk0, cumt, packed, chunked, name):
        has_bias = cumt is not None
        n_pairs = (FH if packed else MH) // 2
        wq = LANES if packed else 2 * HEAD_PAD
        n_q = S // TQ
        assert TQ == TK

        def body(*refs):
            if has_bias:
                q_ref, k_ref, v_ref, do_ref, st_ref, ct_ref, dq_ref, dk_ref, dv_ref, dc_ref, dr_ref = refs
            else:
                q_ref, k_ref, v_ref, do_ref, st_ref, dq_ref, dk_ref, dv_ref = refs
            jb = pl.program_id(1)
            k0 = jb * TK

            @pl.when(jb == 0)
            def _():
                dq_ref[...] = jnp.zeros_like(dq_ref)
                if has_bias:
                    dr_ref[...] = jnp.zeros_like(dr_ref)

            dk_ref[...] = jnp.zeros_like(dk_ref)
            dv_ref[...] = jnp.zeros_like(dv_ref)
            kh = _heads(k_ref[...], packed)
            k_both = jnp.concatenate(kh, axis=0) if packed else None
            v_both = jnp.concatenate(_heads(v_ref[...], True), axis=0)
            ck = jnp.concatenate([ct_ref[0, 0][0:1, :], ct_ref[0, 0][1:2, :]], axis=1) if has_bias else None

            def products(ib):
                rows = pl.ds(pl.multiple_of(ib * TQ, TQ), TQ)
                q2, do2 = q_ref[rows, :], do_ref[rows, :]
                if packed:
                    s = _nt(q2, k_both)
                else:
                    qh = _heads(q2, False)
                    s = jnp.concatenate([_nt(qh[0], kh[0]), _nt(qh[1], kh[1])], axis=1)
                return s, _nt(do2, v_both)

            def update(ib, s, dp, carry, masked):
                q0 = pl.multiple_of(ib * TQ, TQ)
                rows = pl.ds(q0, TQ)
                q2, do2, st = q_ref[rows, :], do_ref[rows, :], st_ref[0, rows, :]
                if not packed:
                    qh = _heads(q2, False)
                if has_bias:
                    s = s - ck
                if masked:
                    allow = _allowed(q0, k0, chunked)
                    s = jnp.where(jnp.concatenate([allow, allow], axis=1), s, NEG)
                p = jnp.concatenate([jnp.exp(s[:, :TK] - st[:, 0:1]), jnp.exp(s[:, TK:] - st[:, 1:2])], axis=1)
                dv2 = _tn(p.astype(BF16), do2)
                ds = jnp.concatenate([p[:, :TK] * (dp[:, :TK] - st[:, 2:3]), p[:, TK:] * (dp[:, TK:] - st[:, 3:4])], axis=1)
                dsb = ds.astype(BF16)
                dv_ref[...] += _merge(dv2[:TK], dv2[TK:])
                if packed:
                    dk2 = _tn(dsb, q2)
                    dk_ref[...] += _merge(dk2[:TK], dk2[TK:])
                    dq_ref[rows, :] += _nn(dsb, k_both)
                else:
                    dk_ref[...] += jnp.concatenate([_tn(dsb[:, :TK], qh[0]), _tn(dsb[:, TK:], qh[1])], axis=1)
                    dq_ref[rows, :] += jnp.concatenate([_nn(dsb[:, :TK], kh[0]), _nn(dsb[:, TK:], kh[1])], axis=1)
                if has_bias:
                    lane = _lane((TQ, LANES))
                    r0 = jnp.sum(ds[:, :TK], axis=-1, keepdims=True)
                    r1 = jnp.sum(ds[:, TK:], axis=-1, keepdims=True)
                    dr_ref[0, rows, :] += jnp.where(lane == 0, r0, jnp.where(lane == 1, r1, 0.0))
                    return carry - jnp.sum(ds, axis=0, keepdims=True)
                return carry

            def step(ib, carry, masked):
                s, dp = products(ib)
                return update(ib, s, dp, carry, masked)

            def two(ib, carry):
                s_a, dp_a = products(ib)
                s_b, dp_b = products(ib + 1)
                return update(ib + 1, s_b, dp_b, update(ib, s_a, dp_a, carry, False), False)

            dc = step(jb, jnp.zeros((1, 2 * TK), F32), True)
            n_rest = n_q - 1 - jb
            dc = lax.fori_loop(0, n_rest // 2, lambda t, cr: two(jb + 1 + 2 * t, cr), dc)
            dc = lax.cond(n_rest % 2 == 1, lambda cr: step(n_q - 1, cr, False), lambda cr: cr, dc)
            if has_bias:
                sub = lax.broadcasted_iota(jnp.int32, (8, TK), 0)
                dc_ref[0, 0] = jnp.where(sub == 0, dc[:, :TK], jnp.where(sub == 1, dc[:, TK:], 0.0))

        in_specs = [pl.BlockSpec((S, wq), lambda p, j: (0, q_blk0 + p)),
                    pl.BlockSpec((TK, wq), lambda p, j: (j, k_blk0 + p)),
                    pl.BlockSpec((TK, LANES), lambda p, j: (j, v_blk0 + p)),
                    pl.BlockSpec((S, LANES), lambda p, j: (0, p)),
                    pl.BlockSpec((1, S, LANES), lambda p, j: (p, 0, 0))]
        args = [q, k, v, do, stats]
        out_specs = [pl.BlockSpec((S, wq), lambda p, j: (0, p)),
                     pl.BlockSpec((TK, wq), lambda p, j: (j, p)),
                     pl.BlockSpec((TK, LANES), lambda p, j: (j, p))]
        out_shape = [jax.ShapeDtypeStruct((S, n_pairs * wq), F32), jax.ShapeDtypeStruct((S, n_pairs * wq), F32),
                     jax.ShapeDtypeStruct((S, n_pairs * LANES), F32)]
        if has_bias:
            in_specs += [pl.BlockSpec((1, 1, 8, TK), lambda p, j: (p, j, 0, 0))]
            args += [cumt]
            out_specs += [pl.BlockSpec((1, 1, 8, TK), lambda p, j: (p, j, 0, 0)),
                          pl.BlockSpec((1, S, LANES), lambda p, j: (p, 0, 0))]
            out_shape += [jax.ShapeDtypeStruct((n_pairs, S // TK, 8, TK), F32),
                          jax.ShapeDtypeStruct((n_pairs, S, LANES), F32)]
        return pl.pallas_call(
            body, name=name, grid=(n_pairs, S // TK), in_specs=in_specs, out_specs=out_specs, out_shape=out_shape,
            compiler_params=_params(("parallel", "arbitrary")))(*args)

    def fox_post(dcum, z, bf_pad):
        def rev(i):
            return n_tok - 1 - i

        def body(dc_ref, z_ref, b_ref, dff_ref, acc_ref, carry):
            @pl.when(pl.program_id(0) == 0)
            def _():
                carry[...] = jnp.zeros_like(carry)
                acc_ref[...] = jnp.zeros_like(acc_ref)

            _, sig_neg = _log_f_terms(z_ref[...], b_ref[...])
            row = lax.broadcasted_iota(jnp.int32, (TM, TM), 0)
            col = lax.broadcasted_iota(jnp.int32, (TM, TM), 1)
            tri = (col >= row).astype(F32)
            dlog = jnp.dot(tri, dc_ref[...], precision=lax.Precision.HIGHEST, preferred_element_type=F32) + carry[...]
            carry[...] = dlog[0:1, :]
            dff = dlog * sig_neg
            dff_ref[...] = dff
            acc_ref[0:1, :] += jnp.sum(dff, axis=0, keepdims=True)

        return pl.pallas_call(
            body, name="fox_post", grid=(n_tok,),
            in_specs=[pl.BlockSpec((TM, LANES), lambda i: (rev(i), 0)),
                      pl.BlockSpec((TM, LANES), lambda i: (rev(i), misc_blk)), const((1, LANES))],
            out_specs=[pl.BlockSpec((TM, LANES), lambda i: (rev(i), 0)), const((8, LANES))],
            out_shape=[jax.ShapeDtypeStruct((S, LANES), F32), jax.ShapeDtypeStruct((8, LANES), F32)],
            scratch_shapes=[pltpu.VMEM((1, LANES), F32)],
            compiler_params=_params(("arbitrary",)))(dcum, z, bf_pad)

    def mla_post(dq, dk, dv, dff, z, gq, gkv, w_uq, w_uk, w_v, cos_t, sin_t):
        def body(dq_ref, dk_ref, dv_ref, dff_ref, ql_ref, kvl_ref, gq_ref, gkv_ref, wq_ref, wk_ref, wv_ref,
                 c_ref, s_ref, zq_ref, zkv_ref, zms_ref, dwq_ref, dwk_ref, dwv_ref, dgq_ref, dgkv_ref):
            @pl.when(pl.program_id(0) == 0)
            def _():
                for r in (dwq_ref, dwk_ref, dwv_ref, dgq_ref, dgkv_ref):
                    r[...] = jnp.zeros_like(r)

            cos1, sin1 = c_ref[...], s_ref[...]
            gqv, gkvv = gq_ref[...], gkv_ref[...]
            qn, qxh, qrstd = _rms(ql_ref[...], gqv)
            dq_pre = _rope_t(dq_ref[...] * MLA_SCALE, jnp.tile(cos1, (1, MH)), jnp.tile(sin1, (1, MH))).astype(BF16)
            dwq_ref[...] += _tn(qn.astype(BF16), dq_pre)
            dql, dgq = _rms_bwd(_nt(dq_pre, wq_ref[...]), gqv, qxh, qrstd)
            zq_ref[...] = dql.astype(BF16)
            dgq_ref[0:1, :] += dgq

            dkv = dk_ref[...]
            lane = _lane(dkv.shape) & (HEAD_PAD - 1)
            dkn = jnp.where(lane < HEAD_DIM, dkv, 0.0).astype(BF16)
            dkr = dkv[:, 0:HEAD_PAD]
            for hd in range(1, MH):
                dkr = dkr + dkv[:, hd * HEAD_PAD:(hd + 1) * HEAD_PAD]
            lane1 = _lane(dkr.shape)
            dkr = jnp.where((lane1 >= ROPE_LO) & (lane1 < ROPE_LO + ROPE_DIM), dkr, 0.0)
            dkr = _rope_t(dkr, cos1, sin1)
            zms_ref[...] = (dkr + dff_ref[...]).astype(BF16)

            kvn, kxh, krstd = _rms(kvl_ref[...], gkvv)
            kvb = kvn.astype(BF16)
            dvb = dv_ref[...].astype(BF16)
            dwk_ref[...] += _tn(kvb, dkn)
            dwv_ref[...] += _tn(kvb, dvb)
            dkvl, dgkv = _rms_bwd(_nt(dkn, wk_ref[...]) + _nt(dvb, wv_ref[...]), gkvv, kxh, krstd)
            zkv_ref[...] = dkvl.astype(BF16)
            dgkv_ref[0:1, :] += dgkv

        return pl.pallas_call(
            body, name="mla_post", grid=(n_tok,),
            in_specs=[tok(QW), tok(QW), tok(MW), tok(LANES), tok(QL, (cfg.o_ql - ZO) // QL), tok(KVL, (cfg.o_kv - ZO) // KVL),
                      const((1, QL)), const((1, KVL)), const((QL, QW)), const((KVL, QW)), const((KVL, MW)),
                      tok(LANES), tok(LANES)],
            out_specs=[tok(QL), tok(KVL), tok(LANES), const((QL, QW)), const((KVL, QW)), const((KVL, MW)),
                       const((8, QL)), const((8, KVL))],
            out_shape=[jax.ShapeDtypeStruct((S, QL), BF16), jax.ShapeDtypeStruct((S, KVL), BF16),
                       jax.ShapeDtypeStruct((S, LANES), BF16), jax.ShapeDtypeStruct((QL, QW), F32),
                       jax.ShapeDtypeStruct((KVL, QW), F32), jax.ShapeDtypeStruct((KVL, MW), F32),
                       jax.ShapeDtypeStruct((8, QL), F32), jax.ShapeDtypeStruct((8, KVL), F32)],
            compiler_params=_params(("arbitrary",)))(dq, dk, dv, dff, z, z, gq, gkv, w_uq, w_uk, w_v, cos_t, sin_t)

    def bwd_in(dz, w_in, x, dxn, g, scale):
        def body(dz_ref, w_ref, x_ref, dx_ref, g_ref, sc_ref, o_ref, acc_ref):
            @pl.when(pl.program_id(0) == 0)
            def _():
                acc_ref[...] = jnp.zeros_like(acc_ref)

            dh = _nt(dz_ref[...], w_ref[...])
            gv, mod = g_ref[...], 1.0 + sc_ref[...]
            _, xh, rstd = _rms(x_ref[...], gv)
            t = dh * xh
            acc_ref[0:1, :] += jnp.sum(dh, axis=0, keepdims=True)
            acc_ref[1:2, :] += jnp.sum(t * gv, axis=0, keepdims=True)
            acc_ref[2:3, :] += jnp.sum(t * mod, axis=0, keepdims=True)
            dx, _ = _rms_bwd(dh, gv * mod, xh, rstd)
            o_ref[...] = dx_ref[...] + dx

        return pl.pallas_call(
            body, name="bwd_in", grid=(n_tok,),
            in_specs=[tok(NZ), const((D, NZ)), tok(D), tok(D), const((1, D)), const((1, D))],
            out_specs=[tok(D), const((8, D))],
            out_shape=[jax.ShapeDtypeStruct((S, D), F32), jax.ShapeDtypeStruct((8, D), F32)],
            compiler_params=_params(("arbitrary",)))(dz, w_in, x, dxn, g, scale)


    def pair_rows(cum):
        n_pairs = FH // 2
        ct = jnp.pad(cum[:, :FH].T.reshape(n_pairs, 2, S), ((0, 0), (0, 6), (0, 0)))
        return ct.reshape(n_pairs, 8, S // TK, TK).transpose(0, 2, 1, 3)

    def bias_grad(dc, dr):
        n_pairs = FH // 2
        d = dc.transpose(0, 2, 1, 3).reshape(n_pairs, 8, S)[:, :2, :].reshape(FH, S).T
        d = d + dr[:, :, :2].transpose(1, 0, 2).reshape(S, FH)
        return jnp.pad(d, ((0, 0), (0, LANES - FH)))

    def layer_forward(x, wl, mod):
        shift, scale, gate = mod
        h, z, qkv = ln_inproj(x, wl.norm_g, scale, shift, wl.w_in)
        ct = pair_rows(fox_prep(z, wl.bf_pad))
        of, lse_f = attn_fwd(qkv, qkv, qkv, cfg.o_fq // LANES, cfg.o_fk // LANES, cfg.o_fv // LANES, ct,
                             True, False, "fox_fwd")
        qp, kp, vp = mla_prep(z, wl.gq, wl.gkv, wl.w_uq, wl.w_uk, wl.w_v, cfg.cos_t, cfg.sin_t)
        om, lse_m = attn_fwd(qp, kp, vp, 0, 0, 0, None, False, True, "mla_fwd")
        xn, u, y = gate_outproj(of, om, z, wl.w_out, x, gate)
        saved = types.SimpleNamespace(x=x, h=h, z=z, qkv=qkv, ct=ct, of=of, lse_f=lse_f, qp=qp, kp=kp, vp=vp,
                                      om=om, lse_m=lse_m, u=u, y=y)
        return xn, saved

    def layer_backward(dxn, sv, wl, mod):
        shift, scale, gate = mod
        do_f, do_m, dfg, dmg, dy, acc_o = bwd_out(dxn, gate, sv.y, wl.w_out, sv.of, sv.om, sv.z)
        dw_out = matmul_tn(sv.u, dy, "dw_out")
        st_f = attn_stats(do_f, sv.of, sv.lse_f, "fox_stats")
        dfq, dfk, dfv, dck, dcr = attn_bwd(sv.qkv, sv.qkv, sv.qkv, do_f, st_f, cfg.o_fq // LANES, cfg.o_fk // LANES,
                                           cfg.o_fv // LANES, sv.ct, True, False, "fox_bwd")
        dff, acc_f = fox_post(bias_grad(dck, dcr), sv.z, wl.bf_pad)
        st_m = attn_stats(do_m, sv.om, sv.lse_m, "mla_stats")
        dqp, dkp, dvp = attn_bwd(sv.qp, sv.kp, sv.vp, do_m, st_m, 0, 0, 0, None, False, True, "mla_bwd")
        zq, zkv, zms, dw_uq, dw_uk, dw_v, dgq, dgkv = mla_post(
            dqp, dkp, dvp, dff, sv.z, wl.gq, wl.gkv, wl.w_uq, wl.w_uk, wl.w_v, cfg.cos_t, cfg.sin_t)
        dz = jnp.concatenate([(dfq * FOX_SCALE).astype(BF16), dfk.astype(BF16), dfv.astype(BF16), dfg, dmg, zq, zkv, zms], axis=1)
        dx, acc_i = bwd_in(dz, wl.w_in, sv.x, dxn, wl.norm_g, scale)
        dw_in = matmul_tn(sv.h, dz, "dw_in")
        grads = types.SimpleNamespace(
            w_in=dw_in, w_out=dw_out, w_uq=dw_uq, w_uk=dw_uk, w_v=dw_v, gq=dgq[0], gkv=dgkv[0],
            b_f=acc_f[0, :FH], norm_g=acc_i[2], dmod=jnp.concatenate([acc_i[0], acc_i[1], acc_o[0]]))
        return dx, grads

    return types.SimpleNamespace(layer_forward=layer_forward, layer_backward=layer_backward, final_loss=final_loss)


def _pack_rows(parts, dtype, row_multiple):
    flat = jnp.concatenate([p.reshape(-1).astype(dtype) for p in parts])
    per = SLAB_COLS * row_multiple
    total = -(-flat.shape[0] // per) * per
    return jnp.pad(flat, (0, total - flat.shape[0])).reshape(total // SLAB_COLS, SLAB_COLS)


def _unpack(flat, shapes):
    out, off = [], 0
    for shp in shapes:
        n = 1
        for d in shp:
            n *= d
        out.append(flat[off:off + n].reshape(shp))
        off += n
    return out


def kernel(x, c, positions, norm_g, w_ada, b_ada, w_in, b_f, q_norm_g, w_uq, kv_norm_g, w_ukv, w_out, final_g, loss_target, m_norm_g, m_w_ada, m_b_ada, m_w_in, m_b_f, m_q_norm_g, m_w_uq, m_kv_norm_g, m_w_ukv, m_w_out, m_final_g, v_norm_g, v_w_ada, v_b_ada, v_w_in, v_b_f, v_q_norm_g, v_w_uq, v_kv_norm_g, v_w_ukv, v_w_out, v_final_g):
    S, D = x.shape[1], x.shape[2]
    L = norm_g.shape[0]
    FH = b_f.shape[1]
    QL, KVL = q_norm_g.shape[1], kv_norm_g.shape[1]
    MH = w_ukv.shape[2] * N_CHIPS // (2 * HEAD_DIM)
    FW, MW = FH * HEAD_DIM, MH * HEAD_DIM
    NA = w_ada.shape[2]
    n_in = w_in.shape[2] * N_CHIPS
    cfg = types.SimpleNamespace(S=S, D=D, FW=FW, MW=MW, QL=QL, KVL=KVL, FH=FH, MH=MH)
    cfg.o_fq, cfg.o_fk, cfg.o_fv, cfg.o_fg, cfg.o_mg = 0, FW, 2 * FW, 3 * FW, 4 * FW
    cfg.o_ql = 4 * FW + MW
    cfg.o_kv = cfg.o_ql + QL
    cfg.o_ms = cfg.o_kv + KVL
    cfg.NZ = cfg.o_ms + LANES
    assert FW == MW and FH % 2 == 0 and MH % 2 == 0 and cfg.o_ql % QL == 0 and cfg.o_kv % KVL == 0 and KVL == LANES
    assert n_in == 4 * FW + FH + QL + KVL + ROPE_DIM + MW

    mx, my, mc = _my_pos()
    my_chip = 2 * mx + my
    my_dev = 2 * my_chip + mc

    inv_freq = 1.0 / (ROPE_THETA ** (jnp.arange(0, ROPE_DIM, 2, dtype=F32) / ROPE_DIM))
    ang = positions[0].astype(F32)[:, None] * inv_freq
    cos, sin = jnp.cos(ang), jnp.sin(ang)
    cfg.cos_t = jnp.concatenate([jnp.ones((S, ROPE_LO), F32), cos, cos, jnp.ones((S, HEAD_PAD - ROPE_LO - ROPE_DIM), F32)], axis=1)
    cfg.sin_t = jnp.concatenate([jnp.zeros((S, ROPE_LO), F32), -sin, sin, jnp.zeros((S, HEAD_PAD - ROPE_LO - ROPE_DIM), F32)], axis=1)

    shard_shapes = [w_in.shape, w_uq.shape, w_ukv.shape, w_out.shape]
    slab = _pack_rows([w_in, w_uq, w_ukv, w_out], BF16, 32)
    half = slab.shape[0] // 2
    mine = lax.dynamic_slice_in_dim(slab, mc * half, half, axis=0)
    got = chip_exchange(mine, True, "weights_chips")
    other = sibling_swap(got.reshape(N_CHIPS * half, SLAB_COLS), "weights_sibling").reshape(N_CHIPS, half, SLAB_COLS)
    lo = jnp.where(mc == 0, got, other)
    hi = jnp.where(mc == 0, other, got)
    slabs = jnp.concatenate([lo, hi], axis=1).reshape(N_CHIPS, -1)
    per_chip = [_unpack(slabs[t], shard_shapes) for t in range(N_CHIPS)]
    w_in_f = jnp.concatenate([p[0] for p in per_chip], axis=2)
    w_uq_f = jnp.concatenate([p[1] for p in per_chip], axis=2)
    w_ukv_f = jnp.concatenate([p[2] for p in per_chip], axis=2)
    w_out_f = jnp.concatenate([p[3] for p in per_chip], axis=1)

    sizes = (FW, FW, FW, FH, FW, QL, KVL, ROPE_DIM, MW)
    offs = [0]
    for sz in sizes:
        offs.append(offs[-1] + sz)
    seg = [w_in_f[:, :, offs[i]:offs[i + 1]] for i in range(len(sizes))]
    fq_w, fk_w, fv_w, ff_w, fg_w, ql_w, kvl_w, kr_w, mg_w = seg
    zeros = lambda n: jnp.zeros((L, D, n), BF16)
    w_in_p = jnp.concatenate([fq_w, fk_w, fv_w, fg_w, mg_w, ql_w, kvl_w, ff_w, zeros(ROPE_LO - FH), kr_w,
                              zeros(HEAD_PAD - ROPE_LO - ROPE_DIM)], axis=2)
    w_uq_p = jnp.pad(w_uq_f.reshape(L, QL, MH, HEAD_DIM + ROPE_DIM), ((0, 0), (0, 0), (0, 0), (0, HEAD_PAD - HEAD_DIM - ROPE_DIM)))
    w_uq_p = w_uq_p.reshape(L, QL, MH * HEAD_PAD)
    w_ukv4 = w_ukv_f.reshape(L, KVL, MH, 2 * HEAD_DIM)
    w_uk_p = jnp.pad(w_ukv4[..., :HEAD_DIM], ((0, 0), (0, 0), (0, 0), (0, HEAD_PAD - HEAD_DIM))).reshape(L, KVL, MH * HEAD_PAD)
    w_v_p = w_ukv4[..., HEAD_DIM:].reshape(L, KVL, MW)

    c_all = allgather8(c.reshape(8, D // 8), "gather_c").reshape(N_DEV, D)
    c_pad = jnp.pad(c_all, ((0, 16 - N_DEV), (0, 0)))
    mod_part = ada_forward(c_pad, w_ada)[:, :N_DEV, :]
    mod_all = allgather8(mod_part.reshape(-1, LANES), "gather_mod").reshape(N_CHIPS, 2, L, N_DEV, NA)[:, 0]
    mod_full = mod_all.transpose(1, 2, 0, 3).reshape(L, N_DEV, N_CHIPS * NA) + b_ada[:, None, :]
    mod_mine = lax.dynamic_index_in_dim(mod_full, my_dev, axis=1, keepdims=True)

    step = make_step(cfg)
    bf_pad = jnp.pad(b_f, ((0, 0), (0, LANES - FH)))
    layers, mods = [], []
    for l in range(L):
        layers.append(types.SimpleNamespace(
            norm_g=norm_g[l][None], w_in=w_in_p[l], bf_pad=bf_pad[l][None], gq=q_norm_g[l][None], gkv=kv_norm_g[l][None],
            w_uq=w_uq_p[l], w_uk=w_uk_p[l], w_v=w_v_p[l], w_out=w_out_f[l]))
        mods.append((mod_mine[l, :, :D], mod_mine[l, :, D:2 * D], mod_mine[l, :, 2 * D:]))

    xl = x[0]
    saved = []
    for l in range(L):
        xl, sv = step.layer_forward(xl, layers[l], mods[l])
        saved.append(sv)
    dx, acc_fin, loss_part = step.final_loss(xl, final_g[None], loss_target[0])
    loss = lax.psum(loss_part[0, 0], ("x", "y", "c"))
    gl = [None] * L
    for l in reversed(range(L)):
        dx, gl[l] = step.layer_backward(dx, saved[l], layers[l], mods[l])
    grad_x = dx[None]

    stack = lambda name: jnp.stack([getattr(g, name) for g in gl])
    small_parts = [stack("norm_g"), stack("dmod"), stack("b_f"), stack("gq"), stack("gkv"), acc_fin[0]]
    small_shapes = [p.shape for p in small_parts]
    small = _pack_rows(small_parts, F32, 8).reshape(-1, LANES)
    small_all = allgather8(small, "gather_small").reshape(N_DEV, -1, LANES)
    small_sum = sum_leading(small_all, "sum_small")
    g_norm_g, g_b_ada, g_b_f, g_q_norm_g, g_kv_norm_g, g_final_g = _unpack(small_sum.reshape(-1), small_shapes)

    n_ng = L * D
    dmod_all = small_all.reshape(N_DEV, -1)[:, n_ng:n_ng + L * 3 * D].reshape(N_DEV, L, 3 * D)
    dmod_cols = lax.dynamic_slice_in_dim(dmod_all, my_chip * NA, NA, axis=2).transpose(1, 0, 2)
    g_w_ada = ada_backward(c_pad, jnp.pad(dmod_cols, ((0, 0), (0, 16 - N_DEV), (0, 0))))

    dw_in_p = stack("w_in")
    ms = cfg.o_ms
    dw_in_f = jnp.concatenate([
        dw_in_p[:, :, 0:3 * FW], dw_in_p[:, :, ms:ms + FH], dw_in_p[:, :, cfg.o_fg:cfg.o_fg + FW],
        dw_in_p[:, :, cfg.o_ql:cfg.o_ql + QL + KVL], dw_in_p[:, :, ms + ROPE_LO:ms + ROPE_LO + ROPE_DIM],
        dw_in_p[:, :, cfg.o_mg:cfg.o_mg + MW]], axis=2)
    dw_uq_f = stack("w_uq").reshape(L, QL, MH, HEAD_PAD)[..., :HEAD_DIM + ROPE_DIM].reshape(L, QL, -1)
    dw_ukv_f = jnp.concatenate([stack("w_uk").reshape(L, KVL, MH, HEAD_PAD)[..., :HEAD_DIM],
                                stack("w_v").reshape(L, KVL, MH, HEAD_DIM)], axis=3).reshape(L, KVL, -1)
    dw_out_f = stack("w_out")
    dest = []
    for t in range(N_CHIPS):
        dest.append(_pack_rows([
            dw_in_f[:, :, t * w_in.shape[2]:(t + 1) * w_in.shape[2]], dw_uq_f[:, :, t * w_uq.shape[2]:(t + 1) * w_uq.shape[2]],
            dw_ukv_f[:, :, t * w_ukv.shape[2]:(t + 1) * w_ukv.shape[2]], dw_out_f[:, t * w_out.shape[1]:(t + 1) * w_out.shape[1], :]],
            F32, 32))
    gs = jnp.stack(dest)
    ghalf = gs.shape[1] // 2
    gs = gs.reshape(N_CHIPS, 2, ghalf, SLAB_COLS)
    keep = lax.dynamic_index_in_dim(gs, mc, axis=1, keepdims=False).reshape(N_CHIPS * ghalf, SLAB_COLS)
    give = lax.dynamic_index_in_dim(gs, 1 - mc, axis=1, keepdims=False).reshape(N_CHIPS * ghalf, SLAB_COLS)
    sib = sibling_swap(give, "grads_sibling")
    chip_part = add_cast(keep, sib, BF16, "grads_chip_sum").reshape(N_CHIPS, ghalf, SLAB_COLS)
    parts = chip_exchange(chip_part, False, "grads_chips")
    red_half = sum_chips(parts, "grads_sum")
    red_other = sibling_swap(red_half, "grads_back")
    red = jnp.concatenate([jnp.where(mc == 0, red_half, red_other), jnp.where(mc == 0, red_other, red_half)], axis=0)
    g_w_in, g_w_uq, g_w_ukv, g_w_out = _unpack(red.reshape(-1), shard_shapes)

    def big(w, g, m, v, name):
        shp = w.shape
        two = lambda a: a.reshape(-1, shp[-1])
        return [o.reshape(shp) for o in adamw(two(w), two(g), two(m), two(v), name)]

    names = ["norm_g", "w_ada", "b_ada", "w_in", "b_f", "q_norm_g", "w_uq", "kv_norm_g", "w_ukv", "w_out", "final_g"]
    ws = dict(norm_g=norm_g, w_ada=w_ada, b_ada=b_ada, w_in=w_in, b_f=b_f, q_norm_g=q_norm_g, w_uq=w_uq,
              kv_norm_g=kv_norm_g, w_ukv=w_ukv, w_out=w_out, final_g=final_g)
    msd = dict(norm_g=m_norm_g, w_ada=m_w_ada, b_ada=m_b_ada, w_in=m_w_in, b_f=m_b_f, q_norm_g=m_q_norm_g, w_uq=m_w_uq,
               kv_norm_g=m_kv_norm_g, w_ukv=m_w_ukv, w_out=m_w_out, final_g=m_final_g)
    vsd = dict(norm_g=v_norm_g, w_ada=v_w_ada, b_ada=v_b_ada, w_in=v_w_in, b_f=v_b_f, q_norm_g=v_q_norm_g, w_uq=v_w_uq,
               kv_norm_g=v_kv_norm_g, w_ukv=v_w_ukv, w_out=v_w_out, final_g=v_final_g)
    gsd = dict(norm_g=g_norm_g, w_ada=g_w_ada, b_ada=g_b_ada, w_in=g_w_in, b_f=g_b_f, q_norm_g=g_q_norm_g, w_uq=g_w_uq,
               kv_norm_g=g_kv_norm_g, w_ukv=g_w_ukv, w_out=g_w_out, final_g=g_final_g)
    small_names = ["norm_g", "b_ada", "b_f", "q_norm_g", "kv_norm_g", "final_g"]
    sm_shapes = [ws[n].shape for n in small_names]
    pk = lambda d: _pack_rows([d[n] for n in small_names], F32, 8).reshape(-1, LANES)
    sm_out = adamw(pk(ws), pk(gsd), pk(msd), pk(vsd), "adamw_small")
    sm_d, sm_m, sm_v = [dict(zip(small_names, _unpack(o.reshape(-1), sm_shapes))) for o in sm_out]
    delta, new_m, new_v = dict(sm_d), dict(sm_m), dict(sm_v)
    for n in ["w_ada", "w_in", "w_uq", "w_ukv", "w_out"]:
        delta[n], new_m[n], new_v[n] = big(ws[n], gsd[n], msd[n], vsd[n], "adamw_" + n)

    return (loss, grad_x, *[gsd[n] for n in names], *[delta[n] for n in names],
            *[new_m[n] for n in names], *[new_v[n] for n in names])
```

```python
import functools
import types

import jax
import jax.numpy as jnp
from jax import lax
from jax.experimental import pallas as pl
from jax.experimental.pallas import tpu as pltpu

F32 = jnp.float32
BF16 = jnp.bfloat16
MESH = pl.DeviceIdType.MESH

N_CHIPS = 4
N_DEV = 8
HEAD_DIM = 64
ROPE_DIM = 32
ROPE_THETA = 10000.0
HEAD_PAD = 128
ROPE_LO = 64
ROPE_HALF = 16
LANES = 128
EPS = 1e-6
NEG = -1e30
ADAM_LR = 0.001
ADAM_B1 = 0.9
ADAM_B2 = 0.999
ADAM_EPS = 1e-08
ADAM_WD = 0.01
ADAM_STEP = 10
VMEM_LIMIT = 48 * 1024 * 1024
SLAB_COLS = 1024


def _params(sem=None, vmem=VMEM_LIMIT):
    return pltpu.CompilerParams(dimension_semantics=sem, vmem_limit_bytes=vmem)


def _nn(a, b):
    return jnp.dot(a, b, preferred_element_type=F32)


def _nt(a, b):
    return lax.dot_general(a, b, (((1,), (1,)), ((), ())), preferred_element_type=F32)


def _tn(a, b):
    return lax.dot_general(a, b, (((0,), (0,)), ((), ())), preferred_element_type=F32)


def _sigmoid(x):
    return 1.0 / (1.0 + jnp.exp(-x))


def _lane(shape):
    return lax.broadcasted_iota(jnp.int32, shape, len(shape) - 1)


def _pick(n, cands):
    for c in cands:
        if n % c == 0:
            return c
    return n


def _my_pos():
    return lax.axis_index("x"), lax.axis_index("y"), lax.axis_index("c")


def allgather8(xs, name):
    m_per, n = xs.shape

    def body(x_ref, out_ref, send_sems, recv_sems, local_sem):
        x, y, c = _my_pos()
        me, sibling = (x, y, c), (x, y, 1 - c)
        chips = [(1 - x, y), (x, 1 - y), (1 - x, 1 - y)]

        def rows(px, py, pc):
            return out_ref.at[pl.ds((4 * px + 2 * py + pc) * m_per, m_per), :]

        def copy(k, block, to, src=None):
            return pltpu.make_async_remote_copy(
                src_ref=rows(*block) if src is None else src, dst_ref=rows(*block),
                send_sem=send_sems.at[k], recv_sem=recv_sems.at[k], device_id=to, device_id_type=MESH)

        mine = pltpu.make_async_copy(x_ref, rows(*me), local_sem)
        mine.start()
        first = [copy(0, me, sibling, src=x_ref)]
        first += [copy(1 + j, me, (*chip, c), src=x_ref) for j, chip in enumerate(chips)]
        for cp in first:
            cp.start()
        passed = [copy(4 + j, (*chip, c), sibling) for j, chip in enumerate(chips)]
        for j, chip in enumerate(chips):
            copy(1 + j, (*chip, c), me).wait_recv()
            passed[j].start()
        copy(0, sibling, me).wait_recv()
        for j, chip in enumerate(chips):
            copy(4 + j, (*chip, 1 - c), me).wait_recv()
        for cp in first + passed:
            cp.wait_send()
        mine.wait()

    return pl.pallas_call(
        body, name=name,
        out_shape=jax.ShapeDtypeStruct((N_DEV * m_per, n), xs.dtype),
        in_specs=[pl.BlockSpec(memory_space=pltpu.VMEM)],
        out_specs=pl.BlockSpec(memory_space=pltpu.VMEM),
        scratch_shapes=[pltpu.SemaphoreType.DMA((7,)), pltpu.SemaphoreType.DMA((7,)), pltpu.SemaphoreType.DMA],
    )(xs)


def chip_exchange(xs, bcast, name):
    r, n = xs.shape[-2:]

    def body(x_ref, out_ref, send_sems, recv_sems, local_sem):
        x, y, c = _my_pos()
        s = 2 * x + y
        chips = [(1 - x, y), (x, 1 - y), (1 - x, 1 - y)]

        def src(t):
            return x_ref if bcast else x_ref.at[t]

        local = pltpu.make_async_copy(src(s), out_ref.at[s], local_sem)
        local.start()
        sends = []
        for k, (tx, ty) in enumerate(chips):
            cp = pltpu.make_async_remote_copy(
                src_ref=src(2 * tx + ty), dst_ref=out_ref.at[s], send_sem=send_sems.at[k],
                recv_sem=recv_sems.at[k], device_id=(tx, ty, c), device_id_type=MESH)
            cp.start()
            sends.append(cp)
        for k, (tx, ty) in enumerate(chips):
            t = 2 * tx + ty
            pltpu.make_async_remote_copy(
                src_ref=src(t), dst_ref=out_ref.at[t], send_sem=send_sems.at[k],
                recv_sem=recv_sems.at[k], device_id=(tx, ty, c), device_id_type=MESH).wait_recv()
        for cp in sends:
            cp.wait_send()
        local.wait()

    return pl.pallas_call(
        body, name=name,
        out_shape=jax.ShapeDtypeStruct((N_CHIPS, r, n), xs.dtype),
        in_specs=[pl.BlockSpec(memory_space=pl.ANY)],
        out_specs=pl.BlockSpec(memory_space=pl.ANY),
        scratch_shapes=[pltpu.SemaphoreType.DMA((3,)), pltpu.SemaphoreType.DMA((3,)), pltpu.SemaphoreType.DMA],
    )(xs)


def sibling_swap(xs, name):
    def body(x_ref, out_ref, send_sem, recv_sem):
        x, y, c = _my_pos()
        cp = pltpu.make_async_remote_copy(
            src_ref=x_ref, dst_ref=out_ref, send_sem=send_sem, recv_sem=recv_sem,
            device_id=(x, y, 1 - c), device_id_type=MESH)
        cp.start()
        cp.wait()

    return pl.pallas_call(
        body, name=name,
        out_shape=jax.ShapeDtypeStruct(xs.shape, xs.dtype),
        in_specs=[pl.BlockSpec(memory_space=pl.ANY)],
        out_specs=pl.BlockSpec(memory_space=pl.ANY),
        scratch_shapes=[pltpu.SemaphoreType.DMA, pltpu.SemaphoreType.DMA],
    )(xs)


def sum_leading(xs, name):
    n, r, c = xs.shape

    def body(x_ref, o_ref):
        acc = x_ref[0]
        for i in range(1, n):
            acc = acc + x_ref[i]
        o_ref[...] = acc

    return pl.pallas_call(body, name=name, out_shape=jax.ShapeDtypeStruct((r, c), xs.dtype))(xs)


def add_cast(a, b, out_dtype, name):
    r, c = a.shape
    tr = _pick(r, (512, 256, 128, 64, 32, 16))

    def body(a_ref, b_ref, o_ref):
        o_ref[...] = (a_ref[...] + b_ref[...]).astype(out_dtype)

    spec = pl.BlockSpec((tr, c), lambda i: (i, 0))
    return pl.pallas_call(body, name=name, grid=(r // tr,), in_specs=[spec, spec], out_specs=spec,
                          out_shape=jax.ShapeDtypeStruct((r, c), out_dtype), compiler_params=_params(("parallel",)))(a, b)


def sum_chips(xs, name):
    n, r, c = xs.shape
    tr = _pick(r, (512, 256, 128, 64, 32, 16))

    def body(x_ref, o_ref):
        acc = x_ref[0].astype(F32)
        for i in range(1, n):
            acc = acc + x_ref[i].astype(F32)
        o_ref[...] = acc

    return pl.pallas_call(body, name=name, grid=(r // tr,),
                          in_specs=[pl.BlockSpec((n, tr, c), lambda i: (0, i, 0))],
                          out_specs=pl.BlockSpec((tr, c), lambda i: (i, 0)),
                          out_shape=jax.ShapeDtypeStruct((r, c), F32), compiler_params=_params(("parallel",)))(xs)


def ada_forward(c_all, w_ada):
    nl, d, n = w_ada.shape
    nb = c_all.shape[0]

    def body(c_ref, w_ref, o_ref):
        cv = c_ref[...]
        ca = (cv * _sigmoid(cv)).astype(BF16)
        o_ref[0] = _nn(ca, w_ref[0].astype(BF16))

    return pl.pallas_call(
        body, name="ada_forward", grid=(nl,),
        in_specs=[pl.BlockSpec((nb, d), lambda l: (0, 0)), pl.BlockSpec((1, d, n), lambda l: (l, 0, 0))],
        out_specs=pl.BlockSpec((1, nb, n), lambda l: (l, 0, 0)),
        out_shape=jax.ShapeDtypeStruct((nl, nb, n), F32), compiler_params=_params(("parallel",)))(c_all, w_ada)


def ada_backward(c_all, dmod):
    nl, nb, n = dmod.shape
    d = c_all.shape[1]

    def body(c_ref, g_ref, o_ref):
        cv = c_ref[...]
        ca = (cv * _sigmoid(cv)).astype(BF16)
        o_ref[0] = _tn(ca, g_ref[0].astype(BF16))

    return pl.pallas_call(
        body, name="ada_backward", grid=(nl,),
        in_specs=[pl.BlockSpec((nb, d), lambda l: (0, 0)), pl.BlockSpec((1, nb, n), lambda l: (l, 0, 0))],
        out_specs=pl.BlockSpec((1, d, n), lambda l: (l, 0, 0)),
        out_shape=jax.ShapeDtypeStruct((nl, d, n), F32), compiler_params=_params(("parallel",)))(c_all, dmod)


def matmul_tn(a, b, name):
    k, m = a.shape
    n = b.shape[1]
    tm, tk = _pick(m, (512, 256, 128)), _pick(k, (512, 256, 128))
    tn = n if n * (tm * 8 + tk * 4) <= VMEM_LIMIT // 2 else _pick(n, (512, 256, 128))

    def body(a_ref, b_ref, o_ref):
        @pl.when(pl.program_id(2) == 0)
        def _():
            o_ref[...] = jnp.zeros_like(o_ref)

        o_ref[...] += _tn(a_ref[...], b_ref[...])

    return pl.pallas_call(
        body, name=name, grid=(m // tm, n // tn, k // tk),
        in_specs=[pl.BlockSpec((tk, tm), lambda i, j, kk: (kk, i)), pl.BlockSpec((tk, tn), lambda i, j, kk: (kk, j))],
        out_specs=pl.BlockSpec((tm, tn), lambda i, j, kk: (i, j)),
        out_shape=jax.ShapeDtypeStruct((m, n), F32),
        compiler_params=_params(("parallel", "parallel", "arbitrary")))(a, b)


def adamw(w, g, m, v, name):
    nl, r, c = w.shape
    tr = _pick(r, (512, 256, 128, 64, 32, 16, 8))

    def body(w_ref, g_ref, m_ref, v_ref, d_ref, mo_ref, vo_ref):
        gv = g_ref[...]
        mn = ADAM_B1 * m_ref[...] + (1.0 - ADAM_B1) * gv
        vn = ADAM_B2 * v_ref[...] + (1.0 - ADAM_B2) * (gv * gv)
        m_hat = mn / (1.0 - ADAM_B1 ** ADAM_STEP)
        v_hat = vn / (1.0 - ADAM_B2 ** ADAM_STEP)
        d_ref[...] = -ADAM_LR * (m_hat / (jnp.sqrt(v_hat) + ADAM_EPS) + ADAM_WD * w_ref[...])
        mo_ref[...] = mn
        vo_ref[...] = vn

    spec = pl.BlockSpec((1, tr, c), lambda l, i: (l, i, 0))
    out = jax.ShapeDtypeStruct((nl, r, c), F32)
    return pl.pallas_call(body, name=name, grid=(nl, r // tr), in_specs=[spec] * 4, out_specs=[spec] * 3,
                          out_shape=[out] * 3, compiler_params=_params(("parallel", "parallel")))(w, g, m, v)


def _rope(t, cos_t, sin_t):
    w = t.shape[1]
    lane = _lane(t.shape) & (HEAD_PAD - 1)
    first_half = (lane >= ROPE_LO) & (lane < ROPE_LO + ROPE_HALF)
    partner = jnp.where(first_half, pltpu.roll(t, w - ROPE_HALF, 1), pltpu.roll(t, ROPE_HALF, 1))
    return t * cos_t + partner * sin_t


def _rope_t(dt, cos_t, sin_t):
    w = dt.shape[1]
    lane = _lane(dt.shape) & (HEAD_PAD - 1)
    first_half = (lane >= ROPE_LO) & (lane < ROPE_LO + ROPE_HALF)
    ds = dt * sin_t
    partner = jnp.where(first_half, pltpu.roll(ds, w - ROPE_HALF, 1), pltpu.roll(ds, ROPE_HALF, 1))
    return dt * cos_t + partner


def _rms(xv, g):
    rstd = lax.rsqrt(jnp.mean(xv * xv, axis=-1, keepdims=True) + EPS)
    xh = xv * rstd
    return xh * g, xh, rstd


def _rms_bwd(dy, g, xh, rstd):
    dxh = dy * g
    dx = rstd * (dxh - xh * jnp.mean(dxh * xh, axis=-1, keepdims=True))
    return dx, jnp.sum(dy * xh, axis=0, keepdims=True)


def make_step(cfg):
    S, D, NZ = cfg.S, cfg.D, cfg.NZ
    FW, MW, QL, KVL, FH, MH = cfg.FW, cfg.MW, cfg.QL, cfg.KVL, cfg.FH, cfg.MH
    QW = MH * HEAD_PAD
    TM = _pick(S, (256, 128))
    TQ = TK = _pick(S, (256, 128))
    n_tok = S // TM
    ZO = 3 * FW
    NZR = NZ - ZO
    misc_blk = (cfg.o_ms - ZO) // LANES
    FOX_SCALE = HEAD_DIM ** -0.5
    MLA_SCALE = (HEAD_DIM + ROPE_DIM) ** -0.5

    def tok(width, col=0):
        return pl.BlockSpec((TM, width), lambda i: (i, col))

    def const(shape):
        return pl.BlockSpec(shape, lambda i: tuple(0 for _ in shape))


    def ln_inproj(x, g, scale, shift, w_in):
        def body(x_ref, g_ref, sc_ref, sh_ref, w_ref, h_ref, z_ref, qkv_ref):
            y, _, _ = _rms(x_ref[...], g_ref[...])
            hb = (y * (1.0 + sc_ref[...]) + sh_ref[...]).astype(BF16)
            h_ref[...] = hb
            z = _nn(hb, w_ref[...])
            z_ref[...] = z[:, ZO:]
            qkv_ref[:, :FW] = (z[:, :FW] * FOX_SCALE).astype(BF16)
            qkv_ref[:, FW:] = z[:, FW:ZO].astype(BF16)

        return pl.pallas_call(
            body, name="ln_inproj", grid=(n_tok,),
            in_specs=[tok(D), const((1, D)), const((1, D)), const((1, D)), const((D, NZ))],
            out_specs=[tok(D), tok(NZR), tok(ZO)],
            out_shape=[jax.ShapeDtypeStruct((S, D), BF16), jax.ShapeDtypeStruct((S, NZR), F32),
                       jax.ShapeDtypeStruct((S, ZO), BF16)],
            compiler_params=_params(("parallel",)))(x, g, scale, shift, w_in)

    def _log_f_terms(misc, bf):
        lane = _lane(misc.shape)
        a = misc + bf
        e = jnp.exp(-jnp.abs(a))
        logf = jnp.minimum(a, 0.0) - jnp.log(1.0 + e)
        sig_neg = jnp.where(a >= 0, e, 1.0) / (1.0 + e)
        valid = lane < FH
        return jnp.where(valid, logf, 0.0), jnp.where(valid, sig_neg, 0.0)

    def fox_prep(z, bf_pad):
        def body(z_ref, b_ref, o_ref, carry):
            @pl.when(pl.program_id(0) == 0)
            def _():
                carry[...] = jnp.zeros_like(carry)

            logf, _ = _log_f_terms(z_ref[...], b_ref[...])
            row = lax.broadcasted_iota(jnp.int32, (TM, TM), 0)
            col = lax.broadcasted_iota(jnp.int32, (TM, TM), 1)
            tri = (col <= row).astype(F32)
            cum = jnp.dot(tri, logf, precision=lax.Precision.HIGHEST, preferred_element_type=F32) + carry[...]
            o_ref[...] = cum
            carry[...] = cum[TM - 1:TM, :]

        return pl.pallas_call(
            body, name="fox_prep", grid=(n_tok,),
            in_specs=[tok(LANES, misc_blk), const((1, LANES))], out_specs=tok(LANES),
            out_shape=jax.ShapeDtypeStruct((S, LANES), F32),
            scratch_shapes=[pltpu.VMEM((1, LANES), F32)],
            compiler_params=_params(("arbitrary",)))(z, bf_pad)

    def mla_prep(z, gq, gkv, w_uq, w_uk, w_v, cos_t, sin_t):
        def body(ql_ref, kvl_ref, ms_ref, gq_ref, gkv_ref, wq_ref, wk_ref, wv_ref, c_ref, s_ref, q_ref, k_ref, v_ref):
            cos1, sin1 = c_ref[...], s_ref[...]
            qn, _, _ = _rms(ql_ref[...], gq_ref[...])
            q = _nn(qn.astype(BF16), wq_ref[...])
            q_ref[...] = (_rope(q, jnp.tile(cos1, (1, MH)), jnp.tile(sin1, (1, MH))) * MLA_SCALE).astype(BF16)
            kvn, _, _ = _rms(kvl_ref[...], gkv_ref[...])
            kvb = kvn.astype(BF16)
            lane = _lane((TM, LANES))
            kr = jnp.where((lane >= ROPE_LO) & (lane < ROPE_LO + ROPE_DIM), ms_ref[...], 0.0)
            kr = _rope(kr, cos1, sin1)
            k_ref[...] = (_nn(kvb, wk_ref[...]) + jnp.tile(kr, (1, MH))).astype(BF16)
            v_ref[...] = _nn(kvb, wv_ref[...]).astype(BF16)

        return pl.pallas_call(
            body, name="mla_prep", grid=(n_tok,),
            in_specs=[tok(QL, (cfg.o_ql - ZO) // QL), tok(KVL, (cfg.o_kv - ZO) // KVL), tok(LANES, misc_blk),
                      const((1, QL)), const((1, KVL)), const((QL, QW)), const((KVL, QW)), const((KVL, MW)),
                      tok(LANES), tok(LANES)],
            out_specs=[tok(QW), tok(QW), tok(MW)],
            out_shape=[jax.ShapeDtypeStruct((S, QW), BF16), jax.ShapeDtypeStruct((S, QW), BF16),
                       jax.ShapeDtypeStruct((S, MW), BF16)],
            compiler_params=_params(("parallel",)))(z, z, z, gq, gkv, w_uq, w_uk, w_v, cos_t, sin_t)

    def _allowed(q0, k0, chunked):
        qi = q0 + lax.broadcasted_iota(jnp.int32, (TQ, TK), 0)
        ki = k0 + lax.broadcasted_iota(jnp.int32, (TQ, TK), 1)
        if chunked:
            return (ki >> 6) <= (qi >> 6)
        return ki <= qi

    def _heads(val, packed):
        if packed:
            lane = _lane(val.shape)
            zero = jnp.zeros_like(val)
            return [jnp.where(lane < HEAD_DIM, val, zero), jnp.where(lane >= HEAD_DIM, val, zero)]
        return [val[:, :HEAD_PAD], val[:, HEAD_PAD:]]

    def _merge(a0, a1):
        return jnp.where(_lane(a0.shape) < HEAD_DIM, a0, a1)

    def attn_fwd(q, k, v, q_blk0, k_blk0, v_blk0, cumt, packed, chunked, name):
        has_bias = cumt is not None
        n_pairs = (FH if packed else MH) // 2
        wq = LANES if packed else 2 * HEAD_PAD
        assert TQ == TK

        def body(*refs):
            if has_bias:
                q_ref, k_ref, v_ref, ct_ref, o_ref, lse_ref = refs
            else:
                q_ref, k_ref, v_ref, o_ref, lse_ref = refs
            i = pl.program_id(1)
            q0 = i * TQ
            qh = _heads(q_ref[...], packed)
            q_both = jnp.concatenate(qh, axis=0) if packed else None

            def scores(kb):
                kblk = k_ref[pl.ds(pl.multiple_of(kb * TK, TK), TK), :]
                if packed:
                    return _nt(q_both, kblk)
                return jnp.concatenate([_nt(qh[0], kblk[:, :HEAD_PAD]), _nt(qh[1], kblk[:, HEAD_PAD:])], axis=0)

            def update(kb, s, carry, masked):
                m, l, acc = carry
                k0 = pl.multiple_of(kb * TK, TK)
                s0, s1 = s[:TQ], s[TQ:]
                if has_bias:
                    ck = ct_ref[0, kb]
                    s0, s1 = s0 - ck[0:1, :], s1 - ck[1:2, :]
                if masked:
                    allow = _allowed(q0, k0, chunked)
                    s0, s1 = jnp.where(allow, s0, NEG), jnp.where(allow, s1, NEG)
                s = jnp.concatenate([s0, s1], axis=0)
                m_new = jnp.maximum(m, jnp.max(s, axis=-1, keepdims=True))
                p = jnp.exp(s - m_new)
                alpha = jnp.exp(m - m_new)
                l = alpha * l + jnp.sum(p, axis=-1, keepdims=True)
                acc = alpha * acc + _nn(p.astype(BF16), v_ref[pl.ds(k0, TK), :])
                return m_new, l, acc

            def two(kb, carry, last_masked):
                s_a, s_b = scores(kb), scores(kb + 1)
                return update(kb + 1, s_b, update(kb, s_a, carry, False), last_masked)

            init = (jnp.full((2 * TQ, 1), NEG, F32), jnp.zeros((2 * TQ, 1), F32), jnp.zeros((2 * TQ, LANES), F32))
            carry = lax.fori_loop(0, i // 2, lambda t, cr: two(2 * t, cr, False), init)
            m, l, acc = lax.cond(i % 2 == 1, lambda cr: two(i - 1, cr, True),
                                 lambda cr: update(i, scores(i), cr, True), carry)
            o = acc / l
            lse = m + jnp.log(l)
            lane = _lane((TQ, LANES))
            o_ref[...] = _merge(o[:TQ], o[TQ:])
            lse_ref[...] = jnp.where(lane == 0, lse[:TQ], jnp.where(lane == 1, lse[TQ:], 0.0))

        in_specs = [pl.BlockSpec((TQ, wq), lambda p, i: (i, q_blk0 + p)),
                    pl.BlockSpec((S, wq), lambda p, i: (0, k_blk0 + p)),
                    pl.BlockSpec((S, LANES), lambda p, i: (0, v_blk0 + p))]
        args = [q, k, v]
        if has_bias:
            in_specs += [pl.BlockSpec((1, S // TK, 8, TK), lambda p, i: (p, 0, 0, 0))]
            args += [cumt]
        return pl.pallas_call(
            body, name=name, grid=(n_pairs, S // TQ), in_specs=in_specs,
            out_specs=[pl.BlockSpec((TQ, LANES), lambda p, i: (i, p)),
                       pl.BlockSpec((TQ, LANES), lambda p, i: (i, p))],
            out_shape=[jax.ShapeDtypeStruct((S, n_pairs * LANES), F32),
                       jax.ShapeDtypeStruct((S, n_pairs * LANES), F32)],
            compiler_params=_params(("parallel", "parallel")))(*args)

    def gate_outproj(of, om, z, w_out, x, gate):
        def body(of_ref, om_ref, fg_ref, mg_ref, w_ref, x_ref, gt_ref, xn_ref, u_ref, y_ref):
            fg, mg = fg_ref[...], mg_ref[...]
            u = jnp.concatenate([of_ref[...] * fg * _sigmoid(fg), om_ref[...] * mg * _sigmoid(mg)], axis=1).astype(BF16)
            y = _nn(u, w_ref[...])
            u_ref[...] = u
            y_ref[...] = y.astype(BF16)
            xn_ref[...] = x_ref[...] + gt_ref[...] * y

        return pl.pallas_call(
            body, name="gate_outproj", grid=(n_tok,),
            in_specs=[tok(FW), tok(MW), tok(FW, (cfg.o_fg - ZO) // FW), tok(MW, (cfg.o_mg - ZO) // MW),
                      const((FW + MW, D)), tok(D), const((1, D))],
            out_specs=[tok(D), tok(FW + MW), tok(D)],
            out_shape=[jax.ShapeDtypeStruct((S, D), F32), jax.ShapeDtypeStruct((S, FW + MW), BF16),
                       jax.ShapeDtypeStruct((S, D), BF16)],
            compiler_params=_params(("parallel",)))(of, om, z, z, w_out, x, gate)

    def final_loss(x, g, target):
        def body(x_ref, g_ref, t_ref, dx_ref, acc_ref, loss_ref):
            @pl.when(pl.program_id(0) == 0)
            def _():
                acc_ref[...] = jnp.zeros_like(acc_ref)
                loss_ref[...] = jnp.zeros_like(loss_ref)

            gv = g_ref[...]
            y, xh, rstd = _rms(x_ref[...], gv)
            e = y - t_ref[...]
            loss_ref[...] += 0.5 * jnp.sum(jnp.sum(e * e, axis=-1, keepdims=True) / D, axis=0, keepdims=True)
            dx, dg = _rms_bwd(e / D, gv, xh, rstd)
            dx_ref[...] = dx
            acc_ref[0:1, :] += dg

        return pl.pallas_call(
            body, name="final_loss", grid=(n_tok,),
            in_specs=[tok(D), const((1, D)), tok(D)],
            out_specs=[tok(D), const((8, D)), const((1, LANES))],
            out_shape=[jax.ShapeDtypeStruct((S, D), F32), jax.ShapeDtypeStruct((8, D), F32),
                       jax.ShapeDtypeStruct((1, LANES), F32)],
            compiler_params=_params(("arbitrary",)))(x, g, target)


    def bwd_out(dxn, gate, y, w_out, of, om, z, lse_f, lse_m):
        def body(dx_ref, gt_ref, y_ref, w_ref, of_ref, om_ref, fg_ref, mg_ref, lf_ref, lm_ref,
                 dof_ref, dom_ref, dfg_ref, dmg_ref, dy_ref, acc_ref, sf_ref, sm_ref):
            @pl.when(pl.program_id(0) == 0)
            def _():
                acc_ref[...] = jnp.zeros_like(acc_ref)

            dxv = dx_ref[...]
            acc_ref[0:1, :] += jnp.sum(dxv * y_ref[...].astype(F32), axis=0, keepdims=True)
            dy = (gt_ref[...] * dxv).astype(BF16)
            dy_ref[...] = dy
            du = _nt(dy, w_ref[...])
            lane = _lane((TM, LANES))
            for lo, width, o_ref, g_ref, do_ref, dg_ref, l_ref, st_ref in (
                    (0, FW, of_ref, fg_ref, dof_ref, dfg_ref, lf_ref, sf_ref),
                    (FW, MW, om_ref, mg_ref, dom_ref, dmg_ref, lm_ref, sm_ref)):
                gv, ov = g_ref[...], o_ref[...]
                sg = _sigmoid(gv)
                dup = du[:, lo:lo + width]
                dob = (dup * gv * sg).astype(BF16)
                do_ref[...] = dob
                dg_ref[...] = (dup * ov * sg * (1.0 + gv * (1.0 - sg))).astype(BF16)
                d = dob.astype(F32) * ov
                for pr in range(width // LANES):
                    cols = slice(pr * LANES, (pr + 1) * LANES)
                    dp = d[:, cols]
                    d0 = jnp.sum(jnp.where(lane < HEAD_DIM, dp, 0.0), axis=-1, keepdims=True)
                    d1 = jnp.sum(jnp.where(lane >= HEAD_DIM, dp, 0.0), axis=-1, keepdims=True)
                    st_ref[:, cols] = jnp.where(lane == 2, d0, jnp.where(lane == 3, d1, l_ref[:, cols]))

        return pl.pallas_call(
            body, name="bwd_out", grid=(n_tok,),
            in_specs=[tok(D), const((1, D)), tok(D), const((FW + MW, D)), tok(FW), tok(MW),
                      tok(FW, (cfg.o_fg - ZO) // FW), tok(MW, (cfg.o_mg - ZO) // MW), tok(FW), tok(MW)],
            out_specs=[tok(FW), tok(MW), tok(FW), tok(MW), tok(D), const((8, D)), tok(FW), tok(MW)],
            out_shape=[jax.ShapeDtypeStruct((S, FW), BF16), jax.ShapeDtypeStruct((S, MW), BF16),
                       jax.ShapeDtypeStruct((S, FW), BF16), jax.ShapeDtypeStruct((S, MW), BF16),
                       jax.ShapeDtypeStruct((S, D), BF16), jax.ShapeDtypeStruct((8, D), F32),
                       jax.ShapeDtypeStruct((S, FW), F32), jax.ShapeDtypeStruct((S, MW), F32)],
            compiler_params=_params(("arbitrary",)))(dxn, gate, y, w_out, of, om, z, z, lse_f, lse_m)

    def attn_bwd(q, k, v, do, stats, q_blk0, k_blk0, v_blk0, cumt, packed, chunked, name):
        has_bias = cumt is not None
        n_pairs = (FH if packed else MH) // 2
        wq = LANES if packed else 2 * HEAD_PAD
        n_q = S // TQ
        assert TQ == TK

        def body(*refs):
            if has_bias:
                q_ref, k_ref, v_ref, do_ref, st_ref, ct_ref, dq_ref, dk_ref, dv_ref, dc_ref, dr_ref = refs
            else:
                q_ref, k_ref, v_ref, do_ref, st_ref, dq_ref, dk_ref, dv_ref = refs
            jb = pl.program_id(1)
            k0 = jb * TK

            @pl.when(jb == 0)
            def _():
                dq_ref[...] = jnp.zeros_like(dq_ref)
                if has_bias:
                    dr_ref[...] = jnp.zeros_like(dr_ref)

            dk_ref[...] = jnp.zeros_like(dk_ref)
            dv_ref[...] = jnp.zeros_like(dv_ref)
            kh = _heads(k_ref[...], packed)
            k_both = jnp.concatenate(kh, axis=0) if packed else None
            v_both = jnp.concatenate(_heads(v_ref[...], True), axis=0)
            ck = jnp.concatenate([ct_ref[0, 0][0:1, :], ct_ref[0, 0][1:2, :]], axis=1) if has_bias else None

            def products(ib):
                rows = pl.ds(pl.multiple_of(ib * TQ, TQ), TQ)
                q2, do2 = q_ref[rows, :], do_ref[rows, :]
                if packed:
                    s = _nt(q2, k_both)
                else:
                    qh = _heads(q2, False)
                    s = jnp.concatenate([_nt(qh[0], kh[0]), _nt(qh[1], kh[1])], axis=1)
                return s, _nt(do2, v_both)

            def update(ib, s, dp, carry, masked):
                q0 = pl.multiple_of(ib * TQ, TQ)
                rows = pl.ds(q0, TQ)
                q2, do2, st = q_ref[rows, :], do_ref[rows, :], st_ref[rows, :]
                if not packed:
                    qh = _heads(q2, False)
                if has_bias:
                    s = s - ck
                if masked:
                    allow = _allowed(q0, k0, chunked)
                    s = jnp.where(jnp.concatenate([allow, allow], axis=1), s, NEG)
                p = jnp.concatenate([jnp.exp(s[:, :TK] - st[:, 0:1]), jnp.exp(s[:, TK:] - st[:, 1:2])], axis=1)
                dv2 = _tn(p.astype(BF16), do2)
                ds = jnp.concatenate([p[:, :TK] * (dp[:, :TK] - st[:, 2:3]), p[:, TK:] * (dp[:, TK:] - st[:, 3:4])], axis=1)
                dsb = ds.astype(BF16)
                dv_ref[...] += _merge(dv2[:TK], dv2[TK:])
                if packed:
                    dk2 = _tn(dsb, q2)
                    dk_ref[...] += _merge(dk2[:TK], dk2[TK:])
                    dq_ref[rows, :] += _nn(dsb, k_both)
                else:
                    dk_ref[...] += jnp.concatenate([_tn(dsb[:, :TK], qh[0]), _tn(dsb[:, TK:], qh[1])], axis=1)
                    dq_ref[rows, :] += jnp.concatenate([_nn(dsb[:, :TK], kh[0]), _nn(dsb[:, TK:], kh[1])], axis=1)
                if has_bias:
                    lane = _lane((TQ, LANES))
                    r0 = jnp.sum(ds[:, :TK], axis=-1, keepdims=True)
                    r1 = jnp.sum(ds[:, TK:], axis=-1, keepdims=True)
                    dr_ref[0, rows, :] += jnp.where(lane == 0, r0, jnp.where(lane == 1, r1, 0.0))
                    return carry - jnp.sum(ds, axis=0, keepdims=True)
                return carry

            def step(ib, carry, masked):
                s, dp = products(ib)
                return update(ib, s, dp, carry, masked)

            def two(ib, carry):
                s_a, dp_a = products(ib)
                s_b, dp_b = products(ib + 1)
                return update(ib + 1, s_b, dp_b, update(ib, s_a, dp_a, carry, False), False)

            dc = step(jb, jnp.zeros((1, 2 * TK), F32), True)
            n_rest = n_q - 1 - jb
            dc = lax.fori_loop(0, n_rest // 2, lambda t, cr: two(jb + 1 + 2 * t, cr), dc)
            dc = lax.cond(n_rest % 2 == 1, lambda cr: step(n_q - 1, cr, False), lambda cr: cr, dc)
            if has_bias:
                sub = lax.broadcasted_iota(jnp.int32, (8, TK), 0)
                dc_ref[0, 0] = jnp.where(sub == 0, dc[:, :TK], jnp.where(sub == 1, dc[:, TK:], 0.0))

        in_specs = [pl.BlockSpec((S, wq), lambda p, j: (0, q_blk0 + p)),
                    pl.BlockSpec((TK, wq), lambda p, j: (j, k_blk0 + p)),
                    pl.BlockSpec((TK, LANES), lambda p, j: (j, v_blk0 + p)),
                    pl.BlockSpec((S, LANES), lambda p, j: (0, p)),
                    pl.BlockSpec((S, LANES), lambda p, j: (0, p))]
        args = [q, k, v, do, stats]
        out_specs = [pl.BlockSpec((S, wq), lambda p, j: (0, p)),
                     pl.BlockSpec((TK, wq), lambda p, j: (j, p)),
                     pl.BlockSpec((TK, LANES), lambda p, j: (j, p))]
        out_shape = [jax.ShapeDtypeStruct((S, n_pairs * wq), F32), jax.ShapeDtypeStruct((S, n_pairs * wq), F32),
                     jax.ShapeDtypeStruct((S, n_pairs * LANES), F32)]
        if has_bias:
            in_specs += [pl.BlockSpec((1, 1, 8, TK), lambda p, j: (p, j, 0, 0))]
            args += [cumt]
            out_specs += [pl.BlockSpec((1, 1, 8, TK), lambda p, j: (p, j, 0, 0)),
                          pl.BlockSpec((1, S, LANES), lambda p, j: (p, 0, 0))]
            out_shape += [jax.ShapeDtypeStruct((n_pairs, S // TK, 8, TK), F32),
                          jax.ShapeDtypeStruct((n_pairs, S, LANES), F32)]
        return pl.pallas_call(
            body, name=name, grid=(n_pairs, S // TK), in_specs=in_specs, out_specs=out_specs, out_shape=out_shape,
            compiler_params=_params(("parallel", "arbitrary")))(*args)

    def fox_post(dcum, z, bf_pad):
        def rev(i):
            return n_tok - 1 - i

        def body(dc_ref, z_ref, b_ref, dff_ref, acc_ref, carry):
            @pl.when(pl.program_id(0) == 0)
            def _():
                carry[...] = jnp.zeros_like(carry)
                acc_ref[...] = jnp.zeros_like(acc_ref)

            _, sig_neg = _log_f_terms(z_ref[...], b_ref[...])
            row = lax.broadcasted_iota(jnp.int32, (TM, TM), 0)
            col = lax.broadcasted_iota(jnp.int32, (TM, TM), 1)
            tri = (col >= row).astype(F32)
            dlog = jnp.dot(tri, dc_ref[...], precision=lax.Precision.HIGHEST, preferred_element_type=F32) + carry[...]
            carry[...] = dlog[0:1, :]
            dff = dlog * sig_neg
            dff_ref[...] = dff
            acc_ref[0:1, :] += jnp.sum(dff, axis=0, keepdims=True)

        return pl.pallas_call(
            body, name="fox_post", grid=(n_tok,),
            in_specs=[pl.BlockSpec((TM, LANES), lambda i: (rev(i), 0)),
                      pl.BlockSpec((TM, LANES), lambda i: (rev(i), misc_blk)), const((1, LANES))],
            out_specs=[pl.BlockSpec((TM, LANES), lambda i: (rev(i), 0)), const((8, LANES))],
            out_shape=[jax.ShapeDtypeStruct((S, LANES), F32), jax.ShapeDtypeStruct((8, LANES), F32)],
            scratch_shapes=[pltpu.VMEM((1, LANES), F32)],
            compiler_params=_params(("arbitrary",)))(dcum, z, bf_pad)

    def mla_post(dq, dk, dv, dff, z, gq, gkv, w_uq, w_uk, w_v, cos_t, sin_t):
        def body(dq_ref, dk_ref, dv_ref, dff_ref, ql_ref, kvl_ref, gq_ref, gkv_ref, wq_ref, wk_ref, wv_ref,
                 c_ref, s_ref, zq_ref, zkv_ref, zms_ref, dwq_ref, dwk_ref, dwv_ref, dgq_ref, dgkv_ref):
            @pl.when(pl.program_id(0) == 0)
            def _():
                for r in (dwq_ref, dwk_ref, dwv_ref, dgq_ref, dgkv_ref):
                    r[...] = jnp.zeros_like(r)

            cos1, sin1 = c_ref[...], s_ref[...]
            gqv, gkvv = gq_ref[...], gkv_ref[...]
            qn, qxh, qrstd = _rms(ql_ref[...], gqv)
            dq_pre = _rope_t(dq_ref[...] * MLA_SCALE, jnp.tile(cos1, (1, MH)), jnp.tile(sin1, (1, MH))).astype(BF16)
            dwq_ref[...] += _tn(qn.astype(BF16), dq_pre)
            dql, dgq = _rms_bwd(_nt(dq_pre, wq_ref[...]), gqv, qxh, qrstd)
            zq_ref[...] = dql.astype(BF16)
            dgq_ref[0:1, :] += dgq

            dkv = dk_ref[...]
            lane = _lane(dkv.shape) & (HEAD_PAD - 1)
            dkn = jnp.where(lane < HEAD_DIM, dkv, 0.0).astype(BF16)
            dkr = dkv[:, 0:HEAD_PAD]
            for hd in range(1, MH):
                dkr = dkr + dkv[:, hd * HEAD_PAD:(hd + 1) * HEAD_PAD]
            lane1 = _lane(dkr.shape)
            dkr = jnp.where((lane1 >= ROPE_LO) & (lane1 < ROPE_LO + ROPE_DIM), dkr, 0.0)
            dkr = _rope_t(dkr, cos1, sin1)
            zms_ref[...] = (dkr + dff_ref[...]).astype(BF16)

            kvn, kxh, krstd = _rms(kvl_ref[...], gkvv)
            kvb = kvn.astype(BF16)
            dvb = dv_ref[...].astype(BF16)
            dwk_ref[...] += _tn(kvb, dkn)
            dwv_ref[...] += _tn(kvb, dvb)
            dkvl, dgkv = _rms_bwd(_nt(dkn, wk_ref[...]) + _nt(dvb, wv_ref[...]), gkvv, kxh, krstd)
            zkv_ref[...] = dkvl.astype(BF16)
            dgkv_ref[0:1, :] += dgkv

        return pl.pallas_call(
            body, name="mla_post", grid=(n_tok,),
            in_specs=[tok(QW), tok(QW), tok(MW), tok(LANES), tok(QL, (cfg.o_ql - ZO) // QL), tok(KVL, (cfg.o_kv - ZO) // KVL),
                      const((1, QL)), const((1, KVL)), const((QL, QW)), const((KVL, QW)), const((KVL, MW)),
                      tok(LANES), tok(LANES)],
            out_specs=[tok(QL), tok(KVL), tok(LANES), const((QL, QW)), const((KVL, QW)), const((KVL, MW)),
                       const((8, QL)), const((8, KVL))],
            out_shape=[jax.ShapeDtypeStruct((S, QL), BF16), jax.ShapeDtypeStruct((S, KVL), BF16),
                       jax.ShapeDtypeStruct((S, LANES), BF16), jax.ShapeDtypeStruct((QL, QW), F32),
                       jax.ShapeDtypeStruct((KVL, QW), F32), jax.ShapeDtypeStruct((KVL, MW), F32),
                       jax.ShapeDtypeStruct((8, QL), F32), jax.ShapeDtypeStruct((8, KVL), F32)],
            compiler_params=_params(("arbitrary",)))(dq, dk, dv, dff, z, z, gq, gkv, w_uq, w_uk, w_v, cos_t, sin_t)

    def bwd_in(dz, w_in, x, dxn, g, scale):
        def body(dz_ref, w_ref, x_ref, dx_ref, g_ref, sc_ref, o_ref, acc_ref):
            @pl.when(pl.program_id(0) == 0)
            def _():
                acc_ref[...] = jnp.zeros_like(acc_ref)

            dh = _nt(dz_ref[...], w_ref[...])
            gv, mod = g_ref[...], 1.0 + sc_ref[...]
            _, xh, rstd = _rms(x_ref[...], gv)
            t = dh * xh
            acc_ref[0:1, :] += jnp.sum(dh, axis=0, keepdims=True)
            acc_ref[1:2, :] += jnp.sum(t * gv, axis=0, keepdims=True)
            acc_ref[2:3, :] += jnp.sum(t * mod, axis=0, keepdims=True)
            dx, _ = _rms_bwd(dh, gv * mod, xh, rstd)
            o_ref[...] = dx_ref[...] + dx

        return pl.pallas_call(
            body, name="bwd_in", grid=(n_tok,),
            in_specs=[tok(NZ), const((D, NZ)), tok(D), tok(D), const((1, D)), const((1, D))],
            out_specs=[tok(D), const((8, D))],
            out_shape=[jax.ShapeDtypeStruct((S, D), F32), jax.ShapeDtypeStruct((8, D), F32)],
            compiler_params=_params(("arbitrary",)))(dz, w_in, x, dxn, g, scale)


    def pair_rows(cum):
        n_pairs = FH // 2
        ct = jnp.pad(cum[:, :FH].T.reshape(n_pairs, 2, S), ((0, 0), (0, 6), (0, 0)))
        return ct.reshape(n_pairs, 8, S // TK, TK).transpose(0, 2, 1, 3)

    def bias_grad(dc, dr):
        n_pairs = FH // 2
        d = dc.transpose(0, 2, 1, 3).reshape(n_pairs, 8, S)[:, :2, :].reshape(FH, S).T
        d = d + dr[:, :, :2].transpose(1, 0, 2).reshape(S, FH)
        return jnp.pad(d, ((0, 0), (0, LANES - FH)))

    def layer_forward(x, wl, mod):
        shift, scale, gate = mod
        h, z, qkv = ln_inproj(x, wl.norm_g, scale, shift, wl.w_in)
        ct = pair_rows(fox_prep(z, wl.bf_pad))
        of, lse_f = attn_fwd(qkv, qkv, qkv, cfg.o_fq // LANES, cfg.o_fk // LANES, cfg.o_fv // LANES, ct,
                             True, False, "fox_fwd")
        qp, kp, vp = mla_prep(z, wl.gq, wl.gkv, wl.w_uq, wl.w_uk, wl.w_v, cfg.cos_t, cfg.sin_t)
        om, lse_m = attn_fwd(qp, kp, vp, 0, 0, 0, None, False, True, "mla_fwd")
        xn, u, y = gate_outproj(of, om, z, wl.w_out, x, gate)
        saved = types.SimpleNamespace(x=x, h=h, z=z, qkv=qkv, ct=ct, of=of, lse_f=lse_f, qp=qp, kp=kp, vp=vp,
                                      om=om, lse_m=lse_m, u=u, y=y)
        return xn, saved

    def layer_backward(dxn, sv, wl, mod):
        shift, scale, gate = mod
        do_f, do_m, dfg, dmg, dy, acc_o, st_f, st_m = bwd_out(dxn, gate, sv.y, wl.w_out, sv.of, sv.om, sv.z,
                                                              sv.lse_f, sv.lse_m)
        dw_out = matmul_tn(sv.u, dy, "dw_out")
        dfq, dfk, dfv, dck, dcr = attn_bwd(sv.qkv, sv.qkv, sv.qkv, do_f, st_f, cfg.o_fq // LANES, cfg.o_fk // LANES,
                                           cfg.o_fv // LANES, sv.ct, True, False, "fox_bwd")
        dff, acc_f = fox_post(bias_grad(dck, dcr), sv.z, wl.bf_pad)
        dqp, dkp, dvp = attn_bwd(sv.qp, sv.kp, sv.vp, do_m, st_m, 0, 0, 0, None, False, True, "mla_bwd")
        zq, zkv, zms, dw_uq, dw_uk, dw_v, dgq, dgkv = mla_post(
            dqp, dkp, dvp, dff, sv.z, wl.gq, wl.gkv, wl.w_uq, wl.w_uk, wl.w_v, cfg.cos_t, cfg.sin_t)
        dz = jnp.concatenate([(dfq * FOX_SCALE).astype(BF16), dfk.astype(BF16), dfv.astype(BF16), dfg, dmg, zq, zkv, zms], axis=1)
        dx, acc_i = bwd_in(dz, wl.w_in, sv.x, dxn, wl.norm_g, scale)
        dw_in = matmul_tn(sv.h, dz, "dw_in")
        grads = types.SimpleNamespace(
            w_in=dw_in, w_out=dw_out, w_uq=dw_uq, w_uk=dw_uk, w_v=dw_v, gq=dgq[0], gkv=dgkv[0],
            b_f=acc_f[0, :FH], norm_g=acc_i[2], dmod=jnp.concatenate([acc_i[0], acc_i[1], acc_o[0]]))
        return dx, grads

    return types.SimpleNamespace(layer_forward=layer_forward, layer_backward=layer_backward, final_loss=final_loss)


def _pack_rows(parts, dtype, row_multiple):
    flat = jnp.concatenate([p.reshape(-1).astype(dtype) for p in parts])
    per = SLAB_COLS * row_multiple
    total = -(-flat.shape[0] // per) * per
    return jnp.pad(flat, (0, total - flat.shape[0])).reshape(total // SLAB_COLS, SLAB_COLS)


def _unpack(flat, shapes):
    out, off = [], 0
    for shp in shapes:
        n = 1
        for d in shp:
            n *= d
        out.append(flat[off:off + n].reshape(shp))
        off += n
    return out


def kernel(x, c, positions, norm_g, w_ada, b_ada, w_in, b_f, q_norm_g, w_uq, kv_norm_g, w_ukv, w_out, final_g, loss_target, m_norm_g, m_w_ada, m_b_ada, m_w_in, m_b_f, m_q_norm_g, m_w_uq, m_kv_norm_g, m_w_ukv, m_w_out, m_final_g, v_norm_g, v_w_ada, v_b_ada, v_w_in, v_b_f, v_q_norm_g, v_w_uq, v_kv_norm_g, v_w_ukv, v_w_out, v_final_g):
    S, D = x.shape[1], x.shape[2]
    L = norm_g.shape[0]
    FH = b_f.shape[1]
    QL, KVL = q_norm_g.shape[1], kv_norm_g.shape[1]
    MH = w_ukv.shape[2] * N_CHIPS // (2 * HEAD_DIM)
    FW, MW = FH * HEAD_DIM, MH * HEAD_DIM
    NA = w_ada.shape[2]
    n_in = w_in.shape[2] * N_CHIPS
    cfg = types.SimpleNamespace(S=S, D=D, FW=FW, MW=MW, QL=QL, KVL=KVL, FH=FH, MH=MH)
    cfg.o_fq, cfg.o_fk, cfg.o_fv, cfg.o_fg, cfg.o_mg = 0, FW, 2 * FW, 3 * FW, 4 * FW
    cfg.o_ql = 4 * FW + MW
    cfg.o_kv = cfg.o_ql + QL
    cfg.o_ms = cfg.o_kv + KVL
    cfg.NZ = cfg.o_ms + LANES
    assert FW == MW and FH % 2 == 0 and MH % 2 == 0 and cfg.o_ql % QL == 0 and cfg.o_kv % KVL == 0 and KVL == LANES
    assert n_in == 4 * FW + FH + QL + KVL + ROPE_DIM + MW

    mx, my, mc = _my_pos()
    my_chip = 2 * mx + my
    my_dev = 2 * my_chip + mc

    inv_freq = 1.0 / (ROPE_THETA ** (jnp.arange(0, ROPE_DIM, 2, dtype=F32) / ROPE_DIM))
    ang = positions[0].astype(F32)[:, None] * inv_freq
    cos, sin = jnp.cos(ang), jnp.sin(ang)
    cfg.cos_t = jnp.concatenate([jnp.ones((S, ROPE_LO), F32), cos, cos, jnp.ones((S, HEAD_PAD - ROPE_LO - ROPE_DIM), F32)], axis=1)
    cfg.sin_t = jnp.concatenate([jnp.zeros((S, ROPE_LO), F32), -sin, sin, jnp.zeros((S, HEAD_PAD - ROPE_LO - ROPE_DIM), F32)], axis=1)

    shard_shapes = [w_in.shape, w_uq.shape, w_ukv.shape, w_out.shape]
    slab = _pack_rows([w_in, w_uq, w_ukv, w_out], BF16, 32)
    half = slab.shape[0] // 2
    mine = lax.dynamic_slice_in_dim(slab, mc * half, half, axis=0)
    got = chip_exchange(mine, True, "weights_chips")
    other = sibling_swap(got.reshape(N_CHIPS * half, SLAB_COLS), "weights_sibling").reshape(N_CHIPS, half, SLAB_COLS)
    lo = jnp.where(mc == 0, got, other)
    hi = jnp.where(mc == 0, other, got)
    slabs = jnp.concatenate([lo, hi], axis=1).reshape(N_CHIPS, -1)
    per_chip = [_unpack(slabs[t], shard_shapes) for t in range(N_CHIPS)]
    w_in_f = jnp.concatenate([p[0] for p in per_chip], axis=2)
    w_uq_f = jnp.concatenate([p[1] for p in per_chip], axis=2)
    w_ukv_f = jnp.concatenate([p[2] for p in per_chip], axis=2)
    w_out_f = jnp.concatenate([p[3] for p in per_chip], axis=1)

    sizes = (FW, FW, FW, FH, FW, QL, KVL, ROPE_DIM, MW)
    offs = [0]
    for sz in sizes:
        offs.append(offs[-1] + sz)
    seg = [w_in_f[:, :, offs[i]:offs[i + 1]] for i in range(len(sizes))]
    fq_w, fk_w, fv_w, ff_w, fg_w, ql_w, kvl_w, kr_w, mg_w = seg
    zeros = lambda n: jnp.zeros((L, D, n), BF16)
    w_in_p = jnp.concatenate([fq_w, fk_w, fv_w, fg_w, mg_w, ql_w, kvl_w, ff_w, zeros(ROPE_LO - FH), kr_w,
                              zeros(HEAD_PAD - ROPE_LO - ROPE_DIM)], axis=2)
    w_uq_p = jnp.pad(w_uq_f.reshape(L, QL, MH, HEAD_DIM + ROPE_DIM), ((0, 0), (0, 0), (0, 0), (0, HEAD_PAD - HEAD_DIM - ROPE_DIM)))
    w_uq_p = w_uq_p.reshape(L, QL, MH * HEAD_PAD)
    w_ukv4 = w_ukv_f.reshape(L, KVL, MH, 2 * HEAD_DIM)
    w_uk_p = jnp.pad(w_ukv4[..., :HEAD_DIM], ((0, 0), (0, 0), (0, 0), (0, HEAD_PAD - HEAD_DIM))).reshape(L, KVL, MH * HEAD_PAD)
    w_v_p = w_ukv4[..., HEAD_DIM:].reshape(L, KVL, MW)

    c_all = allgather8(c.reshape(8, D // 8), "gather_c").reshape(N_DEV, D)
    c_pad = jnp.pad(c_all, ((0, 16 - N_DEV), (0, 0)))
    mod_part = ada_forward(c_pad, w_ada)[:, :N_DEV, :]
    mod_all = allgather8(mod_part.reshape(-1, LANES), "gather_mod").reshape(N_CHIPS, 2, L, N_DEV, NA)[:, 0]
    mod_full = mod_all.transpose(1, 2, 0, 3).reshape(L, N_DEV, N_CHIPS * NA) + b_ada[:, None, :]
    mod_mine = lax.dynamic_index_in_dim(mod_full, my_dev, axis=1, keepdims=True)

    step = make_step(cfg)
    bf_pad = jnp.pad(b_f, ((0, 0), (0, LANES - FH)))
    layers, mods = [], []
    for l in range(L):
        layers.append(types.SimpleNamespace(
            norm_g=norm_g[l][None], w_in=w_in_p[l], bf_pad=bf_pad[l][None], gq=q_norm_g[l][None], gkv=kv_norm_g[l][None],
            w_uq=w_uq_p[l], w_uk=w_uk_p[l], w_v=w_v_p[l], w_out=w_out_f[l]))
        mods.append((mod_mine[l, :, :D], mod_mine[l, :, D:2 * D], mod_mine[l, :, 2 * D:]))

    xl = x[0]
    saved = []
    for l in range(L):
        xl, sv = step.layer_forward(xl, layers[l], mods[l])
        saved.append(sv)
    dx, acc_fin, loss_part = step.final_loss(xl, final_g[None], loss_target[0])
    loss = lax.psum(loss_part[0, 0], ("x", "y", "c"))
    gl = [None] * L
    for l in reversed(range(L)):
        dx, gl[l] = step.layer_backward(dx, saved[l], layers[l], mods[l])
    grad_x = dx[None]

    stack = lambda name: jnp.stack([getattr(g, name) for g in gl])
    small_parts = [stack("norm_g"), stack("dmod"), stack("b_f"), stack("gq"), stack("gkv"), acc_fin[0]]
    small_shapes = [p.shape for p in small_parts]
    small = _pack_rows(small_parts, F32, 8).reshape(-1, LANES)
    small_all = allgather8(small, "gather_small").reshape(N_DEV, -1, LANES)
    small_sum = sum_leading(small_all, "sum_small")
    g_norm_g, g_b_ada, g_b_f, g_q_norm_g, g_kv_norm_g, g_final_g = _unpack(small_sum.reshape(-1), small_shapes)

    n_ng = L * D
    dmod_all = small_all.reshape(N_DEV, -1)[:, n_ng:n_ng + L * 3 * D].reshape(N_DEV, L, 3 * D)
    dmod_cols = lax.dynamic_slice_in_dim(dmod_all, my_chip * NA, NA, axis=2).transpose(1, 0, 2)
    g_w_ada = ada_backward(c_pad, jnp.pad(dmod_cols, ((0, 0), (0, 16 - N_DEV), (0, 0))))

    dw_in_p = stack("w_in")
    ms = cfg.o_ms
    dw_in_f = jnp.concatenate([
        dw_in_p[:, :, 0:3 * FW], dw_in_p[:, :, ms:ms + FH], dw_in_p[:, :, cfg.o_fg:cfg.o_fg + FW],
        dw_in_p[:, :, cfg.o_ql:cfg.o_ql + QL + KVL], dw_in_p[:, :, ms + ROPE_LO:ms + ROPE_LO + ROPE_DIM],
        dw_in_p[:, :, cfg.o_mg:cfg.o_mg + MW]], axis=2)
    dw_uq_f = stack("w_uq").reshape(L, QL, MH, HEAD_PAD)[..., :HEAD_DIM + ROPE_DIM].reshape(L, QL, -1)
    dw_ukv_f = jnp.concatenate([stack("w_uk").reshape(L, KVL, MH, HEAD_PAD)[..., :HEAD_DIM],
                                stack("w_v").reshape(L, KVL, MH, HEAD_DIM)], axis=3).reshape(L, KVL, -1)
    dw_out_f = stack("w_out")
    dest = []
    for t in range(N_CHIPS):
        dest.append(_pack_rows([
            dw_in_f[:, :, t * w_in.shape[2]:(t + 1) * w_in.shape[2]], dw_uq_f[:, :, t * w_uq.shape[2]:(t + 1) * w_uq.shape[2]],
            dw_ukv_f[:, :, t * w_ukv.shape[2]:(t + 1) * w_ukv.shape[2]], dw_out_f[:, t * w_out.shape[1]:(t + 1) * w_out.shape[1], :]],
            F32, 32))
    gs = jnp.stack(dest)
    ghalf = gs.shape[1] // 2
    gs = gs.reshape(N_CHIPS, 2, ghalf, SLAB_COLS)
    keep = lax.dynamic_index_in_dim(gs, mc, axis=1, keepdims=False).reshape(N_CHIPS * ghalf, SLAB_COLS)
    give = lax.dynamic_index_in_dim(gs, 1 - mc, axis=1, keepdims=False).reshape(N_CHIPS * ghalf, SLAB_COLS)
    sib = sibling_swap(give, "grads_sibling")
    chip_part = add_cast(keep, sib, BF16, "grads_chip_sum").reshape(N_CHIPS, ghalf, SLAB_COLS)
    parts = chip_exchange(chip_part, False, "grads_chips")
    red_half = sum_chips(parts, "grads_sum")
    red_other = sibling_swap(red_half, "grads_back")
    red = jnp.concatenate([jnp.where(mc == 0, red_half, red_other), jnp.where(mc == 0, red_other, red_half)], axis=0)
    g_w_in, g_w_uq, g_w_ukv, g_w_out = _unpack(red.reshape(-1), shard_shapes)


    names = ["norm_g", "w_ada", "b_ada", "w_in", "b_f", "q_norm_g", "w_uq", "kv_norm_g", "w_ukv", "w_out", "final_g"]
    ws = dict(norm_g=norm_g, w_ada=w_ada, b_ada=b_ada, w_in=w_in, b_f=b_f, q_norm_g=q_norm_g, w_uq=w_uq,
              kv_norm_g=kv_norm_g, w_ukv=w_ukv, w_out=w_out, final_g=final_g)
    msd = dict(norm_g=m_norm_g, w_ada=m_w_ada, b_ada=m_b_ada, w_in=m_w_in, b_f=m_b_f, q_norm_g=m_q_norm_g, w_uq=m_w_uq,
               kv_norm_g=m_kv_norm_g, w_ukv=m_w_ukv, w_out=m_w_out, final_g=m_final_g)
    vsd = dict(norm_g=v_norm_g, w_ada=v_w_ada, b_ada=v_b_ada, w_in=v_w_in, b_f=v_b_f, q_norm_g=v_q_norm_g, w_uq=v_w_uq,
               kv_norm_g=v_kv_norm_g, w_ukv=v_w_ukv, w_out=v_w_out, final_g=v_final_g)
    gsd = dict(norm_g=g_norm_g, w_ada=g_w_ada, b_ada=g_b_ada, w_in=g_w_in, b_f=g_b_f, q_norm_g=g_q_norm_g, w_uq=g_w_uq,
               kv_norm_g=g_kv_norm_g, w_ukv=g_w_ukv, w_out=g_w_out, final_g=g_final_g)
    small_names = ["norm_g", "b_ada", "b_f", "q_norm_g", "kv_norm_g", "final_g"]
    sm_shapes = [ws[n].shape for n in small_names]
    pk = lambda d: _pack_rows([d[n] for n in small_names], F32, 8).reshape(1, -1, LANES)
    sm_out = adamw(pk(ws), pk(gsd), pk(msd), pk(vsd), "adamw_small")
    sm_d, sm_m, sm_v = [dict(zip(small_names, _unpack(o.reshape(-1), sm_shapes))) for o in sm_out]
    delta, new_m, new_v = dict(sm_d), dict(sm_m), dict(sm_v)
    for n in ["w_ada", "w_in", "w_uq", "w_ukv", "w_out"]:
        delta[n], new_m[n], new_v[n] = adamw(ws[n], gsd[n], msd[n], vsd[n], "adamw_" + n)

    return (loss, grad_x, *[gsd[n] for n in names], *[delta[n] for n in names],
            *[new_m[n] for n in names], *[new_v[n] for n in names])
```

```python
import functools
import types

import jax
import jax.numpy as jnp
from jax import lax
from jax.experimental import pallas as pl
from jax.experimental.pallas import tpu as pltpu

F32 = jnp.float32
BF16 = jnp.bfloat16
MESH = pl.DeviceIdType.MESH

N_CHIPS = 4
N_DEV = 8
HEAD_DIM = 64
ROPE_DIM = 32
ROPE_THETA = 10000.0
HEAD_PAD = 128
ROPE_LO = 64
ROPE_HALF = 16
LANES = 128
EPS = 1e-6
NEG = -1e30
ADAM_LR = 0.001
ADAM_B1 = 0.9
ADAM_B2 = 0.999
ADAM_EPS = 1e-08
ADAM_WD = 0.01
ADAM_STEP = 10
VMEM_LIMIT = 48 * 1024 * 1024
SLAB_COLS = 1024


def _params(sem=None, vmem=VMEM_LIMIT):
    return pltpu.CompilerParams(dimension_semantics=sem, vmem_limit_bytes=vmem)


def _nn(a, b):
    return jnp.dot(a, b, preferred_element_type=F32)


def _nt(a, b):
    return lax.dot_general(a, b, (((1,), (1,)), ((), ())), preferred_element_type=F32)


def _tn(a, b):
    return lax.dot_general(a, b, (((0,), (0,)), ((), ())), preferred_element_type=F32)


def _sigmoid(x):
    return 1.0 / (1.0 + jnp.exp(-x))


def _lane(shape):
    return lax.broadcasted_iota(jnp.int32, shape, len(shape) - 1)


def _pick(n, cands):
    for c in cands:
        if n % c == 0:
            return c
    return n


def _my_pos():
    return lax.axis_index("x"), lax.axis_index("y"), lax.axis_index("c")


def allgather8(xs, name):
    m_per, n = xs.shape

    def body(x_ref, out_ref, send_sems, recv_sems, local_sem):
        x, y, c = _my_pos()
        me, sibling = (x, y, c), (x, y, 1 - c)
        chips = [(1 - x, y), (x, 1 - y), (1 - x, 1 - y)]

        def rows(px, py, pc):
            return out_ref.at[pl.ds((4 * px + 2 * py + pc) * m_per, m_per), :]

        def copy(k, block, to, src=None):
            return pltpu.make_async_remote_copy(
                src_ref=rows(*block) if src is None else src, dst_ref=rows(*block),
                send_sem=send_sems.at[k], recv_sem=recv_sems.at[k], device_id=to, device_id_type=MESH)

        mine = pltpu.make_async_copy(x_ref, rows(*me), local_sem)
        mine.start()
        first = [copy(0, me, sibling, src=x_ref)]
        first += [copy(1 + j, me, (*chip, c), src=x_ref) for j, chip in enumerate(chips)]
        for cp in first:
            cp.start()
        passed = [copy(4 + j, (*chip, c), sibling) for j, chip in enumerate(chips)]
        for j, chip in enumerate(chips):
            copy(1 + j, (*chip, c), me).wait_recv()
            passed[j].start()
        copy(0, sibling, me).wait_recv()
        for j, chip in enumerate(chips):
            copy(4 + j, (*chip, 1 - c), me).wait_recv()
        for cp in first + passed:
            cp.wait_send()
        mine.wait()

    return pl.pallas_call(
        body, name=name,
        out_shape=jax.ShapeDtypeStruct((N_DEV * m_per, n), xs.dtype),
        in_specs=[pl.BlockSpec(memory_space=pltpu.VMEM)],
        out_specs=pl.BlockSpec(memory_space=pltpu.VMEM),
        scratch_shapes=[pltpu.SemaphoreType.DMA((7,)), pltpu.SemaphoreType.DMA((7,)), pltpu.SemaphoreType.DMA],
    )(xs)


def _remote(src, dst, send_sems, recv_sems, k, to):
    return pltpu.make_async_remote_copy(src_ref=src, dst_ref=dst, send_sem=send_sems.at[k], recv_sem=recv_sems.at[k],
                                        device_id=to, device_id_type=MESH)


def _hbm_call(body, name, ins, out_shapes, n_sems, n_local):
    hbm = pl.BlockSpec(memory_space=pl.ANY)
    scratch = [pltpu.SemaphoreType.DMA((n_sems,)), pltpu.SemaphoreType.DMA((n_sems,))]
    if n_local:
        scratch.append(pltpu.SemaphoreType.DMA((n_local,)))
    return pl.pallas_call(body, name=name, out_shape=out_shapes, in_specs=[hbm] * len(ins),
                          out_specs=[hbm] * len(out_shapes), scratch_shapes=scratch)(*ins)


def _layers_half(ref, h, axis=0):
    size = ref.shape[axis] // 2
    idx = (slice(None),) * axis + (pl.ds(h * size, size),)
    return ref.at[idx]


def weights_gather(ws, name):
    n = len(ws)

    def body(*refs):
        x_refs, o_refs = refs[:n], refs[n:2 * n]
        send_sems, recv_sems, local_sems = refs[2 * n:]
        x, y, c = _my_pos()
        s = 2 * x + y
        sibling = (x, y, 1 - c)
        chips = [(1 - x, y), (x, 1 - y), (1 - x, 1 - y)]
        owns, sends = [], []
        for a in range(n):
            own = pltpu.make_async_copy(x_refs[a], o_refs[a].at[s], local_sems.at[a])
            own.start()
            owns.append(own)
            for k, (tx, ty) in enumerate(chips):
                cp = _remote(_layers_half(x_refs[a], c), _layers_half(o_refs[a].at[s], c), send_sems, recv_sems,
                             6 * a + k, (tx, ty, c))
                cp.start()
                sends.append(cp)
        for a in range(n):
            for k, (tx, ty) in enumerate(chips):
                got = _layers_half(o_refs[a].at[2 * tx + ty], c)
                _remote(got, got, send_sems, recv_sems, 6 * a + k, (tx, ty, c)).wait_recv()
                cp = _remote(got, got, send_sems, recv_sems, 6 * a + 3 + k, sibling)
                cp.start()
                sends.append(cp)
        for a in range(n):
            for k, (tx, ty) in enumerate(chips):
                theirs = _layers_half(o_refs[a].at[2 * tx + ty], 1 - c)
                _remote(theirs, theirs, send_sems, recv_sems, 6 * a + 3 + k, sibling).wait_recv()
        for cp in sends:
            cp.wait_send()
        for own in owns:
            own.wait()

    outs = [jax.ShapeDtypeStruct((N_CHIPS,) + w.shape, w.dtype) for w in ws]
    return _hbm_call(body, name, ws, outs, 6 * n, n)


def halves_to_sibling(gs, name):
    n = len(gs)

    def body(*refs):
        x_refs, o_refs, send_sems, recv_sems = refs[:n], refs[n:2 * n], refs[2 * n], refs[2 * n + 1]
        x, y, c = _my_pos()
        cps = [_remote(_layers_half(x_refs[a], 1 - c, axis=1), o_refs[a], send_sems, recv_sems, a, (x, y, 1 - c))
               for a in range(n)]
        for cp in cps:
            cp.start()
        for cp in cps:
            cp.wait()

    outs = [jax.ShapeDtypeStruct((g.shape[0], g.shape[1] // 2) + g.shape[2:], g.dtype) for g in gs]
    return _hbm_call(body, name, gs, outs, n, 0)


def chip_exchange(xs, name):
    n = len(xs)

    def body(*refs):
        x_refs, o_refs = refs[:n], refs[n:2 * n]
        send_sems, recv_sems, local_sems = refs[2 * n:]
        x, y, c = _my_pos()
        s = 2 * x + y
        chips = [(1 - x, y), (x, 1 - y), (1 - x, 1 - y)]
        owns, sends = [], []
        for a in range(n):
            own = pltpu.make_async_copy(x_refs[a].at[s], o_refs[a].at[s], local_sems.at[a])
            own.start()
            owns.append(own)
            for k, (tx, ty) in enumerate(chips):
                cp = _remote(x_refs[a].at[2 * tx + ty], o_refs[a].at[s], send_sems, recv_sems, 3 * a + k, (tx, ty, c))
                cp.start()
                sends.append(cp)
        for a in range(n):
            for k, (tx, ty) in enumerate(chips):
                got = o_refs[a].at[2 * tx + ty]
                _remote(got, got, send_sems, recv_sems, 3 * a + k, (tx, ty, c)).wait_recv()
        for cp in sends:
            cp.wait_send()
        for own in owns:
            own.wait()

    outs = [jax.ShapeDtypeStruct(v.shape, v.dtype) for v in xs]
    return _hbm_call(body, name, xs, outs, 3 * n, n)


def halves_gather(xs, name):
    n = len(xs)

    def body(*refs):
        x_refs, o_refs = refs[:n], refs[n:2 * n]
        send_sems, recv_sems, local_sems = refs[2 * n:]
        x, y, c = _my_pos()
        owns, sends = [], []
        for a in range(n):
            mine = _layers_half(o_refs[a], c)
            own = pltpu.make_async_copy(x_refs[a], mine, local_sems.at[a])
            own.start()
            owns.append(own)
            cp = _remote(x_refs[a], mine, send_sems, recv_sems, a, (x, y, 1 - c))
            cp.start()
            sends.append(cp)
        for a in range(n):
            theirs = _layers_half(o_refs[a], 1 - c)
            _remote(theirs, theirs, send_sems, recv_sems, a, (x, y, 1 - c)).wait_recv()
        for cp in sends:
            cp.wait_send()
        for own in owns:
            own.wait()

    outs = [jax.ShapeDtypeStruct((2 * v.shape[0],) + v.shape[1:], v.dtype) for v in xs]
    return _hbm_call(body, name, xs, outs, n, n)


def sum_leading(xs, name):
    n, r, c = xs.shape

    def body(x_ref, o_ref):
        acc = x_ref[0]
        for i in range(1, n):
            acc = acc + x_ref[i]
        o_ref[...] = acc

    return pl.pallas_call(body, name=name, out_shape=jax.ShapeDtypeStruct((r, c), xs.dtype))(xs)


def add_cast(a, b, out_dtype, name):
    n, r, c = a.shape
    tr = _pick(r, (512, 256, 128, 64, 32, 16))

    def body(a_ref, b_ref, o_ref):
        o_ref[...] = (a_ref[...] + b_ref[...]).astype(out_dtype)

    spec = pl.BlockSpec((1, tr, c), lambda i, j: (i, j, 0))
    return pl.pallas_call(body, name=name, grid=(n, r // tr), in_specs=[spec, spec], out_specs=spec,
                          out_shape=jax.ShapeDtypeStruct((n, r, c), out_dtype),
                          compiler_params=_params(("parallel", "parallel")))(a, b)


def sum_chips(xs, name):
    n, nl, r, c = xs.shape
    tr = _pick(r, (512, 256, 128, 64, 32, 16))

    def body(x_ref, o_ref):
        acc = x_ref[0].astype(F32)
        for i in range(1, n):
            acc = acc + x_ref[i].astype(F32)
        o_ref[...] = acc

    return pl.pallas_call(body, name=name, grid=(nl, r // tr),
                          in_specs=[pl.BlockSpec((n, 1, tr, c), lambda i, j: (0, i, j, 0))],
                          out_specs=pl.BlockSpec((1, tr, c), lambda i, j: (i, j, 0)),
                          out_shape=jax.ShapeDtypeStruct((nl, r, c), F32),
                          compiler_params=_params(("parallel", "parallel")))(xs)


def ada_forward(c_all, w_ada):
    nl, d, n = w_ada.shape
    nb = c_all.shape[0]

    def body(c_ref, w_ref, o_ref):
        cv = c_ref[...]
        ca = (cv * _sigmoid(cv)).astype(BF16)
        o_ref[0] = _nn(ca, w_ref[0].astype(BF16))

    return pl.pallas_call(
        body, name="ada_forward", grid=(nl,),
        in_specs=[pl.BlockSpec((nb, d), lambda l: (0, 0)), pl.BlockSpec((1, d, n), lambda l: (l, 0, 0))],
        out_specs=pl.BlockSpec((1, nb, n), lambda l: (l, 0, 0)),
        out_shape=jax.ShapeDtypeStruct((nl, nb, n), F32), compiler_params=_params(("parallel",)))(c_all, w_ada)


def ada_backward(c_all, dmod):
    nl, nb, n = dmod.shape
    d = c_all.shape[1]

    def body(c_ref, g_ref, o_ref):
        cv = c_ref[...]
        ca = (cv * _sigmoid(cv)).astype(BF16)
        o_ref[0] = _tn(ca, g_ref[0].astype(BF16))

    return pl.pallas_call(
        body, name="ada_backward", grid=(nl,),
        in_specs=[pl.BlockSpec((nb, d), lambda l: (0, 0)), pl.BlockSpec((1, nb, n), lambda l: (l, 0, 0))],
        out_specs=pl.BlockSpec((1, d, n), lambda l: (l, 0, 0)),
        out_shape=jax.ShapeDtypeStruct((nl, d, n), F32), compiler_params=_params(("parallel",)))(c_all, dmod)


def matmul_tn(a, b, name):
    k, m = a.shape
    n = b.shape[1]
    tm, tk = _pick(m, (512, 256, 128)), _pick(k, (512, 256, 128))
    tn = n if n * (tm * 8 + tk * 4) <= VMEM_LIMIT // 2 else _pick(n, (512, 256, 128))

    def body(a_ref, b_ref, o_ref):
        @pl.when(pl.program_id(2) == 0)
        def _():
            o_ref[...] = jnp.zeros_like(o_ref)

        o_ref[...] += _tn(a_ref[...], b_ref[...])

    return pl.pallas_call(
        body, name=name, grid=(m // tm, n // tn, k // tk),
        in_specs=[pl.BlockSpec((tk, tm), lambda i, j, kk: (kk, i)), pl.BlockSpec((tk, tn), lambda i, j, kk: (kk, j))],
        out_specs=pl.BlockSpec((tm, tn), lambda i, j, kk: (i, j)),
        out_shape=jax.ShapeDtypeStruct((m, n), F32),
        compiler_params=_params(("parallel", "parallel", "arbitrary")))(a, b)


def adamw(w, g, m, v, name):
    nl, r, c = w.shape
    tr = _pick(r, (512, 256, 128, 64, 32, 16, 8))

    def body(w_ref, g_ref, m_ref, v_ref, d_ref, mo_ref, vo_ref):
        gv = g_ref[...]
        mn = ADAM_B1 * m_ref[...] + (1.0 - ADAM_B1) * gv
        vn = ADAM_B2 * v_ref[...] + (1.0 - ADAM_B2) * (gv * gv)
        m_hat = mn / (1.0 - ADAM_B1 ** ADAM_STEP)
        v_hat = vn / (1.0 - ADAM_B2 ** ADAM_STEP)
        d_ref[...] = -ADAM_LR * (m_hat / (jnp.sqrt(v_hat) + ADAM_EPS) + ADAM_WD * w_ref[...])
        mo_ref[...] = mn
        vo_ref[...] = vn

    spec = pl.BlockSpec((1, tr, c), lambda l, i: (l, i, 0))
    out = jax.ShapeDtypeStruct((nl, r, c), F32)
    return pl.pallas_call(body, name=name, grid=(nl, r // tr), in_specs=[spec] * 4, out_specs=[spec] * 3,
                          out_shape=[out] * 3, compiler_params=_params(("parallel", "parallel")))(w, g, m, v)


def _rope(t, cos_t, sin_t):
    w = t.shape[1]
    lane = _lane(t.shape) & (HEAD_PAD - 1)
    first_half = (lane >= ROPE_LO) & (lane < ROPE_LO + ROPE_HALF)
    partner = jnp.where(first_half, pltpu.roll(t, w - ROPE_HALF, 1), pltpu.roll(t, ROPE_HALF, 1))
    return t * cos_t + partner * sin_t


def _rope_t(dt, cos_t, sin_t):
    w = dt.shape[1]
    lane = _lane(dt.shape) & (HEAD_PAD - 1)
    first_half = (lane >= ROPE_LO) & (lane < ROPE_LO + ROPE_HALF)
    ds = dt * sin_t
    partner = jnp.where(first_half, pltpu.roll(ds, w - ROPE_HALF, 1), pltpu.roll(ds, ROPE_HALF, 1))
    return dt * cos_t + partner


def _rms(xv, g):
    rstd = lax.rsqrt(jnp.mean(xv * xv, axis=-1, keepdims=True) + EPS)
    xh = xv * rstd
    return xh * g, xh, rstd


def _rms_bwd(dy, g, xh, rstd):
    dxh = dy * g
    dx = rstd * (dxh - xh * jnp.mean(dxh * xh, axis=-1, keepdims=True))
    return dx, jnp.sum(dy * xh, axis=0, keepdims=True)


def make_step(cfg):
    S, D, NZ = cfg.S, cfg.D, cfg.NZ
    FW, MW, QL, KVL, FH, MH = cfg.FW, cfg.MW, cfg.QL, cfg.KVL, cfg.FH, cfg.MH
    QW = MH * HEAD_PAD
    TM = _pick(S, (256, 128))
    TQ = TK = _pick(S, (256, 128))
    n_tok = S // TM
    ZO = 3 * FW
    NZR = NZ - ZO
    misc_blk = (cfg.o_ms - ZO) // LANES
    FOX_SCALE = HEAD_DIM ** -0.5
    MLA_SCALE = (HEAD_DIM + ROPE_DIM) ** -0.5

    def tok(width, col=0):
        return pl.BlockSpec((TM, width), lambda i: (i, col))

    def const(shape):
        return pl.BlockSpec(shape, lambda i: tuple(0 for _ in shape))


    def ln_inproj(x, g, scale, shift, w_in):
        def body(x_ref, g_ref, sc_ref, sh_ref, w_ref, h_ref, z_ref, qkv_ref):
            y, _, _ = _rms(x_ref[...], g_ref[...])
            hb = (y * (1.0 + sc_ref[...]) + sh_ref[...]).astype(BF16)
            h_ref[...] = hb
            z = _nn(hb, w_ref[...])
            z_ref[...] = z[:, ZO:]
            qkv_ref[:, :FW] = (z[:, :FW] * FOX_SCALE).astype(BF16)
            qkv_ref[:, FW:] = z[:, FW:ZO].astype(BF16)

        return pl.pallas_call(
            body, name="ln_inproj", grid=(n_tok,),
            in_specs=[tok(D), const((1, D)), const((1, D)), const((1, D)), const((D, NZ))],
            out_specs=[tok(D), tok(NZR), tok(ZO)],
            out_shape=[jax.ShapeDtypeStruct((S, D), BF16), jax.ShapeDtypeStruct((S, NZR), F32),
                       jax.ShapeDtypeStruct((S, ZO), BF16)],
            compiler_params=_params(("parallel",)))(x, g, scale, shift, w_in)

    def _log_f_terms(misc, bf):
        lane = _lane(misc.shape)
        a = misc + bf
        e = jnp.exp(-jnp.abs(a))
        logf = jnp.minimum(a, 0.0) - jnp.log(1.0 + e)
        sig_neg = jnp.where(a >= 0, e, 1.0) / (1.0 + e)
        valid = lane < FH
        return jnp.where(valid, logf, 0.0), jnp.where(valid, sig_neg, 0.0)

    def fox_prep(z, bf_pad):
        def body(z_ref, b_ref, o_ref, carry):
            @pl.when(pl.program_id(0) == 0)
            def _():
                carry[...] = jnp.zeros_like(carry)

            logf, _ = _log_f_terms(z_ref[...], b_ref[...])
            row = lax.broadcasted_iota(jnp.int32, (TM, TM), 0)
            col = lax.broadcasted_iota(jnp.int32, (TM, TM), 1)
            tri = (col <= row).astype(F32)
            cum = jnp.dot(tri, logf, precision=lax.Precision.HIGHEST, preferred_element_type=F32) + carry[...]
            o_ref[...] = cum
            carry[...] = cum[TM - 1:TM, :]

        return pl.pallas_call(
            body, name="fox_prep", grid=(n_tok,),
            in_specs=[tok(LANES, misc_blk), const((1, LANES))], out_specs=tok(LANES),
            out_shape=jax.ShapeDtypeStruct((S, LANES), F32),
            scratch_shapes=[pltpu.VMEM((1, LANES), F32)],
            compiler_params=_params(("arbitrary",)))(z, bf_pad)

    def mla_prep(z, gq, gkv, w_uq, w_uk, w_v, cos_t, sin_t):
        def body(ql_ref, kvl_ref, ms_ref, gq_ref, gkv_ref, wq_ref, wk_ref, wv_ref, c_ref, s_ref, q_ref, k_ref, v_ref):
            cos1, sin1 = c_ref[...], s_ref[...]
            qn, _, _ = _rms(ql_ref[...], gq_ref[...])
            q = _nn(qn.astype(BF16), wq_ref[...])
            q_ref[...] = (_rope(q, jnp.tile(cos1, (1, MH)), jnp.tile(sin1, (1, MH))) * MLA_SCALE).astype(BF16)
            kvn, _, _ = _rms(kvl_ref[...], gkv_ref[...])
            kvb = kvn.astype(BF16)
            lane = _lane((TM, LANES))
            kr = jnp.where((lane >= ROPE_LO) & (lane < ROPE_LO + ROPE_DIM), ms_ref[...], 0.0)
            kr = _rope(kr, cos1, sin1)
            k_ref[...] = (_nn(kvb, wk_ref[...]) + jnp.tile(kr, (1, MH))).astype(BF16)
            v_ref[...] = _nn(kvb, wv_ref[...]).astype(BF16)

        return pl.pallas_call(
            body, name="mla_prep", grid=(n_tok,),
            in_specs=[tok(QL, (cfg.o_ql - ZO) // QL), tok(KVL, (cfg.o_kv - ZO) // KVL), tok(LANES, misc_blk),
                      const((1, QL)), const((1, KVL)), const((QL, QW)), const((KVL, QW)), const((KVL, MW)),
                      tok(LANES), tok(LANES)],
            out_specs=[tok(QW), tok(QW), tok(MW)],
            out_shape=[jax.ShapeDtypeStruct((S, QW), BF16), jax.ShapeDtypeStruct((S, QW), BF16),
                       jax.ShapeDtypeStruct((S, MW), BF16)],
            compiler_params=_params(("parallel",)))(z, z, z, gq, gkv, w_uq, w_uk, w_v, cos_t, sin_t)

    def _allowed(q0, k0, chunked):
        qi = q0 + lax.broadcasted_iota(jnp.int32, (TQ, TK), 0)
        ki = k0 + lax.broadcasted_iota(jnp.int32, (TQ, TK), 1)
        if chunked:
            return (ki >> 6) <= (qi >> 6)
        return ki <= qi

    def _heads(val, packed):
        if packed:
            lane = _lane(val.shape)
            zero = jnp.zeros_like(val)
            return [jnp.where(lane < HEAD_DIM, val, zero), jnp.where(lane >= HEAD_DIM, val, zero)]
        return [val[:, :HEAD_PAD], val[:, HEAD_PAD:]]

    def _merge(a0, a1):
        return jnp.where(_lane(a0.shape) < HEAD_DIM, a0, a1)

    def attn_fwd(q, k, v, q_blk0, k_blk0, v_blk0, cumt, packed, chunked, name):
        has_bias = cumt is not None
        n_pairs = (FH if packed else MH) // 2
        wq = LANES if packed else 2 * HEAD_PAD
        assert TQ == TK

        def body(*refs):
            if has_bias:
                q_ref, k_ref, v_ref, ct_ref, o_ref, lse_ref = refs
            else:
                q_ref, k_ref, v_ref, o_ref, lse_ref = refs
            i = pl.program_id(1)
            q0 = i * TQ
            qh = _heads(q_ref[...], packed)
            q_both = jnp.concatenate(qh, axis=0) if packed else None

            def scores(kb):
                kblk = k_ref[pl.ds(pl.multiple_of(kb * TK, TK), TK), :]
                if packed:
                    return _nt(q_both, kblk)
                return jnp.concatenate([_nt(qh[0], kblk[:, :HEAD_PAD]), _nt(qh[1], kblk[:, HEAD_PAD:])], axis=0)

            def update(kb, s, carry, masked):
                m, l, acc = carry
                k0 = pl.multiple_of(kb * TK, TK)
                s0, s1 = s[:TQ], s[TQ:]
                if has_bias:
                    ck = ct_ref[0, kb]
                    s0, s1 = s0 - ck[0:1, :], s1 - ck[1:2, :]
                if masked:
                    allow = _allowed(q0, k0, chunked)
                    s0, s1 = jnp.where(allow, s0, NEG), jnp.where(allow, s1, NEG)
                s = jnp.concatenate([s0, s1], axis=0)
                m_new = jnp.maximum(m, jnp.max(s, axis=-1, keepdims=True))
                p = jnp.exp(s - m_new)
                alpha = jnp.exp(m - m_new)
                l = alpha * l + jnp.sum(p, axis=-1, keepdims=True)
                acc = alpha * acc + _nn(p.astype(BF16), v_ref[pl.ds(k0, TK), :])
                return m_new, l, acc

            def two(kb, carry, last_masked):
                s_a, s_b = scores(kb), scores(kb + 1)
                return update(kb + 1, s_b, update(kb, s_a, carry, False), last_masked)

            init = (jnp.full((2 * TQ, 1), NEG, F32), jnp.zeros((2 * TQ, 1), F32), jnp.zeros((2 * TQ, LANES), F32))
            carry = lax.fori_loop(0, i // 2, lambda t, cr: two(2 * t, cr, False), init)
            m, l, acc = lax.cond(i % 2 == 1, lambda cr: two(i - 1, cr, True),
                                 lambda cr: update(i, scores(i), cr, True), carry)
            o = acc / l
            lse = m + jnp.log(l)
            lane = _lane((TQ, LANES))
            o_ref[...] = _merge(o[:TQ], o[TQ:])
            lse_ref[...] = jnp.where(lane == 0, lse[:TQ], jnp.where(lane == 1, lse[TQ:], 0.0))

        in_specs = [pl.BlockSpec((TQ, wq), lambda p, i: (i, q_blk0 + p)),
                    pl.BlockSpec((S, wq), lambda p, i: (0, k_blk0 + p)),
                    pl.BlockSpec((S, LANES), lambda p, i: (0, v_blk0 + p))]
        args = [q, k, v]
        if has_bias:
            in_specs += [pl.BlockSpec((1, S // TK, 8, TK), lambda p, i: (p, 0, 0, 0))]
            args += [cumt]
        return pl.pallas_call(
            body, name=name, grid=(n_pairs, S // TQ), in_specs=in_specs,
            out_specs=[pl.BlockSpec((TQ, LANES), lambda p, i: (i, p)),
                       pl.BlockSpec((TQ, LANES), lambda p, i: (i, p))],
            out_shape=[jax.ShapeDtypeStruct((S, n_pairs * LANES), F32),
                       jax.ShapeDtypeStruct((S, n_pairs * LANES), F32)],
            compiler_params=_params(("parallel", "parallel")))(*args)

    def gate_outproj(of, om, z, w_out, x, gate):
        def body(of_ref, om_ref, fg_ref, mg_ref, w_ref, x_ref, gt_ref, xn_ref, u_ref, y_ref):
            fg, mg = fg_ref[...], mg_ref[...]
            u = jnp.concatenate([of_ref[...] * fg * _sigmoid(fg), om_ref[...] * mg * _sigmoid(mg)], axis=1).astype(BF16)
            y = _nn(u, w_ref[...])
            u_ref[...] = u
            y_ref[...] = y.astype(BF16)
            xn_ref[...] = x_ref[...] + gt_ref[...] * y

        return pl.pallas_call(
            body, name="gate_outproj", grid=(n_tok,),
            in_specs=[tok(FW), tok(MW), tok(FW, (cfg.o_fg - ZO) // FW), tok(MW, (cfg.o_mg - ZO) // MW),
                      const((FW + MW, D)), tok(D), const((1, D))],
            out_specs=[tok(D), tok(FW + MW), tok(D)],
            out_shape=[jax.ShapeDtypeStruct((S, D), F32), jax.ShapeDtypeStruct((S, FW + MW), BF16),
                       jax.ShapeDtypeStruct((S, D), BF16)],
            compiler_params=_params(("parallel",)))(of, om, z, z, w_out, x, gate)

    def final_loss(x, g, target):
        def body(x_ref, g_ref, t_ref, dx_ref, acc_ref, loss_ref):
            @pl.when(pl.program_id(0) == 0)
            def _():
                acc_ref[...] = jnp.zeros_like(acc_ref)
                loss_ref[...] = jnp.zeros_like(loss_ref)

            gv = g_ref[...]
            y, xh, rstd = _rms(x_ref[...], gv)
            e = y - t_ref[...]
            loss_ref[...] += 0.5 * jnp.sum(jnp.sum(e * e, axis=-1, keepdims=True) / D, axis=0, keepdims=True)
            dx, dg = _rms_bwd(e / D, gv, xh, rstd)
            dx_ref[...] = dx
            acc_ref[0:1, :] += dg

        return pl.pallas_call(
            body, name="final_loss", grid=(n_tok,),
            in_specs=[tok(D), const((1, D)), tok(D)],
            out_specs=[tok(D), const((8, D)), const((1, LANES))],
            out_shape=[jax.ShapeDtypeStruct((S, D), F32), jax.ShapeDtypeStruct((8, D), F32),
                       jax.ShapeDtypeStruct((1, LANES), F32)],
            compiler_params=_params(("arbitrary",)))(x, g, target)


    def bwd_out(dxn, gate, y, w_out, of, om, z, lse_f, lse_m):
        def body(dx_ref, gt_ref, y_ref, w_ref, of_ref, om_ref, fg_ref, mg_ref, lf_ref, lm_ref,
                 dof_ref, dom_ref, dfg_ref, dmg_ref, dy_ref, acc_ref, sf_ref, sm_ref):
            @pl.when(pl.program_id(0) == 0)
            def _():
                acc_ref[...] = jnp.zeros_like(acc_ref)

            dxv = dx_ref[...]
            acc_ref[0:1, :] += jnp.sum(dxv * y_ref[...].astype(F32), axis=0, keepdims=True)
            dy = (gt_ref[...] * dxv).astype(BF16)
            dy_ref[...] = dy
            du = _nt(dy, w_ref[...])
            lane = _lane((TM, LANES))
            for lo, width, o_ref, g_ref, do_ref, dg_ref, l_ref, st_ref in (
                    (0, FW, of_ref, fg_ref, dof_ref, dfg_ref, lf_ref, sf_ref),
                    (FW, MW, om_ref, mg_ref, dom_ref, dmg_ref, lm_ref, sm_ref)):
                gv, ov = g_ref[...], o_ref[...]
                sg = _sigmoid(gv)
                dup = du[:, lo:lo + width]
                dob = (dup * gv * sg).astype(BF16)
                do_ref[...] = dob
                dg_ref[...] = (dup * ov * sg * (1.0 + gv * (1.0 - sg))).astype(BF16)
                d = dob.astype(F32) * ov
                for pr in range(width // LANES):
                    cols = slice(pr * LANES, (pr + 1) * LANES)
                    dp = d[:, cols]
                    d0 = jnp.sum(jnp.where(lane < HEAD_DIM, dp, 0.0), axis=-1, keepdims=True)
                    d1 = jnp.sum(jnp.where(lane >= HEAD_DIM, dp, 0.0), axis=-1, keepdims=True)
                    st_ref[:, cols] = jnp.where(lane == 2, d0, jnp.where(lane == 3, d1, l_ref[:, cols]))

        return pl.pallas_call(
            body, name="bwd_out", grid=(n_tok,),
            in_specs=[tok(D), const((1, D)), tok(D), const((FW + MW, D)), tok(FW), tok(MW),
                      tok(FW, (cfg.o_fg - ZO) // FW), tok(MW, (cfg.o_mg - ZO) // MW), tok(FW), tok(MW)],
            out_specs=[tok(FW), tok(MW), tok(FW), tok(MW), tok(D), const((8, D)), tok(FW), tok(MW)],
            out_shape=[jax.ShapeDtypeStruct((S, FW), BF16), jax.ShapeDtypeStruct((S, MW), BF16),
                       jax.ShapeDtypeStruct((S, FW), BF16), jax.ShapeDtypeStruct((S, MW), BF16),
                       jax.ShapeDtypeStruct((S, D), BF16), jax.ShapeDtypeStruct((8, D), F32),
                       jax.ShapeDtypeStruct((S, FW), F32), jax.ShapeDtypeStruct((S, MW), F32)],
            compiler_params=_params(("arbitrary",)))(dxn, gate, y, w_out, of, om, z, z, lse_f, lse_m)

    def attn_bwd(q, k, v, do, stats, q_blk0, k_blk0, v_blk0, cumt, packed, chunked, name):
        has_bias = cumt is not None
        n_pairs = (FH if packed else MH) // 2
        wq = LANES if packed else 2 * HEAD_PAD
        n_q = S // TQ
        assert TQ == TK

        def body(*refs):
            if has_bias:
                q_ref, k_ref, v_ref, do_ref, st_ref, ct_ref, dq_ref, dk_ref, dv_ref, dc_ref, dr_ref = refs
            else:
                q_ref, k_ref, v_ref, do_ref, st_ref, dq_ref, dk_ref, dv_ref = refs
            jb = pl.program_id(1)
            k0 = jb * TK

            @pl.when(jb == 0)
            def _():
                dq_ref[...] = jnp.zeros_like(dq_ref)
                if has_bias:
                    dr_ref[...] = jnp.zeros_like(dr_ref)

            dk_ref[...] = jnp.zeros_like(dk_ref)
            dv_ref[...] = jnp.zeros_like(dv_ref)
            kh = _heads(k_ref[...], packed)
            k_both = jnp.concatenate(kh, axis=0) if packed else None
            v_both = jnp.concatenate(_heads(v_ref[...], True), axis=0)
            ck = jnp.concatenate([ct_ref[0, 0][0:1, :], ct_ref[0, 0][1:2, :]], axis=1) if has_bias else None

            def products(ib):
                rows = pl.ds(pl.multiple_of(ib * TQ, TQ), TQ)
                q2, do2 = q_ref[rows, :], do_ref[rows, :]
                if packed:
                    s = _nt(q2, k_both)
                else:
                    qh = _heads(q2, False)
                    s = jnp.concatenate([_nt(qh[0], kh[0]), _nt(qh[1], kh[1])], axis=1)
                return s, _nt(do2, v_both)

            def update(ib, s, dp, carry, masked):
                q0 = pl.multiple_of(ib * TQ, TQ)
                rows = pl.ds(q0, TQ)
                q2, do2, st = q_ref[rows, :], do_ref[rows, :], st_ref[rows, :]
                if not packed:
                    qh = _heads(q2, False)
                if has_bias:
                    s = s - ck
                if masked:
                    allow = _allowed(q0, k0, chunked)
                    s = jnp.where(jnp.concatenate([allow, allow], axis=1), s, NEG)
                p = jnp.concatenate([jnp.exp(s[:, :TK] - st[:, 0:1]), jnp.exp(s[:, TK:] - st[:, 1:2])], axis=1)
                dv2 = _tn(p.astype(BF16), do2)
                ds = jnp.concatenate([p[:, :TK] * (dp[:, :TK] - st[:, 2:3]), p[:, TK:] * (dp[:, TK:] - st[:, 3:4])], axis=1)
                dsb = ds.astype(BF16)
                dv_ref[...] += _merge(dv2[:TK], dv2[TK:])
                if packed:
                    dk2 = _tn(dsb, q2)
                    dk_ref[...] += _merge(dk2[:TK], dk2[TK:])
                    dq_ref[rows, :] += _nn(dsb, k_both)
                else:
                    dk_ref[...] += jnp.concatenate([_tn(dsb[:, :TK], qh[0]), _tn(dsb[:, TK:], qh[1])], axis=1)
                    dq_ref[rows, :] += jnp.concatenate([_nn(dsb[:, :TK], kh[0]), _nn(dsb[:, TK:], kh[1])], axis=1)
                if has_bias:
                    lane = _lane((TQ, LANES))
                    r0 = jnp.sum(ds[:, :TK], axis=-1, keepdims=True)
                    r1 = jnp.sum(ds[:, TK:], axis=-1, keepdims=True)
                    dr_ref[0, rows, :] += jnp.where(lane == 0, r0, jnp.where(lane == 1, r1, 0.0))
                    return carry - jnp.sum(ds, axis=0, keepdims=True)
                return carry

            def step(ib, carry, masked):
                s, dp = products(ib)
                return update(ib, s, dp, carry, masked)

            def two(ib, carry):
                s_a, dp_a = products(ib)
                s_b, dp_b = products(ib + 1)
                return update(ib + 1, s_b, dp_b, update(ib, s_a, dp_a, carry, False), False)

            dc = step(jb, jnp.zeros((1, 2 * TK), F32), True)
            n_rest = n_q - 1 - jb
            dc = lax.fori_loop(0, n_rest // 2, lambda t, cr: two(jb + 1 + 2 * t, cr), dc)
            dc = lax.cond(n_rest % 2 == 1, lambda cr: step(n_q - 1, cr, False), lambda cr: cr, dc)
            if has_bias:
                sub = lax.broadcasted_iota(jnp.int32, (8, TK), 0)
                dc_ref[0, 0] = jnp.where(sub == 0, dc[:, :TK], jnp.where(sub == 1, dc[:, TK:], 0.0))

        in_specs = [pl.BlockSpec((S, wq), lambda p, j: (0, q_blk0 + p)),
                    pl.BlockSpec((TK, wq), lambda p, j: (j, k_blk0 + p)),
                    pl.BlockSpec((TK, LANES), lambda p, j: (j, v_blk0 + p)),
                    pl.BlockSpec((S, LANES), lambda p, j: (0, p)),
                    pl.BlockSpec((S, LANES), lambda p, j: (0, p))]
        args = [q, k, v, do, stats]
        out_specs = [pl.BlockSpec((S, wq), lambda p, j: (0, p)),
                     pl.BlockSpec((TK, wq), lambda p, j: (j, p)),
                     pl.BlockSpec((TK, LANES), lambda p, j: (j, p))]
        out_shape = [jax.ShapeDtypeStruct((S, n_pairs * wq), F32), jax.ShapeDtypeStruct((S, n_pairs * wq), F32),
                     jax.ShapeDtypeStruct((S, n_pairs * LANES), F32)]
        if has_bias:
            in_specs += [pl.BlockSpec((1, 1, 8, TK), lambda p, j: (p, j, 0, 0))]
            args += [cumt]
            out_specs += [pl.BlockSpec((1, 1, 8, TK), lambda p, j: (p, j, 0, 0)),
                          pl.BlockSpec((1, S, LANES), lambda p, j: (p, 0, 0))]
            out_shape += [jax.ShapeDtypeStruct((n_pairs, S // TK, 8, TK), F32),
                          jax.ShapeDtypeStruct((n_pairs, S, LANES), F32)]
        return pl.pallas_call(
            body, name=name, grid=(n_pairs, S // TK), in_specs=in_specs, out_specs=out_specs, out_shape=out_shape,
            compiler_params=_params(("parallel", "arbitrary")))(*args)

    def fox_post(dcum, z, bf_pad):
        def rev(i):
            return n_tok - 1 - i

        def body(dc_ref, z_ref, b_ref, dff_ref, acc_ref, carry):
            @pl.when(pl.program_id(0) == 0)
            def _():
                carry[...] = jnp.zeros_like(carry)
                acc_ref[...] = jnp.zeros_like(acc_ref)

            _, sig_neg = _log_f_terms(z_ref[...], b_ref[...])
            row = lax.broadcasted_iota(jnp.int32, (TM, TM), 0)
            col = lax.broadcasted_iota(jnp.int32, (TM, TM), 1)
            tri = (col >= row).astype(F32)
            dlog = jnp.dot(tri, dc_ref[...], precision=lax.Precision.HIGHEST, preferred_element_type=F32) + carry[...]
            carry[...] = dlog[0:1, :]
            dff = dlog * sig_neg
            dff_ref[...] = dff
            acc_ref[0:1, :] += jnp.sum(dff, axis=0, keepdims=True)

        return pl.pallas_call(
            body, name="fox_post", grid=(n_tok,),
            in_specs=[pl.BlockSpec((TM, LANES), lambda i: (rev(i), 0)),
                      pl.BlockSpec((TM, LANES), lambda i: (rev(i), misc_blk)), const((1, LANES))],
            out_specs=[pl.BlockSpec((TM, LANES), lambda i: (rev(i), 0)), const((8, LANES))],
            out_shape=[jax.ShapeDtypeStruct((S, LANES), F32), jax.ShapeDtypeStruct((8, LANES), F32)],
            scratch_shapes=[pltpu.VMEM((1, LANES), F32)],
            compiler_params=_params(("arbitrary",)))(dcum, z, bf_pad)

    def mla_post(dq, dk, dv, dff, z, gq, gkv, w_uq, w_uk, w_v, cos_t, sin_t):
        def body(dq_ref, dk_ref, dv_ref, dff_ref, ql_ref, kvl_ref, gq_ref, gkv_ref, wq_ref, wk_ref, wv_ref,
                 c_ref, s_ref, zq_ref, zkv_ref, zms_ref, dwq_ref, dwk_ref, dwv_ref, dgq_ref, dgkv_ref):
            @pl.when(pl.program_id(0) == 0)
            def _():
                for r in (dwq_ref, dwk_ref, dwv_ref, dgq_ref, dgkv_ref):
                    r[...] = jnp.zeros_like(r)

            cos1, sin1 = c_ref[...], s_ref[...]
            gqv, gkvv = gq_ref[...], gkv_ref[...]
            qn, qxh, qrstd = _rms(ql_ref[...], gqv)
            dq_pre = _rope_t(dq_ref[...] * MLA_SCALE, jnp.tile(cos1, (1, MH)), jnp.tile(sin1, (1, MH))).astype(BF16)
            dwq_ref[...] += _tn(qn.astype(BF16), dq_pre)
            dql, dgq = _rms_bwd(_nt(dq_pre, wq_ref[...]), gqv, qxh, qrstd)
            zq_ref[...] = dql.astype(BF16)
            dgq_ref[0:1, :] += dgq

            dkv = dk_ref[...]
            lane = _lane(dkv.shape) & (HEAD_PAD - 1)
            dkn = jnp.where(lane < HEAD_DIM, dkv, 0.0).astype(BF16)
            dkr = dkv[:, 0:HEAD_PAD]
            for hd in range(1, MH):
                dkr = dkr + dkv[:, hd * HEAD_PAD:(hd + 1) * HEAD_PAD]
            lane1 = _lane(dkr.shape)
            dkr = jnp.where((lane1 >= ROPE_LO) & (lane1 < ROPE_LO + ROPE_DIM), dkr, 0.0)
            dkr = _rope_t(dkr, cos1, sin1)
            zms_ref[...] = (dkr + dff_ref[...]).astype(BF16)

            kvn, kxh, krstd = _rms(kvl_ref[...], gkvv)
            kvb = kvn.astype(BF16)
            dvb = dv_ref[...].astype(BF16)
            dwk_ref[...] += _tn(kvb, dkn)
            dwv_ref[...] += _tn(kvb, dvb)
            dkvl, dgkv = _rms_bwd(_nt(dkn, wk_ref[...]) + _nt(dvb, wv_ref[...]), gkvv, kxh, krstd)
            zkv_ref[...] = dkvl.astype(BF16)
            dgkv_ref[0:1, :] += dgkv

        return pl.pallas_call(
            body, name="mla_post", grid=(n_tok,),
            in_specs=[tok(QW), tok(QW), tok(MW), tok(LANES), tok(QL, (cfg.o_ql - ZO) // QL), tok(KVL, (cfg.o_kv - ZO) // KVL),
                      const((1, QL)), const((1, KVL)), const((QL, QW)), const((KVL, QW)), const((KVL, MW)),
                      tok(LANES), tok(LANES)],
            out_specs=[tok(QL), tok(KVL), tok(LANES), const((QL, QW)), const((KVL, QW)), const((KVL, MW)),
                       const((8, QL)), const((8, KVL))],
            out_shape=[jax.ShapeDtypeStruct((S, QL), BF16), jax.ShapeDtypeStruct((S, KVL), BF16),
                       jax.ShapeDtypeStruct((S, LANES), BF16), jax.ShapeDtypeStruct((QL, QW), F32),
                       jax.ShapeDtypeStruct((KVL, QW), F32), jax.ShapeDtypeStruct((KVL, MW), F32),
                       jax.ShapeDtypeStruct((8, QL), F32), jax.ShapeDtypeStruct((8, KVL), F32)],
            compiler_params=_params(("arbitrary",)))(dq, dk, dv, dff, z, z, gq, gkv, w_uq, w_uk, w_v, cos_t, sin_t)

    def bwd_in(dz, w_in, x, dxn, g, scale):
        def body(dz_ref, w_ref, x_ref, dx_ref, g_ref, sc_ref, o_ref, acc_ref):
            @pl.when(pl.program_id(0) == 0)
            def _():
                acc_ref[...] = jnp.zeros_like(acc_ref)

            dh = _nt(dz_ref[...], w_ref[...])
            gv, mod = g_ref[...], 1.0 + sc_ref[...]
            _, xh, rstd = _rms(x_ref[...], gv)
            t = dh * xh
            acc_ref[0:1, :] += jnp.sum(dh, axis=0, keepdims=True)
            acc_ref[1:2, :] += jnp.sum(t * gv, axis=0, keepdims=True)
            acc_ref[2:3, :] += jnp.sum(t * mod, axis=0, keepdims=True)
            dx, _ = _rms_bwd(dh, gv * mod, xh, rstd)
            o_ref[...] = dx_ref[...] + dx

        return pl.pallas_call(
            body, name="bwd_in", grid=(n_tok,),
            in_specs=[tok(NZ), const((D, NZ)), tok(D), tok(D), const((1, D)), const((1, D))],
            out_specs=[tok(D), const((8, D))],
            out_shape=[jax.ShapeDtypeStruct((S, D), F32), jax.ShapeDtypeStruct((8, D), F32)],
            compiler_params=_params(("arbitrary",)))(dz, w_in, x, dxn, g, scale)


    def pair_rows(cum):
        n_pairs = FH // 2
        ct = jnp.pad(cum[:, :FH].T.reshape(n_pairs, 2, S), ((0, 0), (0, 6), (0, 0)))
        return ct.reshape(n_pairs, 8, S // TK, TK).transpose(0, 2, 1, 3)

    def bias_grad(dc, dr):
        n_pairs = FH // 2
        d = dc.transpose(0, 2, 1, 3).reshape(n_pairs, 8, S)[:, :2, :].reshape(FH, S).T
        d = d + dr[:, :, :2].transpose(1, 0, 2).reshape(S, FH)
        return jnp.pad(d, ((0, 0), (0, LANES - FH)))

    def layer_forward(x, wl, mod):
        shift, scale, gate = mod
        h, z, qkv = ln_inproj(x, wl.norm_g, scale, shift, wl.w_in)
        ct = pair_rows(fox_prep(z, wl.bf_pad))
        of, lse_f = attn_fwd(qkv, qkv, qkv, cfg.o_fq // LANES, cfg.o_fk // LANES, cfg.o_fv // LANES, ct,
                             True, False, "fox_fwd")
        qp, kp, vp = mla_prep(z, wl.gq, wl.gkv, wl.w_uq, wl.w_uk, wl.w_v, cfg.cos_t, cfg.sin_t)
        om, lse_m = attn_fwd(qp, kp, vp, 0, 0, 0, None, False, True, "mla_fwd")
        xn, u, y = gate_outproj(of, om, z, wl.w_out, x, gate)
        saved = types.SimpleNamespace(x=x, h=h, z=z, qkv=qkv, ct=ct, of=of, lse_f=lse_f, qp=qp, kp=kp, vp=vp,
                                      om=om, lse_m=lse_m, u=u, y=y)
        return xn, saved

    def layer_backward(dxn, sv, wl, mod):
        shift, scale, gate = mod
        do_f, do_m, dfg, dmg, dy, acc_o, st_f, st_m = bwd_out(dxn, gate, sv.y, wl.w_out, sv.of, sv.om, sv.z,
                                                              sv.lse_f, sv.lse_m)
        dw_out = matmul_tn(sv.u, dy, "dw_out")
        dfq, dfk, dfv, dck, dcr = attn_bwd(sv.qkv, sv.qkv, sv.qkv, do_f, st_f, cfg.o_fq // LANES, cfg.o_fk // LANES,
                                           cfg.o_fv // LANES, sv.ct, True, False, "fox_bwd")
        dff, acc_f = fox_post(bias_grad(dck, dcr), sv.z, wl.bf_pad)
        dqp, dkp, dvp = attn_bwd(sv.qp, sv.kp, sv.vp, do_m, st_m, 0, 0, 0, None, False, True, "mla_bwd")
        zq, zkv, zms, dw_uq, dw_uk, dw_v, dgq, dgkv = mla_post(
            dqp, dkp, dvp, dff, sv.z, wl.gq, wl.gkv, wl.w_uq, wl.w_uk, wl.w_v, cfg.cos_t, cfg.sin_t)
        dz = jnp.concatenate([(dfq * FOX_SCALE).astype(BF16), dfk.astype(BF16), dfv.astype(BF16), dfg, dmg, zq, zkv, zms], axis=1)
        dx, acc_i = bwd_in(dz, wl.w_in, sv.x, dxn, wl.norm_g, scale)
        dw_in = matmul_tn(sv.h, dz, "dw_in")
        grads = types.SimpleNamespace(
            w_in=dw_in, w_out=dw_out, w_uq=dw_uq, w_uk=dw_uk, w_v=dw_v, gq=dgq[0], gkv=dgkv[0],
            b_f=acc_f[0, :FH], norm_g=acc_i[2], dmod=jnp.concatenate([acc_i[0], acc_i[1], acc_o[0]]))
        return dx, grads

    return types.SimpleNamespace(layer_forward=layer_forward, layer_backward=layer_backward, final_loss=final_loss)


def _pack_rows(parts, dtype, row_multiple):
    flat = jnp.concatenate([p.reshape(-1).astype(dtype) for p in parts])
    per = SLAB_COLS * row_multiple
    total = -(-flat.shape[0] // per) * per
    return jnp.pad(flat, (0, total - flat.shape[0])).reshape(total // SLAB_COLS, SLAB_COLS)


def _unpack(flat, shapes):
    out, off = [], 0
    for shp in shapes:
        n = 1
        for d in shp:
            n *= d
        out.append(flat[off:off + n].reshape(shp))
        off += n
    return out


def kernel(x, c, positions, norm_g, w_ada, b_ada, w_in, b_f, q_norm_g, w_uq, kv_norm_g, w_ukv, w_out, final_g, loss_target, m_norm_g, m_w_ada, m_b_ada, m_w_in, m_b_f, m_q_norm_g, m_w_uq, m_kv_norm_g, m_w_ukv, m_w_out, m_final_g, v_norm_g, v_w_ada, v_b_ada, v_w_in, v_b_f, v_q_norm_g, v_w_uq, v_kv_norm_g, v_w_ukv, v_w_out, v_final_g):
    S, D = x.shape[1], x.shape[2]
    L = norm_g.shape[0]
    FH = b_f.shape[1]
    QL, KVL = q_norm_g.shape[1], kv_norm_g.shape[1]
    MH = w_ukv.shape[2] * N_CHIPS // (2 * HEAD_DIM)
    FW, MW = FH * HEAD_DIM, MH * HEAD_DIM
    NA = w_ada.shape[2]
    n_in = w_in.shape[2] * N_CHIPS
    cfg = types.SimpleNamespace(S=S, D=D, FW=FW, MW=MW, QL=QL, KVL=KVL, FH=FH, MH=MH)
    cfg.o_fq, cfg.o_fk, cfg.o_fv, cfg.o_fg, cfg.o_mg = 0, FW, 2 * FW, 3 * FW, 4 * FW
    cfg.o_ql = 4 * FW + MW
    cfg.o_kv = cfg.o_ql + QL
    cfg.o_ms = cfg.o_kv + KVL
    cfg.NZ = cfg.o_ms + LANES
    assert FW == MW and FH % 2 == 0 and MH % 2 == 0 and cfg.o_ql % QL == 0 and cfg.o_kv % KVL == 0 and KVL == LANES
    assert n_in == 4 * FW + FH + QL + KVL + ROPE_DIM + MW

    mx, my, mc = _my_pos()
    my_chip = 2 * mx + my
    my_dev = 2 * my_chip + mc

    inv_freq = 1.0 / (ROPE_THETA ** (jnp.arange(0, ROPE_DIM, 2, dtype=F32) / ROPE_DIM))
    ang = positions[0].astype(F32)[:, None] * inv_freq
    cos, sin = jnp.cos(ang), jnp.sin(ang)
    cfg.cos_t = jnp.concatenate([jnp.ones((S, ROPE_LO), F32), cos, cos, jnp.ones((S, HEAD_PAD - ROPE_LO - ROPE_DIM), F32)], axis=1)
    cfg.sin_t = jnp.concatenate([jnp.zeros((S, ROPE_LO), F32), -sin, sin, jnp.zeros((S, HEAD_PAD - ROPE_LO - ROPE_DIM), F32)], axis=1)

    assert L % 2 == 0
    gathered = weights_gather([w.astype(BF16) for w in (w_in, w_uq, w_ukv, w_out)], "weights_gather")
    w_in_f = jnp.concatenate([gathered[0][t] for t in range(N_CHIPS)], axis=2)
    w_uq_f = jnp.concatenate([gathered[1][t] for t in range(N_CHIPS)], axis=2)
    w_ukv_f = jnp.concatenate([gathered[2][t] for t in range(N_CHIPS)], axis=2)
    w_out_f = jnp.concatenate([gathered[3][t] for t in range(N_CHIPS)], axis=1)

    sizes = (FW, FW, FW, FH, FW, QL, KVL, ROPE_DIM, MW)
    offs = [0]
    for sz in sizes:
        offs.append(offs[-1] + sz)
    seg = [w_in_f[:, :, offs[i]:offs[i + 1]] for i in range(len(sizes))]
    fq_w, fk_w, fv_w, ff_w, fg_w, ql_w, kvl_w, kr_w, mg_w = seg
    zeros = lambda n: jnp.zeros((L, D, n), BF16)
    w_in_p = jnp.concatenate([fq_w, fk_w, fv_w, fg_w, mg_w, ql_w, kvl_w, ff_w, zeros(ROPE_LO - FH), kr_w,
                              zeros(HEAD_PAD - ROPE_LO - ROPE_DIM)], axis=2)
    w_uq_p = jnp.pad(w_uq_f.reshape(L, QL, MH, HEAD_DIM + ROPE_DIM), ((0, 0), (0, 0), (0, 0), (0, HEAD_PAD - HEAD_DIM - ROPE_DIM)))
    w_uq_p = w_uq_p.reshape(L, QL, MH * HEAD_PAD)
    w_ukv4 = w_ukv_f.reshape(L, KVL, MH, 2 * HEAD_DIM)
    w_uk_p = jnp.pad(w_ukv4[..., :HEAD_DIM], ((0, 0), (0, 0), (0, 0), (0, HEAD_PAD - HEAD_DIM))).reshape(L, KVL, MH * HEAD_PAD)
    w_v_p = w_ukv4[..., HEAD_DIM:].reshape(L, KVL, MW)

    c_all = allgather8(c.reshape(8, D // 8), "gather_c").reshape(N_DEV, D)
    c_pad = jnp.pad(c_all, ((0, 16 - N_DEV), (0, 0)))
    mod_part = ada_forward(c_pad, w_ada)[:, :N_DEV, :]
    mod_all = allgather8(mod_part.reshape(-1, LANES), "gather_mod").reshape(N_CHIPS, 2, L, N_DEV, NA)[:, 0]
    mod_full = mod_all.transpose(1, 2, 0, 3).reshape(L, N_DEV, N_CHIPS * NA) + b_ada[:, None, :]
    mod_mine = lax.dynamic_index_in_dim(mod_full, my_dev, axis=1, keepdims=True)

    step = make_step(cfg)
    bf_pad = jnp.pad(b_f, ((0, 0), (0, LANES - FH)))
    layers, mods = [], []
    for l in range(L):
        layers.append(types.SimpleNamespace(
            norm_g=norm_g[l][None], w_in=w_in_p[l], bf_pad=bf_pad[l][None], gq=q_norm_g[l][None], gkv=kv_norm_g[l][None],
            w_uq=w_uq_p[l], w_uk=w_uk_p[l], w_v=w_v_p[l], w_out=w_out_f[l]))
        mods.append((mod_mine[l, :, :D], mod_mine[l, :, D:2 * D], mod_mine[l, :, 2 * D:]))

    xl = x[0]
    saved = []
    for l in range(L):
        xl, sv = step.layer_forward(xl, layers[l], mods[l])
        saved.append(sv)
    dx, acc_fin, loss_part = step.final_loss(xl, final_g[None], loss_target[0])
    loss = lax.psum(loss_part[0, 0], ("x", "y", "c"))
    gl = [None] * L
    for l in reversed(range(L)):
        dx, gl[l] = step.layer_backward(dx, saved[l], layers[l], mods[l])
    grad_x = dx[None]

    stack = lambda name: jnp.stack([getattr(g, name) for g in gl])
    small_parts = [stack("norm_g"), stack("dmod"), stack("b_f"), stack("gq"), stack("gkv"), acc_fin[0]]
    small_shapes = [p.shape for p in small_parts]
    small = _pack_rows(small_parts, F32, 8).reshape(-1, LANES)
    small_all = allgather8(small, "gather_small").reshape(N_DEV, -1, LANES)
    small_sum = sum_leading(small_all, "sum_small")
    g_norm_g, g_b_ada, g_b_f, g_q_norm_g, g_kv_norm_g, g_final_g = _unpack(small_sum.reshape(-1), small_shapes)

    n_ng = L * D
    dmod_all = small_all.reshape(N_DEV, -1)[:, n_ng:n_ng + L * 3 * D].reshape(N_DEV, L, 3 * D)
    dmod_cols = lax.dynamic_slice_in_dim(dmod_all, my_chip * NA, NA, axis=2).transpose(1, 0, 2)
    g_w_ada = ada_backward(c_pad, jnp.pad(dmod_cols, ((0, 0), (0, 16 - N_DEV), (0, 0))))

    dw_in_p = stack("w_in")
    ms = cfg.o_ms
    dw_in_f = jnp.concatenate([
        dw_in_p[:, :, 0:3 * FW], dw_in_p[:, :, ms:ms + FH], dw_in_p[:, :, cfg.o_fg:cfg.o_fg + FW],
        dw_in_p[:, :, cfg.o_ql:cfg.o_ql + QL + KVL], dw_in_p[:, :, ms + ROPE_LO:ms + ROPE_LO + ROPE_DIM],
        dw_in_p[:, :, cfg.o_mg:cfg.o_mg + MW]], axis=2)
    dw_uq_f = stack("w_uq").reshape(L, QL, MH, HEAD_PAD)[..., :HEAD_DIM + ROPE_DIM].reshape(L, QL, -1)
    dw_ukv_f = jnp.concatenate([stack("w_uk").reshape(L, KVL, MH, HEAD_PAD)[..., :HEAD_DIM],
                                stack("w_v").reshape(L, KVL, MH, HEAD_DIM)], axis=3).reshape(L, KVL, -1)
    dw_out_f = stack("w_out")
    gs = [dw_in_f.reshape(L, D, N_CHIPS, -1).transpose(2, 0, 1, 3), dw_uq_f.reshape(L, QL, N_CHIPS, -1).transpose(2, 0, 1, 3),
          dw_ukv_f.reshape(L, KVL, N_CHIPS, -1).transpose(2, 0, 1, 3), dw_out_f.reshape(L, N_CHIPS, -1, D).transpose(1, 0, 2, 3)]
    hl = L // 2
    theirs = halves_to_sibling(gs, "grads_sibling")
    chip_part = []
    big_names = ["w_in", "w_uq", "w_ukv", "w_out"]
    for g, o, nm in zip(gs, theirs, big_names):
        keep = lax.dynamic_slice_in_dim(g, mc * hl, hl, axis=1)
        merged = (N_CHIPS * hl,) + g.shape[2:]
        chip_part.append(add_cast(keep.reshape(merged), o.reshape(merged), BF16, "grads_chip_sum_" + nm).reshape(o.shape))
    parts = chip_exchange(chip_part, "grads_chips")
    red = [sum_chips(p, "grads_sum_" + nm) for p, nm in zip(parts, big_names)]
    g_w_in, g_w_uq, g_w_ukv, g_w_out = halves_gather(red, "grads_back")


    names = ["norm_g", "w_ada", "b_ada", "w_in", "b_f", "q_norm_g", "w_uq", "kv_norm_g", "w_ukv", "w_out", "final_g"]
    ws = dict(norm_g=norm_g, w_ada=w_ada, b_ada=b_ada, w_in=w_in, b_f=b_f, q_norm_g=q_norm_g, w_uq=w_uq,
              kv_norm_g=kv_norm_g, w_ukv=w_ukv, w_out=w_out, final_g=final_g)
    msd = dict(norm_g=m_norm_g, w_ada=m_w_ada, b_ada=m_b_ada, w_in=m_w_in, b_f=m_b_f, q_norm_g=m_q_norm_g, w_uq=m_w_uq,
               kv_norm_g=m_kv_norm_g, w_ukv=m_w_ukv, w_out=m_w_out, final_g=m_final_g)
    vsd = dict(norm_g=v_norm_g, w_ada=v_w_ada, b_ada=v_b_ada, w_in=v_w_in, b_f=v_b_f, q_norm_g=v_q_norm_g, w_uq=v_w_uq,
               kv_norm_g=v_kv_norm_g, w_ukv=v_w_ukv, w_out=v_w_out, final_g=v_final_g)
    gsd = dict(norm_g=g_norm_g, w_ada=g_w_ada, b_ada=g_b_ada, w_in=g_w_in, b_f=g_b_f, q_norm_g=g_q_norm_g, w_uq=g_w_uq,
               kv_norm_g=g_kv_norm_g, w_ukv=g_w_ukv, w_out=g_w_out, final_g=g_final_g)
    small_names = ["norm_g", "b_ada", "b_f", "q_norm_g", "kv_norm_g", "final_g"]
    sm_shapes = [ws[n].shape for n in small_names]
    pk = lambda d: _pack_rows([d[n] for n in small_names], F32, 8).reshape(1, -1, LANES)
    sm_out = adamw(pk(ws), pk(gsd), pk(msd), pk(vsd), "adamw_small")
    sm_d, sm_m, sm_v = [dict(zip(small_names, _unpack(o.reshape(-1), sm_shapes))) for o in sm_out]
    delta, new_m, new_v = dict(sm_d), dict(sm_m), dict(sm_v)
    for n in ["w_ada", "w_in", "w_uq", "w_ukv", "w_out"]:
        delta[n], new_m[n], new_v[n] = adamw(ws[n], gsd[n], msd[n], vsd[n], "adamw_" + n)

    return (loss, grad_x, *[gsd[n] for n in names], *[delta[n] for n in names],
            *[new_m[n] for n in names], *[new_v[n] for n in names])
```

```python
import functools
import types

import jax
import jax.numpy as jnp
from jax import lax
from jax.experimental import pallas as pl
from jax.experimental.pallas import tpu as pltpu

F32 = jnp.float32
BF16 = jnp.bfloat16
MESH = pl.DeviceIdType.MESH

N_CHIPS = 4
N_DEV = 8
HEAD_DIM = 64
ROPE_DIM = 32
ROPE_THETA = 10000.0
HEAD_PAD = 128
ROPE_LO = 64
ROPE_HALF = 16
LANES = 128
EPS = 1e-6
NEG = -1e30
ADAM_LR = 0.001
ADAM_B1 = 0.9
ADAM_B2 = 0.999
ADAM_EPS = 1e-08
ADAM_WD = 0.01
ADAM_STEP = 10
VMEM_LIMIT = 48 * 1024 * 1024
SLAB_COLS = 1024


def _params(sem=None, vmem=VMEM_LIMIT):
    return pltpu.CompilerParams(dimension_semantics=sem, vmem_limit_bytes=vmem)


def _nn(a, b):
    return jnp.dot(a, b, preferred_element_type=F32)


def _nt(a, b):
    return lax.dot_general(a, b, (((1,), (1,)), ((), ())), preferred_element_type=F32)


def _tn(a, b):
    return lax.dot_general(a, b, (((0,), (0,)), ((), ())), preferred_element_type=F32)


def _sigmoid(x):
    return 1.0 / (1.0 + jnp.exp(-x))


def _lane(shape):
    return lax.broadcasted_iota(jnp.int32, shape, len(shape) - 1)


def _pick(n, cands):
    for c in cands:
        if n % c == 0:
            return c
    return n


def _my_pos():
    return lax.axis_index("x"), lax.axis_index("y"), lax.axis_index("c")


def allgather8(xs, name):
    m_per, n = xs.shape

    def body(x_ref, out_ref, send_sems, recv_sems, local_sem):
        x, y, c = _my_pos()
        me, sibling = (x, y, c), (x, y, 1 - c)
        chips = [(1 - x, y), (x, 1 - y), (1 - x, 1 - y)]

        def rows(px, py, pc):
            return out_ref.at[pl.ds((4 * px + 2 * py + pc) * m_per, m_per), :]

        def copy(k, block, to, src=None):
            return pltpu.make_async_remote_copy(
                src_ref=rows(*block) if src is None else src, dst_ref=rows(*block),
                send_sem=send_sems.at[k], recv_sem=recv_sems.at[k], device_id=to, device_id_type=MESH)

        mine = pltpu.make_async_copy(x_ref, rows(*me), local_sem)
        mine.start()
        first = [copy(0, me, sibling, src=x_ref)]
        first += [copy(1 + j, me, (*chip, c), src=x_ref) for j, chip in enumerate(chips)]
        for cp in first:
            cp.start()
        passed = [copy(4 + j, (*chip, c), sibling) for j, chip in enumerate(chips)]
        for j, chip in enumerate(chips):
            copy(1 + j, (*chip, c), me).wait_recv()
            passed[j].start()
        copy(0, sibling, me).wait_recv()
        for j, chip in enumerate(chips):
            copy(4 + j, (*chip, 1 - c), me).wait_recv()
        for cp in first + passed:
            cp.wait_send()
        mine.wait()

    return pl.pallas_call(
        body, name=name,
        out_shape=jax.ShapeDtypeStruct((N_DEV * m_per, n), xs.dtype),
        in_specs=[pl.BlockSpec(memory_space=pltpu.VMEM)],
        out_specs=pl.BlockSpec(memory_space=pltpu.VMEM),
        scratch_shapes=[pltpu.SemaphoreType.DMA((7,)), pltpu.SemaphoreType.DMA((7,)), pltpu.SemaphoreType.DMA],
    )(xs)


def _remote(src, dst, send_sems, recv_sems, k, to):
    return pltpu.make_async_remote_copy(src_ref=src, dst_ref=dst, send_sem=send_sems.at[k], recv_sem=recv_sems.at[k],
                                        device_id=to, device_id_type=MESH)


def _hbm_call(body, name, ins, out_shapes, n_sems, n_local):
    hbm = pl.BlockSpec(memory_space=pl.ANY)
    scratch = [pltpu.SemaphoreType.DMA((n_sems,)), pltpu.SemaphoreType.DMA((n_sems,))]
    if n_local:
        scratch.append(pltpu.SemaphoreType.DMA((n_local,)))
    return pl.pallas_call(body, name=name, out_shape=out_shapes, in_specs=[hbm] * len(ins),
                          out_specs=[hbm] * len(out_shapes), scratch_shapes=scratch)(*ins)


def _layers_half(ref, h, axis=0):
    size = ref.shape[axis] // 2
    idx = (slice(None),) * axis + (pl.ds(h * size, size),)
    return ref.at[idx]


def weights_gather(ws, name):
    n = len(ws)

    def body(*refs):
        x_refs, o_refs = refs[:n], refs[n:2 * n]
        send_sems, recv_sems, local_sems = refs[2 * n:]
        x, y, c = _my_pos()
        s = 2 * x + y
        sibling = (x, y, 1 - c)
        chips = [(1 - x, y), (x, 1 - y), (1 - x, 1 - y)]
        owns, sends = [], []
        for a in range(n):
            own = pltpu.make_async_copy(x_refs[a], o_refs[a].at[s], local_sems.at[a])
            own.start()
            owns.append(own)
            for k, (tx, ty) in enumerate(chips):
                cp = _remote(_layers_half(x_refs[a], c), _layers_half(o_refs[a].at[s], c), send_sems, recv_sems,
                             6 * a + k, (tx, ty, c))
                cp.start()
                sends.append(cp)
        for a in range(n):
            for k, (tx, ty) in enumerate(chips):
                got = _layers_half(o_refs[a].at[2 * tx + ty], c)
                _remote(got, got, send_sems, recv_sems, 6 * a + k, (tx, ty, c)).wait_recv()
                cp = _remote(got, got, send_sems, recv_sems, 6 * a + 3 + k, sibling)
                cp.start()
                sends.append(cp)
        for a in range(n):
            for k, (tx, ty) in enumerate(chips):
                theirs = _layers_half(o_refs[a].at[2 * tx + ty], 1 - c)
                _remote(theirs, theirs, send_sems, recv_sems, 6 * a + 3 + k, sibling).wait_recv()
        for cp in sends:
            cp.wait_send()
        for own in owns:
            own.wait()

    outs = [jax.ShapeDtypeStruct((N_CHIPS,) + w.shape, w.dtype) for w in ws]
    return _hbm_call(body, name, ws, outs, 6 * n, n)


def halves_to_sibling(gs, name):
    n = len(gs)

    def body(*refs):
        x_refs, o_refs, send_sems, recv_sems = refs[:n], refs[n:2 * n], refs[2 * n], refs[2 * n + 1]
        x, y, c = _my_pos()
        cps = [_remote(_layers_half(x_refs[a], 1 - c, axis=1), o_refs[a], send_sems, recv_sems, a, (x, y, 1 - c))
               for a in range(n)]
        for cp in cps:
            cp.start()
        for cp in cps:
            cp.wait()

    outs = [jax.ShapeDtypeStruct((g.shape[0], g.shape[1] // 2) + g.shape[2:], g.dtype) for g in gs]
    return _hbm_call(body, name, gs, outs, n, 0)


def chip_exchange(xs, name):
    n = len(xs)

    def body(*refs):
        x_refs, o_refs = refs[:n], refs[n:2 * n]
        send_sems, recv_sems, local_sems = refs[2 * n:]
        x, y, c = _my_pos()
        s = 2 * x + y
        chips = [(1 - x, y), (x, 1 - y), (1 - x, 1 - y)]
        owns, sends = [], []
        for a in range(n):
            own = pltpu.make_async_copy(x_refs[a].at[s], o_refs[a].at[s], local_sems.at[a])
            own.start()
            owns.append(own)
            for k, (tx, ty) in enumerate(chips):
                cp = _remote(x_refs[a].at[2 * tx + ty], o_refs[a].at[s], send_sems, recv_sems, 3 * a + k, (tx, ty, c))
                cp.start()
                sends.append(cp)
        for a in range(n):
            for k, (tx, ty) in enumerate(chips):
                got = o_refs[a].at[2 * tx + ty]
                _remote(got, got, send_sems, recv_sems, 3 * a + k, (tx, ty, c)).wait_recv()
        for cp in sends:
            cp.wait_send()
        for own in owns:
            own.wait()

    outs = [jax.ShapeDtypeStruct(v.shape, v.dtype) for v in xs]
    return _hbm_call(body, name, xs, outs, 3 * n, n)


def halves_gather(xs, name):
    n = len(xs)

    def body(*refs):
        x_refs, o_refs = refs[:n], refs[n:2 * n]
        send_sems, recv_sems, local_sems = refs[2 * n:]
        x, y, c = _my_pos()
        owns, sends = [], []
        for a in range(n):
            mine = _layers_half(o_refs[a], c)
            own = pltpu.make_async_copy(x_refs[a], mine, local_sems.at[a])
            own.start()
            owns.append(own)
            cp = _remote(x_refs[a], mine, send_sems, recv_sems, a, (x, y, 1 - c))
            cp.start()
            sends.append(cp)
        for a in range(n):
            theirs = _layers_half(o_refs[a], 1 - c)
            _remote(theirs, theirs, send_sems, recv_sems, a, (x, y, 1 - c)).wait_recv()
        for cp in sends:
            cp.wait_send()
        for own in owns:
            own.wait()

    outs = [jax.ShapeDtypeStruct((2 * v.shape[0],) + v.shape[1:], v.dtype) for v in xs]
    return _hbm_call(body, name, xs, outs, n, n)


def sum_leading(xs, name):
    n, r, c = xs.shape

    def body(x_ref, o_ref):
        acc = x_ref[0]
        for i in range(1, n):
            acc = acc + x_ref[i]
        o_ref[...] = acc

    return pl.pallas_call(body, name=name, out_shape=jax.ShapeDtypeStruct((r, c), xs.dtype))(xs)


def add_cast(a, b, out_dtype, name):
    n, r, c = a.shape
    tr = _pick(r, (512, 256, 128, 64, 32, 16))

    def body(a_ref, b_ref, o_ref):
        o_ref[...] = (a_ref[...] + b_ref[...]).astype(out_dtype)

    spec = pl.BlockSpec((1, tr, c), lambda i, j: (i, j, 0))
    return pl.pallas_call(body, name=name, grid=(n, r // tr), in_specs=[spec, spec], out_specs=spec,
                          out_shape=jax.ShapeDtypeStruct((n, r, c), out_dtype),
                          compiler_params=_params(("parallel", "parallel")))(a, b)


def sum_chips(xs, name):
    n, nl, r, c = xs.shape
    tr = _pick(r, (512, 256, 128, 64, 32, 16))

    def body(x_ref, o_ref):
        acc = x_ref[0].astype(F32)
        for i in range(1, n):
            acc = acc + x_ref[i].astype(F32)
        o_ref[...] = acc

    return pl.pallas_call(body, name=name, grid=(nl, r // tr),
                          in_specs=[pl.BlockSpec((n, 1, tr, c), lambda i, j: (0, i, j, 0))],
                          out_specs=pl.BlockSpec((1, tr, c), lambda i, j: (i, j, 0)),
                          out_shape=jax.ShapeDtypeStruct((nl, r, c), F32),
                          compiler_params=_params(("parallel", "parallel")))(xs)


def ada_forward(c_all, w_ada):
    nl, d, n = w_ada.shape
    nb = c_all.shape[0]

    def body(c_ref, w_ref, o_ref):
        cv = c_ref[...]
        ca = (cv * _sigmoid(cv)).astype(BF16)
        o_ref[0] = _nn(ca, w_ref[0].astype(BF16))

    return pl.pallas_call(
        body, name="ada_forward", grid=(nl,),
        in_specs=[pl.BlockSpec((nb, d), lambda l: (0, 0)), pl.BlockSpec((1, d, n), lambda l: (l, 0, 0))],
        out_specs=pl.BlockSpec((1, nb, n), lambda l: (l, 0, 0)),
        out_shape=jax.ShapeDtypeStruct((nl, nb, n), F32), compiler_params=_params(("parallel",)))(c_all, w_ada)


def ada_backward(c_all, dmod):
    nl, nb, n = dmod.shape
    d = c_all.shape[1]

    def body(c_ref, g_ref, o_ref):
        cv = c_ref[...]
        ca = (cv * _sigmoid(cv)).astype(BF16)
        o_ref[0] = _tn(ca, g_ref[0].astype(BF16))

    return pl.pallas_call(
        body, name="ada_backward", grid=(nl,),
        in_specs=[pl.BlockSpec((nb, d), lambda l: (0, 0)), pl.BlockSpec((1, nb, n), lambda l: (l, 0, 0))],
        out_specs=pl.BlockSpec((1, d, n), lambda l: (l, 0, 0)),
        out_shape=jax.ShapeDtypeStruct((nl, d, n), F32), compiler_params=_params(("parallel",)))(c_all, dmod)


def matmul_tn(a, b, name):
    k, m = a.shape
    n = b.shape[1]
    tm, tk = _pick(m, (512, 256, 128)), _pick(k, (512, 256, 128))
    tn = n if n * (tm * 8 + tk * 4) <= VMEM_LIMIT // 2 else _pick(n, (512, 256, 128))

    def body(a_ref, b_ref, o_ref):
        @pl.when(pl.program_id(2) == 0)
        def _():
            o_ref[...] = jnp.zeros_like(o_ref)

        o_ref[...] += _tn(a_ref[...], b_ref[...])

    return pl.pallas_call(
        body, name=name, grid=(m // tm, n // tn, k // tk),
        in_specs=[pl.BlockSpec((tk, tm), lambda i, j, kk: (kk, i)), pl.BlockSpec((tk, tn), lambda i, j, kk: (kk, j))],
        out_specs=pl.BlockSpec((tm, tn), lambda i, j, kk: (i, j)),
        out_shape=jax.ShapeDtypeStruct((m, n), F32),
        compiler_params=_params(("parallel", "parallel", "arbitrary")))(a, b)


def adamw(w, g, m, v, name):
    nl, r, c = w.shape
    tr = _pick(r, (512, 256, 128, 64, 32, 16, 8))

    def body(w_ref, g_ref, m_ref, v_ref, d_ref, mo_ref, vo_ref):
        gv = g_ref[...]
        mn = ADAM_B1 * m_ref[...] + (1.0 - ADAM_B1) * gv
        vn = ADAM_B2 * v_ref[...] + (1.0 - ADAM_B2) * (gv * gv)
        m_hat = mn / (1.0 - ADAM_B1 ** ADAM_STEP)
        v_hat = vn / (1.0 - ADAM_B2 ** ADAM_STEP)
        d_ref[...] = -ADAM_LR * (m_hat / (jnp.sqrt(v_hat) + ADAM_EPS) + ADAM_WD * w_ref[...])
        mo_ref[...] = mn
        vo_ref[...] = vn

    spec = pl.BlockSpec((1, tr, c), lambda l, i: (l, i, 0))
    out = jax.ShapeDtypeStruct((nl, r, c), F32)
    return pl.pallas_call(body, name=name, grid=(nl, r // tr), in_specs=[spec] * 4, out_specs=[spec] * 3,
                          out_shape=[out] * 3, compiler_params=_params(("parallel", "parallel")))(w, g, m, v)


def _rope(t, cos_t, sin_t):
    w = t.shape[1]
    lane = _lane(t.shape) & (HEAD_PAD - 1)
    first_half = (lane >= ROPE_LO) & (lane < ROPE_LO + ROPE_HALF)
    partner = jnp.where(first_half, pltpu.roll(t, w - ROPE_HALF, 1), pltpu.roll(t, ROPE_HALF, 1))
    return t * cos_t + partner * sin_t


def _rope_t(dt, cos_t, sin_t):
    w = dt.shape[1]
    lane = _lane(dt.shape) & (HEAD_PAD - 1)
    first_half = (lane >= ROPE_LO) & (lane < ROPE_LO + ROPE_HALF)
    ds = dt * sin_t
    partner = jnp.where(first_half, pltpu.roll(ds, w - ROPE_HALF, 1), pltpu.roll(ds, ROPE_HALF, 1))
    return dt * cos_t + partner


def _rms(xv, g):
    rstd = lax.rsqrt(jnp.mean(xv * xv, axis=-1, keepdims=True) + EPS)
    xh = xv * rstd
    return xh * g, xh, rstd


def _rms_bwd(dy, g, xh, rstd):
    dxh = dy * g
    dx = rstd * (dxh - xh * jnp.mean(dxh * xh, axis=-1, keepdims=True))
    return dx, jnp.sum(dy * xh, axis=0, keepdims=True)


def make_step(cfg):
    S, D, NZ = cfg.S, cfg.D, cfg.NZ
    FW, MW, QL, KVL, FH, MH = cfg.FW, cfg.MW, cfg.QL, cfg.KVL, cfg.FH, cfg.MH
    QW = MH * HEAD_PAD
    TM = _pick(S, (256, 128))
    TQ = TK = _pick(S, (256, 128))
    n_tok = S // TM
    ZO = 3 * FW
    NZR = NZ - ZO
    misc_blk = (cfg.o_ms - ZO) // LANES
    FOX_SCALE = HEAD_DIM ** -0.5
    MLA_SCALE = (HEAD_DIM + ROPE_DIM) ** -0.5

    def tok(width, col=0):
        return pl.BlockSpec((TM, width), lambda i: (i, col))

    def const(shape):
        return pl.BlockSpec(shape, lambda i: tuple(0 for _ in shape))


    def ln_inproj(x, g, scale, shift, w_in):
        def body(x_ref, g_ref, sc_ref, sh_ref, w_ref, h_ref, z_ref, qkv_ref):
            y, _, _ = _rms(x_ref[...], g_ref[...])
            hb = (y * (1.0 + sc_ref[...]) + sh_ref[...]).astype(BF16)
            h_ref[...] = hb
            z = _nn(hb, w_ref[...])
            z_ref[...] = z[:, ZO:]
            qkv_ref[:, :FW] = (z[:, :FW] * FOX_SCALE).astype(BF16)
            qkv_ref[:, FW:] = z[:, FW:ZO].astype(BF16)

        return pl.pallas_call(
            body, name="ln_inproj", grid=(n_tok,),
            in_specs=[tok(D), const((1, D)), const((1, D)), const((1, D)), const((D, NZ))],
            out_specs=[tok(D), tok(NZR), tok(ZO)],
            out_shape=[jax.ShapeDtypeStruct((S, D), BF16), jax.ShapeDtypeStruct((S, NZR), F32),
                       jax.ShapeDtypeStruct((S, ZO), BF16)],
            compiler_params=_params(("parallel",)))(x, g, scale, shift, w_in)

    def _log_f_terms(misc, bf):
        lane = _lane(misc.shape)
        a = misc + bf
        e = jnp.exp(-jnp.abs(a))
        logf = jnp.minimum(a, 0.0) - jnp.log(1.0 + e)
        sig_neg = jnp.where(a >= 0, e, 1.0) / (1.0 + e)
        valid = lane < FH
        return jnp.where(valid, logf, 0.0), jnp.where(valid, sig_neg, 0.0)

    def fox_prep(z, bf_pad):
        def body(z_ref, b_ref, o_ref, carry):
            @pl.when(pl.program_id(0) == 0)
            def _():
                carry[...] = jnp.zeros_like(carry)

            logf, _ = _log_f_terms(z_ref[...], b_ref[...])
            row = lax.broadcasted_iota(jnp.int32, (TM, TM), 0)
            col = lax.broadcasted_iota(jnp.int32, (TM, TM), 1)
            tri = (col <= row).astype(F32)
            cum = jnp.dot(tri, logf, precision=lax.Precision.HIGHEST, preferred_element_type=F32) + carry[...]
            o_ref[...] = cum
            carry[...] = cum[TM - 1:TM, :]

        return pl.pallas_call(
            body, name="fox_prep", grid=(n_tok,),
            in_specs=[tok(LANES, misc_blk), const((1, LANES))], out_specs=tok(LANES),
            out_shape=jax.ShapeDtypeStruct((S, LANES), F32),
            scratch_shapes=[pltpu.VMEM((1, LANES), F32)],
            compiler_params=_params(("arbitrary",)))(z, bf_pad)

    def mla_prep(z, gq, gkv, w_uq, w_uk, w_v, cos_t, sin_t):
        def body(ql_ref, kvl_ref, ms_ref, gq_ref, gkv_ref, wq_ref, wk_ref, wv_ref, c_ref, s_ref, q_ref, k_ref, v_ref):
            cos1, sin1 = c_ref[...], s_ref[...]
            qn, _, _ = _rms(ql_ref[...], gq_ref[...])
            q = _nn(qn.astype(BF16), wq_ref[...])
            q_ref[...] = (_rope(q, jnp.tile(cos1, (1, MH)), jnp.tile(sin1, (1, MH))) * MLA_SCALE).astype(BF16)
            kvn, _, _ = _rms(kvl_ref[...], gkv_ref[...])
            kvb = kvn.astype(BF16)
            lane = _lane((TM, LANES))
            kr = jnp.where((lane >= ROPE_LO) & (lane < ROPE_LO + ROPE_DIM), ms_ref[...], 0.0)
            kr = _rope(kr, cos1, sin1)
            k_ref[...] = (_nn(kvb, wk_ref[...]) + jnp.tile(kr, (1, MH))).astype(BF16)
            v_ref[...] = _nn(kvb, wv_ref[...]).astype(BF16)

        return pl.pallas_call(
            body, name="mla_prep", grid=(n_tok,),
            in_specs=[tok(QL, (cfg.o_ql - ZO) // QL), tok(KVL, (cfg.o_kv - ZO) // KVL), tok(LANES, misc_blk),
                      const((1, QL)), const((1, KVL)), const((QL, QW)), const((KVL, QW)), const((KVL, MW)),
                      tok(LANES), tok(LANES)],
            out_specs=[tok(QW), tok(QW), tok(MW)],
            out_shape=[jax.ShapeDtypeStruct((S, QW), BF16), jax.ShapeDtypeStruct((S, QW), BF16),
                       jax.ShapeDtypeStruct((S, MW), BF16)],
            compiler_params=_params(("parallel",)))(z, z, z, gq, gkv, w_uq, w_uk, w_v, cos_t, sin_t)

    def _allowed(q0, k0, chunked):
        qi = q0 + lax.broadcasted_iota(jnp.int32, (TQ, TK), 0)
        ki = k0 + lax.broadcasted_iota(jnp.int32, (TQ, TK), 1)
        if chunked:
            return (ki >> 6) <= (qi >> 6)
        return ki <= qi

    def _heads(val, packed):
        if packed:
            lane = _lane(val.shape)
            zero = jnp.zeros_like(val)
            return [jnp.where(lane < HEAD_DIM, val, zero), jnp.where(lane >= HEAD_DIM, val, zero)]
        return [val[:, :HEAD_PAD], val[:, HEAD_PAD:]]

    def _merge(a0, a1):
        return jnp.where(_lane(a0.shape) < HEAD_DIM, a0, a1)

    def attn_fwd(q, k, v, q_blk0, k_blk0, v_blk0, cumt, packed, chunked, name):
        has_bias = cumt is not None
        n_pairs = (FH if packed else MH) // 2
        wq = LANES if packed else 2 * HEAD_PAD
        assert TQ == TK

        def body(*refs):
            if has_bias:
                q_ref, k_ref, v_ref, ct_ref, o_ref, lse_ref = refs
            else:
                q_ref, k_ref, v_ref, o_ref, lse_ref = refs
            i = pl.program_id(1)
            q0 = i * TQ
            qh = _heads(q_ref[...], packed)
            q_both = jnp.concatenate(qh, axis=0) if packed else None

            def scores(kb):
                kblk = k_ref[pl.ds(pl.multiple_of(kb * TK, TK), TK), :]
                if packed:
                    return _nt(q_both, kblk)
                return jnp.concatenate([_nt(qh[0], kblk[:, :HEAD_PAD]), _nt(qh[1], kblk[:, HEAD_PAD:])], axis=0)

            def update(kb, s, carry, masked):
                m, l, acc = carry
                k0 = pl.multiple_of(kb * TK, TK)
                s0, s1 = s[:TQ], s[TQ:]
                if has_bias:
                    ck = ct_ref[0, kb]
                    s0, s1 = s0 - ck[0:1, :], s1 - ck[1:2, :]
                if masked:
                    allow = _allowed(q0, k0, chunked)
                    s0, s1 = jnp.where(allow, s0, NEG), jnp.where(allow, s1, NEG)
                s = jnp.concatenate([s0, s1], axis=0)
                m_new = jnp.maximum(m, jnp.max(s, axis=-1, keepdims=True))
                p = jnp.exp(s - m_new)
                alpha = jnp.exp(m - m_new)
                l = alpha * l + jnp.sum(p, axis=-1, keepdims=True)
                acc = alpha * acc + _nn(p.astype(BF16), v_ref[pl.ds(k0, TK), :])
                return m_new, l, acc

            def two(kb, carry, last_masked):
                s_a, s_b = scores(kb), scores(kb + 1)
                return update(kb + 1, s_b, update(kb, s_a, carry, False), last_masked)

            init = (jnp.full((2 * TQ, 1), NEG, F32), jnp.zeros((2 * TQ, 1), F32), jnp.zeros((2 * TQ, LANES), F32))
            carry = lax.fori_loop(0, i // 2, lambda t, cr: two(2 * t, cr, False), init)
            m, l, acc = lax.cond(i % 2 == 1, lambda cr: two(i - 1, cr, True),
                                 lambda cr: update(i, scores(i), cr, True), carry)
            o = acc / l
            lse = m + jnp.log(l)
            lane = _lane((TQ, LANES))
            o_ref[...] = _merge(o[:TQ], o[TQ:])
            lse_ref[...] = jnp.where(lane == 0, lse[:TQ], jnp.where(lane == 1, lse[TQ:], 0.0))

        in_specs = [pl.BlockSpec((TQ, wq), lambda p, i: (i, q_blk0 + p)),
                    pl.BlockSpec((S, wq), lambda p, i: (0, k_blk0 + p)),
                    pl.BlockSpec((S, LANES), lambda p, i: (0, v_blk0 + p))]
        args = [q, k, v]
        if has_bias:
            in_specs += [pl.BlockSpec((1, S // TK, 8, TK), lambda p, i: (p, 0, 0, 0))]
            args += [cumt]
        return pl.pallas_call(
            body, name=name, grid=(n_pairs, S // TQ), in_specs=in_specs,
            out_specs=[pl.BlockSpec((TQ, LANES), lambda p, i: (i, p)),
                       pl.BlockSpec((TQ, LANES), lambda p, i: (i, p))],
            out_shape=[jax.ShapeDtypeStruct((S, n_pairs * LANES), F32),
                       jax.ShapeDtypeStruct((S, n_pairs * LANES), F32)],
            compiler_params=_params(("parallel", "parallel")))(*args)

    def gate_outproj(of, om, z, w_out, x, gate):
        def body(of_ref, om_ref, fg_ref, mg_ref, w_ref, x_ref, gt_ref, xn_ref, u_ref, y_ref):
            fg, mg = fg_ref[...], mg_ref[...]
            u = jnp.concatenate([of_ref[...] * fg * _sigmoid(fg), om_ref[...] * mg * _sigmoid(mg)], axis=1).astype(BF16)
            y = _nn(u, w_ref[...])
            u_ref[...] = u
            y_ref[...] = y.astype(BF16)
            xn_ref[...] = x_ref[...] + gt_ref[...] * y

        return pl.pallas_call(
            body, name="gate_outproj", grid=(n_tok,),
            in_specs=[tok(FW), tok(MW), tok(FW, (cfg.o_fg - ZO) // FW), tok(MW, (cfg.o_mg - ZO) // MW),
                      const((FW + MW, D)), tok(D), const((1, D))],
            out_specs=[tok(D), tok(FW + MW), tok(D)],
            out_shape=[jax.ShapeDtypeStruct((S, D), F32), jax.ShapeDtypeStruct((S, FW + MW), BF16),
                       jax.ShapeDtypeStruct((S, D), BF16)],
            compiler_params=_params(("parallel",)))(of, om, z, z, w_out, x, gate)

    def final_loss(x, g, target):
        def body(x_ref, g_ref, t_ref, dx_ref, acc_ref, loss_ref):
            @pl.when(pl.program_id(0) == 0)
            def _():
                acc_ref[...] = jnp.zeros_like(acc_ref)
                loss_ref[...] = jnp.zeros_like(loss_ref)

            gv = g_ref[...]
            y, xh, rstd = _rms(x_ref[...], gv)
            e = y - t_ref[...]
            loss_ref[...] += 0.5 * jnp.sum(jnp.sum(e * e, axis=-1, keepdims=True) / D, axis=0, keepdims=True)
            dx, dg = _rms_bwd(e / D, gv, xh, rstd)
            dx_ref[...] = dx
            acc_ref[0:1, :] += dg

        return pl.pallas_call(
            body, name="final_loss", grid=(n_tok,),
            in_specs=[tok(D), const((1, D)), tok(D)],
            out_specs=[tok(D), const((8, D)), const((1, LANES))],
            out_shape=[jax.ShapeDtypeStruct((S, D), F32), jax.ShapeDtypeStruct((8, D), F32),
                       jax.ShapeDtypeStruct((1, LANES), F32)],
            compiler_params=_params(("arbitrary",)))(x, g, target)


    def bwd_out(dxn, gate, y, w_out, of, om, z, lse_f, lse_m):
        def body(dx_ref, gt_ref, y_ref, w_ref, of_ref, om_ref, fg_ref, mg_ref, lf_ref, lm_ref,
                 dof_ref, dom_ref, dfg_ref, dmg_ref, dy_ref, acc_ref, sf_ref, sm_ref):
            @pl.when(pl.program_id(0) == 0)
            def _():
                acc_ref[...] = jnp.zeros_like(acc_ref)

            dxv = dx_ref[...]
            acc_ref[0:1, :] += jnp.sum(dxv * y_ref[...].astype(F32), axis=0, keepdims=True)
            dy = (gt_ref[...] * dxv).astype(BF16)
            dy_ref[...] = dy
            du = _nt(dy, w_ref[...])
            lane = _lane((TM, LANES))
            for lo, width, o_ref, g_ref, do_ref, dg_ref, l_ref, st_ref in (
                    (0, FW, of_ref, fg_ref, dof_ref, dfg_ref, lf_ref, sf_ref),
                    (FW, MW, om_ref, mg_ref, dom_ref, dmg_ref, lm_ref, sm_ref)):
                gv, ov = g_ref[...], o_ref[...]
                sg = _sigmoid(gv)
                dup = du[:, lo:lo + width]
                dob = (dup * gv * sg).astype(BF16)
                do_ref[...] = dob
                dg_ref[...] = (dup * ov * sg * (1.0 + gv * (1.0 - sg))).astype(BF16)
                d = dob.astype(F32) * ov
                for pr in range(width // LANES):
                    cols = slice(pr * LANES, (pr + 1) * LANES)
                    dp = d[:, cols]
                    d0 = jnp.sum(jnp.where(lane < HEAD_DIM, dp, 0.0), axis=-1, keepdims=True)
                    d1 = jnp.sum(jnp.where(lane >= HEAD_DIM, dp, 0.0), axis=-1, keepdims=True)
                    st_ref[:, cols] = jnp.where(lane == 2, d0, jnp.where(lane == 3, d1, l_ref[:, cols]))

        return pl.pallas_call(
            body, name="bwd_out", grid=(n_tok,),
            in_specs=[tok(D), const((1, D)), tok(D), const((FW + MW, D)), tok(FW), tok(MW),
                      tok(FW, (cfg.o_fg - ZO) // FW), tok(MW, (cfg.o_mg - ZO) // MW), tok(FW), tok(MW)],
            out_specs=[tok(FW), tok(MW), tok(FW), tok(MW), tok(D), const((8, D)), tok(FW), tok(MW)],
            out_shape=[jax.ShapeDtypeStruct((S, FW), BF16), jax.ShapeDtypeStruct((S, MW), BF16),
                       jax.ShapeDtypeStruct((S, FW), BF16), jax.ShapeDtypeStruct((S, MW), BF16),
                       jax.ShapeDtypeStruct((S, D), BF16), jax.ShapeDtypeStruct((8, D), F32),
                       jax.ShapeDtypeStruct((S, FW), F32), jax.ShapeDtypeStruct((S, MW), F32)],
            compiler_params=_params(("arbitrary",)))(dxn, gate, y, w_out, of, om, z, z, lse_f, lse_m)

    def attn_bwd(q, k, v, do, stats, q_blk0, k_blk0, v_blk0, cumt, packed, chunked, name):
        has_bias = cumt is not None
        n_pairs = (FH if packed else MH) // 2
        wq = LANES if packed else 2 * HEAD_PAD
        n_q = S // TQ
        assert TQ == TK

        def body(*refs):
            if has_bias:
                q_ref, k_ref, v_ref, do_ref, st_ref, ct_ref, dq_ref, dk_ref, dv_ref, dc_ref, dr_ref = refs
            else:
                q_ref, k_ref, v_ref, do_ref, st_ref, dq_ref, dk_ref, dv_ref = refs
            jb = pl.program_id(1)
            k0 = jb * TK

            @pl.when(jb == 0)
            def _():
                dq_ref[...] = jnp.zeros_like(dq_ref)
                if has_bias:
                    dr_ref[...] = jnp.zeros_like(dr_ref)

            dk_ref[...] = jnp.zeros_like(dk_ref)
            dv_ref[...] = jnp.zeros_like(dv_ref)
            kh = _heads(k_ref[...], packed)
            k_both = jnp.concatenate(kh, axis=0) if packed else None
            v_both = jnp.concatenate(_heads(v_ref[...], True), axis=0)
            ck = jnp.concatenate([ct_ref[0, 0][0:1, :], ct_ref[0, 0][1:2, :]], axis=1) if has_bias else None

            def products(ib):
                rows = pl.ds(pl.multiple_of(ib * TQ, TQ), TQ)
                q2, do2 = q_ref[rows, :], do_ref[rows, :]
                if packed:
                    s = _nt(q2, k_both)
                else:
                    qh = _heads(q2, False)
                    s = jnp.concatenate([_nt(qh[0], kh[0]), _nt(qh[1], kh[1])], axis=1)
                return s, _nt(do2, v_both)

            def update(ib, s, dp, carry, masked):
                q0 = pl.multiple_of(ib * TQ, TQ)
                rows = pl.ds(q0, TQ)
                q2, do2, st = q_ref[rows, :], do_ref[rows, :], st_ref[rows, :]
                if not packed:
                    qh = _heads(q2, False)
                if has_bias:
                    s = s - ck
                if masked:
                    allow = _allowed(q0, k0, chunked)
                    s = jnp.where(jnp.concatenate([allow, allow], axis=1), s, NEG)
                p = jnp.concatenate([jnp.exp(s[:, :TK] - st[:, 0:1]), jnp.exp(s[:, TK:] - st[:, 1:2])], axis=1)
                dv2 = _tn(p.astype(BF16), do2)
                ds = jnp.concatenate([p[:, :TK] * (dp[:, :TK] - st[:, 2:3]), p[:, TK:] * (dp[:, TK:] - st[:, 3:4])], axis=1)
                dsb = ds.astype(BF16)
                dv_ref[...] += _merge(dv2[:TK], dv2[TK:])
                if packed:
                    dk2 = _tn(dsb, q2)
                    dk_ref[...] += _merge(dk2[:TK], dk2[TK:])
                    dq_ref[rows, :] += _nn(dsb, k_both)
                else:
                    dk_ref[...] += jnp.concatenate([_tn(dsb[:, :TK], qh[0]), _tn(dsb[:, TK:], qh[1])], axis=1)
                    dq_ref[rows, :] += jnp.concatenate([_nn(dsb[:, :TK], kh[0]), _nn(dsb[:, TK:], kh[1])], axis=1)
                if has_bias:
                    lane = _lane((TQ, LANES))
                    r0 = jnp.sum(ds[:, :TK], axis=-1, keepdims=True)
                    r1 = jnp.sum(ds[:, TK:], axis=-1, keepdims=True)
                    dr_ref[0, rows, :] += jnp.where(lane == 0, r0, jnp.where(lane == 1, r1, 0.0))
                    return carry - jnp.sum(ds, axis=0, keepdims=True)
                return carry

            def step(ib, carry, masked):
                s, dp = products(ib)
                return update(ib, s, dp, carry, masked)

            def two(ib, carry):
                s_a, dp_a = products(ib)
                s_b, dp_b = products(ib + 1)
                return update(ib + 1, s_b, dp_b, update(ib, s_a, dp_a, carry, False), False)

            dc = step(jb, jnp.zeros((1, 2 * TK), F32), True)
            n_rest = n_q - 1 - jb
            dc = lax.fori_loop(0, n_rest // 2, lambda t, cr: two(jb + 1 + 2 * t, cr), dc)
            dc = lax.cond(n_rest % 2 == 1, lambda cr: step(n_q - 1, cr, False), lambda cr: cr, dc)
            if has_bias:
                sub = lax.broadcasted_iota(jnp.int32, (8, TK), 0)
                dc_ref[0, 0] = jnp.where(sub == 0, dc[:, :TK], jnp.where(sub == 1, dc[:, TK:], 0.0))

        in_specs = [pl.BlockSpec((S, wq), lambda p, j: (0, q_blk0 + p)),
                    pl.BlockSpec((TK, wq), lambda p, j: (j, k_blk0 + p)),
                    pl.BlockSpec((TK, LANES), lambda p, j: (j, v_blk0 + p)),
                    pl.BlockSpec((S, LANES), lambda p, j: (0, p)),
                    pl.BlockSpec((S, LANES), lambda p, j: (0, p))]
        args = [q, k, v, do, stats]
        out_specs = [pl.BlockSpec((S, wq), lambda p, j: (0, p)),
                     pl.BlockSpec((TK, wq), lambda p, j: (j, p)),
                     pl.BlockSpec((TK, LANES), lambda p, j: (j, p))]
        out_shape = [jax.ShapeDtypeStruct((S, n_pairs * wq), F32), jax.ShapeDtypeStruct((S, n_pairs * wq), F32),
                     jax.ShapeDtypeStruct((S, n_pairs * LANES), F32)]
        if has_bias:
            in_specs += [pl.BlockSpec((1, 1, 8, TK), lambda p, j: (p, j, 0, 0))]
            args += [cumt]
            out_specs += [pl.BlockSpec((1, 1, 8, TK), lambda p, j: (p, j, 0, 0)),
                          pl.BlockSpec((1, S, LANES), lambda p, j: (p, 0, 0))]
            out_shape += [jax.ShapeDtypeStruct((n_pairs, S // TK, 8, TK), F32),
                          jax.ShapeDtypeStruct((n_pairs, S, LANES), F32)]
        return pl.pallas_call(
            body, name=name, grid=(n_pairs, S // TK), in_specs=in_specs, out_specs=out_specs, out_shape=out_shape,
            compiler_params=_params(("parallel", "arbitrary")))(*args)

    def fox_post(dcum, z, bf_pad):
        def rev(i):
            return n_tok - 1 - i

        def body(dc_ref, z_ref, b_ref, dff_ref, acc_ref, carry):
            @pl.when(pl.program_id(0) == 0)
            def _():
                carry[...] = jnp.zeros_like(carry)
                acc_ref[...] = jnp.zeros_like(acc_ref)

            _, sig_neg = _log_f_terms(z_ref[...], b_ref[...])
            row = lax.broadcasted_iota(jnp.int32, (TM, TM), 0)
            col = lax.broadcasted_iota(jnp.int32, (TM, TM), 1)
            tri = (col >= row).astype(F32)
            dlog = jnp.dot(tri, dc_ref[...], precision=lax.Precision.HIGHEST, preferred_element_type=F32) + carry[...]
            carry[...] = dlog[0:1, :]
            dff = dlog * sig_neg
            dff_ref[...] = dff
            acc_ref[0:1, :] += jnp.sum(dff, axis=0, keepdims=True)

        return pl.pallas_call(
            body, name="fox_post", grid=(n_tok,),
            in_specs=[pl.BlockSpec((TM, LANES), lambda i: (rev(i), 0)),
                      pl.BlockSpec((TM, LANES), lambda i: (rev(i), misc_blk)), const((1, LANES))],
            out_specs=[pl.BlockSpec((TM, LANES), lambda i: (rev(i), 0)), const((8, LANES))],
            out_shape=[jax.ShapeDtypeStruct((S, LANES), F32), jax.ShapeDtypeStruct((8, LANES), F32)],
            scratch_shapes=[pltpu.VMEM((1, LANES), F32)],
            compiler_params=_params(("arbitrary",)))(dcum, z, bf_pad)

    def mla_post(dq, dk, dv, dff, z, gq, gkv, w_uq, w_uk, w_v, cos_t, sin_t):
        def body(dq_ref, dk_ref, dv_ref, dff_ref, ql_ref, kvl_ref, gq_ref, gkv_ref, wq_ref, wk_ref, wv_ref,
                 c_ref, s_ref, zq_ref, zkv_ref, zms_ref, dwq_ref, dwk_ref, dwv_ref, dgq_ref, dgkv_ref):
            @pl.when(pl.program_id(0) == 0)
            def _():
                for r in (dwq_ref, dwk_ref, dwv_ref, dgq_ref, dgkv_ref):
                    r[...] = jnp.zeros_like(r)

            cos1, sin1 = c_ref[...], s_ref[...]
            gqv, gkvv = gq_ref[...], gkv_ref[...]
            qn, qxh, qrstd = _rms(ql_ref[...], gqv)
            dq_pre = _rope_t(dq_ref[...] * MLA_SCALE, jnp.tile(cos1, (1, MH)), jnp.tile(sin1, (1, MH))).astype(BF16)
            dwq_ref[...] += _tn(qn.astype(BF16), dq_pre)
            dql, dgq = _rms_bwd(_nt(dq_pre, wq_ref[...]), gqv, qxh, qrstd)
            zq_ref[...] = dql.astype(BF16)
            dgq_ref[0:1, :] += dgq

            dkv = dk_ref[...]
            lane = _lane(dkv.shape) & (HEAD_PAD - 1)
            dkn = jnp.where(lane < HEAD_DIM, dkv, 0.0).astype(BF16)
            dkr = dkv[:, 0:HEAD_PAD]
            for hd in range(1, MH):
                dkr = dkr + dkv[:, hd * HEAD_PAD:(hd + 1) * HEAD_PAD]
            lane1 = _lane(dkr.shape)
            dkr = jnp.where((lane1 >= ROPE_LO) & (lane1 < ROPE_LO + ROPE_DIM), dkr, 0.0)
            dkr = _rope_t(dkr, cos1, sin1)
            zms_ref[...] = (dkr + dff_ref[...]).astype(BF16)

            kvn, kxh, krstd = _rms(kvl_ref[...], gkvv)
            kvb = kvn.astype(BF16)
            dvb = dv_ref[...].astype(BF16)
            dwk_ref[...] += _tn(kvb, dkn)
            dwv_ref[...] += _tn(kvb, dvb)
            dkvl, dgkv = _rms_bwd(_nt(dkn, wk_ref[...]) + _nt(dvb, wv_ref[...]), gkvv, kxh, krstd)
            zkv_ref[...] = dkvl.astype(BF16)
            dgkv_ref[0:1, :] += dgkv

        return pl.pallas_call(
            body, name="mla_post", grid=(n_tok,),
            in_specs=[tok(QW), tok(QW), tok(MW), tok(LANES), tok(QL, (cfg.o_ql - ZO) // QL), tok(KVL, (cfg.o_kv - ZO) // KVL),
                      const((1, QL)), const((1, KVL)), const((QL, QW)), const((KVL, QW)), const((KVL, MW)),
                      tok(LANES), tok(LANES)],
            out_specs=[tok(QL), tok(KVL), tok(LANES), const((QL, QW)), const((KVL, QW)), const((KVL, MW)),
                       const((8, QL)), const((8, KVL))],
            out_shape=[jax.ShapeDtypeStruct((S, QL), BF16), jax.ShapeDtypeStruct((S, KVL), BF16),
                       jax.ShapeDtypeStruct((S, LANES), BF16), jax.ShapeDtypeStruct((QL, QW), F32),
                       jax.ShapeDtypeStruct((KVL, QW), F32), jax.ShapeDtypeStruct((KVL, MW), F32),
                       jax.ShapeDtypeStruct((8, QL), F32), jax.ShapeDtypeStruct((8, KVL), F32)],
            compiler_params=_params(("arbitrary",)))(dq, dk, dv, dff, z, z, gq, gkv, w_uq, w_uk, w_v, cos_t, sin_t)

    def bwd_in(dz, w_in, x, dxn, g, scale):
        def body(dz_ref, w_ref, x_ref, dx_ref, g_ref, sc_ref, o_ref, acc_ref):
            @pl.when(pl.program_id(0) == 0)
            def _():
                acc_ref[...] = jnp.zeros_like(acc_ref)

            dh = _nt(dz_ref[...], w_ref[...])
            gv, mod = g_ref[...], 1.0 + sc_ref[...]
            _, xh, rstd = _rms(x_ref[...], gv)
            t = dh * xh
            acc_ref[0:1, :] += jnp.sum(dh, axis=0, keepdims=True)
            acc_ref[1:2, :] += jnp.sum(t * gv, axis=0, keepdims=True)
            acc_ref[2:3, :] += jnp.sum(t * mod, axis=0, keepdims=True)
            dx, _ = _rms_bwd(dh, gv * mod, xh, rstd)
            o_ref[...] = dx_ref[...] + dx

        return pl.pallas_call(
            body, name="bwd_in", grid=(n_tok,),
            in_specs=[tok(NZ), const((D, NZ)), tok(D), tok(D), const((1, D)), const((1, D))],
            out_specs=[tok(D), const((8, D))],
            out_shape=[jax.ShapeDtypeStruct((S, D), F32), jax.ShapeDtypeStruct((8, D), F32)],
            compiler_params=_params(("arbitrary",)))(dz, w_in, x, dxn, g, scale)


    def pair_rows(cum):
        n_pairs = FH // 2
        ct = jnp.pad(cum[:, :FH].T.reshape(n_pairs, 2, S), ((0, 0), (0, 6), (0, 0)))
        return ct.reshape(n_pairs, 8, S // TK, TK).transpose(0, 2, 1, 3)

    def bias_grad(dc, dr):
        n_pairs = FH // 2
        d = dc.transpose(0, 2, 1, 3).reshape(n_pairs, 8, S)[:, :2, :].reshape(FH, S).T
        d = d + dr[:, :, :2].transpose(1, 0, 2).reshape(S, FH)
        return jnp.pad(d, ((0, 0), (0, LANES - FH)))

    def layer_forward(x, wl, mod):
        shift, scale, gate = mod
        h, z, qkv = ln_inproj(x, wl.norm_g, scale, shift, wl.w_in)
        ct = pair_rows(fox_prep(z, wl.bf_pad))
        of, lse_f = attn_fwd(qkv, qkv, qkv, cfg.o_fq // LANES, cfg.o_fk // LANES, cfg.o_fv // LANES, ct,
                             True, False, "fox_fwd")
        qp, kp, vp = mla_prep(z, wl.gq, wl.gkv, wl.w_uq, wl.w_uk, wl.w_v, cfg.cos_t, cfg.sin_t)
        om, lse_m = attn_fwd(qp, kp, vp, 0, 0, 0, None, False, True, "mla_fwd")
        xn, u, y = gate_outproj(of, om, z, wl.w_out, x, gate)
        saved = types.SimpleNamespace(x=x, h=h, z=z, qkv=qkv, ct=ct, of=of, lse_f=lse_f, qp=qp, kp=kp, vp=vp,
                                      om=om, lse_m=lse_m, u=u, y=y)
        return xn, saved

    def layer_backward(dxn, sv, wl, mod):
        shift, scale, gate = mod
        do_f, do_m, dfg, dmg, dy, acc_o, st_f, st_m = bwd_out(dxn, gate, sv.y, wl.w_out, sv.of, sv.om, sv.z,
                                                              sv.lse_f, sv.lse_m)
        dw_out = matmul_tn(sv.u, dy, "dw_out")
        dfq, dfk, dfv, dck, dcr = attn_bwd(sv.qkv, sv.qkv, sv.qkv, do_f, st_f, cfg.o_fq // LANES, cfg.o_fk // LANES,
                                           cfg.o_fv // LANES, sv.ct, True, False, "fox_bwd")
        dff, acc_f = fox_post(bias_grad(dck, dcr), sv.z, wl.bf_pad)
        dqp, dkp, dvp = attn_bwd(sv.qp, sv.kp, sv.vp, do_m, st_m, 0, 0, 0, None, False, True, "mla_bwd")
        zq, zkv, zms, dw_uq, dw_uk, dw_v, dgq, dgkv = mla_post(
            dqp, dkp, dvp, dff, sv.z, wl.gq, wl.gkv, wl.w_uq, wl.w_uk, wl.w_v, cfg.cos_t, cfg.sin_t)
        dz = jnp.concatenate([(dfq * FOX_SCALE).astype(BF16), dfk.astype(BF16), dfv.astype(BF16), dfg, dmg, zq, zkv, zms], axis=1)
        dx, acc_i = bwd_in(dz, wl.w_in, sv.x, dxn, wl.norm_g, scale)
        dw_in = matmul_tn(sv.h, dz, "dw_in")
        grads = types.SimpleNamespace(
            w_in=dw_in, w_out=dw_out, w_uq=dw_uq, w_uk=dw_uk, w_v=dw_v, gq=dgq[0], gkv=dgkv[0],
            b_f=acc_f[0, :FH], norm_g=acc_i[2], dmod=jnp.concatenate([acc_i[0], acc_i[1], acc_o[0]]))
        return dx, grads

    return types.SimpleNamespace(layer_forward=layer_forward, layer_backward=layer_backward, final_loss=final_loss)


def _pack_rows(parts, dtype, row_multiple):
    flat = jnp.concatenate([p.reshape(-1).astype(dtype) for p in parts])
    per = SLAB_COLS * row_multiple
    total = -(-flat.shape[0] // per) * per
    return jnp.pad(flat, (0, total - flat.shape[0])).reshape(total // SLAB_COLS, SLAB_COLS)


def _unpack(flat, shapes):
    out, off = [], 0
    for shp in shapes:
        n = 1
        for d in shp:
            n *= d
        out.append(flat[off:off + n].reshape(shp))
        off += n
    return out


def kernel(x, c, positions, norm_g, w_ada, b_ada, w_in, b_f, q_norm_g, w_uq, kv_norm_g, w_ukv, w_out, final_g, loss_target, m_norm_g, m_w_ada, m_b_ada, m_w_in, m_b_f, m_q_norm_g, m_w_uq, m_kv_norm_g, m_w_ukv, m_w_out, m_final_g, v_norm_g, v_w_ada, v_b_ada, v_w_in, v_b_f, v_q_norm_g, v_w_uq, v_kv_norm_g, v_w_ukv, v_w_out, v_final_g):
    S, D = x.shape[1], x.shape[2]
    L = norm_g.shape[0]
    FH = b_f.shape[1]
    QL, KVL = q_norm_g.shape[1], kv_norm_g.shape[1]
    MH = w_ukv.shape[2] * N_CHIPS // (2 * HEAD_DIM)
    FW, MW = FH * HEAD_DIM, MH * HEAD_DIM
    NA = w_ada.shape[2]
    n_in = w_in.shape[2] * N_CHIPS
    cfg = types.SimpleNamespace(S=S, D=D, FW=FW, MW=MW, QL=QL, KVL=KVL, FH=FH, MH=MH)
    cfg.o_fq, cfg.o_fk, cfg.o_fv, cfg.o_fg, cfg.o_mg = 0, FW, 2 * FW, 3 * FW, 4 * FW
    cfg.o_ql = 4 * FW + MW
    cfg.o_kv = cfg.o_ql + QL
    cfg.o_ms = cfg.o_kv + KVL
    cfg.NZ = cfg.o_ms + LANES
    assert FW == MW and FH % 2 == 0 and MH % 2 == 0 and cfg.o_ql % QL == 0 and cfg.o_kv % KVL == 0 and KVL == LANES
    assert n_in == 4 * FW + FH + QL + KVL + ROPE_DIM + MW

    mx, my, mc = _my_pos()
    my_chip = 2 * mx + my
    my_dev = 2 * my_chip + mc

    inv_freq = 1.0 / (ROPE_THETA ** (jnp.arange(0, ROPE_DIM, 2, dtype=F32) / ROPE_DIM))
    ang = positions[0].astype(F32)[:, None] * inv_freq
    cos, sin = jnp.cos(ang), jnp.sin(ang)
    cfg.cos_t = jnp.concatenate([jnp.ones((S, ROPE_LO), F32), cos, cos, jnp.ones((S, HEAD_PAD - ROPE_LO - ROPE_DIM), F32)], axis=1)
    cfg.sin_t = jnp.concatenate([jnp.zeros((S, ROPE_LO), F32), -sin, sin, jnp.zeros((S, HEAD_PAD - ROPE_LO - ROPE_DIM), F32)], axis=1)

    assert L % 2 == 0
    def lane_pad(a):
        return jnp.pad(a, ((0, 0),) * (a.ndim - 1) + ((0, -a.shape[-1] % LANES),))

    shards = (w_in, w_uq, w_ukv, w_out)
    gathered = weights_gather([lane_pad(w.astype(BF16)) for w in shards], "weights_gather")
    gathered = [g[..., :w.shape[-1]] for g, w in zip(gathered, shards)]
    w_in_f = jnp.concatenate([gathered[0][t] for t in range(N_CHIPS)], axis=2)
    w_uq_f = jnp.concatenate([gathered[1][t] for t in range(N_CHIPS)], axis=2)
    w_ukv_f = jnp.concatenate([gathered[2][t] for t in range(N_CHIPS)], axis=2)
    w_out_f = jnp.concatenate([gathered[3][t] for t in range(N_CHIPS)], axis=1)

    sizes = (FW, FW, FW, FH, FW, QL, KVL, ROPE_DIM, MW)
    offs = [0]
    for sz in sizes:
        offs.append(offs[-1] + sz)
    seg = [w_in_f[:, :, offs[i]:offs[i + 1]] for i in range(len(sizes))]
    fq_w, fk_w, fv_w, ff_w, fg_w, ql_w, kvl_w, kr_w, mg_w = seg
    zeros = lambda n: jnp.zeros((L, D, n), BF16)
    w_in_p = jnp.concatenate([fq_w, fk_w, fv_w, fg_w, mg_w, ql_w, kvl_w, ff_w, zeros(ROPE_LO - FH), kr_w,
                              zeros(HEAD_PAD - ROPE_LO - ROPE_DIM)], axis=2)
    w_uq_p = jnp.pad(w_uq_f.reshape(L, QL, MH, HEAD_DIM + ROPE_DIM), ((0, 0), (0, 0), (0, 0), (0, HEAD_PAD - HEAD_DIM - ROPE_DIM)))
    w_uq_p = w_uq_p.reshape(L, QL, MH * HEAD_PAD)
    w_ukv4 = w_ukv_f.reshape(L, KVL, MH, 2 * HEAD_DIM)
    w_uk_p = jnp.pad(w_ukv4[..., :HEAD_DIM], ((0, 0), (0, 0), (0, 0), (0, HEAD_PAD - HEAD_DIM))).reshape(L, KVL, MH * HEAD_PAD)
    w_v_p = w_ukv4[..., HEAD_DIM:].reshape(L, KVL, MW)

    c_all = allgather8(c.reshape(8, D // 8), "gather_c").reshape(N_DEV, D)
    c_pad = jnp.pad(c_all, ((0, 16 - N_DEV), (0, 0)))
    mod_part = ada_forward(c_pad, w_ada)[:, :N_DEV, :]
    mod_all = allgather8(mod_part.reshape(-1, LANES), "gather_mod").reshape(N_CHIPS, 2, L, N_DEV, NA)[:, 0]
    mod_full = mod_all.transpose(1, 2, 0, 3).reshape(L, N_DEV, N_CHIPS * NA) + b_ada[:, None, :]
    mod_mine = lax.dynamic_index_in_dim(mod_full, my_dev, axis=1, keepdims=True)

    step = make_step(cfg)
    bf_pad = jnp.pad(b_f, ((0, 0), (0, LANES - FH)))
    layers, mods = [], []
    for l in range(L):
        layers.append(types.SimpleNamespace(
            norm_g=norm_g[l][None], w_in=w_in_p[l], bf_pad=bf_pad[l][None], gq=q_norm_g[l][None], gkv=kv_norm_g[l][None],
            w_uq=w_uq_p[l], w_uk=w_uk_p[l], w_v=w_v_p[l], w_out=w_out_f[l]))
        mods.append((mod_mine[l, :, :D], mod_mine[l, :, D:2 * D], mod_mine[l, :, 2 * D:]))

    xl = x[0]
    saved = []
    for l in range(L):
        xl, sv = step.layer_forward(xl, layers[l], mods[l])
        saved.append(sv)
    dx, acc_fin, loss_part = step.final_loss(xl, final_g[None], loss_target[0])
    loss = lax.psum(loss_part[0, 0], ("x", "y", "c"))
    gl = [None] * L
    for l in reversed(range(L)):
        dx, gl[l] = step.layer_backward(dx, saved[l], layers[l], mods[l])
    grad_x = dx[None]

    stack = lambda name: jnp.stack([getattr(g, name) for g in gl])
    small_parts = [stack("norm_g"), stack("dmod"), stack("b_f"), stack("gq"), stack("gkv"), acc_fin[0]]
    small_shapes = [p.shape for p in small_parts]
    small = _pack_rows(small_parts, F32, 8).reshape(-1, LANES)
    small_all = allgather8(small, "gather_small").reshape(N_DEV, -1, LANES)
    small_sum = sum_leading(small_all, "sum_small")
    g_norm_g, g_b_ada, g_b_f, g_q_norm_g, g_kv_norm_g, g_final_g = _unpack(small_sum.reshape(-1), small_shapes)

    n_ng = L * D
    dmod_all = small_all.reshape(N_DEV, -1)[:, n_ng:n_ng + L * 3 * D].reshape(N_DEV, L, 3 * D)
    dmod_cols = lax.dynamic_slice_in_dim(dmod_all, my_chip * NA, NA, axis=2).transpose(1, 0, 2)
    g_w_ada = ada_backward(c_pad, jnp.pad(dmod_cols, ((0, 0), (0, 16 - N_DEV), (0, 0))))

    dw_in_p = stack("w_in")
    ms = cfg.o_ms
    dw_in_f = jnp.concatenate([
        dw_in_p[:, :, 0:3 * FW], dw_in_p[:, :, ms:ms + FH], dw_in_p[:, :, cfg.o_fg:cfg.o_fg + FW],
        dw_in_p[:, :, cfg.o_ql:cfg.o_ql + QL + KVL], dw_in_p[:, :, ms + ROPE_LO:ms + ROPE_LO + ROPE_DIM],
        dw_in_p[:, :, cfg.o_mg:cfg.o_mg + MW]], axis=2)
    dw_uq_f = stack("w_uq").reshape(L, QL, MH, HEAD_PAD)[..., :HEAD_DIM + ROPE_DIM].reshape(L, QL, -1)
    dw_ukv_f = jnp.concatenate([stack("w_uk").reshape(L, KVL, MH, HEAD_PAD)[..., :HEAD_DIM],
                                stack("w_v").reshape(L, KVL, MH, HEAD_DIM)], axis=3).reshape(L, KVL, -1)
    dw_out_f = stack("w_out")
    gs = [dw_in_f.reshape(L, D, N_CHIPS, -1).transpose(2, 0, 1, 3), dw_uq_f.reshape(L, QL, N_CHIPS, -1).transpose(2, 0, 1, 3),
          dw_ukv_f.reshape(L, KVL, N_CHIPS, -1).transpose(2, 0, 1, 3), dw_out_f.reshape(L, N_CHIPS, -1, D).transpose(1, 0, 2, 3)]
    gs = [lane_pad(g) for g in gs]
    hl = L // 2
    theirs = halves_to_sibling(gs, "grads_sibling")
    chip_part = []
    big_names = ["w_in", "w_uq", "w_ukv", "w_out"]
    for g, o, nm in zip(gs, theirs, big_names):
        keep = lax.dynamic_slice_in_dim(g, mc * hl, hl, axis=1)
        merged = (N_CHIPS * hl,) + g.shape[2:]
        chip_part.append(add_cast(keep.reshape(merged), o.reshape(merged), BF16, "grads_chip_sum_" + nm).reshape(o.shape))
    parts = chip_exchange(chip_part, "grads_chips")
    red = [sum_chips(p, "grads_sum_" + nm) for p, nm in zip(parts, big_names)]
    g_w_in, g_w_uq, g_w_ukv, g_w_out = [g[..., :w.shape[-1]] for g, w in zip(halves_gather(red, "grads_back"), shards)]


    names = ["norm_g", "w_ada", "b_ada", "w_in", "b_f", "q_norm_g", "w_uq", "kv_norm_g", "w_ukv", "w_out", "final_g"]
    ws = dict(norm_g=norm_g, w_ada=w_ada, b_ada=b_ada, w_in=w_in, b_f=b_f, q_norm_g=q_norm_g, w_uq=w_uq,
              kv_norm_g=kv_norm_g, w_ukv=w_ukv, w_out=w_out, final_g=final_g)
    msd = dict(norm_g=m_norm_g, w_ada=m_w_ada, b_ada=m_b_ada, w_in=m_w_in, b_f=m_b_f, q_norm_g=m_q_norm_g, w_uq=m_w_uq,
               kv_norm_g=m_kv_norm_g, w_ukv=m_w_ukv, w_out=m_w_out, final_g=m_final_g)
    vsd = dict(norm_g=v_norm_g, w_ada=v_w_ada, b_ada=v_b_ada, w_in=v_w_in, b_f=v_b_f, q_norm_g=v_q_norm_g, w_uq=v_w_uq,
               kv_norm_g=v_kv_norm_g, w_ukv=v_w_ukv, w_out=v_w_out, final_g=v_final_g)
    gsd = dict(norm_g=g_norm_g, w_ada=g_w_ada, b_ada=g_b_ada, w_in=g_w_in, b_f=g_b_f, q_norm_g=g_q_norm_g, w_uq=g_w_uq,
               kv_norm_g=g_kv_norm_g, w_ukv=g_w_ukv, w_out=g_w_out, final_g=g_final_g)
    small_names = ["norm_g", "b_ada", "b_f", "q_norm_g", "kv_norm_g", "final_g"]
    sm_shapes = [ws[n].shape for n in small_names]
    pk = lambda d: _pack_rows([d[n] for n in small_names], F32, 8).reshape(1, -1, LANES)
    sm_out = adamw(pk(ws), pk(gsd), pk(msd), pk(vsd), "adamw_small")
    sm_d, sm_m, sm_v = [dict(zip(small_names, _unpack(o.reshape(-1), sm_shapes))) for o in sm_out]
    delta, new_m, new_v = dict(sm_d), dict(sm_m), dict(sm_v)
    for n in ["w_ada", "w_in", "w_uq", "w_ukv", "w_out"]:
        delta[n], new_m[n], new_v[n] = adamw(ws[n], gsd[n], msd[n], vsd[n], "adamw_" + n)

    return (loss, grad_x, *[gsd[n] for n in names], *[delta[n] for n in names],
            *[new_m[n] for n in names], *[new_v[n] for n in names])
```

```python
import functools
import types

import jax
import jax.numpy as jnp
from jax import lax
from jax.experimental import pallas as pl
from jax.experimental.pallas import tpu as pltpu

F32 = jnp.float32
BF16 = jnp.bfloat16
MESH = pl.DeviceIdType.MESH

N_CHIPS = 4
N_DEV = 8
HEAD_DIM = 64
ROPE_DIM = 32
ROPE_THETA = 10000.0
HEAD_PAD = 128
ROPE_LO = 64
ROPE_HALF = 16
LANES = 128
EPS = 1e-6
NEG = -1e30
ADAM_LR = 0.001
ADAM_B1 = 0.9
ADAM_B2 = 0.999
ADAM_EPS = 1e-08
ADAM_WD = 0.01
ADAM_STEP = 10
VMEM_LIMIT = 48 * 1024 * 1024
SLAB_COLS = 1024


def _params(sem=None, vmem=VMEM_LIMIT):
    return pltpu.CompilerParams(dimension_semantics=sem, vmem_limit_bytes=vmem)


def _nn(a, b):
    return jnp.dot(a, b, preferred_element_type=F32)


def _nt(a, b):
    return lax.dot_general(a, b, (((1,), (1,)), ((), ())), preferred_element_type=F32)


def _tn(a, b):
    return lax.dot_general(a, b, (((0,), (0,)), ((), ())), preferred_element_type=F32)


def _sigmoid(x):
    return 1.0 / (1.0 + jnp.exp(-x))


def _lane(shape):
    return lax.broadcasted_iota(jnp.int32, shape, len(shape) - 1)


def _pick(n, cands):
    for c in cands:
        if n % c == 0:
            return c
    return n


def _my_pos():
    return lax.axis_index("x"), lax.axis_index("y"), lax.axis_index("c")


def allgather8(xs, name):
    m_per, n = xs.shape

    def body(x_ref, out_ref, send_sems, recv_sems, local_sem):
        x, y, c = _my_pos()
        me, sibling = (x, y, c), (x, y, 1 - c)
        chips = [(1 - x, y), (x, 1 - y), (1 - x, 1 - y)]

        def rows(px, py, pc):
            return out_ref.at[pl.ds((4 * px + 2 * py + pc) * m_per, m_per), :]

        def copy(k, block, to, src=None):
            return pltpu.make_async_remote_copy(
                src_ref=rows(*block) if src is None else src, dst_ref=rows(*block),
                send_sem=send_sems.at[k], recv_sem=recv_sems.at[k], device_id=to, device_id_type=MESH)

        mine = pltpu.make_async_copy(x_ref, rows(*me), local_sem)
        mine.start()
        first = [copy(0, me, sibling, src=x_ref)]
        first += [copy(1 + j, me, (*chip, c), src=x_ref) for j, chip in enumerate(chips)]
        for cp in first:
            cp.start()
        passed = [copy(4 + j, (*chip, c), sibling) for j, chip in enumerate(chips)]
        for j, chip in enumerate(chips):
            copy(1 + j, (*chip, c), me).wait_recv()
            passed[j].start()
        copy(0, sibling, me).wait_recv()
        for j, chip in enumerate(chips):
            copy(4 + j, (*chip, 1 - c), me).wait_recv()
        for cp in first + passed:
            cp.wait_send()
        mine.wait()

    return pl.pallas_call(
        body, name=name,
        out_shape=jax.ShapeDtypeStruct((N_DEV * m_per, n), xs.dtype),
        in_specs=[pl.BlockSpec(memory_space=pltpu.VMEM)],
        out_specs=pl.BlockSpec(memory_space=pltpu.VMEM),
        scratch_shapes=[pltpu.SemaphoreType.DMA((7,)), pltpu.SemaphoreType.DMA((7,)), pltpu.SemaphoreType.DMA],
    )(xs)


def _remote(src, dst, send_sems, recv_sems, k, to):
    return pltpu.make_async_remote_copy(src_ref=src, dst_ref=dst, send_sem=send_sems.at[k], recv_sem=recv_sems.at[k],
                                        device_id=to, device_id_type=MESH)


def _hbm_call(body, name, ins, out_shapes, n_sems, aliases=None):
    hbm = pl.BlockSpec(memory_space=pl.ANY)
    scratch = [pltpu.SemaphoreType.DMA((n_sems,)), pltpu.SemaphoreType.DMA((n_sems,))]
    return pl.pallas_call(body, name=name, out_shape=out_shapes, in_specs=[hbm] * len(ins),
                          out_specs=[hbm] * len(out_shapes), scratch_shapes=scratch,
                          input_output_aliases=aliases or {})(*ins)


def _layers_half(ref, h, axis=0):
    size = ref.shape[axis] // 2
    idx = (slice(None),) * axis + (pl.ds(h * size, size),)
    return ref.at[idx]


def weights_gather(ws, name):
    n = len(ws)

    def body(*refs):
        x_refs, o_refs = refs[:n], refs[n:2 * n]
        send_sems, recv_sems = refs[2 * n:]
        x, y, c = _my_pos()
        s = 2 * x + y
        sibling = (x, y, 1 - c)
        chips = [(1 - x, y), (x, 1 - y), (1 - x, 1 - y)]
        sends = []
        for a in range(n):
            for k, (tx, ty) in enumerate(chips):
                cp = _remote(_layers_half(x_refs[a], c), _layers_half(o_refs[a].at[s], c), send_sems, recv_sems,
                             6 * a + k, (tx, ty, c))
                cp.start()
                sends.append(cp)
        for a in range(n):
            for k, (tx, ty) in enumerate(chips):
                got = _layers_half(o_refs[a].at[2 * tx + ty], c)
                _remote(got, got, send_sems, recv_sems, 6 * a + k, (tx, ty, c)).wait_recv()
                cp = _remote(got, got, send_sems, recv_sems, 6 * a + 3 + k, sibling)
                cp.start()
                sends.append(cp)
        for a in range(n):
            for k, (tx, ty) in enumerate(chips):
                theirs = _layers_half(o_refs[a].at[2 * tx + ty], 1 - c)
                _remote(theirs, theirs, send_sems, recv_sems, 6 * a + 3 + k, sibling).wait_recv()
        for cp in sends:
            cp.wait_send()

    outs = [jax.ShapeDtypeStruct((N_CHIPS,) + w.shape, w.dtype) for w in ws]
    return _hbm_call(body, name, ws, outs, 6 * n)


def halves_to_sibling(gs, name):
    n = len(gs)

    def body(*refs):
        x_refs, o_refs, send_sems, recv_sems = refs[:n], refs[n:2 * n], refs[2 * n], refs[2 * n + 1]
        x, y, c = _my_pos()
        cps = [_remote(_layers_half(x_refs[a], 1 - c, axis=1), o_refs[a], send_sems, recv_sems, a, (x, y, 1 - c))
               for a in range(n)]
        for cp in cps:
            cp.start()
        for cp in cps:
            cp.wait()

    outs = [jax.ShapeDtypeStruct((g.shape[0], g.shape[1] // 2) + g.shape[2:], g.dtype) for g in gs]
    return _hbm_call(body, name, gs, outs, n)


def chip_exchange(xs, name):
    n = len(xs)

    def body(*refs):
        x_refs, o_refs = refs[:n], refs[n:2 * n]
        send_sems, recv_sems = refs[2 * n:]
        x, y, c = _my_pos()
        s = 2 * x + y
        chips = [(1 - x, y), (x, 1 - y), (1 - x, 1 - y)]
        sends = []
        for a in range(n):
            for k, (tx, ty) in enumerate(chips):
                cp = _remote(x_refs[a].at[2 * tx + ty], o_refs[a].at[s], send_sems, recv_sems, 3 * a + k, (tx, ty, c))
                cp.start()
                sends.append(cp)
        for a in range(n):
            for k, (tx, ty) in enumerate(chips):
                got = o_refs[a].at[2 * tx + ty]
                _remote(got, got, send_sems, recv_sems, 3 * a + k, (tx, ty, c)).wait_recv()
        for cp in sends:
            cp.wait_send()

    outs = [jax.ShapeDtypeStruct(v.shape, v.dtype) for v in xs]
    return _hbm_call(body, name, xs, outs, 3 * n)


def halves_gather(xs, name):
    n = len(xs)

    def body(*refs):
        x_refs, o_refs, send_sems, recv_sems = refs[:n], refs[n:2 * n], refs[2 * n], refs[2 * n + 1]
        x, y, c = _my_pos()
        sends = []
        for a in range(n):
            cp = _remote(_layers_half(x_refs[a], c), _layers_half(o_refs[a], c), send_sems, recv_sems, a, (x, y, 1 - c))
            cp.start()
            sends.append(cp)
        for a in range(n):
            theirs = _layers_half(o_refs[a], 1 - c)
            _remote(theirs, theirs, send_sems, recv_sems, a, (x, y, 1 - c)).wait_recv()
        for cp in sends:
            cp.wait_send()

    outs = [jax.ShapeDtypeStruct(v.shape, v.dtype) for v in xs]
    return _hbm_call(body, name, xs, outs, n, aliases={a: a for a in range(n)})


def sum_leading(xs, name):
    n, r, c = xs.shape

    def body(x_ref, o_ref):
        acc = x_ref[0]
        for i in range(1, n):
            acc = acc + x_ref[i]
        o_ref[...] = acc

    return pl.pallas_call(body, name=name, out_shape=jax.ShapeDtypeStruct((r, c), xs.dtype))(xs)


def add_cast(a, b, out_dtype, name):
    n, r, c = a.shape
    tr = _pick(r, (512, 256, 128, 64, 32, 16))

    def body(a_ref, b_ref, o_ref):
        o_ref[...] = (a_ref[...] + b_ref[...]).astype(out_dtype)

    spec = pl.BlockSpec((1, tr, c), lambda i, j: (i, j, 0))
    return pl.pallas_call(body, name=name, grid=(n, r // tr), in_specs=[spec, spec], out_specs=spec,
                          out_shape=jax.ShapeDtypeStruct((n, r, c), out_dtype),
                          compiler_params=_params(("parallel", "parallel")))(a, b)


def sum_chips(parts, own, where, name):
    n, nl, r, c = parts.shape
    tr = _pick(r, (512, 256, 128, 64, 32, 16))

    def body(w_ref, p_ref, o_ref, out_ref):
        s = w_ref[0]
        acc = jnp.zeros(out_ref.shape, F32)
        for t in range(n):
            acc = acc + jnp.where(s == t, o_ref[0], p_ref[t]).astype(F32)
        out_ref[...] = acc

    grid_spec = pltpu.PrefetchScalarGridSpec(
        num_scalar_prefetch=1, grid=(nl, r // tr),
        in_specs=[pl.BlockSpec((n, 1, tr, c), lambda i, j, w: (0, i, j, 0)),
                  pl.BlockSpec((1, 1, tr, c), lambda i, j, w: (w[0], i, j, 0))],
        out_specs=pl.BlockSpec((1, tr, c), lambda i, j, w: (w[1] * nl + i, j, 0)))
    return pl.pallas_call(body, name=name, grid_spec=grid_spec, out_shape=jax.ShapeDtypeStruct((2 * nl, r, c), F32),
                          compiler_params=_params(("parallel", "parallel")))(where, parts, own)


def ada_forward(c_all, w_ada):
    nl, d, n = w_ada.shape
    nb = c_all.shape[0]

    def body(c_ref, w_ref, o_ref):
        cv = c_ref[...]
        ca = (cv * _sigmoid(cv)).astype(BF16)
        o_ref[0] = _nn(ca, w_ref[0].astype(BF16))

    return pl.pallas_call(
        body, name="ada_forward", grid=(nl,),
        in_specs=[pl.BlockSpec((nb, d), lambda l: (0, 0)), pl.BlockSpec((1, d, n), lambda l: (l, 0, 0))],
        out_specs=pl.BlockSpec((1, nb, n), lambda l: (l, 0, 0)),
        out_shape=jax.ShapeDtypeStruct((nl, nb, n), F32), compiler_params=_params(("parallel",)))(c_all, w_ada)


def ada_backward(c_all, dmod):
    nl, nb, n = dmod.shape
    d = c_all.shape[1]

    def body(c_ref, g_ref, o_ref):
        cv = c_ref[...]
        ca = (cv * _sigmoid(cv)).astype(BF16)
        o_ref[0] = _tn(ca, g_ref[0].astype(BF16))

    return pl.pallas_call(
        body, name="ada_backward", grid=(nl,),
        in_specs=[pl.BlockSpec((nb, d), lambda l: (0, 0)), pl.BlockSpec((1, nb, n), lambda l: (l, 0, 0))],
        out_specs=pl.BlockSpec((1, d, n), lambda l: (l, 0, 0)),
        out_shape=jax.ShapeDtypeStruct((nl, d, n), F32), compiler_params=_params(("parallel",)))(c_all, dmod)


def matmul_tn(a, b, name):
    k, m = a.shape
    n = b.shape[1]
    tm, tk = _pick(m, (512, 256, 128)), _pick(k, (512, 256, 128))
    tn = n if n * (tm * 8 + tk * 4) <= VMEM_LIMIT // 2 else _pick(n, (512, 256, 128))

    def body(a_ref, b_ref, o_ref):
        @pl.when(pl.program_id(2) == 0)
        def _():
            o_ref[...] = jnp.zeros_like(o_ref)

        o_ref[...] += _tn(a_ref[...], b_ref[...])

    return pl.pallas_call(
        body, name=name, grid=(m // tm, n // tn, k // tk),
        in_specs=[pl.BlockSpec((tk, tm), lambda i, j, kk: (kk, i)), pl.BlockSpec((tk, tn), lambda i, j, kk: (kk, j))],
        out_specs=pl.BlockSpec((tm, tn), lambda i, j, kk: (i, j)),
        out_shape=jax.ShapeDtypeStruct((m, n), F32),
        compiler_params=_params(("parallel", "parallel", "arbitrary")))(a, b)


def adamw(w, g, m, v, name):
    nl, r, c = w.shape
    tr = _pick(r, (512, 256, 128, 64, 32, 16, 8))

    def body(w_ref, g_ref, m_ref, v_ref, d_ref, mo_ref, vo_ref):
        gv = g_ref[...]
        mn = ADAM_B1 * m_ref[...] + (1.0 - ADAM_B1) * gv
        vn = ADAM_B2 * v_ref[...] + (1.0 - ADAM_B2) * (gv * gv)
        m_hat = mn / (1.0 - ADAM_B1 ** ADAM_STEP)
        v_hat = vn / (1.0 - ADAM_B2 ** ADAM_STEP)
        d_ref[...] = -ADAM_LR * (m_hat / (jnp.sqrt(v_hat) + ADAM_EPS) + ADAM_WD * w_ref[...])
        mo_ref[...] = mn
        vo_ref[...] = vn

    spec = pl.BlockSpec((1, tr, c), lambda l, i: (l, i, 0))
    out = jax.ShapeDtypeStruct((nl, r, c), F32)
    return pl.pallas_call(body, name=name, grid=(nl, r // tr), in_specs=[spec] * 4, out_specs=[spec] * 3,
                          out_shape=[out] * 3, compiler_params=_params(("parallel", "parallel")))(w, g, m, v)


def _rope(t, cos_t, sin_t):
    w = t.shape[1]
    lane = _lane(t.shape) & (HEAD_PAD - 1)
    first_half = (lane >= ROPE_LO) & (lane < ROPE_LO + ROPE_HALF)
    partner = jnp.where(first_half, pltpu.roll(t, w - ROPE_HALF, 1), pltpu.roll(t, ROPE_HALF, 1))
    return t * cos_t + partner * sin_t


def _rope_t(dt, cos_t, sin_t):
    w = dt.shape[1]
    lane = _lane(dt.shape) & (HEAD_PAD - 1)
    first_half = (lane >= ROPE_LO) & (lane < ROPE_LO + ROPE_HALF)
    ds = dt * sin_t
    partner = jnp.where(first_half, pltpu.roll(ds, w - ROPE_HALF, 1), pltpu.roll(ds, ROPE_HALF, 1))
    return dt * cos_t + partner


def _rms(xv, g):
    rstd = lax.rsqrt(jnp.mean(xv * xv, axis=-1, keepdims=True) + EPS)
    xh = xv * rstd
    return xh * g, xh, rstd


def _rms_bwd(dy, g, xh, rstd):
    dxh = dy * g
    dx = rstd * (dxh - xh * jnp.mean(dxh * xh, axis=-1, keepdims=True))
    return dx, jnp.sum(dy * xh, axis=0, keepdims=True)


def make_step(cfg):
    S, D, NZ = cfg.S, cfg.D, cfg.NZ
    FW, MW, QL, KVL, FH, MH = cfg.FW, cfg.MW, cfg.QL, cfg.KVL, cfg.FH, cfg.MH
    QW = MH * HEAD_PAD
    TM = _pick(S, (256, 128))
    TQ = TK = _pick(S, (256, 128))
    n_tok = S // TM
    ZO = 3 * FW
    NZR = NZ - ZO
    misc_blk = (cfg.o_ms - ZO) // LANES
    FOX_SCALE = HEAD_DIM ** -0.5
    MLA_SCALE = (HEAD_DIM + ROPE_DIM) ** -0.5

    def tok(width, col=0):
        return pl.BlockSpec((TM, width), lambda i: (i, col))

    def const(shape):
        return pl.BlockSpec(shape, lambda i: tuple(0 for _ in shape))


    def ln_inproj(x, g, scale, shift, w_in):
        def body(x_ref, g_ref, sc_ref, sh_ref, w_ref, h_ref, z_ref, qkv_ref):
            y, _, _ = _rms(x_ref[...], g_ref[...])
            hb = (y * (1.0 + sc_ref[...]) + sh_ref[...]).astype(BF16)
            h_ref[...] = hb
            z = _nn(hb, w_ref[...])
            z_ref[...] = z[:, ZO:]
            qkv_ref[:, :FW] = (z[:, :FW] * FOX_SCALE).astype(BF16)
            qkv_ref[:, FW:] = z[:, FW:ZO].astype(BF16)

        return pl.pallas_call(
            body, name="ln_inproj", grid=(n_tok,),
            in_specs=[tok(D), const((1, D)), const((1, D)), const((1, D)), const((D, NZ))],
            out_specs=[tok(D), tok(NZR), tok(ZO)],
            out_shape=[jax.ShapeDtypeStruct((S, D), BF16), jax.ShapeDtypeStruct((S, NZR), F32),
                       jax.ShapeDtypeStruct((S, ZO), BF16)],
            compiler_params=_params(("parallel",)))(x, g, scale, shift, w_in)

    def _log_f_terms(misc, bf):
        lane = _lane(misc.shape)
        a = misc + bf
        e = jnp.exp(-jnp.abs(a))
        logf = jnp.minimum(a, 0.0) - jnp.log(1.0 + e)
        sig_neg = jnp.where(a >= 0, e, 1.0) / (1.0 + e)
        valid = lane < FH
        return jnp.where(valid, logf, 0.0), jnp.where(valid, sig_neg, 0.0)

    def fox_prep(z, bf_pad):
        def body(z_ref, b_ref, o_ref, carry):
            @pl.when(pl.program_id(0) == 0)
            def _():
                carry[...] = jnp.zeros_like(carry)

            logf, _ = _log_f_terms(z_ref[...], b_ref[...])
            row = lax.broadcasted_iota(jnp.int32, (TM, TM), 0)
            col = lax.broadcasted_iota(jnp.int32, (TM, TM), 1)
            tri = (col <= row).astype(F32)
            cum = jnp.dot(tri, logf, precision=lax.Precision.HIGHEST, preferred_element_type=F32) + carry[...]
            o_ref[...] = cum
            carry[...] = cum[TM - 1:TM, :]

        return pl.pallas_call(
            body, name="fox_prep", grid=(n_tok,),
            in_specs=[tok(LANES, misc_blk), const((1, LANES))], out_specs=tok(LANES),
            out_shape=jax.ShapeDtypeStruct((S, LANES), F32),
            scratch_shapes=[pltpu.VMEM((1, LANES), F32)],
            compiler_params=_params(("arbitrary",)))(z, bf_pad)

    def mla_prep(z, gq, gkv, w_uq, w_uk, w_v, cos_t, sin_t):
        def body(ql_ref, kvl_ref, ms_ref, gq_ref, gkv_ref, wq_ref, wk_ref, wv_ref, c_ref, s_ref, q_ref, k_ref, v_ref):
            cos1, sin1 = c_ref[...], s_ref[...]
            qn, _, _ = _rms(ql_ref[...], gq_ref[...])
            q = _nn(qn.astype(BF16), wq_ref[...])
            q_ref[...] = (_rope(q, jnp.tile(cos1, (1, MH)), jnp.tile(sin1, (1, MH))) * MLA_SCALE).astype(BF16)
            kvn, _, _ = _rms(kvl_ref[...], gkv_ref[...])
            kvb = kvn.astype(BF16)
            lane = _lane((TM, LANES))
            kr = jnp.where((lane >= ROPE_LO) & (lane < ROPE_LO + ROPE_DIM), ms_ref[...], 0.0)
            kr = _rope(kr, cos1, sin1)
            k_ref[...] = (_nn(kvb, wk_ref[...]) + jnp.tile(kr, (1, MH))).astype(BF16)
            v_ref[...] = _nn(kvb, wv_ref[...]).astype(BF16)

        return pl.pallas_call(
            body, name="mla_prep", grid=(n_tok,),
            in_specs=[tok(QL, (cfg.o_ql - ZO) // QL), tok(KVL, (cfg.o_kv - ZO) // KVL), tok(LANES, misc_blk),
                      const((1, QL)), const((1, KVL)), const((QL, QW)), const((KVL, QW)), const((KVL, MW)),
                      tok(LANES), tok(LANES)],
            out_specs=[tok(QW), tok(QW), tok(MW)],
            out_shape=[jax.ShapeDtypeStruct((S, QW), BF16), jax.ShapeDtypeStruct((S, QW), BF16),
                       jax.ShapeDtypeStruct((S, MW), BF16)],
            compiler_params=_params(("parallel",)))(z, z, z, gq, gkv, w_uq, w_uk, w_v, cos_t, sin_t)

    def _allowed(q0, k0, chunked):
        qi = q0 + lax.broadcasted_iota(jnp.int32, (TQ, TK), 0)
        ki = k0 + lax.broadcasted_iota(jnp.int32, (TQ, TK), 1)
        if chunked:
            return (ki >> 6) <= (qi >> 6)
        return ki <= qi

    def _heads(val, packed):
        if packed:
            lane = _lane(val.shape)
            zero = jnp.zeros_like(val)
            return [jnp.where(lane < HEAD_DIM, val, zero), jnp.where(lane >= HEAD_DIM, val, zero)]
        return [val[:, :HEAD_PAD], val[:, HEAD_PAD:]]

    def _merge(a0, a1):
        return jnp.where(_lane(a0.shape) < HEAD_DIM, a0, a1)

    def attn_fwd(q, k, v, q_blk0, k_blk0, v_blk0, cumt, packed, chunked, name):
        has_bias = cumt is not None
        n_pairs = (FH if packed else MH) // 2
        wq = LANES if packed else 2 * HEAD_PAD
        assert TQ == TK

        def body(*refs):
            if has_bias:
                q_ref, k_ref, v_ref, ct_ref, o_ref, lse_ref = refs
            else:
                q_ref, k_ref, v_ref, o_ref, lse_ref = refs
            i = pl.program_id(1)
            q0 = i * TQ
            qh = _heads(q_ref[...], packed)
            q_both = jnp.concatenate(qh, axis=0) if packed else None

            def scores(kb):
                kblk = k_ref[pl.ds(pl.multiple_of(kb * TK, TK), TK), :]
                if packed:
                    return _nt(q_both, kblk)
                return jnp.concatenate([_nt(qh[0], kblk[:, :HEAD_PAD]), _nt(qh[1], kblk[:, HEAD_PAD:])], axis=0)

            def update(kb, s, carry, masked):
                m, l, acc = carry
                k0 = pl.multiple_of(kb * TK, TK)
                s0, s1 = s[:TQ], s[TQ:]
                if has_bias:
                    ck = ct_ref[0, kb]
                    s0, s1 = s0 - ck[0:1, :], s1 - ck[1:2, :]
                if masked:
                    allow = _allowed(q0, k0, chunked)
                    s0, s1 = jnp.where(allow, s0, NEG), jnp.where(allow, s1, NEG)
                s = jnp.concatenate([s0, s1], axis=0)
                m_new = jnp.maximum(m, jnp.max(s, axis=-1, keepdims=True))
                p = jnp.exp(s - m_new)
                alpha = jnp.exp(m - m_new)
                l = alpha * l + jnp.sum(p, axis=-1, keepdims=True)
                acc = alpha * acc + _nn(p.astype(BF16), v_ref[pl.ds(k0, TK), :])
                return m_new, l, acc

            def two(kb, carry, last_masked):
                s_a, s_b = scores(kb), scores(kb + 1)
                return update(kb + 1, s_b, update(kb, s_a, carry, False), last_masked)

            init = (jnp.full((2 * TQ, 1), NEG, F32), jnp.zeros((2 * TQ, 1), F32), jnp.zeros((2 * TQ, LANES), F32))
            carry = lax.fori_loop(0, i // 2, lambda t, cr: two(2 * t, cr, False), init)
            m, l, acc = lax.cond(i % 2 == 1, lambda cr: two(i - 1, cr, True),
                                 lambda cr: update(i, scores(i), cr, True), carry)
            o = acc / l
            lse = m + jnp.log(l)
            lane = _lane((TQ, LANES))
            o_ref[...] = _merge(o[:TQ], o[TQ:])
            lse_ref[...] = jnp.where(lane == 0, lse[:TQ], jnp.where(lane == 1, lse[TQ:], 0.0))

        in_specs = [pl.BlockSpec((TQ, wq), lambda p, i: (i, q_blk0 + p)),
                    pl.BlockSpec((S, wq), lambda p, i: (0, k_blk0 + p)),
                    pl.BlockSpec((S, LANES), lambda p, i: (0, v_blk0 + p))]
        args = [q, k, v]
        if has_bias:
            in_specs += [pl.BlockSpec((1, S // TK, 8, TK), lambda p, i: (p, 0, 0, 0))]
            args += [cumt]
        return pl.pallas_call(
            body, name=name, grid=(n_pairs, S // TQ), in_specs=in_specs,
            out_specs=[pl.BlockSpec((TQ, LANES), lambda p, i: (i, p)),
                       pl.BlockSpec((TQ, LANES), lambda p, i: (i, p))],
            out_shape=[jax.ShapeDtypeStruct((S, n_pairs * LANES), F32),
                       jax.ShapeDtypeStruct((S, n_pairs * LANES), F32)],
            compiler_params=_params(("parallel", "parallel")))(*args)

    def gate_outproj(of, om, z, w_out, x, gate):
        def body(of_ref, om_ref, fg_ref, mg_ref, w_ref, x_ref, gt_ref, xn_ref, u_ref, y_ref):
            fg, mg = fg_ref[...], mg_ref[...]
            u = jnp.concatenate([of_ref[...] * fg * _sigmoid(fg), om_ref[...] * mg * _sigmoid(mg)], axis=1).astype(BF16)
            y = _nn(u, w_ref[...])
            u_ref[...] = u
            y_ref[...] = y.astype(BF16)
            xn_ref[...] = x_ref[...] + gt_ref[...] * y

        return pl.pallas_call(
            body, name="gate_outproj", grid=(n_tok,),
            in_specs=[tok(FW), tok(MW), tok(FW, (cfg.o_fg - ZO) // FW), tok(MW, (cfg.o_mg - ZO) // MW),
                      const((FW + MW, D)), tok(D), const((1, D))],
            out_specs=[tok(D), tok(FW + MW), tok(D)],
            out_shape=[jax.ShapeDtypeStruct((S, D), F32), jax.ShapeDtypeStruct((S, FW + MW), BF16),
                       jax.ShapeDtypeStruct((S, D), BF16)],
            compiler_params=_params(("parallel",)))(of, om, z, z, w_out, x, gate)

    def final_loss(x, g, target):
        def body(x_ref, g_ref, t_ref, dx_ref, acc_ref, loss_ref):
            @pl.when(pl.program_id(0) == 0)
            def _():
                acc_ref[...] = jnp.zeros_like(acc_ref)
                loss_ref[...] = jnp.zeros_like(loss_ref)

            gv = g_ref[...]
            y, xh, rstd = _rms(x_ref[...], gv)
            e = y - t_ref[...]
            loss_ref[...] += 0.5 * jnp.sum(jnp.sum(e * e, axis=-1, keepdims=True) / D, axis=0, keepdims=True)
            dx, dg = _rms_bwd(e / D, gv, xh, rstd)
            dx_ref[...] = dx
            acc_ref[0:1, :] += dg

        return pl.pallas_call(
            body, name="final_loss", grid=(n_tok,),
            in_specs=[tok(D), const((1, D)), tok(D)],
            out_specs=[tok(D), const((8, D)), const((1, LANES))],
            out_shape=[jax.ShapeDtypeStruct((S, D), F32), jax.ShapeDtypeStruct((8, D), F32),
                       jax.ShapeDtypeStruct((1, LANES), F32)],
            compiler_params=_params(("arbitrary",)))(x, g, target)


    def bwd_out(dxn, gate, y, w_out, of, om, z, lse_f, lse_m):
        def body(dx_ref, gt_ref, y_ref, w_ref, of_ref, om_ref, fg_ref, mg_ref, lf_ref, lm_ref,
                 dof_ref, dom_ref, dfg_ref, dmg_ref, dy_ref, acc_ref, sf_ref, sm_ref):
            @pl.when(pl.program_id(0) == 0)
            def _():
                acc_ref[...] = jnp.zeros_like(acc_ref)

            dxv = dx_ref[...]
            acc_ref[0:1, :] += jnp.sum(dxv * y_ref[...].astype(F32), axis=0, keepdims=True)
            dy = (gt_ref[...] * dxv).astype(BF16)
            dy_ref[...] = dy
            du = _nt(dy, w_ref[...])
            lane = _lane((TM, LANES))
            for lo, width, o_ref, g_ref, do_ref, dg_ref, l_ref, st_ref in (
                    (0, FW, of_ref, fg_ref, dof_ref, dfg_ref, lf_ref, sf_ref),
                    (FW, MW, om_ref, mg_ref, dom_ref, dmg_ref, lm_ref, sm_ref)):
                gv, ov = g_ref[...], o_ref[...]
                sg = _sigmoid(gv)
                dup = du[:, lo:lo + width]
                dob = (dup * gv * sg).astype(BF16)
                do_ref[...] = dob
                dg_ref[...] = (dup * ov * sg * (1.0 + gv * (1.0 - sg))).astype(BF16)
                d = dob.astype(F32) * ov
                for pr in range(width // LANES):
                    cols = slice(pr * LANES, (pr + 1) * LANES)
                    dp = d[:, cols]
                    d0 = jnp.sum(jnp.where(lane < HEAD_DIM, dp, 0.0), axis=-1, keepdims=True)
                    d1 = jnp.sum(jnp.where(lane >= HEAD_DIM, dp, 0.0), axis=-1, keepdims=True)
                    st_ref[:, cols] = jnp.where(lane == 2, d0, jnp.where(lane == 3, d1, l_ref[:, cols]))

        return pl.pallas_call(
            body, name="bwd_out", grid=(n_tok,),
            in_specs=[tok(D), const((1, D)), tok(D), const((FW + MW, D)), tok(FW), tok(MW),
                      tok(FW, (cfg.o_fg - ZO) // FW), tok(MW, (cfg.o_mg - ZO) // MW), tok(FW), tok(MW)],
            out_specs=[tok(FW), tok(MW), tok(FW), tok(MW), tok(D), const((8, D)), tok(FW), tok(MW)],
            out_shape=[jax.ShapeDtypeStruct((S, FW), BF16), jax.ShapeDtypeStruct((S, MW), BF16),
                       jax.ShapeDtypeStruct((S, FW), BF16), jax.ShapeDtypeStruct((S, MW), BF16),
                       jax.ShapeDtypeStruct((S, D), BF16), jax.ShapeDtypeStruct((8, D), F32),
                       jax.ShapeDtypeStruct((S, FW), F32), jax.ShapeDtypeStruct((S, MW), F32)],
            compiler_params=_params(("arbitrary",)))(dxn, gate, y, w_out, of, om, z, z, lse_f, lse_m)

    def attn_bwd(q, k, v, do, stats, q_blk0, k_blk0, v_blk0, cumt, packed, chunked, name):
        has_bias = cumt is not None
        n_pairs = (FH if packed else MH) // 2
        wq = LANES if packed else 2 * HEAD_PAD
        n_q = S // TQ
        assert TQ == TK

        def body(*refs):
            if has_bias:
                q_ref, k_ref, v_ref, do_ref, st_ref, ct_ref, dq_ref, dk_ref, dv_ref, dc_ref, dr_ref = refs
            else:
                q_ref, k_ref, v_ref, do_ref, st_ref, dq_ref, dk_ref, dv_ref = refs
            jb = pl.program_id(1)
            k0 = jb * TK

            @pl.when(jb == 0)
            def _():
                dq_ref[...] = jnp.zeros_like(dq_ref)
                if has_bias:
                    dr_ref[...] = jnp.zeros_like(dr_ref)

            dk_ref[...] = jnp.zeros_like(dk_ref)
            dv_ref[...] = jnp.zeros_like(dv_ref)
            kh = _heads(k_ref[...], packed)
            k_both = jnp.concatenate(kh, axis=0) if packed else None
            v_both = jnp.concatenate(_heads(v_ref[...], True), axis=0)
            ck = jnp.concatenate([ct_ref[0, 0][0:1, :], ct_ref[0, 0][1:2, :]], axis=1) if has_bias else None

            def products(ib):
                rows = pl.ds(pl.multiple_of(ib * TQ, TQ), TQ)
                q2, do2 = q_ref[rows, :], do_ref[rows, :]
                if packed:
                    s = _nt(q2, k_both)
                else:
                    qh = _heads(q2, False)
                    s = jnp.concatenate([_nt(qh[0], kh[0]), _nt(qh[1], kh[1])], axis=1)
                return s, _nt(do2, v_both)

            def update(ib, s, dp, carry, masked):
                q0 = pl.multiple_of(ib * TQ, TQ)
                rows = pl.ds(q0, TQ)
                q2, do2, st = q_ref[rows, :], do_ref[rows, :], st_ref[rows, :]
                if not packed:
                    qh = _heads(q2, False)
                if has_bias:
                    s = s - ck
                if masked:
                    allow = _allowed(q0, k0, chunked)
                    s = jnp.where(jnp.concatenate([allow, allow], axis=1), s, NEG)
                p = jnp.concatenate([jnp.exp(s[:, :TK] - st[:, 0:1]), jnp.exp(s[:, TK:] - st[:, 1:2])], axis=1)
                dv2 = _tn(p.astype(BF16), do2)
                ds = jnp.concatenate([p[:, :TK] * (dp[:, :TK] - st[:, 2:3]), p[:, TK:] * (dp[:, TK:] - st[:, 3:4])], axis=1)
                dsb = ds.astype(BF16)
                dv_ref[...] += _merge(dv2[:TK], dv2[TK:])
                if packed:
                    dk2 = _tn(dsb, q2)
                    dk_ref[...] += _merge(dk2[:TK], dk2[TK:])
                    dq_ref[rows, :] += _nn(dsb, k_both)
                else:
                    dk_ref[...] += jnp.concatenate([_tn(dsb[:, :TK], qh[0]), _tn(dsb[:, TK:], qh[1])], axis=1)
                    dq_ref[rows, :] += jnp.concatenate([_nn(dsb[:, :TK], kh[0]), _nn(dsb[:, TK:], kh[1])], axis=1)
                if has_bias:
                    lane = _lane((TQ, LANES))
                    r0 = jnp.sum(ds[:, :TK], axis=-1, keepdims=True)
                    r1 = jnp.sum(ds[:, TK:], axis=-1, keepdims=True)
                    dr_ref[0, rows, :] += jnp.where(lane == 0, r0, jnp.where(lane == 1, r1, 0.0))
                    return carry - jnp.sum(ds, axis=0, keepdims=True)
                return carry

            def step(ib, carry, masked):
                s, dp = products(ib)
                return update(ib, s, dp, carry, masked)

            def two(ib, carry):
                s_a, dp_a = products(ib)
                s_b, dp_b = products(ib + 1)
                return update(ib + 1, s_b, dp_b, update(ib, s_a, dp_a, carry, False), False)

            dc = step(jb, jnp.zeros((1, 2 * TK), F32), True)
            n_rest = n_q - 1 - jb
            dc = lax.fori_loop(0, n_rest // 2, lambda t, cr: two(jb + 1 + 2 * t, cr), dc)
            dc = lax.cond(n_rest % 2 == 1, lambda cr: step(n_q - 1, cr, False), lambda cr: cr, dc)
            if has_bias:
                sub = lax.broadcasted_iota(jnp.int32, (8, TK), 0)
                dc_ref[0, 0] = jnp.where(sub == 0, dc[:, :TK], jnp.where(sub == 1, dc[:, TK:], 0.0))

        in_specs = [pl.BlockSpec((S, wq), lambda p, j: (0, q_blk0 + p)),
                    pl.BlockSpec((TK, wq), lambda p, j: (j, k_blk0 + p)),
                    pl.BlockSpec((TK, LANES), lambda p, j: (j, v_blk0 + p)),
                    pl.BlockSpec((S, LANES), lambda p, j: (0, p)),
                    pl.BlockSpec((S, LANES), lambda p, j: (0, p))]
        args = [q, k, v, do, stats]
        out_specs = [pl.BlockSpec((S, wq), lambda p, j: (0, p)),
                     pl.BlockSpec((TK, wq), lambda p, j: (j, p)),
                     pl.BlockSpec((TK, LANES), lambda p, j: (j, p))]
        out_shape = [jax.ShapeDtypeStruct((S, n_pairs * wq), F32), jax.ShapeDtypeStruct((S, n_pairs * wq), F32),
                     jax.ShapeDtypeStruct((S, n_pairs * LANES), F32)]
        if has_bias:
            in_specs += [pl.BlockSpec((1, 1, 8, TK), lambda p, j: (p, j, 0, 0))]
            args += [cumt]
            out_specs += [pl.BlockSpec((1, 1, 8, TK), lambda p, j: (p, j, 0, 0)),
                          pl.BlockSpec((1, S, LANES), lambda p, j: (p, 0, 0))]
            out_shape += [jax.ShapeDtypeStruct((n_pairs, S // TK, 8, TK), F32),
                          jax.ShapeDtypeStruct((n_pairs, S, LANES), F32)]
        return pl.pallas_call(
            body, name=name, grid=(n_pairs, S // TK), in_specs=in_specs, out_specs=out_specs, out_shape=out_shape,
            compiler_params=_params(("parallel", "arbitrary")))(*args)

    def fox_post(dcum, z, bf_pad):
        def rev(i):
            return n_tok - 1 - i

        def body(dc_ref, z_ref, b_ref, dff_ref, acc_ref, carry):
            @pl.when(pl.program_id(0) == 0)
            def _():
                carry[...] = jnp.zeros_like(carry)
                acc_ref[...] = jnp.zeros_like(acc_ref)

            _, sig_neg = _log_f_terms(z_ref[...], b_ref[...])
            row = lax.broadcasted_iota(jnp.int32, (TM, TM), 0)
            col = lax.broadcasted_iota(jnp.int32, (TM, TM), 1)
            tri = (col >= row).astype(F32)
            dlog = jnp.dot(tri, dc_ref[...], precision=lax.Precision.HIGHEST, preferred_element_type=F32) + carry[...]
            carry[...] = dlog[0:1, :]
            dff = dlog * sig_neg
            dff_ref[...] = dff
            acc_ref[0:1, :] += jnp.sum(dff, axis=0, keepdims=True)

        return pl.pallas_call(
            body, name="fox_post", grid=(n_tok,),
            in_specs=[pl.BlockSpec((TM, LANES), lambda i: (rev(i), 0)),
                      pl.BlockSpec((TM, LANES), lambda i: (rev(i), misc_blk)), const((1, LANES))],
            out_specs=[pl.BlockSpec((TM, LANES), lambda i: (rev(i), 0)), const((8, LANES))],
            out_shape=[jax.ShapeDtypeStruct((S, LANES), F32), jax.ShapeDtypeStruct((8, LANES), F32)],
            scratch_shapes=[pltpu.VMEM((1, LANES), F32)],
            compiler_params=_params(("arbitrary",)))(dcum, z, bf_pad)

    def mla_post(dq, dk, dv, dff, z, gq, gkv, w_uq, w_uk, w_v, cos_t, sin_t):
        def body(dq_ref, dk_ref, dv_ref, dff_ref, ql_ref, kvl_ref, gq_ref, gkv_ref, wq_ref, wk_ref, wv_ref,
                 c_ref, s_ref, zq_ref, zkv_ref, zms_ref, dwq_ref, dwk_ref, dwv_ref, dgq_ref, dgkv_ref):
            @pl.when(pl.program_id(0) == 0)
            def _():
                for r in (dwq_ref, dwk_ref, dwv_ref, dgq_ref, dgkv_ref):
                    r[...] = jnp.zeros_like(r)

            cos1, sin1 = c_ref[...], s_ref[...]
            gqv, gkvv = gq_ref[...], gkv_ref[...]
            qn, qxh, qrstd = _rms(ql_ref[...], gqv)
            dq_pre = _rope_t(dq_ref[...] * MLA_SCALE, jnp.tile(cos1, (1, MH)), jnp.tile(sin1, (1, MH))).astype(BF16)
            dwq_ref[...] += _tn(qn.astype(BF16), dq_pre)
            dql, dgq = _rms_bwd(_nt(dq_pre, wq_ref[...]), gqv, qxh, qrstd)
            zq_ref[...] = dql.astype(BF16)
            dgq_ref[0:1, :] += dgq

            dkv = dk_ref[...]
            lane = _lane(dkv.shape) & (HEAD_PAD - 1)
            dkn = jnp.where(lane < HEAD_DIM, dkv, 0.0).astype(BF16)
            dkr = dkv[:, 0:HEAD_PAD]
            for hd in range(1, MH):
                dkr = dkr + dkv[:, hd * HEAD_PAD:(hd + 1) * HEAD_PAD]
            lane1 = _lane(dkr.shape)
            dkr = jnp.where((lane1 >= ROPE_LO) & (lane1 < ROPE_LO + ROPE_DIM), dkr, 0.0)
            dkr = _rope_t(dkr, cos1, sin1)
            zms_ref[...] = (dkr + dff_ref[...]).astype(BF16)

            kvn, kxh, krstd = _rms(kvl_ref[...], gkvv)
            kvb = kvn.astype(BF16)
            dvb = dv_ref[...].astype(BF16)
            dwk_ref[...] += _tn(kvb, dkn)
            dwv_ref[...] += _tn(kvb, dvb)
            dkvl, dgkv = _rms_bwd(_nt(dkn, wk_ref[...]) + _nt(dvb, wv_ref[...]), gkvv, kxh, krstd)
            zkv_ref[...] = dkvl.astype(BF16)
            dgkv_ref[0:1, :] += dgkv

        return pl.pallas_call(
            body, name="mla_post", grid=(n_tok,),
            in_specs=[tok(QW), tok(QW), tok(MW), tok(LANES), tok(QL, (cfg.o_ql - ZO) // QL), tok(KVL, (cfg.o_kv - ZO) // KVL),
                      const((1, QL)), const((1, KVL)), const((QL, QW)), const((KVL, QW)), const((KVL, MW)),
                      tok(LANES), tok(LANES)],
            out_specs=[tok(QL), tok(KVL), tok(LANES), const((QL, QW)), const((KVL, QW)), const((KVL, MW)),
                       const((8, QL)), const((8, KVL))],
            out_shape=[jax.ShapeDtypeStruct((S, QL), BF16), jax.ShapeDtypeStruct((S, KVL), BF16),
                       jax.ShapeDtypeStruct((S, LANES), BF16), jax.ShapeDtypeStruct((QL, QW), F32),
                       jax.ShapeDtypeStruct((KVL, QW), F32), jax.ShapeDtypeStruct((KVL, MW), F32),
                       jax.ShapeDtypeStruct((8, QL), F32), jax.ShapeDtypeStruct((8, KVL), F32)],
            compiler_params=_params(("arbitrary",)))(dq, dk, dv, dff, z, z, gq, gkv, w_uq, w_uk, w_v, cos_t, sin_t)

    def bwd_in(dz, w_in, x, dxn, g, scale):
        def body(dz_ref, w_ref, x_ref, dx_ref, g_ref, sc_ref, o_ref, acc_ref):
            @pl.when(pl.program_id(0) == 0)
            def _():
                acc_ref[...] = jnp.zeros_like(acc_ref)

            dh = _nt(dz_ref[...], w_ref[...])
            gv, mod = g_ref[...], 1.0 + sc_ref[...]
            _, xh, rstd = _rms(x_ref[...], gv)
            t = dh * xh
            acc_ref[0:1, :] += jnp.sum(dh, axis=0, keepdims=True)
            acc_ref[1:2, :] += jnp.sum(t * gv, axis=0, keepdims=True)
            acc_ref[2:3, :] += jnp.sum(t * mod, axis=0, keepdims=True)
            dx, _ = _rms_bwd(dh, gv * mod, xh, rstd)
            o_ref[...] = dx_ref[...] + dx

        return pl.pallas_call(
            body, name="bwd_in", grid=(n_tok,),
            in_specs=[tok(NZ), const((D, NZ)), tok(D), tok(D), const((1, D)), const((1, D))],
            out_specs=[tok(D), const((8, D))],
            out_shape=[jax.ShapeDtypeStruct((S, D), F32), jax.ShapeDtypeStruct((8, D), F32)],
            compiler_params=_params(("arbitrary",)))(dz, w_in, x, dxn, g, scale)


    def pair_rows(cum):
        n_pairs = FH // 2
        ct = jnp.pad(cum[:, :FH].T.reshape(n_pairs, 2, S), ((0, 0), (0, 6), (0, 0)))
        return ct.reshape(n_pairs, 8, S // TK, TK).transpose(0, 2, 1, 3)

    def bias_grad(dc, dr):
        n_pairs = FH // 2
        d = dc.transpose(0, 2, 1, 3).reshape(n_pairs, 8, S)[:, :2, :].reshape(FH, S).T
        d = d + dr[:, :, :2].transpose(1, 0, 2).reshape(S, FH)
        return jnp.pad(d, ((0, 0), (0, LANES - FH)))

    def layer_forward(x, wl, mod):
        shift, scale, gate = mod
        h, z, qkv = ln_inproj(x, wl.norm_g, scale, shift, wl.w_in)
        ct = pair_rows(fox_prep(z, wl.bf_pad))
        of, lse_f = attn_fwd(qkv, qkv, qkv, cfg.o_fq // LANES, cfg.o_fk // LANES, cfg.o_fv // LANES, ct,
                             True, False, "fox_fwd")
        qp, kp, vp = mla_prep(z, wl.gq, wl.gkv, wl.w_uq, wl.w_uk, wl.w_v, cfg.cos_t, cfg.sin_t)
        om, lse_m = attn_fwd(qp, kp, vp, 0, 0, 0, None, False, True, "mla_fwd")
        xn, u, y = gate_outproj(of, om, z, wl.w_out, x, gate)
        saved = types.SimpleNamespace(x=x, h=h, z=z, qkv=qkv, ct=ct, of=of, lse_f=lse_f, qp=qp, kp=kp, vp=vp,
                                      om=om, lse_m=lse_m, u=u, y=y)
        return xn, saved

    def layer_backward(dxn, sv, wl, mod):
        shift, scale, gate = mod
        do_f, do_m, dfg, dmg, dy, acc_o, st_f, st_m = bwd_out(dxn, gate, sv.y, wl.w_out, sv.of, sv.om, sv.z,
                                                              sv.lse_f, sv.lse_m)
        dw_out = matmul_tn(sv.u, dy, "dw_out")
        dfq, dfk, dfv, dck, dcr = attn_bwd(sv.qkv, sv.qkv, sv.qkv, do_f, st_f, cfg.o_fq // LANES, cfg.o_fk // LANES,
                                           cfg.o_fv // LANES, sv.ct, True, False, "fox_bwd")
        dff, acc_f = fox_post(bias_grad(dck, dcr), sv.z, wl.bf_pad)
        dqp, dkp, dvp = attn_bwd(sv.qp, sv.kp, sv.vp, do_m, st_m, 0, 0, 0, None, False, True, "mla_bwd")
        zq, zkv, zms, dw_uq, dw_uk, dw_v, dgq, dgkv = mla_post(
            dqp, dkp, dvp, dff, sv.z, wl.gq, wl.gkv, wl.w_uq, wl.w_uk, wl.w_v, cfg.cos_t, cfg.sin_t)
        dz = jnp.concatenate([(dfq * FOX_SCALE).astype(BF16), dfk.astype(BF16), dfv.astype(BF16), dfg, dmg, zq, zkv, zms], axis=1)
        dx, acc_i = bwd_in(dz, wl.w_in, sv.x, dxn, wl.norm_g, scale)
        dw_in = matmul_tn(sv.h, dz, "dw_in")
        grads = types.SimpleNamespace(
            w_in=dw_in, w_out=dw_out, w_uq=dw_uq, w_uk=dw_uk, w_v=dw_v, gq=dgq[0], gkv=dgkv[0],
            b_f=acc_f[0, :FH], norm_g=acc_i[2], dmod=jnp.concatenate([acc_i[0], acc_i[1], acc_o[0]]))
        return dx, grads

    return types.SimpleNamespace(layer_forward=layer_forward, layer_backward=layer_backward, final_loss=final_loss)


def _pack_rows(parts, dtype, row_multiple):
    flat = jnp.concatenate([p.reshape(-1).astype(dtype) for p in parts])
    per = SLAB_COLS * row_multiple
    total = -(-flat.shape[0] // per) * per
    return jnp.pad(flat, (0, total - flat.shape[0])).reshape(total // SLAB_COLS, SLAB_COLS)


def _unpack(flat, shapes):
    out, off = [], 0
    for shp in shapes:
        n = 1
        for d in shp:
            n *= d
        out.append(flat[off:off + n].reshape(shp))
        off += n
    return out


def kernel(x, c, positions, norm_g, w_ada, b_ada, w_in, b_f, q_norm_g, w_uq, kv_norm_g, w_ukv, w_out, final_g, loss_target, m_norm_g, m_w_ada, m_b_ada, m_w_in, m_b_f, m_q_norm_g, m_w_uq, m_kv_norm_g, m_w_ukv, m_w_out, m_final_g, v_norm_g, v_w_ada, v_b_ada, v_w_in, v_b_f, v_q_norm_g, v_w_uq, v_kv_norm_g, v_w_ukv, v_w_out, v_final_g):
    S, D = x.shape[1], x.shape[2]
    L = norm_g.shape[0]
    FH = b_f.shape[1]
    QL, KVL = q_norm_g.shape[1], kv_norm_g.shape[1]
    MH = w_ukv.shape[2] * N_CHIPS // (2 * HEAD_DIM)
    FW, MW = FH * HEAD_DIM, MH * HEAD_DIM
    NA = w_ada.shape[2]
    n_in = w_in.shape[2] * N_CHIPS
    cfg = types.SimpleNamespace(S=S, D=D, FW=FW, MW=MW, QL=QL, KVL=KVL, FH=FH, MH=MH)
    cfg.o_fq, cfg.o_fk, cfg.o_fv, cfg.o_fg, cfg.o_mg = 0, FW, 2 * FW, 3 * FW, 4 * FW
    cfg.o_ql = 4 * FW + MW
    cfg.o_kv = cfg.o_ql + QL
    cfg.o_ms = cfg.o_kv + KVL
    cfg.NZ = cfg.o_ms + LANES
    assert FW == MW and FH % 2 == 0 and MH % 2 == 0 and cfg.o_ql % QL == 0 and cfg.o_kv % KVL == 0 and KVL == LANES
    assert n_in == 4 * FW + FH + QL + KVL + ROPE_DIM + MW

    mx, my, mc = _my_pos()
    my_chip = 2 * mx + my
    my_dev = 2 * my_chip + mc

    inv_freq = 1.0 / (ROPE_THETA ** (jnp.arange(0, ROPE_DIM, 2, dtype=F32) / ROPE_DIM))
    ang = positions[0].astype(F32)[:, None] * inv_freq
    cos, sin = jnp.cos(ang), jnp.sin(ang)
    cfg.cos_t = jnp.concatenate([jnp.ones((S, ROPE_LO), F32), cos, cos, jnp.ones((S, HEAD_PAD - ROPE_LO - ROPE_DIM), F32)], axis=1)
    cfg.sin_t = jnp.concatenate([jnp.zeros((S, ROPE_LO), F32), -sin, sin, jnp.zeros((S, HEAD_PAD - ROPE_LO - ROPE_DIM), F32)], axis=1)

    assert L % 2 == 0
    def lane_pad(a):
        return jnp.pad(a, ((0, 0),) * (a.ndim - 1) + ((0, -a.shape[-1] % LANES),))

    shards = (w_in, w_uq, w_ukv, w_out)
    gathered = weights_gather([lane_pad(w.astype(BF16)) for w in shards], "weights_gather")
    def all_chips(g, w, axis):
        return jnp.concatenate([jnp.where(my_chip == t, w.astype(BF16), g[t][..., :w.shape[-1]]) for t in range(N_CHIPS)],
                               axis=axis)

    w_in_f = all_chips(gathered[0], w_in, 2)
    w_uq_f = all_chips(gathered[1], w_uq, 2)
    w_ukv_f = all_chips(gathered[2], w_ukv, 2)
    w_out_f = all_chips(gathered[3], w_out, 1)

    sizes = (FW, FW, FW, FH, FW, QL, KVL, ROPE_DIM, MW)
    offs = [0]
    for sz in sizes:
        offs.append(offs[-1] + sz)
    seg = [w_in_f[:, :, offs[i]:offs[i + 1]] for i in range(len(sizes))]
    fq_w, fk_w, fv_w, ff_w, fg_w, ql_w, kvl_w, kr_w, mg_w = seg
    zeros = lambda n: jnp.zeros((L, D, n), BF16)
    w_in_p = jnp.concatenate([fq_w, fk_w, fv_w, fg_w, mg_w, ql_w, kvl_w, ff_w, zeros(ROPE_LO - FH), kr_w,
                              zeros(HEAD_PAD - ROPE_LO - ROPE_DIM)], axis=2)
    w_uq_p = jnp.pad(w_uq_f.reshape(L, QL, MH, HEAD_DIM + ROPE_DIM), ((0, 0), (0, 0), (0, 0), (0, HEAD_PAD - HEAD_DIM - ROPE_DIM)))
    w_uq_p = w_uq_p.reshape(L, QL, MH * HEAD_PAD)
    w_ukv4 = w_ukv_f.reshape(L, KVL, MH, 2 * HEAD_DIM)
    w_uk_p = jnp.pad(w_ukv4[..., :HEAD_DIM], ((0, 0), (0, 0), (0, 0), (0, HEAD_PAD - HEAD_DIM))).reshape(L, KVL, MH * HEAD_PAD)
    w_v_p = w_ukv4[..., HEAD_DIM:].reshape(L, KVL, MW)

    c_all = allgather8(c.reshape(8, D // 8), "gather_c").reshape(N_DEV, D)
    c_pad = jnp.pad(c_all, ((0, 16 - N_DEV), (0, 0)))
    mod_part = ada_forward(c_pad, w_ada)[:, :N_DEV, :]
    mod_all = allgather8(mod_part.reshape(-1, LANES), "gather_mod").reshape(N_CHIPS, 2, L, N_DEV, NA)[:, 0]
    mod_full = mod_all.transpose(1, 2, 0, 3).reshape(L, N_DEV, N_CHIPS * NA) + b_ada[:, None, :]
    mod_mine = lax.dynamic_index_in_dim(mod_full, my_dev, axis=1, keepdims=True)

    step = make_step(cfg)
    bf_pad = jnp.pad(b_f, ((0, 0), (0, LANES - FH)))
    layers, mods = [], []
    for l in range(L):
        layers.append(types.SimpleNamespace(
            norm_g=norm_g[l][None], w_in=w_in_p[l], bf_pad=bf_pad[l][None], gq=q_norm_g[l][None], gkv=kv_norm_g[l][None],
            w_uq=w_uq_p[l], w_uk=w_uk_p[l], w_v=w_v_p[l], w_out=w_out_f[l]))
        mods.append((mod_mine[l, :, :D], mod_mine[l, :, D:2 * D], mod_mine[l, :, 2 * D:]))

    xl = x[0]
    saved = []
    for l in range(L):
        xl, sv = step.layer_forward(xl, layers[l], mods[l])
        saved.append(sv)
    dx, acc_fin, loss_part = step.final_loss(xl, final_g[None], loss_target[0])
    loss = lax.psum(loss_part[0, 0], ("x", "y", "c"))
    gl = [None] * L
    for l in reversed(range(L)):
        dx, gl[l] = step.layer_backward(dx, saved[l], layers[l], mods[l])
    grad_x = dx[None]

    stack = lambda name: jnp.stack([getattr(g, name) for g in gl])
    small_parts = [stack("norm_g"), stack("dmod"), stack("b_f"), stack("gq"), stack("gkv"), acc_fin[0]]
    small_shapes = [p.shape for p in small_parts]
    small = _pack_rows(small_parts, F32, 8).reshape(-1, LANES)
    small_all = allgather8(small, "gather_small").reshape(N_DEV, -1, LANES)
    small_sum = sum_leading(small_all, "sum_small")
    g_norm_g, g_b_ada, g_b_f, g_q_norm_g, g_kv_norm_g, g_final_g = _unpack(small_sum.reshape(-1), small_shapes)

    n_ng = L * D
    dmod_all = small_all.reshape(N_DEV, -1)[:, n_ng:n_ng + L * 3 * D].reshape(N_DEV, L, 3 * D)
    dmod_cols = lax.dynamic_slice_in_dim(dmod_all, my_chip * NA, NA, axis=2).transpose(1, 0, 2)
    g_w_ada = ada_backward(c_pad, jnp.pad(dmod_cols, ((0, 0), (0, 16 - N_DEV), (0, 0))))

    dw_in_p = stack("w_in")
    ms = cfg.o_ms
    dw_in_f = jnp.concatenate([
        dw_in_p[:, :, 0:3 * FW], dw_in_p[:, :, ms:ms + FH], dw_in_p[:, :, cfg.o_fg:cfg.o_fg + FW],
        dw_in_p[:, :, cfg.o_ql:cfg.o_ql + QL + KVL], dw_in_p[:, :, ms + ROPE_LO:ms + ROPE_LO + ROPE_DIM],
        dw_in_p[:, :, cfg.o_mg:cfg.o_mg + MW]], axis=2)
    dw_uq_f = stack("w_uq").reshape(L, QL, MH, HEAD_PAD)[..., :HEAD_DIM + ROPE_DIM].reshape(L, QL, -1)
    dw_ukv_f = jnp.concatenate([stack("w_uk").reshape(L, KVL, MH, HEAD_PAD)[..., :HEAD_DIM],
                                stack("w_v").reshape(L, KVL, MH, HEAD_DIM)], axis=3).reshape(L, KVL, -1)
    dw_out_f = stack("w_out")
    gs = [dw_in_f.reshape(L, D, N_CHIPS, -1).transpose(2, 0, 1, 3), dw_uq_f.reshape(L, QL, N_CHIPS, -1).transpose(2, 0, 1, 3),
          dw_ukv_f.reshape(L, KVL, N_CHIPS, -1).transpose(2, 0, 1, 3), dw_out_f.reshape(L, N_CHIPS, -1, D).transpose(1, 0, 2, 3)]
    gs = [lane_pad(g) for g in gs]
    hl = L // 2
    theirs = halves_to_sibling(gs, "grads_sibling")
    chip_part = []
    big_names = ["w_in", "w_uq", "w_ukv", "w_out"]
    for g, o, nm in zip(gs, theirs, big_names):
        keep = lax.dynamic_slice_in_dim(g, mc * hl, hl, axis=1)
        merged = (N_CHIPS * hl,) + g.shape[2:]
        chip_part.append(add_cast(keep.reshape(merged), o.reshape(merged), BF16, "grads_chip_sum_" + nm).reshape(o.shape))
    parts = chip_exchange(chip_part, "grads_chips")
    where = jnp.stack([my_chip, mc]).astype(jnp.int32)
    red = [sum_chips(p, o, where, "grads_sum_" + nm) for p, o, nm in zip(parts, chip_part, big_names)]
    g_w_in, g_w_uq, g_w_ukv, g_w_out = [g[..., :w.shape[-1]] for g, w in zip(halves_gather(red, "grads_back"), shards)]


    names = ["norm_g", "w_ada", "b_ada", "w_in", "b_f", "q_norm_g", "w_uq", "kv_norm_g", "w_ukv", "w_out", "final_g"]
    ws = dict(norm_g=norm_g, w_ada=w_ada, b_ada=b_ada, w_in=w_in, b_f=b_f, q_norm_g=q_norm_g, w_uq=w_uq,
              kv_norm_g=kv_norm_g, w_ukv=w_ukv, w_out=w_out, final_g=final_g)
    msd = dict(norm_g=m_norm_g, w_ada=m_w_ada, b_ada=m_b_ada, w_in=m_w_in, b_f=m_b_f, q_norm_g=m_q_norm_g, w_uq=m_w_uq,
               kv_norm_g=m_kv_norm_g, w_ukv=m_w_ukv, w_out=m_w_out, final_g=m_final_g)
    vsd = dict(norm_g=v_norm_g, w_ada=v_w_ada, b_ada=v_b_ada, w_in=v_w_in, b_f=v_b_f, q_norm_g=v_q_norm_g, w_uq=v_w_uq,
               kv_norm_g=v_kv_norm_g, w_ukv=v_w_ukv, w_out=v_w_out, final_g=v_final_g)
    gsd = dict(norm_g=g_norm_g, w_ada=g_w_ada, b_ada=g_b_ada, w_in=g_w_in, b_f=g_b_f, q_norm_g=g_q_norm_g, w_uq=g_w_uq,
               kv_norm_g=g_kv_norm_g, w_ukv=g_w_ukv, w_out=g_w_out, final_g=g_final_g)
    small_names = ["norm_g", "b_ada", "b_f", "q_norm_g", "kv_norm_g", "final_g"]
    sm_shapes = [ws[n].shape for n in small_names]
    pk = lambda d: _pack_rows([d[n] for n in small_names], F32, 8).reshape(1, -1, LANES)
    sm_out = adamw(pk(ws), pk(gsd), pk(msd), pk(vsd), "adamw_small")
    sm_d, sm_m, sm_v = [dict(zip(small_names, _unpack(o.reshape(-1), sm_shapes))) for o in sm_out]
    delta, new_m, new_v = dict(sm_d), dict(sm_m), dict(sm_v)
    for n in ["w_ada", "w_in", "w_uq", "w_ukv", "w_out"]:
        delta[n], new_m[n], new_v[n] = adamw(ws[n], gsd[n], msd[n], vsd[n], "adamw_" + n)

    return (loss, grad_x, *[gsd[n] for n in names], *[delta[n] for n in names],
            *[new_m[n] for n in names], *[new_v[n] for n in names])
```

```python
import functools
import types

import jax
import jax.numpy as jnp
from jax import lax
from jax.experimental import pallas as pl
from jax.experimental.pallas import tpu as pltpu

F32 = jnp.float32
BF16 = jnp.bfloat16
MESH = pl.DeviceIdType.MESH

N_CHIPS = 4
N_DEV = 8
HEAD_DIM = 64
ROPE_DIM = 32
ROPE_THETA = 10000.0
HEAD_PAD = 128
ROPE_LO = 64
ROPE_HALF = 16
LANES = 128
EPS = 1e-6
NEG = -1e30
ADAM_LR = 0.001
ADAM_B1 = 0.9
ADAM_B2 = 0.999
ADAM_EPS = 1e-08
ADAM_WD = 0.01
ADAM_STEP = 10
VMEM_LIMIT = 48 * 1024 * 1024
SLAB_COLS = 1024


def _params(sem=None, vmem=VMEM_LIMIT):
    return pltpu.CompilerParams(dimension_semantics=sem, vmem_limit_bytes=vmem)


def _nn(a, b):
    return jnp.dot(a, b, preferred_element_type=F32)


def _nt(a, b):
    return lax.dot_general(a, b, (((1,), (1,)), ((), ())), preferred_element_type=F32)


def _tn(a, b):
    return lax.dot_general(a, b, (((0,), (0,)), ((), ())), preferred_element_type=F32)


def _sigmoid(x):
    return 1.0 / (1.0 + jnp.exp(-x))


def _lane(shape):
    return lax.broadcasted_iota(jnp.int32, shape, len(shape) - 1)


def _pick(n, cands):
    for c in cands:
        if n % c == 0:
            return c
    return n


def _my_pos():
    return lax.axis_index("x"), lax.axis_index("y"), lax.axis_index("c")


def allgather8(xs, name):
    m_per, n = xs.shape

    def body(x_ref, out_ref, send_sems, recv_sems, local_sem):
        x, y, c = _my_pos()
        me, sibling = (x, y, c), (x, y, 1 - c)
        chips = [(1 - x, y), (x, 1 - y), (1 - x, 1 - y)]

        def rows(px, py, pc):
            return out_ref.at[pl.ds((4 * px + 2 * py + pc) * m_per, m_per), :]

        def copy(k, block, to, src=None):
            return pltpu.make_async_remote_copy(
                src_ref=rows(*block) if src is None else src, dst_ref=rows(*block),
                send_sem=send_sems.at[k], recv_sem=recv_sems.at[k], device_id=to, device_id_type=MESH)

        mine = pltpu.make_async_copy(x_ref, rows(*me), local_sem)
        mine.start()
        first = [copy(0, me, sibling, src=x_ref)]
        first += [copy(1 + j, me, (*chip, c), src=x_ref) for j, chip in enumerate(chips)]
        for cp in first:
            cp.start()
        passed = [copy(4 + j, (*chip, c), sibling) for j, chip in enumerate(chips)]
        for j, chip in enumerate(chips):
            copy(1 + j, (*chip, c), me).wait_recv()
            passed[j].start()
        copy(0, sibling, me).wait_recv()
        for j, chip in enumerate(chips):
            copy(4 + j, (*chip, 1 - c), me).wait_recv()
        for cp in first + passed:
            cp.wait_send()
        mine.wait()

    return pl.pallas_call(
        body, name=name,
        out_shape=jax.ShapeDtypeStruct((N_DEV * m_per, n), xs.dtype),
        in_specs=[pl.BlockSpec(memory_space=pltpu.VMEM)],
        out_specs=pl.BlockSpec(memory_space=pltpu.VMEM),
        scratch_shapes=[pltpu.SemaphoreType.DMA((7,)), pltpu.SemaphoreType.DMA((7,)), pltpu.SemaphoreType.DMA],
    )(xs)


def _remote(src, dst, send_sems, recv_sems, k, to):
    return pltpu.make_async_remote_copy(src_ref=src, dst_ref=dst, send_sem=send_sems.at[k], recv_sem=recv_sems.at[k],
                                        device_id=to, device_id_type=MESH)


def _hbm_call(body, name, ins, out_shapes, n_sems, aliases=None):
    hbm = pl.BlockSpec(memory_space=pl.ANY)
    scratch = [pltpu.SemaphoreType.DMA((n_sems,)), pltpu.SemaphoreType.DMA((n_sems,))]
    return pl.pallas_call(body, name=name, out_shape=out_shapes, in_specs=[hbm] * len(ins),
                          out_specs=[hbm] * len(out_shapes), scratch_shapes=scratch,
                          input_output_aliases=aliases or {})(*ins)


def _layers_half(ref, h, axis=0):
    size = ref.shape[axis] // 2
    idx = (slice(None),) * axis + (pl.ds(h * size, size),)
    return ref.at[idx]


def weights_gather(ws, name):
    n = len(ws)

    def body(*refs):
        x_refs, o_refs = refs[:n], refs[n:2 * n]
        send_sems, recv_sems = refs[2 * n:]
        x, y, c = _my_pos()
        s = 2 * x + y
        sibling = (x, y, 1 - c)
        chips = [(1 - x, y), (x, 1 - y), (1 - x, 1 - y)]
        sends = []
        for a in range(n):
            for k, (tx, ty) in enumerate(chips):
                cp = _remote(_layers_half(x_refs[a], c), _layers_half(o_refs[a].at[s], c), send_sems, recv_sems,
                             6 * a + k, (tx, ty, c))
                cp.start()
                sends.append(cp)
        for a in range(n):
            for k, (tx, ty) in enumerate(chips):
                got = _layers_half(o_refs[a].at[2 * tx + ty], c)
                _remote(got, got, send_sems, recv_sems, 6 * a + k, (tx, ty, c)).wait_recv()
                cp = _remote(got, got, send_sems, recv_sems, 6 * a + 3 + k, sibling)
                cp.start()
                sends.append(cp)
        for a in range(n):
            for k, (tx, ty) in enumerate(chips):
                theirs = _layers_half(o_refs[a].at[2 * tx + ty], 1 - c)
                _remote(theirs, theirs, send_sems, recv_sems, 6 * a + 3 + k, sibling).wait_recv()
        for cp in sends:
            cp.wait_send()

    outs = [jax.ShapeDtypeStruct((N_CHIPS,) + w.shape, w.dtype) for w in ws]
    return _hbm_call(body, name, ws, outs, 6 * n)


def halves_to_sibling(gs, name):
    n = len(gs)

    def body(*refs):
        x_refs, o_refs, send_sems, recv_sems = refs[:n], refs[n:2 * n], refs[2 * n], refs[2 * n + 1]
        x, y, c = _my_pos()
        cps = [_remote(_layers_half(x_refs[a], 1 - c, axis=1), o_refs[a], send_sems, recv_sems, a, (x, y, 1 - c))
               for a in range(n)]
        for cp in cps:
            cp.start()
        for cp in cps:
            cp.wait()

    outs = [jax.ShapeDtypeStruct((g.shape[0], g.shape[1] // 2) + g.shape[2:], g.dtype) for g in gs]
    return _hbm_call(body, name, gs, outs, n)


def chip_exchange(xs, name):
    n = len(xs)

    def body(*refs):
        x_refs, o_refs = refs[:n], refs[n:2 * n]
        send_sems, recv_sems = refs[2 * n:]
        x, y, c = _my_pos()
        s = 2 * x + y
        chips = [(1 - x, y), (x, 1 - y), (1 - x, 1 - y)]
        sends = []
        for a in range(n):
            for k, (tx, ty) in enumerate(chips):
                cp = _remote(x_refs[a].at[2 * tx + ty], o_refs[a].at[s], send_sems, recv_sems, 3 * a + k, (tx, ty, c))
                cp.start()
                sends.append(cp)
        for a in range(n):
            for k, (tx, ty) in enumerate(chips):
                got = o_refs[a].at[2 * tx + ty]
                _remote(got, got, send_sems, recv_sems, 3 * a + k, (tx, ty, c)).wait_recv()
        for cp in sends:
            cp.wait_send()

    outs = [jax.ShapeDtypeStruct(v.shape, v.dtype) for v in xs]
    return _hbm_call(body, name, xs, outs, 3 * n)


def halves_gather(xs, name):
    n = len(xs)

    def body(*refs):
        x_refs, o_refs, send_sems, recv_sems = refs[:n], refs[n:2 * n], refs[2 * n], refs[2 * n + 1]
        x, y, c = _my_pos()
        sends = []
        for a in range(n):
            cp = _remote(_layers_half(x_refs[a], c), _layers_half(o_refs[a], c), send_sems, recv_sems, a, (x, y, 1 - c))
            cp.start()
            sends.append(cp)
        for a in range(n):
            theirs = _layers_half(o_refs[a], 1 - c)
            _remote(theirs, theirs, send_sems, recv_sems, a, (x, y, 1 - c)).wait_recv()
        for cp in sends:
            cp.wait_send()

    outs = [jax.ShapeDtypeStruct(v.shape, v.dtype) for v in xs]
    return _hbm_call(body, name, xs, outs, n, aliases={a: a for a in range(n)})


def sum_leading(xs, name):
    n, r, c = xs.shape

    def body(x_ref, o_ref):
        acc = x_ref[0]
        for i in range(1, n):
            acc = acc + x_ref[i]
        o_ref[...] = acc

    return pl.pallas_call(body, name=name, out_shape=jax.ShapeDtypeStruct((r, c), xs.dtype))(xs)


def add_cast(a, b, out_dtype, name):
    n, r, c = a.shape
    tr = _pick(r, (512, 256, 128, 64, 32, 16))

    def body(a_ref, b_ref, o_ref):
        o_ref[...] = (a_ref[...] + b_ref[...]).astype(out_dtype)

    spec = pl.BlockSpec((1, tr, c), lambda i, j: (i, j, 0))
    return pl.pallas_call(body, name=name, grid=(n, r // tr), in_specs=[spec, spec], out_specs=spec,
                          out_shape=jax.ShapeDtypeStruct((n, r, c), out_dtype),
                          compiler_params=_params(("parallel", "parallel")))(a, b)


def sum_chips(parts, own, where, name):
    n, nl, r, c = parts.shape
    tr = _pick(r, (512, 256, 128, 64, 32, 16))

    def body(w_ref, p_ref, o_ref, out_ref):
        s = w_ref[0]
        acc = jnp.zeros(out_ref.shape, F32)
        for t in range(n):
            acc = acc + jnp.where(s == t, o_ref[0], p_ref[t]).astype(F32)
        out_ref[...] = acc

    grid_spec = pltpu.PrefetchScalarGridSpec(
        num_scalar_prefetch=1, grid=(nl, r // tr),
        in_specs=[pl.BlockSpec((n, 1, tr, c), lambda i, j, w: (0, i, j, 0)),
                  pl.BlockSpec((1, 1, tr, c), lambda i, j, w: (w[0], i, j, 0))],
        out_specs=pl.BlockSpec((1, tr, c), lambda i, j, w: (w[1] * nl + i, j, 0)))
    return pl.pallas_call(body, name=name, grid_spec=grid_spec, out_shape=jax.ShapeDtypeStruct((2 * nl, r, c), F32),
                          compiler_params=_params(("parallel", "parallel")))(where, parts, own)


def ada_forward(c_all, w_ada):
    nl, d, n = w_ada.shape
    nb = c_all.shape[0]

    def body(c_ref, w_ref, o_ref):
        cv = c_ref[...]
        ca = (cv * _sigmoid(cv)).astype(BF16)
        o_ref[0] = _nn(ca, w_ref[0].astype(BF16))

    return pl.pallas_call(
        body, name="ada_forward", grid=(nl,),
        in_specs=[pl.BlockSpec((nb, d), lambda l: (0, 0)), pl.BlockSpec((1, d, n), lambda l: (l, 0, 0))],
        out_specs=pl.BlockSpec((1, nb, n), lambda l: (l, 0, 0)),
        out_shape=jax.ShapeDtypeStruct((nl, nb, n), F32), compiler_params=_params(("parallel",)))(c_all, w_ada)


def ada_backward(c_all, dmod):
    nl, nb, n = dmod.shape
    d = c_all.shape[1]

    def body(c_ref, g_ref, o_ref):
        cv = c_ref[...]
        ca = (cv * _sigmoid(cv)).astype(BF16)
        o_ref[0] = _tn(ca, g_ref[0].astype(BF16))

    return pl.pallas_call(
        body, name="ada_backward", grid=(nl,),
        in_specs=[pl.BlockSpec((nb, d), lambda l: (0, 0)), pl.BlockSpec((1, nb, n), lambda l: (l, 0, 0))],
        out_specs=pl.BlockSpec((1, d, n), lambda l: (l, 0, 0)),
        out_shape=jax.ShapeDtypeStruct((nl, d, n), F32), compiler_params=_params(("parallel",)))(c_all, dmod)


def matmul_tn(a, b, name):
    k, m = a.shape
    n = b.shape[1]
    tm, tk = _pick(m, (512, 256, 128)), _pick(k, (512, 256, 128))
    tn = n if n * (tm * 8 + tk * 4) <= VMEM_LIMIT // 2 else _pick(n, (512, 256, 128))

    def body(a_ref, b_ref, o_ref):
        @pl.when(pl.program_id(2) == 0)
        def _():
            o_ref[...] = jnp.zeros_like(o_ref)

        o_ref[...] += _tn(a_ref[...], b_ref[...])

    return pl.pallas_call(
        body, name=name, grid=(m // tm, n // tn, k // tk),
        in_specs=[pl.BlockSpec((tk, tm), lambda i, j, kk: (kk, i)), pl.BlockSpec((tk, tn), lambda i, j, kk: (kk, j))],
        out_specs=pl.BlockSpec((tm, tn), lambda i, j, kk: (i, j)),
        out_shape=jax.ShapeDtypeStruct((m, n), F32),
        compiler_params=_params(("parallel", "parallel", "arbitrary")))(a, b)


def adamw(w, g, m, v, name):
    nl, r, c = w.shape
    tr = _pick(r, (512, 256, 128, 64, 32, 16, 8))

    def body(w_ref, g_ref, m_ref, v_ref, d_ref, mo_ref, vo_ref):
        gv = g_ref[...]
        mn = ADAM_B1 * m_ref[...] + (1.0 - ADAM_B1) * gv
        vn = ADAM_B2 * v_ref[...] + (1.0 - ADAM_B2) * (gv * gv)
        m_hat = mn / (1.0 - ADAM_B1 ** ADAM_STEP)
        v_hat = vn / (1.0 - ADAM_B2 ** ADAM_STEP)
        d_ref[...] = -ADAM_LR * (m_hat / (jnp.sqrt(v_hat) + ADAM_EPS) + ADAM_WD * w_ref[...])
        mo_ref[...] = mn
        vo_ref[...] = vn

    spec = pl.BlockSpec((1, tr, c), lambda l, i: (l, i, 0))
    out = jax.ShapeDtypeStruct((nl, r, c), F32)
    return pl.pallas_call(body, name=name, grid=(nl, r // tr), in_specs=[spec] * 4, out_specs=[spec] * 3,
                          out_shape=[out] * 3, compiler_params=_params(("parallel", "parallel")))(w, g, m, v)


def _rope(t, cos_t, sin_t):
    w = t.shape[1]
    lane = _lane(t.shape) & (HEAD_PAD - 1)
    first_half = (lane >= ROPE_LO) & (lane < ROPE_LO + ROPE_HALF)
    partner = jnp.where(first_half, pltpu.roll(t, w - ROPE_HALF, 1), pltpu.roll(t, ROPE_HALF, 1))
    return t * cos_t + partner * sin_t


def _rope_t(dt, cos_t, sin_t):
    w = dt.shape[1]
    lane = _lane(dt.shape) & (HEAD_PAD - 1)
    first_half = (lane >= ROPE_LO) & (lane < ROPE_LO + ROPE_HALF)
    ds = dt * sin_t
    partner = jnp.where(first_half, pltpu.roll(ds, w - ROPE_HALF, 1), pltpu.roll(ds, ROPE_HALF, 1))
    return dt * cos_t + partner


def _rms(xv, g):
    rstd = lax.rsqrt(jnp.mean(xv * xv, axis=-1, keepdims=True) + EPS)
    xh = xv * rstd
    return xh * g, xh, rstd


def _rms_bwd(dy, g, xh, rstd):
    dxh = dy * g
    dx = rstd * (dxh - xh * jnp.mean(dxh * xh, axis=-1, keepdims=True))
    return dx, jnp.sum(dy * xh, axis=0, keepdims=True)


def make_step(cfg):
    S, D, NZ = cfg.S, cfg.D, cfg.NZ
    FW, MW, QL, KVL, FH, MH = cfg.FW, cfg.MW, cfg.QL, cfg.KVL, cfg.FH, cfg.MH
    QW = MH * HEAD_PAD
    TM = _pick(S, (512, 256, 128))
    TQ = TK = _pick(S, (256, 128))
    n_tok = S // TM
    ZO = 3 * FW
    NZR = NZ - ZO
    misc_blk = (cfg.o_ms - ZO) // LANES
    FOX_SCALE = HEAD_DIM ** -0.5
    MLA_SCALE = (HEAD_DIM + ROPE_DIM) ** -0.5

    def tok(width, col=0):
        return pl.BlockSpec((TM, width), lambda i: (i, col))

    def const(shape):
        return pl.BlockSpec(shape, lambda i: tuple(0 for _ in shape))


    def ln_inproj(x, g, scale, shift, w_in):
        def body(x_ref, g_ref, sc_ref, sh_ref, w_ref, h_ref, z_ref, qkv_ref):
            y, _, _ = _rms(x_ref[...], g_ref[...])
            hb = (y * (1.0 + sc_ref[...]) + sh_ref[...]).astype(BF16)
            h_ref[...] = hb
            z = _nn(hb, w_ref[...])
            z_ref[...] = z[:, ZO:]
            qkv_ref[:, :FW] = (z[:, :FW] * FOX_SCALE).astype(BF16)
            qkv_ref[:, FW:] = z[:, FW:ZO].astype(BF16)

        return pl.pallas_call(
            body, name="ln_inproj", grid=(n_tok,),
            in_specs=[tok(D), const((1, D)), const((1, D)), const((1, D)), const((D, NZ))],
            out_specs=[tok(D), tok(NZR), tok(ZO)],
            out_shape=[jax.ShapeDtypeStruct((S, D), BF16), jax.ShapeDtypeStruct((S, NZR), F32),
                       jax.ShapeDtypeStruct((S, ZO), BF16)],
            compiler_params=_params(("parallel",)))(x, g, scale, shift, w_in)

    def _log_f_terms(misc, bf):
        lane = _lane(misc.shape)
        a = misc + bf
        e = jnp.exp(-jnp.abs(a))
        logf = jnp.minimum(a, 0.0) - jnp.log(1.0 + e)
        sig_neg = jnp.where(a >= 0, e, 1.0) / (1.0 + e)
        valid = lane < FH
        return jnp.where(valid, logf, 0.0), jnp.where(valid, sig_neg, 0.0)

    def fox_prep(z, bf_pad):
        def body(z_ref, b_ref, o_ref, carry):
            @pl.when(pl.program_id(0) == 0)
            def _():
                carry[...] = jnp.zeros_like(carry)

            logf, _ = _log_f_terms(z_ref[...], b_ref[...])
            row = lax.broadcasted_iota(jnp.int32, (TM, TM), 0)
            col = lax.broadcasted_iota(jnp.int32, (TM, TM), 1)
            tri = (col <= row).astype(F32)
            cum = jnp.dot(tri, logf, precision=lax.Precision.HIGHEST, preferred_element_type=F32) + carry[...]
            o_ref[...] = cum
            carry[...] = cum[TM - 1:TM, :]

        return pl.pallas_call(
            body, name="fox_prep", grid=(n_tok,),
            in_specs=[tok(LANES, misc_blk), const((1, LANES))], out_specs=tok(LANES),
            out_shape=jax.ShapeDtypeStruct((S, LANES), F32),
            scratch_shapes=[pltpu.VMEM((1, LANES), F32)],
            compiler_params=_params(("arbitrary",)))(z, bf_pad)

    def mla_prep(z, gq, gkv, w_uq, w_uk, w_v, cos_t, sin_t):
        def body(ql_ref, kvl_ref, ms_ref, gq_ref, gkv_ref, wq_ref, wk_ref, wv_ref, c_ref, s_ref, q_ref, k_ref, v_ref):
            cos1, sin1 = c_ref[...], s_ref[...]
            qn, _, _ = _rms(ql_ref[...], gq_ref[...])
            q = _nn(qn.astype(BF16), wq_ref[...])
            q_ref[...] = (_rope(q, jnp.tile(cos1, (1, MH)), jnp.tile(sin1, (1, MH))) * MLA_SCALE).astype(BF16)
            kvn, _, _ = _rms(kvl_ref[...], gkv_ref[...])
            kvb = kvn.astype(BF16)
            lane = _lane((TM, LANES))
            kr = jnp.where((lane >= ROPE_LO) & (lane < ROPE_LO + ROPE_DIM), ms_ref[...], 0.0)
            kr = _rope(kr, cos1, sin1)
            k_ref[...] = (_nn(kvb, wk_ref[...]) + jnp.tile(kr, (1, MH))).astype(BF16)
            v_ref[...] = _nn(kvb, wv_ref[...]).astype(BF16)

        return pl.pallas_call(
            body, name="mla_prep", grid=(n_tok,),
            in_specs=[tok(QL, (cfg.o_ql - ZO) // QL), tok(KVL, (cfg.o_kv - ZO) // KVL), tok(LANES, misc_blk),
                      const((1, QL)), const((1, KVL)), const((QL, QW)), const((KVL, QW)), const((KVL, MW)),
                      tok(LANES), tok(LANES)],
            out_specs=[tok(QW), tok(QW), tok(MW)],
            out_shape=[jax.ShapeDtypeStruct((S, QW), BF16), jax.ShapeDtypeStruct((S, QW), BF16),
                       jax.ShapeDtypeStruct((S, MW), BF16)],
            compiler_params=_params(("parallel",)))(z, z, z, gq, gkv, w_uq, w_uk, w_v, cos_t, sin_t)

    def _allowed(q0, k0, chunked):
        qi = q0 + lax.broadcasted_iota(jnp.int32, (TQ, TK), 0)
        ki = k0 + lax.broadcasted_iota(jnp.int32, (TQ, TK), 1)
        if chunked:
            return (ki >> 6) <= (qi >> 6)
        return ki <= qi

    def _heads(val, packed):
        if packed:
            lane = _lane(val.shape)
            zero = jnp.zeros_like(val)
            return [jnp.where(lane < HEAD_DIM, val, zero), jnp.where(lane >= HEAD_DIM, val, zero)]
        return [val[:, :HEAD_PAD], val[:, HEAD_PAD:]]

    def _merge(a0, a1):
        return jnp.where(_lane(a0.shape) < HEAD_DIM, a0, a1)

    def attn_fwd(q, k, v, q_blk0, k_blk0, v_blk0, cumt, packed, chunked, name):
        has_bias = cumt is not None
        n_pairs = (FH if packed else MH) // 2
        wq = LANES if packed else 2 * HEAD_PAD
        assert TQ == TK

        def body(*refs):
            if has_bias:
                q_ref, k_ref, v_ref, ct_ref, o_ref, lse_ref = refs
            else:
                q_ref, k_ref, v_ref, o_ref, lse_ref = refs
            i = pl.program_id(1)
            q0 = i * TQ
            qh = _heads(q_ref[...], packed)
            q_both = jnp.concatenate(qh, axis=0) if packed else None

            def scores(kb):
                kblk = k_ref[pl.ds(pl.multiple_of(kb * TK, TK), TK), :]
                if packed:
                    return _nt(q_both, kblk)
                return jnp.concatenate([_nt(qh[0], kblk[:, :HEAD_PAD]), _nt(qh[1], kblk[:, HEAD_PAD:])], axis=0)

            def update(kb, s, carry, masked):
                m, l, acc = carry
                k0 = pl.multiple_of(kb * TK, TK)
                s0, s1 = s[:TQ], s[TQ:]
                if has_bias:
                    ck = ct_ref[0, kb]
                    s0, s1 = s0 - ck[0:1, :], s1 - ck[1:2, :]
                if masked:
                    allow = _allowed(q0, k0, chunked)
                    s0, s1 = jnp.where(allow, s0, NEG), jnp.where(allow, s1, NEG)
                s = jnp.concatenate([s0, s1], axis=0)
                m_new = jnp.maximum(m, jnp.max(s, axis=-1, keepdims=True))
                p = jnp.exp(s - m_new)
                alpha = jnp.exp(m - m_new)
                l = alpha * l + jnp.sum(p, axis=-1, keepdims=True)
                acc = alpha * acc + _nn(p.astype(BF16), v_ref[pl.ds(k0, TK), :])
                return m_new, l, acc

            def two(kb, carry, last_masked):
                s_a, s_b = scores(kb), scores(kb + 1)
                return update(kb + 1, s_b, update(kb, s_a, carry, False), last_masked)

            init = (jnp.full((2 * TQ, 1), NEG, F32), jnp.zeros((2 * TQ, 1), F32), jnp.zeros((2 * TQ, LANES), F32))
            carry = lax.fori_loop(0, i // 2, lambda t, cr: two(2 * t, cr, False), init)
            m, l, acc = lax.cond(i % 2 == 1, lambda cr: two(i - 1, cr, True),
                                 lambda cr: update(i, scores(i), cr, True), carry)
            o = acc / l
            lse = m + jnp.log(l)
            lane = _lane((TQ, LANES))
            o_ref[...] = _merge(o[:TQ], o[TQ:])
            lse_ref[...] = jnp.where(lane == 0, lse[:TQ], jnp.where(lane == 1, lse[TQ:], 0.0))

        in_specs = [pl.BlockSpec((TQ, wq), lambda p, i: (i, q_blk0 + p)),
                    pl.BlockSpec((S, wq), lambda p, i: (0, k_blk0 + p)),
                    pl.BlockSpec((S, LANES), lambda p, i: (0, v_blk0 + p))]
        args = [q, k, v]
        if has_bias:
            in_specs += [pl.BlockSpec((1, S // TK, 8, TK), lambda p, i: (p, 0, 0, 0))]
            args += [cumt]
        return pl.pallas_call(
            body, name=name, grid=(n_pairs, S // TQ), in_specs=in_specs,
            out_specs=[pl.BlockSpec((TQ, LANES), lambda p, i: (i, p)),
                       pl.BlockSpec((TQ, LANES), lambda p, i: (i, p))],
            out_shape=[jax.ShapeDtypeStruct((S, n_pairs * LANES), F32),
                       jax.ShapeDtypeStruct((S, n_pairs * LANES), F32)],
            compiler_params=_params(("parallel", "parallel")))(*args)

    def gate_outproj(of, om, z, w_out, x, gate):
        def body(of_ref, om_ref, fg_ref, mg_ref, w_ref, x_ref, gt_ref, xn_ref, u_ref, y_ref):
            fg, mg = fg_ref[...], mg_ref[...]
            u = jnp.concatenate([of_ref[...] * fg * _sigmoid(fg), om_ref[...] * mg * _sigmoid(mg)], axis=1).astype(BF16)
            y = _nn(u, w_ref[...])
            u_ref[...] = u
            y_ref[...] = y.astype(BF16)
            xn_ref[...] = x_ref[...] + gt_ref[...] * y

        return pl.pallas_call(
            body, name="gate_outproj", grid=(n_tok,),
            in_specs=[tok(FW), tok(MW), tok(FW, (cfg.o_fg - ZO) // FW), tok(MW, (cfg.o_mg - ZO) // MW),
                      const((FW + MW, D)), tok(D), const((1, D))],
            out_specs=[tok(D), tok(FW + MW), tok(D)],
            out_shape=[jax.ShapeDtypeStruct((S, D), F32), jax.ShapeDtypeStruct((S, FW + MW), BF16),
                       jax.ShapeDtypeStruct((S, D), BF16)],
            compiler_params=_params(("parallel",)))(of, om, z, z, w_out, x, gate)

    def final_loss(x, g, target):
        def body(x_ref, g_ref, t_ref, dx_ref, acc_ref, loss_ref):
            @pl.when(pl.program_id(0) == 0)
            def _():
                acc_ref[...] = jnp.zeros_like(acc_ref)
                loss_ref[...] = jnp.zeros_like(loss_ref)

            gv = g_ref[...]
            y, xh, rstd = _rms(x_ref[...], gv)
            e = y - t_ref[...]
            loss_ref[...] += 0.5 * jnp.sum(jnp.sum(e * e, axis=-1, keepdims=True) / D, axis=0, keepdims=True)
            dx, dg = _rms_bwd(e / D, gv, xh, rstd)
            dx_ref[...] = dx
            acc_ref[0:1, :] += dg

        return pl.pallas_call(
            body, name="final_loss", grid=(n_tok,),
            in_specs=[tok(D), const((1, D)), tok(D)],
            out_specs=[tok(D), const((8, D)), const((1, LANES))],
            out_shape=[jax.ShapeDtypeStruct((S, D), F32), jax.ShapeDtypeStruct((8, D), F32),
                       jax.ShapeDtypeStruct((1, LANES), F32)],
            compiler_params=_params(("arbitrary",)))(x, g, target)


    def bwd_out(dxn, gate, y, w_out, of, om, z, lse_f, lse_m):
        def body(dx_ref, gt_ref, y_ref, w_ref, of_ref, om_ref, fg_ref, mg_ref, lf_ref, lm_ref,
                 dof_ref, dom_ref, dfg_ref, dmg_ref, dy_ref, acc_ref, sf_ref, sm_ref):
            @pl.when(pl.program_id(0) == 0)
            def _():
                acc_ref[...] = jnp.zeros_like(acc_ref)

            dxv = dx_ref[...]
            acc_ref[0:1, :] += jnp.sum(dxv * y_ref[...].astype(F32), axis=0, keepdims=True)
            dy = (gt_ref[...] * dxv).astype(BF16)
            dy_ref[...] = dy
            du = _nt(dy, w_ref[...])
            lane = _lane((TM, LANES))
            for lo, width, o_ref, g_ref, do_ref, dg_ref, l_ref, st_ref in (
                    (0, FW, of_ref, fg_ref, dof_ref, dfg_ref, lf_ref, sf_ref),
                    (FW, MW, om_ref, mg_ref, dom_ref, dmg_ref, lm_ref, sm_ref)):
                gv, ov = g_ref[...], o_ref[...]
                sg = _sigmoid(gv)
                dup = du[:, lo:lo + width]
                dob = (dup * gv * sg).astype(BF16)
                do_ref[...] = dob
                dg_ref[...] = (dup * ov * sg * (1.0 + gv * (1.0 - sg))).astype(BF16)
                d = dob.astype(F32) * ov
                for pr in range(width // LANES):
                    cols = slice(pr * LANES, (pr + 1) * LANES)
                    dp = d[:, cols]
                    d0 = jnp.sum(jnp.where(lane < HEAD_DIM, dp, 0.0), axis=-1, keepdims=True)
                    d1 = jnp.sum(jnp.where(lane >= HEAD_DIM, dp, 0.0), axis=-1, keepdims=True)
                    st_ref[:, cols] = jnp.where(lane == 2, d0, jnp.where(lane == 3, d1, l_ref[:, cols]))

        return pl.pallas_call(
            body, name="bwd_out", grid=(n_tok,),
            in_specs=[tok(D), const((1, D)), tok(D), const((FW + MW, D)), tok(FW), tok(MW),
                      tok(FW, (cfg.o_fg - ZO) // FW), tok(MW, (cfg.o_mg - ZO) // MW), tok(FW), tok(MW)],
            out_specs=[tok(FW), tok(MW), tok(FW), tok(MW), tok(D), const((8, D)), tok(FW), tok(MW)],
            out_shape=[jax.ShapeDtypeStruct((S, FW), BF16), jax.ShapeDtypeStruct((S, MW), BF16),
                       jax.ShapeDtypeStruct((S, FW), BF16), jax.ShapeDtypeStruct((S, MW), BF16),
                       jax.ShapeDtypeStruct((S, D), BF16), jax.ShapeDtypeStruct((8, D), F32),
                       jax.ShapeDtypeStruct((S, FW), F32), jax.ShapeDtypeStruct((S, MW), F32)],
            compiler_params=_params(("arbitrary",)))(dxn, gate, y, w_out, of, om, z, z, lse_f, lse_m)

    def attn_bwd(q, k, v, do, stats, q_blk0, k_blk0, v_blk0, cumt, packed, chunked, name):
        has_bias = cumt is not None
        n_pairs = (FH if packed else MH) // 2
        wq = LANES if packed else 2 * HEAD_PAD
        n_q = S // TQ
        assert TQ == TK

        def body(*refs):
            if has_bias:
                q_ref, k_ref, v_ref, do_ref, st_ref, ct_ref, dq_ref, dk_ref, dv_ref, dc_ref, dr_ref = refs
            else:
                q_ref, k_ref, v_ref, do_ref, st_ref, dq_ref, dk_ref, dv_ref = refs
            jb = pl.program_id(1)
            k0 = jb * TK

            @pl.when(jb == 0)
            def _():
                dq_ref[...] = jnp.zeros_like(dq_ref)
                if has_bias:
                    dr_ref[...] = jnp.zeros_like(dr_ref)

            dk_ref[...] = jnp.zeros_like(dk_ref)
            dv_ref[...] = jnp.zeros_like(dv_ref)
            kh = _heads(k_ref[...], packed)
            k_both = jnp.concatenate(kh, axis=0) if packed else None
            v_both = jnp.concatenate(_heads(v_ref[...], True), axis=0)
            ck = jnp.concatenate([ct_ref[0, 0][0:1, :], ct_ref[0, 0][1:2, :]], axis=1) if has_bias else None

            def products(ib):
                rows = pl.ds(pl.multiple_of(ib * TQ, TQ), TQ)
                q2, do2 = q_ref[rows, :], do_ref[rows, :]
                if packed:
                    s = _nt(q2, k_both)
                else:
                    qh = _heads(q2, False)
                    s = jnp.concatenate([_nt(qh[0], kh[0]), _nt(qh[1], kh[1])], axis=1)
                return s, _nt(do2, v_both)

            def update(ib, s, dp, carry, masked):
                q0 = pl.multiple_of(ib * TQ, TQ)
                rows = pl.ds(q0, TQ)
                q2, do2, st = q_ref[rows, :], do_ref[rows, :], st_ref[rows, :]
                if not packed:
                    qh = _heads(q2, False)
                if has_bias:
                    s = s - ck
                if masked:
                    allow = _allowed(q0, k0, chunked)
                    s = jnp.where(jnp.concatenate([allow, allow], axis=1), s, NEG)
                p = jnp.concatenate([jnp.exp(s[:, :TK] - st[:, 0:1]), jnp.exp(s[:, TK:] - st[:, 1:2])], axis=1)
                dv2 = _tn(p.astype(BF16), do2)
                ds = jnp.concatenate([p[:, :TK] * (dp[:, :TK] - st[:, 2:3]), p[:, TK:] * (dp[:, TK:] - st[:, 3:4])], axis=1)
                dsb = ds.astype(BF16)
                dv_ref[...] += _merge(dv2[:TK], dv2[TK:])
                if packed:
                    dk2 = _tn(dsb, q2)
                    dk_ref[...] += _merge(dk2[:TK], dk2[TK:])
                    dq_ref[rows, :] += _nn(dsb, k_both)
                else:
                    dk_ref[...] += jnp.concatenate([_tn(dsb[:, :TK], qh[0]), _tn(dsb[:, TK:], qh[1])], axis=1)
                    dq_ref[rows, :] += jnp.concatenate([_nn(dsb[:, :TK], kh[0]), _nn(dsb[:, TK:], kh[1])], axis=1)
                if has_bias:
                    lane = _lane((TQ, LANES))
                    r0 = jnp.sum(ds[:, :TK], axis=-1, keepdims=True)
                    r1 = jnp.sum(ds[:, TK:], axis=-1, keepdims=True)
                    dr_ref[0, rows, :] += jnp.where(lane == 0, r0, jnp.where(lane == 1, r1, 0.0))
                    return carry - jnp.sum(ds, axis=0, keepdims=True)
                return carry

            def step(ib, carry, masked):
                s, dp = products(ib)
                return update(ib, s, dp, carry, masked)

            def two(ib, carry):
                s_a, dp_a = products(ib)
                s_b, dp_b = products(ib + 1)
                return update(ib + 1, s_b, dp_b, update(ib, s_a, dp_a, carry, False), False)

            dc = step(jb, jnp.zeros((1, 2 * TK), F32), True)
            n_rest = n_q - 1 - jb
            dc = lax.fori_loop(0, n_rest // 2, lambda t, cr: two(jb + 1 + 2 * t, cr), dc)
            dc = lax.cond(n_rest % 2 == 1, lambda cr: step(n_q - 1, cr, False), lambda cr: cr, dc)
            if has_bias:
                sub = lax.broadcasted_iota(jnp.int32, (8, TK), 0)
                dc_ref[0, 0] = jnp.where(sub == 0, dc[:, :TK], jnp.where(sub == 1, dc[:, TK:], 0.0))

        in_specs = [pl.BlockSpec((S, wq), lambda p, j: (0, q_blk0 + p)),
                    pl.BlockSpec((TK, wq), lambda p, j: (j, k_blk0 + p)),
                    pl.BlockSpec((TK, LANES), lambda p, j: (j, v_blk0 + p)),
                    pl.BlockSpec((S, LANES), lambda p, j: (0, p)),
                    pl.BlockSpec((S, LANES), lambda p, j: (0, p))]
        args = [q, k, v, do, stats]
        out_specs = [pl.BlockSpec((S, wq), lambda p, j: (0, p)),
                     pl.BlockSpec((TK, wq), lambda p, j: (j, p)),
                     pl.BlockSpec((TK, LANES), lambda p, j: (j, p))]
        out_shape = [jax.ShapeDtypeStruct((S, n_pairs * wq), F32), jax.ShapeDtypeStruct((S, n_pairs * wq), F32),
                     jax.ShapeDtypeStruct((S, n_pairs * LANES), F32)]
        if has_bias:
            in_specs += [pl.BlockSpec((1, 1, 8, TK), lambda p, j: (p, j, 0, 0))]
            args += [cumt]
            out_specs += [pl.BlockSpec((1, 1, 8, TK), lambda p, j: (p, j, 0, 0)),
                          pl.BlockSpec((1, S, LANES), lambda p, j: (p, 0, 0))]
            out_shape += [jax.ShapeDtypeStruct((n_pairs, S // TK, 8, TK), F32),
                          jax.ShapeDtypeStruct((n_pairs, S, LANES), F32)]
        return pl.pallas_call(
            body, name=name, grid=(n_pairs, S // TK), in_specs=in_specs, out_specs=out_specs, out_shape=out_shape,
            compiler_params=_params(("parallel", "arbitrary")))(*args)

    def fox_post(dcum, z, bf_pad):
        def rev(i):
            return n_tok - 1 - i

        def body(dc_ref, z_ref, b_ref, dff_ref, acc_ref, carry):
            @pl.when(pl.program_id(0) == 0)
            def _():
                carry[...] = jnp.zeros_like(carry)
                acc_ref[...] = jnp.zeros_like(acc_ref)

            _, sig_neg = _log_f_terms(z_ref[...], b_ref[...])
            row = lax.broadcasted_iota(jnp.int32, (TM, TM), 0)
            col = lax.broadcasted_iota(jnp.int32, (TM, TM), 1)
            tri = (col >= row).astype(F32)
            dlog = jnp.dot(tri, dc_ref[...], precision=lax.Precision.HIGHEST, preferred_element_type=F32) + carry[...]
            carry[...] = dlog[0:1, :]
            dff = dlog * sig_neg
            dff_ref[...] = dff
            acc_ref[0:1, :] += jnp.sum(dff, axis=0, keepdims=True)

        return pl.pallas_call(
            body, name="fox_post", grid=(n_tok,),
            in_specs=[pl.BlockSpec((TM, LANES), lambda i: (rev(i), 0)),
                      pl.BlockSpec((TM, LANES), lambda i: (rev(i), misc_blk)), const((1, LANES))],
            out_specs=[pl.BlockSpec((TM, LANES), lambda i: (rev(i), 0)), const((8, LANES))],
            out_shape=[jax.ShapeDtypeStruct((S, LANES), F32), jax.ShapeDtypeStruct((8, LANES), F32)],
            scratch_shapes=[pltpu.VMEM((1, LANES), F32)],
            compiler_params=_params(("arbitrary",)))(dcum, z, bf_pad)

    def mla_post(dq, dk, dv, dff, z, gq, gkv, w_uq, w_uk, w_v, cos_t, sin_t):
        def body(dq_ref, dk_ref, dv_ref, dff_ref, ql_ref, kvl_ref, gq_ref, gkv_ref, wq_ref, wk_ref, wv_ref,
                 c_ref, s_ref, zq_ref, zkv_ref, zms_ref, dwq_ref, dwk_ref, dwv_ref, dgq_ref, dgkv_ref):
            @pl.when(pl.program_id(0) == 0)
            def _():
                for r in (dwq_ref, dwk_ref, dwv_ref, dgq_ref, dgkv_ref):
                    r[...] = jnp.zeros_like(r)

            cos1, sin1 = c_ref[...], s_ref[...]
            gqv, gkvv = gq_ref[...], gkv_ref[...]
            qn, qxh, qrstd = _rms(ql_ref[...], gqv)
            dq_pre = _rope_t(dq_ref[...] * MLA_SCALE, jnp.tile(cos1, (1, MH)), jnp.tile(sin1, (1, MH))).astype(BF16)
            dwq_ref[...] += _tn(qn.astype(BF16), dq_pre)
            dql, dgq = _rms_bwd(_nt(dq_pre, wq_ref[...]), gqv, qxh, qrstd)
            zq_ref[...] = dql.astype(BF16)
            dgq_ref[0:1, :] += dgq

            dkv = dk_ref[...]
            lane = _lane(dkv.shape) & (HEAD_PAD - 1)
            dkn = jnp.where(lane < HEAD_DIM, dkv, 0.0).astype(BF16)
            dkr = dkv[:, 0:HEAD_PAD]
            for hd in range(1, MH):
                dkr = dkr + dkv[:, hd * HEAD_PAD:(hd + 1) * HEAD_PAD]
            lane1 = _lane(dkr.shape)
            dkr = jnp.where((lane1 >= ROPE_LO) & (lane1 < ROPE_LO + ROPE_DIM), dkr, 0.0)
            dkr = _rope_t(dkr, cos1, sin1)
            zms_ref[...] = (dkr + dff_ref[...]).astype(BF16)

            kvn, kxh, krstd = _rms(kvl_ref[...], gkvv)
            kvb = kvn.astype(BF16)
            dvb = dv_ref[...].astype(BF16)
            dwk_ref[...] += _tn(kvb, dkn)
            dwv_ref[...] += _tn(kvb, dvb)
            dkvl, dgkv = _rms_bwd(_nt(dkn, wk_ref[...]) + _nt(dvb, wv_ref[...]), gkvv, kxh, krstd)
            zkv_ref[...] = dkvl.astype(BF16)
            dgkv_ref[0:1, :] += dgkv

        return pl.pallas_call(
            body, name="mla_post", grid=(n_tok,),
            in_specs=[tok(QW), tok(QW), tok(MW), tok(LANES), tok(QL, (cfg.o_ql - ZO) // QL), tok(KVL, (cfg.o_kv - ZO) // KVL),
                      const((1, QL)), const((1, KVL)), const((QL, QW)), const((KVL, QW)), const((KVL, MW)),
                      tok(LANES), tok(LANES)],
            out_specs=[tok(QL), tok(KVL), tok(LANES), const((QL, QW)), const((KVL, QW)), const((KVL, MW)),
                       const((8, QL)), const((8, KVL))],
            out_shape=[jax.ShapeDtypeStruct((S, QL), BF16), jax.ShapeDtypeStruct((S, KVL), BF16),
                       jax.ShapeDtypeStruct((S, LANES), BF16), jax.ShapeDtypeStruct((QL, QW), F32),
                       jax.ShapeDtypeStruct((KVL, QW), F32), jax.ShapeDtypeStruct((KVL, MW), F32),
                       jax.ShapeDtypeStruct((8, QL), F32), jax.ShapeDtypeStruct((8, KVL), F32)],
            compiler_params=_params(("arbitrary",)))(dq, dk, dv, dff, z, z, gq, gkv, w_uq, w_uk, w_v, cos_t, sin_t)

    def bwd_in(dz, w_in, x, dxn, g, scale):
        def body(dz_ref, w_ref, x_ref, dx_ref, g_ref, sc_ref, o_ref, acc_ref):
            @pl.when(pl.program_id(0) == 0)
            def _():
                acc_ref[...] = jnp.zeros_like(acc_ref)

            dh = _nt(dz_ref[...], w_ref[...])
            gv, mod = g_ref[...], 1.0 + sc_ref[...]
            _, xh, rstd = _rms(x_ref[...], gv)
            t = dh * xh
            acc_ref[0:1, :] += jnp.sum(dh, axis=0, keepdims=True)
            acc_ref[1:2, :] += jnp.sum(t * gv, axis=0, keepdims=True)
            acc_ref[2:3, :] += jnp.sum(t * mod, axis=0, keepdims=True)
            dx, _ = _rms_bwd(dh, gv * mod, xh, rstd)
            o_ref[...] = dx_ref[...] + dx

        return pl.pallas_call(
            body, name="bwd_in", grid=(n_tok,),
            in_specs=[tok(NZ), const((D, NZ)), tok(D), tok(D), const((1, D)), const((1, D))],
            out_specs=[tok(D), const((8, D))],
            out_shape=[jax.ShapeDtypeStruct((S, D), F32), jax.ShapeDtypeStruct((8, D), F32)],
            compiler_params=_params(("arbitrary",)))(dz, w_in, x, dxn, g, scale)


    def pair_rows(cum):
        n_pairs = FH // 2
        ct = jnp.pad(cum[:, :FH].T.reshape(n_pairs, 2, S), ((0, 0), (0, 6), (0, 0)))
        return ct.reshape(n_pairs, 8, S // TK, TK).transpose(0, 2, 1, 3)

    def bias_grad(dc, dr):
        n_pairs = FH // 2
        d = dc.transpose(0, 2, 1, 3).reshape(n_pairs, 8, S)[:, :2, :].reshape(FH, S).T
        d = d + dr[:, :, :2].transpose(1, 0, 2).reshape(S, FH)
        return jnp.pad(d, ((0, 0), (0, LANES - FH)))

    def layer_forward(x, wl, mod):
        shift, scale, gate = mod
        h, z, qkv = ln_inproj(x, wl.norm_g, scale, shift, wl.w_in)
        ct = pair_rows(fox_prep(z, wl.bf_pad))
        of, lse_f = attn_fwd(qkv, qkv, qkv, cfg.o_fq // LANES, cfg.o_fk // LANES, cfg.o_fv // LANES, ct,
                             True, False, "fox_fwd")
        qp, kp, vp = mla_prep(z, wl.gq, wl.gkv, wl.w_uq, wl.w_uk, wl.w_v, cfg.cos_t, cfg.sin_t)
        om, lse_m = attn_fwd(qp, kp, vp, 0, 0, 0, None, False, True, "mla_fwd")
        xn, u, y = gate_outproj(of, om, z, wl.w_out, x, gate)
        saved = types.SimpleNamespace(x=x, h=h, z=z, qkv=qkv, ct=ct, of=of, lse_f=lse_f, qp=qp, kp=kp, vp=vp,
                                      om=om, lse_m=lse_m, u=u, y=y)
        return xn, saved

    def layer_backward(dxn, sv, wl, mod):
        shift, scale, gate = mod
        do_f, do_m, dfg, dmg, dy, acc_o, st_f, st_m = bwd_out(dxn, gate, sv.y, wl.w_out, sv.of, sv.om, sv.z,
                                                              sv.lse_f, sv.lse_m)
        dw_out = matmul_tn(sv.u, dy, "dw_out")
        dfq, dfk, dfv, dck, dcr = attn_bwd(sv.qkv, sv.qkv, sv.qkv, do_f, st_f, cfg.o_fq // LANES, cfg.o_fk // LANES,
                                           cfg.o_fv // LANES, sv.ct, True, False, "fox_bwd")
        dff, acc_f = fox_post(bias_grad(dck, dcr), sv.z, wl.bf_pad)
        dqp, dkp, dvp = attn_bwd(sv.qp, sv.kp, sv.vp, do_m, st_m, 0, 0, 0, None, False, True, "mla_bwd")
        zq, zkv, zms, dw_uq, dw_uk, dw_v, dgq, dgkv = mla_post(
            dqp, dkp, dvp, dff, sv.z, wl.gq, wl.gkv, wl.w_uq, wl.w_uk, wl.w_v, cfg.cos_t, cfg.sin_t)
        dz = jnp.concatenate([(dfq * FOX_SCALE).astype(BF16), dfk.astype(BF16), dfv.astype(BF16), dfg, dmg, zq, zkv, zms], axis=1)
        dx, acc_i = bwd_in(dz, wl.w_in, sv.x, dxn, wl.norm_g, scale)
        dw_in = matmul_tn(sv.h, dz, "dw_in")
        grads = types.SimpleNamespace(
            w_in=dw_in, w_out=dw_out, w_uq=dw_uq, w_uk=dw_uk, w_v=dw_v, gq=dgq[0], gkv=dgkv[0],
            b_f=acc_f[0, :FH], norm_g=acc_i[2], dmod=jnp.concatenate([acc_i[0], acc_i[1], acc_o[0]]))
        return dx, grads

    return types.SimpleNamespace(layer_forward=layer_forward, layer_backward=layer_backward, final_loss=final_loss)


def _pack_rows(parts, dtype, row_multiple):
    flat = jnp.concatenate([p.reshape(-1).astype(dtype) for p in parts])
    per = SLAB_COLS * row_multiple
    total = -(-flat.shape[0] // per) * per
    return jnp.pad(flat, (0, total - flat.shape[0])).reshape(total // SLAB_COLS, SLAB_COLS)


def _unpack(flat, shapes):
    out, off = [], 0
    for shp in shapes:
        n = 1
        for d in shp:
            n *= d
        out.append(flat[off:off + n].reshape(shp))
        off += n
    return out


def kernel(x, c, positions, norm_g, w_ada, b_ada, w_in, b_f, q_norm_g, w_uq, kv_norm_g, w_ukv, w_out, final_g, loss_target, m_norm_g, m_w_ada, m_b_ada, m_w_in, m_b_f, m_q_norm_g, m_w_uq, m_kv_norm_g, m_w_ukv, m_w_out, m_final_g, v_norm_g, v_w_ada, v_b_ada, v_w_in, v_b_f, v_q_norm_g, v_w_uq, v_kv_norm_g, v_w_ukv, v_w_out, v_final_g):
    S, D = x.shape[1], x.shape[2]
    L = norm_g.shape[0]
    FH = b_f.shape[1]
    QL, KVL = q_norm_g.shape[1], kv_norm_g.shape[1]
    MH = w_ukv.shape[2] * N_CHIPS // (2 * HEAD_DIM)
    FW, MW = FH * HEAD_DIM, MH * HEAD_DIM
    NA = w_ada.shape[2]
    n_in = w_in.shape[2] * N_CHIPS
    cfg = types.SimpleNamespace(S=S, D=D, FW=FW, MW=MW, QL=QL, KVL=KVL, FH=FH, MH=MH)
    cfg.o_fq, cfg.o_fk, cfg.o_fv, cfg.o_fg, cfg.o_mg = 0, FW, 2 * FW, 3 * FW, 4 * FW
    cfg.o_ql = 4 * FW + MW
    cfg.o_kv = cfg.o_ql + QL
    cfg.o_ms = cfg.o_kv + KVL
    cfg.NZ = cfg.o_ms + LANES
    assert FW == MW and FH % 2 == 0 and MH % 2 == 0 and cfg.o_ql % QL == 0 and cfg.o_kv % KVL == 0 and KVL == LANES
    assert n_in == 4 * FW + FH + QL + KVL + ROPE_DIM + MW

    mx, my, mc = _my_pos()
    my_chip = 2 * mx + my
    my_dev = 2 * my_chip + mc

    inv_freq = 1.0 / (ROPE_THETA ** (jnp.arange(0, ROPE_DIM, 2, dtype=F32) / ROPE_DIM))
    ang = positions[0].astype(F32)[:, None] * inv_freq
    cos, sin = jnp.cos(ang), jnp.sin(ang)
    cfg.cos_t = jnp.concatenate([jnp.ones((S, ROPE_LO), F32), cos, cos, jnp.ones((S, HEAD_PAD - ROPE_LO - ROPE_DIM), F32)], axis=1)
    cfg.sin_t = jnp.concatenate([jnp.zeros((S, ROPE_LO), F32), -sin, sin, jnp.zeros((S, HEAD_PAD - ROPE_LO - ROPE_DIM), F32)], axis=1)

    assert L % 2 == 0
    def lane_pad(a):
        return jnp.pad(a, ((0, 0),) * (a.ndim - 1) + ((0, -a.shape[-1] % LANES),))

    shards = (w_in, w_uq, w_ukv, w_out)
    gathered = weights_gather([lane_pad(w.astype(BF16)) for w in shards], "weights_gather")
    def all_chips(g, w, axis):
        return jnp.concatenate([jnp.where(my_chip == t, w.astype(BF16), g[t][..., :w.shape[-1]]) for t in range(N_CHIPS)],
                               axis=axis)

    w_in_f = all_chips(gathered[0], w_in, 2)
    w_uq_f = all_chips(gathered[1], w_uq, 2)
    w_ukv_f = all_chips(gathered[2], w_ukv, 2)
    w_out_f = all_chips(gathered[3], w_out, 1)

    sizes = (FW, FW, FW, FH, FW, QL, KVL, ROPE_DIM, MW)
    offs = [0]
    for sz in sizes:
        offs.append(offs[-1] + sz)
    seg = [w_in_f[:, :, offs[i]:offs[i + 1]] for i in range(len(sizes))]
    fq_w, fk_w, fv_w, ff_w, fg_w, ql_w, kvl_w, kr_w, mg_w = seg
    zeros = lambda n: jnp.zeros((L, D, n), BF16)
    w_in_p = jnp.concatenate([fq_w, fk_w, fv_w, fg_w, mg_w, ql_w, kvl_w, ff_w, zeros(ROPE_LO - FH), kr_w,
                              zeros(HEAD_PAD - ROPE_LO - ROPE_DIM)], axis=2)
    w_uq_p = jnp.pad(w_uq_f.reshape(L, QL, MH, HEAD_DIM + ROPE_DIM), ((0, 0), (0, 0), (0, 0), (0, HEAD_PAD - HEAD_DIM - ROPE_DIM)))
    w_uq_p = w_uq_p.reshape(L, QL, MH * HEAD_PAD)
    w_ukv4 = w_ukv_f.reshape(L, KVL, MH, 2 * HEAD_DIM)
    w_uk_p = jnp.pad(w_ukv4[..., :HEAD_DIM], ((0, 0), (0, 0), (0, 0), (0, HEAD_PAD - HEAD_DIM))).reshape(L, KVL, MH * HEAD_PAD)
    w_v_p = w_ukv4[..., HEAD_DIM:].reshape(L, KVL, MW)

    c_all = allgather8(c.reshape(8, D // 8), "gather_c").reshape(N_DEV, D)
    c_pad = jnp.pad(c_all, ((0, 16 - N_DEV), (0, 0)))
    mod_part = ada_forward(c_pad, w_ada)[:, :N_DEV, :]
    mod_all = allgather8(mod_part.reshape(-1, LANES), "gather_mod").reshape(N_CHIPS, 2, L, N_DEV, NA)[:, 0]
    mod_full = mod_all.transpose(1, 2, 0, 3).reshape(L, N_DEV, N_CHIPS * NA) + b_ada[:, None, :]
    mod_mine = lax.dynamic_index_in_dim(mod_full, my_dev, axis=1, keepdims=True)

    step = make_step(cfg)
    bf_pad = jnp.pad(b_f, ((0, 0), (0, LANES - FH)))
    layers, mods = [], []
    for l in range(L):
        layers.append(types.SimpleNamespace(
            norm_g=norm_g[l][None], w_in=w_in_p[l], bf_pad=bf_pad[l][None], gq=q_norm_g[l][None], gkv=kv_norm_g[l][None],
            w_uq=w_uq_p[l], w_uk=w_uk_p[l], w_v=w_v_p[l], w_out=w_out_f[l]))
        mods.append((mod_mine[l, :, :D], mod_mine[l, :, D:2 * D], mod_mine[l, :, 2 * D:]))

    xl = x[0]
    saved = []
    for l in range(L):
        xl, sv = step.layer_forward(xl, layers[l], mods[l])
        saved.append(sv)
    dx, acc_fin, loss_part = step.final_loss(xl, final_g[None], loss_target[0])
    loss = lax.psum(loss_part[0, 0], ("x", "y", "c"))
    gl = [None] * L
    for l in reversed(range(L)):
        dx, gl[l] = step.layer_backward(dx, saved[l], layers[l], mods[l])
    grad_x = dx[None]

    stack = lambda name: jnp.stack([getattr(g, name) for g in gl])
    small_parts = [stack("norm_g"), stack("dmod"), stack("b_f"), stack("gq"), stack("gkv"), acc_fin[0]]
    small_shapes = [p.shape for p in small_parts]
    small = _pack_rows(small_parts, F32, 8).reshape(-1, LANES)
    small_all = allgather8(small, "gather_small").reshape(N_DEV, -1, LANES)
    small_sum = sum_leading(small_all, "sum_small")
    g_norm_g, g_b_ada, g_b_f, g_q_norm_g, g_kv_norm_g, g_final_g = _unpack(small_sum.reshape(-1), small_shapes)

    n_ng = L * D
    dmod_all = small_all.reshape(N_DEV, -1)[:, n_ng:n_ng + L * 3 * D].reshape(N_DEV, L, 3 * D)
    dmod_cols = lax.dynamic_slice_in_dim(dmod_all, my_chip * NA, NA, axis=2).transpose(1, 0, 2)
    g_w_ada = ada_backward(c_pad, jnp.pad(dmod_cols, ((0, 0), (0, 16 - N_DEV), (0, 0))))

    ms = cfg.o_ms
    runs = [(0, 3 * FW, 0), (3 * FW, FH, ms), (3 * FW + FH, FW, cfg.o_fg), (4 * FW + FH, QL + KVL, cfg.o_ql),
            (4 * FW + FH + QL + KVL, ROPE_DIM, ms + ROPE_LO), (4 * FW + FH + QL + KVL + ROPE_DIM, MW, cfg.o_mg)]

    def shard_columns(dw, t):
        a, b = t * w_in.shape[2], (t + 1) * w_in.shape[2]
        return jnp.concatenate([dw[:, p0 + max(a, r0) - r0:p0 + min(b, r0 + sz) - r0]
                                for r0, sz, p0 in runs if max(a, r0) < min(b, r0 + sz)], axis=1)

    g_in4 = jnp.stack([jnp.stack([shard_columns(g.w_in, t) for g in gl]) for t in range(N_CHIPS)])
    dw_uq_f = stack("w_uq").reshape(L, QL, MH, HEAD_PAD)[..., :HEAD_DIM + ROPE_DIM].reshape(L, QL, -1)
    dw_ukv_f = jnp.concatenate([stack("w_uk").reshape(L, KVL, MH, HEAD_PAD)[..., :HEAD_DIM],
                                stack("w_v").reshape(L, KVL, MH, HEAD_DIM)], axis=3).reshape(L, KVL, -1)
    dw_out_f = stack("w_out")
    gs = [g_in4, dw_uq_f.reshape(L, QL, N_CHIPS, -1).transpose(2, 0, 1, 3),
          dw_ukv_f.reshape(L, KVL, N_CHIPS, -1).transpose(2, 0, 1, 3), dw_out_f.reshape(L, N_CHIPS, -1, D).transpose(1, 0, 2, 3)]
    gs = [lane_pad(g) for g in gs]
    hl = L // 2
    theirs = halves_to_sibling(gs, "grads_sibling")
    chip_part = []
    big_names = ["w_in", "w_uq", "w_ukv", "w_out"]
    for g, o, nm in zip(gs, theirs, big_names):
        keep = lax.dynamic_slice_in_dim(g, mc * hl, hl, axis=1)
        merged = (N_CHIPS * hl,) + g.shape[2:]
        chip_part.append(add_cast(keep.reshape(merged), o.reshape(merged), BF16, "grads_chip_sum_" + nm).reshape(o.shape))
    parts = chip_exchange(chip_part, "grads_chips")
    where = jnp.stack([my_chip, mc]).astype(jnp.int32)
    red = [sum_chips(p, o, where, "grads_sum_" + nm) for p, o, nm in zip(parts, chip_part, big_names)]
    g_w_in, g_w_uq, g_w_ukv, g_w_out = [g[..., :w.shape[-1]] for g, w in zip(halves_gather(red, "grads_back"), shards)]


    names = ["norm_g", "w_ada", "b_ada", "w_in", "b_f", "q_norm_g", "w_uq", "kv_norm_g", "w_ukv", "w_out", "final_g"]
    ws = dict(norm_g=norm_g, w_ada=w_ada, b_ada=b_ada, w_in=w_in, b_f=b_f, q_norm_g=q_norm_g, w_uq=w_uq,
              kv_norm_g=kv_norm_g, w_ukv=w_ukv, w_out=w_out, final_g=final_g)
    msd = dict(norm_g=m_norm_g, w_ada=m_w_ada, b_ada=m_b_ada, w_in=m_w_in, b_f=m_b_f, q_norm_g=m_q_norm_g, w_uq=m_w_uq,
               kv_norm_g=m_kv_norm_g, w_ukv=m_w_ukv, w_out=m_w_out, final_g=m_final_g)
    vsd = dict(norm_g=v_norm_g, w_ada=v_w_ada, b_ada=v_b_ada, w_in=v_w_in, b_f=v_b_f, q_norm_g=v_q_norm_g, w_uq=v_w_uq,
               kv_norm_g=v_kv_norm_g, w_ukv=v_w_ukv, w_out=v_w_out, final_g=v_final_g)
    gsd = dict(norm_g=g_norm_g, w_ada=g_w_ada, b_ada=g_b_ada, w_in=g_w_in, b_f=g_b_f, q_norm_g=g_q_norm_g, w_uq=g_w_uq,
               kv_norm_g=g_kv_norm_g, w_ukv=g_w_ukv, w_out=g_w_out, final_g=g_final_g)
    small_names = ["norm_g", "b_ada", "b_f", "q_norm_g", "kv_norm_g", "final_g"]
    sm_shapes = [ws[n].shape for n in small_names]
    pk = lambda d: _pack_rows([d[n] for n in small_names], F32, 8).reshape(1, -1, LANES)
    sm_out = adamw(pk(ws), pk(gsd), pk(msd), pk(vsd), "adamw_small")
    sm_d, sm_m, sm_v = [dict(zip(small_names, _unpack(o.reshape(-1), sm_shapes))) for o in sm_out]
    delta, new_m, new_v = dict(sm_d), dict(sm_m), dict(sm_v)
    for n in ["w_ada", "w_in", "w_uq", "w_ukv", "w_out"]:
        delta[n], new_m[n], new_v[n] = adamw(ws[n], gsd[n], msd[n], vsd[n], "adamw_" + n)

    return (loss, grad_x, *[gsd[n] for n in names], *[delta[n] for n in names],
            *[new_m[n] for n in names], *[new_v[n] for n in names])
```

```python
import types

import jax
import jax.numpy as jnp
from jax import lax
from jax.experimental import pallas as pl
from jax.experimental.pallas import tpu as pltpu

F32 = jnp.float32
BF16 = jnp.bfloat16
MESH = pl.DeviceIdType.MESH

N_CHIPS = 4
N_DEV = 8
HEAD_DIM = 64
ROPE_DIM = 32
ROPE_THETA = 10000.0
HEAD_PAD = 128
ROPE_LO = 64
ROPE_HALF = 16
CHUNK_SHIFT = 6
LANES = 128
EPS = 1e-6
NEG = -1e30
ADAM_LR = 0.001
ADAM_B1 = 0.9
ADAM_B2 = 0.999
ADAM_EPS = 1e-08
ADAM_WD = 0.01
ADAM_STEP = 10
VMEM_LIMIT = 48 * 1024 * 1024
SLAB_COLS = 1024


def _params(sem=None, vmem=VMEM_LIMIT):
    return pltpu.CompilerParams(dimension_semantics=sem, vmem_limit_bytes=vmem)


def _nn(a, b):
    return jnp.dot(a, b, preferred_element_type=F32)


def _nt(a, b):
    return lax.dot_general(a, b, (((1,), (1,)), ((), ())), preferred_element_type=F32)


def _tn(a, b):
    return lax.dot_general(a, b, (((0,), (0,)), ((), ())), preferred_element_type=F32)


def _sigmoid(x):
    return 1.0 / (1.0 + jnp.exp(-x))


def _lane(shape):
    return lax.broadcasted_iota(jnp.int32, shape, len(shape) - 1)


def _pick(n, cands):
    for c in cands:
        if n % c == 0:
            return c
    return n


def _my_pos():
    return lax.axis_index("x"), lax.axis_index("y"), lax.axis_index("c")


def allgather8(xs, name):
    m_per, n = xs.shape

    def body(x_ref, out_ref, send_sems, recv_sems, local_sem):
        x, y, c = _my_pos()
        me, sibling = (x, y, c), (x, y, 1 - c)
        chips = [(1 - x, y), (x, 1 - y), (1 - x, 1 - y)]

        def rows(px, py, pc):
            return out_ref.at[pl.ds((4 * px + 2 * py + pc) * m_per, m_per), :]

        def copy(k, block, to, src=None):
            return pltpu.make_async_remote_copy(
                src_ref=rows(*block) if src is None else src, dst_ref=rows(*block),
                send_sem=send_sems.at[k], recv_sem=recv_sems.at[k], device_id=to, device_id_type=MESH)

        mine = pltpu.make_async_copy(x_ref, rows(*me), local_sem)
        mine.start()
        first = [copy(0, me, sibling, src=x_ref)]
        first += [copy(1 + j, me, (*chip, c), src=x_ref) for j, chip in enumerate(chips)]
        for cp in first:
            cp.start()
        passed = [copy(4 + j, (*chip, c), sibling) for j, chip in enumerate(chips)]
        for j, chip in enumerate(chips):
            copy(1 + j, (*chip, c), me).wait_recv()
            passed[j].start()
        copy(0, sibling, me).wait_recv()
        for j, chip in enumerate(chips):
            copy(4 + j, (*chip, 1 - c), me).wait_recv()
        for cp in first + passed:
            cp.wait_send()
        mine.wait()

    return pl.pallas_call(
        body, name=name,
        out_shape=jax.ShapeDtypeStruct((N_DEV * m_per, n), xs.dtype),
        in_specs=[pl.BlockSpec(memory_space=pltpu.VMEM)],
        out_specs=pl.BlockSpec(memory_space=pltpu.VMEM),
        scratch_shapes=[pltpu.SemaphoreType.DMA((7,)), pltpu.SemaphoreType.DMA((7,)), pltpu.SemaphoreType.DMA],
    )(xs)


def _remote(src, dst, send_sems, recv_sems, k, to):
    return pltpu.make_async_remote_copy(src_ref=src, dst_ref=dst, send_sem=send_sems.at[k], recv_sem=recv_sems.at[k],
                                        device_id=to, device_id_type=MESH)


def _hbm_call(body, name, ins, out_shapes, n_sems, aliases=None):
    hbm = pl.BlockSpec(memory_space=pl.ANY)
    scratch = [pltpu.SemaphoreType.DMA((n_sems,)), pltpu.SemaphoreType.DMA((n_sems,))]
    return pl.pallas_call(body, name=name, out_shape=out_shapes, in_specs=[hbm] * len(ins),
                          out_specs=[hbm] * len(out_shapes), scratch_shapes=scratch,
                          input_output_aliases=aliases or {})(*ins)


def _layers_half(ref, h, axis=0):
    size = ref.shape[axis] // 2
    idx = (slice(None),) * axis + (pl.ds(h * size, size),)
    return ref.at[idx]


def weights_gather(ws, name):
    n = len(ws)

    def body(*refs):
        x_refs, o_refs = refs[:n], refs[n:2 * n]
        send_sems, recv_sems = refs[2 * n:]
        x, y, c = _my_pos()
        s = 2 * x + y
        sibling = (x, y, 1 - c)
        chips = [(1 - x, y), (x, 1 - y), (1 - x, 1 - y)]
        sends = []
        for a in range(n):
            for k, (tx, ty) in enumerate(chips):
                cp = _remote(_layers_half(x_refs[a], c), _layers_half(o_refs[a].at[s], c), send_sems, recv_sems,
                             6 * a + k, (tx, ty, c))
                cp.start()
                sends.append(cp)
        for a in range(n):
            for k, (tx, ty) in enumerate(chips):
                got = _layers_half(o_refs[a].at[2 * tx + ty], c)
                _remote(got, got, send_sems, recv_sems, 6 * a + k, (tx, ty, c)).wait_recv()
                cp = _remote(got, got, send_sems, recv_sems, 6 * a + 3 + k, sibling)
                cp.start()
                sends.append(cp)
        for a in range(n):
            for k, (tx, ty) in enumerate(chips):
                theirs = _layers_half(o_refs[a].at[2 * tx + ty], 1 - c)
                _remote(theirs, theirs, send_sems, recv_sems, 6 * a + 3 + k, sibling).wait_recv()
        for cp in sends:
            cp.wait_send()

    outs = [jax.ShapeDtypeStruct((N_CHIPS,) + w.shape, w.dtype) for w in ws]
    return _hbm_call(body, name, ws, outs, 6 * n)


def halves_to_sibling(gs, name):
    n = len(gs)

    def body(*refs):
        x_refs, o_refs, send_sems, recv_sems = refs[:n], refs[n:2 * n], refs[2 * n], refs[2 * n + 1]
        x, y, c = _my_pos()
        cps = [_remote(_layers_half(x_refs[a], 1 - c, axis=1), o_refs[a], send_sems, recv_sems, a, (x, y, 1 - c))
               for a in range(n)]
        for cp in cps:
            cp.start()
        for cp in cps:
            cp.wait()

    outs = [jax.ShapeDtypeStruct((g.shape[0], g.shape[1] // 2) + g.shape[2:], g.dtype) for g in gs]
    return _hbm_call(body, name, gs, outs, n)


def chip_exchange(xs, name):
    n = len(xs)

    def body(*refs):
        x_refs, o_refs = refs[:n], refs[n:2 * n]
        send_sems, recv_sems = refs[2 * n:]
        x, y, c = _my_pos()
        s = 2 * x + y
        chips = [(1 - x, y), (x, 1 - y), (1 - x, 1 - y)]
        sends = []
        for a in range(n):
            for k, (tx, ty) in enumerate(chips):
                cp = _remote(x_refs[a].at[2 * tx + ty], o_refs[a].at[s], send_sems, recv_sems, 3 * a + k, (tx, ty, c))
                cp.start()
                sends.append(cp)
        for a in range(n):
            for k, (tx, ty) in enumerate(chips):
                got = o_refs[a].at[2 * tx + ty]
                _remote(got, got, send_sems, recv_sems, 3 * a + k, (tx, ty, c)).wait_recv()
        for cp in sends:
            cp.wait_send()

    outs = [jax.ShapeDtypeStruct(v.shape, v.dtype) for v in xs]
    return _hbm_call(body, name, xs, outs, 3 * n)


def halves_gather(xs, name):
    n = len(xs)

    def body(*refs):
        x_refs, o_refs, send_sems, recv_sems = refs[:n], refs[n:2 * n], refs[2 * n], refs[2 * n + 1]
        x, y, c = _my_pos()
        sends = []
        for a in range(n):
            cp = _remote(_layers_half(x_refs[a], c), _layers_half(o_refs[a], c), send_sems, recv_sems, a, (x, y, 1 - c))
            cp.start()
            sends.append(cp)
        for a in range(n):
            theirs = _layers_half(o_refs[a], 1 - c)
            _remote(theirs, theirs, send_sems, recv_sems, a, (x, y, 1 - c)).wait_recv()
        for cp in sends:
            cp.wait_send()

    outs = [jax.ShapeDtypeStruct(v.shape, v.dtype) for v in xs]
    return _hbm_call(body, name, xs, outs, n, aliases={a: a for a in range(n)})


def sum_leading(xs, name):
    n, r, c = xs.shape

    def body(x_ref, o_ref):
        acc = x_ref[0]
        for i in range(1, n):
            acc = acc + x_ref[i]
        o_ref[...] = acc

    return pl.pallas_call(body, name=name, out_shape=jax.ShapeDtypeStruct((r, c), xs.dtype))(xs)


def add_cast(a, b, out_dtype, name):
    n, r, c = a.shape
    tr = _pick(r, (512, 256, 128, 64, 32, 16))

    def body(a_ref, b_ref, o_ref):
        o_ref[...] = (a_ref[...].astype(F32) + b_ref[...].astype(F32)).astype(out_dtype)

    spec = pl.BlockSpec((1, tr, c), lambda i, j: (i, j, 0))
    return pl.pallas_call(body, name=name, grid=(n, r // tr), in_specs=[spec, spec], out_specs=spec,
                          out_shape=jax.ShapeDtypeStruct((n, r, c), out_dtype),
                          compiler_params=_params(("parallel", "parallel")))(a, b)


def sum_chips(parts, own, where, name):
    n, nl, r, c = parts.shape
    tr = _pick(r, (512, 256, 128, 64, 32, 16))

    def body(w_ref, p_ref, o_ref, out_ref):
        s = w_ref[0]
        acc = jnp.zeros(out_ref.shape, F32)
        for t in range(n):
            acc = acc + jnp.where(s == t, o_ref[0], p_ref[t]).astype(F32)
        out_ref[...] = acc

    grid_spec = pltpu.PrefetchScalarGridSpec(
        num_scalar_prefetch=1, grid=(nl, r // tr),
        in_specs=[pl.BlockSpec((n, 1, tr, c), lambda i, j, w: (0, i, j, 0)),
                  pl.BlockSpec((1, 1, tr, c), lambda i, j, w: (w[0], i, j, 0))],
        out_specs=pl.BlockSpec((1, tr, c), lambda i, j, w: (w[1] * nl + i, j, 0)))
    return pl.pallas_call(body, name=name, grid_spec=grid_spec, out_shape=jax.ShapeDtypeStruct((2 * nl, r, c), F32),
                          compiler_params=_params(("parallel", "parallel")))(where, parts, own)


def ada_forward(c_all, w_ada):
    nl, d, n = w_ada.shape
    nb = c_all.shape[0]

    def body(c_ref, w_ref, o_ref):
        cv = c_ref[...]
        ca = (cv * _sigmoid(cv)).astype(BF16)
        o_ref[0] = _nn(ca, w_ref[0].astype(BF16))

    return pl.pallas_call(
        body, name="ada_forward", grid=(nl,),
        in_specs=[pl.BlockSpec((nb, d), lambda l: (0, 0)), pl.BlockSpec((1, d, n), lambda l: (l, 0, 0))],
        out_specs=pl.BlockSpec((1, nb, n), lambda l: (l, 0, 0)),
        out_shape=jax.ShapeDtypeStruct((nl, nb, n), F32), compiler_params=_params(("parallel",)))(c_all, w_ada)


def ada_backward(c_all, dmod):
    nl, nb, n = dmod.shape
    d = c_all.shape[1]

    def body(c_ref, g_ref, o_ref):
        cv = c_ref[...]
        ca = (cv * _sigmoid(cv)).astype(BF16)
        o_ref[0] = _tn(ca, g_ref[0].astype(BF16))

    return pl.pallas_call(
        body, name="ada_backward", grid=(nl,),
        in_specs=[pl.BlockSpec((nb, d), lambda l: (0, 0)), pl.BlockSpec((1, nb, n), lambda l: (l, 0, 0))],
        out_specs=pl.BlockSpec((1, d, n), lambda l: (l, 0, 0)),
        out_shape=jax.ShapeDtypeStruct((nl, d, n), F32), compiler_params=_params(("parallel",)))(c_all, dmod)


def matmul_tn(a, b, name):
    k, m = a.shape
    n = b.shape[1]
    tm, tk = _pick(m, (512, 256, 128)), _pick(k, (512, 256, 128))
    tn = n if n * (tm * 8 + tk * 4) <= VMEM_LIMIT // 2 else _pick(n, (512, 256, 128))

    def body(a_ref, b_ref, o_ref):
        @pl.when(pl.program_id(2) == 0)
        def _():
            o_ref[...] = jnp.zeros_like(o_ref)

        o_ref[...] += _tn(a_ref[...], b_ref[...])

    return pl.pallas_call(
        body, name=name, grid=(m // tm, n // tn, k // tk),
        in_specs=[pl.BlockSpec((tk, tm), lambda i, j, kk: (kk, i)), pl.BlockSpec((tk, tn), lambda i, j, kk: (kk, j))],
        out_specs=pl.BlockSpec((tm, tn), lambda i, j, kk: (i, j)),
        out_shape=jax.ShapeDtypeStruct((m, n), F32),
        compiler_params=_params(("parallel", "parallel", "arbitrary")))(a, b)


def adamw(w, g, m, v, name):
    nl, r, c = w.shape
    tr = _pick(r, (512, 256, 128, 64, 32, 16, 8))

    def body(w_ref, g_ref, m_ref, v_ref, d_ref, mo_ref, vo_ref):
        gv = g_ref[...]
        mn = ADAM_B1 * m_ref[...] + (1.0 - ADAM_B1) * gv
        vn = ADAM_B2 * v_ref[...] + (1.0 - ADAM_B2) * (gv * gv)
        m_hat = mn / (1.0 - ADAM_B1 ** ADAM_STEP)
        v_hat = vn / (1.0 - ADAM_B2 ** ADAM_STEP)
        d_ref[...] = -ADAM_LR * (m_hat / (jnp.sqrt(v_hat) + ADAM_EPS) + ADAM_WD * w_ref[...])
        mo_ref[...] = mn
        vo_ref[...] = vn

    spec = pl.BlockSpec((1, tr, c), lambda l, i: (l, i, 0))
    out = jax.ShapeDtypeStruct((nl, r, c), F32)
    return pl.pallas_call(body, name=name, grid=(nl, r // tr), in_specs=[spec] * 4, out_specs=[spec] * 3,
                          out_shape=[out] * 3, compiler_params=_params(("parallel", "parallel")))(w, g, m, v)


def _rope(t, cos_t, sin_t):
    w = t.shape[1]
    lane = _lane(t.shape) & (HEAD_PAD - 1)
    first_half = (lane >= ROPE_LO) & (lane < ROPE_LO + ROPE_HALF)
    partner = jnp.where(first_half, pltpu.roll(t, w - ROPE_HALF, 1), pltpu.roll(t, ROPE_HALF, 1))
    return t * cos_t + partner * sin_t


def _rope_t(dt, cos_t, sin_t):
    w = dt.shape[1]
    lane = _lane(dt.shape) & (HEAD_PAD - 1)
    first_half = (lane >= ROPE_LO) & (lane < ROPE_LO + ROPE_HALF)
    ds = dt * sin_t
    partner = jnp.where(first_half, pltpu.roll(ds, w - ROPE_HALF, 1), pltpu.roll(ds, ROPE_HALF, 1))
    return dt * cos_t + partner


def _rms(xv, g):
    rstd = lax.rsqrt(jnp.mean(xv * xv, axis=-1, keepdims=True) + EPS)
    xh = xv * rstd
    return xh * g, xh, rstd


def _rms_bwd(dy, g, xh, rstd):
    dxh = dy * g
    dx = rstd * (dxh - xh * jnp.mean(dxh * xh, axis=-1, keepdims=True))
    return dx, jnp.sum(dy * xh, axis=0, keepdims=True)


def make_step(cfg):
    S, D, NZ = cfg.S, cfg.D, cfg.NZ
    FW, MW, QL, KVL, FH, MH = cfg.FW, cfg.MW, cfg.QL, cfg.KVL, cfg.FH, cfg.MH
    QW = MH * HEAD_PAD
    TM = _pick(S, (512, 256, 128))
    TQ = TK = _pick(S, (256, 128))
    n_tok = S // TM
    ZO = 3 * FW
    NZR = NZ - ZO
    misc_blk = (cfg.o_ms - ZO) // LANES
    FWD_UNROLL = 2
    FOX_SCALE = HEAD_DIM ** -0.5
    MLA_SCALE = (HEAD_DIM + ROPE_DIM) ** -0.5

    def tok(width, col=0):
        return pl.BlockSpec((TM, width), lambda i: (i, col))

    def const(shape):
        return pl.BlockSpec(shape, lambda i: tuple(0 for _ in shape))

    def layer(l, shape):
        return pl.BlockSpec((None,) + shape, lambda i: (l,) + tuple(0 for _ in shape))


    def ln_inproj(x, g, scale, shift, w_in, l):
        def body(x_ref, g_ref, sc_ref, sh_ref, w_ref, h_ref, z_ref, qkv_ref):
            y, _, _ = _rms(x_ref[...], g_ref[...])
            hb = (y * (1.0 + sc_ref[...]) + sh_ref[...]).astype(BF16)
            h_ref[...] = hb
            z = _nn(hb, w_ref[...])
            z_ref[...] = z[:, ZO:]
            qkv_ref[:, :FW] = (z[:, :FW] * FOX_SCALE).astype(BF16)
            qkv_ref[:, FW:] = z[:, FW:ZO].astype(BF16)

        return pl.pallas_call(
            body, name="ln_inproj", grid=(n_tok,),
            in_specs=[tok(D), const((1, D)), const((1, D)), const((1, D)), layer(l, (D, NZ))],
            out_specs=[tok(D), tok(NZR), tok(ZO)],
            out_shape=[jax.ShapeDtypeStruct((S, D), BF16), jax.ShapeDtypeStruct((S, NZR), F32),
                       jax.ShapeDtypeStruct((S, ZO), BF16)],
            compiler_params=_params(("parallel",)))(x, g, scale, shift, w_in)

    def _log_f_terms(misc, bf):
        lane = _lane(misc.shape)
        a = misc + bf
        e = jnp.exp(-jnp.abs(a))
        logf = jnp.minimum(a, 0.0) - jnp.log(1.0 + e)
        sig_neg = jnp.where(a >= 0, e, 1.0) / (1.0 + e)
        valid = lane < FH
        return jnp.where(valid, logf, 0.0), jnp.where(valid, sig_neg, 0.0)

    def fox_prep(z, bf_pad):
        def body(z_ref, b_ref, o_ref, carry):
            @pl.when(pl.program_id(0) == 0)
            def _():
                carry[...] = jnp.zeros_like(carry)

            logf, _ = _log_f_terms(z_ref[...], b_ref[...])
            row = lax.broadcasted_iota(jnp.int32, (TM, TM), 0)
            col = lax.broadcasted_iota(jnp.int32, (TM, TM), 1)
            tri = (col <= row).astype(F32)
            cum = jnp.dot(tri, logf, precision=lax.Precision.HIGHEST, preferred_element_type=F32) + carry[...]
            o_ref[...] = cum
            carry[...] = cum[TM - 1:TM, :]

        return pl.pallas_call(
            body, name="fox_prep", grid=(n_tok,),
            in_specs=[tok(LANES, misc_blk), const((1, LANES))], out_specs=tok(LANES),
            out_shape=jax.ShapeDtypeStruct((S, LANES), F32),
            scratch_shapes=[pltpu.VMEM((1, LANES), F32)],
            compiler_params=_params(("arbitrary",)))(z, bf_pad)

    def mla_prep(z, gq, gkv, w_uq, w_uk, w_v, l, cos_t, sin_t):
        def body(ql_ref, kvl_ref, ms_ref, gq_ref, gkv_ref, wq_ref, wk_ref, wv_ref, c_ref, s_ref, q_ref, k_ref, v_ref):
            cos1, sin1 = c_ref[...], s_ref[...]
            qn, _, _ = _rms(ql_ref[...], gq_ref[...])
            q = _nn(qn.astype(BF16), wq_ref[...])
            q_ref[...] = (_rope(q, jnp.tile(cos1, (1, MH)), jnp.tile(sin1, (1, MH))) * MLA_SCALE).astype(BF16)
            kvn, _, _ = _rms(kvl_ref[...], gkv_ref[...])
            kvb = kvn.astype(BF16)
            lane = _lane((TM, LANES))
            kr = jnp.where((lane >= ROPE_LO) & (lane < ROPE_LO + ROPE_DIM), ms_ref[...], 0.0)
            kr = _rope(kr, cos1, sin1)
            k_ref[...] = (_nn(kvb, wk_ref[...]) + jnp.tile(kr, (1, MH))).astype(BF16)
            v_ref[...] = _nn(kvb, wv_ref[...]).astype(BF16)

        return pl.pallas_call(
            body, name="mla_prep", grid=(n_tok,),
            in_specs=[tok(QL, (cfg.o_ql - ZO) // QL), tok(KVL, (cfg.o_kv - ZO) // KVL), tok(LANES, misc_blk),
                      const((1, QL)), const((1, KVL)), layer(l, (QL, QW)), layer(l, (KVL, QW)), layer(l, (KVL, MW)),
                      tok(LANES), tok(LANES)],
            out_specs=[tok(QW), tok(QW), tok(MW)],
            out_shape=[jax.ShapeDtypeStruct((S, QW), BF16), jax.ShapeDtypeStruct((S, QW), BF16),
                       jax.ShapeDtypeStruct((S, MW), BF16)],
            compiler_params=_params(("parallel",)))(z, z, z, gq, gkv, w_uq, w_uk, w_v, cos_t, sin_t)

    def _allowed(q0, k0, chunked):
        qi = q0 + lax.broadcasted_iota(jnp.int32, (TQ, TK), 0)
        ki = k0 + lax.broadcasted_iota(jnp.int32, (TQ, TK), 1)
        if chunked:
            return (ki >> CHUNK_SHIFT) <= (qi >> CHUNK_SHIFT)
        return ki <= qi

    def _heads(val, packed):
        if packed:
            lane = _lane(val.shape)
            zero = jnp.zeros_like(val)
            return [jnp.where(lane < HEAD_DIM, val, zero), jnp.where(lane >= HEAD_DIM, val, zero)]
        return [val[:, :HEAD_PAD], val[:, HEAD_PAD:]]

    def _merge(a0, a1):
        return jnp.where(_lane(a0.shape) < HEAD_DIM, a0, a1)

    def attn_fwd(q, k, v, q_blk0, k_blk0, v_blk0, cumt, packed, chunked, name):
        has_bias = cumt is not None
        n_pairs = (FH if packed else MH) // 2
        wq = LANES if packed else 2 * HEAD_PAD
        assert TQ == TK

        def body(*refs):
            if has_bias:
                q_ref, k_ref, v_ref, ct_ref, o_ref, lse_ref = refs
            else:
                q_ref, k_ref, v_ref, o_ref, lse_ref = refs
            i = pl.program_id(1)
            q0 = i * TQ
            qh = _heads(q_ref[...], packed)
            q_both = jnp.concatenate(qh, axis=0) if packed else None

            def scores(kb):
                kblk = k_ref[pl.ds(pl.multiple_of(kb * TK, TK), TK), :]
                if packed:
                    return _nt(q_both, kblk)
                return jnp.concatenate([_nt(qh[0], kblk[:, :HEAD_PAD]), _nt(qh[1], kblk[:, HEAD_PAD:])], axis=0)

            def update(kb, s, carry, masked):
                m, l, acc = carry
                k0 = pl.multiple_of(kb * TK, TK)
                s0, s1 = s[:TQ], s[TQ:]
                if has_bias:
                    ck = ct_ref[0, kb]
                    s0, s1 = s0 - ck[0:1, :], s1 - ck[1:2, :]
                if masked:
                    allow = _allowed(q0, k0, chunked)
                    s0, s1 = jnp.where(allow, s0, NEG), jnp.where(allow, s1, NEG)
                s = jnp.concatenate([s0, s1], axis=0)
                m_new = jnp.maximum(m, jnp.max(s, axis=-1, keepdims=True))
                p = jnp.exp(s - m_new)
                alpha = jnp.exp(m - m_new)
                l = alpha * l + jnp.sum(p, axis=-1, keepdims=True)
                acc = alpha * acc + _nn(p.astype(BF16), v_ref[pl.ds(k0, TK), :])
                return m_new, l, acc

            def several(kb, carry, n, last_masked):
                ss = [scores(kb + u) for u in range(n)]
                for u in range(n):
                    carry = update(kb + u, ss[u], carry, last_masked and u == n - 1)
                return carry

            init = (jnp.full((2 * TQ, 1), NEG, F32), jnp.zeros((2 * TQ, 1), F32), jnp.zeros((2 * TQ, LANES), F32))
            carry = lax.fori_loop(0, i // FWD_UNROLL, lambda t, cr: several(FWD_UNROLL * t, cr, FWD_UNROLL, False), init)
            m, l, acc = lax.switch(i % FWD_UNROLL,
                                   [lambda cr, r=r: several(i - r, cr, r + 1, True) for r in range(FWD_UNROLL)], carry)
            o = acc / l
            lse = m + jnp.log(l)
            lane = _lane((TQ, LANES))
            o_ref[...] = _merge(o[:TQ], o[TQ:])
            lse_ref[...] = jnp.where(lane == 0, lse[:TQ], jnp.where(lane == 1, lse[TQ:], 0.0))

        in_specs = [pl.BlockSpec((TQ, wq), lambda p, i: (i, q_blk0 + p)),
                    pl.BlockSpec((S, wq), lambda p, i: (0, k_blk0 + p)),
                    pl.BlockSpec((S, LANES), lambda p, i: (0, v_blk0 + p))]
        args = [q, k, v]
        if has_bias:
            in_specs += [pl.BlockSpec((1, S // TK, 8, TK), lambda p, i: (p, 0, 0, 0))]
            args += [cumt]
        return pl.pallas_call(
            body, name=name, grid=(n_pairs, S // TQ), in_specs=in_specs,
            out_specs=[pl.BlockSpec((TQ, LANES), lambda p, i: (i, p)),
                       pl.BlockSpec((TQ, LANES), lambda p, i: (i, p))],
            out_shape=[jax.ShapeDtypeStruct((S, n_pairs * LANES), F32),
                       jax.ShapeDtypeStruct((S, n_pairs * LANES), F32)],
            compiler_params=_params(("parallel", "parallel")))(*args)

    def gate_outproj(of, om, z, w_out, l, x, gate):
        def body(of_ref, om_ref, fg_ref, mg_ref, w_ref, x_ref, gt_ref, xn_ref, u_ref, y_ref):
            fg, mg = fg_ref[...], mg_ref[...]
            u = jnp.concatenate([of_ref[...] * fg * _sigmoid(fg), om_ref[...] * mg * _sigmoid(mg)], axis=1).astype(BF16)
            y = _nn(u, w_ref[...])
            u_ref[...] = u
            y_ref[...] = y.astype(BF16)
            xn_ref[...] = x_ref[...] + gt_ref[...] * y

        return pl.pallas_call(
            body, name="gate_outproj", grid=(n_tok,),
            in_specs=[tok(FW), tok(MW), tok(FW, (cfg.o_fg - ZO) // FW), tok(MW, (cfg.o_mg - ZO) // MW),
                      layer(l, (FW + MW, D)), tok(D), const((1, D))],
            out_specs=[tok(D), tok(FW + MW), tok(D)],
            out_shape=[jax.ShapeDtypeStruct((S, D), F32), jax.ShapeDtypeStruct((S, FW + MW), BF16),
                       jax.ShapeDtypeStruct((S, D), BF16)],
            compiler_params=_params(("parallel",)))(of, om, z, z, w_out, x, gate)

    def final_loss(x, g, target):
        def body(x_ref, g_ref, t_ref, dx_ref, acc_ref, loss_ref):
            @pl.when(pl.program_id(0) == 0)
            def _():
                acc_ref[...] = jnp.zeros_like(acc_ref)
                loss_ref[...] = jnp.zeros_like(loss_ref)

            gv = g_ref[...]
            y, xh, rstd = _rms(x_ref[...], gv)
            e = y - t_ref[...]
            loss_ref[...] += 0.5 * jnp.sum(jnp.sum(e * e, axis=-1, keepdims=True) / D, axis=0, keepdims=True)
            dx, dg = _rms_bwd(e / D, gv, xh, rstd)
            dx_ref[...] = dx
            acc_ref[0:1, :] += dg

        return pl.pallas_call(
            body, name="final_loss", grid=(n_tok,),
            in_specs=[tok(D), const((1, D)), tok(D)],
            out_specs=[tok(D), const((8, D)), const((1, LANES))],
            out_shape=[jax.ShapeDtypeStruct((S, D), F32), jax.ShapeDtypeStruct((8, D), F32),
                       jax.ShapeDtypeStruct((1, LANES), F32)],
            compiler_params=_params(("arbitrary",)))(x, g, target)


    def bwd_out(dxn, gate, y, w_out, l, of, om, z, lse_f, lse_m):
        def body(dx_ref, gt_ref, y_ref, w_ref, of_ref, om_ref, fg_ref, mg_ref, lf_ref, lm_ref,
                 dof_ref, dom_ref, dfg_ref, dmg_ref, dy_ref, acc_ref, sf_ref, sm_ref):
            @pl.when(pl.program_id(0) == 0)
            def _():
                acc_ref[...] = jnp.zeros_like(acc_ref)

            dxv = dx_ref[...]
            acc_ref[0:1, :] += jnp.sum(dxv * y_ref[...].astype(F32), axis=0, keepdims=True)
            dy = (gt_ref[...] * dxv).astype(BF16)
            dy_ref[...] = dy
            du = _nt(dy, w_ref[...])
            lane = _lane((TM, LANES))
            for lo, width, o_ref, g_ref, do_ref, dg_ref, l_ref, st_ref in (
                    (0, FW, of_ref, fg_ref, dof_ref, dfg_ref, lf_ref, sf_ref),
                    (FW, MW, om_ref, mg_ref, dom_ref, dmg_ref, lm_ref, sm_ref)):
                gv, ov = g_ref[...], o_ref[...]
                sg = _sigmoid(gv)
                dup = du[:, lo:lo + width]
                dob = (dup * gv * sg).astype(BF16)
                do_ref[...] = dob
                dg_ref[...] = (dup * ov * sg * (1.0 + gv * (1.0 - sg))).astype(BF16)
                d = dob.astype(F32) * ov
                for pr in range(width // LANES):
                    cols = slice(pr * LANES, (pr + 1) * LANES)
                    dp = d[:, cols]
                    d0 = jnp.sum(jnp.where(lane < HEAD_DIM, dp, 0.0), axis=-1, keepdims=True)
                    d1 = jnp.sum(jnp.where(lane >= HEAD_DIM, dp, 0.0), axis=-1, keepdims=True)
                    st_ref[:, cols] = jnp.where(lane == 2, d0, jnp.where(lane == 3, d1, l_ref[:, cols]))

        return pl.pallas_call(
            body, name="bwd_out", grid=(n_tok,),
            in_specs=[tok(D), const((1, D)), tok(D), layer(l, (FW + MW, D)), tok(FW), tok(MW),
                      tok(FW, (cfg.o_fg - ZO) // FW), tok(MW, (cfg.o_mg - ZO) // MW), tok(FW), tok(MW)],
            out_specs=[tok(FW), tok(MW), tok(FW), tok(MW), tok(D), const((8, D)), tok(FW), tok(MW)],
            out_shape=[jax.ShapeDtypeStruct((S, FW), BF16), jax.ShapeDtypeStruct((S, MW), BF16),
                       jax.ShapeDtypeStruct((S, FW), BF16), jax.ShapeDtypeStruct((S, MW), BF16),
                       jax.ShapeDtypeStruct((S, D), BF16), jax.ShapeDtypeStruct((8, D), F32),
                       jax.ShapeDtypeStruct((S, FW), F32), jax.ShapeDtypeStruct((S, MW), F32)],
            compiler_params=_params(("arbitrary",)))(dxn, gate, y, w_out, of, om, z, z, lse_f, lse_m)

    def attn_bwd(q, k, v, do, stats, q_blk0, k_blk0, v_blk0, cumt, packed, chunked, name):
        has_bias = cumt is not None
        n_pairs = (FH if packed else MH) // 2
        wq = LANES if packed else 2 * HEAD_PAD
        n_q = S // TQ
        assert TQ == TK

        def body(*refs):
            if has_bias:
                q_ref, k_ref, v_ref, do_ref, st_ref, ct_ref, dq_ref, dk_ref, dv_ref, dc_ref, dr_ref = refs
            else:
                q_ref, k_ref, v_ref, do_ref, st_ref, dq_ref, dk_ref, dv_ref = refs
            jb = pl.program_id(1)
            k0 = jb * TK

            @pl.when(jb == 0)
            def _():
                dq_ref[...] = jnp.zeros_like(dq_ref)
                if has_bias:
                    dr_ref[...] = jnp.zeros_like(dr_ref)

            dk_ref[...] = jnp.zeros_like(dk_ref)
            dv_ref[...] = jnp.zeros_like(dv_ref)
            kh = _heads(k_ref[...], packed)
            k_both = jnp.concatenate(kh, axis=0) if packed else None
            v_both = jnp.concatenate(_heads(v_ref[...], True), axis=0)
            ck = jnp.concatenate([ct_ref[0, 0][0:1, :], ct_ref[0, 0][1:2, :]], axis=1) if has_bias else None

            def products(ib):
                rows = pl.ds(pl.multiple_of(ib * TQ, TQ), TQ)
                q2, do2 = q_ref[rows, :], do_ref[rows, :]
                if packed:
                    s = _nt(q2, k_both)
                else:
                    qh = _heads(q2, False)
                    s = jnp.concatenate([_nt(qh[0], kh[0]), _nt(qh[1], kh[1])], axis=1)
                return s, _nt(do2, v_both)

            def update(ib, s, dp, carry, masked):
                q0 = pl.multiple_of(ib * TQ, TQ)
                rows = pl.ds(q0, TQ)
                q2, do2, st = q_ref[rows, :], do_ref[rows, :], st_ref[rows, :]
                if not packed:
                    qh = _heads(q2, False)
                if has_bias:
                    s = s - ck
                if masked:
                    allow = _allowed(q0, k0, chunked)
                    s = jnp.where(jnp.concatenate([allow, allow], axis=1), s, NEG)
                p = jnp.concatenate([jnp.exp(s[:, :TK] - st[:, 0:1]), jnp.exp(s[:, TK:] - st[:, 1:2])], axis=1)
                dv2 = _tn(p.astype(BF16), do2)
                ds = jnp.concatenate([p[:, :TK] * (dp[:, :TK] - st[:, 2:3]), p[:, TK:] * (dp[:, TK:] - st[:, 3:4])], axis=1)
                dsb = ds.astype(BF16)
                dv_ref[...] += _merge(dv2[:TK], dv2[TK:])
                if packed:
                    dk2 = _tn(dsb, q2)
                    dk_ref[...] += _merge(dk2[:TK], dk2[TK:])
                    dq_ref[rows, :] += _nn(dsb, k_both)
                else:
                    dk_ref[...] += jnp.concatenate([_tn(dsb[:, :TK], qh[0]), _tn(dsb[:, TK:], qh[1])], axis=1)
                    dq_ref[rows, :] += jnp.concatenate([_nn(dsb[:, :TK], kh[0]), _nn(dsb[:, TK:], kh[1])], axis=1)
                if has_bias:
                    lane = _lane((TQ, LANES))
                    r0 = jnp.sum(ds[:, :TK], axis=-1, keepdims=True)
                    r1 = jnp.sum(ds[:, TK:], axis=-1, keepdims=True)
                    dr_ref[0, rows, :] += jnp.where(lane == 0, r0, jnp.where(lane == 1, r1, 0.0))
                    return carry - jnp.sum(ds, axis=0, keepdims=True)
                return carry

            def step(ib, carry, masked):
                s, dp = products(ib)
                return update(ib, s, dp, carry, masked)

            def two(ib, carry):
                s_a, dp_a = products(ib)
                s_b, dp_b = products(ib + 1)
                return update(ib + 1, s_b, dp_b, update(ib, s_a, dp_a, carry, False), False)

            dc = step(jb, jnp.zeros((1, 2 * TK), F32), True)
            n_rest = n_q - 1 - jb
            dc = lax.fori_loop(0, n_rest // 2, lambda t, cr: two(jb + 1 + 2 * t, cr), dc)
            dc = lax.cond(n_rest % 2 == 1, lambda cr: step(n_q - 1, cr, False), lambda cr: cr, dc)
            if has_bias:
                sub = lax.broadcasted_iota(jnp.int32, (8, TK), 0)
                dc_ref[0, 0] = jnp.where(sub == 0, dc[:, :TK], jnp.where(sub == 1, dc[:, TK:], 0.0))

        in_specs = [pl.BlockSpec((S, wq), lambda p, j: (0, q_blk0 + p)),
                    pl.BlockSpec((TK, wq), lambda p, j: (j, k_blk0 + p)),
                    pl.BlockSpec((TK, LANES), lambda p, j: (j, v_blk0 + p)),
                    pl.BlockSpec((S, LANES), lambda p, j: (0, p)),
                    pl.BlockSpec((S, LANES), lambda p, j: (0, p))]
        args = [q, k, v, do, stats]
        out_specs = [pl.BlockSpec((S, wq), lambda p, j: (0, p)),
                     pl.BlockSpec((TK, wq), lambda p, j: (j, p)),
                     pl.BlockSpec((TK, LANES), lambda p, j: (j, p))]
        out_shape = [jax.ShapeDtypeStruct((S, n_pairs * wq), F32), jax.ShapeDtypeStruct((S, n_pairs * wq), F32),
                     jax.ShapeDtypeStruct((S, n_pairs * LANES), F32)]
        if has_bias:
            in_specs += [pl.BlockSpec((1, 1, 8, TK), lambda p, j: (p, j, 0, 0))]
            args += [cumt]
            out_specs += [pl.BlockSpec((1, 1, 8, TK), lambda p, j: (p, j, 0, 0)),
                          pl.BlockSpec((1, S, LANES), lambda p, j: (p, 0, 0))]
            out_shape += [jax.ShapeDtypeStruct((n_pairs, S // TK, 8, TK), F32),
                          jax.ShapeDtypeStruct((n_pairs, S, LANES), F32)]
        return pl.pallas_call(
            body, name=name, grid=(n_pairs, S // TK), in_specs=in_specs, out_specs=out_specs, out_shape=out_shape,
            compiler_params=_params(("parallel", "arbitrary")))(*args)

    def fox_post(dcum, z, bf_pad):
        def rev(i):
            return n_tok - 1 - i

        def body(dc_ref, z_ref, b_ref, dff_ref, acc_ref, carry):
            @pl.when(pl.program_id(0) == 0)
            def _():
                carry[...] = jnp.zeros_like(carry)
                acc_ref[...] = jnp.zeros_like(acc_ref)

            _, sig_neg = _log_f_terms(z_ref[...], b_ref[...])
            row = lax.broadcasted_iota(jnp.int32, (TM, TM), 0)
            col = lax.broadcasted_iota(jnp.int32, (TM, TM), 1)
            tri = (col >= row).astype(F32)
            dlog = jnp.dot(tri, dc_ref[...], precision=lax.Precision.HIGHEST, preferred_element_type=F32) + carry[...]
            carry[...] = dlog[0:1, :]
            dff = dlog * sig_neg
            dff_ref[...] = dff
            acc_ref[0:1, :] += jnp.sum(dff, axis=0, keepdims=True)

        return pl.pallas_call(
            body, name="fox_post", grid=(n_tok,),
            in_specs=[pl.BlockSpec((TM, LANES), lambda i: (rev(i), 0)),
                      pl.BlockSpec((TM, LANES), lambda i: (rev(i), misc_blk)), const((1, LANES))],
            out_specs=[pl.BlockSpec((TM, LANES), lambda i: (rev(i), 0)), const((8, LANES))],
            out_shape=[jax.ShapeDtypeStruct((S, LANES), F32), jax.ShapeDtypeStruct((8, LANES), F32)],
            scratch_shapes=[pltpu.VMEM((1, LANES), F32)],
            compiler_params=_params(("arbitrary",)))(dcum, z, bf_pad)

    def mla_post(dq, dk, dv, dff, z, gq, gkv, w_uq, w_uk, w_v, l, cos_t, sin_t):
        def body(dq_ref, dk_ref, dv_ref, dff_ref, ql_ref, kvl_ref, gq_ref, gkv_ref, wq_ref, wk_ref, wv_ref,
                 c_ref, s_ref, zq_ref, zkv_ref, zms_ref, dwq_ref, dwk_ref, dwv_ref, dgq_ref, dgkv_ref):
            @pl.when(pl.program_id(0) == 0)
            def _():
                for r in (dwq_ref, dwk_ref, dwv_ref, dgq_ref, dgkv_ref):
                    r[...] = jnp.zeros_like(r)

            cos1, sin1 = c_ref[...], s_ref[...]
            gqv, gkvv = gq_ref[...], gkv_ref[...]
            qn, qxh, qrstd = _rms(ql_ref[...], gqv)
            dq_pre = _rope_t(dq_ref[...] * MLA_SCALE, jnp.tile(cos1, (1, MH)), jnp.tile(sin1, (1, MH))).astype(BF16)
            dwq_ref[...] += _tn(qn.astype(BF16), dq_pre)
            dql, dgq = _rms_bwd(_nt(dq_pre, wq_ref[...]), gqv, qxh, qrstd)
            zq_ref[...] = dql.astype(BF16)
            dgq_ref[0:1, :] += dgq

            dkv = dk_ref[...]
            lane = _lane(dkv.shape) & (HEAD_PAD - 1)
            dkn = jnp.where(lane < HEAD_DIM, dkv, 0.0).astype(BF16)
            dkr = dkv[:, 0:HEAD_PAD]
            for hd in range(1, MH):
                dkr = dkr + dkv[:, hd * HEAD_PAD:(hd + 1) * HEAD_PAD]
            lane1 = _lane(dkr.shape)
            dkr = jnp.where((lane1 >= ROPE_LO) & (lane1 < ROPE_LO + ROPE_DIM), dkr, 0.0)
            dkr = _rope_t(dkr, cos1, sin1)
            zms_ref[...] = (dkr + dff_ref[...]).astype(BF16)

            kvn, kxh, krstd = _rms(kvl_ref[...], gkvv)
            kvb = kvn.astype(BF16)
            dvb = dv_ref[...].astype(BF16)
            dwk_ref[...] += _tn(kvb, dkn)
            dwv_ref[...] += _tn(kvb, dvb)
            dkvl, dgkv = _rms_bwd(_nt(dkn, wk_ref[...]) + _nt(dvb, wv_ref[...]), gkvv, kxh, krstd)
            zkv_ref[...] = dkvl.astype(BF16)
            dgkv_ref[0:1, :] += dgkv

        return pl.pallas_call(
            body, name="mla_post", grid=(n_tok,),
            in_specs=[tok(QW), tok(QW), tok(MW), tok(LANES), tok(QL, (cfg.o_ql - ZO) // QL), tok(KVL, (cfg.o_kv - ZO) // KVL),
                      const((1, QL)), const((1, KVL)), layer(l, (QL, QW)), layer(l, (KVL, QW)), layer(l, (KVL, MW)),
                      tok(LANES), tok(LANES)],
            out_specs=[tok(QL), tok(KVL), tok(LANES), const((QL, QW)), const((KVL, QW)), const((KVL, MW)),
                       const((8, QL)), const((8, KVL))],
            out_shape=[jax.ShapeDtypeStruct((S, QL), BF16), jax.ShapeDtypeStruct((S, KVL), BF16),
                       jax.ShapeDtypeStruct((S, LANES), BF16), jax.ShapeDtypeStruct((QL, QW), F32),
                       jax.ShapeDtypeStruct((KVL, QW), F32), jax.ShapeDtypeStruct((KVL, MW), F32),
                       jax.ShapeDtypeStruct((8, QL), F32), jax.ShapeDtypeStruct((8, KVL), F32)],
            compiler_params=_params(("arbitrary",)))(dq, dk, dv, dff, z, z, gq, gkv, w_uq, w_uk, w_v, cos_t, sin_t)

    def bwd_in(dz, w_in, l, x, dxn, g, scale):
        def body(dz_ref, w_ref, x_ref, dx_ref, g_ref, sc_ref, o_ref, acc_ref):
            @pl.when(pl.program_id(0) == 0)
            def _():
                acc_ref[...] = jnp.zeros_like(acc_ref)

            dh = _nt(dz_ref[...], w_ref[...])
            gv, mod = g_ref[...], 1.0 + sc_ref[...]
            _, xh, rstd = _rms(x_ref[...], gv)
            t = dh * xh
            acc_ref[0:1, :] += jnp.sum(dh, axis=0, keepdims=True)
            acc_ref[1:2, :] += jnp.sum(t * gv, axis=0, keepdims=True)
            acc_ref[2:3, :] += jnp.sum(t * mod, axis=0, keepdims=True)
            dx, _ = _rms_bwd(dh, gv * mod, xh, rstd)
            o_ref[...] = dx_ref[...] + dx

        return pl.pallas_call(
            body, name="bwd_in", grid=(n_tok,),
            in_specs=[tok(NZ), layer(l, (D, NZ)), tok(D), tok(D), const((1, D)), const((1, D))],
            out_specs=[tok(D), const((8, D))],
            out_shape=[jax.ShapeDtypeStruct((S, D), F32), jax.ShapeDtypeStruct((8, D), F32)],
            compiler_params=_params(("arbitrary",)))(dz, w_in, x, dxn, g, scale)


    def pair_rows(cum):
        n_pairs = FH // 2
        ct = jnp.pad(cum[:, :FH].T.reshape(n_pairs, 2, S), ((0, 0), (0, 6), (0, 0)))
        return ct.reshape(n_pairs, 8, S // TK, TK).transpose(0, 2, 1, 3)

    def bias_grad(dc, dr):
        n_pairs = FH // 2
        d = dc.transpose(0, 2, 1, 3).reshape(n_pairs, 8, S)[:, :2, :].reshape(FH, S).T
        d = d + dr[:, :, :2].transpose(1, 0, 2).reshape(S, FH)
        return jnp.pad(d, ((0, 0), (0, LANES - FH)))

    def layer_forward(x, wl, mod):
        shift, scale, gate = mod
        h, z, qkv = ln_inproj(x, wl.norm_g, scale, shift, wl.w_in, wl.l)
        ct = pair_rows(fox_prep(z, wl.bf_pad))
        of, lse_f = attn_fwd(qkv, qkv, qkv, cfg.o_fq // LANES, cfg.o_fk // LANES, cfg.o_fv // LANES, ct,
                             True, False, "fox_fwd")
        qp, kp, vp = mla_prep(z, wl.gq, wl.gkv, wl.w_uq, wl.w_uk, wl.w_v, wl.l, cfg.cos_t, cfg.sin_t)
        om, lse_m = attn_fwd(qp, kp, vp, 0, 0, 0, None, False, True, "mla_fwd")
        xn, u, y = gate_outproj(of, om, z, wl.w_out, wl.l, x, gate)
        saved = types.SimpleNamespace(x=x, h=h, z=z, qkv=qkv, ct=ct, of=of, lse_f=lse_f, qp=qp, kp=kp, vp=vp,
                                      om=om, lse_m=lse_m, u=u, y=y)
        return xn, saved

    def layer_backward(dxn, sv, wl, mod):
        shift, scale, gate = mod
        do_f, do_m, dfg, dmg, dy, acc_o, st_f, st_m = bwd_out(dxn, gate, sv.y, wl.w_out, wl.l, sv.of, sv.om, sv.z,
                                                              sv.lse_f, sv.lse_m)
        dw_out = matmul_tn(sv.u, dy, "dw_out")
        dfq, dfk, dfv, dck, dcr = attn_bwd(sv.qkv, sv.qkv, sv.qkv, do_f, st_f, cfg.o_fq // LANES, cfg.o_fk // LANES,
                                           cfg.o_fv // LANES, sv.ct, True, False, "fox_bwd")
        dff, acc_f = fox_post(bias_grad(dck, dcr), sv.z, wl.bf_pad)
        dqp, dkp, dvp = attn_bwd(sv.qp, sv.kp, sv.vp, do_m, st_m, 0, 0, 0, None, False, True, "mla_bwd")
        zq, zkv, zms, dw_uq, dw_uk, dw_v, dgq, dgkv = mla_post(
            dqp, dkp, dvp, dff, sv.z, wl.gq, wl.gkv, wl.w_uq, wl.w_uk, wl.w_v, wl.l, cfg.cos_t, cfg.sin_t)
        dz = jnp.concatenate([(dfq * FOX_SCALE).astype(BF16), dfk.astype(BF16), dfv.astype(BF16), dfg, dmg, zq, zkv, zms], axis=1)
        dx, acc_i = bwd_in(dz, wl.w_in, wl.l, sv.x, dxn, wl.norm_g, scale)
        dw_in = matmul_tn(sv.h, dz, "dw_in")
        grads = types.SimpleNamespace(
            w_in=dw_in, w_out=dw_out, w_uq=dw_uq, w_uk=dw_uk, w_v=dw_v, gq=dgq[0], gkv=dgkv[0],
            b_f=acc_f[0, :FH], norm_g=acc_i[2], dmod=jnp.concatenate([acc_i[0], acc_i[1], acc_o[0]]))
        return dx, grads

    return types.SimpleNamespace(layer_forward=layer_forward, layer_backward=layer_backward, final_loss=final_loss)


def _pack_rows(parts, dtype, row_multiple):
    flat = jnp.concatenate([p.reshape(-1).astype(dtype) for p in parts])
    per = SLAB_COLS * row_multiple
    total = -(-flat.shape[0] // per) * per
    return jnp.pad(flat, (0, total - flat.shape[0])).reshape(total // SLAB_COLS, SLAB_COLS)


def _unpack(flat, shapes):
    out, off = [], 0
    for shp in shapes:
        n = 1
        for d in shp:
            n *= d
        out.append(flat[off:off + n].reshape(shp))
        off += n
    return out


def kernel(x, c, positions, norm_g, w_ada, b_ada, w_in, b_f, q_norm_g, w_uq, kv_norm_g, w_ukv, w_out, final_g, loss_target, m_norm_g, m_w_ada, m_b_ada, m_w_in, m_b_f, m_q_norm_g, m_w_uq, m_kv_norm_g, m_w_ukv, m_w_out, m_final_g, v_norm_g, v_w_ada, v_b_ada, v_w_in, v_b_f, v_q_norm_g, v_w_uq, v_kv_norm_g, v_w_ukv, v_w_out, v_final_g):
    S, D = x.shape[1], x.shape[2]
    L = norm_g.shape[0]
    FH = b_f.shape[1]
    QL, KVL = q_norm_g.shape[1], kv_norm_g.shape[1]
    MH = w_ukv.shape[2] * N_CHIPS // (2 * HEAD_DIM)
    FW, MW = FH * HEAD_DIM, MH * HEAD_DIM
    NA = w_ada.shape[2]
    n_in = w_in.shape[2] * N_CHIPS
    cfg = types.SimpleNamespace(S=S, D=D, FW=FW, MW=MW, QL=QL, KVL=KVL, FH=FH, MH=MH)
    cfg.o_fq, cfg.o_fk, cfg.o_fv, cfg.o_fg, cfg.o_mg = 0, FW, 2 * FW, 3 * FW, 4 * FW
    cfg.o_ql = 4 * FW + MW
    cfg.o_kv = cfg.o_ql + QL
    cfg.o_ms = cfg.o_kv + KVL
    cfg.NZ = cfg.o_ms + LANES
    assert FW == MW and FH % 2 == 0 and MH % 2 == 0 and cfg.o_ql % QL == 0 and cfg.o_kv % KVL == 0 and KVL == LANES
    assert n_in == 4 * FW + FH + QL + KVL + ROPE_DIM + MW

    mx, my, mc = _my_pos()
    my_chip = 2 * mx + my
    my_dev = 2 * my_chip + mc

    inv_freq = 1.0 / (ROPE_THETA ** (jnp.arange(0, ROPE_DIM, 2, dtype=F32) / ROPE_DIM))
    ang = positions[0].astype(F32)[:, None] * inv_freq
    cos, sin = jnp.cos(ang), jnp.sin(ang)
    cfg.cos_t = jnp.concatenate([jnp.ones((S, ROPE_LO), F32), cos, cos, jnp.ones((S, HEAD_PAD - ROPE_LO - ROPE_DIM), F32)], axis=1)
    cfg.sin_t = jnp.concatenate([jnp.zeros((S, ROPE_LO), F32), -sin, sin, jnp.zeros((S, HEAD_PAD - ROPE_LO - ROPE_DIM), F32)], axis=1)

    assert L % 2 == 0
    def lane_pad(a):
        return jnp.pad(a, ((0, 0),) * (a.ndim - 1) + ((0, -a.shape[-1] % LANES),))

    shards = (w_in, w_uq, w_ukv, w_out)
    gathered = weights_gather([lane_pad(w.astype(BF16)) for w in shards], "weights_gather")
    def all_chips(g, w, axis):
        return jnp.concatenate([jnp.where(my_chip == t, w.astype(BF16), g[t][..., :w.shape[-1]]) for t in range(N_CHIPS)],
                               axis=axis)

    w_in_f = all_chips(gathered[0], w_in, 2)
    w_uq_f = all_chips(gathered[1], w_uq, 2)
    w_ukv_f = all_chips(gathered[2], w_ukv, 2)
    w_out_f = all_chips(gathered[3], w_out, 1)

    sizes = (FW, FW, FW, FH, FW, QL, KVL, ROPE_DIM, MW)
    offs = [0]
    for sz in sizes:
        offs.append(offs[-1] + sz)
    seg = [w_in_f[:, :, offs[i]:offs[i + 1]] for i in range(len(sizes))]
    fq_w, fk_w, fv_w, ff_w, fg_w, ql_w, kvl_w, kr_w, mg_w = seg
    zeros = lambda n: jnp.zeros((L, D, n), BF16)
    w_in_p = jnp.concatenate([fq_w, fk_w, fv_w, fg_w, mg_w, ql_w, kvl_w, ff_w, zeros(ROPE_LO - FH), kr_w,
                              zeros(HEAD_PAD - ROPE_LO - ROPE_DIM)], axis=2)
    w_uq_p = jnp.pad(w_uq_f.reshape(L, QL, MH, HEAD_DIM + ROPE_DIM), ((0, 0), (0, 0), (0, 0), (0, HEAD_PAD - HEAD_DIM - ROPE_DIM)))
    w_uq_p = w_uq_p.reshape(L, QL, MH * HEAD_PAD)
    w_ukv4 = w_ukv_f.reshape(L, KVL, MH, 2 * HEAD_DIM)
    w_uk_p = jnp.pad(w_ukv4[..., :HEAD_DIM], ((0, 0), (0, 0), (0, 0), (0, HEAD_PAD - HEAD_DIM))).reshape(L, KVL, MH * HEAD_PAD)
    w_v_p = w_ukv4[..., HEAD_DIM:].reshape(L, KVL, MW)

    c_all = allgather8(c.reshape(8, D // 8), "gather_c").reshape(N_DEV, D)
    c_pad = jnp.pad(c_all, ((0, 16 - N_DEV), (0, 0)))
    mod_part = ada_forward(c_pad, w_ada)[:, :N_DEV, :]
    mod_all = allgather8(mod_part.reshape(-1, LANES), "gather_mod").reshape(N_CHIPS, 2, L, N_DEV, NA)[:, 0]
    mod_full = mod_all.transpose(1, 2, 0, 3).reshape(L, N_DEV, N_CHIPS * NA) + b_ada[:, None, :]
    mod_mine = lax.dynamic_index_in_dim(mod_full, my_dev, axis=1, keepdims=True)

    step = make_step(cfg)
    bf_pad = jnp.pad(b_f, ((0, 0), (0, LANES - FH)))
    layers, mods = [], []
    for l in range(L):
        layers.append(types.SimpleNamespace(
            l=l, norm_g=norm_g[l][None], w_in=w_in_p, bf_pad=bf_pad[l][None], gq=q_norm_g[l][None], gkv=kv_norm_g[l][None],
            w_uq=w_uq_p, w_uk=w_uk_p, w_v=w_v_p, w_out=w_out_f))
        mods.append((mod_mine[l, :, :D], mod_mine[l, :, D:2 * D], mod_mine[l, :, 2 * D:]))

    xl = x[0]
    saved = []
    for l in range(L):
        xl, sv = step.layer_forward(xl, layers[l], mods[l])
        saved.append(sv)
    dx, acc_fin, loss_part = step.final_loss(xl, final_g[None], loss_target[0])
    loss = lax.psum(loss_part[0, 0], ("x", "y", "c"))
    gl = [None] * L
    for l in reversed(range(L)):
        dx, gl[l] = step.layer_backward(dx, saved[l], layers[l], mods[l])
    grad_x = dx[None]

    stack = lambda name: jnp.stack([getattr(g, name) for g in gl])
    small_parts = [stack("norm_g"), stack("dmod"), stack("b_f"), stack("gq"), stack("gkv"), acc_fin[0]]
    small_shapes = [p.shape for p in small_parts]
    small = _pack_rows(small_parts, F32, 8).reshape(-1, LANES)
    small_all = allgather8(small, "gather_small").reshape(N_DEV, -1, LANES)
    small_sum = sum_leading(small_all, "sum_small")
    g_norm_g, g_b_ada, g_b_f, g_q_norm_g, g_kv_norm_g, g_final_g = _unpack(small_sum.reshape(-1), small_shapes)

    n_ng = L * D
    dmod_all = small_all.reshape(N_DEV, -1)[:, n_ng:n_ng + L * 3 * D].reshape(N_DEV, L, 3 * D)
    dmod_cols = lax.dynamic_slice_in_dim(dmod_all, my_chip * NA, NA, axis=2).transpose(1, 0, 2)
    g_w_ada = ada_backward(c_pad, jnp.pad(dmod_cols, ((0, 0), (0, 16 - N_DEV), (0, 0))))

    ms = cfg.o_ms
    runs = [(0, 3 * FW, 0), (3 * FW, FH, ms), (3 * FW + FH, FW, cfg.o_fg), (4 * FW + FH, QL + KVL, cfg.o_ql),
            (4 * FW + FH + QL + KVL, ROPE_DIM, ms + ROPE_LO), (4 * FW + FH + QL + KVL + ROPE_DIM, MW, cfg.o_mg)]

    def shard_columns(dw, t):
        a, b = t * w_in.shape[2], (t + 1) * w_in.shape[2]
        return jnp.concatenate([dw[:, p0 + max(a, r0) - r0:p0 + min(b, r0 + sz) - r0]
                                for r0, sz, p0 in runs if max(a, r0) < min(b, r0 + sz)], axis=1)

    g_in4 = jnp.stack([jnp.stack([shard_columns(g.w_in, t) for g in gl]) for t in range(N_CHIPS)])
    dw_uq_f = stack("w_uq").reshape(L, QL, MH, HEAD_PAD)[..., :HEAD_DIM + ROPE_DIM].reshape(L, QL, -1)
    dw_ukv_f = jnp.concatenate([stack("w_uk").reshape(L, KVL, MH, HEAD_PAD)[..., :HEAD_DIM],
                                stack("w_v").reshape(L, KVL, MH, HEAD_DIM)], axis=3).reshape(L, KVL, -1)
    dw_out_f = stack("w_out")
    gs = [g_in4, dw_uq_f.reshape(L, QL, N_CHIPS, -1).transpose(2, 0, 1, 3),
          dw_ukv_f.reshape(L, KVL, N_CHIPS, -1).transpose(2, 0, 1, 3), dw_out_f.reshape(L, N_CHIPS, -1, D).transpose(1, 0, 2, 3)]
    gs = [lane_pad(g.astype(BF16)) for g in gs]
    hl = L // 2
    theirs = halves_to_sibling(gs, "grads_sibling")
    chip_part = []
    big_names = ["w_in", "w_uq", "w_ukv", "w_out"]
    for g, o, nm in zip(gs, theirs, big_names):
        keep = lax.dynamic_slice_in_dim(g, mc * hl, hl, axis=1)
        merged = (N_CHIPS * hl,) + g.shape[2:]
        chip_part.append(add_cast(keep.reshape(merged), o.reshape(merged), BF16, "grads_chip_sum_" + nm).reshape(o.shape))
    parts = chip_exchange(chip_part, "grads_chips")
    where = jnp.stack([my_chip, mc]).astype(jnp.int32)
    red = [sum_chips(p, o, where, "grads_sum_" + nm) for p, o, nm in zip(parts, chip_part, big_names)]
    g_w_in, g_w_uq, g_w_ukv, g_w_out = [g[..., :w.shape[-1]] for g, w in zip(halves_gather(red, "grads_back"), shards)]


    names = ["norm_g", "w_ada", "b_ada", "w_in", "b_f", "q_norm_g", "w_uq", "kv_norm_g", "w_ukv", "w_out", "final_g"]
    ws = dict(norm_g=norm_g, w_ada=w_ada, b_ada=b_ada, w_in=w_in, b_f=b_f, q_norm_g=q_norm_g, w_uq=w_uq,
              kv_norm_g=kv_norm_g, w_ukv=w_ukv, w_out=w_out, final_g=final_g)
    msd = dict(norm_g=m_norm_g, w_ada=m_w_ada, b_ada=m_b_ada, w_in=m_w_in, b_f=m_b_f, q_norm_g=m_q_norm_g, w_uq=m_w_uq,
               kv_norm_g=m_kv_norm_g, w_ukv=m_w_ukv, w_out=m_w_out, final_g=m_final_g)
    vsd = dict(norm_g=v_norm_g, w_ada=v_w_ada, b_ada=v_b_ada, w_in=v_w_in, b_f=v_b_f, q_norm_g=v_q_norm_g, w_uq=v_w_uq,
               kv_norm_g=v_kv_norm_g, w_ukv=v_w_ukv, w_out=v_w_out, final_g=v_final_g)
    gsd = dict(norm_g=g_norm_g, w_ada=g_w_ada, b_ada=g_b_ada, w_in=g_w_in, b_f=g_b_f, q_norm_g=g_q_norm_g, w_uq=g_w_uq,
               kv_norm_g=g_kv_norm_g, w_ukv=g_w_ukv, w_out=g_w_out, final_g=g_final_g)
    small_names = ["norm_g", "b_ada", "b_f", "q_norm_g", "kv_norm_g", "final_g"]
    sm_shapes = [ws[n].shape for n in small_names]
    pk = lambda d: _pack_rows([d[n] for n in small_names], F32, 8).reshape(1, -1, LANES)
    sm_out = adamw(pk(ws), pk(gsd), pk(msd), pk(vsd), "adamw_small")
    sm_d, sm_m, sm_v = [dict(zip(small_names, _unpack(o.reshape(-1), sm_shapes))) for o in sm_out]
    delta, new_m, new_v = dict(sm_d), dict(sm_m), dict(sm_v)
    for n in ["w_ada", "w_in", "w_uq", "w_ukv", "w_out"]:
        delta[n], new_m[n], new_v[n] = adamw(ws[n], gsd[n], msd[n], vsd[n], "adamw_" + n)

    return (loss, grad_x, *[gsd[n] for n in names], *[delta[n] for n in names],
            *[new_m[n] for n in names], *[new_v[n] for n in names])
```

```python
import types

import jax
import jax.numpy as jnp
from jax import lax
from jax.experimental import pallas as pl
from jax.experimental.pallas import tpu as pltpu

F32 = jnp.float32
BF16 = jnp.bfloat16
MESH = pl.DeviceIdType.MESH

N_CHIPS = 4
N_DEV = 8
HEAD_DIM = 64
ROPE_DIM = 32
ROPE_THETA = 10000.0
HEAD_PAD = 128
ROPE_LO = 64
ROPE_HALF = 16
CHUNK_SHIFT = 6
LANES = 128
EPS = 1e-6
NEG = -1e30
ADAM_LR = 0.001
ADAM_B1 = 0.9
ADAM_B2 = 0.999
ADAM_EPS = 1e-08
ADAM_WD = 0.01
ADAM_STEP = 10
VMEM_LIMIT = 48 * 1024 * 1024
SLAB_COLS = 1024


def _params(sem=None, vmem=VMEM_LIMIT):
    return pltpu.CompilerParams(dimension_semantics=sem, vmem_limit_bytes=vmem)


def _nn(a, b):
    return jnp.dot(a, b, preferred_element_type=F32)


def _nt(a, b):
    return lax.dot_general(a, b, (((1,), (1,)), ((), ())), preferred_element_type=F32)


def _tn(a, b):
    return lax.dot_general(a, b, (((0,), (0,)), ((), ())), preferred_element_type=F32)


def _sigmoid(x):
    return 1.0 / (1.0 + jnp.exp(-x))


def _lane(shape):
    return lax.broadcasted_iota(jnp.int32, shape, len(shape) - 1)


def _pick(n, cands):
    for c in cands:
        if n % c == 0:
            return c
    return n


def _my_pos():
    return lax.axis_index("x"), lax.axis_index("y"), lax.axis_index("c")


def allgather8(xs, name):
    m_per, n = xs.shape

    def body(x_ref, out_ref, send_sems, recv_sems, local_sem):
        x, y, c = _my_pos()
        me, sibling = (x, y, c), (x, y, 1 - c)
        chips = [(1 - x, y), (x, 1 - y), (1 - x, 1 - y)]

        def rows(px, py, pc):
            return out_ref.at[pl.ds((4 * px + 2 * py + pc) * m_per, m_per), :]

        def copy(k, block, to, src=None):
            return pltpu.make_async_remote_copy(
                src_ref=rows(*block) if src is None else src, dst_ref=rows(*block),
                send_sem=send_sems.at[k], recv_sem=recv_sems.at[k], device_id=to, device_id_type=MESH)

        mine = pltpu.make_async_copy(x_ref, rows(*me), local_sem)
        mine.start()
        first = [copy(0, me, sibling, src=x_ref)]
        first += [copy(1 + j, me, (*chip, c), src=x_ref) for j, chip in enumerate(chips)]
        for cp in first:
            cp.start()
        passed = [copy(4 + j, (*chip, c), sibling) for j, chip in enumerate(chips)]
        for j, chip in enumerate(chips):
            copy(1 + j, (*chip, c), me).wait_recv()
            passed[j].start()
        copy(0, sibling, me).wait_recv()
        for j, chip in enumerate(chips):
            copy(4 + j, (*chip, 1 - c), me).wait_recv()
        for cp in first + passed:
            cp.wait_send()
        mine.wait()

    return pl.pallas_call(
        body, name=name,
        out_shape=jax.ShapeDtypeStruct((N_DEV * m_per, n), xs.dtype),
        in_specs=[pl.BlockSpec(memory_space=pltpu.VMEM)],
        out_specs=pl.BlockSpec(memory_space=pltpu.VMEM),
        scratch_shapes=[pltpu.SemaphoreType.DMA((7,)), pltpu.SemaphoreType.DMA((7,)), pltpu.SemaphoreType.DMA],
    )(xs)


def _remote(src, dst, send_sems, recv_sems, k, to):
    return pltpu.make_async_remote_copy(src_ref=src, dst_ref=dst, send_sem=send_sems.at[k], recv_sem=recv_sems.at[k],
                                        device_id=to, device_id_type=MESH)


def _hbm_call(body, name, ins, out_shapes, n_sems, aliases=None):
    hbm = pl.BlockSpec(memory_space=pl.ANY)
    scratch = [pltpu.SemaphoreType.DMA((n_sems,)), pltpu.SemaphoreType.DMA((n_sems,))]
    return pl.pallas_call(body, name=name, out_shape=out_shapes, in_specs=[hbm] * len(ins),
                          out_specs=[hbm] * len(out_shapes), scratch_shapes=scratch,
                          input_output_aliases=aliases or {})(*ins)


def _layers_half(ref, h, axis=0):
    size = ref.shape[axis] // 2
    idx = (slice(None),) * axis + (pl.ds(h * size, size),)
    return ref.at[idx]


class GatherOverIci:
    def __init__(self, ws):
        self.ins = list(ws)
        self.outs = [jax.ShapeDtypeStruct((N_CHIPS,) + w.shape, w.dtype) for w in ws]
        self.aliases = {}
        self.n_sems = 3 * len(ws)

    def _copies(self, x_refs, o_refs, send_sems, recv_sems):
        x, y, c = _my_pos()
        s = 2 * x + y
        out = []
        for a in range(len(x_refs)):
            for k, (tx, ty) in enumerate([(1 - x, y), (x, 1 - y), (1 - x, 1 - y)]):
                mine = _remote(_layers_half(x_refs[a], c, 1), _layers_half(o_refs[a].at[s], c, 1), send_sems, recv_sems,
                               3 * a + k, (tx, ty, c))
                got = _layers_half(o_refs[a].at[2 * tx + ty], c, 1)
                out.append((mine, _remote(got, got, send_sems, recv_sems, 3 * a + k, (tx, ty, c))))
        return out

    def start(self, x_refs, o_refs, send_sems, recv_sems):
        for mine, _ in self._copies(x_refs, o_refs, send_sems, recv_sems):
            mine.start()

    def wait(self, x_refs, o_refs, send_sems, recv_sems):
        for mine, theirs in self._copies(x_refs, o_refs, send_sems, recv_sems):
            theirs.wait_recv()
            mine.wait_send()


class GatherToSibling:
    def __init__(self, gathered):
        self.ins = list(gathered)
        self.outs = [jax.ShapeDtypeStruct(g.shape, g.dtype) for g in gathered]
        self.aliases = {a: a for a in range(len(gathered))}
        self.n_sems = 3 * len(gathered)

    def _copies(self, o_refs, send_sems, recv_sems):
        x, y, c = _my_pos()
        out = []
        for a in range(len(o_refs)):
            for k, (tx, ty) in enumerate([(1 - x, y), (x, 1 - y), (1 - x, 1 - y)]):
                got = _layers_half(o_refs[a].at[2 * tx + ty], c, 1)
                theirs = _layers_half(o_refs[a].at[2 * tx + ty], 1 - c, 1)
                out.append((_remote(got, got, send_sems, recv_sems, 3 * a + k, (x, y, 1 - c)),
                            _remote(theirs, theirs, send_sems, recv_sems, 3 * a + k, (x, y, 1 - c))))
        return out

    def start(self, x_refs, o_refs, send_sems, recv_sems):
        for mine, _ in self._copies(o_refs, send_sems, recv_sems):
            mine.start()

    def wait(self, x_refs, o_refs, send_sems, recv_sems):
        for mine, theirs in self._copies(o_refs, send_sems, recv_sems):
            theirs.wait_recv()
            mine.wait_send()


def run_side(side, name):
    n_in, n_out = len(side.ins), len(side.outs)

    def body(*refs):
        parts = refs[:n_in], refs[n_in:n_in + n_out], refs[n_in + n_out], refs[n_in + n_out + 1]
        side.start(*parts)
        side.wait(*parts)

    return _hbm_call(body, name, side.ins, side.outs, side.n_sems, aliases=side.aliases)


def weights_gather(ws, name):
    return run_side(GatherToSibling(run_side(GatherOverIci(ws), name + "_ici")), name + "_sibling")


def halves_to_sibling(gs, name):
    n = len(gs)

    def body(*refs):
        x_refs, o_refs, send_sems, recv_sems = refs[:n], refs[n:2 * n], refs[2 * n], refs[2 * n + 1]
        x, y, c = _my_pos()
        cps = [_remote(_layers_half(x_refs[a], 1 - c, axis=1), o_refs[a], send_sems, recv_sems, a, (x, y, 1 - c))
               for a in range(n)]
        for cp in cps:
            cp.start()
        for cp in cps:
            cp.wait()

    outs = [jax.ShapeDtypeStruct((g.shape[0], g.shape[1] // 2) + g.shape[2:], g.dtype) for g in gs]
    return _hbm_call(body, name, gs, outs, n)


def chip_exchange(xs, name):
    n = len(xs)

    def body(*refs):
        x_refs, o_refs = refs[:n], refs[n:2 * n]
        send_sems, recv_sems = refs[2 * n:]
        x, y, c = _my_pos()
        s = 2 * x + y
        chips = [(1 - x, y), (x, 1 - y), (1 - x, 1 - y)]
        sends = []
        for a in range(n):
            for k, (tx, ty) in enumerate(chips):
                cp = _remote(x_refs[a].at[2 * tx + ty], o_refs[a].at[s], send_sems, recv_sems, 3 * a + k, (tx, ty, c))
                cp.start()
                sends.append(cp)
        for a in range(n):
            for k, (tx, ty) in enumerate(chips):
                got = o_refs[a].at[2 * tx + ty]
                _remote(got, got, send_sems, recv_sems, 3 * a + k, (tx, ty, c)).wait_recv()
        for cp in sends:
            cp.wait_send()

    outs = [jax.ShapeDtypeStruct(v.shape, v.dtype) for v in xs]
    return _hbm_call(body, name, xs, outs, 3 * n)


def halves_gather(xs, name):
    n = len(xs)

    def body(*refs):
        x_refs, o_refs, send_sems, recv_sems = refs[:n], refs[n:2 * n], refs[2 * n], refs[2 * n + 1]
        x, y, c = _my_pos()
        sends = []
        for a in range(n):
            cp = _remote(_layers_half(x_refs[a], c), _layers_half(o_refs[a], c), send_sems, recv_sems, a, (x, y, 1 - c))
            cp.start()
            sends.append(cp)
        for a in range(n):
            theirs = _layers_half(o_refs[a], 1 - c)
            _remote(theirs, theirs, send_sems, recv_sems, a, (x, y, 1 - c)).wait_recv()
        for cp in sends:
            cp.wait_send()

    outs = [jax.ShapeDtypeStruct(v.shape, v.dtype) for v in xs]
    return _hbm_call(body, name, xs, outs, n, aliases={a: a for a in range(n)})


def sum_leading(xs, name):
    n, r, c = xs.shape

    def body(x_ref, o_ref):
        acc = x_ref[0]
        for i in range(1, n):
            acc = acc + x_ref[i]
        o_ref[...] = acc

    return pl.pallas_call(body, name=name, out_shape=jax.ShapeDtypeStruct((r, c), xs.dtype))(xs)


def add_cast(a, b, out_dtype, name):
    n, r, c = a.shape
    tr = _pick(r, (512, 256, 128, 64, 32, 16))

    def body(a_ref, b_ref, o_ref):
        o_ref[...] = (a_ref[...].astype(F32) + b_ref[...].astype(F32)).astype(out_dtype)

    spec = pl.BlockSpec((1, tr, c), lambda i, j: (i, j, 0))
    return pl.pallas_call(body, name=name, grid=(n, r // tr), in_specs=[spec, spec], out_specs=spec,
                          out_shape=jax.ShapeDtypeStruct((n, r, c), out_dtype),
                          compiler_params=_params(("parallel", "parallel")))(a, b)


def sum_chips(parts, own, where, name):
    n, nl, r, c = parts.shape
    tr = _pick(r, (512, 256, 128, 64, 32, 16))

    def body(w_ref, p_ref, o_ref, out_ref):
        s = w_ref[0]
        acc = jnp.zeros(out_ref.shape, F32)
        for t in range(n):
            acc = acc + jnp.where(s == t, o_ref[0], p_ref[t]).astype(F32)
        out_ref[...] = acc

    grid_spec = pltpu.PrefetchScalarGridSpec(
        num_scalar_prefetch=1, grid=(nl, r // tr),
        in_specs=[pl.BlockSpec((n, 1, tr, c), lambda i, j, w: (0, i, j, 0)),
                  pl.BlockSpec((1, 1, tr, c), lambda i, j, w: (w[0], i, j, 0))],
        out_specs=pl.BlockSpec((1, tr, c), lambda i, j, w: (w[1] * nl + i, j, 0)))
    return pl.pallas_call(body, name=name, grid_spec=grid_spec, out_shape=jax.ShapeDtypeStruct((2 * nl, r, c), F32),
                          compiler_params=_params(("parallel", "parallel")))(where, parts, own)


def ada_forward(c_all, w_ada):
    nl, d, n = w_ada.shape
    nb = c_all.shape[0]

    def body(c_ref, w_ref, o_ref):
        cv = c_ref[...]
        ca = (cv * _sigmoid(cv)).astype(BF16)
        o_ref[0] = _nn(ca, w_ref[0].astype(BF16))

    return pl.pallas_call(
        body, name="ada_forward", grid=(nl,),
        in_specs=[pl.BlockSpec((nb, d), lambda l: (0, 0)), pl.BlockSpec((1, d, n), lambda l: (l, 0, 0))],
        out_specs=pl.BlockSpec((1, nb, n), lambda l: (l, 0, 0)),
        out_shape=jax.ShapeDtypeStruct((nl, nb, n), F32), compiler_params=_params(("parallel",)))(c_all, w_ada)


def ada_backward(c_all, dmod):
    nl, nb, n = dmod.shape
    d = c_all.shape[1]

    def body(c_ref, g_ref, o_ref):
        cv = c_ref[...]
        ca = (cv * _sigmoid(cv)).astype(BF16)
        o_ref[0] = _tn(ca, g_ref[0].astype(BF16))

    return pl.pallas_call(
        body, name="ada_backward", grid=(nl,),
        in_specs=[pl.BlockSpec((nb, d), lambda l: (0, 0)), pl.BlockSpec((1, nb, n), lambda l: (l, 0, 0))],
        out_specs=pl.BlockSpec((1, d, n), lambda l: (l, 0, 0)),
        out_shape=jax.ShapeDtypeStruct((nl, d, n), F32), compiler_params=_params(("parallel",)))(c_all, dmod)


def matmul_tn(a, b, name):
    k, m = a.shape
    n = b.shape[1]
    tm, tk = _pick(m, (512, 256, 128)), _pick(k, (512, 256, 128))
    tn = n if n * (tm * 8 + tk * 4) <= VMEM_LIMIT // 2 else _pick(n, (512, 256, 128))

    def body(a_ref, b_ref, o_ref):
        @pl.when(pl.program_id(2) == 0)
        def _():
            o_ref[...] = jnp.zeros_like(o_ref)

        o_ref[...] += _tn(a_ref[...], b_ref[...])

    return pl.pallas_call(
        body, name=name, grid=(m // tm, n // tn, k // tk),
        in_specs=[pl.BlockSpec((tk, tm), lambda i, j, kk: (kk, i)), pl.BlockSpec((tk, tn), lambda i, j, kk: (kk, j))],
        out_specs=pl.BlockSpec((tm, tn), lambda i, j, kk: (i, j)),
        out_shape=jax.ShapeDtypeStruct((m, n), F32),
        compiler_params=_params(("parallel", "parallel", "arbitrary")))(a, b)


def adamw(w, g, m, v, name):
    nl, r, c = w.shape
    tr = _pick(r, (512, 256, 128, 64, 32, 16, 8))

    def body(w_ref, g_ref, m_ref, v_ref, d_ref, mo_ref, vo_ref):
        gv = g_ref[...]
        mn = ADAM_B1 * m_ref[...] + (1.0 - ADAM_B1) * gv
        vn = ADAM_B2 * v_ref[...] + (1.0 - ADAM_B2) * (gv * gv)
        m_hat = mn / (1.0 - ADAM_B1 ** ADAM_STEP)
        v_hat = vn / (1.0 - ADAM_B2 ** ADAM_STEP)
        d_ref[...] = -ADAM_LR * (m_hat / (jnp.sqrt(v_hat) + ADAM_EPS) + ADAM_WD * w_ref[...])
        mo_ref[...] = mn
        vo_ref[...] = vn

    spec = pl.BlockSpec((1, tr, c), lambda l, i: (l, i, 0))
    out = jax.ShapeDtypeStruct((nl, r, c), F32)
    return pl.pallas_call(body, name=name, grid=(nl, r // tr), in_specs=[spec] * 4, out_specs=[spec] * 3,
                          out_shape=[out] * 3, compiler_params=_params(("parallel", "parallel")))(w, g, m, v)


def _rope(t, cos_t, sin_t):
    w = t.shape[1]
    lane = _lane(t.shape) & (HEAD_PAD - 1)
    first_half = (lane >= ROPE_LO) & (lane < ROPE_LO + ROPE_HALF)
    partner = jnp.where(first_half, pltpu.roll(t, w - ROPE_HALF, 1), pltpu.roll(t, ROPE_HALF, 1))
    return t * cos_t + partner * sin_t


def _rope_t(dt, cos_t, sin_t):
    w = dt.shape[1]
    lane = _lane(dt.shape) & (HEAD_PAD - 1)
    first_half = (lane >= ROPE_LO) & (lane < ROPE_LO + ROPE_HALF)
    ds = dt * sin_t
    partner = jnp.where(first_half, pltpu.roll(ds, w - ROPE_HALF, 1), pltpu.roll(ds, ROPE_HALF, 1))
    return dt * cos_t + partner


def _rms(xv, g):
    rstd = lax.rsqrt(jnp.mean(xv * xv, axis=-1, keepdims=True) + EPS)
    xh = xv * rstd
    return xh * g, xh, rstd


def _rms_bwd(dy, g, xh, rstd):
    dxh = dy * g
    dx = rstd * (dxh - xh * jnp.mean(dxh * xh, axis=-1, keepdims=True))
    return dx, jnp.sum(dy * xh, axis=0, keepdims=True)


def make_step(cfg):
    S, D, NZ = cfg.S, cfg.D, cfg.NZ
    FW, MW, QL, KVL, FH, MH = cfg.FW, cfg.MW, cfg.QL, cfg.KVL, cfg.FH, cfg.MH
    QW = MH * HEAD_PAD
    TM = _pick(S, (512, 256, 128))
    TQ = TK = _pick(S, (256, 128))
    n_tok = S // TM
    ZO = 3 * FW
    NZR = NZ - ZO
    misc_blk = (cfg.o_ms - ZO) // LANES
    FWD_UNROLL = 2
    FOX_SCALE = HEAD_DIM ** -0.5
    MLA_SCALE = (HEAD_DIM + ROPE_DIM) ** -0.5

    def tok(width, col=0):
        return pl.BlockSpec((TM, width), lambda i: (i, col))

    def const(shape):
        return pl.BlockSpec(shape, lambda i: tuple(0 for _ in shape))

    def layer(l, shape):
        return pl.BlockSpec((None,) + shape, lambda i: (l,) + tuple(0 for _ in shape))


    def ln_inproj(x, g, scale, shift, w_in, l):
        def body(x_ref, g_ref, sc_ref, sh_ref, w_ref, h_ref, z_ref, qkv_ref):
            y, _, _ = _rms(x_ref[...], g_ref[...])
            hb = (y * (1.0 + sc_ref[...]) + sh_ref[...]).astype(BF16)
            h_ref[...] = hb
            z = _nn(hb, w_ref[...])
            z_ref[...] = z[:, ZO:]
            qkv_ref[:, :FW] = (z[:, :FW] * FOX_SCALE).astype(BF16)
            qkv_ref[:, FW:] = z[:, FW:ZO].astype(BF16)

        return pl.pallas_call(
            body, name="ln_inproj", grid=(n_tok,),
            in_specs=[tok(D), const((1, D)), const((1, D)), const((1, D)), layer(l, (D, NZ))],
            out_specs=[tok(D), tok(NZR), tok(ZO)],
            out_shape=[jax.ShapeDtypeStruct((S, D), BF16), jax.ShapeDtypeStruct((S, NZR), F32),
                       jax.ShapeDtypeStruct((S, ZO), BF16)],
            compiler_params=_params(("parallel",)))(x, g, scale, shift, w_in)

    def _log_f_terms(misc, bf):
        lane = _lane(misc.shape)
        a = misc + bf
        e = jnp.exp(-jnp.abs(a))
        logf = jnp.minimum(a, 0.0) - jnp.log(1.0 + e)
        sig_neg = jnp.where(a >= 0, e, 1.0) / (1.0 + e)
        valid = lane < FH
        return jnp.where(valid, logf, 0.0), jnp.where(valid, sig_neg, 0.0)

    def fox_prep(z, bf_pad):
        def body(z_ref, b_ref, o_ref, carry):
            @pl.when(pl.program_id(0) == 0)
            def _():
                carry[...] = jnp.zeros_like(carry)

            logf, _ = _log_f_terms(z_ref[...], b_ref[...])
            row = lax.broadcasted_iota(jnp.int32, (TM, TM), 0)
            col = lax.broadcasted_iota(jnp.int32, (TM, TM), 1)
            tri = (col <= row).astype(F32)
            cum = jnp.dot(tri, logf, precision=lax.Precision.HIGHEST, preferred_element_type=F32) + carry[...]
            o_ref[...] = cum
            carry[...] = cum[TM - 1:TM, :]

        return pl.pallas_call(
            body, name="fox_prep", grid=(n_tok,),
            in_specs=[tok(LANES, misc_blk), const((1, LANES))], out_specs=tok(LANES),
            out_shape=jax.ShapeDtypeStruct((S, LANES), F32),
            scratch_shapes=[pltpu.VMEM((1, LANES), F32)],
            compiler_params=_params(("arbitrary",)))(z, bf_pad)

    def mla_prep(z, gq, gkv, w_uq, w_uk, w_v, l, cos_t, sin_t):
        def body(ql_ref, kvl_ref, ms_ref, gq_ref, gkv_ref, wq_ref, wk_ref, wv_ref, c_ref, s_ref, q_ref, k_ref, v_ref):
            cos1, sin1 = c_ref[...], s_ref[...]
            qn, _, _ = _rms(ql_ref[...], gq_ref[...])
            q = _nn(qn.astype(BF16), wq_ref[...])
            q_ref[...] = (_rope(q, jnp.tile(cos1, (1, MH)), jnp.tile(sin1, (1, MH))) * MLA_SCALE).astype(BF16)
            kvn, _, _ = _rms(kvl_ref[...], gkv_ref[...])
            kvb = kvn.astype(BF16)
            lane = _lane((TM, LANES))
            kr = jnp.where((lane >= ROPE_LO) & (lane < ROPE_LO + ROPE_DIM), ms_ref[...], 0.0)
            kr = _rope(kr, cos1, sin1)
            k_ref[...] = (_nn(kvb, wk_ref[...]) + jnp.tile(kr, (1, MH))).astype(BF16)
            v_ref[...] = _nn(kvb, wv_ref[...]).astype(BF16)

        return pl.pallas_call(
            body, name="mla_prep", grid=(n_tok,),
            in_specs=[tok(QL, (cfg.o_ql - ZO) // QL), tok(KVL, (cfg.o_kv - ZO) // KVL), tok(LANES, misc_blk),
                      const((1, QL)), const((1, KVL)), layer(l, (QL, QW)), layer(l, (KVL, QW)), layer(l, (KVL, MW)),
                      tok(LANES), tok(LANES)],
            out_specs=[tok(QW), tok(QW), tok(MW)],
            out_shape=[jax.ShapeDtypeStruct((S, QW), BF16), jax.ShapeDtypeStruct((S, QW), BF16),
                       jax.ShapeDtypeStruct((S, MW), BF16)],
            compiler_params=_params(("parallel",)))(z, z, z, gq, gkv, w_uq, w_uk, w_v, cos_t, sin_t)

    def _allowed(q0, k0, chunked):
        qi = q0 + lax.broadcasted_iota(jnp.int32, (TQ, TK), 0)
        ki = k0 + lax.broadcasted_iota(jnp.int32, (TQ, TK), 1)
        if chunked:
            return (ki >> CHUNK_SHIFT) <= (qi >> CHUNK_SHIFT)
        return ki <= qi

    def _heads(val, packed):
        if packed:
            lane = _lane(val.shape)
            zero = jnp.zeros_like(val)
            return [jnp.where(lane < HEAD_DIM, val, zero), jnp.where(lane >= HEAD_DIM, val, zero)]
        return [val[:, :HEAD_PAD], val[:, HEAD_PAD:]]

    def _merge(a0, a1):
        return jnp.where(_lane(a0.shape) < HEAD_DIM, a0, a1)

    def attn_fwd(q, k, v, q_blk0, k_blk0, v_blk0, cumt, packed, chunked, name, side=None):
        has_bias = cumt is not None
        n_pairs = (FH if packed else MH) // 2
        wq = LANES if packed else 2 * HEAD_PAD
        assert TQ == TK

        n_main = 4 if has_bias else 3
        n_side_in, n_side_out = (len(side.ins), len(side.outs)) if side else (0, 0)

        def body(*refs):
            if has_bias:
                q_ref, k_ref, v_ref, ct_ref = refs[:n_main]
            else:
                q_ref, k_ref, v_ref = refs[:n_main]
            o_ref, lse_ref = refs[n_main + n_side_in:n_main + n_side_in + 2]
            i = pl.program_id(1)
            if side:
                side_refs = (refs[n_main:n_main + n_side_in], refs[n_main + n_side_in + 2:n_main + n_side_in + 2 + n_side_out],
                             refs[-2], refs[-1])

                @pl.when((pl.program_id(0) == 0) & (i == 0))
                def _():
                    side.start(*side_refs)

            q0 = i * TQ
            qh = _heads(q_ref[...], packed)
            q_both = jnp.concatenate(qh, axis=0) if packed else None

            def scores(kb):
                kblk = k_ref[pl.ds(pl.multiple_of(kb * TK, TK), TK), :]
                if packed:
                    return _nt(q_both, kblk)
                return jnp.concatenate([_nt(qh[0], kblk[:, :HEAD_PAD]), _nt(qh[1], kblk[:, HEAD_PAD:])], axis=0)

            def update(kb, s, carry, masked):
                m, l, acc = carry
                k0 = pl.multiple_of(kb * TK, TK)
                s0, s1 = s[:TQ], s[TQ:]
                if has_bias:
                    ck = ct_ref[0, kb]
                    s0, s1 = s0 - ck[0:1, :], s1 - ck[1:2, :]
                if masked:
                    allow = _allowed(q0, k0, chunked)
                    s0, s1 = jnp.where(allow, s0, NEG), jnp.where(allow, s1, NEG)
                s = jnp.concatenate([s0, s1], axis=0)
                m_new = jnp.maximum(m, jnp.max(s, axis=-1, keepdims=True))
                p = jnp.exp(s - m_new)
                alpha = jnp.exp(m - m_new)
                l = alpha * l + jnp.sum(p, axis=-1, keepdims=True)
                acc = alpha * acc + _nn(p.astype(BF16), v_ref[pl.ds(k0, TK), :])
                return m_new, l, acc

            def several(kb, carry, n, last_masked):
                ss = [scores(kb + u) for u in range(n)]
                for u in range(n):
                    carry = update(kb + u, ss[u], carry, last_masked and u == n - 1)
                return carry

            init = (jnp.full((2 * TQ, 1), NEG, F32), jnp.zeros((2 * TQ, 1), F32), jnp.zeros((2 * TQ, LANES), F32))
            carry = lax.fori_loop(0, i // FWD_UNROLL, lambda t, cr: several(FWD_UNROLL * t, cr, FWD_UNROLL, False), init)
            m, l, acc = lax.switch(i % FWD_UNROLL,
                                   [lambda cr, r=r: several(i - r, cr, r + 1, True) for r in range(FWD_UNROLL)], carry)
            o = acc / l
            lse = m + jnp.log(l)
            lane = _lane((TQ, LANES))
            o_ref[...] = _merge(o[:TQ], o[TQ:])
            lse_ref[...] = jnp.where(lane == 0, lse[:TQ], jnp.where(lane == 1, lse[TQ:], 0.0))
            if side:
                @pl.when((pl.program_id(0) == n_pairs - 1) & (i == S // TQ - 1))
                def _():
                    side.wait(*side_refs)

        in_specs = [pl.BlockSpec((TQ, wq), lambda p, i: (i, q_blk0 + p)),
                    pl.BlockSpec((S, wq), lambda p, i: (0, k_blk0 + p)),
                    pl.BlockSpec((S, LANES), lambda p, i: (0, v_blk0 + p))]
        args = [q, k, v]
        if has_bias:
            in_specs += [pl.BlockSpec((1, S // TK, 8, TK), lambda p, i: (p, 0, 0, 0))]
            args += [cumt]
        out_specs = [pl.BlockSpec((TQ, LANES), lambda p, i: (i, p)), pl.BlockSpec((TQ, LANES), lambda p, i: (i, p))]
        out_shape = [jax.ShapeDtypeStruct((S, n_pairs * LANES), F32), jax.ShapeDtypeStruct((S, n_pairs * LANES), F32)]
        extra = {}
        if side:
            hbm = pl.BlockSpec(memory_space=pl.ANY)
            in_specs += [hbm] * n_side_in
            args += side.ins
            out_specs += [hbm] * n_side_out
            out_shape += side.outs
            extra = dict(scratch_shapes=[pltpu.SemaphoreType.DMA((side.n_sems,)), pltpu.SemaphoreType.DMA((side.n_sems,))],
                         input_output_aliases={n_main + a: 2 + b for a, b in side.aliases.items()})
        outs = pl.pallas_call(
            body, name=name, grid=(n_pairs, S // TQ), in_specs=in_specs, out_specs=out_specs, out_shape=out_shape,
            compiler_params=_params(("arbitrary", "arbitrary")), **extra)(*args)
        return (outs[0], outs[1], list(outs[2:])) if side else outs

    def gate_outproj(of, om, z, w_out, l, x, gate):
        def body(of_ref, om_ref, fg_ref, mg_ref, w_ref, x_ref, gt_ref, xn_ref, u_ref, y_ref):
            fg, mg = fg_ref[...], mg_ref[...]
            u = jnp.concatenate([of_ref[...] * fg * _sigmoid(fg), om_ref[...] * mg * _sigmoid(mg)], axis=1).astype(BF16)
            y = _nn(u, w_ref[...])
            u_ref[...] = u
            y_ref[...] = y.astype(BF16)
            xn_ref[...] = x_ref[...] + gt_ref[...] * y

        return pl.pallas_call(
            body, name="gate_outproj", grid=(n_tok,),
            in_specs=[tok(FW), tok(MW), tok(FW, (cfg.o_fg - ZO) // FW), tok(MW, (cfg.o_mg - ZO) // MW),
                      layer(l, (FW + MW, D)), tok(D), const((1, D))],
            out_specs=[tok(D), tok(FW + MW), tok(D)],
            out_shape=[jax.ShapeDtypeStruct((S, D), F32), jax.ShapeDtypeStruct((S, FW + MW), BF16),
                       jax.ShapeDtypeStruct((S, D), BF16)],
            compiler_params=_params(("parallel",)))(of, om, z, z, w_out, x, gate)

    def final_loss(x, g, target):
        def body(x_ref, g_ref, t_ref, dx_ref, acc_ref, loss_ref):
            @pl.when(pl.program_id(0) == 0)
            def _():
                acc_ref[...] = jnp.zeros_like(acc_ref)
                loss_ref[...] = jnp.zeros_like(loss_ref)

            gv = g_ref[...]
            y, xh, rstd = _rms(x_ref[...], gv)
            e = y - t_ref[...]
            loss_ref[...] += 0.5 * jnp.sum(jnp.sum(e * e, axis=-1, keepdims=True) / D, axis=0, keepdims=True)
            dx, dg = _rms_bwd(e / D, gv, xh, rstd)
            dx_ref[...] = dx
            acc_ref[0:1, :] += dg

        return pl.pallas_call(
            body, name="final_loss", grid=(n_tok,),
            in_specs=[tok(D), const((1, D)), tok(D)],
            out_specs=[tok(D), const((8, D)), const((1, LANES))],
            out_shape=[jax.ShapeDtypeStruct((S, D), F32), jax.ShapeDtypeStruct((8, D), F32),
                       jax.ShapeDtypeStruct((1, LANES), F32)],
            compiler_params=_params(("arbitrary",)))(x, g, target)


    def bwd_out(dxn, gate, y, w_out, l, of, om, z, lse_f, lse_m):
        def body(dx_ref, gt_ref, y_ref, w_ref, of_ref, om_ref, fg_ref, mg_ref, lf_ref, lm_ref,
                 dof_ref, dom_ref, dfg_ref, dmg_ref, dy_ref, acc_ref, sf_ref, sm_ref):
            @pl.when(pl.program_id(0) == 0)
            def _():
                acc_ref[...] = jnp.zeros_like(acc_ref)

            dxv = dx_ref[...]
            acc_ref[0:1, :] += jnp.sum(dxv * y_ref[...].astype(F32), axis=0, keepdims=True)
            dy = (gt_ref[...] * dxv).astype(BF16)
            dy_ref[...] = dy
            du = _nt(dy, w_ref[...])
            lane = _lane((TM, LANES))
            for lo, width, o_ref, g_ref, do_ref, dg_ref, l_ref, st_ref in (
                    (0, FW, of_ref, fg_ref, dof_ref, dfg_ref, lf_ref, sf_ref),
                    (FW, MW, om_ref, mg_ref, dom_ref, dmg_ref, lm_ref, sm_ref)):
                gv, ov = g_ref[...], o_ref[...]
                sg = _sigmoid(gv)
                dup = du[:, lo:lo + width]
                dob = (dup * gv * sg).astype(BF16)
                do_ref[...] = dob
                dg_ref[...] = (dup * ov * sg * (1.0 + gv * (1.0 - sg))).astype(BF16)
                d = dob.astype(F32) * ov
                for pr in range(width // LANES):
                    cols = slice(pr * LANES, (pr + 1) * LANES)
                    dp = d[:, cols]
                    d0 = jnp.sum(jnp.where(lane < HEAD_DIM, dp, 0.0), axis=-1, keepdims=True)
                    d1 = jnp.sum(jnp.where(lane >= HEAD_DIM, dp, 0.0), axis=-1, keepdims=True)
                    st_ref[:, cols] = jnp.where(lane == 2, d0, jnp.where(lane == 3, d1, l_ref[:, cols]))

        return pl.pallas_call(
            body, name="bwd_out", grid=(n_tok,),
            in_specs=[tok(D), const((1, D)), tok(D), layer(l, (FW + MW, D)), tok(FW), tok(MW),
                      tok(FW, (cfg.o_fg - ZO) // FW), tok(MW, (cfg.o_mg - ZO) // MW), tok(FW), tok(MW)],
            out_specs=[tok(FW), tok(MW), tok(FW), tok(MW), tok(D), const((8, D)), tok(FW), tok(MW)],
            out_shape=[jax.ShapeDtypeStruct((S, FW), BF16), jax.ShapeDtypeStruct((S, MW), BF16),
                       jax.ShapeDtypeStruct((S, FW), BF16), jax.ShapeDtypeStruct((S, MW), BF16),
                       jax.ShapeDtypeStruct((S, D), BF16), jax.ShapeDtypeStruct((8, D), F32),
                       jax.ShapeDtypeStruct((S, FW), F32), jax.ShapeDtypeStruct((S, MW), F32)],
            compiler_params=_params(("arbitrary",)))(dxn, gate, y, w_out, of, om, z, z, lse_f, lse_m)

    def attn_bwd(q, k, v, do, stats, q_blk0, k_blk0, v_blk0, cumt, packed, chunked, name):
        has_bias = cumt is not None
        n_pairs = (FH if packed else MH) // 2
        wq = LANES if packed else 2 * HEAD_PAD
        n_q = S // TQ
        assert TQ == TK

        def body(*refs):
            if has_bias:
                q_ref, k_ref, v_ref, do_ref, st_ref, ct_ref, dq_ref, dk_ref, dv_ref, dc_ref, dr_ref = refs
            else:
                q_ref, k_ref, v_ref, do_ref, st_ref, dq_ref, dk_ref, dv_ref = refs
            jb = pl.program_id(1)
            k0 = jb * TK

            @pl.when(jb == 0)
            def _():
                dq_ref[...] = jnp.zeros_like(dq_ref)
                if has_bias:
                    dr_ref[...] = jnp.zeros_like(dr_ref)

            dk_ref[...] = jnp.zeros_like(dk_ref)
            dv_ref[...] = jnp.zeros_like(dv_ref)
            kh = _heads(k_ref[...], packed)
            k_both = jnp.concatenate(kh, axis=0) if packed else None
            v_both = jnp.concatenate(_heads(v_ref[...], True), axis=0)
            ck = jnp.concatenate([ct_ref[0, 0][0:1, :], ct_ref[0, 0][1:2, :]], axis=1) if has_bias else None

            def products(ib):
                rows = pl.ds(pl.multiple_of(ib * TQ, TQ), TQ)
                q2, do2 = q_ref[rows, :], do_ref[rows, :]
                if packed:
                    s = _nt(q2, k_both)
                else:
                    qh = _heads(q2, False)
                    s = jnp.concatenate([_nt(qh[0], kh[0]), _nt(qh[1], kh[1])], axis=1)
                return s, _nt(do2, v_both)

            def update(ib, s, dp, carry, masked):
                q0 = pl.multiple_of(ib * TQ, TQ)
                rows = pl.ds(q0, TQ)
                q2, do2, st = q_ref[rows, :], do_ref[rows, :], st_ref[rows, :]
                if not packed:
                    qh = _heads(q2, False)
                if has_bias:
                    s = s - ck
                if masked:
                    allow = _allowed(q0, k0, chunked)
                    s = jnp.where(jnp.concatenate([allow, allow], axis=1), s, NEG)
                p = jnp.concatenate([jnp.exp(s[:, :TK] - st[:, 0:1]), jnp.exp(s[:, TK:] - st[:, 1:2])], axis=1)
                dv2 = _tn(p.astype(BF16), do2)
                ds = jnp.concatenate([p[:, :TK] * (dp[:, :TK] - st[:, 2:3]), p[:, TK:] * (dp[:, TK:] - st[:, 3:4])], axis=1)
                dsb = ds.astype(BF16)
                dv_ref[...] += _merge(dv2[:TK], dv2[TK:])
                if packed:
                    dk2 = _tn(dsb, q2)
                    dk_ref[...] += _merge(dk2[:TK], dk2[TK:])
                    dq_ref[rows, :] += _nn(dsb, k_both)
                else:
                    dk_ref[...] += jnp.concatenate([_tn(dsb[:, :TK], qh[0]), _tn(dsb[:, TK:], qh[1])], axis=1)
                    dq_ref[rows, :] += jnp.concatenate([_nn(dsb[:, :TK], kh[0]), _nn(dsb[:, TK:], kh[1])], axis=1)
                if has_bias:
                    lane = _lane((TQ, LANES))
                    r0 = jnp.sum(ds[:, :TK], axis=-1, keepdims=True)
                    r1 = jnp.sum(ds[:, TK:], axis=-1, keepdims=True)
                    dr_ref[0, rows, :] += jnp.where(lane == 0, r0, jnp.where(lane == 1, r1, 0.0))
                    return carry - jnp.sum(ds, axis=0, keepdims=True)
                return carry

            def step(ib, carry, masked):
                s, dp = products(ib)
                return update(ib, s, dp, carry, masked)

            def two(ib, carry):
                s_a, dp_a = products(ib)
                s_b, dp_b = products(ib + 1)
                return update(ib + 1, s_b, dp_b, update(ib, s_a, dp_a, carry, False), False)

            dc = step(jb, jnp.zeros((1, 2 * TK), F32), True)
            n_rest = n_q - 1 - jb
            dc = lax.fori_loop(0, n_rest // 2, lambda t, cr: two(jb + 1 + 2 * t, cr), dc)
            dc = lax.cond(n_rest % 2 == 1, lambda cr: step(n_q - 1, cr, False), lambda cr: cr, dc)
            if has_bias:
                sub = lax.broadcasted_iota(jnp.int32, (8, TK), 0)
                dc_ref[0, 0] = jnp.where(sub == 0, dc[:, :TK], jnp.where(sub == 1, dc[:, TK:], 0.0))

        in_specs = [pl.BlockSpec((S, wq), lambda p, j: (0, q_blk0 + p)),
                    pl.BlockSpec((TK, wq), lambda p, j: (j, k_blk0 + p)),
                    pl.BlockSpec((TK, LANES), lambda p, j: (j, v_blk0 + p)),
                    pl.BlockSpec((S, LANES), lambda p, j: (0, p)),
                    pl.BlockSpec((S, LANES), lambda p, j: (0, p))]
        args = [q, k, v, do, stats]
        out_specs = [pl.BlockSpec((S, wq), lambda p, j: (0, p)),
                     pl.BlockSpec((TK, wq), lambda p, j: (j, p)),
                     pl.BlockSpec((TK, LANES), lambda p, j: (j, p))]
        out_shape = [jax.ShapeDtypeStruct((S, n_pairs * wq), F32), jax.ShapeDtypeStruct((S, n_pairs * wq), F32),
                     jax.ShapeDtypeStruct((S, n_pairs * LANES), F32)]
        if has_bias:
            in_specs += [pl.BlockSpec((1, 1, 8, TK), lambda p, j: (p, j, 0, 0))]
            args += [cumt]
            out_specs += [pl.BlockSpec((1, 1, 8, TK), lambda p, j: (p, j, 0, 0)),
                          pl.BlockSpec((1, S, LANES), lambda p, j: (p, 0, 0))]
            out_shape += [jax.ShapeDtypeStruct((n_pairs, S // TK, 8, TK), F32),
                          jax.ShapeDtypeStruct((n_pairs, S, LANES), F32)]
        return pl.pallas_call(
            body, name=name, grid=(n_pairs, S // TK), in_specs=in_specs, out_specs=out_specs, out_shape=out_shape,
            compiler_params=_params(("parallel", "arbitrary")))(*args)

    def fox_post(dcum, z, bf_pad):
        def rev(i):
            return n_tok - 1 - i

        def body(dc_ref, z_ref, b_ref, dff_ref, acc_ref, carry):
            @pl.when(pl.program_id(0) == 0)
            def _():
                carry[...] = jnp.zeros_like(carry)
                acc_ref[...] = jnp.zeros_like(acc_ref)

            _, sig_neg = _log_f_terms(z_ref[...], b_ref[...])
            row = lax.broadcasted_iota(jnp.int32, (TM, TM), 0)
            col = lax.broadcasted_iota(jnp.int32, (TM, TM), 1)
            tri = (col >= row).astype(F32)
            dlog = jnp.dot(tri, dc_ref[...], precision=lax.Precision.HIGHEST, preferred_element_type=F32) + carry[...]
            carry[...] = dlog[0:1, :]
            dff = dlog * sig_neg
            dff_ref[...] = dff
            acc_ref[0:1, :] += jnp.sum(dff, axis=0, keepdims=True)

        return pl.pallas_call(
            body, name="fox_post", grid=(n_tok,),
            in_specs=[pl.BlockSpec((TM, LANES), lambda i: (rev(i), 0)),
                      pl.BlockSpec((TM, LANES), lambda i: (rev(i), misc_blk)), const((1, LANES))],
            out_specs=[pl.BlockSpec((TM, LANES), lambda i: (rev(i), 0)), const((8, LANES))],
            out_shape=[jax.ShapeDtypeStruct((S, LANES), F32), jax.ShapeDtypeStruct((8, LANES), F32)],
            scratch_shapes=[pltpu.VMEM((1, LANES), F32)],
            compiler_params=_params(("arbitrary",)))(dcum, z, bf_pad)

    def mla_post(dq, dk, dv, dff, z, gq, gkv, w_uq, w_uk, w_v, l, cos_t, sin_t):
        def body(dq_ref, dk_ref, dv_ref, dff_ref, ql_ref, kvl_ref, gq_ref, gkv_ref, wq_ref, wk_ref, wv_ref,
                 c_ref, s_ref, zq_ref, zkv_ref, zms_ref, dwq_ref, dwk_ref, dwv_ref, dgq_ref, dgkv_ref):
            @pl.when(pl.program_id(0) == 0)
            def _():
                for r in (dwq_ref, dwk_ref, dwv_ref, dgq_ref, dgkv_ref):
                    r[...] = jnp.zeros_like(r)

            cos1, sin1 = c_ref[...], s_ref[...]
            gqv, gkvv = gq_ref[...], gkv_ref[...]
            qn, qxh, qrstd = _rms(ql_ref[...], gqv)
            dq_pre = _rope_t(dq_ref[...] * MLA_SCALE, jnp.tile(cos1, (1, MH)), jnp.tile(sin1, (1, MH))).astype(BF16)
            dwq_ref[...] += _tn(qn.astype(BF16), dq_pre)
            dql, dgq = _rms_bwd(_nt(dq_pre, wq_ref[...]), gqv, qxh, qrstd)
            zq_ref[...] = dql.astype(BF16)
            dgq_ref[0:1, :] += dgq

            dkv = dk_ref[...]
            lane = _lane(dkv.shape) & (HEAD_PAD - 1)
            dkn = jnp.where(lane < HEAD_DIM, dkv, 0.0).astype(BF16)
            dkr = dkv[:, 0:HEAD_PAD]
            for hd in range(1, MH):
                dkr = dkr + dkv[:, hd * HEAD_PAD:(hd + 1) * HEAD_PAD]
            lane1 = _lane(dkr.shape)
            dkr = jnp.where((lane1 >= ROPE_LO) & (lane1 < ROPE_LO + ROPE_DIM), dkr, 0.0)
            dkr = _rope_t(dkr, cos1, sin1)
            zms_ref[...] = (dkr + dff_ref[...]).astype(BF16)

            kvn, kxh, krstd = _rms(kvl_ref[...], gkvv)
            kvb = kvn.astype(BF16)
            dvb = dv_ref[...].astype(BF16)
            dwk_ref[...] += _tn(kvb, dkn)
            dwv_ref[...] += _tn(kvb, dvb)
            dkvl, dgkv = _rms_bwd(_nt(dkn, wk_ref[...]) + _nt(dvb, wv_ref[...]), gkvv, kxh, krstd)
            zkv_ref[...] = dkvl.astype(BF16)
            dgkv_ref[0:1, :] += dgkv

        return pl.pallas_call(
            body, name="mla_post", grid=(n_tok,),
            in_specs=[tok(QW), tok(QW), tok(MW), tok(LANES), tok(QL, (cfg.o_ql - ZO) // QL), tok(KVL, (cfg.o_kv - ZO) // KVL),
                      const((1, QL)), const((1, KVL)), layer(l, (QL, QW)), layer(l, (KVL, QW)), layer(l, (KVL, MW)),
                      tok(LANES), tok(LANES)],
            out_specs=[tok(QL), tok(KVL), tok(LANES), const((QL, QW)), const((KVL, QW)), const((KVL, MW)),
                       const((8, QL)), const((8, KVL))],
            out_shape=[jax.ShapeDtypeStruct((S, QL), BF16), jax.ShapeDtypeStruct((S, KVL), BF16),
                       jax.ShapeDtypeStruct((S, LANES), BF16), jax.ShapeDtypeStruct((QL, QW), F32),
                       jax.ShapeDtypeStruct((KVL, QW), F32), jax.ShapeDtypeStruct((KVL, MW), F32),
                       jax.ShapeDtypeStruct((8, QL), F32), jax.ShapeDtypeStruct((8, KVL), F32)],
            compiler_params=_params(("arbitrary",)))(dq, dk, dv, dff, z, z, gq, gkv, w_uq, w_uk, w_v, cos_t, sin_t)

    def bwd_in(dz, w_in, l, x, dxn, g, scale):
        def body(dz_ref, w_ref, x_ref, dx_ref, g_ref, sc_ref, o_ref, acc_ref):
            @pl.when(pl.program_id(0) == 0)
            def _():
                acc_ref[...] = jnp.zeros_like(acc_ref)

            dh = _nt(dz_ref[...], w_ref[...])
            gv, mod = g_ref[...], 1.0 + sc_ref[...]
            _, xh, rstd = _rms(x_ref[...], gv)
            t = dh * xh
            acc_ref[0:1, :] += jnp.sum(dh, axis=0, keepdims=True)
            acc_ref[1:2, :] += jnp.sum(t * gv, axis=0, keepdims=True)
            acc_ref[2:3, :] += jnp.sum(t * mod, axis=0, keepdims=True)
            dx, _ = _rms_bwd(dh, gv * mod, xh, rstd)
            o_ref[...] = dx_ref[...] + dx

        return pl.pallas_call(
            body, name="bwd_in", grid=(n_tok,),
            in_specs=[tok(NZ), layer(l, (D, NZ)), tok(D), tok(D), const((1, D)), const((1, D))],
            out_specs=[tok(D), const((8, D))],
            out_shape=[jax.ShapeDtypeStruct((S, D), F32), jax.ShapeDtypeStruct((8, D), F32)],
            compiler_params=_params(("arbitrary",)))(dz, w_in, x, dxn, g, scale)


    def pair_rows(cum):
        n_pairs = FH // 2
        ct = jnp.pad(cum[:, :FH].T.reshape(n_pairs, 2, S), ((0, 0), (0, 6), (0, 0)))
        return ct.reshape(n_pairs, 8, S // TK, TK).transpose(0, 2, 1, 3)

    def bias_grad(dc, dr):
        n_pairs = FH // 2
        d = dc.transpose(0, 2, 1, 3).reshape(n_pairs, 8, S)[:, :2, :].reshape(FH, S).T
        d = d + dr[:, :, :2].transpose(1, 0, 2).reshape(S, FH)
        return jnp.pad(d, ((0, 0), (0, LANES - FH)))

    def layer_forward(x, wl, mod, later_shards=None):
        shift, scale, gate = mod
        h, z, qkv = ln_inproj(x, wl.norm_g, scale, shift, wl.w_in, wl.l)
        ct = pair_rows(fox_prep(z, wl.bf_pad))
        fox = attn_fwd(qkv, qkv, qkv, cfg.o_fq // LANES, cfg.o_fk // LANES, cfg.o_fv // LANES, ct, True, False,
                       "fox_fwd_gather" if later_shards else "fox_fwd", GatherOverIci(later_shards) if later_shards else None)
        of, lse_f = fox[0], fox[1]
        qp, kp, vp = mla_prep(z, wl.gq, wl.gkv, wl.w_uq, wl.w_uk, wl.w_v, wl.l, cfg.cos_t, cfg.sin_t)
        mla = attn_fwd(qp, kp, vp, 0, 0, 0, None, False, True, "mla_fwd_gather" if later_shards else "mla_fwd",
                       GatherToSibling(fox[2]) if later_shards else None)
        om, lse_m = mla[0], mla[1]
        xn, u, y = gate_outproj(of, om, z, wl.w_out, wl.l, x, gate)
        saved = types.SimpleNamespace(x=x, h=h, z=z, qkv=qkv, ct=ct, of=of, lse_f=lse_f, qp=qp, kp=kp, vp=vp,
                                      om=om, lse_m=lse_m, u=u, y=y)
        return (xn, saved, mla[2]) if later_shards else (xn, saved)

    def layer_backward(dxn, sv, wl, mod):
        shift, scale, gate = mod
        do_f, do_m, dfg, dmg, dy, acc_o, st_f, st_m = bwd_out(dxn, gate, sv.y, wl.w_out, wl.l, sv.of, sv.om, sv.z,
                                                              sv.lse_f, sv.lse_m)
        dw_out = matmul_tn(sv.u, dy, "dw_out")
        dfq, dfk, dfv, dck, dcr = attn_bwd(sv.qkv, sv.qkv, sv.qkv, do_f, st_f, cfg.o_fq // LANES, cfg.o_fk // LANES,
                                           cfg.o_fv // LANES, sv.ct, True, False, "fox_bwd")
        dff, acc_f = fox_post(bias_grad(dck, dcr), sv.z, wl.bf_pad)
        dqp, dkp, dvp = attn_bwd(sv.qp, sv.kp, sv.vp, do_m, st_m, 0, 0, 0, None, False, True, "mla_bwd")
        zq, zkv, zms, dw_uq, dw_uk, dw_v, dgq, dgkv = mla_post(
            dqp, dkp, dvp, dff, sv.z, wl.gq, wl.gkv, wl.w_uq, wl.w_uk, wl.w_v, wl.l, cfg.cos_t, cfg.sin_t)
        dz = jnp.concatenate([(dfq * FOX_SCALE).astype(BF16), dfk.astype(BF16), dfv.astype(BF16), dfg, dmg, zq, zkv, zms], axis=1)
        dx, acc_i = bwd_in(dz, wl.w_in, wl.l, sv.x, dxn, wl.norm_g, scale)
        dw_in = matmul_tn(sv.h, dz, "dw_in")
        grads = types.SimpleNamespace(
            w_in=dw_in, w_out=dw_out, w_uq=dw_uq, w_uk=dw_uk, w_v=dw_v, gq=dgq[0], gkv=dgkv[0],
            b_f=acc_f[0, :FH], norm_g=acc_i[2], dmod=jnp.concatenate([acc_i[0], acc_i[1], acc_o[0]]))
        return dx, grads

    return types.SimpleNamespace(layer_forward=layer_forward, layer_backward=layer_backward, final_loss=final_loss)


def _pack_rows(parts, dtype, row_multiple):
    flat = jnp.concatenate([p.reshape(-1).astype(dtype) for p in parts])
    per = SLAB_COLS * row_multiple
    total = -(-flat.shape[0] // per) * per
    return jnp.pad(flat, (0, total - flat.shape[0])).reshape(total // SLAB_COLS, SLAB_COLS)


def _unpack(flat, shapes):
    out, off = [], 0
    for shp in shapes:
        n = 1
        for d in shp:
            n *= d
        out.append(flat[off:off + n].reshape(shp))
        off += n
    return out


def kernel(x, c, positions, norm_g, w_ada, b_ada, w_in, b_f, q_norm_g, w_uq, kv_norm_g, w_ukv, w_out, final_g, loss_target, m_norm_g, m_w_ada, m_b_ada, m_w_in, m_b_f, m_q_norm_g, m_w_uq, m_kv_norm_g, m_w_ukv, m_w_out, m_final_g, v_norm_g, v_w_ada, v_b_ada, v_w_in, v_b_f, v_q_norm_g, v_w_uq, v_kv_norm_g, v_w_ukv, v_w_out, v_final_g):
    S, D = x.shape[1], x.shape[2]
    L = norm_g.shape[0]
    FH = b_f.shape[1]
    QL, KVL = q_norm_g.shape[1], kv_norm_g.shape[1]
    MH = w_ukv.shape[2] * N_CHIPS // (2 * HEAD_DIM)
    FW, MW = FH * HEAD_DIM, MH * HEAD_DIM
    NA = w_ada.shape[2]
    n_in = w_in.shape[2] * N_CHIPS
    cfg = types.SimpleNamespace(S=S, D=D, FW=FW, MW=MW, QL=QL, KVL=KVL, FH=FH, MH=MH)
    cfg.o_fq, cfg.o_fk, cfg.o_fv, cfg.o_fg, cfg.o_mg = 0, FW, 2 * FW, 3 * FW, 4 * FW
    cfg.o_ql = 4 * FW + MW
    cfg.o_kv = cfg.o_ql + QL
    cfg.o_ms = cfg.o_kv + KVL
    cfg.NZ = cfg.o_ms + LANES
    assert FW == MW and FH % 2 == 0 and MH % 2 == 0 and cfg.o_ql % QL == 0 and cfg.o_kv % KVL == 0 and KVL == LANES
    assert n_in == 4 * FW + FH + QL + KVL + ROPE_DIM + MW

    mx, my, mc = _my_pos()
    my_chip = 2 * mx + my
    my_dev = 2 * my_chip + mc

    inv_freq = 1.0 / (ROPE_THETA ** (jnp.arange(0, ROPE_DIM, 2, dtype=F32) / ROPE_DIM))
    ang = positions[0].astype(F32)[:, None] * inv_freq
    cos, sin = jnp.cos(ang), jnp.sin(ang)
    cfg.cos_t = jnp.concatenate([jnp.ones((S, ROPE_LO), F32), cos, cos, jnp.ones((S, HEAD_PAD - ROPE_LO - ROPE_DIM), F32)], axis=1)
    cfg.sin_t = jnp.concatenate([jnp.zeros((S, ROPE_LO), F32), -sin, sin, jnp.zeros((S, HEAD_PAD - ROPE_LO - ROPE_DIM), F32)], axis=1)

    assert L % 2 == 0 and L > 1
    def lane_pad(a):
        return jnp.pad(a, ((0, 0),) * (a.ndim - 1) + ((0, -a.shape[-1] % LANES),))

    shards = (w_in, w_uq, w_ukv, w_out)
    padded = [lane_pad(w.astype(BF16)) for w in shards]

    def kernel_layouts(gathered, own):
        def all_chips(g, w, axis):
            return jnp.concatenate([jnp.where(my_chip == t, w.astype(BF16), g[t][..., :w.shape[-1]])
                                    for t in range(N_CHIPS)], axis=axis)

        n = own[0].shape[0]
        w_in_f = all_chips(gathered[0], own[0], 2)
        w_uq_f = all_chips(gathered[1], own[1], 2)
        w_ukv_f = all_chips(gathered[2], own[2], 2)
        w_out_f = all_chips(gathered[3], own[3], 1)
        sizes = (FW, FW, FW, FH, FW, QL, KVL, ROPE_DIM, MW)
        offs = [0]
        for sz in sizes:
            offs.append(offs[-1] + sz)
        fq_w, fk_w, fv_w, ff_w, fg_w, ql_w, kvl_w, kr_w, mg_w = [w_in_f[:, :, offs[i]:offs[i + 1]] for i in range(len(sizes))]
        zeros = lambda width: jnp.zeros((n, D, width), BF16)
        w_in_p = jnp.concatenate([fq_w, fk_w, fv_w, fg_w, mg_w, ql_w, kvl_w, ff_w, zeros(ROPE_LO - FH), kr_w,
                                  zeros(HEAD_PAD - ROPE_LO - ROPE_DIM)], axis=2)
        w_uq_p = jnp.pad(w_uq_f.reshape(n, QL, MH, HEAD_DIM + ROPE_DIM),
                         ((0, 0), (0, 0), (0, 0), (0, HEAD_PAD - HEAD_DIM - ROPE_DIM))).reshape(n, QL, MH * HEAD_PAD)
        w_ukv4 = w_ukv_f.reshape(n, KVL, MH, 2 * HEAD_DIM)
        w_uk_p = jnp.pad(w_ukv4[..., :HEAD_DIM], ((0, 0), (0, 0), (0, 0), (0, HEAD_PAD - HEAD_DIM))).reshape(n, KVL, MH * HEAD_PAD)
        return types.SimpleNamespace(w_in=w_in_p, w_uq=w_uq_p, w_uk=w_uk_p, w_v=w_ukv4[..., HEAD_DIM:].reshape(n, KVL, MW),
                                     w_out=w_out_f)

    first = kernel_layouts(weights_gather([p[:1] for p in padded], "weights_first"), [w[:1] for w in shards])

    c_all = allgather8(c.reshape(8, D // 8), "gather_c").reshape(N_DEV, D)
    c_pad = jnp.pad(c_all, ((0, 16 - N_DEV), (0, 0)))
    mod_part = ada_forward(c_pad, w_ada)[:, :N_DEV, :]
    mod_all = allgather8(mod_part.reshape(-1, LANES), "gather_mod").reshape(N_CHIPS, 2, L, N_DEV, NA)[:, 0]
    mod_full = mod_all.transpose(1, 2, 0, 3).reshape(L, N_DEV, N_CHIPS * NA) + b_ada[:, None, :]
    mod_mine = lax.dynamic_index_in_dim(mod_full, my_dev, axis=1, keepdims=True)

    step = make_step(cfg)
    bf_pad = jnp.pad(b_f, ((0, 0), (0, LANES - FH)))
    mods = [(mod_mine[l, :, :D], mod_mine[l, :, D:2 * D], mod_mine[l, :, 2 * D:]) for l in range(L)]

    def layer_params(l, ws_, at):
        return types.SimpleNamespace(l=at, norm_g=norm_g[l][None], bf_pad=bf_pad[l][None], gq=q_norm_g[l][None],
                                     gkv=kv_norm_g[l][None], w_in=ws_.w_in, w_uq=ws_.w_uq, w_uk=ws_.w_uk, w_v=ws_.w_v,
                                     w_out=ws_.w_out)

    layers = [layer_params(0, first, 0)]
    xl, sv, later = step.layer_forward(x[0], layers[0], mods[0], [p[1:] for p in padded])
    saved = [sv]
    rest = kernel_layouts(later, [w[1:] for w in shards])
    for l in range(1, L):
        layers.append(layer_params(l, rest, l - 1))
        xl, sv = step.layer_forward(xl, layers[l], mods[l])
        saved.append(sv)
    dx, acc_fin, loss_part = step.final_loss(xl, final_g[None], loss_target[0])
    loss = lax.psum(loss_part[0, 0], ("x", "y", "c"))
    gl = [None] * L
    for l in reversed(range(L)):
        dx, gl[l] = step.layer_backward(dx, saved[l], layers[l], mods[l])
    grad_x = dx[None]

    stack = lambda name: jnp.stack([getattr(g, name) for g in gl])
    small_parts = [stack("norm_g"), stack("dmod"), stack("b_f"), stack("gq"), stack("gkv"), acc_fin[0]]
    small_shapes = [p.shape for p in small_parts]
    small = _pack_rows(small_parts, F32, 8).reshape(-1, LANES)
    small_all = allgather8(small, "gather_small").reshape(N_DEV, -1, LANES)
    small_sum = sum_leading(small_all, "sum_small")
    g_norm_g, g_b_ada, g_b_f, g_q_norm_g, g_kv_norm_g, g_final_g = _unpack(small_sum.reshape(-1), small_shapes)

    n_ng = L * D
    dmod_all = small_all.reshape(N_DEV, -1)[:, n_ng:n_ng + L * 3 * D].reshape(N_DEV, L, 3 * D)
    dmod_cols = lax.dynamic_slice_in_dim(dmod_all, my_chip * NA, NA, axis=2).transpose(1, 0, 2)
    g_w_ada = ada_backward(c_pad, jnp.pad(dmod_cols, ((0, 0), (0, 16 - N_DEV), (0, 0))))

    ms = cfg.o_ms
    runs = [(0, 3 * FW, 0), (3 * FW, FH, ms), (3 * FW + FH, FW, cfg.o_fg), (4 * FW + FH, QL + KVL, cfg.o_ql),
            (4 * FW + FH + QL + KVL, ROPE_DIM, ms + ROPE_LO), (4 * FW + FH + QL + KVL + ROPE_DIM, MW, cfg.o_mg)]

    def shard_columns(dw, t):
        a, b = t * w_in.shape[2], (t + 1) * w_in.shape[2]
        return jnp.concatenate([dw[:, p0 + max(a, r0) - r0:p0 + min(b, r0 + sz) - r0]
                                for r0, sz, p0 in runs if max(a, r0) < min(b, r0 + sz)], axis=1)

    g_in4 = jnp.stack([jnp.stack([shard_columns(g.w_in, t) for g in gl]) for t in range(N_CHIPS)])
    dw_uq_f = stack("w_uq").reshape(L, QL, MH, HEAD_PAD)[..., :HEAD_DIM + ROPE_DIM].reshape(L, QL, -1)
    dw_ukv_f = jnp.concatenate([stack("w_uk").reshape(L, KVL, MH, HEAD_PAD)[..., :HEAD_DIM],
                                stack("w_v").reshape(L, KVL, MH, HEAD_DIM)], axis=3).reshape(L, KVL, -1)
    dw_out_f = stack("w_out")
    gs = [g_in4, dw_uq_f.reshape(L, QL, N_CHIPS, -1).transpose(2, 0, 1, 3),
          dw_ukv_f.reshape(L, KVL, N_CHIPS, -1).transpose(2, 0, 1, 3), dw_out_f.reshape(L, N_CHIPS, -1, D).transpose(1, 0, 2, 3)]
    gs = [lane_pad(g.astype(BF16)) for g in gs]
    hl = L // 2
    theirs = halves_to_sibling(gs, "grads_sibling")
    chip_part = []
    big_names = ["w_in", "w_uq", "w_ukv", "w_out"]
    for g, o, nm in zip(gs, theirs, big_names):
        keep = lax.dynamic_slice_in_dim(g, mc * hl, hl, axis=1)
        merged = (N_CHIPS * hl,) + g.shape[2:]
        chip_part.append(add_cast(keep.reshape(merged), o.reshape(merged), BF16, "grads_chip_sum_" + nm).reshape(o.shape))
    parts = chip_exchange(chip_part, "grads_chips")
    where = jnp.stack([my_chip, mc]).astype(jnp.int32)
    red = [sum_chips(p, o, where, "grads_sum_" + nm) for p, o, nm in zip(parts, chip_part, big_names)]
    g_w_in, g_w_uq, g_w_ukv, g_w_out = [g[..., :w.shape[-1]] for g, w in zip(halves_gather(red, "grads_back"), shards)]


    names = ["norm_g", "w_ada", "b_ada", "w_in", "b_f", "q_norm_g", "w_uq", "kv_norm_g", "w_ukv", "w_out", "final_g"]
    ws = dict(norm_g=norm_g, w_ada=w_ada, b_ada=b_ada, w_in=w_in, b_f=b_f, q_norm_g=q_norm_g, w_uq=w_uq,
              kv_norm_g=kv_norm_g, w_ukv=w_ukv, w_out=w_out, final_g=final_g)
    msd = dict(norm_g=m_norm_g, w_ada=m_w_ada, b_ada=m_b_ada, w_in=m_w_in, b_f=m_b_f, q_norm_g=m_q_norm_g, w_uq=m_w_uq,
               kv_norm_g=m_kv_norm_g, w_ukv=m_w_ukv, w_out=m_w_out, final_g=m_final_g)
    vsd = dict(norm_g=v_norm_g, w_ada=v_w_ada, b_ada=v_b_ada, w_in=v_w_in, b_f=v_b_f, q_norm_g=v_q_norm_g, w_uq=v_w_uq,
               kv_norm_g=v_kv_norm_g, w_ukv=v_w_ukv, w_out=v_w_out, final_g=v_final_g)
    gsd = dict(norm_g=g_norm_g, w_ada=g_w_ada, b_ada=g_b_ada, w_in=g_w_in, b_f=g_b_f, q_norm_g=g_q_norm_g, w_uq=g_w_uq,
               kv_norm_g=g_kv_norm_g, w_ukv=g_w_ukv, w_out=g_w_out, final_g=g_final_g)
    small_names = ["norm_g", "b_ada", "b_f", "q_norm_g", "kv_norm_g", "final_g"]
    sm_shapes = [ws[n].shape for n in small_names]
    pk = lambda d: _pack_rows([d[n] for n in small_names], F32, 8).reshape(1, -1, LANES)
    sm_out = adamw(pk(ws), pk(gsd), pk(msd), pk(vsd), "adamw_small")
    sm_d, sm_m, sm_v = [dict(zip(small_names, _unpack(o.reshape(-1), sm_shapes))) for o in sm_out]
    delta, new_m, new_v = dict(sm_d), dict(sm_m), dict(sm_v)
    for n in ["w_ada", "w_in", "w_uq", "w_ukv", "w_out"]:
        delta[n], new_m[n], new_v[n] = adamw(ws[n], gsd[n], msd[n], vsd[n], "adamw_" + n)

    return (loss, grad_x, *[gsd[n] for n in names], *[delta[n] for n in names],
            *[new_m[n] for n in names], *[new_v[n] for n in names])
```

```python
import types

import jax
import jax.numpy as jnp
from jax import lax
from jax.experimental import pallas as pl
from jax.experimental.pallas import tpu as pltpu

F32 = jnp.float32
BF16 = jnp.bfloat16
MESH = pl.DeviceIdType.MESH

N_CHIPS = 4
N_DEV = 8
HEAD_DIM = 64
ROPE_DIM = 32
ROPE_THETA = 10000.0
HEAD_PAD = 128
ROPE_LO = 64
ROPE_HALF = 16
CHUNK_SHIFT = 6
LANES = 128
EPS = 1e-6
NEG = -1e30
ADAM_LR = 0.001
ADAM_B1 = 0.9
ADAM_B2 = 0.999
ADAM_EPS = 1e-08
ADAM_WD = 0.01
ADAM_STEP = 10
VMEM_LIMIT = 48 * 1024 * 1024
SLAB_COLS = 1024


def _params(sem=None, vmem=VMEM_LIMIT):
    return pltpu.CompilerParams(dimension_semantics=sem, vmem_limit_bytes=vmem)


def _nn(a, b):
    return jnp.dot(a, b, preferred_element_type=F32)


def _nt(a, b):
    return lax.dot_general(a, b, (((1,), (1,)), ((), ())), preferred_element_type=F32)


def _tn(a, b):
    return lax.dot_general(a, b, (((0,), (0,)), ((), ())), preferred_element_type=F32)


def _sigmoid(x):
    return 1.0 / (1.0 + jnp.exp(-x))


def _lane(shape):
    return lax.broadcasted_iota(jnp.int32, shape, len(shape) - 1)


def _pick(n, cands):
    for c in cands:
        if n % c == 0:
            return c
    return n


def _my_pos():
    return lax.axis_index("x"), lax.axis_index("y"), lax.axis_index("c")


def allgather8(xs, name):
    m_per, n = xs.shape

    def body(x_ref, out_ref, send_sems, recv_sems, local_sem):
        x, y, c = _my_pos()
        me, sibling = (x, y, c), (x, y, 1 - c)
        chips = [(1 - x, y), (x, 1 - y), (1 - x, 1 - y)]

        def rows(px, py, pc):
            return out_ref.at[pl.ds((4 * px + 2 * py + pc) * m_per, m_per), :]

        def copy(k, block, to, src=None):
            return pltpu.make_async_remote_copy(
                src_ref=rows(*block) if src is None else src, dst_ref=rows(*block),
                send_sem=send_sems.at[k], recv_sem=recv_sems.at[k], device_id=to, device_id_type=MESH)

        mine = pltpu.make_async_copy(x_ref, rows(*me), local_sem)
        mine.start()
        first = [copy(0, me, sibling, src=x_ref)]
        first += [copy(1 + j, me, (*chip, c), src=x_ref) for j, chip in enumerate(chips)]
        for cp in first:
            cp.start()
        passed = [copy(4 + j, (*chip, c), sibling) for j, chip in enumerate(chips)]
        for j, chip in enumerate(chips):
            copy(1 + j, (*chip, c), me).wait_recv()
            passed[j].start()
        copy(0, sibling, me).wait_recv()
        for j, chip in enumerate(chips):
            copy(4 + j, (*chip, 1 - c), me).wait_recv()
        for cp in first + passed:
            cp.wait_send()
        mine.wait()

    return pl.pallas_call(
        body, name=name,
        out_shape=jax.ShapeDtypeStruct((N_DEV * m_per, n), xs.dtype),
        in_specs=[pl.BlockSpec(memory_space=pltpu.VMEM)],
        out_specs=pl.BlockSpec(memory_space=pltpu.VMEM),
        scratch_shapes=[pltpu.SemaphoreType.DMA((7,)), pltpu.SemaphoreType.DMA((7,)), pltpu.SemaphoreType.DMA],
    )(xs)


def _remote(src, dst, send_sems, recv_sems, k, to):
    return pltpu.make_async_remote_copy(src_ref=src, dst_ref=dst, send_sem=send_sems.at[k], recv_sem=recv_sems.at[k],
                                        device_id=to, device_id_type=MESH)


def _hbm_call(body, name, ins, out_shapes, n_sems, aliases=None):
    hbm = pl.BlockSpec(memory_space=pl.ANY)
    scratch = [pltpu.SemaphoreType.DMA((n_sems,)), pltpu.SemaphoreType.DMA((n_sems,))]
    return pl.pallas_call(body, name=name, out_shape=out_shapes, in_specs=[hbm] * len(ins),
                          out_specs=[hbm] * len(out_shapes), scratch_shapes=scratch,
                          input_output_aliases=aliases or {})(*ins)


def _layers_half(ref, h, axis=0):
    size = ref.shape[axis] // 2
    idx = (slice(None),) * axis + (pl.ds(h * size, size),)
    return ref.at[idx]


class GatherOverIci:
    def __init__(self, ws):
        self.ins = list(ws)
        self.outs = [jax.ShapeDtypeStruct((N_CHIPS,) + w.shape, w.dtype) for w in ws]
        self.aliases = {}
        self.n_sems = 3 * len(ws)

    def _copies(self, x_refs, o_refs, send_sems, recv_sems):
        x, y, c = _my_pos()
        s = 2 * x + y
        out = []
        for a in range(len(x_refs)):
            for k, (tx, ty) in enumerate([(1 - x, y), (x, 1 - y), (1 - x, 1 - y)]):
                mine = _remote(_layers_half(x_refs[a], c, 1), _layers_half(o_refs[a].at[s], c, 1), send_sems, recv_sems,
                               3 * a + k, (tx, ty, c))
                got = _layers_half(o_refs[a].at[2 * tx + ty], c, 1)
                out.append((mine, _remote(got, got, send_sems, recv_sems, 3 * a + k, (tx, ty, c))))
        return out

    def start(self, x_refs, o_refs, send_sems, recv_sems):
        for mine, _ in self._copies(x_refs, o_refs, send_sems, recv_sems):
            mine.start()

    def wait(self, x_refs, o_refs, send_sems, recv_sems):
        for mine, theirs in self._copies(x_refs, o_refs, send_sems, recv_sems):
            theirs.wait_recv()
            mine.wait_send()


class GatherToSibling:
    def __init__(self, gathered):
        self.ins = list(gathered)
        self.outs = [jax.ShapeDtypeStruct(g.shape, g.dtype) for g in gathered]
        self.aliases = {a: a for a in range(len(gathered))}
        self.n_sems = 3 * len(gathered)

    def _copies(self, o_refs, send_sems, recv_sems):
        x, y, c = _my_pos()
        out = []
        for a in range(len(o_refs)):
            for k, (tx, ty) in enumerate([(1 - x, y), (x, 1 - y), (1 - x, 1 - y)]):
                got = _layers_half(o_refs[a].at[2 * tx + ty], c, 1)
                theirs = _layers_half(o_refs[a].at[2 * tx + ty], 1 - c, 1)
                out.append((_remote(got, got, send_sems, recv_sems, 3 * a + k, (x, y, 1 - c)),
                            _remote(theirs, theirs, send_sems, recv_sems, 3 * a + k, (x, y, 1 - c))))
        return out

    def start(self, x_refs, o_refs, send_sems, recv_sems):
        for mine, _ in self._copies(o_refs, send_sems, recv_sems):
            mine.start()

    def wait(self, x_refs, o_refs, send_sems, recv_sems):
        for mine, theirs in self._copies(o_refs, send_sems, recv_sems):
            theirs.wait_recv()
            mine.wait_send()


def run_side(side, name):
    n_in, n_out = len(side.ins), len(side.outs)

    def body(*refs):
        parts = refs[:n_in], refs[n_in:n_in + n_out], refs[n_in + n_out], refs[n_in + n_out + 1]
        side.start(*parts)
        side.wait(*parts)

    return _hbm_call(body, name, side.ins, side.outs, side.n_sems, aliases=side.aliases)


def weights_gather(ws, name):
    return run_side(GatherToSibling(run_side(GatherOverIci(ws), name + "_ici")), name + "_sibling")


def halves_to_sibling(gs, name):
    n = len(gs)

    def body(*refs):
        x_refs, o_refs, send_sems, recv_sems = refs[:n], refs[n:2 * n], refs[2 * n], refs[2 * n + 1]
        x, y, c = _my_pos()
        cps = [_remote(_layers_half(x_refs[a], 1 - c, axis=1), o_refs[a], send_sems, recv_sems, a, (x, y, 1 - c))
               for a in range(n)]
        for cp in cps:
            cp.start()
        for cp in cps:
            cp.wait()

    outs = [jax.ShapeDtypeStruct((g.shape[0], g.shape[1] // 2) + g.shape[2:], g.dtype) for g in gs]
    return _hbm_call(body, name, gs, outs, n)


class ExchangeOverIci:
    def __init__(self, xs):
        self.ins = list(xs)
        self.outs = [jax.ShapeDtypeStruct(v.shape, v.dtype) for v in xs]
        self.aliases = {}
        self.n_sems = 3 * len(xs)

    def _copies(self, x_refs, o_refs, send_sems, recv_sems):
        x, y, c = _my_pos()
        s = 2 * x + y
        out = []
        for a in range(len(x_refs)):
            for k, (tx, ty) in enumerate([(1 - x, y), (x, 1 - y), (1 - x, 1 - y)]):
                got = o_refs[a].at[2 * tx + ty]
                out.append((_remote(x_refs[a].at[2 * tx + ty], o_refs[a].at[s], send_sems, recv_sems, 3 * a + k, (tx, ty, c)),
                            _remote(got, got, send_sems, recv_sems, 3 * a + k, (tx, ty, c))))
        return out

    def start(self, x_refs, o_refs, send_sems, recv_sems):
        for mine, _ in self._copies(x_refs, o_refs, send_sems, recv_sems):
            mine.start()

    def wait(self, x_refs, o_refs, send_sems, recv_sems):
        for mine, theirs in self._copies(x_refs, o_refs, send_sems, recv_sems):
            theirs.wait_recv()
            mine.wait_send()


def halves_gather(xs, name):
    n = len(xs)

    def body(*refs):
        x_refs, o_refs, send_sems, recv_sems = refs[:n], refs[n:2 * n], refs[2 * n], refs[2 * n + 1]
        x, y, c = _my_pos()
        sends = []
        for a in range(n):
            cp = _remote(_layers_half(x_refs[a], c), _layers_half(o_refs[a], c), send_sems, recv_sems, a, (x, y, 1 - c))
            cp.start()
            sends.append(cp)
        for a in range(n):
            theirs = _layers_half(o_refs[a], 1 - c)
            _remote(theirs, theirs, send_sems, recv_sems, a, (x, y, 1 - c)).wait_recv()
        for cp in sends:
            cp.wait_send()

    outs = [jax.ShapeDtypeStruct(v.shape, v.dtype) for v in xs]
    return _hbm_call(body, name, xs, outs, n, aliases={a: a for a in range(n)})


def sum_leading(xs, name):
    n, r, c = xs.shape

    def body(x_ref, o_ref):
        acc = x_ref[0]
        for i in range(1, n):
            acc = acc + x_ref[i]
        o_ref[...] = acc

    return pl.pallas_call(body, name=name, out_shape=jax.ShapeDtypeStruct((r, c), xs.dtype))(xs)


def add_cast(a, b, out_dtype, name):
    n, r, c = a.shape
    tr = _pick(r, (512, 256, 128, 64, 32, 16))

    def body(a_ref, b_ref, o_ref):
        o_ref[...] = (a_ref[...].astype(F32) + b_ref[...].astype(F32)).astype(out_dtype)

    spec = pl.BlockSpec((1, tr, c), lambda i, j: (i, j, 0))
    return pl.pallas_call(body, name=name, grid=(n, r // tr), in_specs=[spec, spec], out_specs=spec,
                          out_shape=jax.ShapeDtypeStruct((n, r, c), out_dtype),
                          compiler_params=_params(("parallel", "parallel")))(a, b)


def sum_chips(parts, own, where, name):
    n, nl, r, c = parts.shape
    tr = _pick(r, (512, 256, 128, 64, 32, 16))

    def body(w_ref, p_ref, o_ref, out_ref):
        s = w_ref[0]
        acc = jnp.zeros(out_ref.shape, F32)
        for t in range(n):
            acc = acc + jnp.where(s == t, o_ref[0], p_ref[t]).astype(F32)
        out_ref[...] = acc

    grid_spec = pltpu.PrefetchScalarGridSpec(
        num_scalar_prefetch=1, grid=(nl, r // tr),
        in_specs=[pl.BlockSpec((n, 1, tr, c), lambda i, j, w: (0, i, j, 0)),
                  pl.BlockSpec((1, 1, tr, c), lambda i, j, w: (w[0], i, j, 0))],
        out_specs=pl.BlockSpec((1, tr, c), lambda i, j, w: (w[1] * nl + i, j, 0)))
    return pl.pallas_call(body, name=name, grid_spec=grid_spec, out_shape=jax.ShapeDtypeStruct((2 * nl, r, c), F32),
                          compiler_params=_params(("parallel", "parallel")))(where, parts, own)


def ada_forward(c_all, w_ada):
    nl, d, n = w_ada.shape
    nb = c_all.shape[0]

    def body(c_ref, w_ref, o_ref):
        cv = c_ref[...]
        ca = (cv * _sigmoid(cv)).astype(BF16)
        o_ref[0] = _nn(ca, w_ref[0].astype(BF16))

    return pl.pallas_call(
        body, name="ada_forward", grid=(nl,),
        in_specs=[pl.BlockSpec((nb, d), lambda l: (0, 0)), pl.BlockSpec((1, d, n), lambda l: (l, 0, 0))],
        out_specs=pl.BlockSpec((1, nb, n), lambda l: (l, 0, 0)),
        out_shape=jax.ShapeDtypeStruct((nl, nb, n), F32), compiler_params=_params(("parallel",)))(c_all, w_ada)


def ada_backward(c_all, dmod):
    nl, nb, n = dmod.shape
    d = c_all.shape[1]

    def body(c_ref, g_ref, o_ref):
        cv = c_ref[...]
        ca = (cv * _sigmoid(cv)).astype(BF16)
        o_ref[0] = _tn(ca, g_ref[0].astype(BF16))

    return pl.pallas_call(
        body, name="ada_backward", grid=(nl,),
        in_specs=[pl.BlockSpec((nb, d), lambda l: (0, 0)), pl.BlockSpec((1, nb, n), lambda l: (l, 0, 0))],
        out_specs=pl.BlockSpec((1, d, n), lambda l: (l, 0, 0)),
        out_shape=jax.ShapeDtypeStruct((nl, d, n), F32), compiler_params=_params(("parallel",)))(c_all, dmod)


def matmul_tn(a, b, name):
    k, m = a.shape
    n = b.shape[1]
    tm, tk = _pick(m, (512, 256, 128)), _pick(k, (512, 256, 128))
    tn = n if n * (tm * 8 + tk * 4) <= VMEM_LIMIT // 2 else _pick(n, (512, 256, 128))

    def body(a_ref, b_ref, o_ref):
        @pl.when(pl.program_id(2) == 0)
        def _():
            o_ref[...] = jnp.zeros_like(o_ref)

        o_ref[...] += _tn(a_ref[...], b_ref[...])

    return pl.pallas_call(
        body, name=name, grid=(m // tm, n // tn, k // tk),
        in_specs=[pl.BlockSpec((tk, tm), lambda i, j, kk: (kk, i)), pl.BlockSpec((tk, tn), lambda i, j, kk: (kk, j))],
        out_specs=pl.BlockSpec((tm, tn), lambda i, j, kk: (i, j)),
        out_shape=jax.ShapeDtypeStruct((m, n), F32),
        compiler_params=_params(("parallel", "parallel", "arbitrary")))(a, b)


def adamw(w, g, m, v, name):
    nl, r, c = w.shape
    tr = _pick(r, (512, 256, 128, 64, 32, 16, 8))

    def body(w_ref, g_ref, m_ref, v_ref, d_ref, mo_ref, vo_ref):
        gv = g_ref[...]
        mn = ADAM_B1 * m_ref[...] + (1.0 - ADAM_B1) * gv
        vn = ADAM_B2 * v_ref[...] + (1.0 - ADAM_B2) * (gv * gv)
        m_hat = mn / (1.0 - ADAM_B1 ** ADAM_STEP)
        v_hat = vn / (1.0 - ADAM_B2 ** ADAM_STEP)
        d_ref[...] = -ADAM_LR * (m_hat / (jnp.sqrt(v_hat) + ADAM_EPS) + ADAM_WD * w_ref[...])
        mo_ref[...] = mn
        vo_ref[...] = vn

    spec = pl.BlockSpec((1, tr, c), lambda l, i: (l, i, 0))
    out = jax.ShapeDtypeStruct((nl, r, c), F32)
    return pl.pallas_call(body, name=name, grid=(nl, r // tr), in_specs=[spec] * 4, out_specs=[spec] * 3,
                          out_shape=[out] * 3, compiler_params=_params(("parallel", "parallel")))(w, g, m, v)


def _rope(t, cos_t, sin_t):
    w = t.shape[1]
    lane = _lane(t.shape) & (HEAD_PAD - 1)
    first_half = (lane >= ROPE_LO) & (lane < ROPE_LO + ROPE_HALF)
    partner = jnp.where(first_half, pltpu.roll(t, w - ROPE_HALF, 1), pltpu.roll(t, ROPE_HALF, 1))
    return t * cos_t + partner * sin_t


def _rope_t(dt, cos_t, sin_t):
    w = dt.shape[1]
    lane = _lane(dt.shape) & (HEAD_PAD - 1)
    first_half = (lane >= ROPE_LO) & (lane < ROPE_LO + ROPE_HALF)
    ds = dt * sin_t
    partner = jnp.where(first_half, pltpu.roll(ds, w - ROPE_HALF, 1), pltpu.roll(ds, ROPE_HALF, 1))
    return dt * cos_t + partner


def _rms(xv, g):
    rstd = lax.rsqrt(jnp.mean(xv * xv, axis=-1, keepdims=True) + EPS)
    xh = xv * rstd
    return xh * g, xh, rstd


def _rms_bwd(dy, g, xh, rstd):
    dxh = dy * g
    dx = rstd * (dxh - xh * jnp.mean(dxh * xh, axis=-1, keepdims=True))
    return dx, jnp.sum(dy * xh, axis=0, keepdims=True)


def make_step(cfg):
    S, D, NZ = cfg.S, cfg.D, cfg.NZ
    FW, MW, QL, KVL, FH, MH = cfg.FW, cfg.MW, cfg.QL, cfg.KVL, cfg.FH, cfg.MH
    QW = MH * HEAD_PAD
    TM = _pick(S, (512, 256, 128))
    TQ = TK = _pick(S, (256, 128))
    n_tok = S // TM
    ZO = 3 * FW
    NZR = NZ - ZO
    misc_blk = (cfg.o_ms - ZO) // LANES
    FWD_UNROLL = 2
    FOX_SCALE = HEAD_DIM ** -0.5
    MLA_SCALE = (HEAD_DIM + ROPE_DIM) ** -0.5

    def tok(width, col=0):
        return pl.BlockSpec((TM, width), lambda i: (i, col))

    def const(shape):
        return pl.BlockSpec(shape, lambda i: tuple(0 for _ in shape))

    def layer(l, shape):
        return pl.BlockSpec((None,) + shape, lambda i: (l,) + tuple(0 for _ in shape))


    def ln_inproj(x, g, scale, shift, w_in, l):
        def body(x_ref, g_ref, sc_ref, sh_ref, w_ref, h_ref, z_ref, qkv_ref):
            y, _, _ = _rms(x_ref[...], g_ref[...])
            hb = (y * (1.0 + sc_ref[...]) + sh_ref[...]).astype(BF16)
            h_ref[...] = hb
            z = _nn(hb, w_ref[...])
            z_ref[...] = z[:, ZO:]
            qkv_ref[:, :FW] = (z[:, :FW] * FOX_SCALE).astype(BF16)
            qkv_ref[:, FW:] = z[:, FW:ZO].astype(BF16)

        return pl.pallas_call(
            body, name="ln_inproj", grid=(n_tok,),
            in_specs=[tok(D), const((1, D)), const((1, D)), const((1, D)), layer(l, (D, NZ))],
            out_specs=[tok(D), tok(NZR), tok(ZO)],
            out_shape=[jax.ShapeDtypeStruct((S, D), BF16), jax.ShapeDtypeStruct((S, NZR), F32),
                       jax.ShapeDtypeStruct((S, ZO), BF16)],
            compiler_params=_params(("parallel",)))(x, g, scale, shift, w_in)

    def _log_f_terms(misc, bf):
        lane = _lane(misc.shape)
        a = misc + bf
        e = jnp.exp(-jnp.abs(a))
        logf = jnp.minimum(a, 0.0) - jnp.log(1.0 + e)
        sig_neg = jnp.where(a >= 0, e, 1.0) / (1.0 + e)
        valid = lane < FH
        return jnp.where(valid, logf, 0.0), jnp.where(valid, sig_neg, 0.0)

    def fox_prep(z, bf_pad):
        def body(z_ref, b_ref, o_ref, carry):
            @pl.when(pl.program_id(0) == 0)
            def _():
                carry[...] = jnp.zeros_like(carry)

            logf, _ = _log_f_terms(z_ref[...], b_ref[...])
            row = lax.broadcasted_iota(jnp.int32, (TM, TM), 0)
            col = lax.broadcasted_iota(jnp.int32, (TM, TM), 1)
            tri = (col <= row).astype(F32)
            cum = jnp.dot(tri, logf, precision=lax.Precision.HIGHEST, preferred_element_type=F32) + carry[...]
            o_ref[...] = cum
            carry[...] = cum[TM - 1:TM, :]

        return pl.pallas_call(
            body, name="fox_prep", grid=(n_tok,),
            in_specs=[tok(LANES, misc_blk), const((1, LANES))], out_specs=tok(LANES),
            out_shape=jax.ShapeDtypeStruct((S, LANES), F32),
            scratch_shapes=[pltpu.VMEM((1, LANES), F32)],
            compiler_params=_params(("arbitrary",)))(z, bf_pad)

    def mla_prep(z, gq, gkv, w_uq, w_uk, w_v, l, cos_t, sin_t):
        def body(ql_ref, kvl_ref, ms_ref, gq_ref, gkv_ref, wq_ref, wk_ref, wv_ref, c_ref, s_ref, q_ref, k_ref, v_ref):
            cos1, sin1 = c_ref[...], s_ref[...]
            qn, _, _ = _rms(ql_ref[...], gq_ref[...])
            q = _nn(qn.astype(BF16), wq_ref[...])
            q_ref[...] = (_rope(q, jnp.tile(cos1, (1, MH)), jnp.tile(sin1, (1, MH))) * MLA_SCALE).astype(BF16)
            kvn, _, _ = _rms(kvl_ref[...], gkv_ref[...])
            kvb = kvn.astype(BF16)
            lane = _lane((TM, LANES))
            kr = jnp.where((lane >= ROPE_LO) & (lane < ROPE_LO + ROPE_DIM), ms_ref[...], 0.0)
            kr = _rope(kr, cos1, sin1)
            k_ref[...] = (_nn(kvb, wk_ref[...]) + jnp.tile(kr, (1, MH))).astype(BF16)
            v_ref[...] = _nn(kvb, wv_ref[...]).astype(BF16)

        return pl.pallas_call(
            body, name="mla_prep", grid=(n_tok,),
            in_specs=[tok(QL, (cfg.o_ql - ZO) // QL), tok(KVL, (cfg.o_kv - ZO) // KVL), tok(LANES, misc_blk),
                      const((1, QL)), const((1, KVL)), layer(l, (QL, QW)), layer(l, (KVL, QW)), layer(l, (KVL, MW)),
                      tok(LANES), tok(LANES)],
            out_specs=[tok(QW), tok(QW), tok(MW)],
            out_shape=[jax.ShapeDtypeStruct((S, QW), BF16), jax.ShapeDtypeStruct((S, QW), BF16),
                       jax.ShapeDtypeStruct((S, MW), BF16)],
            compiler_params=_params(("parallel",)))(z, z, z, gq, gkv, w_uq, w_uk, w_v, cos_t, sin_t)

    def _allowed(q0, k0, chunked):
        qi = q0 + lax.broadcasted_iota(jnp.int32, (TQ, TK), 0)
        ki = k0 + lax.broadcasted_iota(jnp.int32, (TQ, TK), 1)
        if chunked:
            return (ki >> CHUNK_SHIFT) <= (qi >> CHUNK_SHIFT)
        return ki <= qi

    def _heads(val, packed):
        if packed:
            lane = _lane(val.shape)
            zero = jnp.zeros_like(val)
            return [jnp.where(lane < HEAD_DIM, val, zero), jnp.where(lane >= HEAD_DIM, val, zero)]
        return [val[:, :HEAD_PAD], val[:, HEAD_PAD:]]

    def _merge(a0, a1):
        return jnp.where(_lane(a0.shape) < HEAD_DIM, a0, a1)

    def attn_fwd(q, k, v, q_blk0, k_blk0, v_blk0, cumt, packed, chunked, name, side=None):
        has_bias = cumt is not None
        n_pairs = (FH if packed else MH) // 2
        wq = LANES if packed else 2 * HEAD_PAD
        assert TQ == TK

        n_main = 4 if has_bias else 3
        n_side_in, n_side_out = (len(side.ins), len(side.outs)) if side else (0, 0)

        def body(*refs):
            if has_bias:
                q_ref, k_ref, v_ref, ct_ref = refs[:n_main]
            else:
                q_ref, k_ref, v_ref = refs[:n_main]
            o_ref, lse_ref = refs[n_main + n_side_in:n_main + n_side_in + 2]
            i = pl.program_id(1)
            if side:
                side_refs = (refs[n_main:n_main + n_side_in], refs[n_main + n_side_in + 2:n_main + n_side_in + 2 + n_side_out],
                             refs[-2], refs[-1])

                @pl.when((pl.program_id(0) == 0) & (i == 0))
                def _():
                    side.start(*side_refs)

            q0 = i * TQ
            qh = _heads(q_ref[...], packed)
            q_both = jnp.concatenate(qh, axis=0) if packed else None

            def scores(kb):
                kblk = k_ref[pl.ds(pl.multiple_of(kb * TK, TK), TK), :]
                if packed:
                    return _nt(q_both, kblk)
                return jnp.concatenate([_nt(qh[0], kblk[:, :HEAD_PAD]), _nt(qh[1], kblk[:, HEAD_PAD:])], axis=0)

            def update(kb, s, carry, masked):
                m, l, acc = carry
                k0 = pl.multiple_of(kb * TK, TK)
                s0, s1 = s[:TQ], s[TQ:]
                if has_bias:
                    ck = ct_ref[0, kb]
                    s0, s1 = s0 - ck[0:1, :], s1 - ck[1:2, :]
                if masked:
                    allow = _allowed(q0, k0, chunked)
                    s0, s1 = jnp.where(allow, s0, NEG), jnp.where(allow, s1, NEG)
                s = jnp.concatenate([s0, s1], axis=0)
                m_new = jnp.maximum(m, jnp.max(s, axis=-1, keepdims=True))
                p = jnp.exp(s - m_new)
                alpha = jnp.exp(m - m_new)
                l = alpha * l + jnp.sum(p, axis=-1, keepdims=True)
                acc = alpha * acc + _nn(p.astype(BF16), v_ref[pl.ds(k0, TK), :])
                return m_new, l, acc

            def several(kb, carry, n, last_masked):
                ss = [scores(kb + u) for u in range(n)]
                for u in range(n):
                    carry = update(kb + u, ss[u], carry, last_masked and u == n - 1)
                return carry

            init = (jnp.full((2 * TQ, 1), NEG, F32), jnp.zeros((2 * TQ, 1), F32), jnp.zeros((2 * TQ, LANES), F32))
            carry = lax.fori_loop(0, i // FWD_UNROLL, lambda t, cr: several(FWD_UNROLL * t, cr, FWD_UNROLL, False), init)
            m, l, acc = lax.switch(i % FWD_UNROLL,
                                   [lambda cr, r=r: several(i - r, cr, r + 1, True) for r in range(FWD_UNROLL)], carry)
            o = acc / l
            lse = m + jnp.log(l)
            lane = _lane((TQ, LANES))
            o_ref[...] = _merge(o[:TQ], o[TQ:])
            lse_ref[...] = jnp.where(lane == 0, lse[:TQ], jnp.where(lane == 1, lse[TQ:], 0.0))
            if side:
                @pl.when((pl.program_id(0) == n_pairs - 1) & (i == S // TQ - 1))
                def _():
                    side.wait(*side_refs)

        in_specs = [pl.BlockSpec((TQ, wq), lambda p, i: (i, q_blk0 + p)),
                    pl.BlockSpec((S, wq), lambda p, i: (0, k_blk0 + p)),
                    pl.BlockSpec((S, LANES), lambda p, i: (0, v_blk0 + p))]
        args = [q, k, v]
        if has_bias:
            in_specs += [pl.BlockSpec((1, S // TK, 8, TK), lambda p, i: (p, 0, 0, 0))]
            args += [cumt]
        out_specs = [pl.BlockSpec((TQ, LANES), lambda p, i: (i, p)), pl.BlockSpec((TQ, LANES), lambda p, i: (i, p))]
        out_shape = [jax.ShapeDtypeStruct((S, n_pairs * LANES), F32), jax.ShapeDtypeStruct((S, n_pairs * LANES), F32)]
        extra = {}
        if side:
            hbm = pl.BlockSpec(memory_space=pl.ANY)
            in_specs += [hbm] * n_side_in
            args += side.ins
            out_specs += [hbm] * n_side_out
            out_shape += side.outs
            extra = dict(scratch_shapes=[pltpu.SemaphoreType.DMA((side.n_sems,)), pltpu.SemaphoreType.DMA((side.n_sems,))],
                         input_output_aliases={n_main + a: 2 + b for a, b in side.aliases.items()})
        outs = pl.pallas_call(
            body, name=name, grid=(n_pairs, S // TQ), in_specs=in_specs, out_specs=out_specs, out_shape=out_shape,
            compiler_params=_params(("arbitrary", "arbitrary")), **extra)(*args)
        return (outs[0], outs[1], list(outs[2:])) if side else outs

    def gate_outproj(of, om, z, w_out, l, x, gate):
        def body(of_ref, om_ref, fg_ref, mg_ref, w_ref, x_ref, gt_ref, xn_ref, u_ref, y_ref):
            fg, mg = fg_ref[...], mg_ref[...]
            u = jnp.concatenate([of_ref[...] * fg * _sigmoid(fg), om_ref[...] * mg * _sigmoid(mg)], axis=1).astype(BF16)
            y = _nn(u, w_ref[...])
            u_ref[...] = u
            y_ref[...] = y.astype(BF16)
            xn_ref[...] = x_ref[...] + gt_ref[...] * y

        return pl.pallas_call(
            body, name="gate_outproj", grid=(n_tok,),
            in_specs=[tok(FW), tok(MW), tok(FW, (cfg.o_fg - ZO) // FW), tok(MW, (cfg.o_mg - ZO) // MW),
                      layer(l, (FW + MW, D)), tok(D), const((1, D))],
            out_specs=[tok(D), tok(FW + MW), tok(D)],
            out_shape=[jax.ShapeDtypeStruct((S, D), F32), jax.ShapeDtypeStruct((S, FW + MW), BF16),
                       jax.ShapeDtypeStruct((S, D), BF16)],
            compiler_params=_params(("parallel",)))(of, om, z, z, w_out, x, gate)

    def final_loss(x, g, target):
        def body(x_ref, g_ref, t_ref, dx_ref, acc_ref, loss_ref):
            @pl.when(pl.program_id(0) == 0)
            def _():
                acc_ref[...] = jnp.zeros_like(acc_ref)
                loss_ref[...] = jnp.zeros_like(loss_ref)

            gv = g_ref[...]
            y, xh, rstd = _rms(x_ref[...], gv)
            e = y - t_ref[...]
            loss_ref[...] += 0.5 * jnp.sum(jnp.sum(e * e, axis=-1, keepdims=True) / D, axis=0, keepdims=True)
            dx, dg = _rms_bwd(e / D, gv, xh, rstd)
            dx_ref[...] = dx
            acc_ref[0:1, :] += dg

        return pl.pallas_call(
            body, name="final_loss", grid=(n_tok,),
            in_specs=[tok(D), const((1, D)), tok(D)],
            out_specs=[tok(D), const((8, D)), const((1, LANES))],
            out_shape=[jax.ShapeDtypeStruct((S, D), F32), jax.ShapeDtypeStruct((8, D), F32),
                       jax.ShapeDtypeStruct((1, LANES), F32)],
            compiler_params=_params(("arbitrary",)))(x, g, target)


    def bwd_out(dxn, gate, y, w_out, l, of, om, z, lse_f, lse_m):
        def body(dx_ref, gt_ref, y_ref, w_ref, of_ref, om_ref, fg_ref, mg_ref, lf_ref, lm_ref,
                 dof_ref, dom_ref, dfg_ref, dmg_ref, dy_ref, acc_ref, sf_ref, sm_ref):
            @pl.when(pl.program_id(0) == 0)
            def _():
                acc_ref[...] = jnp.zeros_like(acc_ref)

            dxv = dx_ref[...]
            acc_ref[0:1, :] += jnp.sum(dxv * y_ref[...].astype(F32), axis=0, keepdims=True)
            dy = (gt_ref[...] * dxv).astype(BF16)
            dy_ref[...] = dy
            du = _nt(dy, w_ref[...])
            lane = _lane((TM, LANES))
            for lo, width, o_ref, g_ref, do_ref, dg_ref, l_ref, st_ref in (
                    (0, FW, of_ref, fg_ref, dof_ref, dfg_ref, lf_ref, sf_ref),
                    (FW, MW, om_ref, mg_ref, dom_ref, dmg_ref, lm_ref, sm_ref)):
                gv, ov = g_ref[...], o_ref[...]
                sg = _sigmoid(gv)
                dup = du[:, lo:lo + width]
                dob = (dup * gv * sg).astype(BF16)
                do_ref[...] = dob
                dg_ref[...] = (dup * ov * sg * (1.0 + gv * (1.0 - sg))).astype(BF16)
                d = dob.astype(F32) * ov
                for pr in range(width // LANES):
                    cols = slice(pr * LANES, (pr + 1) * LANES)
                    dp = d[:, cols]
                    d0 = jnp.sum(jnp.where(lane < HEAD_DIM, dp, 0.0), axis=-1, keepdims=True)
                    d1 = jnp.sum(jnp.where(lane >= HEAD_DIM, dp, 0.0), axis=-1, keepdims=True)
                    st_ref[:, cols] = jnp.where(lane == 2, d0, jnp.where(lane == 3, d1, l_ref[:, cols]))

        return pl.pallas_call(
            body, name="bwd_out", grid=(n_tok,),
            in_specs=[tok(D), const((1, D)), tok(D), layer(l, (FW + MW, D)), tok(FW), tok(MW),
                      tok(FW, (cfg.o_fg - ZO) // FW), tok(MW, (cfg.o_mg - ZO) // MW), tok(FW), tok(MW)],
            out_specs=[tok(FW), tok(MW), tok(FW), tok(MW), tok(D), const((8, D)), tok(FW), tok(MW)],
            out_shape=[jax.ShapeDtypeStruct((S, FW), BF16), jax.ShapeDtypeStruct((S, MW), BF16),
                       jax.ShapeDtypeStruct((S, FW), BF16), jax.ShapeDtypeStruct((S, MW), BF16),
                       jax.ShapeDtypeStruct((S, D), BF16), jax.ShapeDtypeStruct((8, D), F32),
                       jax.ShapeDtypeStruct((S, FW), F32), jax.ShapeDtypeStruct((S, MW), F32)],
            compiler_params=_params(("arbitrary",)))(dxn, gate, y, w_out, of, om, z, z, lse_f, lse_m)

    def attn_bwd(q, k, v, do, stats, q_blk0, k_blk0, v_blk0, cumt, packed, chunked, name, side=None):
        has_bias = cumt is not None
        n_pairs = (FH if packed else MH) // 2
        wq = LANES if packed else 2 * HEAD_PAD
        n_q = S // TQ
        assert TQ == TK
        n_in, n_out = (6, 5) if has_bias else (5, 3)
        n_side_in, n_side_out = (len(side.ins), len(side.outs)) if side else (0, 0)

        def body(*refs):
            main_out = refs[n_in + n_side_in:n_in + n_side_in + n_out]
            if has_bias:
                q_ref, k_ref, v_ref, do_ref, st_ref, ct_ref = refs[:n_in]
                dq_ref, dk_ref, dv_ref, dc_ref, dr_ref = main_out
            else:
                q_ref, k_ref, v_ref, do_ref, st_ref = refs[:n_in]
                dq_ref, dk_ref, dv_ref = main_out
            jb = pl.program_id(1)
            k0 = jb * TK
            if side:
                side_refs = (refs[n_in:n_in + n_side_in], refs[n_in + n_side_in + n_out:n_in + n_side_in + n_out + n_side_out],
                             refs[-2], refs[-1])

                @pl.when((pl.program_id(0) == 0) & (jb == 0))
                def _():
                    side.start(*side_refs)

            @pl.when(jb == 0)
            def _():
                dq_ref[...] = jnp.zeros_like(dq_ref)
                if has_bias:
                    dr_ref[...] = jnp.zeros_like(dr_ref)

            dk_ref[...] = jnp.zeros_like(dk_ref)
            dv_ref[...] = jnp.zeros_like(dv_ref)
            kh = _heads(k_ref[...], packed)
            k_both = jnp.concatenate(kh, axis=0) if packed else None
            v_both = jnp.concatenate(_heads(v_ref[...], True), axis=0)
            ck = jnp.concatenate([ct_ref[0, 0][0:1, :], ct_ref[0, 0][1:2, :]], axis=1) if has_bias else None

            def products(ib):
                rows = pl.ds(pl.multiple_of(ib * TQ, TQ), TQ)
                q2, do2 = q_ref[rows, :], do_ref[rows, :]
                if packed:
                    s = _nt(q2, k_both)
                else:
                    qh = _heads(q2, False)
                    s = jnp.concatenate([_nt(qh[0], kh[0]), _nt(qh[1], kh[1])], axis=1)
                return s, _nt(do2, v_both)

            def update(ib, s, dp, carry, masked):
                q0 = pl.multiple_of(ib * TQ, TQ)
                rows = pl.ds(q0, TQ)
                q2, do2, st = q_ref[rows, :], do_ref[rows, :], st_ref[rows, :]
                if not packed:
                    qh = _heads(q2, False)
                if has_bias:
                    s = s - ck
                if masked:
                    allow = _allowed(q0, k0, chunked)
                    s = jnp.where(jnp.concatenate([allow, allow], axis=1), s, NEG)
                p = jnp.concatenate([jnp.exp(s[:, :TK] - st[:, 0:1]), jnp.exp(s[:, TK:] - st[:, 1:2])], axis=1)
                dv2 = _tn(p.astype(BF16), do2)
                ds = jnp.concatenate([p[:, :TK] * (dp[:, :TK] - st[:, 2:3]), p[:, TK:] * (dp[:, TK:] - st[:, 3:4])], axis=1)
                dsb = ds.astype(BF16)
                dv_ref[...] += _merge(dv2[:TK], dv2[TK:])
                if packed:
                    dk2 = _tn(dsb, q2)
                    dk_ref[...] += _merge(dk2[:TK], dk2[TK:])
                    dq_ref[rows, :] += _nn(dsb, k_both)
                else:
                    dk_ref[...] += jnp.concatenate([_tn(dsb[:, :TK], qh[0]), _tn(dsb[:, TK:], qh[1])], axis=1)
                    dq_ref[rows, :] += jnp.concatenate([_nn(dsb[:, :TK], kh[0]), _nn(dsb[:, TK:], kh[1])], axis=1)
                if has_bias:
                    lane = _lane((TQ, LANES))
                    r0 = jnp.sum(ds[:, :TK], axis=-1, keepdims=True)
                    r1 = jnp.sum(ds[:, TK:], axis=-1, keepdims=True)
                    dr_ref[0, rows, :] += jnp.where(lane == 0, r0, jnp.where(lane == 1, r1, 0.0))
                    return carry - jnp.sum(ds, axis=0, keepdims=True)
                return carry

            def step(ib, carry, masked):
                s, dp = products(ib)
                return update(ib, s, dp, carry, masked)

            def two(ib, carry):
                s_a, dp_a = products(ib)
                s_b, dp_b = products(ib + 1)
                return update(ib + 1, s_b, dp_b, update(ib, s_a, dp_a, carry, False), False)

            dc = step(jb, jnp.zeros((1, 2 * TK), F32), True)
            n_rest = n_q - 1 - jb
            dc = lax.fori_loop(0, n_rest // 2, lambda t, cr: two(jb + 1 + 2 * t, cr), dc)
            dc = lax.cond(n_rest % 2 == 1, lambda cr: step(n_q - 1, cr, False), lambda cr: cr, dc)
            if has_bias:
                sub = lax.broadcasted_iota(jnp.int32, (8, TK), 0)
                dc_ref[0, 0] = jnp.where(sub == 0, dc[:, :TK], jnp.where(sub == 1, dc[:, TK:], 0.0))
            if side:
                @pl.when((pl.program_id(0) == n_pairs - 1) & (jb == S // TK - 1))
                def _():
                    side.wait(*side_refs)

        in_specs = [pl.BlockSpec((S, wq), lambda p, j: (0, q_blk0 + p)),
                    pl.BlockSpec((TK, wq), lambda p, j: (j, k_blk0 + p)),
                    pl.BlockSpec((TK, LANES), lambda p, j: (j, v_blk0 + p)),
                    pl.BlockSpec((S, LANES), lambda p, j: (0, p)),
                    pl.BlockSpec((S, LANES), lambda p, j: (0, p))]
        args = [q, k, v, do, stats]
        out_specs = [pl.BlockSpec((S, wq), lambda p, j: (0, p)),
                     pl.BlockSpec((TK, wq), lambda p, j: (j, p)),
                     pl.BlockSpec((TK, LANES), lambda p, j: (j, p))]
        out_shape = [jax.ShapeDtypeStruct((S, n_pairs * wq), F32), jax.ShapeDtypeStruct((S, n_pairs * wq), F32),
                     jax.ShapeDtypeStruct((S, n_pairs * LANES), F32)]
        if has_bias:
            in_specs += [pl.BlockSpec((1, 1, 8, TK), lambda p, j: (p, j, 0, 0))]
            args += [cumt]
            out_specs += [pl.BlockSpec((1, 1, 8, TK), lambda p, j: (p, j, 0, 0)),
                          pl.BlockSpec((1, S, LANES), lambda p, j: (p, 0, 0))]
            out_shape += [jax.ShapeDtypeStruct((n_pairs, S // TK, 8, TK), F32),
                          jax.ShapeDtypeStruct((n_pairs, S, LANES), F32)]
        extra = {}
        if side:
            hbm = pl.BlockSpec(memory_space=pl.ANY)
            in_specs += [hbm] * n_side_in
            args += side.ins
            out_specs += [hbm] * n_side_out
            out_shape += side.outs
            extra = dict(scratch_shapes=[pltpu.SemaphoreType.DMA((side.n_sems,)), pltpu.SemaphoreType.DMA((side.n_sems,))],
                         input_output_aliases={n_in + a: n_out + b for a, b in side.aliases.items()})
        outs = pl.pallas_call(
            body, name=name, grid=(n_pairs, S // TK), in_specs=in_specs, out_specs=out_specs, out_shape=out_shape,
            compiler_params=_params(("arbitrary", "arbitrary")), **extra)(*args)
        return (*outs[:n_out], list(outs[n_out:])) if side else outs

    def fox_post(dcum, z, bf_pad):
        def rev(i):
            return n_tok - 1 - i

        def body(dc_ref, z_ref, b_ref, dff_ref, acc_ref, carry):
            @pl.when(pl.program_id(0) == 0)
            def _():
                carry[...] = jnp.zeros_like(carry)
                acc_ref[...] = jnp.zeros_like(acc_ref)

            _, sig_neg = _log_f_terms(z_ref[...], b_ref[...])
            row = lax.broadcasted_iota(jnp.int32, (TM, TM), 0)
            col = lax.broadcasted_iota(jnp.int32, (TM, TM), 1)
            tri = (col >= row).astype(F32)
            dlog = jnp.dot(tri, dc_ref[...], precision=lax.Precision.HIGHEST, preferred_element_type=F32) + carry[...]
            carry[...] = dlog[0:1, :]
            dff = dlog * sig_neg
            dff_ref[...] = dff
            acc_ref[0:1, :] += jnp.sum(dff, axis=0, keepdims=True)

        return pl.pallas_call(
            body, name="fox_post", grid=(n_tok,),
            in_specs=[pl.BlockSpec((TM, LANES), lambda i: (rev(i), 0)),
                      pl.BlockSpec((TM, LANES), lambda i: (rev(i), misc_blk)), const((1, LANES))],
            out_specs=[pl.BlockSpec((TM, LANES), lambda i: (rev(i), 0)), const((8, LANES))],
            out_shape=[jax.ShapeDtypeStruct((S, LANES), F32), jax.ShapeDtypeStruct((8, LANES), F32)],
            scratch_shapes=[pltpu.VMEM((1, LANES), F32)],
            compiler_params=_params(("arbitrary",)))(dcum, z, bf_pad)

    def mla_post(dq, dk, dv, dff, z, gq, gkv, w_uq, w_uk, w_v, l, cos_t, sin_t):
        def body(dq_ref, dk_ref, dv_ref, dff_ref, ql_ref, kvl_ref, gq_ref, gkv_ref, wq_ref, wk_ref, wv_ref,
                 c_ref, s_ref, zq_ref, zkv_ref, zms_ref, dwq_ref, dwk_ref, dwv_ref, dgq_ref, dgkv_ref):
            @pl.when(pl.program_id(0) == 0)
            def _():
                for r in (dwq_ref, dwk_ref, dwv_ref, dgq_ref, dgkv_ref):
                    r[...] = jnp.zeros_like(r)

            cos1, sin1 = c_ref[...], s_ref[...]
            gqv, gkvv = gq_ref[...], gkv_ref[...]
            qn, qxh, qrstd = _rms(ql_ref[...], gqv)
            dq_pre = _rope_t(dq_ref[...] * MLA_SCALE, jnp.tile(cos1, (1, MH)), jnp.tile(sin1, (1, MH))).astype(BF16)
            dwq_ref[...] += _tn(qn.astype(BF16), dq_pre)
            dql, dgq = _rms_bwd(_nt(dq_pre, wq_ref[...]), gqv, qxh, qrstd)
            zq_ref[...] = dql.astype(BF16)
            dgq_ref[0:1, :] += dgq

            dkv = dk_ref[...]
            lane = _lane(dkv.shape) & (HEAD_PAD - 1)
            dkn = jnp.where(lane < HEAD_DIM, dkv, 0.0).astype(BF16)
            dkr = dkv[:, 0:HEAD_PAD]
            for hd in range(1, MH):
                dkr = dkr + dkv[:, hd * HEAD_PAD:(hd + 1) * HEAD_PAD]
            lane1 = _lane(dkr.shape)
            dkr = jnp.where((lane1 >= ROPE_LO) & (lane1 < ROPE_LO + ROPE_DIM), dkr, 0.0)
            dkr = _rope_t(dkr, cos1, sin1)
            zms_ref[...] = (dkr + dff_ref[...]).astype(BF16)

            kvn, kxh, krstd = _rms(kvl_ref[...], gkvv)
            kvb = kvn.astype(BF16)
            dvb = dv_ref[...].astype(BF16)
            dwk_ref[...] += _tn(kvb, dkn)
            dwv_ref[...] += _tn(kvb, dvb)
            dkvl, dgkv = _rms_bwd(_nt(dkn, wk_ref[...]) + _nt(dvb, wv_ref[...]), gkvv, kxh, krstd)
            zkv_ref[...] = dkvl.astype(BF16)
            dgkv_ref[0:1, :] += dgkv

        return pl.pallas_call(
            body, name="mla_post", grid=(n_tok,),
            in_specs=[tok(QW), tok(QW), tok(MW), tok(LANES), tok(QL, (cfg.o_ql - ZO) // QL), tok(KVL, (cfg.o_kv - ZO) // KVL),
                      const((1, QL)), const((1, KVL)), layer(l, (QL, QW)), layer(l, (KVL, QW)), layer(l, (KVL, MW)),
                      tok(LANES), tok(LANES)],
            out_specs=[tok(QL), tok(KVL), tok(LANES), const((QL, QW)), const((KVL, QW)), const((KVL, MW)),
                       const((8, QL)), const((8, KVL))],
            out_shape=[jax.ShapeDtypeStruct((S, QL), BF16), jax.ShapeDtypeStruct((S, KVL), BF16),
                       jax.ShapeDtypeStruct((S, LANES), BF16), jax.ShapeDtypeStruct((QL, QW), F32),
                       jax.ShapeDtypeStruct((KVL, QW), F32), jax.ShapeDtypeStruct((KVL, MW), F32),
                       jax.ShapeDtypeStruct((8, QL), F32), jax.ShapeDtypeStruct((8, KVL), F32)],
            compiler_params=_params(("arbitrary",)))(dq, dk, dv, dff, z, z, gq, gkv, w_uq, w_uk, w_v, cos_t, sin_t)

    def bwd_in(dz, w_in, l, x, dxn, g, scale):
        def body(dz_ref, w_ref, x_ref, dx_ref, g_ref, sc_ref, o_ref, acc_ref):
            @pl.when(pl.program_id(0) == 0)
            def _():
                acc_ref[...] = jnp.zeros_like(acc_ref)

            dh = _nt(dz_ref[...], w_ref[...])
            gv, mod = g_ref[...], 1.0 + sc_ref[...]
            _, xh, rstd = _rms(x_ref[...], gv)
            t = dh * xh
            acc_ref[0:1, :] += jnp.sum(dh, axis=0, keepdims=True)
            acc_ref[1:2, :] += jnp.sum(t * gv, axis=0, keepdims=True)
            acc_ref[2:3, :] += jnp.sum(t * mod, axis=0, keepdims=True)
            dx, _ = _rms_bwd(dh, gv * mod, xh, rstd)
            o_ref[...] = dx_ref[...] + dx

        return pl.pallas_call(
            body, name="bwd_in", grid=(n_tok,),
            in_specs=[tok(NZ), layer(l, (D, NZ)), tok(D), tok(D), const((1, D)), const((1, D))],
            out_specs=[tok(D), const((8, D))],
            out_shape=[jax.ShapeDtypeStruct((S, D), F32), jax.ShapeDtypeStruct((8, D), F32)],
            compiler_params=_params(("arbitrary",)))(dz, w_in, x, dxn, g, scale)


    def pair_rows(cum):
        n_pairs = FH // 2
        ct = jnp.pad(cum[:, :FH].T.reshape(n_pairs, 2, S), ((0, 0), (0, 6), (0, 0)))
        return ct.reshape(n_pairs, 8, S // TK, TK).transpose(0, 2, 1, 3)

    def bias_grad(dc, dr):
        n_pairs = FH // 2
        d = dc.transpose(0, 2, 1, 3).reshape(n_pairs, 8, S)[:, :2, :].reshape(FH, S).T
        d = d + dr[:, :, :2].transpose(1, 0, 2).reshape(S, FH)
        return jnp.pad(d, ((0, 0), (0, LANES - FH)))

    def layer_forward(x, wl, mod, later_shards=None):
        shift, scale, gate = mod
        h, z, qkv = ln_inproj(x, wl.norm_g, scale, shift, wl.w_in, wl.l)
        ct = pair_rows(fox_prep(z, wl.bf_pad))
        fox = attn_fwd(qkv, qkv, qkv, cfg.o_fq // LANES, cfg.o_fk // LANES, cfg.o_fv // LANES, ct, True, False,
                       "fox_fwd_gather" if later_shards else "fox_fwd", GatherOverIci(later_shards) if later_shards else None)
        of, lse_f = fox[0], fox[1]
        qp, kp, vp = mla_prep(z, wl.gq, wl.gkv, wl.w_uq, wl.w_uk, wl.w_v, wl.l, cfg.cos_t, cfg.sin_t)
        mla = attn_fwd(qp, kp, vp, 0, 0, 0, None, False, True, "mla_fwd_gather" if later_shards else "mla_fwd",
                       GatherToSibling(fox[2]) if later_shards else None)
        om, lse_m = mla[0], mla[1]
        xn, u, y = gate_outproj(of, om, z, wl.w_out, wl.l, x, gate)
        saved = types.SimpleNamespace(x=x, h=h, z=z, qkv=qkv, ct=ct, of=of, lse_f=lse_f, qp=qp, kp=kp, vp=vp,
                                      om=om, lse_m=lse_m, u=u, y=y)
        return (xn, saved, mla[2]) if later_shards else (xn, saved)

    def layer_backward(dxn, sv, wl, mod, side=None):
        shift, scale, gate = mod
        do_f, do_m, dfg, dmg, dy, acc_o, st_f, st_m = bwd_out(dxn, gate, sv.y, wl.w_out, wl.l, sv.of, sv.om, sv.z,
                                                              sv.lse_f, sv.lse_m)
        dw_out = matmul_tn(sv.u, dy, "dw_out")
        dfq, dfk, dfv, dck, dcr, *rode = attn_bwd(sv.qkv, sv.qkv, sv.qkv, do_f, st_f, cfg.o_fq // LANES, cfg.o_fk // LANES,
                                                  cfg.o_fv // LANES, sv.ct, True, False,
                                                  "fox_bwd_exchange" if side else "fox_bwd", side)
        dff, acc_f = fox_post(bias_grad(dck, dcr), sv.z, wl.bf_pad)
        dqp, dkp, dvp = attn_bwd(sv.qp, sv.kp, sv.vp, do_m, st_m, 0, 0, 0, None, False, True, "mla_bwd")
        zq, zkv, zms, dw_uq, dw_uk, dw_v, dgq, dgkv = mla_post(
            dqp, dkp, dvp, dff, sv.z, wl.gq, wl.gkv, wl.w_uq, wl.w_uk, wl.w_v, wl.l, cfg.cos_t, cfg.sin_t)
        dz = jnp.concatenate([(dfq * FOX_SCALE).astype(BF16), dfk.astype(BF16), dfv.astype(BF16), dfg, dmg, zq, zkv, zms], axis=1)
        dx, acc_i = bwd_in(dz, wl.w_in, wl.l, sv.x, dxn, wl.norm_g, scale)
        dw_in = matmul_tn(sv.h, dz, "dw_in")
        grads = types.SimpleNamespace(
            w_in=dw_in, w_out=dw_out, w_uq=dw_uq, w_uk=dw_uk, w_v=dw_v, gq=dgq[0], gkv=dgkv[0],
            b_f=acc_f[0, :FH], norm_g=acc_i[2], dmod=jnp.concatenate([acc_i[0], acc_i[1], acc_o[0]]))
        return (dx, grads, rode[0]) if side else (dx, grads)

    return types.SimpleNamespace(layer_forward=layer_forward, layer_backward=layer_backward, final_loss=final_loss)


def _pack_rows(parts, dtype, row_multiple):
    flat = jnp.concatenate([p.reshape(-1).astype(dtype) for p in parts])
    per = SLAB_COLS * row_multiple
    total = -(-flat.shape[0] // per) * per
    return jnp.pad(flat, (0, total - flat.shape[0])).reshape(total // SLAB_COLS, SLAB_COLS)


def _unpack(flat, shapes):
    out, off = [], 0
    for shp in shapes:
        n = 1
        for d in shp:
            n *= d
        out.append(flat[off:off + n].reshape(shp))
        off += n
    return out


def kernel(x, c, positions, norm_g, w_ada, b_ada, w_in, b_f, q_norm_g, w_uq, kv_norm_g, w_ukv, w_out, final_g, loss_target, m_norm_g, m_w_ada, m_b_ada, m_w_in, m_b_f, m_q_norm_g, m_w_uq, m_kv_norm_g, m_w_ukv, m_w_out, m_final_g, v_norm_g, v_w_ada, v_b_ada, v_w_in, v_b_f, v_q_norm_g, v_w_uq, v_kv_norm_g, v_w_ukv, v_w_out, v_final_g):
    S, D = x.shape[1], x.shape[2]
    L = norm_g.shape[0]
    FH = b_f.shape[1]
    QL, KVL = q_norm_g.shape[1], kv_norm_g.shape[1]
    MH = w_ukv.shape[2] * N_CHIPS // (2 * HEAD_DIM)
    FW, MW = FH * HEAD_DIM, MH * HEAD_DIM
    NA = w_ada.shape[2]
    n_in = w_in.shape[2] * N_CHIPS
    cfg = types.SimpleNamespace(S=S, D=D, FW=FW, MW=MW, QL=QL, KVL=KVL, FH=FH, MH=MH)
    cfg.o_fq, cfg.o_fk, cfg.o_fv, cfg.o_fg, cfg.o_mg = 0, FW, 2 * FW, 3 * FW, 4 * FW
    cfg.o_ql = 4 * FW + MW
    cfg.o_kv = cfg.o_ql + QL
    cfg.o_ms = cfg.o_kv + KVL
    cfg.NZ = cfg.o_ms + LANES
    assert FW == MW and FH % 2 == 0 and MH % 2 == 0 and cfg.o_ql % QL == 0 and cfg.o_kv % KVL == 0 and KVL == LANES
    assert n_in == 4 * FW + FH + QL + KVL + ROPE_DIM + MW

    mx, my, mc = _my_pos()
    my_chip = 2 * mx + my
    my_dev = 2 * my_chip + mc

    inv_freq = 1.0 / (ROPE_THETA ** (jnp.arange(0, ROPE_DIM, 2, dtype=F32) / ROPE_DIM))
    ang = positions[0].astype(F32)[:, None] * inv_freq
    cos, sin = jnp.cos(ang), jnp.sin(ang)
    cfg.cos_t = jnp.concatenate([jnp.ones((S, ROPE_LO), F32), cos, cos, jnp.ones((S, HEAD_PAD - ROPE_LO - ROPE_DIM), F32)], axis=1)
    cfg.sin_t = jnp.concatenate([jnp.zeros((S, ROPE_LO), F32), -sin, sin, jnp.zeros((S, HEAD_PAD - ROPE_LO - ROPE_DIM), F32)], axis=1)

    assert L % 4 == 0
    def lane_pad(a):
        return jnp.pad(a, ((0, 0),) * (a.ndim - 1) + ((0, -a.shape[-1] % LANES),))

    shards = (w_in, w_uq, w_ukv, w_out)
    padded = [lane_pad(w.astype(BF16)) for w in shards]

    def kernel_layouts(gathered, own):
        def all_chips(g, w, axis):
            return jnp.concatenate([jnp.where(my_chip == t, w.astype(BF16), g[t][..., :w.shape[-1]])
                                    for t in range(N_CHIPS)], axis=axis)

        n = own[0].shape[0]
        w_in_f = all_chips(gathered[0], own[0], 2)
        w_uq_f = all_chips(gathered[1], own[1], 2)
        w_ukv_f = all_chips(gathered[2], own[2], 2)
        w_out_f = all_chips(gathered[3], own[3], 1)
        sizes = (FW, FW, FW, FH, FW, QL, KVL, ROPE_DIM, MW)
        offs = [0]
        for sz in sizes:
            offs.append(offs[-1] + sz)
        fq_w, fk_w, fv_w, ff_w, fg_w, ql_w, kvl_w, kr_w, mg_w = [w_in_f[:, :, offs[i]:offs[i + 1]] for i in range(len(sizes))]
        zeros = lambda width: jnp.zeros((n, D, width), BF16)
        w_in_p = jnp.concatenate([fq_w, fk_w, fv_w, fg_w, mg_w, ql_w, kvl_w, ff_w, zeros(ROPE_LO - FH), kr_w,
                                  zeros(HEAD_PAD - ROPE_LO - ROPE_DIM)], axis=2)
        w_uq_p = jnp.pad(w_uq_f.reshape(n, QL, MH, HEAD_DIM + ROPE_DIM),
                         ((0, 0), (0, 0), (0, 0), (0, HEAD_PAD - HEAD_DIM - ROPE_DIM))).reshape(n, QL, MH * HEAD_PAD)
        w_ukv4 = w_ukv_f.reshape(n, KVL, MH, 2 * HEAD_DIM)
        w_uk_p = jnp.pad(w_ukv4[..., :HEAD_DIM], ((0, 0), (0, 0), (0, 0), (0, HEAD_PAD - HEAD_DIM))).reshape(n, KVL, MH * HEAD_PAD)
        return types.SimpleNamespace(w_in=w_in_p, w_uq=w_uq_p, w_uk=w_uk_p, w_v=w_ukv4[..., HEAD_DIM:].reshape(n, KVL, MW),
                                     w_out=w_out_f)

    first = kernel_layouts(weights_gather([p[:1] for p in padded], "weights_first"), [w[:1] for w in shards])

    c_all = allgather8(c.reshape(8, D // 8), "gather_c").reshape(N_DEV, D)
    c_pad = jnp.pad(c_all, ((0, 16 - N_DEV), (0, 0)))
    mod_part = ada_forward(c_pad, w_ada)[:, :N_DEV, :]
    mod_all = allgather8(mod_part.reshape(-1, LANES), "gather_mod").reshape(N_CHIPS, 2, L, N_DEV, NA)[:, 0]
    mod_full = mod_all.transpose(1, 2, 0, 3).reshape(L, N_DEV, N_CHIPS * NA) + b_ada[:, None, :]
    mod_mine = lax.dynamic_index_in_dim(mod_full, my_dev, axis=1, keepdims=True)

    step = make_step(cfg)
    bf_pad = jnp.pad(b_f, ((0, 0), (0, LANES - FH)))
    mods = [(mod_mine[l, :, :D], mod_mine[l, :, D:2 * D], mod_mine[l, :, 2 * D:]) for l in range(L)]

    def layer_params(l, ws_, at):
        return types.SimpleNamespace(l=at, norm_g=norm_g[l][None], bf_pad=bf_pad[l][None], gq=q_norm_g[l][None],
                                     gkv=kv_norm_g[l][None], w_in=ws_.w_in, w_uq=ws_.w_uq, w_uk=ws_.w_uk, w_v=ws_.w_v,
                                     w_out=ws_.w_out)

    layers = [layer_params(0, first, 0)]
    xl, sv, later = step.layer_forward(x[0], layers[0], mods[0], [p[1:] for p in padded])
    saved = [sv]
    rest = kernel_layouts(later, [w[1:] for w in shards])
    for l in range(1, L):
        layers.append(layer_params(l, rest, l - 1))
        xl, sv = step.layer_forward(xl, layers[l], mods[l])
        saved.append(sv)
    dx, acc_fin, loss_part = step.final_loss(xl, final_g[None], loss_target[0])
    loss = lax.psum(loss_part[0, 0], ("x", "y", "c"))
    ms = cfg.o_ms
    runs = [(0, 3 * FW, 0), (3 * FW, FH, ms), (3 * FW + FH, FW, cfg.o_fg), (4 * FW + FH, QL + KVL, cfg.o_ql),
            (4 * FW + FH + QL + KVL, ROPE_DIM, ms + ROPE_LO), (4 * FW + FH + QL + KVL + ROPE_DIM, MW, cfg.o_mg)]
    big_names = ["w_in", "w_uq", "w_ukv", "w_out"]

    def shard_columns(dw, t):
        a, b = t * w_in.shape[2], (t + 1) * w_in.shape[2]
        return jnp.concatenate([dw[:, p0 + max(a, r0) - r0:p0 + min(b, r0 + sz) - r0]
                                for r0, sz, p0 in runs if max(a, r0) < min(b, r0 + sz)], axis=1)

    def chip_parts(group, tag):
        n = len(group)
        stk = lambda name: jnp.stack([getattr(g, name) for g in group])
        g_in4 = jnp.stack([jnp.stack([shard_columns(g.w_in, t) for g in group]) for t in range(N_CHIPS)])
        dw_uq_f = stk("w_uq").reshape(n, QL, MH, HEAD_PAD)[..., :HEAD_DIM + ROPE_DIM].reshape(n, QL, -1)
        dw_ukv_f = jnp.concatenate([stk("w_uk").reshape(n, KVL, MH, HEAD_PAD)[..., :HEAD_DIM],
                                    stk("w_v").reshape(n, KVL, MH, HEAD_DIM)], axis=3).reshape(n, KVL, -1)
        gs = [g_in4, dw_uq_f.reshape(n, QL, N_CHIPS, -1).transpose(2, 0, 1, 3),
              dw_ukv_f.reshape(n, KVL, N_CHIPS, -1).transpose(2, 0, 1, 3), stk("w_out").reshape(n, N_CHIPS, -1, D).transpose(1, 0, 2, 3)]
        gs = [lane_pad(g.astype(BF16)) for g in gs]
        theirs = halves_to_sibling(gs, "grads_sibling_" + tag)
        out = []
        for g, o, nm in zip(gs, theirs, big_names):
            keep = lax.dynamic_slice_in_dim(g, mc * (n // 2), n // 2, axis=1)
            merged = (N_CHIPS * (n // 2),) + g.shape[2:]
            out.append(add_cast(keep.reshape(merged), o.reshape(merged), BF16, "grads_chip_sum_%s_%s" % (nm, tag)).reshape(o.shape))
        return out

    def reduced(chip_part, parts, tag):
        where = jnp.stack([my_chip, mc]).astype(jnp.int32)
        red = [sum_chips(p, o, where, "grads_sum_%s_%s" % (nm, tag)) for p, o, nm in zip(parts, chip_part, big_names)]
        return [g[..., :w.shape[-1]] for g, w in zip(halves_gather(red, "grads_back_" + tag), shards)]

    gl = [None] * L
    half_l = L // 2
    for l in range(L - 1, half_l - 1, -1):
        dx, gl[l] = step.layer_backward(dx, saved[l], layers[l], mods[l])
    part_hi = chip_parts(gl[half_l:], "hi")
    dx, gl[half_l - 1], parts_hi = step.layer_backward(dx, saved[half_l - 1], layers[half_l - 1], mods[half_l - 1],
                                                       ExchangeOverIci(part_hi))
    for l in range(half_l - 2, -1, -1):
        dx, gl[l] = step.layer_backward(dx, saved[l], layers[l], mods[l])
    part_lo = chip_parts(gl[:half_l], "lo")
    parts_lo = run_side(ExchangeOverIci(part_lo), "grads_chips_lo")
    g_w_in, g_w_uq, g_w_ukv, g_w_out = [jnp.concatenate([lo, hi]) for lo, hi in
                                        zip(reduced(part_lo, parts_lo, "lo"), reduced(part_hi, parts_hi, "hi"))]
    grad_x = dx[None]

    stack = lambda name: jnp.stack([getattr(g, name) for g in gl])
    small_parts = [stack("norm_g"), stack("dmod"), stack("b_f"), stack("gq"), stack("gkv"), acc_fin[0]]
    small_shapes = [p.shape for p in small_parts]
    small = _pack_rows(small_parts, F32, 8).reshape(-1, LANES)
    small_all = allgather8(small, "gather_small").reshape(N_DEV, -1, LANES)
    small_sum = sum_leading(small_all, "sum_small")
    g_norm_g, g_b_ada, g_b_f, g_q_norm_g, g_kv_norm_g, g_final_g = _unpack(small_sum.reshape(-1), small_shapes)

    n_ng = L * D
    dmod_all = small_all.reshape(N_DEV, -1)[:, n_ng:n_ng + L * 3 * D].reshape(N_DEV, L, 3 * D)
    dmod_cols = lax.dynamic_slice_in_dim(dmod_all, my_chip * NA, NA, axis=2).transpose(1, 0, 2)
    g_w_ada = ada_backward(c_pad, jnp.pad(dmod_cols, ((0, 0), (0, 16 - N_DEV), (0, 0))))


    names = ["norm_g", "w_ada", "b_ada", "w_in", "b_f", "q_norm_g", "w_uq", "kv_norm_g", "w_ukv", "w_out", "final_g"]
    ws = dict(norm_g=norm_g, w_ada=w_ada, b_ada=b_ada, w_in=w_in, b_f=b_f, q_norm_g=q_norm_g, w_uq=w_uq,
              kv_norm_g=kv_norm_g, w_ukv=w_ukv, w_out=w_out, final_g=final_g)
    msd = dict(norm_g=m_norm_g, w_ada=m_w_ada, b_ada=m_b_ada, w_in=m_w_in, b_f=m_b_f, q_norm_g=m_q_norm_g, w_uq=m_w_uq,
               kv_norm_g=m_kv_norm_g, w_ukv=m_w_ukv, w_out=m_w_out, final_g=m_final_g)
    vsd = dict(norm_g=v_norm_g, w_ada=v_w_ada, b_ada=v_b_ada, w_in=v_w_in, b_f=v_b_f, q_norm_g=v_q_norm_g, w_uq=v_w_uq,
               kv_norm_g=v_kv_norm_g, w_ukv=v_w_ukv, w_out=v_w_out, final_g=v_final_g)
    gsd = dict(norm_g=g_norm_g, w_ada=g_w_ada, b_ada=g_b_ada, w_in=g_w_in, b_f=g_b_f, q_norm_g=g_q_norm_g, w_uq=g_w_uq,
               kv_norm_g=g_kv_norm_g, w_ukv=g_w_ukv, w_out=g_w_out, final_g=g_final_g)
    small_names = ["norm_g", "b_ada", "b_f", "q_norm_g", "kv_norm_g", "final_g"]
    sm_shapes = [ws[n].shape for n in small_names]
    pk = lambda d: _pack_rows([d[n] for n in small_names], F32, 8).reshape(1, -1, LANES)
    sm_out = adamw(pk(ws), pk(gsd), pk(msd), pk(vsd), "adamw_small")
    sm_d, sm_m, sm_v = [dict(zip(small_names, _unpack(o.reshape(-1), sm_shapes))) for o in sm_out]
    delta, new_m, new_v = dict(sm_d), dict(sm_m), dict(sm_v)
    for n in ["w_ada", "w_in", "w_uq", "w_ukv", "w_out"]:
        delta[n], new_m[n], new_v[n] = adamw(ws[n], gsd[n], msd[n], vsd[n], "adamw_" + n)

    return (loss, grad_x, *[gsd[n] for n in names], *[delta[n] for n in names],
            *[new_m[n] for n in names], *[new_v[n] for n in names])
```

```python
import types

import jax
import jax.numpy as jnp
from jax import lax
from jax.experimental import pallas as pl
from jax.experimental.pallas import tpu as pltpu

F32 = jnp.float32
BF16 = jnp.bfloat16
MESH = pl.DeviceIdType.MESH

N_CHIPS = 4
N_DEV = 8
HEAD_DIM = 64
ROPE_DIM = 32
ROPE_THETA = 10000.0
HEAD_PAD = 128
ROPE_LO = 64
ROPE_HALF = 16
CHUNK_SHIFT = 6
LANES = 128
EPS = 1e-6
NEG = -1e30
ADAM_LR = 0.001
ADAM_B1 = 0.9
ADAM_B2 = 0.999
ADAM_EPS = 1e-08
ADAM_WD = 0.01
ADAM_STEP = 10
VMEM_LIMIT = 48 * 1024 * 1024
SLAB_COLS = 1024


def _params(sem=None, vmem=VMEM_LIMIT):
    return pltpu.CompilerParams(dimension_semantics=sem, vmem_limit_bytes=vmem)


def _nn(a, b):
    return jnp.dot(a, b, preferred_element_type=F32)


def _nt(a, b):
    return lax.dot_general(a, b, (((1,), (1,)), ((), ())), preferred_element_type=F32)


def _tn(a, b):
    return lax.dot_general(a, b, (((0,), (0,)), ((), ())), preferred_element_type=F32)


def _sigmoid(x):
    return 1.0 / (1.0 + jnp.exp(-x))


def _lane(shape):
    return lax.broadcasted_iota(jnp.int32, shape, len(shape) - 1)


def _pick(n, cands):
    for c in cands:
        if n % c == 0:
            return c
    return n


def _my_pos():
    return lax.axis_index("x"), lax.axis_index("y"), lax.axis_index("c")


def allgather8(xs, name):
    m_per, n = xs.shape

    def body(x_ref, out_ref, send_sems, recv_sems, local_sem):
        x, y, c = _my_pos()
        me, sibling = (x, y, c), (x, y, 1 - c)
        chips = [(1 - x, y), (x, 1 - y), (1 - x, 1 - y)]

        def rows(px, py, pc):
            return out_ref.at[pl.ds((4 * px + 2 * py + pc) * m_per, m_per), :]

        def copy(k, block, to, src=None):
            return pltpu.make_async_remote_copy(
                src_ref=rows(*block) if src is None else src, dst_ref=rows(*block),
                send_sem=send_sems.at[k], recv_sem=recv_sems.at[k], device_id=to, device_id_type=MESH)

        mine = pltpu.make_async_copy(x_ref, rows(*me), local_sem)
        mine.start()
        first = [copy(0, me, sibling, src=x_ref)]
        first += [copy(1 + j, me, (*chip, c), src=x_ref) for j, chip in enumerate(chips)]
        for cp in first:
            cp.start()
        passed = [copy(4 + j, (*chip, c), sibling) for j, chip in enumerate(chips)]
        for j, chip in enumerate(chips):
            copy(1 + j, (*chip, c), me).wait_recv()
            passed[j].start()
        copy(0, sibling, me).wait_recv()
        for j, chip in enumerate(chips):
            copy(4 + j, (*chip, 1 - c), me).wait_recv()
        for cp in first + passed:
            cp.wait_send()
        mine.wait()

    return pl.pallas_call(
        body, name=name,
        out_shape=jax.ShapeDtypeStruct((N_DEV * m_per, n), xs.dtype),
        in_specs=[pl.BlockSpec(memory_space=pltpu.VMEM)],
        out_specs=pl.BlockSpec(memory_space=pltpu.VMEM),
        scratch_shapes=[pltpu.SemaphoreType.DMA((7,)), pltpu.SemaphoreType.DMA((7,)), pltpu.SemaphoreType.DMA],
    )(xs)


def _remote(src, dst, send_sems, recv_sems, k, to):
    return pltpu.make_async_remote_copy(src_ref=src, dst_ref=dst, send_sem=send_sems.at[k], recv_sem=recv_sems.at[k],
                                        device_id=to, device_id_type=MESH)


def _hbm_call(body, name, ins, out_shapes, n_sems, aliases=None):
    hbm = pl.BlockSpec(memory_space=pl.ANY)
    scratch = [pltpu.SemaphoreType.DMA((n_sems,)), pltpu.SemaphoreType.DMA((n_sems,))]
    return pl.pallas_call(body, name=name, out_shape=out_shapes, in_specs=[hbm] * len(ins),
                          out_specs=[hbm] * len(out_shapes), scratch_shapes=scratch,
                          input_output_aliases=aliases or {})(*ins)


def _layers_half(ref, h, axis=0):
    size = ref.shape[axis] // 2
    idx = (slice(None),) * axis + (pl.ds(h * size, size),)
    return ref.at[idx]


class GatherOverIci:
    def __init__(self, ws):
        self.ins = list(ws)
        self.outs = [jax.ShapeDtypeStruct((N_CHIPS,) + w.shape, w.dtype) for w in ws]
        self.aliases = {}
        self.n_sems = 3 * len(ws)

    def _copies(self, x_refs, o_refs, send_sems, recv_sems):
        x, y, c = _my_pos()
        s = 2 * x + y
        out = []
        for a in range(len(x_refs)):
            for k, (tx, ty) in enumerate([(1 - x, y), (x, 1 - y), (1 - x, 1 - y)]):
                mine = _remote(_layers_half(x_refs[a], c, 1), _layers_half(o_refs[a].at[s], c, 1), send_sems, recv_sems,
                               3 * a + k, (tx, ty, c))
                got = _layers_half(o_refs[a].at[2 * tx + ty], c, 1)
                out.append((mine, _remote(got, got, send_sems, recv_sems, 3 * a + k, (tx, ty, c))))
        return out

    def start(self, x_refs, o_refs, send_sems, recv_sems):
        for mine, _ in self._copies(x_refs, o_refs, send_sems, recv_sems):
            mine.start()

    def wait(self, x_refs, o_refs, send_sems, recv_sems):
        for mine, theirs in self._copies(x_refs, o_refs, send_sems, recv_sems):
            theirs.wait_recv()
            mine.wait_send()


class GatherToSibling:
    def __init__(self, gathered):
        self.ins = list(gathered)
        self.outs = [jax.ShapeDtypeStruct(g.shape, g.dtype) for g in gathered]
        self.aliases = {a: a for a in range(len(gathered))}
        self.n_sems = 3 * len(gathered)

    def _copies(self, o_refs, send_sems, recv_sems):
        x, y, c = _my_pos()
        out = []
        for a in range(len(o_refs)):
            for k, (tx, ty) in enumerate([(1 - x, y), (x, 1 - y), (1 - x, 1 - y)]):
                got = _layers_half(o_refs[a].at[2 * tx + ty], c, 1)
                theirs = _layers_half(o_refs[a].at[2 * tx + ty], 1 - c, 1)
                out.append((_remote(got, got, send_sems, recv_sems, 3 * a + k, (x, y, 1 - c)),
                            _remote(theirs, theirs, send_sems, recv_sems, 3 * a + k, (x, y, 1 - c))))
        return out

    def start(self, x_refs, o_refs, send_sems, recv_sems):
        for mine, _ in self._copies(o_refs, send_sems, recv_sems):
            mine.start()

    def wait(self, x_refs, o_refs, send_sems, recv_sems):
        for mine, theirs in self._copies(o_refs, send_sems, recv_sems):
            theirs.wait_recv()
            mine.wait_send()


def run_side(side, name):
    n_in, n_out = len(side.ins), len(side.outs)

    def body(*refs):
        parts = refs[:n_in], refs[n_in:n_in + n_out], refs[n_in + n_out], refs[n_in + n_out + 1]
        side.start(*parts)
        side.wait(*parts)

    return _hbm_call(body, name, side.ins, side.outs, side.n_sems, aliases=side.aliases)


def weights_gather(ws, name):
    return run_side(GatherToSibling(run_side(GatherOverIci(ws), name + "_ici")), name + "_sibling")


def halves_to_sibling(gs, name):
    n = len(gs)

    def body(*refs):
        x_refs, o_refs, send_sems, recv_sems = refs[:n], refs[n:2 * n], refs[2 * n], refs[2 * n + 1]
        x, y, c = _my_pos()
        cps = [_remote(_layers_half(x_refs[a], 1 - c, axis=1), o_refs[a], send_sems, recv_sems, a, (x, y, 1 - c))
               for a in range(n)]
        for cp in cps:
            cp.start()
        for cp in cps:
            cp.wait()

    outs = [jax.ShapeDtypeStruct((g.shape[0], g.shape[1] // 2) + g.shape[2:], g.dtype) for g in gs]
    return _hbm_call(body, name, gs, outs, n)


class ExchangeOverIci:
    def __init__(self, xs):
        self.ins = list(xs)
        self.outs = [jax.ShapeDtypeStruct(v.shape, v.dtype) for v in xs]
        self.aliases = {}
        self.n_sems = 3 * len(xs)

    def _copies(self, x_refs, o_refs, send_sems, recv_sems):
        x, y, c = _my_pos()
        s = 2 * x + y
        out = []
        for a in range(len(x_refs)):
            for k, (tx, ty) in enumerate([(1 - x, y), (x, 1 - y), (1 - x, 1 - y)]):
                got = o_refs[a].at[2 * tx + ty]
                out.append((_remote(x_refs[a].at[2 * tx + ty], o_refs[a].at[s], send_sems, recv_sems, 3 * a + k, (tx, ty, c)),
                            _remote(got, got, send_sems, recv_sems, 3 * a + k, (tx, ty, c))))
        return out

    def start(self, x_refs, o_refs, send_sems, recv_sems):
        for mine, _ in self._copies(x_refs, o_refs, send_sems, recv_sems):
            mine.start()

    def wait(self, x_refs, o_refs, send_sems, recv_sems):
        for mine, theirs in self._copies(x_refs, o_refs, send_sems, recv_sems):
            theirs.wait_recv()
            mine.wait_send()


def halves_gather(xs, name):
    n = len(xs)

    def body(*refs):
        x_refs, o_refs, send_sems, recv_sems = refs[:n], refs[n:2 * n], refs[2 * n], refs[2 * n + 1]
        x, y, c = _my_pos()
        sends = []
        for a in range(n):
            cp = _remote(_layers_half(x_refs[a], c), _layers_half(o_refs[a], c), send_sems, recv_sems, a, (x, y, 1 - c))
            cp.start()
            sends.append(cp)
        for a in range(n):
            theirs = _layers_half(o_refs[a], 1 - c)
            _remote(theirs, theirs, send_sems, recv_sems, a, (x, y, 1 - c)).wait_recv()
        for cp in sends:
            cp.wait_send()

    outs = [jax.ShapeDtypeStruct(v.shape, v.dtype) for v in xs]
    return _hbm_call(body, name, xs, outs, n, aliases={a: a for a in range(n)})


def sum_leading(xs, name):
    n, r, c = xs.shape

    def body(x_ref, o_ref):
        acc = x_ref[0]
        for i in range(1, n):
            acc = acc + x_ref[i]
        o_ref[...] = acc

    return pl.pallas_call(body, name=name, out_shape=jax.ShapeDtypeStruct((r, c), xs.dtype))(xs)


def add_cast(a, b, out_dtype, name):
    n, r, c = a.shape
    tr = _pick(r, (512, 256, 128, 64, 32, 16))

    def body(a_ref, b_ref, o_ref):
        o_ref[...] = (a_ref[...].astype(F32) + b_ref[...].astype(F32)).astype(out_dtype)

    spec = pl.BlockSpec((1, tr, c), lambda i, j: (i, j, 0))
    return pl.pallas_call(body, name=name, grid=(n, r // tr), in_specs=[spec, spec], out_specs=spec,
                          out_shape=jax.ShapeDtypeStruct((n, r, c), out_dtype),
                          compiler_params=_params(("parallel", "parallel")))(a, b)


def sum_chips(parts, own, where, name):
    n, nl, r, c = parts.shape
    tr = _pick(r, (512, 256, 128, 64, 32, 16))

    def body(w_ref, p_ref, o_ref, out_ref):
        s = w_ref[0]
        acc = jnp.zeros(out_ref.shape, F32)
        for t in range(n):
            acc = acc + jnp.where(s == t, o_ref[0], p_ref[t]).astype(F32)
        out_ref[...] = acc

    grid_spec = pltpu.PrefetchScalarGridSpec(
        num_scalar_prefetch=1, grid=(nl, r // tr),
        in_specs=[pl.BlockSpec((n, 1, tr, c), lambda i, j, w: (0, i, j, 0)),
                  pl.BlockSpec((1, 1, tr, c), lambda i, j, w: (w[0], i, j, 0))],
        out_specs=pl.BlockSpec((1, tr, c), lambda i, j, w: (w[1] * nl + i, j, 0)))
    return pl.pallas_call(body, name=name, grid_spec=grid_spec, out_shape=jax.ShapeDtypeStruct((2 * nl, r, c), F32),
                          compiler_params=_params(("parallel", "parallel")))(where, parts, own)


def ada_forward(c_all, w_ada):
    nl, d, n = w_ada.shape
    nb = c_all.shape[0]

    def body(c_ref, w_ref, o_ref):
        cv = c_ref[...]
        ca = (cv * _sigmoid(cv)).astype(BF16)
        o_ref[0] = _nn(ca, w_ref[0].astype(BF16))

    return pl.pallas_call(
        body, name="ada_forward", grid=(nl,),
        in_specs=[pl.BlockSpec((nb, d), lambda l: (0, 0)), pl.BlockSpec((1, d, n), lambda l: (l, 0, 0))],
        out_specs=pl.BlockSpec((1, nb, n), lambda l: (l, 0, 0)),
        out_shape=jax.ShapeDtypeStruct((nl, nb, n), F32), compiler_params=_params(("parallel",)))(c_all, w_ada)


def ada_backward(c_all, dmod):
    nl, nb, n = dmod.shape
    d = c_all.shape[1]

    def body(c_ref, g_ref, o_ref):
        cv = c_ref[...]
        ca = (cv * _sigmoid(cv)).astype(BF16)
        o_ref[0] = _tn(ca, g_ref[0].astype(BF16))

    return pl.pallas_call(
        body, name="ada_backward", grid=(nl,),
        in_specs=[pl.BlockSpec((nb, d), lambda l: (0, 0)), pl.BlockSpec((1, nb, n), lambda l: (l, 0, 0))],
        out_specs=pl.BlockSpec((1, d, n), lambda l: (l, 0, 0)),
        out_shape=jax.ShapeDtypeStruct((nl, d, n), F32), compiler_params=_params(("parallel",)))(c_all, dmod)


def matmul_tn(a, b, name):
    k, m = a.shape
    n = b.shape[1]
    tm, tk = _pick(m, (512, 256, 128)), _pick(k, (512, 256, 128))
    tn = n if n * (tm * 8 + tk * 4) <= VMEM_LIMIT // 2 else _pick(n, (512, 256, 128))

    def body(a_ref, b_ref, o_ref):
        @pl.when(pl.program_id(2) == 0)
        def _():
            o_ref[...] = jnp.zeros_like(o_ref)

        o_ref[...] += _tn(a_ref[...], b_ref[...])

    return pl.pallas_call(
        body, name=name, grid=(m // tm, n // tn, k // tk),
        in_specs=[pl.BlockSpec((tk, tm), lambda i, j, kk: (kk, i)), pl.BlockSpec((tk, tn), lambda i, j, kk: (kk, j))],
        out_specs=pl.BlockSpec((tm, tn), lambda i, j, kk: (i, j)),
        out_shape=jax.ShapeDtypeStruct((m, n), F32),
        compiler_params=_params(("parallel", "parallel", "arbitrary")))(a, b)


def adamw(w, g, m, v, name):
    nl, r, c = w.shape
    tr = _pick(r, (512, 256, 128, 64, 32, 16, 8))

    def body(w_ref, g_ref, m_ref, v_ref, d_ref, mo_ref, vo_ref):
        gv = g_ref[...]
        mn = ADAM_B1 * m_ref[...] + (1.0 - ADAM_B1) * gv
        vn = ADAM_B2 * v_ref[...] + (1.0 - ADAM_B2) * (gv * gv)
        m_hat = mn / (1.0 - ADAM_B1 ** ADAM_STEP)
        v_hat = vn / (1.0 - ADAM_B2 ** ADAM_STEP)
        d_ref[...] = -ADAM_LR * (m_hat / (jnp.sqrt(v_hat) + ADAM_EPS) + ADAM_WD * w_ref[...])
        mo_ref[...] = mn
        vo_ref[...] = vn

    spec = pl.BlockSpec((1, tr, c), lambda l, i: (l, i, 0))
    out = jax.ShapeDtypeStruct((nl, r, c), F32)
    return pl.pallas_call(body, name=name, grid=(nl, r // tr), in_specs=[spec] * 4, out_specs=[spec] * 3,
                          out_shape=[out] * 3, compiler_params=_params(("parallel", "parallel")))(w, g, m, v)


def _rope(t, cos_t, sin_t):
    w = t.shape[1]
    lane = _lane(t.shape) & (HEAD_PAD - 1)
    first_half = (lane >= ROPE_LO) & (lane < ROPE_LO + ROPE_HALF)
    partner = jnp.where(first_half, pltpu.roll(t, w - ROPE_HALF, 1), pltpu.roll(t, ROPE_HALF, 1))
    return t * cos_t + partner * sin_t


def _rope_t(dt, cos_t, sin_t):
    w = dt.shape[1]
    lane = _lane(dt.shape) & (HEAD_PAD - 1)
    first_half = (lane >= ROPE_LO) & (lane < ROPE_LO + ROPE_HALF)
    ds = dt * sin_t
    partner = jnp.where(first_half, pltpu.roll(ds, w - ROPE_HALF, 1), pltpu.roll(ds, ROPE_HALF, 1))
    return dt * cos_t + partner


def _rms(xv, g):
    rstd = lax.rsqrt(jnp.mean(xv * xv, axis=-1, keepdims=True) + EPS)
    xh = xv * rstd
    return xh * g, xh, rstd


def _rms_bwd(dy, g, xh, rstd):
    dxh = dy * g
    dx = rstd * (dxh - xh * jnp.mean(dxh * xh, axis=-1, keepdims=True))
    return dx, jnp.sum(dy * xh, axis=0, keepdims=True)


def make_step(cfg):
    S, D, NZ = cfg.S, cfg.D, cfg.NZ
    FW, MW, QL, KVL, FH, MH = cfg.FW, cfg.MW, cfg.QL, cfg.KVL, cfg.FH, cfg.MH
    QW = MH * HEAD_PAD
    TM = _pick(S, (512, 256, 128))
    TQ = TK = _pick(S, (256, 128))
    n_tok = S // TM
    ZO = 3 * FW
    NZR = NZ - ZO
    misc_blk = (cfg.o_ms - ZO) // LANES
    FWD_UNROLL = 2
    FOX_SCALE = HEAD_DIM ** -0.5
    MLA_SCALE = (HEAD_DIM + ROPE_DIM) ** -0.5

    def tok(width, col=0):
        return pl.BlockSpec((TM, width), lambda i: (i, col))

    def const(shape):
        return pl.BlockSpec(shape, lambda i: tuple(0 for _ in shape))

    def layer(l, shape):
        return pl.BlockSpec((None,) + shape, lambda i: (l,) + tuple(0 for _ in shape))


    def ln_inproj(x, g, scale, shift, w_in, l):
        def body(x_ref, g_ref, sc_ref, sh_ref, w_ref, h_ref, z_ref, qkv_ref):
            y, _, _ = _rms(x_ref[...], g_ref[...])
            hb = (y * (1.0 + sc_ref[...]) + sh_ref[...]).astype(BF16)
            h_ref[...] = hb
            z = _nn(hb, w_ref[...])
            z_ref[...] = z[:, ZO:]
            qkv_ref[:, :FW] = (z[:, :FW] * FOX_SCALE).astype(BF16)
            qkv_ref[:, FW:] = z[:, FW:ZO].astype(BF16)

        return pl.pallas_call(
            body, name="ln_inproj", grid=(n_tok,),
            in_specs=[tok(D), const((1, D)), const((1, D)), const((1, D)), layer(l, (D, NZ))],
            out_specs=[tok(D), tok(NZR), tok(ZO)],
            out_shape=[jax.ShapeDtypeStruct((S, D), BF16), jax.ShapeDtypeStruct((S, NZR), F32),
                       jax.ShapeDtypeStruct((S, ZO), BF16)],
            compiler_params=_params(("parallel",)))(x, g, scale, shift, w_in)

    def _log_f_terms(misc, bf):
        lane = _lane(misc.shape)
        a = misc + bf
        e = jnp.exp(-jnp.abs(a))
        logf = jnp.minimum(a, 0.0) - jnp.log(1.0 + e)
        sig_neg = jnp.where(a >= 0, e, 1.0) / (1.0 + e)
        valid = lane < FH
        return jnp.where(valid, logf, 0.0), jnp.where(valid, sig_neg, 0.0)

    def fox_prep(z, bf_pad):
        def body(z_ref, b_ref, o_ref, carry):
            @pl.when(pl.program_id(0) == 0)
            def _():
                carry[...] = jnp.zeros_like(carry)

            logf, _ = _log_f_terms(z_ref[...], b_ref[...])
            row = lax.broadcasted_iota(jnp.int32, (TM, TM), 0)
            col = lax.broadcasted_iota(jnp.int32, (TM, TM), 1)
            tri = (col <= row).astype(F32)
            cum = jnp.dot(tri, logf, precision=lax.Precision.HIGHEST, preferred_element_type=F32) + carry[...]
            o_ref[...] = cum
            carry[...] = cum[TM - 1:TM, :]

        return pl.pallas_call(
            body, name="fox_prep", grid=(n_tok,),
            in_specs=[tok(LANES, misc_blk), const((1, LANES))], out_specs=tok(LANES),
            out_shape=jax.ShapeDtypeStruct((S, LANES), F32),
            scratch_shapes=[pltpu.VMEM((1, LANES), F32)],
            compiler_params=_params(("arbitrary",)))(z, bf_pad)

    def mla_prep(z, gq, gkv, w_uq, w_uk, w_v, l, cos_t, sin_t):
        def body(ql_ref, kvl_ref, ms_ref, gq_ref, gkv_ref, wq_ref, wk_ref, wv_ref, c_ref, s_ref, q_ref, k_ref, v_ref):
            cos1, sin1 = c_ref[...], s_ref[...]
            qn, _, _ = _rms(ql_ref[...], gq_ref[...])
            q = _nn(qn.astype(BF16), wq_ref[...])
            q_ref[...] = (_rope(q, jnp.tile(cos1, (1, MH)), jnp.tile(sin1, (1, MH))) * MLA_SCALE).astype(BF16)
            kvn, _, _ = _rms(kvl_ref[...], gkv_ref[...])
            kvb = kvn.astype(BF16)
            lane = _lane((TM, LANES))
            kr = jnp.where((lane >= ROPE_LO) & (lane < ROPE_LO + ROPE_DIM), ms_ref[...], 0.0)
            kr = _rope(kr, cos1, sin1)
            k_ref[...] = (_nn(kvb, wk_ref[...]) + jnp.tile(kr, (1, MH))).astype(BF16)
            v_ref[...] = _nn(kvb, wv_ref[...]).astype(BF16)

        return pl.pallas_call(
            body, name="mla_prep", grid=(n_tok,),
            in_specs=[tok(QL, (cfg.o_ql - ZO) // QL), tok(KVL, (cfg.o_kv - ZO) // KVL), tok(LANES, misc_blk),
                      const((1, QL)), const((1, KVL)), layer(l, (QL, QW)), layer(l, (KVL, QW)), layer(l, (KVL, MW)),
                      tok(LANES), tok(LANES)],
            out_specs=[tok(QW), tok(QW), tok(MW)],
            out_shape=[jax.ShapeDtypeStruct((S, QW), BF16), jax.ShapeDtypeStruct((S, QW), BF16),
                       jax.ShapeDtypeStruct((S, MW), BF16)],
            compiler_params=_params(("parallel",)))(z, z, z, gq, gkv, w_uq, w_uk, w_v, cos_t, sin_t)

    def _allowed(q0, k0, chunked):
        qi = q0 + lax.broadcasted_iota(jnp.int32, (TQ, TK), 0)
        ki = k0 + lax.broadcasted_iota(jnp.int32, (TQ, TK), 1)
        if chunked:
            return (ki >> CHUNK_SHIFT) <= (qi >> CHUNK_SHIFT)
        return ki <= qi

    def _heads(val, packed):
        if packed:
            lane = _lane(val.shape)
            zero = jnp.zeros_like(val)
            return [jnp.where(lane < HEAD_DIM, val, zero), jnp.where(lane >= HEAD_DIM, val, zero)]
        return [val[:, :HEAD_PAD], val[:, HEAD_PAD:]]

    def _merge(a0, a1):
        return jnp.where(_lane(a0.shape) < HEAD_DIM, a0, a1)

    def attn_fwd(q, k, v, q_blk0, k_blk0, v_blk0, cumt, packed, chunked, name, side=None):
        has_bias = cumt is not None
        n_pairs = (FH if packed else MH) // 2
        wq = LANES if packed else 2 * HEAD_PAD
        assert TQ == TK

        n_main = 4 if has_bias else 3
        n_side_in, n_side_out = (len(side.ins), len(side.outs)) if side else (0, 0)

        def body(*refs):
            if has_bias:
                q_ref, k_ref, v_ref, ct_ref = refs[:n_main]
            else:
                q_ref, k_ref, v_ref = refs[:n_main]
            o_ref, lse_ref = refs[n_main + n_side_in:n_main + n_side_in + 2]
            i = pl.program_id(1)
            if side:
                side_refs = (refs[n_main:n_main + n_side_in], refs[n_main + n_side_in + 2:n_main + n_side_in + 2 + n_side_out],
                             refs[-2], refs[-1])

                @pl.when((pl.program_id(0) == 0) & (i == 0))
                def _():
                    side.start(*side_refs)

            q0 = i * TQ
            qh = _heads(q_ref[...], packed)
            q_both = jnp.concatenate(qh, axis=0) if packed else None

            def scores(kb):
                kblk = k_ref[pl.ds(pl.multiple_of(kb * TK, TK), TK), :]
                if packed:
                    return _nt(q_both, kblk)
                return jnp.concatenate([_nt(qh[0], kblk[:, :HEAD_PAD]), _nt(qh[1], kblk[:, HEAD_PAD:])], axis=0)

            def update(kb, s, carry, masked):
                m, l, acc = carry
                k0 = pl.multiple_of(kb * TK, TK)
                s0, s1 = s[:TQ], s[TQ:]
                if has_bias:
                    ck = ct_ref[0, kb]
                    s0, s1 = s0 - ck[0:1, :], s1 - ck[1:2, :]
                if masked:
                    allow = _allowed(q0, k0, chunked)
                    s0, s1 = jnp.where(allow, s0, NEG), jnp.where(allow, s1, NEG)
                s = jnp.concatenate([s0, s1], axis=0)
                m_new = jnp.maximum(m, jnp.max(s, axis=-1, keepdims=True))
                p = jnp.exp(s - m_new)
                alpha = jnp.exp(m - m_new)
                l = alpha * l + jnp.sum(p, axis=-1, keepdims=True)
                acc = alpha * acc + _nn(p.astype(BF16), v_ref[pl.ds(k0, TK), :])
                return m_new, l, acc

            def several(kb, carry, n, last_masked):
                ss = [scores(kb + u) for u in range(n)]
                for u in range(n):
                    carry = update(kb + u, ss[u], carry, last_masked and u == n - 1)
                return carry

            init = (jnp.full((2 * TQ, 1), NEG, F32), jnp.zeros((2 * TQ, 1), F32), jnp.zeros((2 * TQ, LANES), F32))
            carry = lax.fori_loop(0, i // FWD_UNROLL, lambda t, cr: several(FWD_UNROLL * t, cr, FWD_UNROLL, False), init)
            m, l, acc = lax.switch(i % FWD_UNROLL,
                                   [lambda cr, r=r: several(i - r, cr, r + 1, True) for r in range(FWD_UNROLL)], carry)
            o = acc / l
            lse = m + jnp.log(l)
            lane = _lane((TQ, LANES))
            o_ref[...] = _merge(o[:TQ], o[TQ:])
            lse_ref[...] = jnp.where(lane == 0, lse[:TQ], jnp.where(lane == 1, lse[TQ:], 0.0))
            if side:
                @pl.when((pl.program_id(0) == n_pairs - 1) & (i == S // TQ - 1))
                def _():
                    side.wait(*side_refs)

        in_specs = [pl.BlockSpec((TQ, wq), lambda p, i: (i, q_blk0 + p)),
                    pl.BlockSpec((S, wq), lambda p, i: (0, k_blk0 + p)),
                    pl.BlockSpec((S, LANES), lambda p, i: (0, v_blk0 + p))]
        args = [q, k, v]
        if has_bias:
            in_specs += [pl.BlockSpec((1, S // TK, 8, TK), lambda p, i: (p, 0, 0, 0))]
            args += [cumt]
        out_specs = [pl.BlockSpec((TQ, LANES), lambda p, i: (i, p)), pl.BlockSpec((TQ, LANES), lambda p, i: (i, p))]
        out_shape = [jax.ShapeDtypeStruct((S, n_pairs * LANES), F32), jax.ShapeDtypeStruct((S, n_pairs * LANES), F32)]
        extra = {}
        if side:
            hbm = pl.BlockSpec(memory_space=pl.ANY)
            in_specs += [hbm] * n_side_in
            args += side.ins
            out_specs += [hbm] * n_side_out
            out_shape += side.outs
            extra = dict(scratch_shapes=[pltpu.SemaphoreType.DMA((side.n_sems,)), pltpu.SemaphoreType.DMA((side.n_sems,))],
                         input_output_aliases={n_main + a: 2 + b for a, b in side.aliases.items()})
        outs = pl.pallas_call(
            body, name=name, grid=(n_pairs, S // TQ), in_specs=in_specs, out_specs=out_specs, out_shape=out_shape,
            compiler_params=_params(("arbitrary", "arbitrary")), **extra)(*args)
        return (outs[0], outs[1], list(outs[2:])) if side else outs

    def gate_outproj(of, om, z, w_out, l, x, gate):
        def body(of_ref, om_ref, fg_ref, mg_ref, w_ref, x_ref, gt_ref, xn_ref, u_ref, y_ref):
            fg, mg = fg_ref[...], mg_ref[...]
            u = jnp.concatenate([of_ref[...] * fg * _sigmoid(fg), om_ref[...] * mg * _sigmoid(mg)], axis=1).astype(BF16)
            y = _nn(u, w_ref[...])
            u_ref[...] = u
            y_ref[...] = y.astype(BF16)
            xn_ref[...] = x_ref[...] + gt_ref[...] * y

        return pl.pallas_call(
            body, name="gate_outproj", grid=(n_tok,),
            in_specs=[tok(FW), tok(MW), tok(FW, (cfg.o_fg - ZO) // FW), tok(MW, (cfg.o_mg - ZO) // MW),
                      layer(l, (FW + MW, D)), tok(D), const((1, D))],
            out_specs=[tok(D), tok(FW + MW), tok(D)],
            out_shape=[jax.ShapeDtypeStruct((S, D), F32), jax.ShapeDtypeStruct((S, FW + MW), BF16),
                       jax.ShapeDtypeStruct((S, D), BF16)],
            compiler_params=_params(("parallel",)))(of, om, z, z, w_out, x, gate)

    def final_loss(x, g, target):
        def body(x_ref, g_ref, t_ref, dx_ref, acc_ref, loss_ref):
            @pl.when(pl.program_id(0) == 0)
            def _():
                acc_ref[...] = jnp.zeros_like(acc_ref)
                loss_ref[...] = jnp.zeros_like(loss_ref)

            gv = g_ref[...]
            y, xh, rstd = _rms(x_ref[...], gv)
            e = y - t_ref[...]
            loss_ref[...] += 0.5 * jnp.sum(jnp.sum(e * e, axis=-1, keepdims=True) / D, axis=0, keepdims=True)
            dx, dg = _rms_bwd(e / D, gv, xh, rstd)
            dx_ref[...] = dx
            acc_ref[0:1, :] += dg

        return pl.pallas_call(
            body, name="final_loss", grid=(n_tok,),
            in_specs=[tok(D), const((1, D)), tok(D)],
            out_specs=[tok(D), const((8, D)), const((1, LANES))],
            out_shape=[jax.ShapeDtypeStruct((S, D), F32), jax.ShapeDtypeStruct((8, D), F32),
                       jax.ShapeDtypeStruct((1, LANES), F32)],
            compiler_params=_params(("arbitrary",)))(x, g, target)


    def bwd_out(dxn, gate, y, w_out, l, of, om, z, lse_f, lse_m):
        def body(dx_ref, gt_ref, y_ref, w_ref, of_ref, om_ref, fg_ref, mg_ref, lf_ref, lm_ref,
                 dof_ref, dom_ref, dfg_ref, dmg_ref, dy_ref, acc_ref, sf_ref, sm_ref):
            @pl.when(pl.program_id(0) == 0)
            def _():
                acc_ref[...] = jnp.zeros_like(acc_ref)

            dxv = dx_ref[...]
            acc_ref[0:1, :] += jnp.sum(dxv * y_ref[...].astype(F32), axis=0, keepdims=True)
            dy = (gt_ref[...] * dxv).astype(BF16)
            dy_ref[...] = dy
            du = _nt(dy, w_ref[...])
            lane = _lane((TM, LANES))
            for lo, width, o_ref, g_ref, do_ref, dg_ref, l_ref, st_ref in (
                    (0, FW, of_ref, fg_ref, dof_ref, dfg_ref, lf_ref, sf_ref),
                    (FW, MW, om_ref, mg_ref, dom_ref, dmg_ref, lm_ref, sm_ref)):
                gv, ov = g_ref[...], o_ref[...]
                sg = _sigmoid(gv)
                dup = du[:, lo:lo + width]
                dob = (dup * gv * sg).astype(BF16)
                do_ref[...] = dob
                dg_ref[...] = (dup * ov * sg * (1.0 + gv * (1.0 - sg))).astype(BF16)
                d = dob.astype(F32) * ov
                for pr in range(width // LANES):
                    cols = slice(pr * LANES, (pr + 1) * LANES)
                    dp = d[:, cols]
                    d0 = jnp.sum(jnp.where(lane < HEAD_DIM, dp, 0.0), axis=-1, keepdims=True)
                    d1 = jnp.sum(jnp.where(lane >= HEAD_DIM, dp, 0.0), axis=-1, keepdims=True)
                    st_ref[:, cols] = jnp.where(lane == 2, d0, jnp.where(lane == 3, d1, l_ref[:, cols]))

        return pl.pallas_call(
            body, name="bwd_out", grid=(n_tok,),
            in_specs=[tok(D), const((1, D)), tok(D), layer(l, (FW + MW, D)), tok(FW), tok(MW),
                      tok(FW, (cfg.o_fg - ZO) // FW), tok(MW, (cfg.o_mg - ZO) // MW), tok(FW), tok(MW)],
            out_specs=[tok(FW), tok(MW), tok(FW), tok(MW), tok(D), const((8, D)), tok(FW), tok(MW)],
            out_shape=[jax.ShapeDtypeStruct((S, FW), BF16), jax.ShapeDtypeStruct((S, MW), BF16),
                       jax.ShapeDtypeStruct((S, FW), BF16), jax.ShapeDtypeStruct((S, MW), BF16),
                       jax.ShapeDtypeStruct((S, D), BF16), jax.ShapeDtypeStruct((8, D), F32),
                       jax.ShapeDtypeStruct((S, FW), F32), jax.ShapeDtypeStruct((S, MW), F32)],
            compiler_params=_params(("arbitrary",)))(dxn, gate, y, w_out, of, om, z, z, lse_f, lse_m)

    def attn_bwd(q, k, v, do, stats, q_blk0, k_blk0, v_blk0, cumt, packed, chunked, name, side=None):
        has_bias = cumt is not None
        n_pairs = (FH if packed else MH) // 2
        wq = LANES if packed else 2 * HEAD_PAD
        n_q = S // TQ
        assert TQ == TK
        n_in, n_out = (6, 5) if has_bias else (5, 3)
        n_side_in, n_side_out = (len(side.ins), len(side.outs)) if side else (0, 0)

        def body(*refs):
            main_out = refs[n_in + n_side_in:n_in + n_side_in + n_out]
            if has_bias:
                q_ref, k_ref, v_ref, do_ref, st_ref, ct_ref = refs[:n_in]
                dq_ref, dk_ref, dv_ref, dc_ref, dr_ref = main_out
            else:
                q_ref, k_ref, v_ref, do_ref, st_ref = refs[:n_in]
                dq_ref, dk_ref, dv_ref = main_out
            jb = pl.program_id(1)
            k0 = jb * TK
            if side:
                side_refs = (refs[n_in:n_in + n_side_in], refs[n_in + n_side_in + n_out:n_in + n_side_in + n_out + n_side_out],
                             refs[-2], refs[-1])

                @pl.when((pl.program_id(0) == 0) & (jb == 0))
                def _():
                    side.start(*side_refs)

            @pl.when(jb == 0)
            def _():
                dq_ref[...] = jnp.zeros_like(dq_ref)
                if has_bias:
                    dr_ref[...] = jnp.zeros_like(dr_ref)

            dk_ref[...] = jnp.zeros_like(dk_ref)
            dv_ref[...] = jnp.zeros_like(dv_ref)
            kh = _heads(k_ref[...], packed)
            k_both = jnp.concatenate(kh, axis=0) if packed else None
            v_both = jnp.concatenate(_heads(v_ref[...], True), axis=0)
            if has_bias:
                ct = jnp.transpose(ct_ref[0, 0])
                ck = jnp.concatenate([ct[:, 0:1], ct[:, 1:2]], axis=0)

            def products(ib):
                rows = pl.ds(pl.multiple_of(ib * TQ, TQ), TQ)
                q2, do2 = q_ref[rows, :], do_ref[rows, :]
                if packed:
                    s = _nt(k_both, q2)
                else:
                    qh = _heads(q2, False)
                    s = jnp.concatenate([_nt(kh[0], qh[0]), _nt(kh[1], qh[1])], axis=0)
                return s, _nt(v_both, do2)

            def update(ib, s, dp, carry, masked):
                q0 = pl.multiple_of(ib * TQ, TQ)
                rows = pl.ds(q0, TQ)
                q2, do2 = q_ref[rows, :], do_ref[rows, :]
                st = jnp.transpose(st_ref[rows, :])
                if not packed:
                    qh = _heads(q2, False)
                if has_bias:
                    s = s - ck
                if masked:
                    allow = jnp.transpose(_allowed(q0, k0, chunked))
                    s = jnp.where(jnp.concatenate([allow, allow], axis=0), s, NEG)
                p = jnp.concatenate([jnp.exp(s[:TK] - st[0:1, :]), jnp.exp(s[TK:] - st[1:2, :])], axis=0)
                dv2 = _nn(p.astype(BF16), do2)
                ds = jnp.concatenate([p[:TK] * (dp[:TK] - st[2:3, :]), p[TK:] * (dp[TK:] - st[3:4, :])], axis=0)
                dsb = ds.astype(BF16)
                dv_ref[...] += _merge(dv2[:TK], dv2[TK:])
                if packed:
                    dk2 = _nn(dsb, q2)
                    dk_ref[...] += _merge(dk2[:TK], dk2[TK:])
                    dq_ref[rows, :] += _tn(dsb, k_both)
                else:
                    dk_ref[...] += jnp.concatenate([_nn(dsb[:TK], qh[0]), _nn(dsb[TK:], qh[1])], axis=1)
                    dq_ref[rows, :] += jnp.concatenate([_tn(dsb[:TK], kh[0]), _tn(dsb[TK:], kh[1])], axis=1)
                if has_bias:
                    sub = lax.broadcasted_iota(jnp.int32, (8, TQ), 0)
                    r0 = jnp.sum(ds[:TK], axis=0, keepdims=True)
                    r1 = jnp.sum(ds[TK:], axis=0, keepdims=True)
                    dr_ref[0, ib] += jnp.where(sub == 0, r0, jnp.where(sub == 1, r1, 0.0))
                    return carry - jnp.sum(ds, axis=1, keepdims=True)
                return carry

            def step(ib, carry, masked):
                s, dp = products(ib)
                return update(ib, s, dp, carry, masked)

            def two(ib, carry):
                s_a, dp_a = products(ib)
                s_b, dp_b = products(ib + 1)
                return update(ib + 1, s_b, dp_b, update(ib, s_a, dp_a, carry, False), False)

            dc = step(jb, jnp.zeros((2 * TK, 1), F32), True)
            n_rest = n_q - 1 - jb
            dc = lax.fori_loop(0, n_rest // 2, lambda t, cr: two(jb + 1 + 2 * t, cr), dc)
            dc = lax.cond(n_rest % 2 == 1, lambda cr: step(n_q - 1, cr, False), lambda cr: cr, dc)
            if has_bias:
                lane = _lane((TK, LANES))
                dc_ref[0] = jnp.where(lane == 0, dc[:TK], jnp.where(lane == 1, dc[TK:], 0.0))
            if side:
                @pl.when((pl.program_id(0) == n_pairs - 1) & (jb == S // TK - 1))
                def _():
                    side.wait(*side_refs)

        in_specs = [pl.BlockSpec((S, wq), lambda p, j: (0, q_blk0 + p)),
                    pl.BlockSpec((TK, wq), lambda p, j: (j, k_blk0 + p)),
                    pl.BlockSpec((TK, LANES), lambda p, j: (j, v_blk0 + p)),
                    pl.BlockSpec((S, LANES), lambda p, j: (0, p)),
                    pl.BlockSpec((S, LANES), lambda p, j: (0, p))]
        args = [q, k, v, do, stats]
        out_specs = [pl.BlockSpec((S, wq), lambda p, j: (0, p)),
                     pl.BlockSpec((TK, wq), lambda p, j: (j, p)),
                     pl.BlockSpec((TK, LANES), lambda p, j: (j, p))]
        out_shape = [jax.ShapeDtypeStruct((S, n_pairs * wq), F32), jax.ShapeDtypeStruct((S, n_pairs * wq), F32),
                     jax.ShapeDtypeStruct((S, n_pairs * LANES), F32)]
        if has_bias:
            in_specs += [pl.BlockSpec((1, 1, 8, TK), lambda p, j: (p, j, 0, 0))]
            args += [cumt]
            out_specs += [pl.BlockSpec((1, TK, LANES), lambda p, j: (p, j, 0)),
                          pl.BlockSpec((1, S // TQ, 8, TQ), lambda p, j: (p, 0, 0, 0))]
            out_shape += [jax.ShapeDtypeStruct((n_pairs, S, LANES), F32),
                          jax.ShapeDtypeStruct((n_pairs, S // TQ, 8, TQ), F32)]
        extra = {}
        if side:
            hbm = pl.BlockSpec(memory_space=pl.ANY)
            in_specs += [hbm] * n_side_in
            args += side.ins
            out_specs += [hbm] * n_side_out
            out_shape += side.outs
            extra = dict(scratch_shapes=[pltpu.SemaphoreType.DMA((side.n_sems,)), pltpu.SemaphoreType.DMA((side.n_sems,))],
                         input_output_aliases={n_in + a: n_out + b for a, b in side.aliases.items()})
        outs = pl.pallas_call(
            body, name=name, grid=(n_pairs, S // TK), in_specs=in_specs, out_specs=out_specs, out_shape=out_shape,
            compiler_params=_params(("arbitrary", "arbitrary")), **extra)(*args)
        return (*outs[:n_out], list(outs[n_out:])) if side else outs

    def fox_post(dcum, z, bf_pad):
        def rev(i):
            return n_tok - 1 - i

        def body(dc_ref, z_ref, b_ref, dff_ref, acc_ref, carry):
            @pl.when(pl.program_id(0) == 0)
            def _():
                carry[...] = jnp.zeros_like(carry)
                acc_ref[...] = jnp.zeros_like(acc_ref)

            _, sig_neg = _log_f_terms(z_ref[...], b_ref[...])
            row = lax.broadcasted_iota(jnp.int32, (TM, TM), 0)
            col = lax.broadcasted_iota(jnp.int32, (TM, TM), 1)
            tri = (col >= row).astype(F32)
            dlog = jnp.dot(tri, dc_ref[...], precision=lax.Precision.HIGHEST, preferred_element_type=F32) + carry[...]
            carry[...] = dlog[0:1, :]
            dff = dlog * sig_neg
            dff_ref[...] = dff
            acc_ref[0:1, :] += jnp.sum(dff, axis=0, keepdims=True)

        return pl.pallas_call(
            body, name="fox_post", grid=(n_tok,),
            in_specs=[pl.BlockSpec((TM, LANES), lambda i: (rev(i), 0)),
                      pl.BlockSpec((TM, LANES), lambda i: (rev(i), misc_blk)), const((1, LANES))],
            out_specs=[pl.BlockSpec((TM, LANES), lambda i: (rev(i), 0)), const((8, LANES))],
            out_shape=[jax.ShapeDtypeStruct((S, LANES), F32), jax.ShapeDtypeStruct((8, LANES), F32)],
            scratch_shapes=[pltpu.VMEM((1, LANES), F32)],
            compiler_params=_params(("arbitrary",)))(dcum, z, bf_pad)

    def mla_post(dq, dk, dv, dff, z, gq, gkv, w_uq, w_uk, w_v, l, cos_t, sin_t):
        def body(dq_ref, dk_ref, dv_ref, dff_ref, ql_ref, kvl_ref, gq_ref, gkv_ref, wq_ref, wk_ref, wv_ref,
                 c_ref, s_ref, zq_ref, zkv_ref, zms_ref, dwq_ref, dwk_ref, dwv_ref, dgq_ref, dgkv_ref):
            @pl.when(pl.program_id(0) == 0)
            def _():
                for r in (dwq_ref, dwk_ref, dwv_ref, dgq_ref, dgkv_ref):
                    r[...] = jnp.zeros_like(r)

            cos1, sin1 = c_ref[...], s_ref[...]
            gqv, gkvv = gq_ref[...], gkv_ref[...]
            qn, qxh, qrstd = _rms(ql_ref[...], gqv)
            dq_pre = _rope_t(dq_ref[...] * MLA_SCALE, jnp.tile(cos1, (1, MH)), jnp.tile(sin1, (1, MH))).astype(BF16)
            dwq_ref[...] += _tn(qn.astype(BF16), dq_pre)
            dql, dgq = _rms_bwd(_nt(dq_pre, wq_ref[...]), gqv, qxh, qrstd)
            zq_ref[...] = dql.astype(BF16)
            dgq_ref[0:1, :] += dgq

            dkv = dk_ref[...]
            lane = _lane(dkv.shape) & (HEAD_PAD - 1)
            dkn = jnp.where(lane < HEAD_DIM, dkv, 0.0).astype(BF16)
            dkr = dkv[:, 0:HEAD_PAD]
            for hd in range(1, MH):
                dkr = dkr + dkv[:, hd * HEAD_PAD:(hd + 1) * HEAD_PAD]
            lane1 = _lane(dkr.shape)
            dkr = jnp.where((lane1 >= ROPE_LO) & (lane1 < ROPE_LO + ROPE_DIM), dkr, 0.0)
            dkr = _rope_t(dkr, cos1, sin1)
            zms_ref[...] = (dkr + dff_ref[...]).astype(BF16)

            kvn, kxh, krstd = _rms(kvl_ref[...], gkvv)
            kvb = kvn.astype(BF16)
            dvb = dv_ref[...].astype(BF16)
            dwk_ref[...] += _tn(kvb, dkn)
            dwv_ref[...] += _tn(kvb, dvb)
            dkvl, dgkv = _rms_bwd(_nt(dkn, wk_ref[...]) + _nt(dvb, wv_ref[...]), gkvv, kxh, krstd)
            zkv_ref[...] = dkvl.astype(BF16)
            dgkv_ref[0:1, :] += dgkv

        return pl.pallas_call(
            body, name="mla_post", grid=(n_tok,),
            in_specs=[tok(QW), tok(QW), tok(MW), tok(LANES), tok(QL, (cfg.o_ql - ZO) // QL), tok(KVL, (cfg.o_kv - ZO) // KVL),
                      const((1, QL)), const((1, KVL)), layer(l, (QL, QW)), layer(l, (KVL, QW)), layer(l, (KVL, MW)),
                      tok(LANES), tok(LANES)],
            out_specs=[tok(QL), tok(KVL), tok(LANES), const((QL, QW)), const((KVL, QW)), const((KVL, MW)),
                       const((8, QL)), const((8, KVL))],
            out_shape=[jax.ShapeDtypeStruct((S, QL), BF16), jax.ShapeDtypeStruct((S, KVL), BF16),
                       jax.ShapeDtypeStruct((S, LANES), BF16), jax.ShapeDtypeStruct((QL, QW), F32),
                       jax.ShapeDtypeStruct((KVL, QW), F32), jax.ShapeDtypeStruct((KVL, MW), F32),
                       jax.ShapeDtypeStruct((8, QL), F32), jax.ShapeDtypeStruct((8, KVL), F32)],
            compiler_params=_params(("arbitrary",)))(dq, dk, dv, dff, z, z, gq, gkv, w_uq, w_uk, w_v, cos_t, sin_t)

    def bwd_in(dz, w_in, l, x, dxn, g, scale):
        def body(dz_ref, w_ref, x_ref, dx_ref, g_ref, sc_ref, o_ref, acc_ref):
            @pl.when(pl.program_id(0) == 0)
            def _():
                acc_ref[...] = jnp.zeros_like(acc_ref)

            dh = _nt(dz_ref[...], w_ref[...])
            gv, mod = g_ref[...], 1.0 + sc_ref[...]
            _, xh, rstd = _rms(x_ref[...], gv)
            t = dh * xh
            acc_ref[0:1, :] += jnp.sum(dh, axis=0, keepdims=True)
            acc_ref[1:2, :] += jnp.sum(t * gv, axis=0, keepdims=True)
            acc_ref[2:3, :] += jnp.sum(t * mod, axis=0, keepdims=True)
            dx, _ = _rms_bwd(dh, gv * mod, xh, rstd)
            o_ref[...] = dx_ref[...] + dx

        return pl.pallas_call(
            body, name="bwd_in", grid=(n_tok,),
            in_specs=[tok(NZ), layer(l, (D, NZ)), tok(D), tok(D), const((1, D)), const((1, D))],
            out_specs=[tok(D), const((8, D))],
            out_shape=[jax.ShapeDtypeStruct((S, D), F32), jax.ShapeDtypeStruct((8, D), F32)],
            compiler_params=_params(("arbitrary",)))(dz, w_in, x, dxn, g, scale)


    def pair_rows(cum):
        n_pairs = FH // 2
        ct = jnp.pad(cum[:, :FH].T.reshape(n_pairs, 2, S), ((0, 0), (0, 6), (0, 0)))
        return ct.reshape(n_pairs, 8, S // TK, TK).transpose(0, 2, 1, 3)

    def bias_grad(dc, dr):
        n_pairs = FH // 2
        d = dr.transpose(0, 2, 1, 3).reshape(n_pairs, 8, S)[:, :2, :].reshape(FH, S).T
        d = d + dc[:, :, :2].transpose(1, 0, 2).reshape(S, FH)
        return jnp.pad(d, ((0, 0), (0, LANES - FH)))

    def layer_forward(x, wl, mod, later_shards=None):
        shift, scale, gate = mod
        h, z, qkv = ln_inproj(x, wl.norm_g, scale, shift, wl.w_in, wl.l)
        ct = pair_rows(fox_prep(z, wl.bf_pad))
        fox = attn_fwd(qkv, qkv, qkv, cfg.o_fq // LANES, cfg.o_fk // LANES, cfg.o_fv // LANES, ct, True, False,
                       "fox_fwd_gather" if later_shards else "fox_fwd", GatherOverIci(later_shards) if later_shards else None)
        of, lse_f = fox[0], fox[1]
        qp, kp, vp = mla_prep(z, wl.gq, wl.gkv, wl.w_uq, wl.w_uk, wl.w_v, wl.l, cfg.cos_t, cfg.sin_t)
        mla = attn_fwd(qp, kp, vp, 0, 0, 0, None, False, True, "mla_fwd_gather" if later_shards else "mla_fwd",
                       GatherToSibling(fox[2]) if later_shards else None)
        om, lse_m = mla[0], mla[1]
        xn, u, y = gate_outproj(of, om, z, wl.w_out, wl.l, x, gate)
        saved = types.SimpleNamespace(x=x, h=h, z=z, qkv=qkv, ct=ct, of=of, lse_f=lse_f, qp=qp, kp=kp, vp=vp,
                                      om=om, lse_m=lse_m, u=u, y=y)
        return (xn, saved, mla[2]) if later_shards else (xn, saved)

    def layer_backward(dxn, sv, wl, mod, side=None):
        shift, scale, gate = mod
        do_f, do_m, dfg, dmg, dy, acc_o, st_f, st_m = bwd_out(dxn, gate, sv.y, wl.w_out, wl.l, sv.of, sv.om, sv.z,
                                                              sv.lse_f, sv.lse_m)
        dw_out = matmul_tn(sv.u, dy, "dw_out")
        dfq, dfk, dfv, dck, dcr, *rode = attn_bwd(sv.qkv, sv.qkv, sv.qkv, do_f, st_f, cfg.o_fq // LANES, cfg.o_fk // LANES,
                                                  cfg.o_fv // LANES, sv.ct, True, False,
                                                  "fox_bwd_exchange" if side else "fox_bwd", side)
        dff, acc_f = fox_post(bias_grad(dck, dcr), sv.z, wl.bf_pad)
        dqp, dkp, dvp = attn_bwd(sv.qp, sv.kp, sv.vp, do_m, st_m, 0, 0, 0, None, False, True, "mla_bwd")
        zq, zkv, zms, dw_uq, dw_uk, dw_v, dgq, dgkv = mla_post(
            dqp, dkp, dvp, dff, sv.z, wl.gq, wl.gkv, wl.w_uq, wl.w_uk, wl.w_v, wl.l, cfg.cos_t, cfg.sin_t)
        dz = jnp.concatenate([(dfq * FOX_SCALE).astype(BF16), dfk.astype(BF16), dfv.astype(BF16), dfg, dmg, zq, zkv, zms], axis=1)
        dx, acc_i = bwd_in(dz, wl.w_in, wl.l, sv.x, dxn, wl.norm_g, scale)
        dw_in = matmul_tn(sv.h, dz, "dw_in")
        grads = types.SimpleNamespace(
            w_in=dw_in, w_out=dw_out, w_uq=dw_uq, w_uk=dw_uk, w_v=dw_v, gq=dgq[0], gkv=dgkv[0],
            b_f=acc_f[0, :FH], norm_g=acc_i[2], dmod=jnp.concatenate([acc_i[0], acc_i[1], acc_o[0]]))
        return (dx, grads, rode[0]) if side else (dx, grads)

    return types.SimpleNamespace(layer_forward=layer_forward, layer_backward=layer_backward, final_loss=final_loss)


def _pack_rows(parts, dtype, row_multiple):
    flat = jnp.concatenate([p.reshape(-1).astype(dtype) for p in parts])
    per = SLAB_COLS * row_multiple
    total = -(-flat.shape[0] // per) * per
    return jnp.pad(flat, (0, total - flat.shape[0])).reshape(total // SLAB_COLS, SLAB_COLS)


def _unpack(flat, shapes):
    out, off = [], 0
    for shp in shapes:
        n = 1
        for d in shp:
            n *= d
        out.append(flat[off:off + n].reshape(shp))
        off += n
    return out


def kernel(x, c, positions, norm_g, w_ada, b_ada, w_in, b_f, q_norm_g, w_uq, kv_norm_g, w_ukv, w_out, final_g, loss_target, m_norm_g, m_w_ada, m_b_ada, m_w_in, m_b_f, m_q_norm_g, m_w_uq, m_kv_norm_g, m_w_ukv, m_w_out, m_final_g, v_norm_g, v_w_ada, v_b_ada, v_w_in, v_b_f, v_q_norm_g, v_w_uq, v_kv_norm_g, v_w_ukv, v_w_out, v_final_g):
    S, D = x.shape[1], x.shape[2]
    L = norm_g.shape[0]
    FH = b_f.shape[1]
    QL, KVL = q_norm_g.shape[1], kv_norm_g.shape[1]
    MH = w_ukv.shape[2] * N_CHIPS // (2 * HEAD_DIM)
    FW, MW = FH * HEAD_DIM, MH * HEAD_DIM
    NA = w_ada.shape[2]
    n_in = w_in.shape[2] * N_CHIPS
    cfg = types.SimpleNamespace(S=S, D=D, FW=FW, MW=MW, QL=QL, KVL=KVL, FH=FH, MH=MH)
    cfg.o_fq, cfg.o_fk, cfg.o_fv, cfg.o_fg, cfg.o_mg = 0, FW, 2 * FW, 3 * FW, 4 * FW
    cfg.o_ql = 4 * FW + MW
    cfg.o_kv = cfg.o_ql + QL
    cfg.o_ms = cfg.o_kv + KVL
    cfg.NZ = cfg.o_ms + LANES
    assert FW == MW and FH % 2 == 0 and MH % 2 == 0 and cfg.o_ql % QL == 0 and cfg.o_kv % KVL == 0 and KVL == LANES
    assert n_in == 4 * FW + FH + QL + KVL + ROPE_DIM + MW

    mx, my, mc = _my_pos()
    my_chip = 2 * mx + my
    my_dev = 2 * my_chip + mc

    inv_freq = 1.0 / (ROPE_THETA ** (jnp.arange(0, ROPE_DIM, 2, dtype=F32) / ROPE_DIM))
    ang = positions[0].astype(F32)[:, None] * inv_freq
    cos, sin = jnp.cos(ang), jnp.sin(ang)
    cfg.cos_t = jnp.concatenate([jnp.ones((S, ROPE_LO), F32), cos, cos, jnp.ones((S, HEAD_PAD - ROPE_LO - ROPE_DIM), F32)], axis=1)
    cfg.sin_t = jnp.concatenate([jnp.zeros((S, ROPE_LO), F32), -sin, sin, jnp.zeros((S, HEAD_PAD - ROPE_LO - ROPE_DIM), F32)], axis=1)

    assert L % 4 == 0
    def lane_pad(a):
        return jnp.pad(a, ((0, 0),) * (a.ndim - 1) + ((0, -a.shape[-1] % LANES),))

    shards = (w_in, w_uq, w_ukv, w_out)
    padded = [lane_pad(w.astype(BF16)) for w in shards]

    def kernel_layouts(gathered, own):
        def all_chips(g, w, axis):
            return jnp.concatenate([jnp.where(my_chip == t, w.astype(BF16), g[t][..., :w.shape[-1]])
                                    for t in range(N_CHIPS)], axis=axis)

        n = own[0].shape[0]
        w_in_f = all_chips(gathered[0], own[0], 2)
        w_uq_f = all_chips(gathered[1], own[1], 2)
        w_ukv_f = all_chips(gathered[2], own[2], 2)
        w_out_f = all_chips(gathered[3], own[3], 1)
        sizes = (FW, FW, FW, FH, FW, QL, KVL, ROPE_DIM, MW)
        offs = [0]
        for sz in sizes:
            offs.append(offs[-1] + sz)
        fq_w, fk_w, fv_w, ff_w, fg_w, ql_w, kvl_w, kr_w, mg_w = [w_in_f[:, :, offs[i]:offs[i + 1]] for i in range(len(sizes))]
        zeros = lambda width: jnp.zeros((n, D, width), BF16)
        w_in_p = jnp.concatenate([fq_w, fk_w, fv_w, fg_w, mg_w, ql_w, kvl_w, ff_w, zeros(ROPE_LO - FH), kr_w,
                                  zeros(HEAD_PAD - ROPE_LO - ROPE_DIM)], axis=2)
        w_uq_p = jnp.pad(w_uq_f.reshape(n, QL, MH, HEAD_DIM + ROPE_DIM),
                         ((0, 0), (0, 0), (0, 0), (0, HEAD_PAD - HEAD_DIM - ROPE_DIM))).reshape(n, QL, MH * HEAD_PAD)
        w_ukv4 = w_ukv_f.reshape(n, KVL, MH, 2 * HEAD_DIM)
        w_uk_p = jnp.pad(w_ukv4[..., :HEAD_DIM], ((0, 0), (0, 0), (0, 0), (0, HEAD_PAD - HEAD_DIM))).reshape(n, KVL, MH * HEAD_PAD)
        return types.SimpleNamespace(w_in=w_in_p, w_uq=w_uq_p, w_uk=w_uk_p, w_v=w_ukv4[..., HEAD_DIM:].reshape(n, KVL, MW),
                                     w_out=w_out_f)

    first = kernel_layouts(weights_gather([p[:1] for p in padded], "weights_first"), [w[:1] for w in shards])

    c_all = allgather8(c.reshape(8, D // 8), "gather_c").reshape(N_DEV, D)
    c_pad = jnp.pad(c_all, ((0, 16 - N_DEV), (0, 0)))
    mod_part = ada_forward(c_pad, w_ada)[:, :N_DEV, :]
    mod_all = allgather8(mod_part.reshape(-1, LANES), "gather_mod").reshape(N_CHIPS, 2, L, N_DEV, NA)[:, 0]
    mod_full = mod_all.transpose(1, 2, 0, 3).reshape(L, N_DEV, N_CHIPS * NA) + b_ada[:, None, :]
    mod_mine = lax.dynamic_index_in_dim(mod_full, my_dev, axis=1, keepdims=True)

    step = make_step(cfg)
    bf_pad = jnp.pad(b_f, ((0, 0), (0, LANES - FH)))
    mods = [(mod_mine[l, :, :D], mod_mine[l, :, D:2 * D], mod_mine[l, :, 2 * D:]) for l in range(L)]

    def layer_params(l, ws_, at):
        return types.SimpleNamespace(l=at, norm_g=norm_g[l][None], bf_pad=bf_pad[l][None], gq=q_norm_g[l][None],
                                     gkv=kv_norm_g[l][None], w_in=ws_.w_in, w_uq=ws_.w_uq, w_uk=ws_.w_uk, w_v=ws_.w_v,
                                     w_out=ws_.w_out)

    layers = [layer_params(0, first, 0)]
    xl, sv, later = step.layer_forward(x[0], layers[0], mods[0], [p[1:] for p in padded])
    saved = [sv]
    rest = kernel_layouts(later, [w[1:] for w in shards])
    for l in range(1, L):
        layers.append(layer_params(l, rest, l - 1))
        xl, sv = step.layer_forward(xl, layers[l], mods[l])
        saved.append(sv)
    dx, acc_fin, loss_part = step.final_loss(xl, final_g[None], loss_target[0])
    loss = lax.psum(loss_part[0, 0], ("x", "y", "c"))
    ms = cfg.o_ms
    runs = [(0, 3 * FW, 0), (3 * FW, FH, ms), (3 * FW + FH, FW, cfg.o_fg), (4 * FW + FH, QL + KVL, cfg.o_ql),
            (4 * FW + FH + QL + KVL, ROPE_DIM, ms + ROPE_LO), (4 * FW + FH + QL + KVL + ROPE_DIM, MW, cfg.o_mg)]
    big_names = ["w_in", "w_uq", "w_ukv", "w_out"]

    def shard_columns(dw, t):
        a, b = t * w_in.shape[2], (t + 1) * w_in.shape[2]
        return jnp.concatenate([dw[:, p0 + max(a, r0) - r0:p0 + min(b, r0 + sz) - r0]
                                for r0, sz, p0 in runs if max(a, r0) < min(b, r0 + sz)], axis=1)

    def chip_parts(group, tag):
        n = len(group)
        stk = lambda name: jnp.stack([getattr(g, name) for g in group])
        g_in4 = jnp.stack([jnp.stack([shard_columns(g.w_in, t) for g in group]) for t in range(N_CHIPS)])
        dw_uq_f = stk("w_uq").reshape(n, QL, MH, HEAD_PAD)[..., :HEAD_DIM + ROPE_DIM].reshape(n, QL, -1)
        dw_ukv_f = jnp.concatenate([stk("w_uk").reshape(n, KVL, MH, HEAD_PAD)[..., :HEAD_DIM],
                                    stk("w_v").reshape(n, KVL, MH, HEAD_DIM)], axis=3).reshape(n, KVL, -1)
        gs = [g_in4, dw_uq_f.reshape(n, QL, N_CHIPS, -1).transpose(2, 0, 1, 3),
              dw_ukv_f.reshape(n, KVL, N_CHIPS, -1).transpose(2, 0, 1, 3), stk("w_out").reshape(n, N_CHIPS, -1, D).transpose(1, 0, 2, 3)]
        gs = [lane_pad(g.astype(BF16)) for g in gs]
        theirs = halves_to_sibling(gs, "grads_sibling_" + tag)
        out = []
        for g, o, nm in zip(gs, theirs, big_names):
            keep = lax.dynamic_slice_in_dim(g, mc * (n // 2), n // 2, axis=1)
            merged = (N_CHIPS * (n // 2),) + g.shape[2:]
            out.append(add_cast(keep.reshape(merged), o.reshape(merged), BF16, "grads_chip_sum_%s_%s" % (nm, tag)).reshape(o.shape))
        return out

    def reduced(chip_part, parts, tag):
        where = jnp.stack([my_chip, mc]).astype(jnp.int32)
        red = [sum_chips(p, o, where, "grads_sum_%s_%s" % (nm, tag)) for p, o, nm in zip(parts, chip_part, big_names)]
        return [g[..., :w.shape[-1]] for g, w in zip(halves_gather(red, "grads_back_" + tag), shards)]

    gl = [None] * L
    half_l = L // 2
    for l in range(L - 1, half_l - 1, -1):
        dx, gl[l] = step.layer_backward(dx, saved[l], layers[l], mods[l])
    part_hi = chip_parts(gl[half_l:], "hi")
    dx, gl[half_l - 1], parts_hi = step.layer_backward(dx, saved[half_l - 1], layers[half_l - 1], mods[half_l - 1],
                                                       ExchangeOverIci(part_hi))
    for l in range(half_l - 2, -1, -1):
        dx, gl[l] = step.layer_backward(dx, saved[l], layers[l], mods[l])
    part_lo = chip_parts(gl[:half_l], "lo")
    parts_lo = run_side(ExchangeOverIci(part_lo), "grads_chips_lo")
    g_w_in, g_w_uq, g_w_ukv, g_w_out = [jnp.concatenate([lo, hi]) for lo, hi in
                                        zip(reduced(part_lo, parts_lo, "lo"), reduced(part_hi, parts_hi, "hi"))]
    grad_x = dx[None]

    stack = lambda name: jnp.stack([getattr(g, name) for g in gl])
    small_parts = [stack("norm_g"), stack("dmod"), stack("b_f"), stack("gq"), stack("gkv"), acc_fin[0]]
    small_shapes = [p.shape for p in small_parts]
    small = _pack_rows(small_parts, F32, 8).reshape(-1, LANES)
    small_all = allgather8(small, "gather_small").reshape(N_DEV, -1, LANES)
    small_sum = sum_leading(small_all, "sum_small")
    g_norm_g, g_b_ada, g_b_f, g_q_norm_g, g_kv_norm_g, g_final_g = _unpack(small_sum.reshape(-1), small_shapes)

    n_ng = L * D
    dmod_all = small_all.reshape(N_DEV, -1)[:, n_ng:n_ng + L * 3 * D].reshape(N_DEV, L, 3 * D)
    dmod_cols = lax.dynamic_slice_in_dim(dmod_all, my_chip * NA, NA, axis=2).transpose(1, 0, 2)
    g_w_ada = ada_backward(c_pad, jnp.pad(dmod_cols, ((0, 0), (0, 16 - N_DEV), (0, 0))))


    names = ["norm_g", "w_ada", "b_ada", "w_in", "b_f", "q_norm_g", "w_uq", "kv_norm_g", "w_ukv", "w_out", "final_g"]
    ws = dict(norm_g=norm_g, w_ada=w_ada, b_ada=b_ada, w_in=w_in, b_f=b_f, q_norm_g=q_norm_g, w_uq=w_uq,
              kv_norm_g=kv_norm_g, w_ukv=w_ukv, w_out=w_out, final_g=final_g)
    msd = dict(norm_g=m_norm_g, w_ada=m_w_ada, b_ada=m_b_ada, w_in=m_w_in, b_f=m_b_f, q_norm_g=m_q_norm_g, w_uq=m_w_uq,
               kv_norm_g=m_kv_norm_g, w_ukv=m_w_ukv, w_out=m_w_out, final_g=m_final_g)
    vsd = dict(norm_g=v_norm_g, w_ada=v_w_ada, b_ada=v_b_ada, w_in=v_w_in, b_f=v_b_f, q_norm_g=v_q_norm_g, w_uq=v_w_uq,
               kv_norm_g=v_kv_norm_g, w_ukv=v_w_ukv, w_out=v_w_out, final_g=v_final_g)
    gsd = dict(norm_g=g_norm_g, w_ada=g_w_ada, b_ada=g_b_ada, w_in=g_w_in, b_f=g_b_f, q_norm_g=g_q_norm_g, w_uq=g_w_uq,
               kv_norm_g=g_kv_norm_g, w_ukv=g_w_ukv, w_out=g_w_out, final_g=g_final_g)
    small_names = ["norm_g", "b_ada", "b_f", "q_norm_g", "kv_norm_g", "final_g"]
    sm_shapes = [ws[n].shape for n in small_names]
    pk = lambda d: _pack_rows([d[n] for n in small_names], F32, 8).reshape(1, -1, LANES)
    sm_out = adamw(pk(ws), pk(gsd), pk(msd), pk(vsd), "adamw_small")
    sm_d, sm_m, sm_v = [dict(zip(small_names, _unpack(o.reshape(-1), sm_shapes))) for o in sm_out]
    delta, new_m, new_v = dict(sm_d), dict(sm_m), dict(sm_v)
    for n in ["w_ada", "w_in", "w_uq", "w_ukv", "w_out"]:
        delta[n], new_m[n], new_v[n] = adamw(ws[n], gsd[n], msd[n], vsd[n], "adamw_" + n)

    return (loss, grad_x, *[gsd[n] for n in names], *[delta[n] for n in names],
            *[new_m[n] for n in names], *[new_v[n] for n in names])
```

```python
import types

import jax
import jax.numpy as jnp
from jax import lax
from jax.experimental import pallas as pl
from jax.experimental.pallas import tpu as pltpu

F32 = jnp.float32
BF16 = jnp.bfloat16
MESH = pl.DeviceIdType.MESH

N_CHIPS = 4
N_DEV = 8
HEAD_DIM = 64
ROPE_DIM = 32
ROPE_THETA = 10000.0
HEAD_PAD = 128
ROPE_LO = 64
ROPE_HALF = 16
CHUNK_SHIFT = 6
LANES = 128
EPS = 1e-6
NEG = -1e30
ADAM_LR = 0.001
ADAM_B1 = 0.9
ADAM_B2 = 0.999
ADAM_EPS = 1e-08
ADAM_WD = 0.01
ADAM_STEP = 10
VMEM_LIMIT = 48 * 1024 * 1024
SLAB_COLS = 1024


def _params(sem=None, vmem=VMEM_LIMIT):
    return pltpu.CompilerParams(dimension_semantics=sem, vmem_limit_bytes=vmem)


def _nn(a, b):
    return jnp.dot(a, b, preferred_element_type=F32)


def _nt(a, b):
    return lax.dot_general(a, b, (((1,), (1,)), ((), ())), preferred_element_type=F32)


def _tn(a, b):
    return lax.dot_general(a, b, (((0,), (0,)), ((), ())), preferred_element_type=F32)


def _sigmoid(x):
    return 1.0 / (1.0 + jnp.exp(-x))


def _lane(shape):
    return lax.broadcasted_iota(jnp.int32, shape, len(shape) - 1)


def _pick(n, cands):
    for c in cands:
        if n % c == 0:
            return c
    return n


def _my_pos():
    return lax.axis_index("x"), lax.axis_index("y"), lax.axis_index("c")


def allgather8(xs, name):
    m_per, n = xs.shape

    def body(x_ref, out_ref, send_sems, recv_sems, local_sem):
        x, y, c = _my_pos()
        me, sibling = (x, y, c), (x, y, 1 - c)
        chips = [(1 - x, y), (x, 1 - y), (1 - x, 1 - y)]

        def rows(px, py, pc):
            return out_ref.at[pl.ds((4 * px + 2 * py + pc) * m_per, m_per), :]

        def copy(k, block, to, src=None):
            return pltpu.make_async_remote_copy(
                src_ref=rows(*block) if src is None else src, dst_ref=rows(*block),
                send_sem=send_sems.at[k], recv_sem=recv_sems.at[k], device_id=to, device_id_type=MESH)

        mine = pltpu.make_async_copy(x_ref, rows(*me), local_sem)
        mine.start()
        first = [copy(0, me, sibling, src=x_ref)]
        first += [copy(1 + j, me, (*chip, c), src=x_ref) for j, chip in enumerate(chips)]
        for cp in first:
            cp.start()
        passed = [copy(4 + j, (*chip, c), sibling) for j, chip in enumerate(chips)]
        for j, chip in enumerate(chips):
            copy(1 + j, (*chip, c), me).wait_recv()
            passed[j].start()
        copy(0, sibling, me).wait_recv()
        for j, chip in enumerate(chips):
            copy(4 + j, (*chip, 1 - c), me).wait_recv()
        for cp in first + passed:
            cp.wait_send()
        mine.wait()

    return pl.pallas_call(
        body, name=name,
        out_shape=jax.ShapeDtypeStruct((N_DEV * m_per, n), xs.dtype),
        in_specs=[pl.BlockSpec(memory_space=pltpu.VMEM)],
        out_specs=pl.BlockSpec(memory_space=pltpu.VMEM),
        scratch_shapes=[pltpu.SemaphoreType.DMA((7,)), pltpu.SemaphoreType.DMA((7,)), pltpu.SemaphoreType.DMA],
    )(xs)


def _remote(src, dst, send_sems, recv_sems, k, to):
    return pltpu.make_async_remote_copy(src_ref=src, dst_ref=dst, send_sem=send_sems.at[k], recv_sem=recv_sems.at[k],
                                        device_id=to, device_id_type=MESH)


def _hbm_call(body, name, ins, out_shapes, n_sems, aliases=None):
    hbm = pl.BlockSpec(memory_space=pl.ANY)
    scratch = [pltpu.SemaphoreType.DMA((n_sems,)), pltpu.SemaphoreType.DMA((n_sems,))]
    return pl.pallas_call(body, name=name, out_shape=out_shapes, in_specs=[hbm] * len(ins),
                          out_specs=[hbm] * len(out_shapes), scratch_shapes=scratch,
                          input_output_aliases=aliases or {})(*ins)


def _layers_half(ref, h, axis=0):
    size = ref.shape[axis] // 2
    idx = (slice(None),) * axis + (pl.ds(h * size, size),)
    return ref.at[idx]


class GatherOverIci:
    def __init__(self, ws):
        self.ins = list(ws)
        self.outs = [jax.ShapeDtypeStruct((N_CHIPS,) + w.shape, w.dtype) for w in ws]
        self.aliases = {}
        self.n_sems = 3 * len(ws)

    def _copies(self, x_refs, o_refs, send_sems, recv_sems):
        x, y, c = _my_pos()
        s = 2 * x + y
        out = []
        for a in range(len(x_refs)):
            for k, (tx, ty) in enumerate([(1 - x, y), (x, 1 - y), (1 - x, 1 - y)]):
                mine = _remote(_layers_half(x_refs[a], c, 1), _layers_half(o_refs[a].at[s], c, 1), send_sems, recv_sems,
                               3 * a + k, (tx, ty, c))
                got = _layers_half(o_refs[a].at[2 * tx + ty], c, 1)
                out.append((mine, _remote(got, got, send_sems, recv_sems, 3 * a + k, (tx, ty, c))))
        return out

    def start(self, x_refs, o_refs, send_sems, recv_sems):
        for mine, _ in self._copies(x_refs, o_refs, send_sems, recv_sems):
            mine.start()

    def wait(self, x_refs, o_refs, send_sems, recv_sems):
        for mine, theirs in self._copies(x_refs, o_refs, send_sems, recv_sems):
            theirs.wait_recv()
            mine.wait_send()


class GatherToSibling:
    def __init__(self, gathered):
        self.ins = list(gathered)
        self.outs = [jax.ShapeDtypeStruct(g.shape, g.dtype) for g in gathered]
        self.aliases = {a: a for a in range(len(gathered))}
        self.n_sems = 3 * len(gathered)

    def _copies(self, o_refs, send_sems, recv_sems):
        x, y, c = _my_pos()
        out = []
        for a in range(len(o_refs)):
            for k, (tx, ty) in enumerate([(1 - x, y), (x, 1 - y), (1 - x, 1 - y)]):
                got = _layers_half(o_refs[a].at[2 * tx + ty], c, 1)
                theirs = _layers_half(o_refs[a].at[2 * tx + ty], 1 - c, 1)
                out.append((_remote(got, got, send_sems, recv_sems, 3 * a + k, (x, y, 1 - c)),
                            _remote(theirs, theirs, send_sems, recv_sems, 3 * a + k, (x, y, 1 - c))))
        return out

    def start(self, x_refs, o_refs, send_sems, recv_sems):
        for mine, _ in self._copies(o_refs, send_sems, recv_sems):
            mine.start()

    def wait(self, x_refs, o_refs, send_sems, recv_sems):
        for mine, theirs in self._copies(o_refs, send_sems, recv_sems):
            theirs.wait_recv()
            mine.wait_send()


def run_side(side, name):
    n_in, n_out = len(side.ins), len(side.outs)

    def body(*refs):
        parts = refs[:n_in], refs[n_in:n_in + n_out], refs[n_in + n_out], refs[n_in + n_out + 1]
        side.start(*parts)
        side.wait(*parts)

    return _hbm_call(body, name, side.ins, side.outs, side.n_sems, aliases=side.aliases)


def weights_gather(ws, name):
    return run_side(GatherToSibling(run_side(GatherOverIci(ws), name + "_ici")), name + "_sibling")


def halves_to_sibling(gs, name):
    n = len(gs)

    def body(*refs):
        x_refs, o_refs, send_sems, recv_sems = refs[:n], refs[n:2 * n], refs[2 * n], refs[2 * n + 1]
        x, y, c = _my_pos()
        cps = [_remote(_layers_half(x_refs[a], 1 - c, axis=1), o_refs[a], send_sems, recv_sems, a, (x, y, 1 - c))
               for a in range(n)]
        for cp in cps:
            cp.start()
        for cp in cps:
            cp.wait()

    outs = [jax.ShapeDtypeStruct((g.shape[0], g.shape[1] // 2) + g.shape[2:], g.dtype) for g in gs]
    return _hbm_call(body, name, gs, outs, n)


class ExchangeOverIci:
    def __init__(self, xs):
        self.ins = list(xs)
        self.outs = [jax.ShapeDtypeStruct(v.shape, v.dtype) for v in xs]
        self.aliases = {}
        self.n_sems = 3 * len(xs)

    def _copies(self, x_refs, o_refs, send_sems, recv_sems):
        x, y, c = _my_pos()
        s = 2 * x + y
        out = []
        for a in range(len(x_refs)):
            for k, (tx, ty) in enumerate([(1 - x, y), (x, 1 - y), (1 - x, 1 - y)]):
                got = o_refs[a].at[2 * tx + ty]
                out.append((_remote(x_refs[a].at[2 * tx + ty], o_refs[a].at[s], send_sems, recv_sems, 3 * a + k, (tx, ty, c)),
                            _remote(got, got, send_sems, recv_sems, 3 * a + k, (tx, ty, c))))
        return out

    def start(self, x_refs, o_refs, send_sems, recv_sems):
        for mine, _ in self._copies(x_refs, o_refs, send_sems, recv_sems):
            mine.start()

    def wait(self, x_refs, o_refs, send_sems, recv_sems):
        for mine, theirs in self._copies(x_refs, o_refs, send_sems, recv_sems):
            theirs.wait_recv()
            mine.wait_send()


def halves_gather(xs, name):
    n = len(xs)

    def body(*refs):
        x_refs, o_refs, send_sems, recv_sems = refs[:n], refs[n:2 * n], refs[2 * n], refs[2 * n + 1]
        x, y, c = _my_pos()
        sends = []
        for a in range(n):
            cp = _remote(_layers_half(x_refs[a], c), _layers_half(o_refs[a], c), send_sems, recv_sems, a, (x, y, 1 - c))
            cp.start()
            sends.append(cp)
        for a in range(n):
            theirs = _layers_half(o_refs[a], 1 - c)
            _remote(theirs, theirs, send_sems, recv_sems, a, (x, y, 1 - c)).wait_recv()
        for cp in sends:
            cp.wait_send()

    outs = [jax.ShapeDtypeStruct(v.shape, v.dtype) for v in xs]
    return _hbm_call(body, name, xs, outs, n, aliases={a: a for a in range(n)})


def sum_leading(xs, name):
    n, r, c = xs.shape

    def body(x_ref, o_ref):
        acc = x_ref[0]
        for i in range(1, n):
            acc = acc + x_ref[i]
        o_ref[...] = acc

    return pl.pallas_call(body, name=name, out_shape=jax.ShapeDtypeStruct((r, c), xs.dtype))(xs)


def add_cast(a, b, out_dtype, name):
    n, r, c = a.shape
    tr = _pick(r, (512, 256, 128, 64, 32, 16))

    def body(a_ref, b_ref, o_ref):
        o_ref[...] = (a_ref[...].astype(F32) + b_ref[...].astype(F32)).astype(out_dtype)

    spec = pl.BlockSpec((1, tr, c), lambda i, j: (i, j, 0))
    return pl.pallas_call(body, name=name, grid=(n, r // tr), in_specs=[spec, spec], out_specs=spec,
                          out_shape=jax.ShapeDtypeStruct((n, r, c), out_dtype),
                          compiler_params=_params(("parallel", "parallel")))(a, b)


def sum_chips(parts, own, where, name):
    n, nl, r, c = parts.shape
    tr = _pick(r, (512, 256, 128, 64, 32, 16))

    def body(w_ref, p_ref, o_ref, out_ref):
        s = w_ref[0]
        acc = jnp.zeros(out_ref.shape, F32)
        for t in range(n):
            acc = acc + jnp.where(s == t, o_ref[0], p_ref[t]).astype(F32)
        out_ref[...] = acc

    grid_spec = pltpu.PrefetchScalarGridSpec(
        num_scalar_prefetch=1, grid=(nl, r // tr),
        in_specs=[pl.BlockSpec((n, 1, tr, c), lambda i, j, w: (0, i, j, 0)),
                  pl.BlockSpec((1, 1, tr, c), lambda i, j, w: (w[0], i, j, 0))],
        out_specs=pl.BlockSpec((1, tr, c), lambda i, j, w: (w[1] * nl + i, j, 0)))
    return pl.pallas_call(body, name=name, grid_spec=grid_spec, out_shape=jax.ShapeDtypeStruct((2 * nl, r, c), F32),
                          compiler_params=_params(("parallel", "parallel")))(where, parts, own)


def ada_forward(c_all, w_ada):
    nl, d, n = w_ada.shape
    nb = c_all.shape[0]

    def body(c_ref, w_ref, o_ref):
        cv = c_ref[...]
        ca = (cv * _sigmoid(cv)).astype(BF16)
        o_ref[0] = _nn(ca, w_ref[0].astype(BF16))

    return pl.pallas_call(
        body, name="ada_forward", grid=(nl,),
        in_specs=[pl.BlockSpec((nb, d), lambda l: (0, 0)), pl.BlockSpec((1, d, n), lambda l: (l, 0, 0))],
        out_specs=pl.BlockSpec((1, nb, n), lambda l: (l, 0, 0)),
        out_shape=jax.ShapeDtypeStruct((nl, nb, n), F32), compiler_params=_params(("parallel",)))(c_all, w_ada)


def ada_backward(c_all, dmod):
    nl, nb, n = dmod.shape
    d = c_all.shape[1]

    def body(c_ref, g_ref, o_ref):
        cv = c_ref[...]
        ca = (cv * _sigmoid(cv)).astype(BF16)
        o_ref[0] = _tn(ca, g_ref[0].astype(BF16))

    return pl.pallas_call(
        body, name="ada_backward", grid=(nl,),
        in_specs=[pl.BlockSpec((nb, d), lambda l: (0, 0)), pl.BlockSpec((1, nb, n), lambda l: (l, 0, 0))],
        out_specs=pl.BlockSpec((1, d, n), lambda l: (l, 0, 0)),
        out_shape=jax.ShapeDtypeStruct((nl, d, n), F32), compiler_params=_params(("parallel",)))(c_all, dmod)


def matmul_tn(a, b, name):
    k, m = a.shape
    n = b.shape[1]
    tm, tk = _pick(m, (512, 256, 128)), _pick(k, (512, 256, 128))
    tn = n if n * (tm * 8 + tk * 4) <= VMEM_LIMIT // 2 else _pick(n, (512, 256, 128))

    def body(a_ref, b_ref, o_ref):
        @pl.when(pl.program_id(2) == 0)
        def _():
            o_ref[...] = jnp.zeros_like(o_ref)

        o_ref[...] += _tn(a_ref[...], b_ref[...])

    return pl.pallas_call(
        body, name=name, grid=(m // tm, n // tn, k // tk),
        in_specs=[pl.BlockSpec((tk, tm), lambda i, j, kk: (kk, i)), pl.BlockSpec((tk, tn), lambda i, j, kk: (kk, j))],
        out_specs=pl.BlockSpec((tm, tn), lambda i, j, kk: (i, j)),
        out_shape=jax.ShapeDtypeStruct((m, n), F32),
        compiler_params=_params(("parallel", "parallel", "arbitrary")))(a, b)


def adamw(w, g, m, v, name):
    nl, r, c = w.shape
    tr = _pick(r, (512, 256, 128, 64, 32, 16, 8))

    def body(w_ref, g_ref, m_ref, v_ref, d_ref, mo_ref, vo_ref):
        gv = g_ref[...]
        mn = ADAM_B1 * m_ref[...] + (1.0 - ADAM_B1) * gv
        vn = ADAM_B2 * v_ref[...] + (1.0 - ADAM_B2) * (gv * gv)
        m_hat = mn / (1.0 - ADAM_B1 ** ADAM_STEP)
        v_hat = vn / (1.0 - ADAM_B2 ** ADAM_STEP)
        d_ref[...] = -ADAM_LR * (m_hat / (jnp.sqrt(v_hat) + ADAM_EPS) + ADAM_WD * w_ref[...])
        mo_ref[...] = mn
        vo_ref[...] = vn

    spec = pl.BlockSpec((1, tr, c), lambda l, i: (l, i, 0))
    out = jax.ShapeDtypeStruct((nl, r, c), F32)
    return pl.pallas_call(body, name=name, grid=(nl, r // tr), in_specs=[spec] * 4, out_specs=[spec] * 3,
                          out_shape=[out] * 3, compiler_params=_params(("parallel", "parallel")))(w, g, m, v)


def _rope(t, cos_t, sin_t):
    w = t.shape[1]
    lane = _lane(t.shape) & (HEAD_PAD - 1)
    first_half = (lane >= ROPE_LO) & (lane < ROPE_LO + ROPE_HALF)
    partner = jnp.where(first_half, pltpu.roll(t, w - ROPE_HALF, 1), pltpu.roll(t, ROPE_HALF, 1))
    return t * cos_t + partner * sin_t


def _rope_t(dt, cos_t, sin_t):
    w = dt.shape[1]
    lane = _lane(dt.shape) & (HEAD_PAD - 1)
    first_half = (lane >= ROPE_LO) & (lane < ROPE_LO + ROPE_HALF)
    ds = dt * sin_t
    partner = jnp.where(first_half, pltpu.roll(ds, w - ROPE_HALF, 1), pltpu.roll(ds, ROPE_HALF, 1))
    return dt * cos_t + partner


def _rms(xv, g):
    rstd = lax.rsqrt(jnp.mean(xv * xv, axis=-1, keepdims=True) + EPS)
    xh = xv * rstd
    return xh * g, xh, rstd


def _rms_bwd(dy, g, xh, rstd):
    dxh = dy * g
    dx = rstd * (dxh - xh * jnp.mean(dxh * xh, axis=-1, keepdims=True))
    return dx, jnp.sum(dy * xh, axis=0, keepdims=True)


def make_step(cfg):
    S, D, NZ = cfg.S, cfg.D, cfg.NZ
    FW, MW, QL, KVL, FH, MH = cfg.FW, cfg.MW, cfg.QL, cfg.KVL, cfg.FH, cfg.MH
    QW = MH * HEAD_PAD
    TM = _pick(S, (512, 256, 128))
    TQ = TK = _pick(S, (256, 128))
    n_tok = S // TM
    ZO = 3 * FW
    NZR = NZ - ZO
    misc_blk = (cfg.o_ms - ZO) // LANES
    FWD_UNROLL = 2
    FOX_SCALE = HEAD_DIM ** -0.5
    MLA_SCALE = (HEAD_DIM + ROPE_DIM) ** -0.5

    def tok(width, col=0):
        return pl.BlockSpec((TM, width), lambda i: (i, col))

    def const(shape):
        return pl.BlockSpec(shape, lambda i: tuple(0 for _ in shape))

    def layer(l, shape):
        return pl.BlockSpec((None,) + shape, lambda i: (l,) + tuple(0 for _ in shape))


    def ln_inproj(x, g, scale, shift, w_in, l):
        def body(x_ref, g_ref, sc_ref, sh_ref, w_ref, h_ref, z_ref, qkv_ref):
            y, _, _ = _rms(x_ref[...], g_ref[...])
            hb = (y * (1.0 + sc_ref[...]) + sh_ref[...]).astype(BF16)
            h_ref[...] = hb
            z = _nn(hb, w_ref[...])
            z_ref[...] = z[:, ZO:]
            qkv_ref[:, :FW] = (z[:, :FW] * FOX_SCALE).astype(BF16)
            qkv_ref[:, FW:] = z[:, FW:ZO].astype(BF16)

        return pl.pallas_call(
            body, name="ln_inproj", grid=(n_tok,),
            in_specs=[tok(D), const((1, D)), const((1, D)), const((1, D)), layer(l, (D, NZ))],
            out_specs=[tok(D), tok(NZR), tok(ZO)],
            out_shape=[jax.ShapeDtypeStruct((S, D), BF16), jax.ShapeDtypeStruct((S, NZR), F32),
                       jax.ShapeDtypeStruct((S, ZO), BF16)],
            compiler_params=_params(("parallel",)))(x, g, scale, shift, w_in)

    def _log_f_terms(misc, bf):
        lane = _lane(misc.shape)
        a = misc + bf
        e = jnp.exp(-jnp.abs(a))
        logf = jnp.minimum(a, 0.0) - jnp.log(1.0 + e)
        sig_neg = jnp.where(a >= 0, e, 1.0) / (1.0 + e)
        valid = lane < FH
        return jnp.where(valid, logf, 0.0), jnp.where(valid, sig_neg, 0.0)

    def fox_prep(z, bf_pad):
        def body(z_ref, b_ref, o_ref, carry):
            @pl.when(pl.program_id(0) == 0)
            def _():
                carry[...] = jnp.zeros_like(carry)

            logf, _ = _log_f_terms(z_ref[...], b_ref[...])
            row = lax.broadcasted_iota(jnp.int32, (TM, TM), 0)
            col = lax.broadcasted_iota(jnp.int32, (TM, TM), 1)
            tri = (col <= row).astype(F32)
            cum = jnp.dot(tri, logf, precision=lax.Precision.HIGHEST, preferred_element_type=F32) + carry[...]
            o_ref[...] = cum
            carry[...] = cum[TM - 1:TM, :]

        return pl.pallas_call(
            body, name="fox_prep", grid=(n_tok,),
            in_specs=[tok(LANES, misc_blk), const((1, LANES))], out_specs=tok(LANES),
            out_shape=jax.ShapeDtypeStruct((S, LANES), F32),
            scratch_shapes=[pltpu.VMEM((1, LANES), F32)],
            compiler_params=_params(("arbitrary",)))(z, bf_pad)

    def mla_prep(z, gq, gkv, w_uq, w_uk, w_v, l, cos_t, sin_t):
        def body(ql_ref, kvl_ref, ms_ref, gq_ref, gkv_ref, wq_ref, wk_ref, wv_ref, c_ref, s_ref, q_ref, k_ref, v_ref):
            cos1, sin1 = c_ref[...], s_ref[...]
            qn, _, _ = _rms(ql_ref[...], gq_ref[...])
            q = _nn(qn.astype(BF16), wq_ref[...])
            q_ref[...] = (_rope(q, jnp.tile(cos1, (1, MH)), jnp.tile(sin1, (1, MH))) * MLA_SCALE).astype(BF16)
            kvn, _, _ = _rms(kvl_ref[...], gkv_ref[...])
            kvb = kvn.astype(BF16)
            lane = _lane((TM, LANES))
            kr = jnp.where((lane >= ROPE_LO) & (lane < ROPE_LO + ROPE_DIM), ms_ref[...], 0.0)
            kr = _rope(kr, cos1, sin1)
            k_ref[...] = (_nn(kvb, wk_ref[...]) + jnp.tile(kr, (1, MH))).astype(BF16)
            v_ref[...] = _nn(kvb, wv_ref[...]).astype(BF16)

        return pl.pallas_call(
            body, name="mla_prep", grid=(n_tok,),
            in_specs=[tok(QL, (cfg.o_ql - ZO) // QL), tok(KVL, (cfg.o_kv - ZO) // KVL), tok(LANES, misc_blk),
                      const((1, QL)), const((1, KVL)), layer(l, (QL, QW)), layer(l, (KVL, QW)), layer(l, (KVL, MW)),
                      tok(LANES), tok(LANES)],
            out_specs=[tok(QW), tok(QW), tok(MW)],
            out_shape=[jax.ShapeDtypeStruct((S, QW), BF16), jax.ShapeDtypeStruct((S, QW), BF16),
                       jax.ShapeDtypeStruct((S, MW), BF16)],
            compiler_params=_params(("parallel",)))(z, z, z, gq, gkv, w_uq, w_uk, w_v, cos_t, sin_t)

    def _allowed(q0, k0, chunked):
        qi = q0 + lax.broadcasted_iota(jnp.int32, (TQ, TK), 0)
        ki = k0 + lax.broadcasted_iota(jnp.int32, (TQ, TK), 1)
        if chunked:
            return (ki >> CHUNK_SHIFT) <= (qi >> CHUNK_SHIFT)
        return ki <= qi

    def _heads(val, packed):
        if packed:
            lane = _lane(val.shape)
            zero = jnp.zeros_like(val)
            return [jnp.where(lane < HEAD_DIM, val, zero), jnp.where(lane >= HEAD_DIM, val, zero)]
        return [val[:, :HEAD_PAD], val[:, HEAD_PAD:]]

    def _merge(a0, a1):
        return jnp.where(_lane(a0.shape) < HEAD_DIM, a0, a1)

    def attn_fwd(q, k, v, q_blk0, k_blk0, v_blk0, cumt, packed, chunked, name, side=None):
        has_bias = cumt is not None
        n_pairs = (FH if packed else MH) // 2
        wq = LANES if packed else 2 * HEAD_PAD
        assert TQ == TK

        n_main = 4 if has_bias else 3
        n_side_in, n_side_out = (len(side.ins), len(side.outs)) if side else (0, 0)

        def body(*refs):
            if has_bias:
                q_ref, k_ref, v_ref, ct_ref = refs[:n_main]
            else:
                q_ref, k_ref, v_ref = refs[:n_main]
            o_ref, lse_ref = refs[n_main + n_side_in:n_main + n_side_in + 2]
            i = pl.program_id(1)
            if side:
                side_refs = (refs[n_main:n_main + n_side_in], refs[n_main + n_side_in + 2:n_main + n_side_in + 2 + n_side_out],
                             refs[-2], refs[-1])

                @pl.when((pl.program_id(0) == 0) & (i == 0))
                def _():
                    side.start(*side_refs)

            q0 = i * TQ
            qh = _heads(q_ref[...], packed)
            q_both = jnp.concatenate(qh, axis=0) if packed else None

            def scores(kb):
                kblk = k_ref[pl.ds(pl.multiple_of(kb * TK, TK), TK), :]
                if packed:
                    return _nt(kblk, q_both)
                return jnp.concatenate([_nt(kblk[:, :HEAD_PAD], qh[0]), _nt(kblk[:, HEAD_PAD:], qh[1])], axis=1)

            def update(kb, s, carry, masked):
                m, l, acc = carry
                k0 = pl.multiple_of(kb * TK, TK)
                s0, s1 = s[:, :TQ], s[:, TQ:]
                if has_bias:
                    ck = jnp.transpose(ct_ref[0, kb])
                    s0, s1 = s0 - ck[:, 0:1], s1 - ck[:, 1:2]
                if masked:
                    allow = jnp.transpose(_allowed(q0, k0, chunked))
                    s0, s1 = jnp.where(allow, s0, NEG), jnp.where(allow, s1, NEG)
                s = jnp.concatenate([s0, s1], axis=1)
                m_new = jnp.maximum(m, jnp.max(s, axis=0, keepdims=True))
                p = jnp.exp(s - m_new)
                alpha = jnp.exp(m - m_new)
                l = alpha * l + jnp.sum(p, axis=0, keepdims=True)
                acc = alpha * acc + _tn(v_ref[pl.ds(k0, TK), :], p.astype(BF16))
                return m_new, l, acc

            def several(kb, carry, n, last_masked):
                ss = [scores(kb + u) for u in range(n)]
                for u in range(n):
                    carry = update(kb + u, ss[u], carry, last_masked and u == n - 1)
                return carry

            init = (jnp.full((1, 2 * TQ), NEG, F32), jnp.zeros((1, 2 * TQ), F32), jnp.zeros((LANES, 2 * TQ), F32))
            carry = lax.fori_loop(0, i // FWD_UNROLL, lambda t, cr: several(FWD_UNROLL * t, cr, FWD_UNROLL, False), init)
            m, l, acc = lax.switch(i % FWD_UNROLL,
                                   [lambda cr, r=r: several(i - r, cr, r + 1, True) for r in range(FWD_UNROLL)], carry)
            o_t = acc / l
            o_ref[...] = jnp.transpose(jnp.concatenate([o_t[:HEAD_DIM, :TQ], o_t[HEAD_DIM:, TQ:]], axis=0))
            lse = m + jnp.log(l)
            sub = lax.broadcasted_iota(jnp.int32, (LANES, TQ), 0)
            lse_ref[...] = jnp.transpose(jnp.where(sub == 0, lse[:, :TQ], jnp.where(sub == 1, lse[:, TQ:], 0.0)))
            if side:
                @pl.when((pl.program_id(0) == n_pairs - 1) & (i == S // TQ - 1))
                def _():
                    side.wait(*side_refs)

        in_specs = [pl.BlockSpec((TQ, wq), lambda p, i: (i, q_blk0 + p)),
                    pl.BlockSpec((S, wq), lambda p, i: (0, k_blk0 + p)),
                    pl.BlockSpec((S, LANES), lambda p, i: (0, v_blk0 + p))]
        args = [q, k, v]
        if has_bias:
            in_specs += [pl.BlockSpec((1, S // TK, 8, TK), lambda p, i: (p, 0, 0, 0))]
            args += [cumt]
        out_specs = [pl.BlockSpec((TQ, LANES), lambda p, i: (i, p)), pl.BlockSpec((TQ, LANES), lambda p, i: (i, p))]
        out_shape = [jax.ShapeDtypeStruct((S, n_pairs * LANES), F32), jax.ShapeDtypeStruct((S, n_pairs * LANES), F32)]
        extra = {}
        if side:
            hbm = pl.BlockSpec(memory_space=pl.ANY)
            in_specs += [hbm] * n_side_in
            args += side.ins
            out_specs += [hbm] * n_side_out
            out_shape += side.outs
            extra = dict(scratch_shapes=[pltpu.SemaphoreType.DMA((side.n_sems,)), pltpu.SemaphoreType.DMA((side.n_sems,))],
                         input_output_aliases={n_main + a: 2 + b for a, b in side.aliases.items()})
        outs = pl.pallas_call(
            body, name=name, grid=(n_pairs, S // TQ), in_specs=in_specs, out_specs=out_specs, out_shape=out_shape,
            compiler_params=_params(("arbitrary", "arbitrary")), **extra)(*args)
        return (outs[0], outs[1], list(outs[2:])) if side else outs

    def gate_outproj(of, om, z, w_out, l, x, gate):
        def body(of_ref, om_ref, fg_ref, mg_ref, w_ref, x_ref, gt_ref, xn_ref, u_ref, y_ref):
            fg, mg = fg_ref[...], mg_ref[...]
            u = jnp.concatenate([of_ref[...] * fg * _sigmoid(fg), om_ref[...] * mg * _sigmoid(mg)], axis=1).astype(BF16)
            y = _nn(u, w_ref[...])
            u_ref[...] = u
            y_ref[...] = y.astype(BF16)
            xn_ref[...] = x_ref[...] + gt_ref[...] * y

        return pl.pallas_call(
            body, name="gate_outproj", grid=(n_tok,),
            in_specs=[tok(FW), tok(MW), tok(FW, (cfg.o_fg - ZO) // FW), tok(MW, (cfg.o_mg - ZO) // MW),
                      layer(l, (FW + MW, D)), tok(D), const((1, D))],
            out_specs=[tok(D), tok(FW + MW), tok(D)],
            out_shape=[jax.ShapeDtypeStruct((S, D), F32), jax.ShapeDtypeStruct((S, FW + MW), BF16),
                       jax.ShapeDtypeStruct((S, D), BF16)],
            compiler_params=_params(("parallel",)))(of, om, z, z, w_out, x, gate)

    def final_loss(x, g, target):
        def body(x_ref, g_ref, t_ref, dx_ref, acc_ref, loss_ref):
            @pl.when(pl.program_id(0) == 0)
            def _():
                acc_ref[...] = jnp.zeros_like(acc_ref)
                loss_ref[...] = jnp.zeros_like(loss_ref)

            gv = g_ref[...]
            y, xh, rstd = _rms(x_ref[...], gv)
            e = y - t_ref[...]
            loss_ref[...] += 0.5 * jnp.sum(jnp.sum(e * e, axis=-1, keepdims=True) / D, axis=0, keepdims=True)
            dx, dg = _rms_bwd(e / D, gv, xh, rstd)
            dx_ref[...] = dx
            acc_ref[0:1, :] += dg

        return pl.pallas_call(
            body, name="final_loss", grid=(n_tok,),
            in_specs=[tok(D), const((1, D)), tok(D)],
            out_specs=[tok(D), const((8, D)), const((1, LANES))],
            out_shape=[jax.ShapeDtypeStruct((S, D), F32), jax.ShapeDtypeStruct((8, D), F32),
                       jax.ShapeDtypeStruct((1, LANES), F32)],
            compiler_params=_params(("arbitrary",)))(x, g, target)


    def bwd_out(dxn, gate, y, w_out, l, of, om, z, lse_f, lse_m):
        def body(dx_ref, gt_ref, y_ref, w_ref, of_ref, om_ref, fg_ref, mg_ref, lf_ref, lm_ref,
                 dof_ref, dom_ref, dfg_ref, dmg_ref, dy_ref, acc_ref, sf_ref, sm_ref):
            @pl.when(pl.program_id(0) == 0)
            def _():
                acc_ref[...] = jnp.zeros_like(acc_ref)

            dxv = dx_ref[...]
            acc_ref[0:1, :] += jnp.sum(dxv * y_ref[...].astype(F32), axis=0, keepdims=True)
            dy = (gt_ref[...] * dxv).astype(BF16)
            dy_ref[...] = dy
            du = _nt(dy, w_ref[...])
            lane = _lane((TM, LANES))
            for lo, width, o_ref, g_ref, do_ref, dg_ref, l_ref, st_ref in (
                    (0, FW, of_ref, fg_ref, dof_ref, dfg_ref, lf_ref, sf_ref),
                    (FW, MW, om_ref, mg_ref, dom_ref, dmg_ref, lm_ref, sm_ref)):
                gv, ov = g_ref[...], o_ref[...]
                sg = _sigmoid(gv)
                dup = du[:, lo:lo + width]
                dob = (dup * gv * sg).astype(BF16)
                do_ref[...] = dob
                dg_ref[...] = (dup * ov * sg * (1.0 + gv * (1.0 - sg))).astype(BF16)
                d = dob.astype(F32) * ov
                for pr in range(width // LANES):
                    cols = slice(pr * LANES, (pr + 1) * LANES)
                    dp = d[:, cols]
                    d0 = jnp.sum(jnp.where(lane < HEAD_DIM, dp, 0.0), axis=-1, keepdims=True)
                    d1 = jnp.sum(jnp.where(lane >= HEAD_DIM, dp, 0.0), axis=-1, keepdims=True)
                    st_ref[:, cols] = jnp.where(lane == 2, d0, jnp.where(lane == 3, d1, l_ref[:, cols]))

        return pl.pallas_call(
            body, name="bwd_out", grid=(n_tok,),
            in_specs=[tok(D), const((1, D)), tok(D), layer(l, (FW + MW, D)), tok(FW), tok(MW),
                      tok(FW, (cfg.o_fg - ZO) // FW), tok(MW, (cfg.o_mg - ZO) // MW), tok(FW), tok(MW)],
            out_specs=[tok(FW), tok(MW), tok(FW), tok(MW), tok(D), const((8, D)), tok(FW), tok(MW)],
            out_shape=[jax.ShapeDtypeStruct((S, FW), BF16), jax.ShapeDtypeStruct((S, MW), BF16),
                       jax.ShapeDtypeStruct((S, FW), BF16), jax.ShapeDtypeStruct((S, MW), BF16),
                       jax.ShapeDtypeStruct((S, D), BF16), jax.ShapeDtypeStruct((8, D), F32),
                       jax.ShapeDtypeStruct((S, FW), F32), jax.ShapeDtypeStruct((S, MW), F32)],
            compiler_params=_params(("arbitrary",)))(dxn, gate, y, w_out, of, om, z, z, lse_f, lse_m)

    def attn_bwd(q, k, v, do, stats, q_blk0, k_blk0, v_blk0, cumt, packed, chunked, name, side=None):
        has_bias = cumt is not None
        n_pairs = (FH if packed else MH) // 2
        wq = LANES if packed else 2 * HEAD_PAD
        n_q = S // TQ
        assert TQ == TK
        n_in, n_out = (6, 5) if has_bias else (5, 3)
        n_side_in, n_side_out = (len(side.ins), len(side.outs)) if side else (0, 0)

        def body(*refs):
            main_out = refs[n_in + n_side_in:n_in + n_side_in + n_out]
            if has_bias:
                q_ref, k_ref, v_ref, do_ref, st_ref, ct_ref = refs[:n_in]
                dq_ref, dk_ref, dv_ref, dc_ref, dr_ref = main_out
            else:
                q_ref, k_ref, v_ref, do_ref, st_ref = refs[:n_in]
                dq_ref, dk_ref, dv_ref = main_out
            jb = pl.program_id(1)
            k0 = jb * TK
            if side:
                side_refs = (refs[n_in:n_in + n_side_in], refs[n_in + n_side_in + n_out:n_in + n_side_in + n_out + n_side_out],
                             refs[-2], refs[-1])

                @pl.when((pl.program_id(0) == 0) & (jb == 0))
                def _():
                    side.start(*side_refs)

            @pl.when(jb == 0)
            def _():
                dq_ref[...] = jnp.zeros_like(dq_ref)
                if has_bias:
                    dr_ref[...] = jnp.zeros_like(dr_ref)

            dk_ref[...] = jnp.zeros_like(dk_ref)
            dv_ref[...] = jnp.zeros_like(dv_ref)
            kh = _heads(k_ref[...], packed)
            k_both = jnp.concatenate(kh, axis=0) if packed else None
            v_both = jnp.concatenate(_heads(v_ref[...], True), axis=0)
            if has_bias:
                ct = jnp.transpose(ct_ref[0, 0])
                ck = jnp.concatenate([ct[:, 0:1], ct[:, 1:2]], axis=0)

            def products(ib):
                rows = pl.ds(pl.multiple_of(ib * TQ, TQ), TQ)
                q2, do2 = q_ref[rows, :], do_ref[rows, :]
                if packed:
                    s = _nt(k_both, q2)
                else:
                    qh = _heads(q2, False)
                    s = jnp.concatenate([_nt(kh[0], qh[0]), _nt(kh[1], qh[1])], axis=0)
                return s, _nt(v_both, do2)

            def update(ib, s, dp, carry, masked):
                q0 = pl.multiple_of(ib * TQ, TQ)
                rows = pl.ds(q0, TQ)
                q2, do2 = q_ref[rows, :], do_ref[rows, :]
                st = jnp.transpose(st_ref[rows, :])
                if not packed:
                    qh = _heads(q2, False)
                if has_bias:
                    s = s - ck
                if masked:
                    allow = jnp.transpose(_allowed(q0, k0, chunked))
                    s = jnp.where(jnp.concatenate([allow, allow], axis=0), s, NEG)
                p = jnp.concatenate([jnp.exp(s[:TK] - st[0:1, :]), jnp.exp(s[TK:] - st[1:2, :])], axis=0)
                dv2 = _nn(p.astype(BF16), do2)
                ds = jnp.concatenate([p[:TK] * (dp[:TK] - st[2:3, :]), p[TK:] * (dp[TK:] - st[3:4, :])], axis=0)
                dsb = ds.astype(BF16)
                dv_ref[...] += _merge(dv2[:TK], dv2[TK:])
                if packed:
                    dk2 = _nn(dsb, q2)
                    dk_ref[...] += _merge(dk2[:TK], dk2[TK:])
                    dq_ref[rows, :] += _tn(dsb, k_both)
                else:
                    dk_ref[...] += jnp.concatenate([_nn(dsb[:TK], qh[0]), _nn(dsb[TK:], qh[1])], axis=1)
                    dq_ref[rows, :] += jnp.concatenate([_tn(dsb[:TK], kh[0]), _tn(dsb[TK:], kh[1])], axis=1)
                if has_bias:
                    sub = lax.broadcasted_iota(jnp.int32, (8, TQ), 0)
                    r0 = jnp.sum(ds[:TK], axis=0, keepdims=True)
                    r1 = jnp.sum(ds[TK:], axis=0, keepdims=True)
                    dr_ref[0, ib] += jnp.where(sub == 0, r0, jnp.where(sub == 1, r1, 0.0))
                    return carry - jnp.sum(ds, axis=1, keepdims=True)
                return carry

            def step(ib, carry, masked):
                s, dp = products(ib)
                return update(ib, s, dp, carry, masked)

            def two(ib, carry):
                s_a, dp_a = products(ib)
                s_b, dp_b = products(ib + 1)
                return update(ib + 1, s_b, dp_b, update(ib, s_a, dp_a, carry, False), False)

            dc = step(jb, jnp.zeros((2 * TK, 1), F32), True)
            n_rest = n_q - 1 - jb
            dc = lax.fori_loop(0, n_rest // 2, lambda t, cr: two(jb + 1 + 2 * t, cr), dc)
            dc = lax.cond(n_rest % 2 == 1, lambda cr: step(n_q - 1, cr, False), lambda cr: cr, dc)
            if has_bias:
                lane = _lane((TK, LANES))
                dc_ref[0] = jnp.where(lane == 0, dc[:TK], jnp.where(lane == 1, dc[TK:], 0.0))
            if side:
                @pl.when((pl.program_id(0) == n_pairs - 1) & (jb == S // TK - 1))
                def _():
                    side.wait(*side_refs)

        in_specs = [pl.BlockSpec((S, wq), lambda p, j: (0, q_blk0 + p)),
                    pl.BlockSpec((TK, wq), lambda p, j: (j, k_blk0 + p)),
                    pl.BlockSpec((TK, LANES), lambda p, j: (j, v_blk0 + p)),
                    pl.BlockSpec((S, LANES), lambda p, j: (0, p)),
                    pl.BlockSpec((S, LANES), lambda p, j: (0, p))]
        args = [q, k, v, do, stats]
        out_specs = [pl.BlockSpec((S, wq), lambda p, j: (0, p)),
                     pl.BlockSpec((TK, wq), lambda p, j: (j, p)),
                     pl.BlockSpec((TK, LANES), lambda p, j: (j, p))]
        out_shape = [jax.ShapeDtypeStruct((S, n_pairs * wq), F32), jax.ShapeDtypeStruct((S, n_pairs * wq), F32),
                     jax.ShapeDtypeStruct((S, n_pairs * LANES), F32)]
        if has_bias:
            in_specs += [pl.BlockSpec((1, 1, 8, TK), lambda p, j: (p, j, 0, 0))]
            args += [cumt]
            out_specs += [pl.BlockSpec((1, TK, LANES), lambda p, j: (p, j, 0)),
                          pl.BlockSpec((1, S // TQ, 8, TQ), lambda p, j: (p, 0, 0, 0))]
            out_shape += [jax.ShapeDtypeStruct((n_pairs, S, LANES), F32),
                          jax.ShapeDtypeStruct((n_pairs, S // TQ, 8, TQ), F32)]
        extra = {}
        if side:
            hbm = pl.BlockSpec(memory_space=pl.ANY)
            in_specs += [hbm] * n_side_in
            args += side.ins
            out_specs += [hbm] * n_side_out
            out_shape += side.outs
            extra = dict(scratch_shapes=[pltpu.SemaphoreType.DMA((side.n_sems,)), pltpu.SemaphoreType.DMA((side.n_sems,))],
                         input_output_aliases={n_in + a: n_out + b for a, b in side.aliases.items()})
        outs = pl.pallas_call(
            body, name=name, grid=(n_pairs, S // TK), in_specs=in_specs, out_specs=out_specs, out_shape=out_shape,
            compiler_params=_params(("arbitrary", "arbitrary")), **extra)(*args)
        return (*outs[:n_out], list(outs[n_out:])) if side else outs

    def fox_post(dcum, z, bf_pad):
        def rev(i):
            return n_tok - 1 - i

        def body(dc_ref, z_ref, b_ref, dff_ref, acc_ref, carry):
            @pl.when(pl.program_id(0) == 0)
            def _():
                carry[...] = jnp.zeros_like(carry)
                acc_ref[...] = jnp.zeros_like(acc_ref)

            _, sig_neg = _log_f_terms(z_ref[...], b_ref[...])
            row = lax.broadcasted_iota(jnp.int32, (TM, TM), 0)
            col = lax.broadcasted_iota(jnp.int32, (TM, TM), 1)
            tri = (col >= row).astype(F32)
            dlog = jnp.dot(tri, dc_ref[...], precision=lax.Precision.HIGHEST, preferred_element_type=F32) + carry[...]
            carry[...] = dlog[0:1, :]
            dff = dlog * sig_neg
            dff_ref[...] = dff
            acc_ref[0:1, :] += jnp.sum(dff, axis=0, keepdims=True)

        return pl.pallas_call(
            body, name="fox_post", grid=(n_tok,),
            in_specs=[pl.BlockSpec((TM, LANES), lambda i: (rev(i), 0)),
                      pl.BlockSpec((TM, LANES), lambda i: (rev(i), misc_blk)), const((1, LANES))],
            out_specs=[pl.BlockSpec((TM, LANES), lambda i: (rev(i), 0)), const((8, LANES))],
            out_shape=[jax.ShapeDtypeStruct((S, LANES), F32), jax.ShapeDtypeStruct((8, LANES), F32)],
            scratch_shapes=[pltpu.VMEM((1, LANES), F32)],
            compiler_params=_params(("arbitrary",)))(dcum, z, bf_pad)

    def mla_post(dq, dk, dv, dff, z, gq, gkv, w_uq, w_uk, w_v, l, cos_t, sin_t):
        def body(dq_ref, dk_ref, dv_ref, dff_ref, ql_ref, kvl_ref, gq_ref, gkv_ref, wq_ref, wk_ref, wv_ref,
                 c_ref, s_ref, zq_ref, zkv_ref, zms_ref, dwq_ref, dwk_ref, dwv_ref, dgq_ref, dgkv_ref):
            @pl.when(pl.program_id(0) == 0)
            def _():
                for r in (dwq_ref, dwk_ref, dwv_ref, dgq_ref, dgkv_ref):
                    r[...] = jnp.zeros_like(r)

            cos1, sin1 = c_ref[...], s_ref[...]
            gqv, gkvv = gq_ref[...], gkv_ref[...]
            qn, qxh, qrstd = _rms(ql_ref[...], gqv)
            dq_pre = _rope_t(dq_ref[...] * MLA_SCALE, jnp.tile(cos1, (1, MH)), jnp.tile(sin1, (1, MH))).astype(BF16)
            dwq_ref[...] += _tn(qn.astype(BF16), dq_pre)
            dql, dgq = _rms_bwd(_nt(dq_pre, wq_ref[...]), gqv, qxh, qrstd)
            zq_ref[...] = dql.astype(BF16)
            dgq_ref[0:1, :] += dgq

            dkv = dk_ref[...]
            lane = _lane(dkv.shape) & (HEAD_PAD - 1)
            dkn = jnp.where(lane < HEAD_DIM, dkv, 0.0).astype(BF16)
            dkr = dkv[:, 0:HEAD_PAD]
            for hd in range(1, MH):
                dkr = dkr + dkv[:, hd * HEAD_PAD:(hd + 1) * HEAD_PAD]
            lane1 = _lane(dkr.shape)
            dkr = jnp.where((lane1 >= ROPE_LO) & (lane1 < ROPE_LO + ROPE_DIM), dkr, 0.0)
            dkr = _rope_t(dkr, cos1, sin1)
            zms_ref[...] = (dkr + dff_ref[...]).astype(BF16)

            kvn, kxh, krstd = _rms(kvl_ref[...], gkvv)
            kvb = kvn.astype(BF16)
            dvb = dv_ref[...].astype(BF16)
            dwk_ref[...] += _tn(kvb, dkn)
            dwv_ref[...] += _tn(kvb, dvb)
            dkvl, dgkv = _rms_bwd(_nt(dkn, wk_ref[...]) + _nt(dvb, wv_ref[...]), gkvv, kxh, krstd)
            zkv_ref[...] = dkvl.astype(BF16)
            dgkv_ref[0:1, :] += dgkv

        return pl.pallas_call(
            body, name="mla_post", grid=(n_tok,),
            in_specs=[tok(QW), tok(QW), tok(MW), tok(LANES), tok(QL, (cfg.o_ql - ZO) // QL), tok(KVL, (cfg.o_kv - ZO) // KVL),
                      const((1, QL)), const((1, KVL)), layer(l, (QL, QW)), layer(l, (KVL, QW)), layer(l, (KVL, MW)),
                      tok(LANES), tok(LANES)],
            out_specs=[tok(QL), tok(KVL), tok(LANES), const((QL, QW)), const((KVL, QW)), const((KVL, MW)),
                       const((8, QL)), const((8, KVL))],
            out_shape=[jax.ShapeDtypeStruct((S, QL), BF16), jax.ShapeDtypeStruct((S, KVL), BF16),
                       jax.ShapeDtypeStruct((S, LANES), BF16), jax.ShapeDtypeStruct((QL, QW), F32),
                       jax.ShapeDtypeStruct((KVL, QW), F32), jax.ShapeDtypeStruct((KVL, MW), F32),
                       jax.ShapeDtypeStruct((8, QL), F32), jax.ShapeDtypeStruct((8, KVL), F32)],
            compiler_params=_params(("arbitrary",)))(dq, dk, dv, dff, z, z, gq, gkv, w_uq, w_uk, w_v, cos_t, sin_t)

    def bwd_in(dz, w_in, l, x, dxn, g, scale):
        def body(dz_ref, w_ref, x_ref, dx_ref, g_ref, sc_ref, o_ref, acc_ref):
            @pl.when(pl.program_id(0) == 0)
            def _():
                acc_ref[...] = jnp.zeros_like(acc_ref)

            dh = _nt(dz_ref[...], w_ref[...])
            gv, mod = g_ref[...], 1.0 + sc_ref[...]
            _, xh, rstd = _rms(x_ref[...], gv)
            t = dh * xh
            acc_ref[0:1, :] += jnp.sum(dh, axis=0, keepdims=True)
            acc_ref[1:2, :] += jnp.sum(t * gv, axis=0, keepdims=True)
            acc_ref[2:3, :] += jnp.sum(t * mod, axis=0, keepdims=True)
            dx, _ = _rms_bwd(dh, gv * mod, xh, rstd)
            o_ref[...] = dx_ref[...] + dx

        return pl.pallas_call(
            body, name="bwd_in", grid=(n_tok,),
            in_specs=[tok(NZ), layer(l, (D, NZ)), tok(D), tok(D), const((1, D)), const((1, D))],
            out_specs=[tok(D), const((8, D))],
            out_shape=[jax.ShapeDtypeStruct((S, D), F32), jax.ShapeDtypeStruct((8, D), F32)],
            compiler_params=_params(("arbitrary",)))(dz, w_in, x, dxn, g, scale)


    def pair_rows(cum):
        n_pairs = FH // 2
        ct = jnp.pad(cum[:, :FH].T.reshape(n_pairs, 2, S), ((0, 0), (0, 6), (0, 0)))
        return ct.reshape(n_pairs, 8, S // TK, TK).transpose(0, 2, 1, 3)

    def bias_grad(dc, dr):
        n_pairs = FH // 2
        d = dr.transpose(0, 2, 1, 3).reshape(n_pairs, 8, S)[:, :2, :].reshape(FH, S).T
        d = d + dc[:, :, :2].transpose(1, 0, 2).reshape(S, FH)
        return jnp.pad(d, ((0, 0), (0, LANES - FH)))

    def layer_forward(x, wl, mod, later_shards=None):
        shift, scale, gate = mod
        h, z, qkv = ln_inproj(x, wl.norm_g, scale, shift, wl.w_in, wl.l)
        ct = pair_rows(fox_prep(z, wl.bf_pad))
        fox = attn_fwd(qkv, qkv, qkv, cfg.o_fq // LANES, cfg.o_fk // LANES, cfg.o_fv // LANES, ct, True, False,
                       "fox_fwd_gather" if later_shards else "fox_fwd", GatherOverIci(later_shards) if later_shards else None)
        of, lse_f = fox[0], fox[1]
        qp, kp, vp = mla_prep(z, wl.gq, wl.gkv, wl.w_uq, wl.w_uk, wl.w_v, wl.l, cfg.cos_t, cfg.sin_t)
        mla = attn_fwd(qp, kp, vp, 0, 0, 0, None, False, True, "mla_fwd_gather" if later_shards else "mla_fwd",
                       GatherToSibling(fox[2]) if later_shards else None)
        om, lse_m = mla[0], mla[1]
        xn, u, y = gate_outproj(of, om, z, wl.w_out, wl.l, x, gate)
        saved = types.SimpleNamespace(x=x, h=h, z=z, qkv=qkv, ct=ct, of=of, lse_f=lse_f, qp=qp, kp=kp, vp=vp,
                                      om=om, lse_m=lse_m, u=u, y=y)
        return (xn, saved, mla[2]) if later_shards else (xn, saved)

    def layer_backward(dxn, sv, wl, mod, side=None):
        shift, scale, gate = mod
        do_f, do_m, dfg, dmg, dy, acc_o, st_f, st_m = bwd_out(dxn, gate, sv.y, wl.w_out, wl.l, sv.of, sv.om, sv.z,
                                                              sv.lse_f, sv.lse_m)
        dw_out = matmul_tn(sv.u, dy, "dw_out")
        dfq, dfk, dfv, dck, dcr, *rode = attn_bwd(sv.qkv, sv.qkv, sv.qkv, do_f, st_f, cfg.o_fq // LANES, cfg.o_fk // LANES,
                                                  cfg.o_fv // LANES, sv.ct, True, False,
                                                  "fox_bwd_exchange" if side else "fox_bwd", side)
        dff, acc_f = fox_post(bias_grad(dck, dcr), sv.z, wl.bf_pad)
        dqp, dkp, dvp = attn_bwd(sv.qp, sv.kp, sv.vp, do_m, st_m, 0, 0, 0, None, False, True, "mla_bwd")
        zq, zkv, zms, dw_uq, dw_uk, dw_v, dgq, dgkv = mla_post(
            dqp, dkp, dvp, dff, sv.z, wl.gq, wl.gkv, wl.w_uq, wl.w_uk, wl.w_v, wl.l, cfg.cos_t, cfg.sin_t)
        dz = jnp.concatenate([(dfq * FOX_SCALE).astype(BF16), dfk.astype(BF16), dfv.astype(BF16), dfg, dmg, zq, zkv, zms], axis=1)
        dx, acc_i = bwd_in(dz, wl.w_in, wl.l, sv.x, dxn, wl.norm_g, scale)
        dw_in = matmul_tn(sv.h, dz, "dw_in")
        grads = types.SimpleNamespace(
            w_in=dw_in, w_out=dw_out, w_uq=dw_uq, w_uk=dw_uk, w_v=dw_v, gq=dgq[0], gkv=dgkv[0],
            b_f=acc_f[0, :FH], norm_g=acc_i[2], dmod=jnp.concatenate([acc_i[0], acc_i[1], acc_o[0]]))
        return (dx, grads, rode[0]) if side else (dx, grads)

    return types.SimpleNamespace(layer_forward=layer_forward, layer_backward=layer_backward, final_loss=final_loss)


def _pack_rows(parts, dtype, row_multiple):
    flat = jnp.concatenate([p.reshape(-1).astype(dtype) for p in parts])
    per = SLAB_COLS * row_multiple
    total = -(-flat.shape[0] // per) * per
    return jnp.pad(flat, (0, total - flat.shape[0])).reshape(total // SLAB_COLS, SLAB_COLS)


def _unpack(flat, shapes):
    out, off = [], 0
    for shp in shapes:
        n = 1
        for d in shp:
            n *= d
        out.append(flat[off:off + n].reshape(shp))
        off += n
    return out


def kernel(x, c, positions, norm_g, w_ada, b_ada, w_in, b_f, q_norm_g, w_uq, kv_norm_g, w_ukv, w_out, final_g, loss_target, m_norm_g, m_w_ada, m_b_ada, m_w_in, m_b_f, m_q_norm_g, m_w_uq, m_kv_norm_g, m_w_ukv, m_w_out, m_final_g, v_norm_g, v_w_ada, v_b_ada, v_w_in, v_b_f, v_q_norm_g, v_w_uq, v_kv_norm_g, v_w_ukv, v_w_out, v_final_g):
    S, D = x.shape[1], x.shape[2]
    L = norm_g.shape[0]
    FH = b_f.shape[1]
    QL, KVL = q_norm_g.shape[1], kv_norm_g.shape[1]
    MH = w_ukv.shape[2] * N_CHIPS // (2 * HEAD_DIM)
    FW, MW = FH * HEAD_DIM, MH * HEAD_DIM
    NA = w_ada.shape[2]
    n_in = w_in.shape[2] * N_CHIPS
    cfg = types.SimpleNamespace(S=S, D=D, FW=FW, MW=MW, QL=QL, KVL=KVL, FH=FH, MH=MH)
    cfg.o_fq, cfg.o_fk, cfg.o_fv, cfg.o_fg, cfg.o_mg = 0, FW, 2 * FW, 3 * FW, 4 * FW
    cfg.o_ql = 4 * FW + MW
    cfg.o_kv = cfg.o_ql + QL
    cfg.o_ms = cfg.o_kv + KVL
    cfg.NZ = cfg.o_ms + LANES
    assert FW == MW and FH % 2 == 0 and MH % 2 == 0 and cfg.o_ql % QL == 0 and cfg.o_kv % KVL == 0 and KVL == LANES
    assert n_in == 4 * FW + FH + QL + KVL + ROPE_DIM + MW

    mx, my, mc = _my_pos()
    my_chip = 2 * mx + my
    my_dev = 2 * my_chip + mc

    inv_freq = 1.0 / (ROPE_THETA ** (jnp.arange(0, ROPE_DIM, 2, dtype=F32) / ROPE_DIM))
    ang = positions[0].astype(F32)[:, None] * inv_freq
    cos, sin = jnp.cos(ang), jnp.sin(ang)
    cfg.cos_t = jnp.concatenate([jnp.ones((S, ROPE_LO), F32), cos, cos, jnp.ones((S, HEAD_PAD - ROPE_LO - ROPE_DIM), F32)], axis=1)
    cfg.sin_t = jnp.concatenate([jnp.zeros((S, ROPE_LO), F32), -sin, sin, jnp.zeros((S, HEAD_PAD - ROPE_LO - ROPE_DIM), F32)], axis=1)

    assert L % 4 == 0
    def lane_pad(a):
        return jnp.pad(a, ((0, 0),) * (a.ndim - 1) + ((0, -a.shape[-1] % LANES),))

    shards = (w_in, w_uq, w_ukv, w_out)
    padded = [lane_pad(w.astype(BF16)) for w in shards]

    def kernel_layouts(gathered, own):
        def all_chips(g, w, axis):
            return jnp.concatenate([jnp.where(my_chip == t, w.astype(BF16), g[t][..., :w.shape[-1]])
                                    for t in range(N_CHIPS)], axis=axis)

        n = own[0].shape[0]
        w_in_f = all_chips(gathered[0], own[0], 2)
        w_uq_f = all_chips(gathered[1], own[1], 2)
        w_ukv_f = all_chips(gathered[2], own[2], 2)
        w_out_f = all_chips(gathered[3], own[3], 1)
        sizes = (FW, FW, FW, FH, FW, QL, KVL, ROPE_DIM, MW)
        offs = [0]
        for sz in sizes:
            offs.append(offs[-1] + sz)
        fq_w, fk_w, fv_w, ff_w, fg_w, ql_w, kvl_w, kr_w, mg_w = [w_in_f[:, :, offs[i]:offs[i + 1]] for i in range(len(sizes))]
        zeros = lambda width: jnp.zeros((n, D, width), BF16)
        w_in_p = jnp.concatenate([fq_w, fk_w, fv_w, fg_w, mg_w, ql_w, kvl_w, ff_w, zeros(ROPE_LO - FH), kr_w,
                                  zeros(HEAD_PAD - ROPE_LO - ROPE_DIM)], axis=2)
        w_uq_p = jnp.pad(w_uq_f.reshape(n, QL, MH, HEAD_DIM + ROPE_DIM),
                         ((0, 0), (0, 0), (0, 0), (0, HEAD_PAD - HEAD_DIM - ROPE_DIM))).reshape(n, QL, MH * HEAD_PAD)
        w_ukv4 = w_ukv_f.reshape(n, KVL, MH, 2 * HEAD_DIM)
        w_uk_p = jnp.pad(w_ukv4[..., :HEAD_DIM], ((0, 0), (0, 0), (0, 0), (0, HEAD_PAD - HEAD_DIM))).reshape(n, KVL, MH * HEAD_PAD)
        return types.SimpleNamespace(w_in=w_in_p, w_uq=w_uq_p, w_uk=w_uk_p, w_v=w_ukv4[..., HEAD_DIM:].reshape(n, KVL, MW),
                                     w_out=w_out_f)

    first = kernel_layouts(weights_gather([p[:1] for p in padded], "weights_first"), [w[:1] for w in shards])

    c_all = allgather8(c.reshape(8, D // 8), "gather_c").reshape(N_DEV, D)
    c_pad = jnp.pad(c_all, ((0, 16 - N_DEV), (0, 0)))
    mod_part = ada_forward(c_pad, w_ada)[:, :N_DEV, :]
    mod_all = allgather8(mod_part.reshape(-1, LANES), "gather_mod").reshape(N_CHIPS, 2, L, N_DEV, NA)[:, 0]
    mod_full = mod_all.transpose(1, 2, 0, 3).reshape(L, N_DEV, N_CHIPS * NA) + b_ada[:, None, :]
    mod_mine = lax.dynamic_index_in_dim(mod_full, my_dev, axis=1, keepdims=True)

    step = make_step(cfg)
    bf_pad = jnp.pad(b_f, ((0, 0), (0, LANES - FH)))
    mods = [(mod_mine[l, :, :D], mod_mine[l, :, D:2 * D], mod_mine[l, :, 2 * D:]) for l in range(L)]

    def layer_params(l, ws_, at):
        return types.SimpleNamespace(l=at, norm_g=norm_g[l][None], bf_pad=bf_pad[l][None], gq=q_norm_g[l][None],
                                     gkv=kv_norm_g[l][None], w_in=ws_.w_in, w_uq=ws_.w_uq, w_uk=ws_.w_uk, w_v=ws_.w_v,
                                     w_out=ws_.w_out)

    layers = [layer_params(0, first, 0)]
    xl, sv, later = step.layer_forward(x[0], layers[0], mods[0], [p[1:] for p in padded])
    saved = [sv]
    rest = kernel_layouts(later, [w[1:] for w in shards])
    for l in range(1, L):
        layers.append(layer_params(l, rest, l - 1))
        xl, sv = step.layer_forward(xl, layers[l], mods[l])
        saved.append(sv)
    dx, acc_fin, loss_part = step.final_loss(xl, final_g[None], loss_target[0])
    loss = lax.psum(loss_part[0, 0], ("x", "y", "c"))
    ms = cfg.o_ms
    runs = [(0, 3 * FW, 0), (3 * FW, FH, ms), (3 * FW + FH, FW, cfg.o_fg), (4 * FW + FH, QL + KVL, cfg.o_ql),
            (4 * FW + FH + QL + KVL, ROPE_DIM, ms + ROPE_LO), (4 * FW + FH + QL + KVL + ROPE_DIM, MW, cfg.o_mg)]
    big_names = ["w_in", "w_uq", "w_ukv", "w_out"]

    def shard_columns(dw, t):
        a, b = t * w_in.shape[2], (t + 1) * w_in.shape[2]
        return jnp.concatenate([dw[:, p0 + max(a, r0) - r0:p0 + min(b, r0 + sz) - r0]
                                for r0, sz, p0 in runs if max(a, r0) < min(b, r0 + sz)], axis=1)

    def chip_parts(group, tag):
        n = len(group)
        stk = lambda name: jnp.stack([getattr(g, name) for g in group])
        g_in4 = jnp.stack([jnp.stack([shard_columns(g.w_in, t) for g in group]) for t in range(N_CHIPS)])
        dw_uq_f = stk("w_uq").reshape(n, QL, MH, HEAD_PAD)[..., :HEAD_DIM + ROPE_DIM].reshape(n, QL, -1)
        dw_ukv_f = jnp.concatenate([stk("w_uk").reshape(n, KVL, MH, HEAD_PAD)[..., :HEAD_DIM],
                                    stk("w_v").reshape(n, KVL, MH, HEAD_DIM)], axis=3).reshape(n, KVL, -1)
        gs = [g_in4, dw_uq_f.reshape(n, QL, N_CHIPS, -1).transpose(2, 0, 1, 3),
              dw_ukv_f.reshape(n, KVL, N_CHIPS, -1).transpose(2, 0, 1, 3), stk("w_out").reshape(n, N_CHIPS, -1, D).transpose(1, 0, 2, 3)]
        gs = [lane_pad(g.astype(BF16)) for g in gs]
        theirs = halves_to_sibling(gs, "grads_sibling_" + tag)
        out = []
        for g, o, nm in zip(gs, theirs, big_names):
            keep = lax.dynamic_slice_in_dim(g, mc * (n // 2), n // 2, axis=1)
            merged = (N_CHIPS * (n // 2),) + g.shape[2:]
            out.append(add_cast(keep.reshape(merged), o.reshape(merged), BF16, "grads_chip_sum_%s_%s" % (nm, tag)).reshape(o.shape))
        return out

    def reduced(chip_part, parts, tag):
        where = jnp.stack([my_chip, mc]).astype(jnp.int32)
        red = [sum_chips(p, o, where, "grads_sum_%s_%s" % (nm, tag)) for p, o, nm in zip(parts, chip_part, big_names)]
        return [g[..., :w.shape[-1]] for g, w in zip(halves_gather(red, "grads_back_" + tag), shards)]

    gl = [None] * L
    half_l = L // 2
    for l in range(L - 1, half_l - 1, -1):
        dx, gl[l] = step.layer_backward(dx, saved[l], layers[l], mods[l])
    part_hi = chip_parts(gl[half_l:], "hi")
    dx, gl[half_l - 1], parts_hi = step.layer_backward(dx, saved[half_l - 1], layers[half_l - 1], mods[half_l - 1],
                                                       ExchangeOverIci(part_hi))
    for l in range(half_l - 2, -1, -1):
        dx, gl[l] = step.layer_backward(dx, saved[l], layers[l], mods[l])
    part_lo = chip_parts(gl[:half_l], "lo")
    parts_lo = run_side(ExchangeOverIci(part_lo), "grads_chips_lo")
    g_w_in, g_w_uq, g_w_ukv, g_w_out = [jnp.concatenate([lo, hi]) for lo, hi in
                                        zip(reduced(part_lo, parts_lo, "lo"), reduced(part_hi, parts_hi, "hi"))]
    grad_x = dx[None]

    stack = lambda name: jnp.stack([getattr(g, name) for g in gl])
    small_parts = [stack("norm_g"), stack("dmod"), stack("b_f"), stack("gq"), stack("gkv"), acc_fin[0]]
    small_shapes = [p.shape for p in small_parts]
    small = _pack_rows(small_parts, F32, 8).reshape(-1, LANES)
    small_all = allgather8(small, "gather_small").reshape(N_DEV, -1, LANES)
    small_sum = sum_leading(small_all, "sum_small")
    g_norm_g, g_b_ada, g_b_f, g_q_norm_g, g_kv_norm_g, g_final_g = _unpack(small_sum.reshape(-1), small_shapes)

    n_ng = L * D
    dmod_all = small_all.reshape(N_DEV, -1)[:, n_ng:n_ng + L * 3 * D].reshape(N_DEV, L, 3 * D)
    dmod_cols = lax.dynamic_slice_in_dim(dmod_all, my_chip * NA, NA, axis=2).transpose(1, 0, 2)
    g_w_ada = ada_backward(c_pad, jnp.pad(dmod_cols, ((0, 0), (0, 16 - N_DEV), (0, 0))))


    names = ["norm_g", "w_ada", "b_ada", "w_in", "b_f", "q_norm_g", "w_uq", "kv_norm_g", "w_ukv", "w_out", "final_g"]
    ws = dict(norm_g=norm_g, w_ada=w_ada, b_ada=b_ada, w_in=w_in, b_f=b_f, q_norm_g=q_norm_g, w_uq=w_uq,
              kv_norm_g=kv_norm_g, w_ukv=w_ukv, w_out=w_out, final_g=final_g)
    msd = dict(norm_g=m_norm_g, w_ada=m_w_ada, b_ada=m_b_ada, w_in=m_w_in, b_f=m_b_f, q_norm_g=m_q_norm_g, w_uq=m_w_uq,
               kv_norm_g=m_kv_norm_g, w_ukv=m_w_ukv, w_out=m_w_out, final_g=m_final_g)
    vsd = dict(norm_g=v_norm_g, w_ada=v_w_ada, b_ada=v_b_ada, w_in=v_w_in, b_f=v_b_f, q_norm_g=v_q_norm_g, w_uq=v_w_uq,
               kv_norm_g=v_kv_norm_g, w_ukv=v_w_ukv, w_out=v_w_out, final_g=v_final_g)
    gsd = dict(norm_g=g_norm_g, w_ada=g_w_ada, b_ada=g_b_ada, w_in=g_w_in, b_f=g_b_f, q_norm_g=g_q_norm_g, w_uq=g_w_uq,
               kv_norm_g=g_kv_norm_g, w_ukv=g_w_ukv, w_out=g_w_out, final_g=g_final_g)
    small_names = ["norm_g", "b_ada", "b_f", "q_norm_g", "kv_norm_g", "final_g"]
    sm_shapes = [ws[n].shape for n in small_names]
    pk = lambda d: _pack_rows([d[n] for n in small_names], F32, 8).reshape(1, -1, LANES)
    sm_out = adamw(pk(ws), pk(gsd), pk(msd), pk(vsd), "adamw_small")
    sm_d, sm_m, sm_v = [dict(zip(small_names, _unpack(o.reshape(-1), sm_shapes))) for o in sm_out]
    delta, new_m, new_v = dict(sm_d), dict(sm_m), dict(sm_v)
    for n in ["w_ada", "w_in", "w_uq", "w_ukv", "w_out"]:
        delta[n], new_m[n], new_v[n] = adamw(ws[n], gsd[n], msd[n], vsd[n], "adamw_" + n)

    return (loss, grad_x, *[gsd[n] for n in names], *[delta[n] for n in names],
            *[new_m[n] for n in names], *[new_v[n] for n in names])
```

```python
import types

import jax
import jax.numpy as jnp
from jax import lax
from jax.experimental import pallas as pl
from jax.experimental.pallas import tpu as pltpu

F32 = jnp.float32
BF16 = jnp.bfloat16
MESH = pl.DeviceIdType.MESH

N_CHIPS = 4
N_DEV = 8
HEAD_DIM = 64
ROPE_DIM = 32
ROPE_THETA = 10000.0
HEAD_PAD = 128
ROPE_LO = 64
ROPE_HALF = 16
CHUNK_SHIFT = 6
LANES = 128
EPS = 1e-6
NEG = -1e30
ADAM_LR = 0.001
ADAM_B1 = 0.9
ADAM_B2 = 0.999
ADAM_EPS = 1e-08
ADAM_WD = 0.01
ADAM_STEP = 10
VMEM_LIMIT = 48 * 1024 * 1024
SLAB_COLS = 1024


def _params(sem=None, vmem=VMEM_LIMIT):
    return pltpu.CompilerParams(dimension_semantics=sem, vmem_limit_bytes=vmem)


def _nn(a, b):
    return jnp.dot(a, b, preferred_element_type=F32)


def _nt(a, b):
    return lax.dot_general(a, b, (((1,), (1,)), ((), ())), preferred_element_type=F32)


def _tn(a, b):
    return lax.dot_general(a, b, (((0,), (0,)), ((), ())), preferred_element_type=F32)


def _sigmoid(x):
    return 1.0 / (1.0 + jnp.exp(-x))


def _lane(shape):
    return lax.broadcasted_iota(jnp.int32, shape, len(shape) - 1)


def _pick(n, cands):
    for c in cands:
        if n % c == 0:
            return c
    return n


def _my_pos():
    return lax.axis_index("x"), lax.axis_index("y"), lax.axis_index("c")


def allgather8(xs, name):
    m_per, n = xs.shape

    def body(x_ref, out_ref, send_sems, recv_sems, local_sem):
        x, y, c = _my_pos()
        me, sibling = (x, y, c), (x, y, 1 - c)
        chips = [(1 - x, y), (x, 1 - y), (1 - x, 1 - y)]

        def rows(px, py, pc):
            return out_ref.at[pl.ds((4 * px + 2 * py + pc) * m_per, m_per), :]

        def copy(k, block, to, src=None):
            return pltpu.make_async_remote_copy(
                src_ref=rows(*block) if src is None else src, dst_ref=rows(*block),
                send_sem=send_sems.at[k], recv_sem=recv_sems.at[k], device_id=to, device_id_type=MESH)

        mine = pltpu.make_async_copy(x_ref, rows(*me), local_sem)
        mine.start()
        first = [copy(0, me, sibling, src=x_ref)]
        first += [copy(1 + j, me, (*chip, c), src=x_ref) for j, chip in enumerate(chips)]
        for cp in first:
            cp.start()
        passed = [copy(4 + j, (*chip, c), sibling) for j, chip in enumerate(chips)]
        for j, chip in enumerate(chips):
            copy(1 + j, (*chip, c), me).wait_recv()
            passed[j].start()
        copy(0, sibling, me).wait_recv()
        for j, chip in enumerate(chips):
            copy(4 + j, (*chip, 1 - c), me).wait_recv()
        for cp in first + passed:
            cp.wait_send()
        mine.wait()

    return pl.pallas_call(
        body, name=name,
        out_shape=jax.ShapeDtypeStruct((N_DEV * m_per, n), xs.dtype),
        in_specs=[pl.BlockSpec(memory_space=pltpu.VMEM)],
        out_specs=pl.BlockSpec(memory_space=pltpu.VMEM),
        scratch_shapes=[pltpu.SemaphoreType.DMA((7,)), pltpu.SemaphoreType.DMA((7,)), pltpu.SemaphoreType.DMA],
    )(xs)


def _remote(src, dst, send_sems, recv_sems, k, to):
    return pltpu.make_async_remote_copy(src_ref=src, dst_ref=dst, send_sem=send_sems.at[k], recv_sem=recv_sems.at[k],
                                        device_id=to, device_id_type=MESH)


def _hbm_call(body, name, ins, out_shapes, n_sems, aliases=None):
    hbm = pl.BlockSpec(memory_space=pl.ANY)
    scratch = [pltpu.SemaphoreType.DMA((n_sems,)), pltpu.SemaphoreType.DMA((n_sems,))]
    return pl.pallas_call(body, name=name, out_shape=out_shapes, in_specs=[hbm] * len(ins),
                          out_specs=[hbm] * len(out_shapes), scratch_shapes=scratch,
                          input_output_aliases=aliases or {})(*ins)


def _layers_half(ref, h, axis=0):
    size = ref.shape[axis] // 2
    idx = (slice(None),) * axis + (pl.ds(h * size, size),)
    return ref.at[idx]


class GatherOverIci:
    def __init__(self, ws):
        self.ins = list(ws)
        self.outs = [jax.ShapeDtypeStruct((N_CHIPS,) + w.shape, w.dtype) for w in ws]
        self.aliases = {}
        self.n_sems = 3 * len(ws)

    def _copies(self, x_refs, o_refs, send_sems, recv_sems):
        x, y, c = _my_pos()
        s = 2 * x + y
        out = []
        for a in range(len(x_refs)):
            for k, (tx, ty) in enumerate([(1 - x, y), (x, 1 - y), (1 - x, 1 - y)]):
                mine = _remote(_layers_half(x_refs[a], c, 1), _layers_half(o_refs[a].at[s], c, 1), send_sems, recv_sems,
                               3 * a + k, (tx, ty, c))
                got = _layers_half(o_refs[a].at[2 * tx + ty], c, 1)
                out.append((mine, _remote(got, got, send_sems, recv_sems, 3 * a + k, (tx, ty, c))))
        return out

    def start(self, x_refs, o_refs, send_sems, recv_sems):
        for mine, _ in self._copies(x_refs, o_refs, send_sems, recv_sems):
            mine.start()

    def wait(self, x_refs, o_refs, send_sems, recv_sems):
        for mine, theirs in self._copies(x_refs, o_refs, send_sems, recv_sems):
            theirs.wait_recv()
            mine.wait_send()


class GatherToSibling:
    def __init__(self, gathered):
        self.ins = list(gathered)
        self.outs = [jax.ShapeDtypeStruct(g.shape, g.dtype) for g in gathered]
        self.aliases = {a: a for a in range(len(gathered))}
        self.n_sems = 3 * len(gathered)

    def _copies(self, o_refs, send_sems, recv_sems):
        x, y, c = _my_pos()
        out = []
        for a in range(len(o_refs)):
            for k, (tx, ty) in enumerate([(1 - x, y), (x, 1 - y), (1 - x, 1 - y)]):
                got = _layers_half(o_refs[a].at[2 * tx + ty], c, 1)
                theirs = _layers_half(o_refs[a].at[2 * tx + ty], 1 - c, 1)
                out.append((_remote(got, got, send_sems, recv_sems, 3 * a + k, (x, y, 1 - c)),
                            _remote(theirs, theirs, send_sems, recv_sems, 3 * a + k, (x, y, 1 - c))))
        return out

    def start(self, x_refs, o_refs, send_sems, recv_sems):
        for mine, _ in self._copies(o_refs, send_sems, recv_sems):
            mine.start()

    def wait(self, x_refs, o_refs, send_sems, recv_sems):
        for mine, theirs in self._copies(o_refs, send_sems, recv_sems):
            theirs.wait_recv()
            mine.wait_send()


def run_side(side, name):
    n_in, n_out = len(side.ins), len(side.outs)

    def body(*refs):
        parts = refs[:n_in], refs[n_in:n_in + n_out], refs[n_in + n_out], refs[n_in + n_out + 1]
        side.start(*parts)
        side.wait(*parts)

    return _hbm_call(body, name, side.ins, side.outs, side.n_sems, aliases=side.aliases)


def weights_gather(ws, name):
    return run_side(GatherToSibling(run_side(GatherOverIci(ws), name + "_ici")), name + "_sibling")


def halves_to_sibling(gs, name):
    n = len(gs)

    def body(*refs):
        x_refs, o_refs, send_sems, recv_sems = refs[:n], refs[n:2 * n], refs[2 * n], refs[2 * n + 1]
        x, y, c = _my_pos()
        cps = [_remote(_layers_half(x_refs[a], 1 - c, axis=1), o_refs[a], send_sems, recv_sems, a, (x, y, 1 - c))
               for a in range(n)]
        for cp in cps:
            cp.start()
        for cp in cps:
            cp.wait()

    outs = [jax.ShapeDtypeStruct((g.shape[0], g.shape[1] // 2) + g.shape[2:], g.dtype) for g in gs]
    return _hbm_call(body, name, gs, outs, n)


class ExchangeOverIci:
    def __init__(self, xs):
        self.ins = list(xs)
        self.outs = [jax.ShapeDtypeStruct(v.shape, v.dtype) for v in xs]
        self.aliases = {}
        self.n_sems = 3 * len(xs)

    def _copies(self, x_refs, o_refs, send_sems, recv_sems):
        x, y, c = _my_pos()
        s = 2 * x + y
        out = []
        for a in range(len(x_refs)):
            for k, (tx, ty) in enumerate([(1 - x, y), (x, 1 - y), (1 - x, 1 - y)]):
                got = o_refs[a].at[2 * tx + ty]
                out.append((_remote(x_refs[a].at[2 * tx + ty], o_refs[a].at[s], send_sems, recv_sems, 3 * a + k, (tx, ty, c)),
                            _remote(got, got, send_sems, recv_sems, 3 * a + k, (tx, ty, c))))
        return out

    def start(self, x_refs, o_refs, send_sems, recv_sems):
        for mine, _ in self._copies(x_refs, o_refs, send_sems, recv_sems):
            mine.start()

    def wait(self, x_refs, o_refs, send_sems, recv_sems):
        for mine, theirs in self._copies(x_refs, o_refs, send_sems, recv_sems):
            theirs.wait_recv()
            mine.wait_send()


def halves_gather(xs, name):
    n = len(xs)

    def body(*refs):
        x_refs, o_refs, send_sems, recv_sems = refs[:n], refs[n:2 * n], refs[2 * n], refs[2 * n + 1]
        x, y, c = _my_pos()
        sends = []
        for a in range(n):
            cp = _remote(_layers_half(x_refs[a], c), _layers_half(o_refs[a], c), send_sems, recv_sems, a, (x, y, 1 - c))
            cp.start()
            sends.append(cp)
        for a in range(n):
            theirs = _layers_half(o_refs[a], 1 - c)
            _remote(theirs, theirs, send_sems, recv_sems, a, (x, y, 1 - c)).wait_recv()
        for cp in sends:
            cp.wait_send()

    outs = [jax.ShapeDtypeStruct(v.shape, v.dtype) for v in xs]
    return _hbm_call(body, name, xs, outs, n, aliases={a: a for a in range(n)})


def sum_leading(xs, name):
    n, r, c = xs.shape

    def body(x_ref, o_ref):
        acc = x_ref[0]
        for i in range(1, n):
            acc = acc + x_ref[i]
        o_ref[...] = acc

    return pl.pallas_call(body, name=name, out_shape=jax.ShapeDtypeStruct((r, c), xs.dtype))(xs)


def add_cast(a, b, out_dtype, name):
    n, r, c = a.shape
    tr = _pick(r, (512, 256, 128, 64, 32, 16))

    def body(a_ref, b_ref, o_ref):
        o_ref[...] = (a_ref[...].astype(F32) + b_ref[...].astype(F32)).astype(out_dtype)

    spec = pl.BlockSpec((1, tr, c), lambda i, j: (i, j, 0))
    return pl.pallas_call(body, name=name, grid=(n, r // tr), in_specs=[spec, spec], out_specs=spec,
                          out_shape=jax.ShapeDtypeStruct((n, r, c), out_dtype),
                          compiler_params=_params(("parallel", "parallel")))(a, b)


def sum_chips(parts, own, where, name):
    n, nl, r, c = parts.shape
    tr = _pick(r, (512, 256, 128, 64, 32, 16))

    def body(w_ref, p_ref, o_ref, out_ref):
        s = w_ref[0]
        acc = jnp.zeros(out_ref.shape, F32)
        for t in range(n):
            acc = acc + jnp.where(s == t, o_ref[0], p_ref[t]).astype(F32)
        out_ref[...] = acc

    grid_spec = pltpu.PrefetchScalarGridSpec(
        num_scalar_prefetch=1, grid=(nl, r // tr),
        in_specs=[pl.BlockSpec((n, 1, tr, c), lambda i, j, w: (0, i, j, 0)),
                  pl.BlockSpec((1, 1, tr, c), lambda i, j, w: (w[0], i, j, 0))],
        out_specs=pl.BlockSpec((1, tr, c), lambda i, j, w: (w[1] * nl + i, j, 0)))
    return pl.pallas_call(body, name=name, grid_spec=grid_spec, out_shape=jax.ShapeDtypeStruct((2 * nl, r, c), F32),
                          compiler_params=_params(("parallel", "parallel")))(where, parts, own)


def ada_forward(c_all, w_ada):
    nl, d, n = w_ada.shape
    nb = c_all.shape[0]

    def body(c_ref, w_ref, o_ref):
        cv = c_ref[...]
        ca = (cv * _sigmoid(cv)).astype(BF16)
        o_ref[0] = _nn(ca, w_ref[0].astype(BF16))

    return pl.pallas_call(
        body, name="ada_forward", grid=(nl,),
        in_specs=[pl.BlockSpec((nb, d), lambda l: (0, 0)), pl.BlockSpec((1, d, n), lambda l: (l, 0, 0))],
        out_specs=pl.BlockSpec((1, nb, n), lambda l: (l, 0, 0)),
        out_shape=jax.ShapeDtypeStruct((nl, nb, n), F32), compiler_params=_params(("parallel",)))(c_all, w_ada)


def ada_backward(c_all, dmod):
    nl, nb, n = dmod.shape
    d = c_all.shape[1]

    def body(c_ref, g_ref, o_ref):
        cv = c_ref[...]
        ca = (cv * _sigmoid(cv)).astype(BF16)
        o_ref[0] = _tn(ca, g_ref[0].astype(BF16))

    return pl.pallas_call(
        body, name="ada_backward", grid=(nl,),
        in_specs=[pl.BlockSpec((nb, d), lambda l: (0, 0)), pl.BlockSpec((1, nb, n), lambda l: (l, 0, 0))],
        out_specs=pl.BlockSpec((1, d, n), lambda l: (l, 0, 0)),
        out_shape=jax.ShapeDtypeStruct((nl, d, n), F32), compiler_params=_params(("parallel",)))(c_all, dmod)


def matmul_tn(a, b, name):
    k, m = a.shape
    n = b.shape[1]
    tm, tk = _pick(m, (512, 256, 128)), _pick(k, (512, 256, 128))
    tn = n if n * (tm * 8 + tk * 4) <= VMEM_LIMIT // 2 else _pick(n, (512, 256, 128))

    def body(a_ref, b_ref, o_ref):
        @pl.when(pl.program_id(2) == 0)
        def _():
            o_ref[...] = jnp.zeros_like(o_ref)

        o_ref[...] += _tn(a_ref[...], b_ref[...])

    return pl.pallas_call(
        body, name=name, grid=(m // tm, n // tn, k // tk),
        in_specs=[pl.BlockSpec((tk, tm), lambda i, j, kk: (kk, i)), pl.BlockSpec((tk, tn), lambda i, j, kk: (kk, j))],
        out_specs=pl.BlockSpec((tm, tn), lambda i, j, kk: (i, j)),
        out_shape=jax.ShapeDtypeStruct((m, n), F32),
        compiler_params=_params(("parallel", "parallel", "arbitrary")))(a, b)


def adamw(w, g, m, v, name):
    nl, r, c = w.shape
    tr = _pick(r, (512, 256, 128, 64, 32, 16, 8))

    def body(w_ref, g_ref, m_ref, v_ref, d_ref, mo_ref, vo_ref):
        gv = g_ref[...]
        mn = ADAM_B1 * m_ref[...] + (1.0 - ADAM_B1) * gv
        vn = ADAM_B2 * v_ref[...] + (1.0 - ADAM_B2) * (gv * gv)
        m_hat = mn / (1.0 - ADAM_B1 ** ADAM_STEP)
        v_hat = vn / (1.0 - ADAM_B2 ** ADAM_STEP)
        d_ref[...] = -ADAM_LR * (m_hat / (jnp.sqrt(v_hat) + ADAM_EPS) + ADAM_WD * w_ref[...])
        mo_ref[...] = mn
        vo_ref[...] = vn

    spec = pl.BlockSpec((1, tr, c), lambda l, i: (l, i, 0))
    out = jax.ShapeDtypeStruct((nl, r, c), F32)
    return pl.pallas_call(body, name=name, grid=(nl, r // tr), in_specs=[spec] * 4, out_specs=[spec] * 3,
                          out_shape=[out] * 3, compiler_params=_params(("parallel", "parallel")))(w, g, m, v)


def _rope(t, cos_t, sin_t):
    w = t.shape[1]
    lane = _lane(t.shape) & (HEAD_PAD - 1)
    first_half = (lane >= ROPE_LO) & (lane < ROPE_LO + ROPE_HALF)
    partner = jnp.where(first_half, pltpu.roll(t, w - ROPE_HALF, 1), pltpu.roll(t, ROPE_HALF, 1))
    return t * cos_t + partner * sin_t


def _rope_t(dt, cos_t, sin_t):
    w = dt.shape[1]
    lane = _lane(dt.shape) & (HEAD_PAD - 1)
    first_half = (lane >= ROPE_LO) & (lane < ROPE_LO + ROPE_HALF)
    ds = dt * sin_t
    partner = jnp.where(first_half, pltpu.roll(ds, w - ROPE_HALF, 1), pltpu.roll(ds, ROPE_HALF, 1))
    return dt * cos_t + partner


def _rms(xv, g):
    rstd = lax.rsqrt(jnp.mean(xv * xv, axis=-1, keepdims=True) + EPS)
    xh = xv * rstd
    return xh * g, xh, rstd


def _rms_bwd(dy, g, xh, rstd):
    dxh = dy * g
    dx = rstd * (dxh - xh * jnp.mean(dxh * xh, axis=-1, keepdims=True))
    return dx, jnp.sum(dy * xh, axis=0, keepdims=True)


def make_step(cfg):
    S, D, NZ = cfg.S, cfg.D, cfg.NZ
    FW, MW, QL, KVL, FH, MH = cfg.FW, cfg.MW, cfg.QL, cfg.KVL, cfg.FH, cfg.MH
    QW = MH * HEAD_PAD
    TM = _pick(S, (512, 256, 128))
    TQ = TK = _pick(S, (256, 128))
    n_tok = S // TM
    ZO = 3 * FW
    NZR = NZ - ZO
    misc_blk = (cfg.o_ms - ZO) // LANES
    FWD_UNROLL = 2
    FOX_SCALE = HEAD_DIM ** -0.5
    MLA_SCALE = (HEAD_DIM + ROPE_DIM) ** -0.5

    def tok(width, col=0):
        return pl.BlockSpec((TM, width), lambda i: (i, col))

    def const(shape):
        return pl.BlockSpec(shape, lambda i: tuple(0 for _ in shape))

    def layer(l, shape):
        return pl.BlockSpec((None,) + shape, lambda i: (l,) + tuple(0 for _ in shape))


    def ln_inproj(x, g, scale, shift, w_in, l):
        def body(x_ref, g_ref, sc_ref, sh_ref, w_ref, h_ref, z_ref, qkv_ref):
            y, _, _ = _rms(x_ref[...], g_ref[...])
            hb = (y * (1.0 + sc_ref[...]) + sh_ref[...]).astype(BF16)
            h_ref[...] = hb
            z = _nn(hb, w_ref[...])
            z_ref[...] = z[:, ZO:]
            qkv_ref[:, :FW] = (z[:, :FW] * FOX_SCALE).astype(BF16)
            qkv_ref[:, FW:] = z[:, FW:ZO].astype(BF16)

        return pl.pallas_call(
            body, name="ln_inproj", grid=(n_tok,),
            in_specs=[tok(D), const((1, D)), const((1, D)), const((1, D)), layer(l, (D, NZ))],
            out_specs=[tok(D), tok(NZR), tok(ZO)],
            out_shape=[jax.ShapeDtypeStruct((S, D), BF16), jax.ShapeDtypeStruct((S, NZR), F32),
                       jax.ShapeDtypeStruct((S, ZO), BF16)],
            compiler_params=_params(("parallel",)))(x, g, scale, shift, w_in)

    def _log_f_terms(misc, bf):
        lane = _lane(misc.shape)
        a = misc + bf
        e = jnp.exp(-jnp.abs(a))
        logf = jnp.minimum(a, 0.0) - jnp.log(1.0 + e)
        sig_neg = jnp.where(a >= 0, e, 1.0) / (1.0 + e)
        valid = lane < FH
        return jnp.where(valid, logf, 0.0), jnp.where(valid, sig_neg, 0.0)

    def fox_prep(z, bf_pad):
        def body(z_ref, b_ref, o_ref, carry):
            @pl.when(pl.program_id(0) == 0)
            def _():
                carry[...] = jnp.zeros_like(carry)

            logf, _ = _log_f_terms(z_ref[...], b_ref[...])
            row = lax.broadcasted_iota(jnp.int32, (TM, TM), 0)
            col = lax.broadcasted_iota(jnp.int32, (TM, TM), 1)
            tri = (col <= row).astype(F32)
            cum = jnp.dot(tri, logf, precision=lax.Precision.HIGHEST, preferred_element_type=F32) + carry[...]
            o_ref[...] = cum
            carry[...] = cum[TM - 1:TM, :]

        return pl.pallas_call(
            body, name="fox_prep", grid=(n_tok,),
            in_specs=[tok(LANES, misc_blk), const((1, LANES))], out_specs=tok(LANES),
            out_shape=jax.ShapeDtypeStruct((S, LANES), F32),
            scratch_shapes=[pltpu.VMEM((1, LANES), F32)],
            compiler_params=_params(("arbitrary",)))(z, bf_pad)

    def mla_prep(z, gq, gkv, w_uq, w_uk, w_v, l, cos_t, sin_t):
        def body(ql_ref, kvl_ref, ms_ref, gq_ref, gkv_ref, wq_ref, wk_ref, wv_ref, c_ref, s_ref, q_ref, k_ref, v_ref):
            cos1, sin1 = c_ref[...], s_ref[...]
            qn, _, _ = _rms(ql_ref[...], gq_ref[...])
            q = _nn(qn.astype(BF16), wq_ref[...])
            q_ref[...] = (_rope(q, jnp.tile(cos1, (1, MH)), jnp.tile(sin1, (1, MH))) * MLA_SCALE).astype(BF16)
            kvn, _, _ = _rms(kvl_ref[...], gkv_ref[...])
            kvb = kvn.astype(BF16)
            lane = _lane((TM, LANES))
            kr = jnp.where((lane >= ROPE_LO) & (lane < ROPE_LO + ROPE_DIM), ms_ref[...], 0.0)
            kr = _rope(kr, cos1, sin1)
            k_ref[...] = (_nn(kvb, wk_ref[...]) + jnp.tile(kr, (1, MH))).astype(BF16)
            v_ref[...] = _nn(kvb, wv_ref[...]).astype(BF16)

        return pl.pallas_call(
            body, name="mla_prep", grid=(n_tok,),
            in_specs=[tok(QL, (cfg.o_ql - ZO) // QL), tok(KVL, (cfg.o_kv - ZO) // KVL), tok(LANES, misc_blk),
                      const((1, QL)), const((1, KVL)), layer(l, (QL, QW)), layer(l, (KVL, QW)), layer(l, (KVL, MW)),
                      tok(LANES), tok(LANES)],
            out_specs=[tok(QW), tok(QW), tok(MW)],
            out_shape=[jax.ShapeDtypeStruct((S, QW), BF16), jax.ShapeDtypeStruct((S, QW), BF16),
                       jax.ShapeDtypeStruct((S, MW), BF16)],
            compiler_params=_params(("parallel",)))(z, z, z, gq, gkv, w_uq, w_uk, w_v, cos_t, sin_t)

    def _allowed(q0, k0, chunked):
        qi = q0 + lax.broadcasted_iota(jnp.int32, (TQ, TK), 0)
        ki = k0 + lax.broadcasted_iota(jnp.int32, (TQ, TK), 1)
        if chunked:
            return (ki >> CHUNK_SHIFT) <= (qi >> CHUNK_SHIFT)
        return ki <= qi

    def _heads(val, packed):
        if packed:
            lane = _lane(val.shape)
            zero = jnp.zeros_like(val)
            return [jnp.where(lane < HEAD_DIM, val, zero), jnp.where(lane >= HEAD_DIM, val, zero)]
        return [val[:, :HEAD_PAD], val[:, HEAD_PAD:]]

    def _merge(a0, a1):
        return jnp.where(_lane(a0.shape) < HEAD_DIM, a0, a1)

    def attn_fwd(q, k, v, q_blk0, k_blk0, v_blk0, cumt, packed, chunked, name, side=None):
        has_bias = cumt is not None
        n_pairs = (FH if packed else MH) // 2
        wq = LANES if packed else 2 * HEAD_PAD
        assert TQ == TK

        n_main = 4 if has_bias else 3
        n_side_in, n_side_out = (len(side.ins), len(side.outs)) if side else (0, 0)

        def body(*refs):
            if has_bias:
                q_ref, k_ref, v_ref, ct_ref = refs[:n_main]
            else:
                q_ref, k_ref, v_ref = refs[:n_main]
            o_ref, lse_ref = refs[n_main + n_side_in:n_main + n_side_in + 2]
            i = pl.program_id(1)
            if side:
                side_refs = (refs[n_main:n_main + n_side_in], refs[n_main + n_side_in + 2:n_main + n_side_in + 2 + n_side_out],
                             refs[-2], refs[-1])

                @pl.when((pl.program_id(0) == 0) & (i == 0))
                def _():
                    side.start(*side_refs)

            q0 = i * TQ
            qh = _heads(q_ref[...], packed)
            q_both = jnp.concatenate(qh, axis=0) if packed else None

            keys_first = not has_bias
            heads_axis = 1 if keys_first else 0

            def scores(kb):
                kblk = k_ref[pl.ds(pl.multiple_of(kb * TK, TK), TK), :]
                if keys_first:
                    if packed:
                        return _nt(kblk, q_both)
                    return jnp.concatenate([_nt(kblk[:, :HEAD_PAD], qh[0]), _nt(kblk[:, HEAD_PAD:], qh[1])], axis=1)
                if packed:
                    return _nt(q_both, kblk)
                return jnp.concatenate([_nt(qh[0], kblk[:, :HEAD_PAD]), _nt(qh[1], kblk[:, HEAD_PAD:])], axis=0)

            def update(kb, s, carry, masked):
                m, l, acc = carry
                k0 = pl.multiple_of(kb * TK, TK)
                s0, s1 = (s[:, :TQ], s[:, TQ:]) if keys_first else (s[:TQ], s[TQ:])
                if has_bias:
                    ck = ct_ref[0, kb]
                    s0, s1 = s0 - ck[0:1, :], s1 - ck[1:2, :]
                if masked:
                    allow = _allowed(q0, k0, chunked)
                    allow = jnp.transpose(allow) if keys_first else allow
                    s0, s1 = jnp.where(allow, s0, NEG), jnp.where(allow, s1, NEG)
                s = jnp.concatenate([s0, s1], axis=heads_axis)
                m_new = jnp.maximum(m, jnp.max(s, axis=1 - heads_axis, keepdims=True))
                p = jnp.exp(s - m_new)
                alpha = jnp.exp(m - m_new)
                l = alpha * l + jnp.sum(p, axis=1 - heads_axis, keepdims=True)
                vblk = v_ref[pl.ds(k0, TK), :]
                pv = _tn(vblk, p.astype(BF16)) if keys_first else _nn(p.astype(BF16), vblk)
                return m_new, l, alpha * acc + pv

            def several(kb, carry, n, last_masked):
                ss = [scores(kb + u) for u in range(n)]
                for u in range(n):
                    carry = update(kb + u, ss[u], carry, last_masked and u == n - 1)
                return carry

            stat = (1, 2 * TQ) if keys_first else (2 * TQ, 1)
            init = (jnp.full(stat, NEG, F32), jnp.zeros(stat, F32),
                    jnp.zeros((LANES, 2 * TQ) if keys_first else (2 * TQ, LANES), F32))
            carry = lax.fori_loop(0, i // FWD_UNROLL, lambda t, cr: several(FWD_UNROLL * t, cr, FWD_UNROLL, False), init)
            m, l, acc = lax.switch(i % FWD_UNROLL,
                                   [lambda cr, r=r: several(i - r, cr, r + 1, True) for r in range(FWD_UNROLL)], carry)
            o = acc / l
            lse = m + jnp.log(l)
            if keys_first:
                o_ref[...] = jnp.transpose(jnp.concatenate([o[:HEAD_DIM, :TQ], o[HEAD_DIM:, TQ:]], axis=0))
                sub = lax.broadcasted_iota(jnp.int32, (LANES, TQ), 0)
                lse_ref[...] = jnp.transpose(jnp.where(sub == 0, lse[:, :TQ], jnp.where(sub == 1, lse[:, TQ:], 0.0)))
            else:
                lane = _lane((TQ, LANES))
                o_ref[...] = _merge(o[:TQ], o[TQ:])
                lse_ref[...] = jnp.where(lane == 0, lse[:TQ], jnp.where(lane == 1, lse[TQ:], 0.0))
            if side:
                @pl.when((pl.program_id(0) == n_pairs - 1) & (i == S // TQ - 1))
                def _():
                    side.wait(*side_refs)

        in_specs = [pl.BlockSpec((TQ, wq), lambda p, i: (i, q_blk0 + p)),
                    pl.BlockSpec((S, wq), lambda p, i: (0, k_blk0 + p)),
                    pl.BlockSpec((S, LANES), lambda p, i: (0, v_blk0 + p))]
        args = [q, k, v]
        if has_bias:
            in_specs += [pl.BlockSpec((1, S // TK, 8, TK), lambda p, i: (p, 0, 0, 0))]
            args += [cumt]
        out_specs = [pl.BlockSpec((TQ, LANES), lambda p, i: (i, p)), pl.BlockSpec((TQ, LANES), lambda p, i: (i, p))]
        out_shape = [jax.ShapeDtypeStruct((S, n_pairs * LANES), F32), jax.ShapeDtypeStruct((S, n_pairs * LANES), F32)]
        extra = {}
        if side:
            hbm = pl.BlockSpec(memory_space=pl.ANY)
            in_specs += [hbm] * n_side_in
            args += side.ins
            out_specs += [hbm] * n_side_out
            out_shape += side.outs
            extra = dict(scratch_shapes=[pltpu.SemaphoreType.DMA((side.n_sems,)), pltpu.SemaphoreType.DMA((side.n_sems,))],
                         input_output_aliases={n_main + a: 2 + b for a, b in side.aliases.items()})
        outs = pl.pallas_call(
            body, name=name, grid=(n_pairs, S // TQ), in_specs=in_specs, out_specs=out_specs, out_shape=out_shape,
            compiler_params=_params(("arbitrary", "arbitrary")), **extra)(*args)
        return (outs[0], outs[1], list(outs[2:])) if side else outs

    def gate_outproj(of, om, z, w_out, l, x, gate):
        def body(of_ref, om_ref, fg_ref, mg_ref, w_ref, x_ref, gt_ref, xn_ref, u_ref, y_ref):
            fg, mg = fg_ref[...], mg_ref[...]
            u = jnp.concatenate([of_ref[...] * fg * _sigmoid(fg), om_ref[...] * mg * _sigmoid(mg)], axis=1).astype(BF16)
            y = _nn(u, w_ref[...])
            u_ref[...] = u
            y_ref[...] = y.astype(BF16)
            xn_ref[...] = x_ref[...] + gt_ref[...] * y

        return pl.pallas_call(
            body, name="gate_outproj", grid=(n_tok,),
            in_specs=[tok(FW), tok(MW), tok(FW, (cfg.o_fg - ZO) // FW), tok(MW, (cfg.o_mg - ZO) // MW),
                      layer(l, (FW + MW, D)), tok(D), const((1, D))],
            out_specs=[tok(D), tok(FW + MW), tok(D)],
            out_shape=[jax.ShapeDtypeStruct((S, D), F32), jax.ShapeDtypeStruct((S, FW + MW), BF16),
                       jax.ShapeDtypeStruct((S, D), BF16)],
            compiler_params=_params(("parallel",)))(of, om, z, z, w_out, x, gate)

    def final_loss(x, g, target):
        def body(x_ref, g_ref, t_ref, dx_ref, acc_ref, loss_ref):
            @pl.when(pl.program_id(0) == 0)
            def _():
                acc_ref[...] = jnp.zeros_like(acc_ref)
                loss_ref[...] = jnp.zeros_like(loss_ref)

            gv = g_ref[...]
            y, xh, rstd = _rms(x_ref[...], gv)
            e = y - t_ref[...]
            loss_ref[...] += 0.5 * jnp.sum(jnp.sum(e * e, axis=-1, keepdims=True) / D, axis=0, keepdims=True)
            dx, dg = _rms_bwd(e / D, gv, xh, rstd)
            dx_ref[...] = dx
            acc_ref[0:1, :] += dg

        return pl.pallas_call(
            body, name="final_loss", grid=(n_tok,),
            in_specs=[tok(D), const((1, D)), tok(D)],
            out_specs=[tok(D), const((8, D)), const((1, LANES))],
            out_shape=[jax.ShapeDtypeStruct((S, D), F32), jax.ShapeDtypeStruct((8, D), F32),
                       jax.ShapeDtypeStruct((1, LANES), F32)],
            compiler_params=_params(("arbitrary",)))(x, g, target)


    def bwd_out(dxn, gate, y, w_out, l, of, om, z, lse_f, lse_m):
        def body(dx_ref, gt_ref, y_ref, w_ref, of_ref, om_ref, fg_ref, mg_ref, lf_ref, lm_ref,
                 dof_ref, dom_ref, dfg_ref, dmg_ref, dy_ref, acc_ref, sf_ref, sm_ref):
            @pl.when(pl.program_id(0) == 0)
            def _():
                acc_ref[...] = jnp.zeros_like(acc_ref)

            dxv = dx_ref[...]
            acc_ref[0:1, :] += jnp.sum(dxv * y_ref[...].astype(F32), axis=0, keepdims=True)
            dy = (gt_ref[...] * dxv).astype(BF16)
            dy_ref[...] = dy
            du = _nt(dy, w_ref[...])
            lane = _lane((TM, LANES))
            for lo, width, o_ref, g_ref, do_ref, dg_ref, l_ref, st_ref in (
                    (0, FW, of_ref, fg_ref, dof_ref, dfg_ref, lf_ref, sf_ref),
                    (FW, MW, om_ref, mg_ref, dom_ref, dmg_ref, lm_ref, sm_ref)):
                gv, ov = g_ref[...], o_ref[...]
                sg = _sigmoid(gv)
                dup = du[:, lo:lo + width]
                dob = (dup * gv * sg).astype(BF16)
                do_ref[...] = dob
                dg_ref[...] = (dup * ov * sg * (1.0 + gv * (1.0 - sg))).astype(BF16)
                d = dob.astype(F32) * ov
                for pr in range(width // LANES):
                    cols = slice(pr * LANES, (pr + 1) * LANES)
                    dp = d[:, cols]
                    d0 = jnp.sum(jnp.where(lane < HEAD_DIM, dp, 0.0), axis=-1, keepdims=True)
                    d1 = jnp.sum(jnp.where(lane >= HEAD_DIM, dp, 0.0), axis=-1, keepdims=True)
                    st_ref[:, cols] = jnp.where(lane == 2, d0, jnp.where(lane == 3, d1, l_ref[:, cols]))

        return pl.pallas_call(
            body, name="bwd_out", grid=(n_tok,),
            in_specs=[tok(D), const((1, D)), tok(D), layer(l, (FW + MW, D)), tok(FW), tok(MW),
                      tok(FW, (cfg.o_fg - ZO) // FW), tok(MW, (cfg.o_mg - ZO) // MW), tok(FW), tok(MW)],
            out_specs=[tok(FW), tok(MW), tok(FW), tok(MW), tok(D), const((8, D)), tok(FW), tok(MW)],
            out_shape=[jax.ShapeDtypeStruct((S, FW), BF16), jax.ShapeDtypeStruct((S, MW), BF16),
                       jax.ShapeDtypeStruct((S, FW), BF16), jax.ShapeDtypeStruct((S, MW), BF16),
                       jax.ShapeDtypeStruct((S, D), BF16), jax.ShapeDtypeStruct((8, D), F32),
                       jax.ShapeDtypeStruct((S, FW), F32), jax.ShapeDtypeStruct((S, MW), F32)],
            compiler_params=_params(("arbitrary",)))(dxn, gate, y, w_out, of, om, z, z, lse_f, lse_m)

    def attn_bwd(q, k, v, do, stats, q_blk0, k_blk0, v_blk0, cumt, packed, chunked, name, side=None):
        has_bias = cumt is not None
        n_pairs = (FH if packed else MH) // 2
        wq = LANES if packed else 2 * HEAD_PAD
        n_q = S // TQ
        assert TQ == TK
        n_in, n_out = (6, 5) if has_bias else (5, 3)
        n_side_in, n_side_out = (len(side.ins), len(side.outs)) if side else (0, 0)

        def body(*refs):
            main_out = refs[n_in + n_side_in:n_in + n_side_in + n_out]
            if has_bias:
                q_ref, k_ref, v_ref, do_ref, st_ref, ct_ref = refs[:n_in]
                dq_ref, dk_ref, dv_ref, dc_ref, dr_ref = main_out
            else:
                q_ref, k_ref, v_ref, do_ref, st_ref = refs[:n_in]
                dq_ref, dk_ref, dv_ref = main_out
            jb = pl.program_id(1)
            k0 = jb * TK
            if side:
                side_refs = (refs[n_in:n_in + n_side_in], refs[n_in + n_side_in + n_out:n_in + n_side_in + n_out + n_side_out],
                             refs[-2], refs[-1])

                @pl.when((pl.program_id(0) == 0) & (jb == 0))
                def _():
                    side.start(*side_refs)

            @pl.when(jb == 0)
            def _():
                dq_ref[...] = jnp.zeros_like(dq_ref)
                if has_bias:
                    dr_ref[...] = jnp.zeros_like(dr_ref)

            dk_ref[...] = jnp.zeros_like(dk_ref)
            dv_ref[...] = jnp.zeros_like(dv_ref)
            kh = _heads(k_ref[...], packed)
            k_both = jnp.concatenate(kh, axis=0) if packed else None
            v_both = jnp.concatenate(_heads(v_ref[...], True), axis=0)
            if has_bias:
                ct = jnp.transpose(ct_ref[0, 0])
                ck = jnp.concatenate([ct[:, 0:1], ct[:, 1:2]], axis=0)

            def products(ib):
                rows = pl.ds(pl.multiple_of(ib * TQ, TQ), TQ)
                q2, do2 = q_ref[rows, :], do_ref[rows, :]
                if packed:
                    s = _nt(k_both, q2)
                else:
                    qh = _heads(q2, False)
                    s = jnp.concatenate([_nt(kh[0], qh[0]), _nt(kh[1], qh[1])], axis=0)
                return s, _nt(v_both, do2)

            def update(ib, s, dp, carry, masked):
                q0 = pl.multiple_of(ib * TQ, TQ)
                rows = pl.ds(q0, TQ)
                q2, do2 = q_ref[rows, :], do_ref[rows, :]
                st = jnp.transpose(st_ref[rows, :])
                if not packed:
                    qh = _heads(q2, False)
                if has_bias:
                    s = s - ck
                if masked:
                    allow = jnp.transpose(_allowed(q0, k0, chunked))
                    s = jnp.where(jnp.concatenate([allow, allow], axis=0), s, NEG)
                p = jnp.concatenate([jnp.exp(s[:TK] - st[0:1, :]), jnp.exp(s[TK:] - st[1:2, :])], axis=0)
                dv2 = _nn(p.astype(BF16), do2)
                ds = jnp.concatenate([p[:TK] * (dp[:TK] - st[2:3, :]), p[TK:] * (dp[TK:] - st[3:4, :])], axis=0)
                dsb = ds.astype(BF16)
                dv_ref[...] += _merge(dv2[:TK], dv2[TK:])
                if packed:
                    dk2 = _nn(dsb, q2)
                    dk_ref[...] += _merge(dk2[:TK], dk2[TK:])
                    dq_ref[rows, :] += _tn(dsb, k_both)
                else:
                    dk_ref[...] += jnp.concatenate([_nn(dsb[:TK], qh[0]), _nn(dsb[TK:], qh[1])], axis=1)
                    dq_ref[rows, :] += jnp.concatenate([_tn(dsb[:TK], kh[0]), _tn(dsb[TK:], kh[1])], axis=1)
                if has_bias:
                    sub = lax.broadcasted_iota(jnp.int32, (8, TQ), 0)
                    r0 = jnp.sum(ds[:TK], axis=0, keepdims=True)
                    r1 = jnp.sum(ds[TK:], axis=0, keepdims=True)
                    dr_ref[0, ib] += jnp.where(sub == 0, r0, jnp.where(sub == 1, r1, 0.0))
                    return carry - jnp.sum(ds, axis=1, keepdims=True)
                return carry

            def step(ib, carry, masked):
                s, dp = products(ib)
                return update(ib, s, dp, carry, masked)

            def two(ib, carry):
                s_a, dp_a = products(ib)
                s_b, dp_b = products(ib + 1)
                return update(ib + 1, s_b, dp_b, update(ib, s_a, dp_a, carry, False), False)

            dc = step(jb, jnp.zeros((2 * TK, 1), F32), True)
            n_rest = n_q - 1 - jb
            dc = lax.fori_loop(0, n_rest // 2, lambda t, cr: two(jb + 1 + 2 * t, cr), dc)
            dc = lax.cond(n_rest % 2 == 1, lambda cr: step(n_q - 1, cr, False), lambda cr: cr, dc)
            if has_bias:
                lane = _lane((TK, LANES))
                dc_ref[0] = jnp.where(lane == 0, dc[:TK], jnp.where(lane == 1, dc[TK:], 0.0))
            if side:
                @pl.when((pl.program_id(0) == n_pairs - 1) & (jb == S // TK - 1))
                def _():
                    side.wait(*side_refs)

        in_specs = [pl.BlockSpec((S, wq), lambda p, j: (0, q_blk0 + p)),
                    pl.BlockSpec((TK, wq), lambda p, j: (j, k_blk0 + p)),
                    pl.BlockSpec((TK, LANES), lambda p, j: (j, v_blk0 + p)),
                    pl.BlockSpec((S, LANES), lambda p, j: (0, p)),
                    pl.BlockSpec((S, LANES), lambda p, j: (0, p))]
        args = [q, k, v, do, stats]
        out_specs = [pl.BlockSpec((S, wq), lambda p, j: (0, p)),
                     pl.BlockSpec((TK, wq), lambda p, j: (j, p)),
                     pl.BlockSpec((TK, LANES), lambda p, j: (j, p))]
        out_shape = [jax.ShapeDtypeStruct((S, n_pairs * wq), F32), jax.ShapeDtypeStruct((S, n_pairs * wq), F32),
                     jax.ShapeDtypeStruct((S, n_pairs * LANES), F32)]
        if has_bias:
            in_specs += [pl.BlockSpec((1, 1, 8, TK), lambda p, j: (p, j, 0, 0))]
            args += [cumt]
            out_specs += [pl.BlockSpec((1, TK, LANES), lambda p, j: (p, j, 0)),
                          pl.BlockSpec((1, S // TQ, 8, TQ), lambda p, j: (p, 0, 0, 0))]
            out_shape += [jax.ShapeDtypeStruct((n_pairs, S, LANES), F32),
                          jax.ShapeDtypeStruct((n_pairs, S // TQ, 8, TQ), F32)]
        extra = {}
        if side:
            hbm = pl.BlockSpec(memory_space=pl.ANY)
            in_specs += [hbm] * n_side_in
            args += side.ins
            out_specs += [hbm] * n_side_out
            out_shape += side.outs
            extra = dict(scratch_shapes=[pltpu.SemaphoreType.DMA((side.n_sems,)), pltpu.SemaphoreType.DMA((side.n_sems,))],
                         input_output_aliases={n_in + a: n_out + b for a, b in side.aliases.items()})
        outs = pl.pallas_call(
            body, name=name, grid=(n_pairs, S // TK), in_specs=in_specs, out_specs=out_specs, out_shape=out_shape,
            compiler_params=_params(("arbitrary", "arbitrary")), **extra)(*args)
        return (*outs[:n_out], list(outs[n_out:])) if side else outs

    def fox_post(dcum, z, bf_pad):
        def rev(i):
            return n_tok - 1 - i

        def body(dc_ref, z_ref, b_ref, dff_ref, acc_ref, carry):
            @pl.when(pl.program_id(0) == 0)
            def _():
                carry[...] = jnp.zeros_like(carry)
                acc_ref[...] = jnp.zeros_like(acc_ref)

            _, sig_neg = _log_f_terms(z_ref[...], b_ref[...])
            row = lax.broadcasted_iota(jnp.int32, (TM, TM), 0)
            col = lax.broadcasted_iota(jnp.int32, (TM, TM), 1)
            tri = (col >= row).astype(F32)
            dlog = jnp.dot(tri, dc_ref[...], precision=lax.Precision.HIGHEST, preferred_element_type=F32) + carry[...]
            carry[...] = dlog[0:1, :]
            dff = dlog * sig_neg
            dff_ref[...] = dff
            acc_ref[0:1, :] += jnp.sum(dff, axis=0, keepdims=True)

        return pl.pallas_call(
            body, name="fox_post", grid=(n_tok,),
            in_specs=[pl.BlockSpec((TM, LANES), lambda i: (rev(i), 0)),
                      pl.BlockSpec((TM, LANES), lambda i: (rev(i), misc_blk)), const((1, LANES))],
            out_specs=[pl.BlockSpec((TM, LANES), lambda i: (rev(i), 0)), const((8, LANES))],
            out_shape=[jax.ShapeDtypeStruct((S, LANES), F32), jax.ShapeDtypeStruct((8, LANES), F32)],
            scratch_shapes=[pltpu.VMEM((1, LANES), F32)],
            compiler_params=_params(("arbitrary",)))(dcum, z, bf_pad)

    def mla_post(dq, dk, dv, dff, z, gq, gkv, w_uq, w_uk, w_v, l, cos_t, sin_t):
        def body(dq_ref, dk_ref, dv_ref, dff_ref, ql_ref, kvl_ref, gq_ref, gkv_ref, wq_ref, wk_ref, wv_ref,
                 c_ref, s_ref, zq_ref, zkv_ref, zms_ref, dwq_ref, dwk_ref, dwv_ref, dgq_ref, dgkv_ref):
            @pl.when(pl.program_id(0) == 0)
            def _():
                for r in (dwq_ref, dwk_ref, dwv_ref, dgq_ref, dgkv_ref):
                    r[...] = jnp.zeros_like(r)

            cos1, sin1 = c_ref[...], s_ref[...]
            gqv, gkvv = gq_ref[...], gkv_ref[...]
            qn, qxh, qrstd = _rms(ql_ref[...], gqv)
            dq_pre = _rope_t(dq_ref[...] * MLA_SCALE, jnp.tile(cos1, (1, MH)), jnp.tile(sin1, (1, MH))).astype(BF16)
            dwq_ref[...] += _tn(qn.astype(BF16), dq_pre)
            dql, dgq = _rms_bwd(_nt(dq_pre, wq_ref[...]), gqv, qxh, qrstd)
            zq_ref[...] = dql.astype(BF16)
            dgq_ref[0:1, :] += dgq

            dkv = dk_ref[...]
            lane = _lane(dkv.shape) & (HEAD_PAD - 1)
            dkn = jnp.where(lane < HEAD_DIM, dkv, 0.0).astype(BF16)
            dkr = dkv[:, 0:HEAD_PAD]
            for hd in range(1, MH):
                dkr = dkr + dkv[:, hd * HEAD_PAD:(hd + 1) * HEAD_PAD]
            lane1 = _lane(dkr.shape)
            dkr = jnp.where((lane1 >= ROPE_LO) & (lane1 < ROPE_LO + ROPE_DIM), dkr, 0.0)
            dkr = _rope_t(dkr, cos1, sin1)
            zms_ref[...] = (dkr + dff_ref[...]).astype(BF16)

            kvn, kxh, krstd = _rms(kvl_ref[...], gkvv)
            kvb = kvn.astype(BF16)
            dvb = dv_ref[...].astype(BF16)
            dwk_ref[...] += _tn(kvb, dkn)
            dwv_ref[...] += _tn(kvb, dvb)
            dkvl, dgkv = _rms_bwd(_nt(dkn, wk_ref[...]) + _nt(dvb, wv_ref[...]), gkvv, kxh, krstd)
            zkv_ref[...] = dkvl.astype(BF16)
            dgkv_ref[0:1, :] += dgkv

        return pl.pallas_call(
            body, name="mla_post", grid=(n_tok,),
            in_specs=[tok(QW), tok(QW), tok(MW), tok(LANES), tok(QL, (cfg.o_ql - ZO) // QL), tok(KVL, (cfg.o_kv - ZO) // KVL),
                      const((1, QL)), const((1, KVL)), layer(l, (QL, QW)), layer(l, (KVL, QW)), layer(l, (KVL, MW)),
                      tok(LANES), tok(LANES)],
            out_specs=[tok(QL), tok(KVL), tok(LANES), const((QL, QW)), const((KVL, QW)), const((KVL, MW)),
                       const((8, QL)), const((8, KVL))],
            out_shape=[jax.ShapeDtypeStruct((S, QL), BF16), jax.ShapeDtypeStruct((S, KVL), BF16),
                       jax.ShapeDtypeStruct((S, LANES), BF16), jax.ShapeDtypeStruct((QL, QW), F32),
                       jax.ShapeDtypeStruct((KVL, QW), F32), jax.ShapeDtypeStruct((KVL, MW), F32),
                       jax.ShapeDtypeStruct((8, QL), F32), jax.ShapeDtypeStruct((8, KVL), F32)],
            compiler_params=_params(("arbitrary",)))(dq, dk, dv, dff, z, z, gq, gkv, w_uq, w_uk, w_v, cos_t, sin_t)

    def bwd_in(dz, w_in, l, x, dxn, g, scale):
        def body(dz_ref, w_ref, x_ref, dx_ref, g_ref, sc_ref, o_ref, acc_ref):
            @pl.when(pl.program_id(0) == 0)
            def _():
                acc_ref[...] = jnp.zeros_like(acc_ref)

            dh = _nt(dz_ref[...], w_ref[...])
            gv, mod = g_ref[...], 1.0 + sc_ref[...]
            _, xh, rstd = _rms(x_ref[...], gv)
            t = dh * xh
            acc_ref[0:1, :] += jnp.sum(dh, axis=0, keepdims=True)
            acc_ref[1:2, :] += jnp.sum(t * gv, axis=0, keepdims=True)
            acc_ref[2:3, :] += jnp.sum(t * mod, axis=0, keepdims=True)
            dx, _ = _rms_bwd(dh, gv * mod, xh, rstd)
            o_ref[...] = dx_ref[...] + dx

        return pl.pallas_call(
            body, name="bwd_in", grid=(n_tok,),
            in_specs=[tok(NZ), layer(l, (D, NZ)), tok(D), tok(D), const((1, D)), const((1, D))],
            out_specs=[tok(D), const((8, D))],
            out_shape=[jax.ShapeDtypeStruct((S, D), F32), jax.ShapeDtypeStruct((8, D), F32)],
            compiler_params=_params(("arbitrary",)))(dz, w_in, x, dxn, g, scale)


    def pair_rows(cum):
        n_pairs = FH // 2
        ct = jnp.pad(cum[:, :FH].T.reshape(n_pairs, 2, S), ((0, 0), (0, 6), (0, 0)))
        return ct.reshape(n_pairs, 8, S // TK, TK).transpose(0, 2, 1, 3)

    def bias_grad(dc, dr):
        n_pairs = FH // 2
        d = dr.transpose(0, 2, 1, 3).reshape(n_pairs, 8, S)[:, :2, :].reshape(FH, S).T
        d = d + dc[:, :, :2].transpose(1, 0, 2).reshape(S, FH)
        return jnp.pad(d, ((0, 0), (0, LANES - FH)))

    def layer_forward(x, wl, mod, later_shards=None):
        shift, scale, gate = mod
        h, z, qkv = ln_inproj(x, wl.norm_g, scale, shift, wl.w_in, wl.l)
        ct = pair_rows(fox_prep(z, wl.bf_pad))
        fox = attn_fwd(qkv, qkv, qkv, cfg.o_fq // LANES, cfg.o_fk // LANES, cfg.o_fv // LANES, ct, True, False,
                       "fox_fwd_gather" if later_shards else "fox_fwd", GatherOverIci(later_shards) if later_shards else None)
        of, lse_f = fox[0], fox[1]
        qp, kp, vp = mla_prep(z, wl.gq, wl.gkv, wl.w_uq, wl.w_uk, wl.w_v, wl.l, cfg.cos_t, cfg.sin_t)
        mla = attn_fwd(qp, kp, vp, 0, 0, 0, None, False, True, "mla_fwd_gather" if later_shards else "mla_fwd",
                       GatherToSibling(fox[2]) if later_shards else None)
        om, lse_m = mla[0], mla[1]
        xn, u, y = gate_outproj(of, om, z, wl.w_out, wl.l, x, gate)
        saved = types.SimpleNamespace(x=x, h=h, z=z, qkv=qkv, ct=ct, of=of, lse_f=lse_f, qp=qp, kp=kp, vp=vp,
                                      om=om, lse_m=lse_m, u=u, y=y)
        return (xn, saved, mla[2]) if later_shards else (xn, saved)

    def layer_backward(dxn, sv, wl, mod, side=None):
        shift, scale, gate = mod
        do_f, do_m, dfg, dmg, dy, acc_o, st_f, st_m = bwd_out(dxn, gate, sv.y, wl.w_out, wl.l, sv.of, sv.om, sv.z,
                                                              sv.lse_f, sv.lse_m)
        dw_out = matmul_tn(sv.u, dy, "dw_out")
        dfq, dfk, dfv, dck, dcr, *rode = attn_bwd(sv.qkv, sv.qkv, sv.qkv, do_f, st_f, cfg.o_fq // LANES, cfg.o_fk // LANES,
                                                  cfg.o_fv // LANES, sv.ct, True, False,
                                                  "fox_bwd_exchange" if side else "fox_bwd", side)
        dff, acc_f = fox_post(bias_grad(dck, dcr), sv.z, wl.bf_pad)
        dqp, dkp, dvp = attn_bwd(sv.qp, sv.kp, sv.vp, do_m, st_m, 0, 0, 0, None, False, True, "mla_bwd")
        zq, zkv, zms, dw_uq, dw_uk, dw_v, dgq, dgkv = mla_post(
            dqp, dkp, dvp, dff, sv.z, wl.gq, wl.gkv, wl.w_uq, wl.w_uk, wl.w_v, wl.l, cfg.cos_t, cfg.sin_t)
        dz = jnp.concatenate([(dfq * FOX_SCALE).astype(BF16), dfk.astype(BF16), dfv.astype(BF16), dfg, dmg, zq, zkv, zms], axis=1)
        dx, acc_i = bwd_in(dz, wl.w_in, wl.l, sv.x, dxn, wl.norm_g, scale)
        dw_in = matmul_tn(sv.h, dz, "dw_in")
        grads = types.SimpleNamespace(
            w_in=dw_in, w_out=dw_out, w_uq=dw_uq, w_uk=dw_uk, w_v=dw_v, gq=dgq[0], gkv=dgkv[0],
            b_f=acc_f[0, :FH], norm_g=acc_i[2], dmod=jnp.concatenate([acc_i[0], acc_i[1], acc_o[0]]))
        return (dx, grads, rode[0]) if side else (dx, grads)

    return types.SimpleNamespace(layer_forward=layer_forward, layer_backward=layer_backward, final_loss=final_loss)


def _pack_rows(parts, dtype, row_multiple):
    flat = jnp.concatenate([p.reshape(-1).astype(dtype) for p in parts])
    per = SLAB_COLS * row_multiple
    total = -(-flat.shape[0] // per) * per
    return jnp.pad(flat, (0, total - flat.shape[0])).reshape(total // SLAB_COLS, SLAB_COLS)


def _unpack(flat, shapes):
    out, off = [], 0
    for shp in shapes:
        n = 1
        for d in shp:
            n *= d
        out.append(flat[off:off + n].reshape(shp))
        off += n
    return out


def kernel(x, c, positions, norm_g, w_ada, b_ada, w_in, b_f, q_norm_g, w_uq, kv_norm_g, w_ukv, w_out, final_g, loss_target, m_norm_g, m_w_ada, m_b_ada, m_w_in, m_b_f, m_q_norm_g, m_w_uq, m_kv_norm_g, m_w_ukv, m_w_out, m_final_g, v_norm_g, v_w_ada, v_b_ada, v_w_in, v_b_f, v_q_norm_g, v_w_uq, v_kv_norm_g, v_w_ukv, v_w_out, v_final_g):
    S, D = x.shape[1], x.shape[2]
    L = norm_g.shape[0]
    FH = b_f.shape[1]
    QL, KVL = q_norm_g.shape[1], kv_norm_g.shape[1]
    MH = w_ukv.shape[2] * N_CHIPS // (2 * HEAD_DIM)
    FW, MW = FH * HEAD_DIM, MH * HEAD_DIM
    NA = w_ada.shape[2]
    n_in = w_in.shape[2] * N_CHIPS
    cfg = types.SimpleNamespace(S=S, D=D, FW=FW, MW=MW, QL=QL, KVL=KVL, FH=FH, MH=MH)
    cfg.o_fq, cfg.o_fk, cfg.o_fv, cfg.o_fg, cfg.o_mg = 0, FW, 2 * FW, 3 * FW, 4 * FW
    cfg.o_ql = 4 * FW + MW
    cfg.o_kv = cfg.o_ql + QL
    cfg.o_ms = cfg.o_kv + KVL
    cfg.NZ = cfg.o_ms + LANES
    assert FW == MW and FH % 2 == 0 and MH % 2 == 0 and cfg.o_ql % QL == 0 and cfg.o_kv % KVL == 0 and KVL == LANES
    assert n_in == 4 * FW + FH + QL + KVL + ROPE_DIM + MW

    mx, my, mc = _my_pos()
    my_chip = 2 * mx + my
    my_dev = 2 * my_chip + mc

    inv_freq = 1.0 / (ROPE_THETA ** (jnp.arange(0, ROPE_DIM, 2, dtype=F32) / ROPE_DIM))
    ang = positions[0].astype(F32)[:, None] * inv_freq
    cos, sin = jnp.cos(ang), jnp.sin(ang)
    cfg.cos_t = jnp.concatenate([jnp.ones((S, ROPE_LO), F32), cos, cos, jnp.ones((S, HEAD_PAD - ROPE_LO - ROPE_DIM), F32)], axis=1)
    cfg.sin_t = jnp.concatenate([jnp.zeros((S, ROPE_LO), F32), -sin, sin, jnp.zeros((S, HEAD_PAD - ROPE_LO - ROPE_DIM), F32)], axis=1)

    assert L % 4 == 0
    def lane_pad(a):
        return jnp.pad(a, ((0, 0),) * (a.ndim - 1) + ((0, -a.shape[-1] % LANES),))

    shards = (w_in, w_uq, w_ukv, w_out)
    padded = [lane_pad(w.astype(BF16)) for w in shards]

    def kernel_layouts(gathered, own):
        def all_chips(g, w, axis):
            return jnp.concatenate([jnp.where(my_chip == t, w.astype(BF16), g[t][..., :w.shape[-1]])
                                    for t in range(N_CHIPS)], axis=axis)

        n = own[0].shape[0]
        w_in_f = all_chips(gathered[0], own[0], 2)
        w_uq_f = all_chips(gathered[1], own[1], 2)
        w_ukv_f = all_chips(gathered[2], own[2], 2)
        w_out_f = all_chips(gathered[3], own[3], 1)
        sizes = (FW, FW, FW, FH, FW, QL, KVL, ROPE_DIM, MW)
        offs = [0]
        for sz in sizes:
            offs.append(offs[-1] + sz)
        fq_w, fk_w, fv_w, ff_w, fg_w, ql_w, kvl_w, kr_w, mg_w = [w_in_f[:, :, offs[i]:offs[i + 1]] for i in range(len(sizes))]
        zeros = lambda width: jnp.zeros((n, D, width), BF16)
        w_in_p = jnp.concatenate([fq_w, fk_w, fv_w, fg_w, mg_w, ql_w, kvl_w, ff_w, zeros(ROPE_LO - FH), kr_w,
                                  zeros(HEAD_PAD - ROPE_LO - ROPE_DIM)], axis=2)
        w_uq_p = jnp.pad(w_uq_f.reshape(n, QL, MH, HEAD_DIM + ROPE_DIM),
                         ((0, 0), (0, 0), (0, 0), (0, HEAD_PAD - HEAD_DIM - ROPE_DIM))).reshape(n, QL, MH * HEAD_PAD)
        w_ukv4 = w_ukv_f.reshape(n, KVL, MH, 2 * HEAD_DIM)
        w_uk_p = jnp.pad(w_ukv4[..., :HEAD_DIM], ((0, 0), (0, 0), (0, 0), (0, HEAD_PAD - HEAD_DIM))).reshape(n, KVL, MH * HEAD_PAD)
        return types.SimpleNamespace(w_in=w_in_p, w_uq=w_uq_p, w_uk=w_uk_p, w_v=w_ukv4[..., HEAD_DIM:].reshape(n, KVL, MW),
                                     w_out=w_out_f)

    first = kernel_layouts(weights_gather([p[:1] for p in padded], "weights_first"), [w[:1] for w in shards])

    c_all = allgather8(c.reshape(8, D // 8), "gather_c").reshape(N_DEV, D)
    c_pad = jnp.pad(c_all, ((0, 16 - N_DEV), (0, 0)))
    mod_part = ada_forward(c_pad, w_ada)[:, :N_DEV, :]
    mod_all = allgather8(mod_part.reshape(-1, LANES), "gather_mod").reshape(N_CHIPS, 2, L, N_DEV, NA)[:, 0]
    mod_full = mod_all.transpose(1, 2, 0, 3).reshape(L, N_DEV, N_CHIPS * NA) + b_ada[:, None, :]
    mod_mine = lax.dynamic_index_in_dim(mod_full, my_dev, axis=1, keepdims=True)

    step = make_step(cfg)
    bf_pad = jnp.pad(b_f, ((0, 0), (0, LANES - FH)))
    mods = [(mod_mine[l, :, :D], mod_mine[l, :, D:2 * D], mod_mine[l, :, 2 * D:]) for l in range(L)]

    def layer_params(l, ws_, at):
        return types.SimpleNamespace(l=at, norm_g=norm_g[l][None], bf_pad=bf_pad[l][None], gq=q_norm_g[l][None],
                                     gkv=kv_norm_g[l][None], w_in=ws_.w_in, w_uq=ws_.w_uq, w_uk=ws_.w_uk, w_v=ws_.w_v,
                                     w_out=ws_.w_out)

    layers = [layer_params(0, first, 0)]
    xl, sv, later = step.layer_forward(x[0], layers[0], mods[0], [p[1:] for p in padded])
    saved = [sv]
    rest = kernel_layouts(later, [w[1:] for w in shards])
    for l in range(1, L):
        layers.append(layer_params(l, rest, l - 1))
        xl, sv = step.layer_forward(xl, layers[l], mods[l])
        saved.append(sv)
    dx, acc_fin, loss_part = step.final_loss(xl, final_g[None], loss_target[0])
    loss = lax.psum(loss_part[0, 0], ("x", "y", "c"))
    ms = cfg.o_ms
    runs = [(0, 3 * FW, 0), (3 * FW, FH, ms), (3 * FW + FH, FW, cfg.o_fg), (4 * FW + FH, QL + KVL, cfg.o_ql),
            (4 * FW + FH + QL + KVL, ROPE_DIM, ms + ROPE_LO), (4 * FW + FH + QL + KVL + ROPE_DIM, MW, cfg.o_mg)]
    big_names = ["w_in", "w_uq", "w_ukv", "w_out"]

    def shard_columns(dw, t):
        a, b = t * w_in.shape[2], (t + 1) * w_in.shape[2]
        return jnp.concatenate([dw[:, p0 + max(a, r0) - r0:p0 + min(b, r0 + sz) - r0]
                                for r0, sz, p0 in runs if max(a, r0) < min(b, r0 + sz)], axis=1)

    def chip_parts(group, tag):
        n = len(group)
        stk = lambda name: jnp.stack([getattr(g, name) for g in group])
        g_in4 = jnp.stack([jnp.stack([shard_columns(g.w_in, t) for g in group]) for t in range(N_CHIPS)])
        dw_uq_f = stk("w_uq").reshape(n, QL, MH, HEAD_PAD)[..., :HEAD_DIM + ROPE_DIM].reshape(n, QL, -1)
        dw_ukv_f = jnp.concatenate([stk("w_uk").reshape(n, KVL, MH, HEAD_PAD)[..., :HEAD_DIM],
                                    stk("w_v").reshape(n, KVL, MH, HEAD_DIM)], axis=3).reshape(n, KVL, -1)
        gs = [g_in4, dw_uq_f.reshape(n, QL, N_CHIPS, -1).transpose(2, 0, 1, 3),
              dw_ukv_f.reshape(n, KVL, N_CHIPS, -1).transpose(2, 0, 1, 3), stk("w_out").reshape(n, N_CHIPS, -1, D).transpose(1, 0, 2, 3)]
        gs = [lane_pad(g.astype(BF16)) for g in gs]
        theirs = halves_to_sibling(gs, "grads_sibling_" + tag)
        out = []
        for g, o, nm in zip(gs, theirs, big_names):
            keep = lax.dynamic_slice_in_dim(g, mc * (n // 2), n // 2, axis=1)
            merged = (N_CHIPS * (n // 2),) + g.shape[2:]
            out.append(add_cast(keep.reshape(merged), o.reshape(merged), BF16, "grads_chip_sum_%s_%s" % (nm, tag)).reshape(o.shape))
        return out

    def reduced(chip_part, parts, tag):
        where = jnp.stack([my_chip, mc]).astype(jnp.int32)
        red = [sum_chips(p, o, where, "grads_sum_%s_%s" % (nm, tag)) for p, o, nm in zip(parts, chip_part, big_names)]
        return [g[..., :w.shape[-1]] for g, w in zip(halves_gather(red, "grads_back_" + tag), shards)]

    gl = [None] * L
    half_l = L // 2
    for l in range(L - 1, half_l - 1, -1):
        dx, gl[l] = step.layer_backward(dx, saved[l], layers[l], mods[l])
    part_hi = chip_parts(gl[half_l:], "hi")
    dx, gl[half_l - 1], parts_hi = step.layer_backward(dx, saved[half_l - 1], layers[half_l - 1], mods[half_l - 1],
                                                       ExchangeOverIci(part_hi))
    for l in range(half_l - 2, -1, -1):
        dx, gl[l] = step.layer_backward(dx, saved[l], layers[l], mods[l])
    part_lo = chip_parts(gl[:half_l], "lo")
    parts_lo = run_side(ExchangeOverIci(part_lo), "grads_chips_lo")
    g_w_in, g_w_uq, g_w_ukv, g_w_out = [jnp.concatenate([lo, hi]) for lo, hi in
                                        zip(reduced(part_lo, parts_lo, "lo"), reduced(part_hi, parts_hi, "hi"))]
    grad_x = dx[None]

    stack = lambda name: jnp.stack([getattr(g, name) for g in gl])
    small_parts = [stack("norm_g"), stack("dmod"), stack("b_f"), stack("gq"), stack("gkv"), acc_fin[0]]
    small_shapes = [p.shape for p in small_parts]
    small = _pack_rows(small_parts, F32, 8).reshape(-1, LANES)
    small_all = allgather8(small, "gather_small").reshape(N_DEV, -1, LANES)
    small_sum = sum_leading(small_all, "sum_small")
    g_norm_g, g_b_ada, g_b_f, g_q_norm_g, g_kv_norm_g, g_final_g = _unpack(small_sum.reshape(-1), small_shapes)

    n_ng = L * D
    dmod_all = small_all.reshape(N_DEV, -1)[:, n_ng:n_ng + L * 3 * D].reshape(N_DEV, L, 3 * D)
    dmod_cols = lax.dynamic_slice_in_dim(dmod_all, my_chip * NA, NA, axis=2).transpose(1, 0, 2)
    g_w_ada = ada_backward(c_pad, jnp.pad(dmod_cols, ((0, 0), (0, 16 - N_DEV), (0, 0))))


    names = ["norm_g", "w_ada", "b_ada", "w_in", "b_f", "q_norm_g", "w_uq", "kv_norm_g", "w_ukv", "w_out", "final_g"]
    ws = dict(norm_g=norm_g, w_ada=w_ada, b_ada=b_ada, w_in=w_in, b_f=b_f, q_norm_g=q_norm_g, w_uq=w_uq,
              kv_norm_g=kv_norm_g, w_ukv=w_ukv, w_out=w_out, final_g=final_g)
    msd = dict(norm_g=m_norm_g, w_ada=m_w_ada, b_ada=m_b_ada, w_in=m_w_in, b_f=m_b_f, q_norm_g=m_q_norm_g, w_uq=m_w_uq,
               kv_norm_g=m_kv_norm_g, w_ukv=m_w_ukv, w_out=m_w_out, final_g=m_final_g)
    vsd = dict(norm_g=v_norm_g, w_ada=v_w_ada, b_ada=v_b_ada, w_in=v_w_in, b_f=v_b_f, q_norm_g=v_q_norm_g, w_uq=v_w_uq,
               kv_norm_g=v_kv_norm_g, w_ukv=v_w_ukv, w_out=v_w_out, final_g=v_final_g)
    gsd = dict(norm_g=g_norm_g, w_ada=g_w_ada, b_ada=g_b_ada, w_in=g_w_in, b_f=g_b_f, q_norm_g=g_q_norm_g, w_uq=g_w_uq,
               kv_norm_g=g_kv_norm_g, w_ukv=g_w_ukv, w_out=g_w_out, final_g=g_final_g)
    small_names = ["norm_g", "b_ada", "b_f", "q_norm_g", "kv_norm_g", "final_g"]
    sm_shapes = [ws[n].shape for n in small_names]
    pk = lambda d: _pack_rows([d[n] for n in small_names], F32, 8).reshape(1, -1, LANES)
    sm_out = adamw(pk(ws), pk(gsd), pk(msd), pk(vsd), "adamw_small")
    sm_d, sm_m, sm_v = [dict(zip(small_names, _unpack(o.reshape(-1), sm_shapes))) for o in sm_out]
    delta, new_m, new_v = dict(sm_d), dict(sm_m), dict(sm_v)
    for n in ["w_ada", "w_in", "w_uq", "w_ukv", "w_out"]:
        delta[n], new_m[n], new_v[n] = adamw(ws[n], gsd[n], msd[n], vsd[n], "adamw_" + n)

    return (loss, grad_x, *[gsd[n] for n in names], *[delta[n] for n in names],
            *[new_m[n] for n in names], *[new_v[n] for n in names])
```

```python
import types

import jax
import jax.numpy as jnp
from jax import lax
from jax.experimental import pallas as pl
from jax.experimental.pallas import tpu as pltpu

F32 = jnp.float32
BF16 = jnp.bfloat16
MESH = pl.DeviceIdType.MESH

N_CHIPS = 4
N_DEV = 8
HEAD_DIM = 64
ROPE_DIM = 32
ROPE_THETA = 10000.0
HEAD_PAD = 128
ROPE_LO = 64
ROPE_HALF = 16
CHUNK_SHIFT = 6
LANES = 128
EPS = 1e-6
NEG = -1e30
ADAM_LR = 0.001
ADAM_B1 = 0.9
ADAM_B2 = 0.999
ADAM_EPS = 1e-08
ADAM_WD = 0.01
ADAM_STEP = 10
VMEM_LIMIT = 48 * 1024 * 1024
SLAB_COLS = 1024


def _params(sem=None, vmem=VMEM_LIMIT):
    return pltpu.CompilerParams(dimension_semantics=sem, vmem_limit_bytes=vmem)


def _nn(a, b):
    return jnp.dot(a, b, preferred_element_type=F32)


def _nt(a, b):
    return lax.dot_general(a, b, (((1,), (1,)), ((), ())), preferred_element_type=F32)


def _tn(a, b):
    return lax.dot_general(a, b, (((0,), (0,)), ((), ())), preferred_element_type=F32)


def _sigmoid(x):
    return 1.0 / (1.0 + jnp.exp(-x))


def _lane(shape):
    return lax.broadcasted_iota(jnp.int32, shape, len(shape) - 1)


def _pick(n, cands):
    for c in cands:
        if n % c == 0:
            return c
    return n


def _my_pos():
    return lax.axis_index("x"), lax.axis_index("y"), lax.axis_index("c")


def allgather8(xs, name):
    m_per, n = xs.shape

    def body(x_ref, out_ref, send_sems, recv_sems, local_sem):
        x, y, c = _my_pos()
        me, sibling = (x, y, c), (x, y, 1 - c)
        chips = [(1 - x, y), (x, 1 - y), (1 - x, 1 - y)]

        def rows(px, py, pc):
            return out_ref.at[pl.ds((4 * px + 2 * py + pc) * m_per, m_per), :]

        def copy(k, block, to, src=None):
            return pltpu.make_async_remote_copy(
                src_ref=rows(*block) if src is None else src, dst_ref=rows(*block),
                send_sem=send_sems.at[k], recv_sem=recv_sems.at[k], device_id=to, device_id_type=MESH)

        mine = pltpu.make_async_copy(x_ref, rows(*me), local_sem)
        mine.start()
        first = [copy(0, me, sibling, src=x_ref)]
        first += [copy(1 + j, me, (*chip, c), src=x_ref) for j, chip in enumerate(chips)]
        for cp in first:
            cp.start()
        passed = [copy(4 + j, (*chip, c), sibling) for j, chip in enumerate(chips)]
        for j, chip in enumerate(chips):
            copy(1 + j, (*chip, c), me).wait_recv()
            passed[j].start()
        copy(0, sibling, me).wait_recv()
        for j, chip in enumerate(chips):
            copy(4 + j, (*chip, 1 - c), me).wait_recv()
        for cp in first + passed:
            cp.wait_send()
        mine.wait()

    return pl.pallas_call(
        body, name=name,
        out_shape=jax.ShapeDtypeStruct((N_DEV * m_per, n), xs.dtype),
        in_specs=[pl.BlockSpec(memory_space=pltpu.VMEM)],
        out_specs=pl.BlockSpec(memory_space=pltpu.VMEM),
        scratch_shapes=[pltpu.SemaphoreType.DMA((7,)), pltpu.SemaphoreType.DMA((7,)), pltpu.SemaphoreType.DMA],
    )(xs)


def _remote(src, dst, send_sems, recv_sems, k, to):
    return pltpu.make_async_remote_copy(src_ref=src, dst_ref=dst, send_sem=send_sems.at[k], recv_sem=recv_sems.at[k],
                                        device_id=to, device_id_type=MESH)


def _hbm_call(body, name, ins, out_shapes, n_sems, aliases=None):
    hbm = pl.BlockSpec(memory_space=pl.ANY)
    scratch = [pltpu.SemaphoreType.DMA((n_sems,)), pltpu.SemaphoreType.DMA((n_sems,))]
    return pl.pallas_call(body, name=name, out_shape=out_shapes, in_specs=[hbm] * len(ins),
                          out_specs=[hbm] * len(out_shapes), scratch_shapes=scratch,
                          input_output_aliases=aliases or {})(*ins)


def _layers_half(ref, h, axis=0):
    size = ref.shape[axis] // 2
    idx = (slice(None),) * axis + (pl.ds(h * size, size),)
    return ref.at[idx]


class GatherOverIci:
    def __init__(self, ws):
        self.ins = list(ws)
        self.outs = [jax.ShapeDtypeStruct((N_CHIPS,) + w.shape, w.dtype) for w in ws]
        self.aliases = {}
        self.n_sems = 3 * len(ws)

    def _copies(self, x_refs, o_refs, send_sems, recv_sems):
        x, y, c = _my_pos()
        s = 2 * x + y
        out = []
        for a in range(len(x_refs)):
            for k, (tx, ty) in enumerate([(1 - x, y), (x, 1 - y), (1 - x, 1 - y)]):
                mine = _remote(_layers_half(x_refs[a], c, 1), _layers_half(o_refs[a].at[s], c, 1), send_sems, recv_sems,
                               3 * a + k, (tx, ty, c))
                got = _layers_half(o_refs[a].at[2 * tx + ty], c, 1)
                out.append((mine, _remote(got, got, send_sems, recv_sems, 3 * a + k, (tx, ty, c))))
        return out

    def start(self, x_refs, o_refs, send_sems, recv_sems):
        for mine, _ in self._copies(x_refs, o_refs, send_sems, recv_sems):
            mine.start()

    def wait(self, x_refs, o_refs, send_sems, recv_sems):
        for mine, theirs in self._copies(x_refs, o_refs, send_sems, recv_sems):
            theirs.wait_recv()
            mine.wait_send()


class GatherToSibling:
    def __init__(self, gathered):
        self.ins = list(gathered)
        self.outs = [jax.ShapeDtypeStruct(g.shape, g.dtype) for g in gathered]
        self.aliases = {a: a for a in range(len(gathered))}
        self.n_sems = 3 * len(gathered)

    def _copies(self, o_refs, send_sems, recv_sems):
        x, y, c = _my_pos()
        out = []
        for a in range(len(o_refs)):
            for k, (tx, ty) in enumerate([(1 - x, y), (x, 1 - y), (1 - x, 1 - y)]):
                got = _layers_half(o_refs[a].at[2 * tx + ty], c, 1)
                theirs = _layers_half(o_refs[a].at[2 * tx + ty], 1 - c, 1)
                out.append((_remote(got, got, send_sems, recv_sems, 3 * a + k, (x, y, 1 - c)),
                            _remote(theirs, theirs, send_sems, recv_sems, 3 * a + k, (x, y, 1 - c))))
        return out

    def start(self, x_refs, o_refs, send_sems, recv_sems):
        for mine, _ in self._copies(o_refs, send_sems, recv_sems):
            mine.start()

    def wait(self, x_refs, o_refs, send_sems, recv_sems):
        for mine, theirs in self._copies(o_refs, send_sems, recv_sems):
            theirs.wait_recv()
            mine.wait_send()


def run_side(side, name):
    n_in, n_out = len(side.ins), len(side.outs)

    def body(*refs):
        parts = refs[:n_in], refs[n_in:n_in + n_out], refs[n_in + n_out], refs[n_in + n_out + 1]
        side.start(*parts)
        side.wait(*parts)

    return _hbm_call(body, name, side.ins, side.outs, side.n_sems, aliases=side.aliases)


def weights_gather(ws, name):
    return run_side(GatherToSibling(run_side(GatherOverIci(ws), name + "_ici")), name + "_sibling")


def halves_to_sibling(gs, name):
    n = len(gs)

    def body(*refs):
        x_refs, o_refs, send_sems, recv_sems = refs[:n], refs[n:2 * n], refs[2 * n], refs[2 * n + 1]
        x, y, c = _my_pos()
        cps = [_remote(_layers_half(x_refs[a], 1 - c, axis=1), o_refs[a], send_sems, recv_sems, a, (x, y, 1 - c))
               for a in range(n)]
        for cp in cps:
            cp.start()
        for cp in cps:
            cp.wait()

    outs = [jax.ShapeDtypeStruct((g.shape[0], g.shape[1] // 2) + g.shape[2:], g.dtype) for g in gs]
    return _hbm_call(body, name, gs, outs, n)


class ExchangeOverIci:
    def __init__(self, xs):
        self.ins = list(xs)
        self.outs = [jax.ShapeDtypeStruct(v.shape, v.dtype) for v in xs]
        self.aliases = {}
        self.n_sems = 3 * len(xs)

    def _copies(self, x_refs, o_refs, send_sems, recv_sems):
        x, y, c = _my_pos()
        s = 2 * x + y
        out = []
        for a in range(len(x_refs)):
            for k, (tx, ty) in enumerate([(1 - x, y), (x, 1 - y), (1 - x, 1 - y)]):
                got = o_refs[a].at[2 * tx + ty]
                out.append((_remote(x_refs[a].at[2 * tx + ty], o_refs[a].at[s], send_sems, recv_sems, 3 * a + k, (tx, ty, c)),
                            _remote(got, got, send_sems, recv_sems, 3 * a + k, (tx, ty, c))))
        return out

    def start(self, x_refs, o_refs, send_sems, recv_sems):
        for mine, _ in self._copies(x_refs, o_refs, send_sems, recv_sems):
            mine.start()

    def wait(self, x_refs, o_refs, send_sems, recv_sems):
        for mine, theirs in self._copies(x_refs, o_refs, send_sems, recv_sems):
            theirs.wait_recv()
            mine.wait_send()


def halves_gather(xs, name):
    n = len(xs)

    def body(*refs):
        x_refs, o_refs, send_sems, recv_sems = refs[:n], refs[n:2 * n], refs[2 * n], refs[2 * n + 1]
        x, y, c = _my_pos()
        sends = []
        for a in range(n):
            cp = _remote(_layers_half(x_refs[a], c), _layers_half(o_refs[a], c), send_sems, recv_sems, a, (x, y, 1 - c))
            cp.start()
            sends.append(cp)
        for a in range(n):
            theirs = _layers_half(o_refs[a], 1 - c)
            _remote(theirs, theirs, send_sems, recv_sems, a, (x, y, 1 - c)).wait_recv()
        for cp in sends:
            cp.wait_send()

    outs = [jax.ShapeDtypeStruct(v.shape, v.dtype) for v in xs]
    return _hbm_call(body, name, xs, outs, n, aliases={a: a for a in range(n)})


def sum_leading(xs, name):
    n, r, c = xs.shape

    def body(x_ref, o_ref):
        acc = x_ref[0]
        for i in range(1, n):
            acc = acc + x_ref[i]
        o_ref[...] = acc

    return pl.pallas_call(body, name=name, out_shape=jax.ShapeDtypeStruct((r, c), xs.dtype))(xs)


def add_cast(a, b, out_dtype, name):
    n, r, c = a.shape
    tr = _pick(r, (512, 256, 128, 64, 32, 16))

    def body(a_ref, b_ref, o_ref):
        o_ref[...] = (a_ref[...].astype(F32) + b_ref[...].astype(F32)).astype(out_dtype)

    spec = pl.BlockSpec((1, tr, c), lambda i, j: (i, j, 0))
    return pl.pallas_call(body, name=name, grid=(n, r // tr), in_specs=[spec, spec], out_specs=spec,
                          out_shape=jax.ShapeDtypeStruct((n, r, c), out_dtype),
                          compiler_params=_params(("parallel", "parallel")))(a, b)


def sum_chips(parts, own, where, name):
    n, nl, r, c = parts.shape
    tr = _pick(r, (512, 256, 128, 64, 32, 16))

    def body(w_ref, p_ref, o_ref, out_ref):
        s = w_ref[0]
        acc = jnp.zeros(out_ref.shape, F32)
        for t in range(n):
            acc = acc + jnp.where(s == t, o_ref[0], p_ref[t]).astype(F32)
        out_ref[...] = acc

    grid_spec = pltpu.PrefetchScalarGridSpec(
        num_scalar_prefetch=1, grid=(nl, r // tr),
        in_specs=[pl.BlockSpec((n, 1, tr, c), lambda i, j, w: (0, i, j, 0)),
                  pl.BlockSpec((1, 1, tr, c), lambda i, j, w: (w[0], i, j, 0))],
        out_specs=pl.BlockSpec((1, tr, c), lambda i, j, w: (w[1] * nl + i, j, 0)))
    return pl.pallas_call(body, name=name, grid_spec=grid_spec, out_shape=jax.ShapeDtypeStruct((2 * nl, r, c), F32),
                          compiler_params=_params(("parallel", "parallel")))(where, parts, own)


def ada_forward(c_all, w_ada):
    nl, d, n = w_ada.shape
    nb = c_all.shape[0]

    def body(c_ref, w_ref, o_ref):
        cv = c_ref[...]
        ca = (cv * _sigmoid(cv)).astype(BF16)
        o_ref[0] = _nn(ca, w_ref[0].astype(BF16))

    return pl.pallas_call(
        body, name="ada_forward", grid=(nl,),
        in_specs=[pl.BlockSpec((nb, d), lambda l: (0, 0)), pl.BlockSpec((1, d, n), lambda l: (l, 0, 0))],
        out_specs=pl.BlockSpec((1, nb, n), lambda l: (l, 0, 0)),
        out_shape=jax.ShapeDtypeStruct((nl, nb, n), F32), compiler_params=_params(("parallel",)))(c_all, w_ada)


def ada_backward(c_all, dmod):
    nl, nb, n = dmod.shape
    d = c_all.shape[1]

    def body(c_ref, g_ref, o_ref):
        cv = c_ref[...]
        ca = (cv * _sigmoid(cv)).astype(BF16)
        o_ref[0] = _tn(ca, g_ref[0].astype(BF16))

    return pl.pallas_call(
        body, name="ada_backward", grid=(nl,),
        in_specs=[pl.BlockSpec((nb, d), lambda l: (0, 0)), pl.BlockSpec((1, nb, n), lambda l: (l, 0, 0))],
        out_specs=pl.BlockSpec((1, d, n), lambda l: (l, 0, 0)),
        out_shape=jax.ShapeDtypeStruct((nl, d, n), F32), compiler_params=_params(("parallel",)))(c_all, dmod)


def matmul_tn(a, b, name):
    k, m = a.shape
    n = b.shape[1]
    tm, tk = _pick(m, (512, 256, 128)), _pick(k, (512, 256, 128))
    tn = n if n * (tm * 8 + tk * 4) <= VMEM_LIMIT // 2 else _pick(n, (512, 256, 128))

    def body(a_ref, b_ref, o_ref):
        @pl.when(pl.program_id(2) == 0)
        def _():
            o_ref[...] = jnp.zeros_like(o_ref)

        o_ref[...] += _tn(a_ref[...], b_ref[...])

    return pl.pallas_call(
        body, name=name, grid=(m // tm, n // tn, k // tk),
        in_specs=[pl.BlockSpec((tk, tm), lambda i, j, kk: (kk, i)), pl.BlockSpec((tk, tn), lambda i, j, kk: (kk, j))],
        out_specs=pl.BlockSpec((tm, tn), lambda i, j, kk: (i, j)),
        out_shape=jax.ShapeDtypeStruct((m, n), F32),
        compiler_params=_params(("parallel", "parallel", "arbitrary")))(a, b)


def adamw(w, g, m, v, name):
    nl, r, c = w.shape
    tr = _pick(r, (512, 256, 128, 64, 32, 16, 8))

    def body(w_ref, g_ref, m_ref, v_ref, d_ref, mo_ref, vo_ref):
        gv = g_ref[...]
        mn = ADAM_B1 * m_ref[...] + (1.0 - ADAM_B1) * gv
        vn = ADAM_B2 * v_ref[...] + (1.0 - ADAM_B2) * (gv * gv)
        m_hat = mn / (1.0 - ADAM_B1 ** ADAM_STEP)
        v_hat = vn / (1.0 - ADAM_B2 ** ADAM_STEP)
        d_ref[...] = -ADAM_LR * (m_hat / (jnp.sqrt(v_hat) + ADAM_EPS) + ADAM_WD * w_ref[...])
        mo_ref[...] = mn
        vo_ref[...] = vn

    spec = pl.BlockSpec((1, tr, c), lambda l, i: (l, i, 0))
    out = jax.ShapeDtypeStruct((nl, r, c), F32)
    return pl.pallas_call(body, name=name, grid=(nl, r // tr), in_specs=[spec] * 4, out_specs=[spec] * 3,
                          out_shape=[out] * 3, compiler_params=_params(("parallel", "parallel")))(w, g, m, v)


def _rope(t, cos_t, sin_t):
    w = t.shape[1]
    lane = _lane(t.shape) & (HEAD_PAD - 1)
    first_half = (lane >= ROPE_LO) & (lane < ROPE_LO + ROPE_HALF)
    partner = jnp.where(first_half, pltpu.roll(t, w - ROPE_HALF, 1), pltpu.roll(t, ROPE_HALF, 1))
    return t * cos_t + partner * sin_t


def _rope_t(dt, cos_t, sin_t):
    w = dt.shape[1]
    lane = _lane(dt.shape) & (HEAD_PAD - 1)
    first_half = (lane >= ROPE_LO) & (lane < ROPE_LO + ROPE_HALF)
    ds = dt * sin_t
    partner = jnp.where(first_half, pltpu.roll(ds, w - ROPE_HALF, 1), pltpu.roll(ds, ROPE_HALF, 1))
    return dt * cos_t + partner


def _rms(xv, g):
    rstd = lax.rsqrt(jnp.mean(xv * xv, axis=-1, keepdims=True) + EPS)
    xh = xv * rstd
    return xh * g, xh, rstd


def _rms_bwd(dy, g, xh, rstd):
    dxh = dy * g
    dx = rstd * (dxh - xh * jnp.mean(dxh * xh, axis=-1, keepdims=True))
    return dx, jnp.sum(dy * xh, axis=0, keepdims=True)


def make_step(cfg):
    S, D, NZ = cfg.S, cfg.D, cfg.NZ
    FW, MW, QL, KVL, FH, MH = cfg.FW, cfg.MW, cfg.QL, cfg.KVL, cfg.FH, cfg.MH
    QW = MH * HEAD_PAD
    TM = _pick(S, (512, 256, 128))
    TQ = TK = _pick(S, (256, 128))
    n_tok = S // TM
    ZO = 3 * FW
    NZR = NZ - ZO
    misc_blk = (cfg.o_ms - ZO) // LANES
    FWD_UNROLL = 2
    FOX_SCALE = HEAD_DIM ** -0.5
    MLA_SCALE = (HEAD_DIM + ROPE_DIM) ** -0.5

    def tok(width, col=0):
        return pl.BlockSpec((TM, width), lambda i: (i, col))

    def const(shape):
        return pl.BlockSpec(shape, lambda i: tuple(0 for _ in shape))

    def layer(l, shape):
        return pl.BlockSpec((None,) + shape, lambda i: (l,) + tuple(0 for _ in shape))


    def ln_inproj(x, g, scale, shift, w_in, l):
        def body(x_ref, g_ref, sc_ref, sh_ref, w_ref, h_ref, z_ref, qkv_ref):
            y, _, _ = _rms(x_ref[...], g_ref[...])
            hb = (y * (1.0 + sc_ref[...]) + sh_ref[...]).astype(BF16)
            h_ref[...] = hb
            z = _nn(hb, w_ref[...])
            z_ref[...] = z[:, ZO:]
            qkv_ref[:, :FW] = (z[:, :FW] * FOX_SCALE).astype(BF16)
            qkv_ref[:, FW:] = z[:, FW:ZO].astype(BF16)

        return pl.pallas_call(
            body, name="ln_inproj", grid=(n_tok,),
            in_specs=[tok(D), const((1, D)), const((1, D)), const((1, D)), layer(l, (D, NZ))],
            out_specs=[tok(D), tok(NZR), tok(ZO)],
            out_shape=[jax.ShapeDtypeStruct((S, D), BF16), jax.ShapeDtypeStruct((S, NZR), F32),
                       jax.ShapeDtypeStruct((S, ZO), BF16)],
            compiler_params=_params(("parallel",)))(x, g, scale, shift, w_in)

    def _log_f_terms(misc, bf):
        lane = _lane(misc.shape)
        a = misc + bf
        e = jnp.exp(-jnp.abs(a))
        logf = jnp.minimum(a, 0.0) - jnp.log(1.0 + e)
        sig_neg = jnp.where(a >= 0, e, 1.0) / (1.0 + e)
        valid = lane < FH
        return jnp.where(valid, logf, 0.0), jnp.where(valid, sig_neg, 0.0)

    def fox_prep(z, bf_pad):
        def body(z_ref, b_ref, o_ref, carry):
            @pl.when(pl.program_id(0) == 0)
            def _():
                carry[...] = jnp.zeros_like(carry)

            logf, _ = _log_f_terms(z_ref[...], b_ref[...])
            row = lax.broadcasted_iota(jnp.int32, (TM, TM), 0)
            col = lax.broadcasted_iota(jnp.int32, (TM, TM), 1)
            tri = (col <= row).astype(F32)
            cum = jnp.dot(tri, logf, precision=lax.Precision.HIGHEST, preferred_element_type=F32) + carry[...]
            o_ref[...] = cum
            carry[...] = cum[TM - 1:TM, :]

        return pl.pallas_call(
            body, name="fox_prep", grid=(n_tok,),
            in_specs=[tok(LANES, misc_blk), const((1, LANES))], out_specs=tok(LANES),
            out_shape=jax.ShapeDtypeStruct((S, LANES), F32),
            scratch_shapes=[pltpu.VMEM((1, LANES), F32)],
            compiler_params=_params(("arbitrary",)))(z, bf_pad)

    def mla_prep(z, gq, gkv, w_uq, w_uk, w_v, l, cos_t, sin_t):
        def body(ql_ref, kvl_ref, ms_ref, gq_ref, gkv_ref, wq_ref, wk_ref, wv_ref, c_ref, s_ref, q_ref, k_ref, v_ref):
            cos1, sin1 = c_ref[...], s_ref[...]
            qn, _, _ = _rms(ql_ref[...], gq_ref[...])
            q = _nn(qn.astype(BF16), wq_ref[...])
            q_ref[...] = (_rope(q, jnp.tile(cos1, (1, MH)), jnp.tile(sin1, (1, MH))) * MLA_SCALE).astype(BF16)
            kvn, _, _ = _rms(kvl_ref[...], gkv_ref[...])
            kvb = kvn.astype(BF16)
            lane = _lane((TM, LANES))
            kr = jnp.where((lane >= ROPE_LO) & (lane < ROPE_LO + ROPE_DIM), ms_ref[...], 0.0)
            kr = _rope(kr, cos1, sin1)
            k_ref[...] = (_nn(kvb, wk_ref[...]) + jnp.tile(kr, (1, MH))).astype(BF16)
            v_ref[...] = _nn(kvb, wv_ref[...]).astype(BF16)

        return pl.pallas_call(
            body, name="mla_prep", grid=(n_tok,),
            in_specs=[tok(QL, (cfg.o_ql - ZO) // QL), tok(KVL, (cfg.o_kv - ZO) // KVL), tok(LANES, misc_blk),
                      const((1, QL)), const((1, KVL)), layer(l, (QL, QW)), layer(l, (KVL, QW)), layer(l, (KVL, MW)),
                      tok(LANES), tok(LANES)],
            out_specs=[tok(QW), tok(QW), tok(MW)],
            out_shape=[jax.ShapeDtypeStruct((S, QW), BF16), jax.ShapeDtypeStruct((S, QW), BF16),
                       jax.ShapeDtypeStruct((S, MW), BF16)],
            compiler_params=_params(("parallel",)))(z, z, z, gq, gkv, w_uq, w_uk, w_v, cos_t, sin_t)

    def _allowed(q0, k0, chunked):
        qi = q0 + lax.broadcasted_iota(jnp.int32, (TQ, TK), 0)
        ki = k0 + lax.broadcasted_iota(jnp.int32, (TQ, TK), 1)
        if chunked:
            return (ki >> CHUNK_SHIFT) <= (qi >> CHUNK_SHIFT)
        return ki <= qi

    def _heads(val, packed):
        if packed:
            lane = _lane(val.shape)
            zero = jnp.zeros_like(val)
            return [jnp.where(lane < HEAD_DIM, val, zero), jnp.where(lane >= HEAD_DIM, val, zero)]
        return [val[:, :HEAD_PAD], val[:, HEAD_PAD:]]

    def _merge(a0, a1):
        return jnp.where(_lane(a0.shape) < HEAD_DIM, a0, a1)

    def attn_fwd(q, k, v, q_blk0, k_blk0, v_blk0, cumt, packed, chunked, name, side=None):
        has_bias = cumt is not None
        n_pairs = (FH if packed else MH) // 2
        wq = LANES if packed else 2 * HEAD_PAD
        assert TQ == TK

        n_main = 4 if has_bias else 3
        n_side_in, n_side_out = (len(side.ins), len(side.outs)) if side else (0, 0)

        def body(*refs):
            if has_bias:
                q_ref, k_ref, v_ref, ct_ref = refs[:n_main]
            else:
                q_ref, k_ref, v_ref = refs[:n_main]
            o_ref, lse_ref = refs[n_main + n_side_in:n_main + n_side_in + 2]
            i = pl.program_id(1)
            if side:
                side_refs = (refs[n_main:n_main + n_side_in], refs[n_main + n_side_in + 2:n_main + n_side_in + 2 + n_side_out],
                             refs[-2], refs[-1])

                @pl.when((pl.program_id(0) == 0) & (i == 0))
                def _():
                    side.start(*side_refs)

            q0 = i * TQ
            qh = _heads(q_ref[...], packed)
            q_both = jnp.concatenate(qh, axis=0) if packed else None

            keys_first = True
            heads_axis = 1 if keys_first else 0

            def scores(kb):
                kblk = k_ref[pl.ds(pl.multiple_of(kb * TK, TK), TK), :]
                if keys_first:
                    if packed:
                        return _nt(kblk, q_both)
                    return jnp.concatenate([_nt(kblk[:, :HEAD_PAD], qh[0]), _nt(kblk[:, HEAD_PAD:], qh[1])], axis=1)
                if packed:
                    return _nt(q_both, kblk)
                return jnp.concatenate([_nt(qh[0], kblk[:, :HEAD_PAD]), _nt(qh[1], kblk[:, HEAD_PAD:])], axis=0)

            def update(kb, s, carry, masked):
                m, l, acc = carry
                k0 = pl.multiple_of(kb * TK, TK)
                s0, s1 = (s[:, :TQ], s[:, TQ:]) if keys_first else (s[:TQ], s[TQ:])
                if has_bias:
                    ck = ct_ref[0, :, pl.ds(k0, TK), :]
                    s0, s1 = s0 - jnp.tile(ck[0], (1, TQ // LANES)), s1 - jnp.tile(ck[1], (1, TQ // LANES))
                if masked:
                    allow = _allowed(q0, k0, chunked)
                    allow = jnp.transpose(allow) if keys_first else allow
                    s0, s1 = jnp.where(allow, s0, NEG), jnp.where(allow, s1, NEG)
                s = jnp.concatenate([s0, s1], axis=heads_axis)
                m_new = jnp.maximum(m, jnp.max(s, axis=1 - heads_axis, keepdims=True))
                p = jnp.exp(s - m_new)
                alpha = jnp.exp(m - m_new)
                l = alpha * l + jnp.sum(p, axis=1 - heads_axis, keepdims=True)
                vblk = v_ref[pl.ds(k0, TK), :]
                pv = _tn(vblk, p.astype(BF16)) if keys_first else _nn(p.astype(BF16), vblk)
                return m_new, l, alpha * acc + pv

            def several(kb, carry, n, last_masked):
                ss = [scores(kb + u) for u in range(n)]
                for u in range(n):
                    carry = update(kb + u, ss[u], carry, last_masked and u == n - 1)
                return carry

            stat = (1, 2 * TQ) if keys_first else (2 * TQ, 1)
            init = (jnp.full(stat, NEG, F32), jnp.zeros(stat, F32),
                    jnp.zeros((LANES, 2 * TQ) if keys_first else (2 * TQ, LANES), F32))
            carry = lax.fori_loop(0, i // FWD_UNROLL, lambda t, cr: several(FWD_UNROLL * t, cr, FWD_UNROLL, False), init)
            m, l, acc = lax.switch(i % FWD_UNROLL,
                                   [lambda cr, r=r: several(i - r, cr, r + 1, True) for r in range(FWD_UNROLL)], carry)
            o = acc / l
            lse = m + jnp.log(l)
            if keys_first:
                o_ref[...] = jnp.transpose(jnp.concatenate([o[:HEAD_DIM, :TQ], o[HEAD_DIM:, TQ:]], axis=0))
                sub = lax.broadcasted_iota(jnp.int32, (LANES, TQ), 0)
                lse_ref[...] = jnp.transpose(jnp.where(sub == 0, lse[:, :TQ], jnp.where(sub == 1, lse[:, TQ:], 0.0)))
            else:
                lane = _lane((TQ, LANES))
                o_ref[...] = _merge(o[:TQ], o[TQ:])
                lse_ref[...] = jnp.where(lane == 0, lse[:TQ], jnp.where(lane == 1, lse[TQ:], 0.0))
            if side:
                @pl.when((pl.program_id(0) == n_pairs - 1) & (i == S // TQ - 1))
                def _():
                    side.wait(*side_refs)

        in_specs = [pl.BlockSpec((TQ, wq), lambda p, i: (i, q_blk0 + p)),
                    pl.BlockSpec((S, wq), lambda p, i: (0, k_blk0 + p)),
                    pl.BlockSpec((S, LANES), lambda p, i: (0, v_blk0 + p))]
        args = [q, k, v]
        if has_bias:
            in_specs += [pl.BlockSpec((1, 2, S, LANES), lambda p, i: (p, 0, 0, 0))]
            args += [cumt]
        out_specs = [pl.BlockSpec((TQ, LANES), lambda p, i: (i, p)), pl.BlockSpec((TQ, LANES), lambda p, i: (i, p))]
        out_shape = [jax.ShapeDtypeStruct((S, n_pairs * LANES), F32), jax.ShapeDtypeStruct((S, n_pairs * LANES), F32)]
        extra = {}
        if side:
            hbm = pl.BlockSpec(memory_space=pl.ANY)
            in_specs += [hbm] * n_side_in
            args += side.ins
            out_specs += [hbm] * n_side_out
            out_shape += side.outs
            extra = dict(scratch_shapes=[pltpu.SemaphoreType.DMA((side.n_sems,)), pltpu.SemaphoreType.DMA((side.n_sems,))],
                         input_output_aliases={n_main + a: 2 + b for a, b in side.aliases.items()})
        outs = pl.pallas_call(
            body, name=name, grid=(n_pairs, S // TQ), in_specs=in_specs, out_specs=out_specs, out_shape=out_shape,
            compiler_params=_params(("arbitrary", "arbitrary")), **extra)(*args)
        return (outs[0], outs[1], list(outs[2:])) if side else outs

    def gate_outproj(of, om, z, w_out, l, x, gate):
        def body(of_ref, om_ref, fg_ref, mg_ref, w_ref, x_ref, gt_ref, xn_ref, u_ref, y_ref):
            fg, mg = fg_ref[...], mg_ref[...]
            u = jnp.concatenate([of_ref[...] * fg * _sigmoid(fg), om_ref[...] * mg * _sigmoid(mg)], axis=1).astype(BF16)
            y = _nn(u, w_ref[...])
            u_ref[...] = u
            y_ref[...] = y.astype(BF16)
            xn_ref[...] = x_ref[...] + gt_ref[...] * y

        return pl.pallas_call(
            body, name="gate_outproj", grid=(n_tok,),
            in_specs=[tok(FW), tok(MW), tok(FW, (cfg.o_fg - ZO) // FW), tok(MW, (cfg.o_mg - ZO) // MW),
                      layer(l, (FW + MW, D)), tok(D), const((1, D))],
            out_specs=[tok(D), tok(FW + MW), tok(D)],
            out_shape=[jax.ShapeDtypeStruct((S, D), F32), jax.ShapeDtypeStruct((S, FW + MW), BF16),
                       jax.ShapeDtypeStruct((S, D), BF16)],
            compiler_params=_params(("parallel",)))(of, om, z, z, w_out, x, gate)

    def final_loss(x, g, target):
        def body(x_ref, g_ref, t_ref, dx_ref, acc_ref, loss_ref):
            @pl.when(pl.program_id(0) == 0)
            def _():
                acc_ref[...] = jnp.zeros_like(acc_ref)
                loss_ref[...] = jnp.zeros_like(loss_ref)

            gv = g_ref[...]
            y, xh, rstd = _rms(x_ref[...], gv)
            e = y - t_ref[...]
            loss_ref[...] += 0.5 * jnp.sum(jnp.sum(e * e, axis=-1, keepdims=True) / D, axis=0, keepdims=True)
            dx, dg = _rms_bwd(e / D, gv, xh, rstd)
            dx_ref[...] = dx
            acc_ref[0:1, :] += dg

        return pl.pallas_call(
            body, name="final_loss", grid=(n_tok,),
            in_specs=[tok(D), const((1, D)), tok(D)],
            out_specs=[tok(D), const((8, D)), const((1, LANES))],
            out_shape=[jax.ShapeDtypeStruct((S, D), F32), jax.ShapeDtypeStruct((8, D), F32),
                       jax.ShapeDtypeStruct((1, LANES), F32)],
            compiler_params=_params(("arbitrary",)))(x, g, target)


    def bwd_out(dxn, gate, y, w_out, l, of, om, z, lse_f, lse_m):
        def body(dx_ref, gt_ref, y_ref, w_ref, of_ref, om_ref, fg_ref, mg_ref, lf_ref, lm_ref,
                 dof_ref, dom_ref, dfg_ref, dmg_ref, dy_ref, acc_ref, sf_ref, sm_ref):
            @pl.when(pl.program_id(0) == 0)
            def _():
                acc_ref[...] = jnp.zeros_like(acc_ref)

            dxv = dx_ref[...]
            acc_ref[0:1, :] += jnp.sum(dxv * y_ref[...].astype(F32), axis=0, keepdims=True)
            dy = (gt_ref[...] * dxv).astype(BF16)
            dy_ref[...] = dy
            du = _nt(dy, w_ref[...])
            lane = _lane((TM, LANES))
            for lo, width, o_ref, g_ref, do_ref, dg_ref, l_ref, st_ref in (
                    (0, FW, of_ref, fg_ref, dof_ref, dfg_ref, lf_ref, sf_ref),
                    (FW, MW, om_ref, mg_ref, dom_ref, dmg_ref, lm_ref, sm_ref)):
                gv, ov = g_ref[...], o_ref[...]
                sg = _sigmoid(gv)
                dup = du[:, lo:lo + width]
                dob = (dup * gv * sg).astype(BF16)
                do_ref[...] = dob
                dg_ref[...] = (dup * ov * sg * (1.0 + gv * (1.0 - sg))).astype(BF16)
                d = dob.astype(F32) * ov
                for pr in range(width // LANES):
                    cols = slice(pr * LANES, (pr + 1) * LANES)
                    dp = d[:, cols]
                    d0 = jnp.sum(jnp.where(lane < HEAD_DIM, dp, 0.0), axis=-1, keepdims=True)
                    d1 = jnp.sum(jnp.where(lane >= HEAD_DIM, dp, 0.0), axis=-1, keepdims=True)
                    st_ref[:, cols] = jnp.where(lane == 2, d0, jnp.where(lane == 3, d1, l_ref[:, cols]))

        return pl.pallas_call(
            body, name="bwd_out", grid=(n_tok,),
            in_specs=[tok(D), const((1, D)), tok(D), layer(l, (FW + MW, D)), tok(FW), tok(MW),
                      tok(FW, (cfg.o_fg - ZO) // FW), tok(MW, (cfg.o_mg - ZO) // MW), tok(FW), tok(MW)],
            out_specs=[tok(FW), tok(MW), tok(FW), tok(MW), tok(D), const((8, D)), tok(FW), tok(MW)],
            out_shape=[jax.ShapeDtypeStruct((S, FW), BF16), jax.ShapeDtypeStruct((S, MW), BF16),
                       jax.ShapeDtypeStruct((S, FW), BF16), jax.ShapeDtypeStruct((S, MW), BF16),
                       jax.ShapeDtypeStruct((S, D), BF16), jax.ShapeDtypeStruct((8, D), F32),
                       jax.ShapeDtypeStruct((S, FW), F32), jax.ShapeDtypeStruct((S, MW), F32)],
            compiler_params=_params(("arbitrary",)))(dxn, gate, y, w_out, of, om, z, z, lse_f, lse_m)

    def attn_bwd(q, k, v, do, stats, q_blk0, k_blk0, v_blk0, cumt, packed, chunked, name, side=None):
        has_bias = cumt is not None
        n_pairs = (FH if packed else MH) // 2
        wq = LANES if packed else 2 * HEAD_PAD
        n_q = S // TQ
        assert TQ == TK
        n_in, n_out = (6, 5) if has_bias else (5, 3)
        n_side_in, n_side_out = (len(side.ins), len(side.outs)) if side else (0, 0)

        def body(*refs):
            main_out = refs[n_in + n_side_in:n_in + n_side_in + n_out]
            if has_bias:
                q_ref, k_ref, v_ref, do_ref, st_ref, ct_ref = refs[:n_in]
                dq_ref, dk_ref, dv_ref, dc_ref, dr_ref = main_out
            else:
                q_ref, k_ref, v_ref, do_ref, st_ref = refs[:n_in]
                dq_ref, dk_ref, dv_ref = main_out
            jb = pl.program_id(1)
            k0 = jb * TK
            if side:
                side_refs = (refs[n_in:n_in + n_side_in], refs[n_in + n_side_in + n_out:n_in + n_side_in + n_out + n_side_out],
                             refs[-2], refs[-1])

                @pl.when((pl.program_id(0) == 0) & (jb == 0))
                def _():
                    side.start(*side_refs)

            @pl.when(jb == 0)
            def _():
                dq_ref[...] = jnp.zeros_like(dq_ref)
                if has_bias:
                    dr_ref[...] = jnp.zeros_like(dr_ref)

            dk_ref[...] = jnp.zeros_like(dk_ref)
            dv_ref[...] = jnp.zeros_like(dv_ref)
            kh = _heads(k_ref[...], packed)
            k_both = jnp.concatenate(kh, axis=0) if packed else None
            v_both = jnp.concatenate(_heads(v_ref[...], True), axis=0)
            if has_bias:
                ct = jnp.transpose(ct_ref[0, 0])
                ck = jnp.concatenate([ct[:, 0:1], ct[:, 1:2]], axis=0)

            def products(ib):
                rows = pl.ds(pl.multiple_of(ib * TQ, TQ), TQ)
                q2, do2 = q_ref[rows, :], do_ref[rows, :]
                if packed:
                    s = _nt(k_both, q2)
                else:
                    qh = _heads(q2, False)
                    s = jnp.concatenate([_nt(kh[0], qh[0]), _nt(kh[1], qh[1])], axis=0)
                return s, _nt(v_both, do2)

            def update(ib, s, dp, carry, masked):
                q0 = pl.multiple_of(ib * TQ, TQ)
                rows = pl.ds(q0, TQ)
                q2, do2 = q_ref[rows, :], do_ref[rows, :]
                st = jnp.transpose(st_ref[rows, :])
                if not packed:
                    qh = _heads(q2, False)
                if has_bias:
                    s = s - ck
                if masked:
                    allow = jnp.transpose(_allowed(q0, k0, chunked))
                    s = jnp.where(jnp.concatenate([allow, allow], axis=0), s, NEG)
                p = jnp.concatenate([jnp.exp(s[:TK] - st[0:1, :]), jnp.exp(s[TK:] - st[1:2, :])], axis=0)
                dv2 = _nn(p.astype(BF16), do2)
                ds = jnp.concatenate([p[:TK] * (dp[:TK] - st[2:3, :]), p[TK:] * (dp[TK:] - st[3:4, :])], axis=0)
                dsb = ds.astype(BF16)
                dv_ref[...] += _merge(dv2[:TK], dv2[TK:])
                if packed:
                    dk2 = _nn(dsb, q2)
                    dk_ref[...] += _merge(dk2[:TK], dk2[TK:])
                    dq_ref[rows, :] += _tn(dsb, k_both)
                else:
                    dk_ref[...] += jnp.concatenate([_nn(dsb[:TK], qh[0]), _nn(dsb[TK:], qh[1])], axis=1)
                    dq_ref[rows, :] += jnp.concatenate([_tn(dsb[:TK], kh[0]), _tn(dsb[TK:], kh[1])], axis=1)
                if has_bias:
                    sub = lax.broadcasted_iota(jnp.int32, (8, TQ), 0)
                    r0 = jnp.sum(ds[:TK], axis=0, keepdims=True)
                    r1 = jnp.sum(ds[TK:], axis=0, keepdims=True)
                    dr_ref[0, ib] += jnp.where(sub == 0, r0, jnp.where(sub == 1, r1, 0.0))
                    return carry - jnp.sum(ds, axis=1, keepdims=True)
                return carry

            def step(ib, carry, masked):
                s, dp = products(ib)
                return update(ib, s, dp, carry, masked)

            def two(ib, carry):
                s_a, dp_a = products(ib)
                s_b, dp_b = products(ib + 1)
                return update(ib + 1, s_b, dp_b, update(ib, s_a, dp_a, carry, False), False)

            dc = step(jb, jnp.zeros((2 * TK, 1), F32), True)
            n_rest = n_q - 1 - jb
            dc = lax.fori_loop(0, n_rest // 2, lambda t, cr: two(jb + 1 + 2 * t, cr), dc)
            dc = lax.cond(n_rest % 2 == 1, lambda cr: step(n_q - 1, cr, False), lambda cr: cr, dc)
            if has_bias:
                lane = _lane((TK, LANES))
                dc_ref[0] = jnp.where(lane == 0, dc[:TK], jnp.where(lane == 1, dc[TK:], 0.0))
            if side:
                @pl.when((pl.program_id(0) == n_pairs - 1) & (jb == S // TK - 1))
                def _():
                    side.wait(*side_refs)

        in_specs = [pl.BlockSpec((S, wq), lambda p, j: (0, q_blk0 + p)),
                    pl.BlockSpec((TK, wq), lambda p, j: (j, k_blk0 + p)),
                    pl.BlockSpec((TK, LANES), lambda p, j: (j, v_blk0 + p)),
                    pl.BlockSpec((S, LANES), lambda p, j: (0, p)),
                    pl.BlockSpec((S, LANES), lambda p, j: (0, p))]
        args = [q, k, v, do, stats]
        out_specs = [pl.BlockSpec((S, wq), lambda p, j: (0, p)),
                     pl.BlockSpec((TK, wq), lambda p, j: (j, p)),
                     pl.BlockSpec((TK, LANES), lambda p, j: (j, p))]
        out_shape = [jax.ShapeDtypeStruct((S, n_pairs * wq), F32), jax.ShapeDtypeStruct((S, n_pairs * wq), F32),
                     jax.ShapeDtypeStruct((S, n_pairs * LANES), F32)]
        if has_bias:
            in_specs += [pl.BlockSpec((1, 1, 8, TK), lambda p, j: (p, j, 0, 0))]
            args += [cumt]
            out_specs += [pl.BlockSpec((1, TK, LANES), lambda p, j: (p, j, 0)),
                          pl.BlockSpec((1, S // TQ, 8, TQ), lambda p, j: (p, 0, 0, 0))]
            out_shape += [jax.ShapeDtypeStruct((n_pairs, S, LANES), F32),
                          jax.ShapeDtypeStruct((n_pairs, S // TQ, 8, TQ), F32)]
        extra = {}
        if side:
            hbm = pl.BlockSpec(memory_space=pl.ANY)
            in_specs += [hbm] * n_side_in
            args += side.ins
            out_specs += [hbm] * n_side_out
            out_shape += side.outs
            extra = dict(scratch_shapes=[pltpu.SemaphoreType.DMA((side.n_sems,)), pltpu.SemaphoreType.DMA((side.n_sems,))],
                         input_output_aliases={n_in + a: n_out + b for a, b in side.aliases.items()})
        outs = pl.pallas_call(
            body, name=name, grid=(n_pairs, S // TK), in_specs=in_specs, out_specs=out_specs, out_shape=out_shape,
            compiler_params=_params(("arbitrary", "arbitrary")), **extra)(*args)
        return (*outs[:n_out], list(outs[n_out:])) if side else outs

    def fox_post(dcum, z, bf_pad):
        def rev(i):
            return n_tok - 1 - i

        def body(dc_ref, z_ref, b_ref, dff_ref, acc_ref, carry):
            @pl.when(pl.program_id(0) == 0)
            def _():
                carry[...] = jnp.zeros_like(carry)
                acc_ref[...] = jnp.zeros_like(acc_ref)

            _, sig_neg = _log_f_terms(z_ref[...], b_ref[...])
            row = lax.broadcasted_iota(jnp.int32, (TM, TM), 0)
            col = lax.broadcasted_iota(jnp.int32, (TM, TM), 1)
            tri = (col >= row).astype(F32)
            dlog = jnp.dot(tri, dc_ref[...], precision=lax.Precision.HIGHEST, preferred_element_type=F32) + carry[...]
            carry[...] = dlog[0:1, :]
            dff = dlog * sig_neg
            dff_ref[...] = dff
            acc_ref[0:1, :] += jnp.sum(dff, axis=0, keepdims=True)

        return pl.pallas_call(
            body, name="fox_post", grid=(n_tok,),
            in_specs=[pl.BlockSpec((TM, LANES), lambda i: (rev(i), 0)),
                      pl.BlockSpec((TM, LANES), lambda i: (rev(i), misc_blk)), const((1, LANES))],
            out_specs=[pl.BlockSpec((TM, LANES), lambda i: (rev(i), 0)), const((8, LANES))],
            out_shape=[jax.ShapeDtypeStruct((S, LANES), F32), jax.ShapeDtypeStruct((8, LANES), F32)],
            scratch_shapes=[pltpu.VMEM((1, LANES), F32)],
            compiler_params=_params(("arbitrary",)))(dcum, z, bf_pad)

    def mla_post(dq, dk, dv, dff, z, gq, gkv, w_uq, w_uk, w_v, l, cos_t, sin_t):
        def body(dq_ref, dk_ref, dv_ref, dff_ref, ql_ref, kvl_ref, gq_ref, gkv_ref, wq_ref, wk_ref, wv_ref,
                 c_ref, s_ref, zq_ref, zkv_ref, zms_ref, dwq_ref, dwk_ref, dwv_ref, dgq_ref, dgkv_ref):
            @pl.when(pl.program_id(0) == 0)
            def _():
                for r in (dwq_ref, dwk_ref, dwv_ref, dgq_ref, dgkv_ref):
                    r[...] = jnp.zeros_like(r)

            cos1, sin1 = c_ref[...], s_ref[...]
            gqv, gkvv = gq_ref[...], gkv_ref[...]
            qn, qxh, qrstd = _rms(ql_ref[...], gqv)
            dq_pre = _rope_t(dq_ref[...] * MLA_SCALE, jnp.tile(cos1, (1, MH)), jnp.tile(sin1, (1, MH))).astype(BF16)
            dwq_ref[...] += _tn(qn.astype(BF16), dq_pre)
            dql, dgq = _rms_bwd(_nt(dq_pre, wq_ref[...]), gqv, qxh, qrstd)
            zq_ref[...] = dql.astype(BF16)
            dgq_ref[0:1, :] += dgq

            dkv = dk_ref[...]
            lane = _lane(dkv.shape) & (HEAD_PAD - 1)
            dkn = jnp.where(lane < HEAD_DIM, dkv, 0.0).astype(BF16)
            dkr = dkv[:, 0:HEAD_PAD]
            for hd in range(1, MH):
                dkr = dkr + dkv[:, hd * HEAD_PAD:(hd + 1) * HEAD_PAD]
            lane1 = _lane(dkr.shape)
            dkr = jnp.where((lane1 >= ROPE_LO) & (lane1 < ROPE_LO + ROPE_DIM), dkr, 0.0)
            dkr = _rope_t(dkr, cos1, sin1)
            zms_ref[...] = (dkr + dff_ref[...]).astype(BF16)

            kvn, kxh, krstd = _rms(kvl_ref[...], gkvv)
            kvb = kvn.astype(BF16)
            dvb = dv_ref[...].astype(BF16)
            dwk_ref[...] += _tn(kvb, dkn)
            dwv_ref[...] += _tn(kvb, dvb)
            dkvl, dgkv = _rms_bwd(_nt(dkn, wk_ref[...]) + _nt(dvb, wv_ref[...]), gkvv, kxh, krstd)
            zkv_ref[...] = dkvl.astype(BF16)
            dgkv_ref[0:1, :] += dgkv

        return pl.pallas_call(
            body, name="mla_post", grid=(n_tok,),
            in_specs=[tok(QW), tok(QW), tok(MW), tok(LANES), tok(QL, (cfg.o_ql - ZO) // QL), tok(KVL, (cfg.o_kv - ZO) // KVL),
                      const((1, QL)), const((1, KVL)), layer(l, (QL, QW)), layer(l, (KVL, QW)), layer(l, (KVL, MW)),
                      tok(LANES), tok(LANES)],
            out_specs=[tok(QL), tok(KVL), tok(LANES), const((QL, QW)), const((KVL, QW)), const((KVL, MW)),
                       const((8, QL)), const((8, KVL))],
            out_shape=[jax.ShapeDtypeStruct((S, QL), BF16), jax.ShapeDtypeStruct((S, KVL), BF16),
                       jax.ShapeDtypeStruct((S, LANES), BF16), jax.ShapeDtypeStruct((QL, QW), F32),
                       jax.ShapeDtypeStruct((KVL, QW), F32), jax.ShapeDtypeStruct((KVL, MW), F32),
                       jax.ShapeDtypeStruct((8, QL), F32), jax.ShapeDtypeStruct((8, KVL), F32)],
            compiler_params=_params(("arbitrary",)))(dq, dk, dv, dff, z, z, gq, gkv, w_uq, w_uk, w_v, cos_t, sin_t)

    def bwd_in(dz, w_in, l, x, dxn, g, scale):
        def body(dz_ref, w_ref, x_ref, dx_ref, g_ref, sc_ref, o_ref, acc_ref):
            @pl.when(pl.program_id(0) == 0)
            def _():
                acc_ref[...] = jnp.zeros_like(acc_ref)

            dh = _nt(dz_ref[...], w_ref[...])
            gv, mod = g_ref[...], 1.0 + sc_ref[...]
            _, xh, rstd = _rms(x_ref[...], gv)
            t = dh * xh
            acc_ref[0:1, :] += jnp.sum(dh, axis=0, keepdims=True)
            acc_ref[1:2, :] += jnp.sum(t * gv, axis=0, keepdims=True)
            acc_ref[2:3, :] += jnp.sum(t * mod, axis=0, keepdims=True)
            dx, _ = _rms_bwd(dh, gv * mod, xh, rstd)
            o_ref[...] = dx_ref[...] + dx

        return pl.pallas_call(
            body, name="bwd_in", grid=(n_tok,),
            in_specs=[tok(NZ), layer(l, (D, NZ)), tok(D), tok(D), const((1, D)), const((1, D))],
            out_specs=[tok(D), const((8, D))],
            out_shape=[jax.ShapeDtypeStruct((S, D), F32), jax.ShapeDtypeStruct((8, D), F32)],
            compiler_params=_params(("arbitrary",)))(dz, w_in, x, dxn, g, scale)


    def pair_rows(cum):
        n_pairs = FH // 2
        ct = jnp.pad(cum[:, :FH].T.reshape(n_pairs, 2, S), ((0, 0), (0, 6), (0, 0)))
        return ct.reshape(n_pairs, 8, S // TK, TK).transpose(0, 2, 1, 3)

    def bias_grad(dc, dr):
        n_pairs = FH // 2
        d = dr.transpose(0, 2, 1, 3).reshape(n_pairs, 8, S)[:, :2, :].reshape(FH, S).T
        d = d + dc[:, :, :2].transpose(1, 0, 2).reshape(S, FH)
        return jnp.pad(d, ((0, 0), (0, LANES - FH)))

    def layer_forward(x, wl, mod, later_shards=None):
        shift, scale, gate = mod
        h, z, qkv = ln_inproj(x, wl.norm_g, scale, shift, wl.w_in, wl.l)
        cum = fox_prep(z, wl.bf_pad)
        ct = pair_rows(cum)
        tiles = jnp.broadcast_to(cum[:, :FH].T.reshape(FH // 2, 2, S, 1), (FH // 2, 2, S, LANES))
        fox = attn_fwd(qkv, qkv, qkv, cfg.o_fq // LANES, cfg.o_fk // LANES, cfg.o_fv // LANES, tiles, True, False,
                       "fox_fwd_gather" if later_shards else "fox_fwd", GatherOverIci(later_shards) if later_shards else None)
        of, lse_f = fox[0], fox[1]
        qp, kp, vp = mla_prep(z, wl.gq, wl.gkv, wl.w_uq, wl.w_uk, wl.w_v, wl.l, cfg.cos_t, cfg.sin_t)
        mla = attn_fwd(qp, kp, vp, 0, 0, 0, None, False, True, "mla_fwd_gather" if later_shards else "mla_fwd",
                       GatherToSibling(fox[2]) if later_shards else None)
        om, lse_m = mla[0], mla[1]
        xn, u, y = gate_outproj(of, om, z, wl.w_out, wl.l, x, gate)
        saved = types.SimpleNamespace(x=x, h=h, z=z, qkv=qkv, ct=ct, of=of, lse_f=lse_f, qp=qp, kp=kp, vp=vp,
                                      om=om, lse_m=lse_m, u=u, y=y)
        return (xn, saved, mla[2]) if later_shards else (xn, saved)

    def layer_backward(dxn, sv, wl, mod, side=None):
        shift, scale, gate = mod
        do_f, do_m, dfg, dmg, dy, acc_o, st_f, st_m = bwd_out(dxn, gate, sv.y, wl.w_out, wl.l, sv.of, sv.om, sv.z,
                                                              sv.lse_f, sv.lse_m)
        dw_out = matmul_tn(sv.u, dy, "dw_out")
        dfq, dfk, dfv, dck, dcr, *rode = attn_bwd(sv.qkv, sv.qkv, sv.qkv, do_f, st_f, cfg.o_fq // LANES, cfg.o_fk // LANES,
                                                  cfg.o_fv // LANES, sv.ct, True, False,
                                                  "fox_bwd_exchange" if side else "fox_bwd", side)
        dff, acc_f = fox_post(bias_grad(dck, dcr), sv.z, wl.bf_pad)
        dqp, dkp, dvp = attn_bwd(sv.qp, sv.kp, sv.vp, do_m, st_m, 0, 0, 0, None, False, True, "mla_bwd")
        zq, zkv, zms, dw_uq, dw_uk, dw_v, dgq, dgkv = mla_post(
            dqp, dkp, dvp, dff, sv.z, wl.gq, wl.gkv, wl.w_uq, wl.w_uk, wl.w_v, wl.l, cfg.cos_t, cfg.sin_t)
        dz = jnp.concatenate([(dfq * FOX_SCALE).astype(BF16), dfk.astype(BF16), dfv.astype(BF16), dfg, dmg, zq, zkv, zms], axis=1)
        dx, acc_i = bwd_in(dz, wl.w_in, wl.l, sv.x, dxn, wl.norm_g, scale)
        dw_in = matmul_tn(sv.h, dz, "dw_in")
        grads = types.SimpleNamespace(
            w_in=dw_in, w_out=dw_out, w_uq=dw_uq, w_uk=dw_uk, w_v=dw_v, gq=dgq[0], gkv=dgkv[0],
            b_f=acc_f[0, :FH], norm_g=acc_i[2], dmod=jnp.concatenate([acc_i[0], acc_i[1], acc_o[0]]))
        return (dx, grads, rode[0]) if side else (dx, grads)

    return types.SimpleNamespace(layer_forward=layer_forward, layer_backward=layer_backward, final_loss=final_loss)


def _pack_rows(parts, dtype, row_multiple):
    flat = jnp.concatenate([p.reshape(-1).astype(dtype) for p in parts])
    per = SLAB_COLS * row_multiple
    total = -(-flat.shape[0] // per) * per
    return jnp.pad(flat, (0, total - flat.shape[0])).reshape(total // SLAB_COLS, SLAB_COLS)


def _unpack(flat, shapes):
    out, off = [], 0
    for shp in shapes:
        n = 1
        for d in shp:
            n *= d
        out.append(flat[off:off + n].reshape(shp))
        off += n
    return out


def kernel(x, c, positions, norm_g, w_ada, b_ada, w_in, b_f, q_norm_g, w_uq, kv_norm_g, w_ukv, w_out, final_g, loss_target, m_norm_g, m_w_ada, m_b_ada, m_w_in, m_b_f, m_q_norm_g, m_w_uq, m_kv_norm_g, m_w_ukv, m_w_out, m_final_g, v_norm_g, v_w_ada, v_b_ada, v_w_in, v_b_f, v_q_norm_g, v_w_uq, v_kv_norm_g, v_w_ukv, v_w_out, v_final_g):
    S, D = x.shape[1], x.shape[2]
    L = norm_g.shape[0]
    FH = b_f.shape[1]
    QL, KVL = q_norm_g.shape[1], kv_norm_g.shape[1]
    MH = w_ukv.shape[2] * N_CHIPS // (2 * HEAD_DIM)
    FW, MW = FH * HEAD_DIM, MH * HEAD_DIM
    NA = w_ada.shape[2]
    n_in = w_in.shape[2] * N_CHIPS
    cfg = types.SimpleNamespace(S=S, D=D, FW=FW, MW=MW, QL=QL, KVL=KVL, FH=FH, MH=MH)
    cfg.o_fq, cfg.o_fk, cfg.o_fv, cfg.o_fg, cfg.o_mg = 0, FW, 2 * FW, 3 * FW, 4 * FW
    cfg.o_ql = 4 * FW + MW
    cfg.o_kv = cfg.o_ql + QL
    cfg.o_ms = cfg.o_kv + KVL
    cfg.NZ = cfg.o_ms + LANES
    assert FW == MW and FH % 2 == 0 and MH % 2 == 0 and cfg.o_ql % QL == 0 and cfg.o_kv % KVL == 0 and KVL == LANES
    assert n_in == 4 * FW + FH + QL + KVL + ROPE_DIM + MW

    mx, my, mc = _my_pos()
    my_chip = 2 * mx + my
    my_dev = 2 * my_chip + mc

    inv_freq = 1.0 / (ROPE_THETA ** (jnp.arange(0, ROPE_DIM, 2, dtype=F32) / ROPE_DIM))
    ang = positions[0].astype(F32)[:, None] * inv_freq
    cos, sin = jnp.cos(ang), jnp.sin(ang)
    cfg.cos_t = jnp.concatenate([jnp.ones((S, ROPE_LO), F32), cos, cos, jnp.ones((S, HEAD_PAD - ROPE_LO - ROPE_DIM), F32)], axis=1)
    cfg.sin_t = jnp.concatenate([jnp.zeros((S, ROPE_LO), F32), -sin, sin, jnp.zeros((S, HEAD_PAD - ROPE_LO - ROPE_DIM), F32)], axis=1)

    assert L % 4 == 0
    def lane_pad(a):
        return jnp.pad(a, ((0, 0),) * (a.ndim - 1) + ((0, -a.shape[-1] % LANES),))

    shards = (w_in, w_uq, w_ukv, w_out)
    padded = [lane_pad(w.astype(BF16)) for w in shards]

    def kernel_layouts(gathered, own):
        def all_chips(g, w, axis):
            return jnp.concatenate([jnp.where(my_chip == t, w.astype(BF16), g[t][..., :w.shape[-1]])
                                    for t in range(N_CHIPS)], axis=axis)

        n = own[0].shape[0]
        w_in_f = all_chips(gathered[0], own[0], 2)
        w_uq_f = all_chips(gathered[1], own[1], 2)
        w_ukv_f = all_chips(gathered[2], own[2], 2)
        w_out_f = all_chips(gathered[3], own[3], 1)
        sizes = (FW, FW, FW, FH, FW, QL, KVL, ROPE_DIM, MW)
        offs = [0]
        for sz in sizes:
            offs.append(offs[-1] + sz)
        fq_w, fk_w, fv_w, ff_w, fg_w, ql_w, kvl_w, kr_w, mg_w = [w_in_f[:, :, offs[i]:offs[i + 1]] for i in range(len(sizes))]
        zeros = lambda width: jnp.zeros((n, D, width), BF16)
        w_in_p = jnp.concatenate([fq_w, fk_w, fv_w, fg_w, mg_w, ql_w, kvl_w, ff_w, zeros(ROPE_LO - FH), kr_w,
                                  zeros(HEAD_PAD - ROPE_LO - ROPE_DIM)], axis=2)
        w_uq_p = jnp.pad(w_uq_f.reshape(n, QL, MH, HEAD_DIM + ROPE_DIM),
                         ((0, 0), (0, 0), (0, 0), (0, HEAD_PAD - HEAD_DIM - ROPE_DIM))).reshape(n, QL, MH * HEAD_PAD)
        w_ukv4 = w_ukv_f.reshape(n, KVL, MH, 2 * HEAD_DIM)
        w_uk_p = jnp.pad(w_ukv4[..., :HEAD_DIM], ((0, 0), (0, 0), (0, 0), (0, HEAD_PAD - HEAD_DIM))).reshape(n, KVL, MH * HEAD_PAD)
        return types.SimpleNamespace(w_in=w_in_p, w_uq=w_uq_p, w_uk=w_uk_p, w_v=w_ukv4[..., HEAD_DIM:].reshape(n, KVL, MW),
                                     w_out=w_out_f)

    first = kernel_layouts(weights_gather([p[:1] for p in padded], "weights_first"), [w[:1] for w in shards])

    c_all = allgather8(c.reshape(8, D // 8), "gather_c").reshape(N_DEV, D)
    c_pad = jnp.pad(c_all, ((0, 16 - N_DEV), (0, 0)))
    mod_part = ada_forward(c_pad, w_ada)[:, :N_DEV, :]
    mod_all = allgather8(mod_part.reshape(-1, LANES), "gather_mod").reshape(N_CHIPS, 2, L, N_DEV, NA)[:, 0]
    mod_full = mod_all.transpose(1, 2, 0, 3).reshape(L, N_DEV, N_CHIPS * NA) + b_ada[:, None, :]
    mod_mine = lax.dynamic_index_in_dim(mod_full, my_dev, axis=1, keepdims=True)

    step = make_step(cfg)
    bf_pad = jnp.pad(b_f, ((0, 0), (0, LANES - FH)))
    mods = [(mod_mine[l, :, :D], mod_mine[l, :, D:2 * D], mod_mine[l, :, 2 * D:]) for l in range(L)]

    def layer_params(l, ws_, at):
        return types.SimpleNamespace(l=at, norm_g=norm_g[l][None], bf_pad=bf_pad[l][None], gq=q_norm_g[l][None],
                                     gkv=kv_norm_g[l][None], w_in=ws_.w_in, w_uq=ws_.w_uq, w_uk=ws_.w_uk, w_v=ws_.w_v,
                                     w_out=ws_.w_out)

    layers = [layer_params(0, first, 0)]
    xl, sv, later = step.layer_forward(x[0], layers[0], mods[0], [p[1:] for p in padded])
    saved = [sv]
    rest = kernel_layouts(later, [w[1:] for w in shards])
    for l in range(1, L):
        layers.append(layer_params(l, rest, l - 1))
        xl, sv = step.layer_forward(xl, layers[l], mods[l])
        saved.append(sv)
    dx, acc_fin, loss_part = step.final_loss(xl, final_g[None], loss_target[0])
    loss = lax.psum(loss_part[0, 0], ("x", "y", "c"))
    ms = cfg.o_ms
    runs = [(0, 3 * FW, 0), (3 * FW, FH, ms), (3 * FW + FH, FW, cfg.o_fg), (4 * FW + FH, QL + KVL, cfg.o_ql),
            (4 * FW + FH + QL + KVL, ROPE_DIM, ms + ROPE_LO), (4 * FW + FH + QL + KVL + ROPE_DIM, MW, cfg.o_mg)]
    big_names = ["w_in", "w_uq", "w_ukv", "w_out"]

    def shard_columns(dw, t):
        a, b = t * w_in.shape[2], (t + 1) * w_in.shape[2]
        return jnp.concatenate([dw[:, p0 + max(a, r0) - r0:p0 + min(b, r0 + sz) - r0]
                                for r0, sz, p0 in runs if max(a, r0) < min(b, r0 + sz)], axis=1)

    def chip_parts(group, tag):
        n = len(group)
        stk = lambda name: jnp.stack([getattr(g, name) for g in group])
        g_in4 = jnp.stack([jnp.stack([shard_columns(g.w_in, t) for g in group]) for t in range(N_CHIPS)])
        dw_uq_f = stk("w_uq").reshape(n, QL, MH, HEAD_PAD)[..., :HEAD_DIM + ROPE_DIM].reshape(n, QL, -1)
        dw_ukv_f = jnp.concatenate([stk("w_uk").reshape(n, KVL, MH, HEAD_PAD)[..., :HEAD_DIM],
                                    stk("w_v").reshape(n, KVL, MH, HEAD_DIM)], axis=3).reshape(n, KVL, -1)
        gs = [g_in4, dw_uq_f.reshape(n, QL, N_CHIPS, -1).transpose(2, 0, 1, 3),
              dw_ukv_f.reshape(n, KVL, N_CHIPS, -1).transpose(2, 0, 1, 3), stk("w_out").reshape(n, N_CHIPS, -1, D).transpose(1, 0, 2, 3)]
        gs = [lane_pad(g.astype(BF16)) for g in gs]
        theirs = halves_to_sibling(gs, "grads_sibling_" + tag)
        out = []
        for g, o, nm in zip(gs, theirs, big_names):
            keep = lax.dynamic_slice_in_dim(g, mc * (n // 2), n // 2, axis=1)
            merged = (N_CHIPS * (n // 2),) + g.shape[2:]
            out.append(add_cast(keep.reshape(merged), o.reshape(merged), BF16, "grads_chip_sum_%s_%s" % (nm, tag)).reshape(o.shape))
        return out

    def reduced(chip_part, parts, tag):
        where = jnp.stack([my_chip, mc]).astype(jnp.int32)
        red = [sum_chips(p, o, where, "grads_sum_%s_%s" % (nm, tag)) for p, o, nm in zip(parts, chip_part, big_names)]
        return [g[..., :w.shape[-1]] for g, w in zip(halves_gather(red, "grads_back_" + tag), shards)]

    gl = [None] * L
    half_l = L // 2
    for l in range(L - 1, half_l - 1, -1):
        dx, gl[l] = step.layer_backward(dx, saved[l], layers[l], mods[l])
    part_hi = chip_parts(gl[half_l:], "hi")
    dx, gl[half_l - 1], parts_hi = step.layer_backward(dx, saved[half_l - 1], layers[half_l - 1], mods[half_l - 1],
                                                       ExchangeOverIci(part_hi))
    for l in range(half_l - 2, -1, -1):
        dx, gl[l] = step.layer_backward(dx, saved[l], layers[l], mods[l])
    part_lo = chip_parts(gl[:half_l], "lo")
    parts_lo = run_side(ExchangeOverIci(part_lo), "grads_chips_lo")
    g_w_in, g_w_uq, g_w_ukv, g_w_out = [jnp.concatenate([lo, hi]) for lo, hi in
                                        zip(reduced(part_lo, parts_lo, "lo"), reduced(part_hi, parts_hi, "hi"))]
    grad_x = dx[None]

    stack = lambda name: jnp.stack([getattr(g, name) for g in gl])
    small_parts = [stack("norm_g"), stack("dmod"), stack("b_f"), stack("gq"), stack("gkv"), acc_fin[0]]
    small_shapes = [p.shape for p in small_parts]
    small = _pack_rows(small_parts, F32, 8).reshape(-1, LANES)
    small_all = allgather8(small, "gather_small").reshape(N_DEV, -1, LANES)
    small_sum = sum_leading(small_all, "sum_small")
    g_norm_g, g_b_ada, g_b_f, g_q_norm_g, g_kv_norm_g, g_final_g = _unpack(small_sum.reshape(-1), small_shapes)

    n_ng = L * D
    dmod_all = small_all.reshape(N_DEV, -1)[:, n_ng:n_ng + L * 3 * D].reshape(N_DEV, L, 3 * D)
    dmod_cols = lax.dynamic_slice_in_dim(dmod_all, my_chip * NA, NA, axis=2).transpose(1, 0, 2)
    g_w_ada = ada_backward(c_pad, jnp.pad(dmod_cols, ((0, 0), (0, 16 - N_DEV), (0, 0))))


    names = ["norm_g", "w_ada", "b_ada", "w_in", "b_f", "q_norm_g", "w_uq", "kv_norm_g", "w_ukv", "w_out", "final_g"]
    ws = dict(norm_g=norm_g, w_ada=w_ada, b_ada=b_ada, w_in=w_in, b_f=b_f, q_norm_g=q_norm_g, w_uq=w_uq,
              kv_norm_g=kv_norm_g, w_ukv=w_ukv, w_out=w_out, final_g=final_g)
    msd = dict(norm_g=m_norm_g, w_ada=m_w_ada, b_ada=m_b_ada, w_in=m_w_in, b_f=m_b_f, q_norm_g=m_q_norm_g, w_uq=m_w_uq,
               kv_norm_g=m_kv_norm_g, w_ukv=m_w_ukv, w_out=m_w_out, final_g=m_final_g)
    vsd = dict(norm_g=v_norm_g, w_ada=v_w_ada, b_ada=v_b_ada, w_in=v_w_in, b_f=v_b_f, q_norm_g=v_q_norm_g, w_uq=v_w_uq,
               kv_norm_g=v_kv_norm_g, w_ukv=v_w_ukv, w_out=v_w_out, final_g=v_final_g)
    gsd = dict(norm_g=g_norm_g, w_ada=g_w_ada, b_ada=g_b_ada, w_in=g_w_in, b_f=g_b_f, q_norm_g=g_q_norm_g, w_uq=g_w_uq,
               kv_norm_g=g_kv_norm_g, w_ukv=g_w_ukv, w_out=g_w_out, final_g=g_final_g)
    small_names = ["norm_g", "b_ada", "b_f", "q_norm_g", "kv_norm_g", "final_g"]
    sm_shapes = [ws[n].shape for n in small_names]
    pk = lambda d: _pack_rows([d[n] for n in small_names], F32, 8).reshape(1, -1, LANES)
    sm_out = adamw(pk(ws), pk(gsd), pk(msd), pk(vsd), "adamw_small")
    sm_d, sm_m, sm_v = [dict(zip(small_names, _unpack(o.reshape(-1), sm_shapes))) for o in sm_out]
    delta, new_m, new_v = dict(sm_d), dict(sm_m), dict(sm_v)
    for n in ["w_ada", "w_in", "w_uq", "w_ukv", "w_out"]:
        delta[n], new_m[n], new_v[n] = adamw(ws[n], gsd[n], msd[n], vsd[n], "adamw_" + n)

    return (loss, grad_x, *[gsd[n] for n in names], *[delta[n] for n in names],
            *[new_m[n] for n in names], *[new_v[n] for n in names])
```

```python
import types

import jax
import jax.numpy as jnp
from jax import lax
from jax.experimental import pallas as pl
from jax.experimental.pallas import tpu as pltpu

F32 = jnp.float32
BF16 = jnp.bfloat16
MESH = pl.DeviceIdType.MESH

N_CHIPS = 4
N_DEV = 8
HEAD_DIM = 64
ROPE_DIM = 32
ROPE_THETA = 10000.0
HEAD_PAD = 128
ROPE_LO = 64
ROPE_HALF = 16
CHUNK_SHIFT = 6
LANES = 128
EPS = 1e-6
NEG = -1e30
ADAM_LR = 0.001
ADAM_B1 = 0.9
ADAM_B2 = 0.999
ADAM_EPS = 1e-08
ADAM_WD = 0.01
ADAM_STEP = 10
VMEM_LIMIT = 48 * 1024 * 1024
SLAB_COLS = 1024


def _params(sem=None, vmem=VMEM_LIMIT):
    return pltpu.CompilerParams(dimension_semantics=sem, vmem_limit_bytes=vmem)


def _nn(a, b):
    return jnp.dot(a, b, preferred_element_type=F32)


def _nt(a, b):
    return lax.dot_general(a, b, (((1,), (1,)), ((), ())), preferred_element_type=F32)


def _tn(a, b):
    return lax.dot_general(a, b, (((0,), (0,)), ((), ())), preferred_element_type=F32)


def _sigmoid(x):
    return 1.0 / (1.0 + jnp.exp(-x))


def _lane(shape):
    return lax.broadcasted_iota(jnp.int32, shape, len(shape) - 1)


def _pick(n, cands):
    for c in cands:
        if n % c == 0:
            return c
    return n


def _my_pos():
    return lax.axis_index("x"), lax.axis_index("y"), lax.axis_index("c")


def allgather8(xs, name):
    m_per, n = xs.shape

    def body(x_ref, out_ref, send_sems, recv_sems, local_sem):
        x, y, c = _my_pos()
        me, sibling = (x, y, c), (x, y, 1 - c)
        chips = [(1 - x, y), (x, 1 - y), (1 - x, 1 - y)]

        def rows(px, py, pc):
            return out_ref.at[pl.ds((4 * px + 2 * py + pc) * m_per, m_per), :]

        def copy(k, block, to, src=None):
            return pltpu.make_async_remote_copy(
                src_ref=rows(*block) if src is None else src, dst_ref=rows(*block),
                send_sem=send_sems.at[k], recv_sem=recv_sems.at[k], device_id=to, device_id_type=MESH)

        mine = pltpu.make_async_copy(x_ref, rows(*me), local_sem)
        mine.start()
        first = [copy(0, me, sibling, src=x_ref)]
        first += [copy(1 + j, me, (*chip, c), src=x_ref) for j, chip in enumerate(chips)]
        for cp in first:
            cp.start()
        passed = [copy(4 + j, (*chip, c), sibling) for j, chip in enumerate(chips)]
        for j, chip in enumerate(chips):
            copy(1 + j, (*chip, c), me).wait_recv()
            passed[j].start()
        copy(0, sibling, me).wait_recv()
        for j, chip in enumerate(chips):
            copy(4 + j, (*chip, 1 - c), me).wait_recv()
        for cp in first + passed:
            cp.wait_send()
        mine.wait()

    return pl.pallas_call(
        body, name=name,
        out_shape=jax.ShapeDtypeStruct((N_DEV * m_per, n), xs.dtype),
        in_specs=[pl.BlockSpec(memory_space=pltpu.VMEM)],
        out_specs=pl.BlockSpec(memory_space=pltpu.VMEM),
        scratch_shapes=[pltpu.SemaphoreType.DMA((7,)), pltpu.SemaphoreType.DMA((7,)), pltpu.SemaphoreType.DMA],
    )(xs)


def _remote(src, dst, send_sems, recv_sems, k, to):
    return pltpu.make_async_remote_copy(src_ref=src, dst_ref=dst, send_sem=send_sems.at[k], recv_sem=recv_sems.at[k],
                                        device_id=to, device_id_type=MESH)


def _hbm_call(body, name, ins, out_shapes, n_sems, aliases=None):
    hbm = pl.BlockSpec(memory_space=pl.ANY)
    scratch = [pltpu.SemaphoreType.DMA((n_sems,)), pltpu.SemaphoreType.DMA((n_sems,))]
    return pl.pallas_call(body, name=name, out_shape=out_shapes, in_specs=[hbm] * len(ins),
                          out_specs=[hbm] * len(out_shapes), scratch_shapes=scratch,
                          input_output_aliases=aliases or {})(*ins)


def _layers_half(ref, h, axis=0):
    size = ref.shape[axis] // 2
    idx = (slice(None),) * axis + (pl.ds(h * size, size),)
    return ref.at[idx]


class GatherOverIci:
    def __init__(self, ws):
        self.ins = list(ws)
        self.outs = [jax.ShapeDtypeStruct((N_CHIPS,) + w.shape, w.dtype) for w in ws]
        self.aliases = {}
        self.n_sems = 3 * len(ws)

    def _copies(self, x_refs, o_refs, send_sems, recv_sems):
        x, y, c = _my_pos()
        s = 2 * x + y
        out = []
        for a in range(len(x_refs)):
            for k, (tx, ty) in enumerate([(1 - x, y), (x, 1 - y), (1 - x, 1 - y)]):
                mine = _remote(_layers_half(x_refs[a], c, 1), _layers_half(o_refs[a].at[s], c, 1), send_sems, recv_sems,
                               3 * a + k, (tx, ty, c))
                got = _layers_half(o_refs[a].at[2 * tx + ty], c, 1)
                out.append((mine, _remote(got, got, send_sems, recv_sems, 3 * a + k, (tx, ty, c))))
        return out

    def start(self, x_refs, o_refs, send_sems, recv_sems):
        for mine, _ in self._copies(x_refs, o_refs, send_sems, recv_sems):
            mine.start()

    def wait(self, x_refs, o_refs, send_sems, recv_sems):
        for mine, theirs in self._copies(x_refs, o_refs, send_sems, recv_sems):
            theirs.wait_recv()
            mine.wait_send()


class GatherToSibling:
    def __init__(self, gathered):
        self.ins = list(gathered)
        self.outs = [jax.ShapeDtypeStruct(g.shape, g.dtype) for g in gathered]
        self.aliases = {a: a for a in range(len(gathered))}
        self.n_sems = 3 * len(gathered)

    def _copies(self, o_refs, send_sems, recv_sems):
        x, y, c = _my_pos()
        out = []
        for a in range(len(o_refs)):
            for k, (tx, ty) in enumerate([(1 - x, y), (x, 1 - y), (1 - x, 1 - y)]):
                got = _layers_half(o_refs[a].at[2 * tx + ty], c, 1)
                theirs = _layers_half(o_refs[a].at[2 * tx + ty], 1 - c, 1)
                out.append((_remote(got, got, send_sems, recv_sems, 3 * a + k, (x, y, 1 - c)),
                            _remote(theirs, theirs, send_sems, recv_sems, 3 * a + k, (x, y, 1 - c))))
        return out

    def start(self, x_refs, o_refs, send_sems, recv_sems):
        for mine, _ in self._copies(o_refs, send_sems, recv_sems):
            mine.start()

    def wait(self, x_refs, o_refs, send_sems, recv_sems):
        for mine, theirs in self._copies(o_refs, send_sems, recv_sems):
            theirs.wait_recv()
            mine.wait_send()


def run_side(side, name):
    n_in, n_out = len(side.ins), len(side.outs)

    def body(*refs):
        parts = refs[:n_in], refs[n_in:n_in + n_out], refs[n_in + n_out], refs[n_in + n_out + 1]
        side.start(*parts)
        side.wait(*parts)

    return _hbm_call(body, name, side.ins, side.outs, side.n_sems, aliases=side.aliases)


def weights_gather(ws, name):
    return run_side(GatherToSibling(run_side(GatherOverIci(ws), name + "_ici")), name + "_sibling")


def halves_to_sibling(gs, name):
    n = len(gs)

    def body(*refs):
        x_refs, o_refs, send_sems, recv_sems = refs[:n], refs[n:2 * n], refs[2 * n], refs[2 * n + 1]
        x, y, c = _my_pos()
        cps = [_remote(_layers_half(x_refs[a], 1 - c, axis=1), o_refs[a], send_sems, recv_sems, a, (x, y, 1 - c))
               for a in range(n)]
        for cp in cps:
            cp.start()
        for cp in cps:
            cp.wait()

    outs = [jax.ShapeDtypeStruct((g.shape[0], g.shape[1] // 2) + g.shape[2:], g.dtype) for g in gs]
    return _hbm_call(body, name, gs, outs, n)


class ExchangeOverIci:
    def __init__(self, xs):
        self.ins = list(xs)
        self.outs = [jax.ShapeDtypeStruct(v.shape, v.dtype) for v in xs]
        self.aliases = {}
        self.n_sems = 3 * len(xs)

    def _copies(self, x_refs, o_refs, send_sems, recv_sems):
        x, y, c = _my_pos()
        s = 2 * x + y
        out = []
        for a in range(len(x_refs)):
            for k, (tx, ty) in enumerate([(1 - x, y), (x, 1 - y), (1 - x, 1 - y)]):
                got = o_refs[a].at[2 * tx + ty]
                out.append((_remote(x_refs[a].at[2 * tx + ty], o_refs[a].at[s], send_sems, recv_sems, 3 * a + k, (tx, ty, c)),
                            _remote(got, got, send_sems, recv_sems, 3 * a + k, (tx, ty, c))))
        return out

    def start(self, x_refs, o_refs, send_sems, recv_sems):
        for mine, _ in self._copies(x_refs, o_refs, send_sems, recv_sems):
            mine.start()

    def wait(self, x_refs, o_refs, send_sems, recv_sems):
        for mine, theirs in self._copies(x_refs, o_refs, send_sems, recv_sems):
            theirs.wait_recv()
            mine.wait_send()


def halves_gather(xs, name):
    n = len(xs)

    def body(*refs):
        x_refs, o_refs, send_sems, recv_sems = refs[:n], refs[n:2 * n], refs[2 * n], refs[2 * n + 1]
        x, y, c = _my_pos()
        sends = []
        for a in range(n):
            cp = _remote(_layers_half(x_refs[a], c), _layers_half(o_refs[a], c), send_sems, recv_sems, a, (x, y, 1 - c))
            cp.start()
            sends.append(cp)
        for a in range(n):
            theirs = _layers_half(o_refs[a], 1 - c)
            _remote(theirs, theirs, send_sems, recv_sems, a, (x, y, 1 - c)).wait_recv()
        for cp in sends:
            cp.wait_send()

    outs = [jax.ShapeDtypeStruct(v.shape, v.dtype) for v in xs]
    return _hbm_call(body, name, xs, outs, n, aliases={a: a for a in range(n)})


def sum_leading(xs, name):
    n, r, c = xs.shape

    def body(x_ref, o_ref):
        acc = x_ref[0]
        for i in range(1, n):
            acc = acc + x_ref[i]
        o_ref[...] = acc

    return pl.pallas_call(body, name=name, out_shape=jax.ShapeDtypeStruct((r, c), xs.dtype))(xs)


def add_cast(a, b, out_dtype, name):
    n, r, c = a.shape
    tr = _pick(r, (512, 256, 128, 64, 32, 16))

    def body(a_ref, b_ref, o_ref):
        o_ref[...] = (a_ref[...].astype(F32) + b_ref[...].astype(F32)).astype(out_dtype)

    spec = pl.BlockSpec((1, tr, c), lambda i, j: (i, j, 0))
    return pl.pallas_call(body, name=name, grid=(n, r // tr), in_specs=[spec, spec], out_specs=spec,
                          out_shape=jax.ShapeDtypeStruct((n, r, c), out_dtype),
                          compiler_params=_params(("parallel", "parallel")))(a, b)


def sum_chips(parts, own, where, name):
    n, nl, r, c = parts.shape
    tr = _pick(r, (512, 256, 128, 64, 32, 16))

    def body(w_ref, p_ref, o_ref, out_ref):
        s = w_ref[0]
        acc = jnp.zeros(out_ref.shape, F32)
        for t in range(n):
            acc = acc + jnp.where(s == t, o_ref[0], p_ref[t]).astype(F32)
        out_ref[...] = acc

    grid_spec = pltpu.PrefetchScalarGridSpec(
        num_scalar_prefetch=1, grid=(nl, r // tr),
        in_specs=[pl.BlockSpec((n, 1, tr, c), lambda i, j, w: (0, i, j, 0)),
                  pl.BlockSpec((1, 1, tr, c), lambda i, j, w: (w[0], i, j, 0))],
        out_specs=pl.BlockSpec((1, tr, c), lambda i, j, w: (w[1] * nl + i, j, 0)))
    return pl.pallas_call(body, name=name, grid_spec=grid_spec, out_shape=jax.ShapeDtypeStruct((2 * nl, r, c), F32),
                          compiler_params=_params(("parallel", "parallel")))(where, parts, own)


def ada_forward(c_all, w_ada):
    nl, d, n = w_ada.shape
    nb = c_all.shape[0]

    def body(c_ref, w_ref, o_ref):
        cv = c_ref[...]
        ca = (cv * _sigmoid(cv)).astype(BF16)
        o_ref[0] = _nn(ca, w_ref[0].astype(BF16))

    return pl.pallas_call(
        body, name="ada_forward", grid=(nl,),
        in_specs=[pl.BlockSpec((nb, d), lambda l: (0, 0)), pl.BlockSpec((1, d, n), lambda l: (l, 0, 0))],
        out_specs=pl.BlockSpec((1, nb, n), lambda l: (l, 0, 0)),
        out_shape=jax.ShapeDtypeStruct((nl, nb, n), F32), compiler_params=_params(("parallel",)))(c_all, w_ada)


def ada_backward(c_all, dmod):
    nl, nb, n = dmod.shape
    d = c_all.shape[1]

    def body(c_ref, g_ref, o_ref):
        cv = c_ref[...]
        ca = (cv * _sigmoid(cv)).astype(BF16)
        o_ref[0] = _tn(ca, g_ref[0].astype(BF16))

    return pl.pallas_call(
        body, name="ada_backward", grid=(nl,),
        in_specs=[pl.BlockSpec((nb, d), lambda l: (0, 0)), pl.BlockSpec((1, nb, n), lambda l: (l, 0, 0))],
        out_specs=pl.BlockSpec((1, d, n), lambda l: (l, 0, 0)),
        out_shape=jax.ShapeDtypeStruct((nl, d, n), F32), compiler_params=_params(("parallel",)))(c_all, dmod)


def matmul_tn(a, b, name):
    k, m = a.shape
    n = b.shape[1]
    tm, tk = _pick(m, (512, 256, 128)), _pick(k, (512, 256, 128))
    tn = n if n * (tm * 8 + tk * 4) <= VMEM_LIMIT // 2 else _pick(n, (512, 256, 128))

    def body(a_ref, b_ref, o_ref, acc_ref):
        @pl.when(pl.program_id(2) == 0)
        def _():
            acc_ref[...] = jnp.zeros_like(acc_ref)

        acc_ref[...] += _tn(a_ref[...], b_ref[...])

        @pl.when(pl.program_id(2) == k // tk - 1)
        def _():
            o_ref[...] = acc_ref[...].astype(BF16)

    return pl.pallas_call(
        body, name=name, grid=(m // tm, n // tn, k // tk),
        in_specs=[pl.BlockSpec((tk, tm), lambda i, j, kk: (kk, i)), pl.BlockSpec((tk, tn), lambda i, j, kk: (kk, j))],
        out_specs=pl.BlockSpec((tm, tn), lambda i, j, kk: (i, j)),
        out_shape=jax.ShapeDtypeStruct((m, n), BF16), scratch_shapes=[pltpu.VMEM((tm, tn), F32)],
        compiler_params=_params(("parallel", "parallel", "arbitrary")))(a, b)


def adamw(w, g, m, v, name):
    nl, r, c = w.shape
    tr = _pick(r, (512, 256, 128, 64, 32, 16, 8))

    def body(w_ref, g_ref, m_ref, v_ref, d_ref, mo_ref, vo_ref):
        gv = g_ref[...]
        mn = ADAM_B1 * m_ref[...] + (1.0 - ADAM_B1) * gv
        vn = ADAM_B2 * v_ref[...] + (1.0 - ADAM_B2) * (gv * gv)
        m_hat = mn / (1.0 - ADAM_B1 ** ADAM_STEP)
        v_hat = vn / (1.0 - ADAM_B2 ** ADAM_STEP)
        d_ref[...] = -ADAM_LR * (m_hat / (jnp.sqrt(v_hat) + ADAM_EPS) + ADAM_WD * w_ref[...])
        mo_ref[...] = mn
        vo_ref[...] = vn

    spec = pl.BlockSpec((1, tr, c), lambda l, i: (l, i, 0))
    out = jax.ShapeDtypeStruct((nl, r, c), F32)
    return pl.pallas_call(body, name=name, grid=(nl, r // tr), in_specs=[spec] * 4, out_specs=[spec] * 3,
                          out_shape=[out] * 3, compiler_params=_params(("parallel", "parallel")))(w, g, m, v)


def _rope(t, cos_t, sin_t):
    w = t.shape[1]
    lane = _lane(t.shape) & (HEAD_PAD - 1)
    first_half = (lane >= ROPE_LO) & (lane < ROPE_LO + ROPE_HALF)
    partner = jnp.where(first_half, pltpu.roll(t, w - ROPE_HALF, 1), pltpu.roll(t, ROPE_HALF, 1))
    return t * cos_t + partner * sin_t


def _rope_t(dt, cos_t, sin_t):
    w = dt.shape[1]
    lane = _lane(dt.shape) & (HEAD_PAD - 1)
    first_half = (lane >= ROPE_LO) & (lane < ROPE_LO + ROPE_HALF)
    ds = dt * sin_t
    partner = jnp.where(first_half, pltpu.roll(ds, w - ROPE_HALF, 1), pltpu.roll(ds, ROPE_HALF, 1))
    return dt * cos_t + partner


def _rms(xv, g):
    rstd = lax.rsqrt(jnp.mean(xv * xv, axis=-1, keepdims=True) + EPS)
    xh = xv * rstd
    return xh * g, xh, rstd


def _rms_bwd(dy, g, xh, rstd):
    dxh = dy * g
    dx = rstd * (dxh - xh * jnp.mean(dxh * xh, axis=-1, keepdims=True))
    return dx, jnp.sum(dy * xh, axis=0, keepdims=True)


def make_step(cfg):
    S, D, NZ = cfg.S, cfg.D, cfg.NZ
    FW, MW, QL, KVL, FH, MH = cfg.FW, cfg.MW, cfg.QL, cfg.KVL, cfg.FH, cfg.MH
    QW = MH * HEAD_PAD
    TM = _pick(S, (512, 256, 128))
    TQ = TK = _pick(S, (256, 128))
    n_tok = S // TM
    ZO = 3 * FW
    NZR = NZ - ZO
    misc_blk = (cfg.o_ms - ZO) // LANES
    FWD_UNROLL = 2
    FOX_SCALE = HEAD_DIM ** -0.5
    MLA_SCALE = (HEAD_DIM + ROPE_DIM) ** -0.5

    def tok(width, col=0):
        return pl.BlockSpec((TM, width), lambda i: (i, col))

    def const(shape):
        return pl.BlockSpec(shape, lambda i: tuple(0 for _ in shape))

    def layer(l, shape):
        return pl.BlockSpec((None,) + shape, lambda i: (l,) + tuple(0 for _ in shape))


    def ln_inproj(x, g, scale, shift, w_in, l):
        def body(x_ref, g_ref, sc_ref, sh_ref, w_ref, h_ref, z_ref, qkv_ref):
            y, _, _ = _rms(x_ref[...], g_ref[...])
            hb = (y * (1.0 + sc_ref[...]) + sh_ref[...]).astype(BF16)
            h_ref[...] = hb
            z = _nn(hb, w_ref[...])
            z_ref[...] = z[:, ZO:]
            qkv_ref[:, :FW] = (z[:, :FW] * FOX_SCALE).astype(BF16)
            qkv_ref[:, FW:] = z[:, FW:ZO].astype(BF16)

        return pl.pallas_call(
            body, name="ln_inproj", grid=(n_tok,),
            in_specs=[tok(D), const((1, D)), const((1, D)), const((1, D)), layer(l, (D, NZ))],
            out_specs=[tok(D), tok(NZR), tok(ZO)],
            out_shape=[jax.ShapeDtypeStruct((S, D), BF16), jax.ShapeDtypeStruct((S, NZR), F32),
                       jax.ShapeDtypeStruct((S, ZO), BF16)],
            compiler_params=_params(("parallel",)))(x, g, scale, shift, w_in)

    def _log_f_terms(misc, bf):
        lane = _lane(misc.shape)
        a = misc + bf
        e = jnp.exp(-jnp.abs(a))
        logf = jnp.minimum(a, 0.0) - jnp.log(1.0 + e)
        sig_neg = jnp.where(a >= 0, e, 1.0) / (1.0 + e)
        valid = lane < FH
        return jnp.where(valid, logf, 0.0), jnp.where(valid, sig_neg, 0.0)

    def fox_prep(z, bf_pad):
        def body(z_ref, b_ref, o_ref, t_ref, carry):
            @pl.when(pl.program_id(0) == 0)
            def _():
                carry[...] = jnp.zeros_like(carry)

            logf, _ = _log_f_terms(z_ref[...], b_ref[...])
            row = lax.broadcasted_iota(jnp.int32, (TM, TM), 0)
            col = lax.broadcasted_iota(jnp.int32, (TM, TM), 1)
            tri = (col <= row).astype(F32)
            cum = jnp.dot(tri, logf, precision=lax.Precision.HIGHEST, preferred_element_type=F32) + carry[...]
            o_ref[...] = cum
            carry[...] = cum[TM - 1:TM, :]
            for hd in range(FH):
                t_ref[hd // 2, hd % 2] = jnp.broadcast_to(cum[:, hd:hd + 1], (TM, LANES))

        return pl.pallas_call(
            body, name="fox_prep", grid=(n_tok,),
            in_specs=[tok(LANES, misc_blk), const((1, LANES))],
            out_specs=[tok(LANES), pl.BlockSpec((FH // 2, 2, TM, LANES), lambda i: (0, 0, i, 0))],
            out_shape=[jax.ShapeDtypeStruct((S, LANES), F32), jax.ShapeDtypeStruct((FH // 2, 2, S, LANES), F32)],
            scratch_shapes=[pltpu.VMEM((1, LANES), F32)],
            compiler_params=_params(("arbitrary",)))(z, bf_pad)

    def mla_prep(z, gq, gkv, w_uq, w_uk, w_v, l, cos_t, sin_t):
        def body(ql_ref, kvl_ref, ms_ref, gq_ref, gkv_ref, wq_ref, wk_ref, wv_ref, c_ref, s_ref, q_ref, k_ref, v_ref):
            cos1, sin1 = c_ref[...], s_ref[...]
            qn, _, _ = _rms(ql_ref[...], gq_ref[...])
            q = _nn(qn.astype(BF16), wq_ref[...])
            q_ref[...] = (_rope(q, jnp.tile(cos1, (1, MH)), jnp.tile(sin1, (1, MH))) * MLA_SCALE).astype(BF16)
            kvn, _, _ = _rms(kvl_ref[...], gkv_ref[...])
            kvb = kvn.astype(BF16)
            lane = _lane((TM, LANES))
            kr = jnp.where((lane >= ROPE_LO) & (lane < ROPE_LO + ROPE_DIM), ms_ref[...], 0.0)
            kr = _rope(kr, cos1, sin1)
            k_ref[...] = (_nn(kvb, wk_ref[...]) + jnp.tile(kr, (1, MH))).astype(BF16)
            v_ref[...] = _nn(kvb, wv_ref[...]).astype(BF16)

        return pl.pallas_call(
            body, name="mla_prep", grid=(n_tok,),
            in_specs=[tok(QL, (cfg.o_ql - ZO) // QL), tok(KVL, (cfg.o_kv - ZO) // KVL), tok(LANES, misc_blk),
                      const((1, QL)), const((1, KVL)), layer(l, (QL, QW)), layer(l, (KVL, QW)), layer(l, (KVL, MW)),
                      tok(LANES), tok(LANES)],
            out_specs=[tok(QW), tok(QW), tok(MW)],
            out_shape=[jax.ShapeDtypeStruct((S, QW), BF16), jax.ShapeDtypeStruct((S, QW), BF16),
                       jax.ShapeDtypeStruct((S, MW), BF16)],
            compiler_params=_params(("parallel",)))(z, z, z, gq, gkv, w_uq, w_uk, w_v, cos_t, sin_t)

    def _allowed(q0, k0, chunked):
        qi = q0 + lax.broadcasted_iota(jnp.int32, (TQ, TK), 0)
        ki = k0 + lax.broadcasted_iota(jnp.int32, (TQ, TK), 1)
        if chunked:
            return (ki >> CHUNK_SHIFT) <= (qi >> CHUNK_SHIFT)
        return ki <= qi

    def _heads(val, packed):
        if packed:
            lane = _lane(val.shape)
            zero = jnp.zeros_like(val)
            return [jnp.where(lane < HEAD_DIM, val, zero), jnp.where(lane >= HEAD_DIM, val, zero)]
        return [val[:, :HEAD_PAD], val[:, HEAD_PAD:]]

    def _merge(a0, a1):
        return jnp.where(_lane(a0.shape) < HEAD_DIM, a0, a1)

    def attn_fwd(q, k, v, q_blk0, k_blk0, v_blk0, cumt, packed, chunked, name, side=None):
        has_bias = cumt is not None
        n_pairs = (FH if packed else MH) // 2
        wq = LANES if packed else 2 * HEAD_PAD
        assert TQ == TK

        n_main = 4 if has_bias else 3
        n_side_in, n_side_out = (len(side.ins), len(side.outs)) if side else (0, 0)

        def body(*refs):
            if has_bias:
                q_ref, k_ref, v_ref, ct_ref = refs[:n_main]
            else:
                q_ref, k_ref, v_ref = refs[:n_main]
            o_ref, lse_ref = refs[n_main + n_side_in:n_main + n_side_in + 2]
            i = pl.program_id(1)
            if side:
                side_refs = (refs[n_main:n_main + n_side_in], refs[n_main + n_side_in + 2:n_main + n_side_in + 2 + n_side_out],
                             refs[-2], refs[-1])

                @pl.when((pl.program_id(0) == 0) & (i == 0))
                def _():
                    side.start(*side_refs)

            q0 = i * TQ
            qh = _heads(q_ref[...], packed)
            q_both = jnp.concatenate(qh, axis=0) if packed else None

            def scores(kb):
                kblk = k_ref[pl.ds(pl.multiple_of(kb * TK, TK), TK), :]
                if packed:
                    return _nt(kblk, q_both)
                return jnp.concatenate([_nt(kblk[:, :HEAD_PAD], qh[0]), _nt(kblk[:, HEAD_PAD:], qh[1])], axis=1)

            def update(kb, s, carry, masked):
                m, l, acc = carry
                k0 = pl.multiple_of(kb * TK, TK)
                s0, s1 = s[:, :TQ], s[:, TQ:]
                if has_bias:
                    ck = ct_ref[0, :, pl.ds(k0, TK), :]
                    s0, s1 = s0 - jnp.tile(ck[0], (1, TQ // LANES)), s1 - jnp.tile(ck[1], (1, TQ // LANES))
                if masked:
                    allow = jnp.transpose(_allowed(q0, k0, chunked))
                    s0, s1 = jnp.where(allow, s0, NEG), jnp.where(allow, s1, NEG)
                s = jnp.concatenate([s0, s1], axis=1)
                m_new = jnp.maximum(m, jnp.max(s, axis=0, keepdims=True))
                p = jnp.exp(s - m_new)
                alpha = jnp.exp(m - m_new)
                l = alpha * l + jnp.sum(p, axis=0, keepdims=True)
                return m_new, l, alpha * acc + _tn(v_ref[pl.ds(k0, TK), :], p.astype(BF16))

            def several(kb, carry, n, last_masked):
                ss = [scores(kb + u) for u in range(n)]
                for u in range(n):
                    carry = update(kb + u, ss[u], carry, last_masked and u == n - 1)
                return carry

            init = (jnp.full((1, 2 * TQ), NEG, F32), jnp.zeros((1, 2 * TQ), F32), jnp.zeros((LANES, 2 * TQ), F32))
            carry = lax.fori_loop(0, i // FWD_UNROLL, lambda t, cr: several(FWD_UNROLL * t, cr, FWD_UNROLL, False), init)
            m, l, acc = lax.switch(i % FWD_UNROLL,
                                   [lambda cr, r=r: several(i - r, cr, r + 1, True) for r in range(FWD_UNROLL)], carry)
            o = acc / l
            lse = m + jnp.log(l)
            o_ref[...] = jnp.transpose(jnp.concatenate([o[:HEAD_DIM, :TQ], o[HEAD_DIM:, TQ:]], axis=0))
            sub = lax.broadcasted_iota(jnp.int32, (LANES, TQ), 0)
            lse_ref[...] = jnp.transpose(jnp.where(sub == 0, lse[:, :TQ], jnp.where(sub == 1, lse[:, TQ:], 0.0)))
            if side:
                @pl.when((pl.program_id(0) == n_pairs - 1) & (i == S // TQ - 1))
                def _():
                    side.wait(*side_refs)

        in_specs = [pl.BlockSpec((TQ, wq), lambda p, i: (i, q_blk0 + p)),
                    pl.BlockSpec((S, wq), lambda p, i: (0, k_blk0 + p)),
                    pl.BlockSpec((S, LANES), lambda p, i: (0, v_blk0 + p))]
        args = [q, k, v]
        if has_bias:
            in_specs += [pl.BlockSpec((1, 2, S, LANES), lambda p, i: (p, 0, 0, 0))]
            args += [cumt]
        out_specs = [pl.BlockSpec((TQ, LANES), lambda p, i: (i, p)), pl.BlockSpec((TQ, LANES), lambda p, i: (i, p))]
        out_shape = [jax.ShapeDtypeStruct((S, n_pairs * LANES), F32), jax.ShapeDtypeStruct((S, n_pairs * LANES), F32)]
        extra = {}
        if side:
            hbm = pl.BlockSpec(memory_space=pl.ANY)
            in_specs += [hbm] * n_side_in
            args += side.ins
            out_specs += [hbm] * n_side_out
            out_shape += side.outs
            extra = dict(scratch_shapes=[pltpu.SemaphoreType.DMA((side.n_sems,)), pltpu.SemaphoreType.DMA((side.n_sems,))],
                         input_output_aliases={n_main + a: 2 + b for a, b in side.aliases.items()})
        outs = pl.pallas_call(
            body, name=name, grid=(n_pairs, S // TQ), in_specs=in_specs, out_specs=out_specs, out_shape=out_shape,
            compiler_params=_params(("arbitrary", "arbitrary")), **extra)(*args)
        return (outs[0], outs[1], list(outs[2:])) if side else outs

    def gate_outproj(of, om, z, w_out, l, x, gate):
        def body(of_ref, om_ref, fg_ref, mg_ref, w_ref, x_ref, gt_ref, xn_ref, u_ref, y_ref):
            fg, mg = fg_ref[...], mg_ref[...]
            u = jnp.concatenate([of_ref[...] * fg * _sigmoid(fg), om_ref[...] * mg * _sigmoid(mg)], axis=1).astype(BF16)
            y = _nn(u, w_ref[...])
            u_ref[...] = u
            y_ref[...] = y.astype(BF16)
            xn_ref[...] = x_ref[...] + gt_ref[...] * y

        return pl.pallas_call(
            body, name="gate_outproj", grid=(n_tok,),
            in_specs=[tok(FW), tok(MW), tok(FW, (cfg.o_fg - ZO) // FW), tok(MW, (cfg.o_mg - ZO) // MW),
                      layer(l, (FW + MW, D)), tok(D), const((1, D))],
            out_specs=[tok(D), tok(FW + MW), tok(D)],
            out_shape=[jax.ShapeDtypeStruct((S, D), F32), jax.ShapeDtypeStruct((S, FW + MW), BF16),
                       jax.ShapeDtypeStruct((S, D), BF16)],
            compiler_params=_params(("parallel",)))(of, om, z, z, w_out, x, gate)

    def final_loss(x, g, target):
        def body(x_ref, g_ref, t_ref, dx_ref, acc_ref, loss_ref):
            @pl.when(pl.program_id(0) == 0)
            def _():
                acc_ref[...] = jnp.zeros_like(acc_ref)
                loss_ref[...] = jnp.zeros_like(loss_ref)

            gv = g_ref[...]
            y, xh, rstd = _rms(x_ref[...], gv)
            e = y - t_ref[...]
            loss_ref[...] += 0.5 * jnp.sum(jnp.sum(e * e, axis=-1, keepdims=True) / D, axis=0, keepdims=True)
            dx, dg = _rms_bwd(e / D, gv, xh, rstd)
            dx_ref[...] = dx
            acc_ref[0:1, :] += dg

        return pl.pallas_call(
            body, name="final_loss", grid=(n_tok,),
            in_specs=[tok(D), const((1, D)), tok(D)],
            out_specs=[tok(D), const((8, D)), const((1, LANES))],
            out_shape=[jax.ShapeDtypeStruct((S, D), F32), jax.ShapeDtypeStruct((8, D), F32),
                       jax.ShapeDtypeStruct((1, LANES), F32)],
            compiler_params=_params(("arbitrary",)))(x, g, target)


    def bwd_out(dxn, gate, y, w_out, l, of, om, z, lse_f, lse_m):
        def body(dx_ref, gt_ref, y_ref, w_ref, of_ref, om_ref, fg_ref, mg_ref, lf_ref, lm_ref,
                 dof_ref, dom_ref, dfg_ref, dmg_ref, dy_ref, acc_ref, sf_ref, sm_ref):
            @pl.when(pl.program_id(0) == 0)
            def _():
                acc_ref[...] = jnp.zeros_like(acc_ref)

            dxv = dx_ref[...]
            acc_ref[0:1, :] += jnp.sum(dxv * y_ref[...].astype(F32), axis=0, keepdims=True)
            dy = (gt_ref[...] * dxv).astype(BF16)
            dy_ref[...] = dy
            du = _nt(dy, w_ref[...])
            lane = _lane((TM, LANES))
            for lo, width, o_ref, g_ref, do_ref, dg_ref, l_ref, st_ref in (
                    (0, FW, of_ref, fg_ref, dof_ref, dfg_ref, lf_ref, sf_ref),
                    (FW, MW, om_ref, mg_ref, dom_ref, dmg_ref, lm_ref, sm_ref)):
                gv, ov = g_ref[...], o_ref[...]
                sg = _sigmoid(gv)
                dup = du[:, lo:lo + width]
                dob = (dup * gv * sg).astype(BF16)
                do_ref[...] = dob
                dg_ref[...] = (dup * ov * sg * (1.0 + gv * (1.0 - sg))).astype(BF16)
                d = dob.astype(F32) * ov
                for pr in range(width // LANES):
                    cols = slice(pr * LANES, (pr + 1) * LANES)
                    dp = d[:, cols]
                    d0 = jnp.sum(jnp.where(lane < HEAD_DIM, dp, 0.0), axis=-1, keepdims=True)
                    d1 = jnp.sum(jnp.where(lane >= HEAD_DIM, dp, 0.0), axis=-1, keepdims=True)
                    st_ref[:, cols] = jnp.where(lane == 2, d0, jnp.where(lane == 3, d1, l_ref[:, cols]))

        return pl.pallas_call(
            body, name="bwd_out", grid=(n_tok,),
            in_specs=[tok(D), const((1, D)), tok(D), layer(l, (FW + MW, D)), tok(FW), tok(MW),
                      tok(FW, (cfg.o_fg - ZO) // FW), tok(MW, (cfg.o_mg - ZO) // MW), tok(FW), tok(MW)],
            out_specs=[tok(FW), tok(MW), tok(FW), tok(MW), tok(D), const((8, D)), tok(FW), tok(MW)],
            out_shape=[jax.ShapeDtypeStruct((S, FW), BF16), jax.ShapeDtypeStruct((S, MW), BF16),
                       jax.ShapeDtypeStruct((S, FW), BF16), jax.ShapeDtypeStruct((S, MW), BF16),
                       jax.ShapeDtypeStruct((S, D), BF16), jax.ShapeDtypeStruct((8, D), F32),
                       jax.ShapeDtypeStruct((S, FW), F32), jax.ShapeDtypeStruct((S, MW), F32)],
            compiler_params=_params(("arbitrary",)))(dxn, gate, y, w_out, of, om, z, z, lse_f, lse_m)

    def attn_bwd(q, k, v, do, stats, q_blk0, k_blk0, v_blk0, cumt, packed, chunked, name, side=None):
        has_bias = cumt is not None
        n_pairs = (FH if packed else MH) // 2
        wq = LANES if packed else 2 * HEAD_PAD
        n_q = S // TQ
        assert TQ == TK
        n_in, n_out = (6, 5) if has_bias else (5, 3)
        n_side_in, n_side_out = (len(side.ins), len(side.outs)) if side else (0, 0)

        def body(*refs):
            main_out = refs[n_in + n_side_in:n_in + n_side_in + n_out]
            if has_bias:
                q_ref, k_ref, v_ref, do_ref, st_ref, ct_ref = refs[:n_in]
                dq_ref, dk_ref, dv_ref, dc_ref, dr_ref = main_out
            else:
                q_ref, k_ref, v_ref, do_ref, st_ref = refs[:n_in]
                dq_ref, dk_ref, dv_ref = main_out
            jb = pl.program_id(1)
            k0 = jb * TK
            if side:
                side_refs = (refs[n_in:n_in + n_side_in], refs[n_in + n_side_in + n_out:n_in + n_side_in + n_out + n_side_out],
                             refs[-2], refs[-1])

                @pl.when((pl.program_id(0) == 0) & (jb == 0))
                def _():
                    side.start(*side_refs)

            @pl.when(jb == 0)
            def _():
                dq_ref[...] = jnp.zeros_like(dq_ref)
                if has_bias:
                    dr_ref[...] = jnp.zeros_like(dr_ref)

            dk_ref[...] = jnp.zeros_like(dk_ref)
            dv_ref[...] = jnp.zeros_like(dv_ref)
            kh = _heads(k_ref[...], packed)
            k_both = jnp.concatenate(kh, axis=0) if packed else None
            v_both = jnp.concatenate(_heads(v_ref[...], True), axis=0)
            if has_bias:
                ct = jnp.transpose(ct_ref[0, 0])
                ck = jnp.concatenate([ct[:, 0:1], ct[:, 1:2]], axis=0)

            def products(ib):
                rows = pl.ds(pl.multiple_of(ib * TQ, TQ), TQ)
                q2, do2 = q_ref[rows, :], do_ref[rows, :]
                if packed:
                    s = _nt(k_both, q2)
                else:
                    qh = _heads(q2, False)
                    s = jnp.concatenate([_nt(kh[0], qh[0]), _nt(kh[1], qh[1])], axis=0)
                return s, _nt(v_both, do2)

            def update(ib, s, dp, carry, masked):
                q0 = pl.multiple_of(ib * TQ, TQ)
                rows = pl.ds(q0, TQ)
                q2, do2 = q_ref[rows, :], do_ref[rows, :]
                st = jnp.transpose(st_ref[rows, :])
                if not packed:
                    qh = _heads(q2, False)
                if has_bias:
                    s = s - ck
                if masked:
                    allow = jnp.transpose(_allowed(q0, k0, chunked))
                    s = jnp.where(jnp.concatenate([allow, allow], axis=0), s, NEG)
                p = jnp.concatenate([jnp.exp(s[:TK] - st[0:1, :]), jnp.exp(s[TK:] - st[1:2, :])], axis=0)
                dv2 = _nn(p.astype(BF16), do2)
                ds = jnp.concatenate([p[:TK] * (dp[:TK] - st[2:3, :]), p[TK:] * (dp[TK:] - st[3:4, :])], axis=0)
                dsb = ds.astype(BF16)
                dv_ref[...] += _merge(dv2[:TK], dv2[TK:])
                if packed:
                    dk2 = _nn(dsb, q2)
                    dk_ref[...] += _merge(dk2[:TK], dk2[TK:])
                    dq_ref[rows, :] += _tn(dsb, k_both)
                else:
                    dk_ref[...] += jnp.concatenate([_nn(dsb[:TK], qh[0]), _nn(dsb[TK:], qh[1])], axis=1)
                    dq_ref[rows, :] += jnp.concatenate([_tn(dsb[:TK], kh[0]), _tn(dsb[TK:], kh[1])], axis=1)
                if has_bias:
                    sub = lax.broadcasted_iota(jnp.int32, (8, TQ), 0)
                    r0 = jnp.sum(ds[:TK], axis=0, keepdims=True)
                    r1 = jnp.sum(ds[TK:], axis=0, keepdims=True)
                    dr_ref[0, ib] += jnp.where(sub == 0, r0, jnp.where(sub == 1, r1, 0.0))
                    return carry - jnp.sum(ds, axis=1, keepdims=True)
                return carry

            def step(ib, carry, masked):
                s, dp = products(ib)
                return update(ib, s, dp, carry, masked)

            def two(ib, carry):
                s_a, dp_a = products(ib)
                s_b, dp_b = products(ib + 1)
                return update(ib + 1, s_b, dp_b, update(ib, s_a, dp_a, carry, False), False)

            dc = step(jb, jnp.zeros((2 * TK, 1), F32), True)
            n_rest = n_q - 1 - jb
            dc = lax.fori_loop(0, n_rest // 2, lambda t, cr: two(jb + 1 + 2 * t, cr), dc)
            dc = lax.cond(n_rest % 2 == 1, lambda cr: step(n_q - 1, cr, False), lambda cr: cr, dc)
            if has_bias:
                lane = _lane((TK, LANES))
                dc_ref[0] = jnp.where(lane == 0, dc[:TK], jnp.where(lane == 1, dc[TK:], 0.0))
            if side:
                @pl.when((pl.program_id(0) == n_pairs - 1) & (jb == S // TK - 1))
                def _():
                    side.wait(*side_refs)

        in_specs = [pl.BlockSpec((S, wq), lambda p, j: (0, q_blk0 + p)),
                    pl.BlockSpec((TK, wq), lambda p, j: (j, k_blk0 + p)),
                    pl.BlockSpec((TK, LANES), lambda p, j: (j, v_blk0 + p)),
                    pl.BlockSpec((S, LANES), lambda p, j: (0, p)),
                    pl.BlockSpec((S, LANES), lambda p, j: (0, p))]
        args = [q, k, v, do, stats]
        out_specs = [pl.BlockSpec((S, wq), lambda p, j: (0, p)),
                     pl.BlockSpec((TK, wq), lambda p, j: (j, p)),
                     pl.BlockSpec((TK, LANES), lambda p, j: (j, p))]
        out_shape = [jax.ShapeDtypeStruct((S, n_pairs * wq), F32), jax.ShapeDtypeStruct((S, n_pairs * wq), F32),
                     jax.ShapeDtypeStruct((S, n_pairs * LANES), F32)]
        if has_bias:
            in_specs += [pl.BlockSpec((1, 1, 8, TK), lambda p, j: (p, j, 0, 0))]
            args += [cumt]
            out_specs += [pl.BlockSpec((1, TK, LANES), lambda p, j: (p, j, 0)),
                          pl.BlockSpec((1, S // TQ, 8, TQ), lambda p, j: (p, 0, 0, 0))]
            out_shape += [jax.ShapeDtypeStruct((n_pairs, S, LANES), F32),
                          jax.ShapeDtypeStruct((n_pairs, S // TQ, 8, TQ), F32)]
        extra = {}
        if side:
            hbm = pl.BlockSpec(memory_space=pl.ANY)
            in_specs += [hbm] * n_side_in
            args += side.ins
            out_specs += [hbm] * n_side_out
            out_shape += side.outs
            extra = dict(scratch_shapes=[pltpu.SemaphoreType.DMA((side.n_sems,)), pltpu.SemaphoreType.DMA((side.n_sems,))],
                         input_output_aliases={n_in + a: n_out + b for a, b in side.aliases.items()})
        outs = pl.pallas_call(
            body, name=name, grid=(n_pairs, S // TK), in_specs=in_specs, out_specs=out_specs, out_shape=out_shape,
            compiler_params=_params(("arbitrary", "arbitrary")), **extra)(*args)
        return (*outs[:n_out], list(outs[n_out:])) if side else outs

    def fox_post(dcum, z, bf_pad):
        def rev(i):
            return n_tok - 1 - i

        def body(dc_ref, z_ref, b_ref, dff_ref, acc_ref, carry):
            @pl.when(pl.program_id(0) == 0)
            def _():
                carry[...] = jnp.zeros_like(carry)
                acc_ref[...] = jnp.zeros_like(acc_ref)

            _, sig_neg = _log_f_terms(z_ref[...], b_ref[...])
            row = lax.broadcasted_iota(jnp.int32, (TM, TM), 0)
            col = lax.broadcasted_iota(jnp.int32, (TM, TM), 1)
            tri = (col >= row).astype(F32)
            dlog = jnp.dot(tri, dc_ref[...], precision=lax.Precision.HIGHEST, preferred_element_type=F32) + carry[...]
            carry[...] = dlog[0:1, :]
            dff = dlog * sig_neg
            dff_ref[...] = dff
            acc_ref[0:1, :] += jnp.sum(dff, axis=0, keepdims=True)

        return pl.pallas_call(
            body, name="fox_post", grid=(n_tok,),
            in_specs=[pl.BlockSpec((TM, LANES), lambda i: (rev(i), 0)),
                      pl.BlockSpec((TM, LANES), lambda i: (rev(i), misc_blk)), const((1, LANES))],
            out_specs=[pl.BlockSpec((TM, LANES), lambda i: (rev(i), 0)), const((8, LANES))],
            out_shape=[jax.ShapeDtypeStruct((S, LANES), F32), jax.ShapeDtypeStruct((8, LANES), F32)],
            scratch_shapes=[pltpu.VMEM((1, LANES), F32)],
            compiler_params=_params(("arbitrary",)))(dcum, z, bf_pad)

    def mla_post(dq, dk, dv, dff, z, gq, gkv, w_uq, w_uk, w_v, l, cos_t, sin_t):
        def body(dq_ref, dk_ref, dv_ref, dff_ref, ql_ref, kvl_ref, gq_ref, gkv_ref, wq_ref, wk_ref, wv_ref,
                 c_ref, s_ref, zq_ref, zkv_ref, zms_ref, dwq_ref, dwk_ref, dwv_ref, dgq_ref, dgkv_ref):
            @pl.when(pl.program_id(0) == 0)
            def _():
                for r in (dwq_ref, dwk_ref, dwv_ref, dgq_ref, dgkv_ref):
                    r[...] = jnp.zeros_like(r)

            cos1, sin1 = c_ref[...], s_ref[...]
            gqv, gkvv = gq_ref[...], gkv_ref[...]
            qn, qxh, qrstd = _rms(ql_ref[...], gqv)
            dq_pre = _rope_t(dq_ref[...] * MLA_SCALE, jnp.tile(cos1, (1, MH)), jnp.tile(sin1, (1, MH))).astype(BF16)
            dwq_ref[...] += _tn(qn.astype(BF16), dq_pre)
            dql, dgq = _rms_bwd(_nt(dq_pre, wq_ref[...]), gqv, qxh, qrstd)
            zq_ref[...] = dql.astype(BF16)
            dgq_ref[0:1, :] += dgq

            dkv = dk_ref[...]
            lane = _lane(dkv.shape) & (HEAD_PAD - 1)
            dkn = jnp.where(lane < HEAD_DIM, dkv, 0.0).astype(BF16)
            dkr = dkv[:, 0:HEAD_PAD]
            for hd in range(1, MH):
                dkr = dkr + dkv[:, hd * HEAD_PAD:(hd + 1) * HEAD_PAD]
            lane1 = _lane(dkr.shape)
            dkr = jnp.where((lane1 >= ROPE_LO) & (lane1 < ROPE_LO + ROPE_DIM), dkr, 0.0)
            dkr = _rope_t(dkr, cos1, sin1)
            zms_ref[...] = (dkr + dff_ref[...]).astype(BF16)

            kvn, kxh, krstd = _rms(kvl_ref[...], gkvv)
            kvb = kvn.astype(BF16)
            dvb = dv_ref[...].astype(BF16)
            dwk_ref[...] += _tn(kvb, dkn)
            dwv_ref[...] += _tn(kvb, dvb)
            dkvl, dgkv = _rms_bwd(_nt(dkn, wk_ref[...]) + _nt(dvb, wv_ref[...]), gkvv, kxh, krstd)
            zkv_ref[...] = dkvl.astype(BF16)
            dgkv_ref[0:1, :] += dgkv

        return pl.pallas_call(
            body, name="mla_post", grid=(n_tok,),
            in_specs=[tok(QW), tok(QW), tok(MW), tok(LANES), tok(QL, (cfg.o_ql - ZO) // QL), tok(KVL, (cfg.o_kv - ZO) // KVL),
                      const((1, QL)), const((1, KVL)), layer(l, (QL, QW)), layer(l, (KVL, QW)), layer(l, (KVL, MW)),
                      tok(LANES), tok(LANES)],
            out_specs=[tok(QL), tok(KVL), tok(LANES), const((QL, QW)), const((KVL, QW)), const((KVL, MW)),
                       const((8, QL)), const((8, KVL))],
            out_shape=[jax.ShapeDtypeStruct((S, QL), BF16), jax.ShapeDtypeStruct((S, KVL), BF16),
                       jax.ShapeDtypeStruct((S, LANES), BF16), jax.ShapeDtypeStruct((QL, QW), F32),
                       jax.ShapeDtypeStruct((KVL, QW), F32), jax.ShapeDtypeStruct((KVL, MW), F32),
                       jax.ShapeDtypeStruct((8, QL), F32), jax.ShapeDtypeStruct((8, KVL), F32)],
            compiler_params=_params(("arbitrary",)))(dq, dk, dv, dff, z, z, gq, gkv, w_uq, w_uk, w_v, cos_t, sin_t)

    def bwd_in(dz, w_in, l, x, dxn, g, scale):
        def body(dz_ref, w_ref, x_ref, dx_ref, g_ref, sc_ref, o_ref, acc_ref):
            @pl.when(pl.program_id(0) == 0)
            def _():
                acc_ref[...] = jnp.zeros_like(acc_ref)

            dh = _nt(dz_ref[...], w_ref[...])
            gv, mod = g_ref[...], 1.0 + sc_ref[...]
            _, xh, rstd = _rms(x_ref[...], gv)
            t = dh * xh
            acc_ref[0:1, :] += jnp.sum(dh, axis=0, keepdims=True)
            acc_ref[1:2, :] += jnp.sum(t * gv, axis=0, keepdims=True)
            acc_ref[2:3, :] += jnp.sum(t * mod, axis=0, keepdims=True)
            dx, _ = _rms_bwd(dh, gv * mod, xh, rstd)
            o_ref[...] = dx_ref[...] + dx

        return pl.pallas_call(
            body, name="bwd_in", grid=(n_tok,),
            in_specs=[tok(NZ), layer(l, (D, NZ)), tok(D), tok(D), const((1, D)), const((1, D))],
            out_specs=[tok(D), const((8, D))],
            out_shape=[jax.ShapeDtypeStruct((S, D), F32), jax.ShapeDtypeStruct((8, D), F32)],
            compiler_params=_params(("arbitrary",)))(dz, w_in, x, dxn, g, scale)


    def pair_rows(cum):
        n_pairs = FH // 2
        ct = jnp.pad(cum[:, :FH].T.reshape(n_pairs, 2, S), ((0, 0), (0, 6), (0, 0)))
        return ct.reshape(n_pairs, 8, S // TK, TK).transpose(0, 2, 1, 3)

    def bias_grad(dc, dr):
        n_pairs = FH // 2
        d = dr.transpose(0, 2, 1, 3).reshape(n_pairs, 8, S)[:, :2, :].reshape(FH, S).T
        d = d + dc[:, :, :2].transpose(1, 0, 2).reshape(S, FH)
        return jnp.pad(d, ((0, 0), (0, LANES - FH)))

    def layer_forward(x, wl, mod, later_shards=None):
        shift, scale, gate = mod
        h, z, qkv = ln_inproj(x, wl.norm_g, scale, shift, wl.w_in, wl.l)
        cum, tiles = fox_prep(z, wl.bf_pad)
        ct = pair_rows(cum)
        fox = attn_fwd(qkv, qkv, qkv, cfg.o_fq // LANES, cfg.o_fk // LANES, cfg.o_fv // LANES, tiles, True, False,
                       "fox_fwd_gather" if later_shards else "fox_fwd", GatherOverIci(later_shards) if later_shards else None)
        of, lse_f = fox[0], fox[1]
        qp, kp, vp = mla_prep(z, wl.gq, wl.gkv, wl.w_uq, wl.w_uk, wl.w_v, wl.l, cfg.cos_t, cfg.sin_t)
        mla = attn_fwd(qp, kp, vp, 0, 0, 0, None, False, True, "mla_fwd_gather" if later_shards else "mla_fwd",
                       GatherToSibling(fox[2]) if later_shards else None)
        om, lse_m = mla[0], mla[1]
        xn, u, y = gate_outproj(of, om, z, wl.w_out, wl.l, x, gate)
        saved = types.SimpleNamespace(x=x, h=h, z=z, qkv=qkv, ct=ct, of=of, lse_f=lse_f, qp=qp, kp=kp, vp=vp,
                                      om=om, lse_m=lse_m, u=u, y=y)
        return (xn, saved, mla[2]) if later_shards else (xn, saved)

    def layer_backward(dxn, sv, wl, mod, side=None):
        shift, scale, gate = mod
        do_f, do_m, dfg, dmg, dy, acc_o, st_f, st_m = bwd_out(dxn, gate, sv.y, wl.w_out, wl.l, sv.of, sv.om, sv.z,
                                                              sv.lse_f, sv.lse_m)
        dw_out = matmul_tn(sv.u, dy, "dw_out")
        dfq, dfk, dfv, dck, dcr, *rode = attn_bwd(sv.qkv, sv.qkv, sv.qkv, do_f, st_f, cfg.o_fq // LANES, cfg.o_fk // LANES,
                                                  cfg.o_fv // LANES, sv.ct, True, False,
                                                  "fox_bwd_exchange" if side else "fox_bwd", side)
        dff, acc_f = fox_post(bias_grad(dck, dcr), sv.z, wl.bf_pad)
        dqp, dkp, dvp = attn_bwd(sv.qp, sv.kp, sv.vp, do_m, st_m, 0, 0, 0, None, False, True, "mla_bwd")
        zq, zkv, zms, dw_uq, dw_uk, dw_v, dgq, dgkv = mla_post(
            dqp, dkp, dvp, dff, sv.z, wl.gq, wl.gkv, wl.w_uq, wl.w_uk, wl.w_v, wl.l, cfg.cos_t, cfg.sin_t)
        dz = jnp.concatenate([(dfq * FOX_SCALE).astype(BF16), dfk.astype(BF16), dfv.astype(BF16), dfg, dmg, zq, zkv, zms], axis=1)
        dx, acc_i = bwd_in(dz, wl.w_in, wl.l, sv.x, dxn, wl.norm_g, scale)
        dw_in = matmul_tn(sv.h, dz, "dw_in")
        grads = types.SimpleNamespace(
            w_in=dw_in, w_out=dw_out, w_uq=dw_uq, w_uk=dw_uk, w_v=dw_v, gq=dgq[0], gkv=dgkv[0],
            b_f=acc_f[0, :FH], norm_g=acc_i[2], dmod=jnp.concatenate([acc_i[0], acc_i[1], acc_o[0]]))
        return (dx, grads, rode[0]) if side else (dx, grads)

    return types.SimpleNamespace(layer_forward=layer_forward, layer_backward=layer_backward, final_loss=final_loss)


def _pack_rows(parts, dtype, row_multiple):
    flat = jnp.concatenate([p.reshape(-1).astype(dtype) for p in parts])
    per = SLAB_COLS * row_multiple
    total = -(-flat.shape[0] // per) * per
    return jnp.pad(flat, (0, total - flat.shape[0])).reshape(total // SLAB_COLS, SLAB_COLS)


def _unpack(flat, shapes):
    out, off = [], 0
    for shp in shapes:
        n = 1
        for d in shp:
            n *= d
        out.append(flat[off:off + n].reshape(shp))
        off += n
    return out


def kernel(x, c, positions, norm_g, w_ada, b_ada, w_in, b_f, q_norm_g, w_uq, kv_norm_g, w_ukv, w_out, final_g, loss_target, m_norm_g, m_w_ada, m_b_ada, m_w_in, m_b_f, m_q_norm_g, m_w_uq, m_kv_norm_g, m_w_ukv, m_w_out, m_final_g, v_norm_g, v_w_ada, v_b_ada, v_w_in, v_b_f, v_q_norm_g, v_w_uq, v_kv_norm_g, v_w_ukv, v_w_out, v_final_g):
    S, D = x.shape[1], x.shape[2]
    L = norm_g.shape[0]
    FH = b_f.shape[1]
    QL, KVL = q_norm_g.shape[1], kv_norm_g.shape[1]
    MH = w_ukv.shape[2] * N_CHIPS // (2 * HEAD_DIM)
    FW, MW = FH * HEAD_DIM, MH * HEAD_DIM
    NA = w_ada.shape[2]
    n_in = w_in.shape[2] * N_CHIPS
    cfg = types.SimpleNamespace(S=S, D=D, FW=FW, MW=MW, QL=QL, KVL=KVL, FH=FH, MH=MH)
    cfg.o_fq, cfg.o_fk, cfg.o_fv, cfg.o_fg, cfg.o_mg = 0, FW, 2 * FW, 3 * FW, 4 * FW
    cfg.o_ql = 4 * FW + MW
    cfg.o_kv = cfg.o_ql + QL
    cfg.o_ms = cfg.o_kv + KVL
    cfg.NZ = cfg.o_ms + LANES
    assert FW == MW and FH % 2 == 0 and MH % 2 == 0 and cfg.o_ql % QL == 0 and cfg.o_kv % KVL == 0 and KVL == LANES
    assert n_in == 4 * FW + FH + QL + KVL + ROPE_DIM + MW

    mx, my, mc = _my_pos()
    my_chip = 2 * mx + my
    my_dev = 2 * my_chip + mc

    inv_freq = 1.0 / (ROPE_THETA ** (jnp.arange(0, ROPE_DIM, 2, dtype=F32) / ROPE_DIM))
    ang = positions[0].astype(F32)[:, None] * inv_freq
    cos, sin = jnp.cos(ang), jnp.sin(ang)
    cfg.cos_t = jnp.concatenate([jnp.ones((S, ROPE_LO), F32), cos, cos, jnp.ones((S, HEAD_PAD - ROPE_LO - ROPE_DIM), F32)], axis=1)
    cfg.sin_t = jnp.concatenate([jnp.zeros((S, ROPE_LO), F32), -sin, sin, jnp.zeros((S, HEAD_PAD - ROPE_LO - ROPE_DIM), F32)], axis=1)

    assert L % 4 == 0
    def lane_pad(a):
        return jnp.pad(a, ((0, 0),) * (a.ndim - 1) + ((0, -a.shape[-1] % LANES),))

    shards = (w_in, w_uq, w_ukv, w_out)
    padded = [lane_pad(w.astype(BF16)) for w in shards]

    def kernel_layouts(gathered, own):
        def all_chips(g, w, axis):
            return jnp.concatenate([jnp.where(my_chip == t, w.astype(BF16), g[t][..., :w.shape[-1]])
                                    for t in range(N_CHIPS)], axis=axis)

        n = own[0].shape[0]
        w_in_f = all_chips(gathered[0], own[0], 2)
        w_uq_f = all_chips(gathered[1], own[1], 2)
        w_ukv_f = all_chips(gathered[2], own[2], 2)
        w_out_f = all_chips(gathered[3], own[3], 1)
        sizes = (FW, FW, FW, FH, FW, QL, KVL, ROPE_DIM, MW)
        offs = [0]
        for sz in sizes:
            offs.append(offs[-1] + sz)
        fq_w, fk_w, fv_w, ff_w, fg_w, ql_w, kvl_w, kr_w, mg_w = [w_in_f[:, :, offs[i]:offs[i + 1]] for i in range(len(sizes))]
        zeros = lambda width: jnp.zeros((n, D, width), BF16)
        w_in_p = jnp.concatenate([fq_w, fk_w, fv_w, fg_w, mg_w, ql_w, kvl_w, ff_w, zeros(ROPE_LO - FH), kr_w,
                                  zeros(HEAD_PAD - ROPE_LO - ROPE_DIM)], axis=2)
        w_uq_p = jnp.pad(w_uq_f.reshape(n, QL, MH, HEAD_DIM + ROPE_DIM),
                         ((0, 0), (0, 0), (0, 0), (0, HEAD_PAD - HEAD_DIM - ROPE_DIM))).reshape(n, QL, MH * HEAD_PAD)
        w_ukv4 = w_ukv_f.reshape(n, KVL, MH, 2 * HEAD_DIM)
        w_uk_p = jnp.pad(w_ukv4[..., :HEAD_DIM], ((0, 0), (0, 0), (0, 0), (0, HEAD_PAD - HEAD_DIM))).reshape(n, KVL, MH * HEAD_PAD)
        return types.SimpleNamespace(w_in=w_in_p, w_uq=w_uq_p, w_uk=w_uk_p, w_v=w_ukv4[..., HEAD_DIM:].reshape(n, KVL, MW),
                                     w_out=w_out_f)

    first = kernel_layouts(weights_gather([p[:1] for p in padded], "weights_first"), [w[:1] for w in shards])

    c_all = allgather8(c.reshape(8, D // 8), "gather_c").reshape(N_DEV, D)
    c_pad = jnp.pad(c_all, ((0, 16 - N_DEV), (0, 0)))
    mod_part = ada_forward(c_pad, w_ada)[:, :N_DEV, :]
    mod_all = allgather8(mod_part.reshape(-1, LANES), "gather_mod").reshape(N_CHIPS, 2, L, N_DEV, NA)[:, 0]
    mod_full = mod_all.transpose(1, 2, 0, 3).reshape(L, N_DEV, N_CHIPS * NA) + b_ada[:, None, :]
    mod_mine = lax.dynamic_index_in_dim(mod_full, my_dev, axis=1, keepdims=True)

    step = make_step(cfg)
    bf_pad = jnp.pad(b_f, ((0, 0), (0, LANES - FH)))
    mods = [(mod_mine[l, :, :D], mod_mine[l, :, D:2 * D], mod_mine[l, :, 2 * D:]) for l in range(L)]

    def layer_params(l, ws_, at):
        return types.SimpleNamespace(l=at, norm_g=norm_g[l][None], bf_pad=bf_pad[l][None], gq=q_norm_g[l][None],
                                     gkv=kv_norm_g[l][None], w_in=ws_.w_in, w_uq=ws_.w_uq, w_uk=ws_.w_uk, w_v=ws_.w_v,
                                     w_out=ws_.w_out)

    layers = [layer_params(0, first, 0)]
    xl, sv, later = step.layer_forward(x[0], layers[0], mods[0], [p[1:] for p in padded])
    saved = [sv]
    rest = kernel_layouts(later, [w[1:] for w in shards])
    for l in range(1, L):
        layers.append(layer_params(l, rest, l - 1))
        xl, sv = step.layer_forward(xl, layers[l], mods[l])
        saved.append(sv)
    dx, acc_fin, loss_part = step.final_loss(xl, final_g[None], loss_target[0])
    loss = lax.psum(loss_part[0, 0], ("x", "y", "c"))
    ms = cfg.o_ms
    runs = [(0, 3 * FW, 0), (3 * FW, FH, ms), (3 * FW + FH, FW, cfg.o_fg), (4 * FW + FH, QL + KVL, cfg.o_ql),
            (4 * FW + FH + QL + KVL, ROPE_DIM, ms + ROPE_LO), (4 * FW + FH + QL + KVL + ROPE_DIM, MW, cfg.o_mg)]
    big_names = ["w_in", "w_uq", "w_ukv", "w_out"]

    def shard_columns(dw, t):
        a, b = t * w_in.shape[2], (t + 1) * w_in.shape[2]
        return jnp.concatenate([dw[:, p0 + max(a, r0) - r0:p0 + min(b, r0 + sz) - r0]
                                for r0, sz, p0 in runs if max(a, r0) < min(b, r0 + sz)], axis=1)

    def chip_parts(group, tag):
        n = len(group)
        stk = lambda name: jnp.stack([getattr(g, name) for g in group])
        g_in4 = jnp.stack([jnp.stack([shard_columns(g.w_in, t) for g in group]) for t in range(N_CHIPS)])
        dw_uq_f = stk("w_uq").reshape(n, QL, MH, HEAD_PAD)[..., :HEAD_DIM + ROPE_DIM].reshape(n, QL, -1)
        dw_ukv_f = jnp.concatenate([stk("w_uk").reshape(n, KVL, MH, HEAD_PAD)[..., :HEAD_DIM],
                                    stk("w_v").reshape(n, KVL, MH, HEAD_DIM)], axis=3).reshape(n, KVL, -1)
        gs = [g_in4, dw_uq_f.reshape(n, QL, N_CHIPS, -1).transpose(2, 0, 1, 3),
              dw_ukv_f.reshape(n, KVL, N_CHIPS, -1).transpose(2, 0, 1, 3), stk("w_out").reshape(n, N_CHIPS, -1, D).transpose(1, 0, 2, 3)]
        gs = [lane_pad(g.astype(BF16)) for g in gs]
        theirs = halves_to_sibling(gs, "grads_sibling_" + tag)
        out = []
        for g, o, nm in zip(gs, theirs, big_names):
            keep = lax.dynamic_slice_in_dim(g, mc * (n // 2), n // 2, axis=1)
            merged = (N_CHIPS * (n // 2),) + g.shape[2:]
            out.append(add_cast(keep.reshape(merged), o.reshape(merged), BF16, "grads_chip_sum_%s_%s" % (nm, tag)).reshape(o.shape))
        return out

    def reduced(chip_part, parts, tag):
        where = jnp.stack([my_chip, mc]).astype(jnp.int32)
        red = [sum_chips(p, o, where, "grads_sum_%s_%s" % (nm, tag)) for p, o, nm in zip(parts, chip_part, big_names)]
        return [g[..., :w.shape[-1]] for g, w in zip(halves_gather(red, "grads_back_" + tag), shards)]

    gl = [None] * L
    half_l = L // 2
    for l in range(L - 1, half_l - 1, -1):
        dx, gl[l] = step.layer_backward(dx, saved[l], layers[l], mods[l])
    part_hi = chip_parts(gl[half_l:], "hi")
    dx, gl[half_l - 1], parts_hi = step.layer_backward(dx, saved[half_l - 1], layers[half_l - 1], mods[half_l - 1],
                                                       ExchangeOverIci(part_hi))
    for l in range(half_l - 2, -1, -1):
        dx, gl[l] = step.layer_backward(dx, saved[l], layers[l], mods[l])
    part_lo = chip_parts(gl[:half_l], "lo")
    parts_lo = run_side(ExchangeOverIci(part_lo), "grads_chips_lo")
    g_w_in, g_w_uq, g_w_ukv, g_w_out = [jnp.concatenate([lo, hi]) for lo, hi in
                                        zip(reduced(part_lo, parts_lo, "lo"), reduced(part_hi, parts_hi, "hi"))]
    grad_x = dx[None]

    stack = lambda name: jnp.stack([getattr(g, name) for g in gl])
    small_parts = [stack("norm_g"), stack("dmod"), stack("b_f"), stack("gq"), stack("gkv"), acc_fin[0]]
    small_shapes = [p.shape for p in small_parts]
    small = _pack_rows(small_parts, F32, 8).reshape(-1, LANES)
    small_all = allgather8(small, "gather_small").reshape(N_DEV, -1, LANES)
    small_sum = sum_leading(small_all, "sum_small")
    g_norm_g, g_b_ada, g_b_f, g_q_norm_g, g_kv_norm_g, g_final_g = _unpack(small_sum.reshape(-1), small_shapes)

    n_ng = L * D
    dmod_all = small_all.reshape(N_DEV, -1)[:, n_ng:n_ng + L * 3 * D].reshape(N_DEV, L, 3 * D)
    dmod_cols = lax.dynamic_slice_in_dim(dmod_all, my_chip * NA, NA, axis=2).transpose(1, 0, 2)
    g_w_ada = ada_backward(c_pad, jnp.pad(dmod_cols, ((0, 0), (0, 16 - N_DEV), (0, 0))))


    names = ["norm_g", "w_ada", "b_ada", "w_in", "b_f", "q_norm_g", "w_uq", "kv_norm_g", "w_ukv", "w_out", "final_g"]
    ws = dict(norm_g=norm_g, w_ada=w_ada, b_ada=b_ada, w_in=w_in, b_f=b_f, q_norm_g=q_norm_g, w_uq=w_uq,
              kv_norm_g=kv_norm_g, w_ukv=w_ukv, w_out=w_out, final_g=final_g)
    msd = dict(norm_g=m_norm_g, w_ada=m_w_ada, b_ada=m_b_ada, w_in=m_w_in, b_f=m_b_f, q_norm_g=m_q_norm_g, w_uq=m_w_uq,
               kv_norm_g=m_kv_norm_g, w_ukv=m_w_ukv, w_out=m_w_out, final_g=m_final_g)
    vsd = dict(norm_g=v_norm_g, w_ada=v_w_ada, b_ada=v_b_ada, w_in=v_w_in, b_f=v_b_f, q_norm_g=v_q_norm_g, w_uq=v_w_uq,
               kv_norm_g=v_kv_norm_g, w_ukv=v_w_ukv, w_out=v_w_out, final_g=v_final_g)
    gsd = dict(norm_g=g_norm_g, w_ada=g_w_ada, b_ada=g_b_ada, w_in=g_w_in, b_f=g_b_f, q_norm_g=g_q_norm_g, w_uq=g_w_uq,
               kv_norm_g=g_kv_norm_g, w_ukv=g_w_ukv, w_out=g_w_out, final_g=g_final_g)
    small_names = ["norm_g", "b_ada", "b_f", "q_norm_g", "kv_norm_g", "final_g"]
    sm_shapes = [ws[n].shape for n in small_names]
    pk = lambda d: _pack_rows([d[n] for n in small_names], F32, 8).reshape(1, -1, LANES)
    sm_out = adamw(pk(ws), pk(gsd), pk(msd), pk(vsd), "adamw_small")
    sm_d, sm_m, sm_v = [dict(zip(small_names, _unpack(o.reshape(-1), sm_shapes))) for o in sm_out]
    delta, new_m, new_v = dict(sm_d), dict(sm_m), dict(sm_v)
    for n in ["w_ada", "w_in", "w_uq", "w_ukv", "w_out"]:
        delta[n], new_m[n], new_v[n] = adamw(ws[n], gsd[n], msd[n], vsd[n], "adamw_" + n)

    return (loss, grad_x, *[gsd[n] for n in names], *[delta[n] for n in names],
            *[new_m[n] for n in names], *[new_v[n] for n in names])
```

```python
import types

import jax
import jax.numpy as jnp
from jax import lax
from jax.experimental import pallas as pl
from jax.experimental.pallas import tpu as pltpu

F32 = jnp.float32
BF16 = jnp.bfloat16
MESH = pl.DeviceIdType.MESH

N_CHIPS = 4
N_DEV = 8
HEAD_DIM = 64
ROPE_DIM = 32
ROPE_THETA = 10000.0
HEAD_PAD = 128
ROPE_LO = 64
ROPE_HALF = 16
CHUNK_SHIFT = 6
LANES = 128
EPS = 1e-6
NEG = -1e30
ADAM_LR = 0.001
ADAM_B1 = 0.9
ADAM_B2 = 0.999
ADAM_EPS = 1e-08
ADAM_WD = 0.01
ADAM_STEP = 10
VMEM_LIMIT = 48 * 1024 * 1024
SLAB_COLS = 1024


def _params(sem=None, vmem=VMEM_LIMIT):
    return pltpu.CompilerParams(dimension_semantics=sem, vmem_limit_bytes=vmem)


def _nn(a, b):
    return jnp.dot(a, b, preferred_element_type=F32)


def _nt(a, b):
    return lax.dot_general(a, b, (((1,), (1,)), ((), ())), preferred_element_type=F32)


def _tn(a, b):
    return lax.dot_general(a, b, (((0,), (0,)), ((), ())), preferred_element_type=F32)


def _sigmoid(x):
    return 1.0 / (1.0 + jnp.exp(-x))


def _lane(shape):
    return lax.broadcasted_iota(jnp.int32, shape, len(shape) - 1)


def _pick(n, cands):
    for c in cands:
        if n % c == 0:
            return c
    return n


def _my_pos():
    return lax.axis_index("x"), lax.axis_index("y"), lax.axis_index("c")


def allgather8(xs, name):
    m_per, n = xs.shape

    def body(x_ref, out_ref, send_sems, recv_sems, local_sem):
        x, y, c = _my_pos()
        me, sibling = (x, y, c), (x, y, 1 - c)
        chips = [(1 - x, y), (x, 1 - y), (1 - x, 1 - y)]

        def rows(px, py, pc):
            return out_ref.at[pl.ds((4 * px + 2 * py + pc) * m_per, m_per), :]

        def copy(k, block, to, src=None):
            return pltpu.make_async_remote_copy(
                src_ref=rows(*block) if src is None else src, dst_ref=rows(*block),
                send_sem=send_sems.at[k], recv_sem=recv_sems.at[k], device_id=to, device_id_type=MESH)

        mine = pltpu.make_async_copy(x_ref, rows(*me), local_sem)
        mine.start()
        first = [copy(0, me, sibling, src=x_ref)]
        first += [copy(1 + j, me, (*chip, c), src=x_ref) for j, chip in enumerate(chips)]
        for cp in first:
            cp.start()
        passed = [copy(4 + j, (*chip, c), sibling) for j, chip in enumerate(chips)]
        for j, chip in enumerate(chips):
            copy(1 + j, (*chip, c), me).wait_recv()
            passed[j].start()
        copy(0, sibling, me).wait_recv()
        for j, chip in enumerate(chips):
            copy(4 + j, (*chip, 1 - c), me).wait_recv()
        for cp in first + passed:
            cp.wait_send()
        mine.wait()

    return pl.pallas_call(
        body, name=name,
        out_shape=jax.ShapeDtypeStruct((N_DEV * m_per, n), xs.dtype),
        in_specs=[pl.BlockSpec(memory_space=pltpu.VMEM)],
        out_specs=pl.BlockSpec(memory_space=pltpu.VMEM),
        scratch_shapes=[pltpu.SemaphoreType.DMA((7,)), pltpu.SemaphoreType.DMA((7,)), pltpu.SemaphoreType.DMA],
    )(xs)


def _remote(src, dst, send_sems, recv_sems, k, to):
    return pltpu.make_async_remote_copy(src_ref=src, dst_ref=dst, send_sem=send_sems.at[k], recv_sem=recv_sems.at[k],
                                        device_id=to, device_id_type=MESH)


def _hbm_call(body, name, ins, out_shapes, n_sems, aliases=None):
    hbm = pl.BlockSpec(memory_space=pl.ANY)
    scratch = [pltpu.SemaphoreType.DMA((n_sems,)), pltpu.SemaphoreType.DMA((n_sems,))]
    return pl.pallas_call(body, name=name, out_shape=out_shapes, in_specs=[hbm] * len(ins),
                          out_specs=[hbm] * len(out_shapes), scratch_shapes=scratch,
                          input_output_aliases=aliases or {})(*ins)


def _layers_half(ref, h, axis=0):
    size = ref.shape[axis] // 2
    idx = (slice(None),) * axis + (pl.ds(h * size, size),)
    return ref.at[idx]


class GatherOverIci:
    def __init__(self, ws):
        self.ins = list(ws)
        self.outs = [jax.ShapeDtypeStruct((N_CHIPS,) + w.shape, w.dtype) for w in ws]
        self.aliases = {}
        self.n_sems = 3 * len(ws)

    def _copies(self, x_refs, o_refs, send_sems, recv_sems):
        x, y, c = _my_pos()
        s = 2 * x + y
        out = []
        for a in range(len(x_refs)):
            for k, (tx, ty) in enumerate([(1 - x, y), (x, 1 - y), (1 - x, 1 - y)]):
                mine = _remote(_layers_half(x_refs[a], c, 1), _layers_half(o_refs[a].at[s], c, 1), send_sems, recv_sems,
                               3 * a + k, (tx, ty, c))
                got = _layers_half(o_refs[a].at[2 * tx + ty], c, 1)
                out.append((mine, _remote(got, got, send_sems, recv_sems, 3 * a + k, (tx, ty, c))))
        return out

    def start(self, x_refs, o_refs, send_sems, recv_sems):
        for mine, _ in self._copies(x_refs, o_refs, send_sems, recv_sems):
            mine.start()

    def wait(self, x_refs, o_refs, send_sems, recv_sems):
        for mine, theirs in self._copies(x_refs, o_refs, send_sems, recv_sems):
            theirs.wait_recv()
            mine.wait_send()


class GatherToSibling:
    def __init__(self, gathered):
        self.ins = list(gathered)
        self.outs = [jax.ShapeDtypeStruct(g.shape, g.dtype) for g in gathered]
        self.aliases = {a: a for a in range(len(gathered))}
        self.n_sems = 3 * len(gathered)

    def _copies(self, o_refs, send_sems, recv_sems):
        x, y, c = _my_pos()
        out = []
        for a in range(len(o_refs)):
            for k, (tx, ty) in enumerate([(1 - x, y), (x, 1 - y), (1 - x, 1 - y)]):
                got = _layers_half(o_refs[a].at[2 * tx + ty], c, 1)
                theirs = _layers_half(o_refs[a].at[2 * tx + ty], 1 - c, 1)
                out.append((_remote(got, got, send_sems, recv_sems, 3 * a + k, (x, y, 1 - c)),
                            _remote(theirs, theirs, send_sems, recv_sems, 3 * a + k, (x, y, 1 - c))))
        return out

    def start(self, x_refs, o_refs, send_sems, recv_sems):
        for mine, _ in self._copies(o_refs, send_sems, recv_sems):
            mine.start()

    def wait(self, x_refs, o_refs, send_sems, recv_sems):
        for mine, theirs in self._copies(o_refs, send_sems, recv_sems):
            theirs.wait_recv()
            mine.wait_send()


def run_side(side, name):
    n_in, n_out = len(side.ins), len(side.outs)

    def body(*refs):
        parts = refs[:n_in], refs[n_in:n_in + n_out], refs[n_in + n_out], refs[n_in + n_out + 1]
        side.start(*parts)
        side.wait(*parts)

    return _hbm_call(body, name, side.ins, side.outs, side.n_sems, aliases=side.aliases)


def weights_gather(ws, name):
    return run_side(GatherToSibling(run_side(GatherOverIci(ws), name + "_ici")), name + "_sibling")


def halves_to_sibling(gs, name):
    n = len(gs)

    def body(*refs):
        x_refs, o_refs, send_sems, recv_sems = refs[:n], refs[n:2 * n], refs[2 * n], refs[2 * n + 1]
        x, y, c = _my_pos()
        cps = [_remote(_layers_half(x_refs[a], 1 - c, axis=1), o_refs[a], send_sems, recv_sems, a, (x, y, 1 - c))
               for a in range(n)]
        for cp in cps:
            cp.start()
        for cp in cps:
            cp.wait()

    outs = [jax.ShapeDtypeStruct((g.shape[0], g.shape[1] // 2) + g.shape[2:], g.dtype) for g in gs]
    return _hbm_call(body, name, gs, outs, n)


class ExchangeOverIci:
    def __init__(self, xs):
        self.ins = list(xs)
        self.outs = [jax.ShapeDtypeStruct(v.shape, v.dtype) for v in xs]
        self.aliases = {}
        self.n_sems = 3 * len(xs)

    def _copies(self, x_refs, o_refs, send_sems, recv_sems):
        x, y, c = _my_pos()
        s = 2 * x + y
        out = []
        for a in range(len(x_refs)):
            for k, (tx, ty) in enumerate([(1 - x, y), (x, 1 - y), (1 - x, 1 - y)]):
                got = o_refs[a].at[2 * tx + ty]
                out.append((_remote(x_refs[a].at[2 * tx + ty], o_refs[a].at[s], send_sems, recv_sems, 3 * a + k, (tx, ty, c)),
                            _remote(got, got, send_sems, recv_sems, 3 * a + k, (tx, ty, c))))
        return out

    def start(self, x_refs, o_refs, send_sems, recv_sems):
        for mine, _ in self._copies(x_refs, o_refs, send_sems, recv_sems):
            mine.start()

    def wait(self, x_refs, o_refs, send_sems, recv_sems):
        for mine, theirs in self._copies(x_refs, o_refs, send_sems, recv_sems):
            theirs.wait_recv()
            mine.wait_send()


def halves_gather(xs, name):
    n = len(xs)

    def body(*refs):
        x_refs, o_refs, send_sems, recv_sems = refs[:n], refs[n:2 * n], refs[2 * n], refs[2 * n + 1]
        x, y, c = _my_pos()
        sends = []
        for a in range(n):
            cp = _remote(_layers_half(x_refs[a], c), _layers_half(o_refs[a], c), send_sems, recv_sems, a, (x, y, 1 - c))
            cp.start()
            sends.append(cp)
        for a in range(n):
            theirs = _layers_half(o_refs[a], 1 - c)
            _remote(theirs, theirs, send_sems, recv_sems, a, (x, y, 1 - c)).wait_recv()
        for cp in sends:
            cp.wait_send()

    outs = [jax.ShapeDtypeStruct(v.shape, v.dtype) for v in xs]
    return _hbm_call(body, name, xs, outs, n, aliases={a: a for a in range(n)})


def sum_leading(xs, name):
    n, r, c = xs.shape

    def body(x_ref, o_ref):
        acc = x_ref[0]
        for i in range(1, n):
            acc = acc + x_ref[i]
        o_ref[...] = acc

    return pl.pallas_call(body, name=name, out_shape=jax.ShapeDtypeStruct((r, c), xs.dtype))(xs)


def add_cast(a, b, out_dtype, name):
    n, r, c = a.shape
    tr = _pick(r, (512, 256, 128, 64, 32, 16))

    def body(a_ref, b_ref, o_ref):
        o_ref[...] = (a_ref[...].astype(F32) + b_ref[...].astype(F32)).astype(out_dtype)

    spec = pl.BlockSpec((1, tr, c), lambda i, j: (i, j, 0))
    return pl.pallas_call(body, name=name, grid=(n, r // tr), in_specs=[spec, spec], out_specs=spec,
                          out_shape=jax.ShapeDtypeStruct((n, r, c), out_dtype),
                          compiler_params=_params(("parallel", "parallel")))(a, b)


def sum_chips(parts, own, where, name):
    n, nl, r, c = parts.shape
    tr = _pick(r, (512, 256, 128, 64, 32, 16))

    def body(w_ref, p_ref, o_ref, out_ref):
        s = w_ref[0]
        acc = jnp.zeros(out_ref.shape, F32)
        for t in range(n):
            acc = acc + jnp.where(s == t, o_ref[0], p_ref[t]).astype(F32)
        out_ref[...] = acc

    grid_spec = pltpu.PrefetchScalarGridSpec(
        num_scalar_prefetch=1, grid=(nl, r // tr),
        in_specs=[pl.BlockSpec((n, 1, tr, c), lambda i, j, w: (0, i, j, 0)),
                  pl.BlockSpec((1, 1, tr, c), lambda i, j, w: (w[0], i, j, 0))],
        out_specs=pl.BlockSpec((1, tr, c), lambda i, j, w: (w[1] * nl + i, j, 0)))
    return pl.pallas_call(body, name=name, grid_spec=grid_spec, out_shape=jax.ShapeDtypeStruct((2 * nl, r, c), F32),
                          compiler_params=_params(("parallel", "parallel")))(where, parts, own)


def ada_forward(c_all, w_ada):
    nl, d, n = w_ada.shape
    nb = c_all.shape[0]

    def body(c_ref, w_ref, o_ref):
        cv = c_ref[...]
        ca = (cv * _sigmoid(cv)).astype(BF16)
        o_ref[0] = _nn(ca, w_ref[0].astype(BF16))

    return pl.pallas_call(
        body, name="ada_forward", grid=(nl,),
        in_specs=[pl.BlockSpec((nb, d), lambda l: (0, 0)), pl.BlockSpec((1, d, n), lambda l: (l, 0, 0))],
        out_specs=pl.BlockSpec((1, nb, n), lambda l: (l, 0, 0)),
        out_shape=jax.ShapeDtypeStruct((nl, nb, n), F32), compiler_params=_params(("parallel",)))(c_all, w_ada)


def ada_backward(c_all, dmod):
    nl, nb, n = dmod.shape
    d = c_all.shape[1]

    def body(c_ref, g_ref, o_ref):
        cv = c_ref[...]
        ca = (cv * _sigmoid(cv)).astype(BF16)
        o_ref[0] = _tn(ca, g_ref[0].astype(BF16))

    return pl.pallas_call(
        body, name="ada_backward", grid=(nl,),
        in_specs=[pl.BlockSpec((nb, d), lambda l: (0, 0)), pl.BlockSpec((1, nb, n), lambda l: (l, 0, 0))],
        out_specs=pl.BlockSpec((1, d, n), lambda l: (l, 0, 0)),
        out_shape=jax.ShapeDtypeStruct((nl, d, n), F32), compiler_params=_params(("parallel",)))(c_all, dmod)


def matmul_tn(a, b, name):
    k, m = a.shape
    n = b.shape[1]
    tm, tk = _pick(m, (512, 256, 128)), _pick(k, (512, 256, 128))
    tn = n if n * (tm * 8 + tk * 4) <= VMEM_LIMIT // 2 else _pick(n, (512, 256, 128))

    def body(a_ref, b_ref, o_ref, acc_ref):
        @pl.when(pl.program_id(2) == 0)
        def _():
            acc_ref[...] = jnp.zeros_like(acc_ref)

        acc_ref[...] += _tn(a_ref[...], b_ref[...])

        @pl.when(pl.program_id(2) == k // tk - 1)
        def _():
            o_ref[...] = acc_ref[...].astype(BF16)

    return pl.pallas_call(
        body, name=name, grid=(m // tm, n // tn, k // tk),
        in_specs=[pl.BlockSpec((tk, tm), lambda i, j, kk: (kk, i)), pl.BlockSpec((tk, tn), lambda i, j, kk: (kk, j))],
        out_specs=pl.BlockSpec((tm, tn), lambda i, j, kk: (i, j)),
        out_shape=jax.ShapeDtypeStruct((m, n), BF16), scratch_shapes=[pltpu.VMEM((tm, tn), F32)],
        compiler_params=_params(("parallel", "parallel", "arbitrary")))(a, b)


def adamw(w, g, m, v, name):
    nl, r, c = w.shape
    tr = _pick(r, (512, 256, 128, 64, 32, 16, 8))

    def body(w_ref, g_ref, m_ref, v_ref, d_ref, mo_ref, vo_ref):
        gv = g_ref[...]
        mn = ADAM_B1 * m_ref[...] + (1.0 - ADAM_B1) * gv
        vn = ADAM_B2 * v_ref[...] + (1.0 - ADAM_B2) * (gv * gv)
        m_hat = mn / (1.0 - ADAM_B1 ** ADAM_STEP)
        v_hat = vn / (1.0 - ADAM_B2 ** ADAM_STEP)
        d_ref[...] = -ADAM_LR * (m_hat / (jnp.sqrt(v_hat) + ADAM_EPS) + ADAM_WD * w_ref[...])
        mo_ref[...] = mn
        vo_ref[...] = vn

    spec = pl.BlockSpec((1, tr, c), lambda l, i: (l, i, 0))
    out = jax.ShapeDtypeStruct((nl, r, c), F32)
    return pl.pallas_call(body, name=name, grid=(nl, r // tr), in_specs=[spec] * 4, out_specs=[spec] * 3,
                          out_shape=[out] * 3, compiler_params=_params(("parallel", "parallel")))(w, g, m, v)


def _rope(t, cos_t, sin_t):
    w = t.shape[1]
    lane = _lane(t.shape) & (HEAD_PAD - 1)
    first_half = (lane >= ROPE_LO) & (lane < ROPE_LO + ROPE_HALF)
    partner = jnp.where(first_half, pltpu.roll(t, w - ROPE_HALF, 1), pltpu.roll(t, ROPE_HALF, 1))
    return t * cos_t + partner * sin_t


def _rope_t(dt, cos_t, sin_t):
    w = dt.shape[1]
    lane = _lane(dt.shape) & (HEAD_PAD - 1)
    first_half = (lane >= ROPE_LO) & (lane < ROPE_LO + ROPE_HALF)
    ds = dt * sin_t
    partner = jnp.where(first_half, pltpu.roll(ds, w - ROPE_HALF, 1), pltpu.roll(ds, ROPE_HALF, 1))
    return dt * cos_t + partner


def _rms(xv, g):
    rstd = lax.rsqrt(jnp.mean(xv * xv, axis=-1, keepdims=True) + EPS)
    xh = xv * rstd
    return xh * g, xh, rstd


def _rms_bwd(dy, g, xh, rstd):
    dxh = dy * g
    dx = rstd * (dxh - xh * jnp.mean(dxh * xh, axis=-1, keepdims=True))
    return dx, jnp.sum(dy * xh, axis=0, keepdims=True)


def make_step(cfg):
    S, D, NZ = cfg.S, cfg.D, cfg.NZ
    FW, MW, QL, KVL, FH, MH = cfg.FW, cfg.MW, cfg.QL, cfg.KVL, cfg.FH, cfg.MH
    QW = MH * HEAD_PAD
    TM = _pick(S, (512, 256, 128))
    TQ = TK = _pick(S, (256, 128))
    n_tok = S // TM
    ZO = 3 * FW
    NZR = NZ - ZO
    misc_blk = (cfg.o_ms - ZO) // LANES
    FWD_UNROLL = 4
    FOX_SCALE = HEAD_DIM ** -0.5
    MLA_SCALE = (HEAD_DIM + ROPE_DIM) ** -0.5

    def tok(width, col=0):
        return pl.BlockSpec((TM, width), lambda i: (i, col))

    def const(shape):
        return pl.BlockSpec(shape, lambda i: tuple(0 for _ in shape))

    def layer(l, shape):
        return pl.BlockSpec((None,) + shape, lambda i: (l,) + tuple(0 for _ in shape))


    def ln_inproj(x, g, scale, shift, w_in, l):
        def body(x_ref, g_ref, sc_ref, sh_ref, w_ref, h_ref, z_ref, qkv_ref):
            y, _, _ = _rms(x_ref[...], g_ref[...])
            hb = (y * (1.0 + sc_ref[...]) + sh_ref[...]).astype(BF16)
            h_ref[...] = hb
            z = _nn(hb, w_ref[...])
            z_ref[...] = z[:, ZO:]
            qkv_ref[:, :FW] = (z[:, :FW] * FOX_SCALE).astype(BF16)
            qkv_ref[:, FW:] = z[:, FW:ZO].astype(BF16)

        return pl.pallas_call(
            body, name="ln_inproj", grid=(n_tok,),
            in_specs=[tok(D), const((1, D)), const((1, D)), const((1, D)), layer(l, (D, NZ))],
            out_specs=[tok(D), tok(NZR), tok(ZO)],
            out_shape=[jax.ShapeDtypeStruct((S, D), BF16), jax.ShapeDtypeStruct((S, NZR), F32),
                       jax.ShapeDtypeStruct((S, ZO), BF16)],
            compiler_params=_params(("parallel",)))(x, g, scale, shift, w_in)

    def _log_f_terms(misc, bf):
        lane = _lane(misc.shape)
        a = misc + bf
        e = jnp.exp(-jnp.abs(a))
        logf = jnp.minimum(a, 0.0) - jnp.log(1.0 + e)
        sig_neg = jnp.where(a >= 0, e, 1.0) / (1.0 + e)
        valid = lane < FH
        return jnp.where(valid, logf, 0.0), jnp.where(valid, sig_neg, 0.0)

    def fox_prep(z, bf_pad):
        def body(z_ref, b_ref, o_ref, t_ref, carry):
            @pl.when(pl.program_id(0) == 0)
            def _():
                carry[...] = jnp.zeros_like(carry)

            logf, _ = _log_f_terms(z_ref[...], b_ref[...])
            row = lax.broadcasted_iota(jnp.int32, (TM, TM), 0)
            col = lax.broadcasted_iota(jnp.int32, (TM, TM), 1)
            tri = (col <= row).astype(F32)
            cum = jnp.dot(tri, logf, precision=lax.Precision.HIGHEST, preferred_element_type=F32) + carry[...]
            o_ref[...] = cum
            carry[...] = cum[TM - 1:TM, :]
            for hd in range(FH):
                t_ref[hd // 2, hd % 2] = jnp.broadcast_to(cum[:, hd:hd + 1], (TM, LANES))

        return pl.pallas_call(
            body, name="fox_prep", grid=(n_tok,),
            in_specs=[tok(LANES, misc_blk), const((1, LANES))],
            out_specs=[tok(LANES), pl.BlockSpec((FH // 2, 2, TM, LANES), lambda i: (0, 0, i, 0))],
            out_shape=[jax.ShapeDtypeStruct((S, LANES), F32), jax.ShapeDtypeStruct((FH // 2, 2, S, LANES), F32)],
            scratch_shapes=[pltpu.VMEM((1, LANES), F32)],
            compiler_params=_params(("arbitrary",)))(z, bf_pad)

    def mla_prep(z, gq, gkv, w_uq, w_uk, w_v, l, cos_t, sin_t):
        def body(ql_ref, kvl_ref, ms_ref, gq_ref, gkv_ref, wq_ref, wk_ref, wv_ref, c_ref, s_ref, q_ref, k_ref, v_ref):
            cos1, sin1 = c_ref[...], s_ref[...]
            qn, _, _ = _rms(ql_ref[...], gq_ref[...])
            q = _nn(qn.astype(BF16), wq_ref[...])
            q_ref[...] = (_rope(q, jnp.tile(cos1, (1, MH)), jnp.tile(sin1, (1, MH))) * MLA_SCALE).astype(BF16)
            kvn, _, _ = _rms(kvl_ref[...], gkv_ref[...])
            kvb = kvn.astype(BF16)
            lane = _lane((TM, LANES))
            kr = jnp.where((lane >= ROPE_LO) & (lane < ROPE_LO + ROPE_DIM), ms_ref[...], 0.0)
            kr = _rope(kr, cos1, sin1)
            k_ref[...] = (_nn(kvb, wk_ref[...]) + jnp.tile(kr, (1, MH))).astype(BF16)
            v_ref[...] = _nn(kvb, wv_ref[...]).astype(BF16)

        return pl.pallas_call(
            body, name="mla_prep", grid=(n_tok,),
            in_specs=[tok(QL, (cfg.o_ql - ZO) // QL), tok(KVL, (cfg.o_kv - ZO) // KVL), tok(LANES, misc_blk),
                      const((1, QL)), const((1, KVL)), layer(l, (QL, QW)), layer(l, (KVL, QW)), layer(l, (KVL, MW)),
                      tok(LANES), tok(LANES)],
            out_specs=[tok(QW), tok(QW), tok(MW)],
            out_shape=[jax.ShapeDtypeStruct((S, QW), BF16), jax.ShapeDtypeStruct((S, QW), BF16),
                       jax.ShapeDtypeStruct((S, MW), BF16)],
            compiler_params=_params(("parallel",)))(z, z, z, gq, gkv, w_uq, w_uk, w_v, cos_t, sin_t)

    def _allowed(q0, k0, chunked):
        qi = q0 + lax.broadcasted_iota(jnp.int32, (TQ, TK), 0)
        ki = k0 + lax.broadcasted_iota(jnp.int32, (TQ, TK), 1)
        if chunked:
            return (ki >> CHUNK_SHIFT) <= (qi >> CHUNK_SHIFT)
        return ki <= qi

    def _heads(val, packed):
        if packed:
            lane = _lane(val.shape)
            zero = jnp.zeros_like(val)
            return [jnp.where(lane < HEAD_DIM, val, zero), jnp.where(lane >= HEAD_DIM, val, zero)]
        return [val[:, :HEAD_PAD], val[:, HEAD_PAD:]]

    def _merge(a0, a1):
        return jnp.where(_lane(a0.shape) < HEAD_DIM, a0, a1)

    def attn_fwd(q, k, v, q_blk0, k_blk0, v_blk0, cumt, packed, chunked, name, side=None):
        has_bias = cumt is not None
        n_pairs = (FH if packed else MH) // 2
        wq = LANES if packed else 2 * HEAD_PAD
        assert TQ == TK

        n_main = 4 if has_bias else 3
        n_side_in, n_side_out = (len(side.ins), len(side.outs)) if side else (0, 0)

        def body(*refs):
            if has_bias:
                q_ref, k_ref, v_ref, ct_ref = refs[:n_main]
            else:
                q_ref, k_ref, v_ref = refs[:n_main]
            o_ref, lse_ref = refs[n_main + n_side_in:n_main + n_side_in + 2]
            i = pl.program_id(1)
            if side:
                side_refs = (refs[n_main:n_main + n_side_in], refs[n_main + n_side_in + 2:n_main + n_side_in + 2 + n_side_out],
                             refs[-2], refs[-1])

                @pl.when((pl.program_id(0) == 0) & (i == 0))
                def _():
                    side.start(*side_refs)

            q0 = i * TQ
            qh = _heads(q_ref[...], packed)
            q_both = jnp.concatenate(qh, axis=0) if packed else None

            def scores(kb):
                kblk = k_ref[pl.ds(pl.multiple_of(kb * TK, TK), TK), :]
                if packed:
                    return _nt(kblk, q_both)
                return jnp.concatenate([_nt(kblk[:, :HEAD_PAD], qh[0]), _nt(kblk[:, HEAD_PAD:], qh[1])], axis=1)

            def update(kb, s, carry, masked):
                m, l, acc = carry
                k0 = pl.multiple_of(kb * TK, TK)
                s0, s1 = s[:, :TQ], s[:, TQ:]
                if has_bias:
                    ck = ct_ref[0, :, pl.ds(k0, TK), :]
                    s0, s1 = s0 - jnp.tile(ck[0], (1, TQ // LANES)), s1 - jnp.tile(ck[1], (1, TQ // LANES))
                if masked:
                    allow = jnp.transpose(_allowed(q0, k0, chunked))
                    s0, s1 = jnp.where(allow, s0, NEG), jnp.where(allow, s1, NEG)
                s = jnp.concatenate([s0, s1], axis=1)
                m_new = jnp.maximum(m, jnp.max(s, axis=0, keepdims=True))
                p = jnp.exp(s - m_new)
                alpha = jnp.exp(m - m_new)
                l = alpha * l + jnp.sum(p, axis=0, keepdims=True)
                return m_new, l, alpha * acc + _tn(v_ref[pl.ds(k0, TK), :], p.astype(BF16))

            def several(kb, carry, n, last_masked):
                ss = [scores(kb + u) for u in range(n)]
                for u in range(n):
                    carry = update(kb + u, ss[u], carry, last_masked and u == n - 1)
                return carry

            init = (jnp.full((1, 2 * TQ), NEG, F32), jnp.zeros((1, 2 * TQ), F32), jnp.zeros((LANES, 2 * TQ), F32))
            carry = lax.fori_loop(0, i // FWD_UNROLL, lambda t, cr: several(FWD_UNROLL * t, cr, FWD_UNROLL, False), init)
            m, l, acc = lax.switch(i % FWD_UNROLL,
                                   [lambda cr, r=r: several(i - r, cr, r + 1, True) for r in range(FWD_UNROLL)], carry)
            o = acc / l
            lse = m + jnp.log(l)
            o_ref[...] = jnp.transpose(jnp.concatenate([o[:HEAD_DIM, :TQ], o[HEAD_DIM:, TQ:]], axis=0))
            sub = lax.broadcasted_iota(jnp.int32, (LANES, TQ), 0)
            lse_ref[...] = jnp.transpose(jnp.where(sub == 0, lse[:, :TQ], jnp.where(sub == 1, lse[:, TQ:], 0.0)))
            if side:
                @pl.when((pl.program_id(0) == n_pairs - 1) & (i == S // TQ - 1))
                def _():
                    side.wait(*side_refs)

        in_specs = [pl.BlockSpec((TQ, wq), lambda p, i: (i, q_blk0 + p)),
                    pl.BlockSpec((S, wq), lambda p, i: (0, k_blk0 + p)),
                    pl.BlockSpec((S, LANES), lambda p, i: (0, v_blk0 + p))]
        args = [q, k, v]
        if has_bias:
            in_specs += [pl.BlockSpec((1, 2, S, LANES), lambda p, i: (p, 0, 0, 0))]
            args += [cumt]
        out_specs = [pl.BlockSpec((TQ, LANES), lambda p, i: (i, p)), pl.BlockSpec((TQ, LANES), lambda p, i: (i, p))]
        out_shape = [jax.ShapeDtypeStruct((S, n_pairs * LANES), F32), jax.ShapeDtypeStruct((S, n_pairs * LANES), F32)]
        extra = {}
        if side:
            hbm = pl.BlockSpec(memory_space=pl.ANY)
            in_specs += [hbm] * n_side_in
            args += side.ins
            out_specs += [hbm] * n_side_out
            out_shape += side.outs
            extra = dict(scratch_shapes=[pltpu.SemaphoreType.DMA((side.n_sems,)), pltpu.SemaphoreType.DMA((side.n_sems,))],
                         input_output_aliases={n_main + a: 2 + b for a, b in side.aliases.items()})
        outs = pl.pallas_call(
            body, name=name, grid=(n_pairs, S // TQ), in_specs=in_specs, out_specs=out_specs, out_shape=out_shape,
            compiler_params=_params(("arbitrary", "arbitrary")), **extra)(*args)
        return (outs[0], outs[1], list(outs[2:])) if side else outs

    def gate_outproj(of, om, z, w_out, l, x, gate):
        def body(of_ref, om_ref, fg_ref, mg_ref, w_ref, x_ref, gt_ref, xn_ref, u_ref, y_ref):
            fg, mg = fg_ref[...], mg_ref[...]
            u = jnp.concatenate([of_ref[...] * fg * _sigmoid(fg), om_ref[...] * mg * _sigmoid(mg)], axis=1).astype(BF16)
            y = _nn(u, w_ref[...])
            u_ref[...] = u
            y_ref[...] = y.astype(BF16)
            xn_ref[...] = x_ref[...] + gt_ref[...] * y

        return pl.pallas_call(
            body, name="gate_outproj", grid=(n_tok,),
            in_specs=[tok(FW), tok(MW), tok(FW, (cfg.o_fg - ZO) // FW), tok(MW, (cfg.o_mg - ZO) // MW),
                      layer(l, (FW + MW, D)), tok(D), const((1, D))],
            out_specs=[tok(D), tok(FW + MW), tok(D)],
            out_shape=[jax.ShapeDtypeStruct((S, D), F32), jax.ShapeDtypeStruct((S, FW + MW), BF16),
                       jax.ShapeDtypeStruct((S, D), BF16)],
            compiler_params=_params(("parallel",)))(of, om, z, z, w_out, x, gate)

    def final_loss(x, g, target):
        def body(x_ref, g_ref, t_ref, dx_ref, acc_ref, loss_ref):
            @pl.when(pl.program_id(0) == 0)
            def _():
                acc_ref[...] = jnp.zeros_like(acc_ref)
                loss_ref[...] = jnp.zeros_like(loss_ref)

            gv = g_ref[...]
            y, xh, rstd = _rms(x_ref[...], gv)
            e = y - t_ref[...]
            loss_ref[...] += 0.5 * jnp.sum(jnp.sum(e * e, axis=-1, keepdims=True) / D, axis=0, keepdims=True)
            dx, dg = _rms_bwd(e / D, gv, xh, rstd)
            dx_ref[...] = dx
            acc_ref[0:1, :] += dg

        return pl.pallas_call(
            body, name="final_loss", grid=(n_tok,),
            in_specs=[tok(D), const((1, D)), tok(D)],
            out_specs=[tok(D), const((8, D)), const((1, LANES))],
            out_shape=[jax.ShapeDtypeStruct((S, D), F32), jax.ShapeDtypeStruct((8, D), F32),
                       jax.ShapeDtypeStruct((1, LANES), F32)],
            compiler_params=_params(("arbitrary",)))(x, g, target)


    def bwd_out(dxn, gate, y, w_out, l, of, om, z, lse_f, lse_m):
        def body(dx_ref, gt_ref, y_ref, w_ref, of_ref, om_ref, fg_ref, mg_ref, lf_ref, lm_ref,
                 dof_ref, dom_ref, dfg_ref, dmg_ref, dy_ref, acc_ref, sf_ref, sm_ref):
            @pl.when(pl.program_id(0) == 0)
            def _():
                acc_ref[...] = jnp.zeros_like(acc_ref)

            dxv = dx_ref[...]
            acc_ref[0:1, :] += jnp.sum(dxv * y_ref[...].astype(F32), axis=0, keepdims=True)
            dy = (gt_ref[...] * dxv).astype(BF16)
            dy_ref[...] = dy
            du = _nt(dy, w_ref[...])
            lane = _lane((TM, LANES))
            for lo, width, o_ref, g_ref, do_ref, dg_ref, l_ref, st_ref in (
                    (0, FW, of_ref, fg_ref, dof_ref, dfg_ref, lf_ref, sf_ref),
                    (FW, MW, om_ref, mg_ref, dom_ref, dmg_ref, lm_ref, sm_ref)):
                gv, ov = g_ref[...], o_ref[...]
                sg = _sigmoid(gv)
                dup = du[:, lo:lo + width]
                dob = (dup * gv * sg).astype(BF16)
                do_ref[...] = dob
                dg_ref[...] = (dup * ov * sg * (1.0 + gv * (1.0 - sg))).astype(BF16)
                d = dob.astype(F32) * ov
                for pr in range(width // LANES):
                    cols = slice(pr * LANES, (pr + 1) * LANES)
                    dp = d[:, cols]
                    d0 = jnp.sum(jnp.where(lane < HEAD_DIM, dp, 0.0), axis=-1, keepdims=True)
                    d1 = jnp.sum(jnp.where(lane >= HEAD_DIM, dp, 0.0), axis=-1, keepdims=True)
                    st_ref[:, cols] = jnp.where(lane == 2, d0, jnp.where(lane == 3, d1, l_ref[:, cols]))

        return pl.pallas_call(
            body, name="bwd_out", grid=(n_tok,),
            in_specs=[tok(D), const((1, D)), tok(D), layer(l, (FW + MW, D)), tok(FW), tok(MW),
                      tok(FW, (cfg.o_fg - ZO) // FW), tok(MW, (cfg.o_mg - ZO) // MW), tok(FW), tok(MW)],
            out_specs=[tok(FW), tok(MW), tok(FW), tok(MW), tok(D), const((8, D)), tok(FW), tok(MW)],
            out_shape=[jax.ShapeDtypeStruct((S, FW), BF16), jax.ShapeDtypeStruct((S, MW), BF16),
                       jax.ShapeDtypeStruct((S, FW), BF16), jax.ShapeDtypeStruct((S, MW), BF16),
                       jax.ShapeDtypeStruct((S, D), BF16), jax.ShapeDtypeStruct((8, D), F32),
                       jax.ShapeDtypeStruct((S, FW), F32), jax.ShapeDtypeStruct((S, MW), F32)],
            compiler_params=_params(("arbitrary",)))(dxn, gate, y, w_out, of, om, z, z, lse_f, lse_m)

    def attn_bwd(q, k, v, do, stats, q_blk0, k_blk0, v_blk0, cumt, packed, chunked, name, side=None):
        has_bias = cumt is not None
        n_pairs = (FH if packed else MH) // 2
        wq = LANES if packed else 2 * HEAD_PAD
        n_q = S // TQ
        assert TQ == TK
        n_in, n_out = (6, 5) if has_bias else (5, 3)
        n_side_in, n_side_out = (len(side.ins), len(side.outs)) if side else (0, 0)

        def body(*refs):
            main_out = refs[n_in + n_side_in:n_in + n_side_in + n_out]
            if has_bias:
                q_ref, k_ref, v_ref, do_ref, st_ref, ct_ref = refs[:n_in]
                dq_ref, dk_ref, dv_ref, dc_ref, dr_ref = main_out
            else:
                q_ref, k_ref, v_ref, do_ref, st_ref = refs[:n_in]
                dq_ref, dk_ref, dv_ref = main_out
            jb = pl.program_id(1)
            k0 = jb * TK
            if side:
                side_refs = (refs[n_in:n_in + n_side_in], refs[n_in + n_side_in + n_out:n_in + n_side_in + n_out + n_side_out],
                             refs[-2], refs[-1])

                @pl.when((pl.program_id(0) == 0) & (jb == 0))
                def _():
                    side.start(*side_refs)

            @pl.when(jb == 0)
            def _():
                dq_ref[...] = jnp.zeros_like(dq_ref)
                if has_bias:
                    dr_ref[...] = jnp.zeros_like(dr_ref)

            dk_ref[...] = jnp.zeros_like(dk_ref)
            dv_ref[...] = jnp.zeros_like(dv_ref)
            kh = _heads(k_ref[...], packed)
            k_both = jnp.concatenate(kh, axis=0) if packed else None
            v_both = jnp.concatenate(_heads(v_ref[...], True), axis=0)
            if has_bias:
                ct = jnp.transpose(ct_ref[0, 0])
                ck = jnp.concatenate([ct[:, 0:1], ct[:, 1:2]], axis=0)

            def products(ib):
                rows = pl.ds(pl.multiple_of(ib * TQ, TQ), TQ)
                q2, do2 = q_ref[rows, :], do_ref[rows, :]
                if packed:
                    s = _nt(k_both, q2)
                else:
                    qh = _heads(q2, False)
                    s = jnp.concatenate([_nt(kh[0], qh[0]), _nt(kh[1], qh[1])], axis=0)
                return s, _nt(v_both, do2)

            def update(ib, s, dp, carry, masked):
                q0 = pl.multiple_of(ib * TQ, TQ)
                rows = pl.ds(q0, TQ)
                q2, do2 = q_ref[rows, :], do_ref[rows, :]
                st = jnp.transpose(st_ref[rows, :])
                if not packed:
                    qh = _heads(q2, False)
                if has_bias:
                    s = s - ck
                if masked:
                    allow = jnp.transpose(_allowed(q0, k0, chunked))
                    s = jnp.where(jnp.concatenate([allow, allow], axis=0), s, NEG)
                p = jnp.concatenate([jnp.exp(s[:TK] - st[0:1, :]), jnp.exp(s[TK:] - st[1:2, :])], axis=0)
                dv2 = _nn(p.astype(BF16), do2)
                ds = jnp.concatenate([p[:TK] * (dp[:TK] - st[2:3, :]), p[TK:] * (dp[TK:] - st[3:4, :])], axis=0)
                dsb = ds.astype(BF16)
                dv_ref[...] += _merge(dv2[:TK], dv2[TK:])
                if packed:
                    dk2 = _nn(dsb, q2)
                    dk_ref[...] += _merge(dk2[:TK], dk2[TK:])
                    dq_ref[rows, :] += _tn(dsb, k_both)
                else:
                    dk_ref[...] += jnp.concatenate([_nn(dsb[:TK], qh[0]), _nn(dsb[TK:], qh[1])], axis=1)
                    dq_ref[rows, :] += jnp.concatenate([_tn(dsb[:TK], kh[0]), _tn(dsb[TK:], kh[1])], axis=1)
                if has_bias:
                    sub = lax.broadcasted_iota(jnp.int32, (8, TQ), 0)
                    r0 = jnp.sum(ds[:TK], axis=0, keepdims=True)
                    r1 = jnp.sum(ds[TK:], axis=0, keepdims=True)
                    dr_ref[0, ib] += jnp.where(sub == 0, r0, jnp.where(sub == 1, r1, 0.0))
                    return carry - jnp.sum(ds, axis=1, keepdims=True)
                return carry

            def step(ib, carry, masked):
                s, dp = products(ib)
                return update(ib, s, dp, carry, masked)

            def two(ib, carry):
                s_a, dp_a = products(ib)
                s_b, dp_b = products(ib + 1)
                return update(ib + 1, s_b, dp_b, update(ib, s_a, dp_a, carry, False), False)

            dc = step(jb, jnp.zeros((2 * TK, 1), F32), True)
            n_rest = n_q - 1 - jb
            dc = lax.fori_loop(0, n_rest // 2, lambda t, cr: two(jb + 1 + 2 * t, cr), dc)
            dc = lax.cond(n_rest % 2 == 1, lambda cr: step(n_q - 1, cr, False), lambda cr: cr, dc)
            if has_bias:
                lane = _lane((TK, LANES))
                dc_ref[0] = jnp.where(lane == 0, dc[:TK], jnp.where(lane == 1, dc[TK:], 0.0))
            if side:
                @pl.when((pl.program_id(0) == n_pairs - 1) & (jb == S // TK - 1))
                def _():
                    side.wait(*side_refs)

        in_specs = [pl.BlockSpec((S, wq), lambda p, j: (0, q_blk0 + p)),
                    pl.BlockSpec((TK, wq), lambda p, j: (j, k_blk0 + p)),
                    pl.BlockSpec((TK, LANES), lambda p, j: (j, v_blk0 + p)),
                    pl.BlockSpec((S, LANES), lambda p, j: (0, p)),
                    pl.BlockSpec((S, LANES), lambda p, j: (0, p))]
        args = [q, k, v, do, stats]
        out_specs = [pl.BlockSpec((S, wq), lambda p, j: (0, p)),
                     pl.BlockSpec((TK, wq), lambda p, j: (j, p)),
                     pl.BlockSpec((TK, LANES), lambda p, j: (j, p))]
        out_shape = [jax.ShapeDtypeStruct((S, n_pairs * wq), F32), jax.ShapeDtypeStruct((S, n_pairs * wq), F32),
                     jax.ShapeDtypeStruct((S, n_pairs * LANES), F32)]
        if has_bias:
            in_specs += [pl.BlockSpec((1, 1, 8, TK), lambda p, j: (p, j, 0, 0))]
            args += [cumt]
            out_specs += [pl.BlockSpec((1, TK, LANES), lambda p, j: (p, j, 0)),
                          pl.BlockSpec((1, S // TQ, 8, TQ), lambda p, j: (p, 0, 0, 0))]
            out_shape += [jax.ShapeDtypeStruct((n_pairs, S, LANES), F32),
                          jax.ShapeDtypeStruct((n_pairs, S // TQ, 8, TQ), F32)]
        extra = {}
        if side:
            hbm = pl.BlockSpec(memory_space=pl.ANY)
            in_specs += [hbm] * n_side_in
            args += side.ins
            out_specs += [hbm] * n_side_out
            out_shape += side.outs
            extra = dict(scratch_shapes=[pltpu.SemaphoreType.DMA((side.n_sems,)), pltpu.SemaphoreType.DMA((side.n_sems,))],
                         input_output_aliases={n_in + a: n_out + b for a, b in side.aliases.items()})
        outs = pl.pallas_call(
            body, name=name, grid=(n_pairs, S // TK), in_specs=in_specs, out_specs=out_specs, out_shape=out_shape,
            compiler_params=_params(("arbitrary", "arbitrary")), **extra)(*args)
        return (*outs[:n_out], list(outs[n_out:])) if side else outs

    def fox_post(dcum, z, bf_pad):
        def rev(i):
            return n_tok - 1 - i

        def body(dc_ref, z_ref, b_ref, dff_ref, acc_ref, carry):
            @pl.when(pl.program_id(0) == 0)
            def _():
                carry[...] = jnp.zeros_like(carry)
                acc_ref[...] = jnp.zeros_like(acc_ref)

            _, sig_neg = _log_f_terms(z_ref[...], b_ref[...])
            row = lax.broadcasted_iota(jnp.int32, (TM, TM), 0)
            col = lax.broadcasted_iota(jnp.int32, (TM, TM), 1)
            tri = (col >= row).astype(F32)
            dlog = jnp.dot(tri, dc_ref[...], precision=lax.Precision.HIGHEST, preferred_element_type=F32) + carry[...]
            carry[...] = dlog[0:1, :]
            dff = dlog * sig_neg
            dff_ref[...] = dff
            acc_ref[0:1, :] += jnp.sum(dff, axis=0, keepdims=True)

        return pl.pallas_call(
            body, name="fox_post", grid=(n_tok,),
            in_specs=[pl.BlockSpec((TM, LANES), lambda i: (rev(i), 0)),
                      pl.BlockSpec((TM, LANES), lambda i: (rev(i), misc_blk)), const((1, LANES))],
            out_specs=[pl.BlockSpec((TM, LANES), lambda i: (rev(i), 0)), const((8, LANES))],
            out_shape=[jax.ShapeDtypeStruct((S, LANES), F32), jax.ShapeDtypeStruct((8, LANES), F32)],
            scratch_shapes=[pltpu.VMEM((1, LANES), F32)],
            compiler_params=_params(("arbitrary",)))(dcum, z, bf_pad)

    def mla_post(dq, dk, dv, dff, z, gq, gkv, w_uq, w_uk, w_v, l, cos_t, sin_t):
        def body(dq_ref, dk_ref, dv_ref, dff_ref, ql_ref, kvl_ref, gq_ref, gkv_ref, wq_ref, wk_ref, wv_ref,
                 c_ref, s_ref, zq_ref, zkv_ref, zms_ref, dwq_ref, dwk_ref, dwv_ref, dgq_ref, dgkv_ref):
            @pl.when(pl.program_id(0) == 0)
            def _():
                for r in (dwq_ref, dwk_ref, dwv_ref, dgq_ref, dgkv_ref):
                    r[...] = jnp.zeros_like(r)

            cos1, sin1 = c_ref[...], s_ref[...]
            gqv, gkvv = gq_ref[...], gkv_ref[...]
            qn, qxh, qrstd = _rms(ql_ref[...], gqv)
            dq_pre = _rope_t(dq_ref[...] * MLA_SCALE, jnp.tile(cos1, (1, MH)), jnp.tile(sin1, (1, MH))).astype(BF16)
            dwq_ref[...] += _tn(qn.astype(BF16), dq_pre)
            dql, dgq = _rms_bwd(_nt(dq_pre, wq_ref[...]), gqv, qxh, qrstd)
            zq_ref[...] = dql.astype(BF16)
            dgq_ref[0:1, :] += dgq

            dkv = dk_ref[...]
            lane = _lane(dkv.shape) & (HEAD_PAD - 1)
            dkn = jnp.where(lane < HEAD_DIM, dkv, 0.0).astype(BF16)
            dkr = dkv[:, 0:HEAD_PAD]
            for hd in range(1, MH):
                dkr = dkr + dkv[:, hd * HEAD_PAD:(hd + 1) * HEAD_PAD]
            lane1 = _lane(dkr.shape)
            dkr = jnp.where((lane1 >= ROPE_LO) & (lane1 < ROPE_LO + ROPE_DIM), dkr, 0.0)
            dkr = _rope_t(dkr, cos1, sin1)
            zms_ref[...] = (dkr + dff_ref[...]).astype(BF16)

            kvn, kxh, krstd = _rms(kvl_ref[...], gkvv)
            kvb = kvn.astype(BF16)
            dvb = dv_ref[...].astype(BF16)
            dwk_ref[...] += _tn(kvb, dkn)
            dwv_ref[...] += _tn(kvb, dvb)
            dkvl, dgkv = _rms_bwd(_nt(dkn, wk_ref[...]) + _nt(dvb, wv_ref[...]), gkvv, kxh, krstd)
            zkv_ref[...] = dkvl.astype(BF16)
            dgkv_ref[0:1, :] += dgkv

        return pl.pallas_call(
            body, name="mla_post", grid=(n_tok,),
            in_specs=[tok(QW), tok(QW), tok(MW), tok(LANES), tok(QL, (cfg.o_ql - ZO) // QL), tok(KVL, (cfg.o_kv - ZO) // KVL),
                      const((1, QL)), const((1, KVL)), layer(l, (QL, QW)), layer(l, (KVL, QW)), layer(l, (KVL, MW)),
                      tok(LANES), tok(LANES)],
            out_specs=[tok(QL), tok(KVL), tok(LANES), const((QL, QW)), const((KVL, QW)), const((KVL, MW)),
                       const((8, QL)), const((8, KVL))],
            out_shape=[jax.ShapeDtypeStruct((S, QL), BF16), jax.ShapeDtypeStruct((S, KVL), BF16),
                       jax.ShapeDtypeStruct((S, LANES), BF16), jax.ShapeDtypeStruct((QL, QW), F32),
                       jax.ShapeDtypeStruct((KVL, QW), F32), jax.ShapeDtypeStruct((KVL, MW), F32),
                       jax.ShapeDtypeStruct((8, QL), F32), jax.ShapeDtypeStruct((8, KVL), F32)],
            compiler_params=_params(("arbitrary",)))(dq, dk, dv, dff, z, z, gq, gkv, w_uq, w_uk, w_v, cos_t, sin_t)

    def bwd_in(dz, w_in, l, x, dxn, g, scale):
        def body(dz_ref, w_ref, x_ref, dx_ref, g_ref, sc_ref, o_ref, acc_ref):
            @pl.when(pl.program_id(0) == 0)
            def _():
                acc_ref[...] = jnp.zeros_like(acc_ref)

            dh = _nt(dz_ref[...], w_ref[...])
            gv, mod = g_ref[...], 1.0 + sc_ref[...]
            _, xh, rstd = _rms(x_ref[...], gv)
            t = dh * xh
            acc_ref[0:1, :] += jnp.sum(dh, axis=0, keepdims=True)
            acc_ref[1:2, :] += jnp.sum(t * gv, axis=0, keepdims=True)
            acc_ref[2:3, :] += jnp.sum(t * mod, axis=0, keepdims=True)
            dx, _ = _rms_bwd(dh, gv * mod, xh, rstd)
            o_ref[...] = dx_ref[...] + dx

        return pl.pallas_call(
            body, name="bwd_in", grid=(n_tok,),
            in_specs=[tok(NZ), layer(l, (D, NZ)), tok(D), tok(D), const((1, D)), const((1, D))],
            out_specs=[tok(D), const((8, D))],
            out_shape=[jax.ShapeDtypeStruct((S, D), F32), jax.ShapeDtypeStruct((8, D), F32)],
            compiler_params=_params(("arbitrary",)))(dz, w_in, x, dxn, g, scale)


    def pair_rows(cum):
        n_pairs = FH // 2
        ct = jnp.pad(cum[:, :FH].T.reshape(n_pairs, 2, S), ((0, 0), (0, 6), (0, 0)))
        return ct.reshape(n_pairs, 8, S // TK, TK).transpose(0, 2, 1, 3)

    def bias_grad(dc, dr):
        n_pairs = FH // 2
        d = dr.transpose(0, 2, 1, 3).reshape(n_pairs, 8, S)[:, :2, :].reshape(FH, S).T
        d = d + dc[:, :, :2].transpose(1, 0, 2).reshape(S, FH)
        return jnp.pad(d, ((0, 0), (0, LANES - FH)))

    def layer_forward(x, wl, mod, later_shards=None):
        shift, scale, gate = mod
        h, z, qkv = ln_inproj(x, wl.norm_g, scale, shift, wl.w_in, wl.l)
        cum, tiles = fox_prep(z, wl.bf_pad)
        ct = pair_rows(cum)
        fox = attn_fwd(qkv, qkv, qkv, cfg.o_fq // LANES, cfg.o_fk // LANES, cfg.o_fv // LANES, tiles, True, False,
                       "fox_fwd_gather" if later_shards else "fox_fwd", GatherOverIci(later_shards) if later_shards else None)
        of, lse_f = fox[0], fox[1]
        qp, kp, vp = mla_prep(z, wl.gq, wl.gkv, wl.w_uq, wl.w_uk, wl.w_v, wl.l, cfg.cos_t, cfg.sin_t)
        mla = attn_fwd(qp, kp, vp, 0, 0, 0, None, False, True, "mla_fwd_gather" if later_shards else "mla_fwd",
                       GatherToSibling(fox[2]) if later_shards else None)
        om, lse_m = mla[0], mla[1]
        xn, u, y = gate_outproj(of, om, z, wl.w_out, wl.l, x, gate)
        saved = types.SimpleNamespace(x=x, h=h, z=z, qkv=qkv, ct=ct, of=of, lse_f=lse_f, qp=qp, kp=kp, vp=vp,
                                      om=om, lse_m=lse_m, u=u, y=y)
        return (xn, saved, mla[2]) if later_shards else (xn, saved)

    def layer_backward(dxn, sv, wl, mod, side=None):
        shift, scale, gate = mod
        do_f, do_m, dfg, dmg, dy, acc_o, st_f, st_m = bwd_out(dxn, gate, sv.y, wl.w_out, wl.l, sv.of, sv.om, sv.z,
                                                              sv.lse_f, sv.lse_m)
        dw_out = matmul_tn(sv.u, dy, "dw_out")
        dfq, dfk, dfv, dck, dcr, *rode = attn_bwd(sv.qkv, sv.qkv, sv.qkv, do_f, st_f, cfg.o_fq // LANES, cfg.o_fk // LANES,
                                                  cfg.o_fv // LANES, sv.ct, True, False,
                                                  "fox_bwd_exchange" if side else "fox_bwd", side)
        dff, acc_f = fox_post(bias_grad(dck, dcr), sv.z, wl.bf_pad)
        dqp, dkp, dvp = attn_bwd(sv.qp, sv.kp, sv.vp, do_m, st_m, 0, 0, 0, None, False, True, "mla_bwd")
        zq, zkv, zms, dw_uq, dw_uk, dw_v, dgq, dgkv = mla_post(
            dqp, dkp, dvp, dff, sv.z, wl.gq, wl.gkv, wl.w_uq, wl.w_uk, wl.w_v, wl.l, cfg.cos_t, cfg.sin_t)
        dz = jnp.concatenate([(dfq * FOX_SCALE).astype(BF16), dfk.astype(BF16), dfv.astype(BF16), dfg, dmg, zq, zkv, zms], axis=1)
        dx, acc_i = bwd_in(dz, wl.w_in, wl.l, sv.x, dxn, wl.norm_g, scale)
        dw_in = matmul_tn(sv.h, dz, "dw_in")
        grads = types.SimpleNamespace(
            w_in=dw_in, w_out=dw_out, w_uq=dw_uq, w_uk=dw_uk, w_v=dw_v, gq=dgq[0], gkv=dgkv[0],
            b_f=acc_f[0, :FH], norm_g=acc_i[2], dmod=jnp.concatenate([acc_i[0], acc_i[1], acc_o[0]]))
        return (dx, grads, rode[0]) if side else (dx, grads)

    return types.SimpleNamespace(layer_forward=layer_forward, layer_backward=layer_backward, final_loss=final_loss)


def _pack_rows(parts, dtype, row_multiple):
    flat = jnp.concatenate([p.reshape(-1).astype(dtype) for p in parts])
    per = SLAB_COLS * row_multiple
    total = -(-flat.shape[0] // per) * per
    return jnp.pad(flat, (0, total - flat.shape[0])).reshape(total // SLAB_COLS, SLAB_COLS)


def _unpack(flat, shapes):
    out, off = [], 0
    for shp in shapes:
        n = 1
        for d in shp:
            n *= d
        out.append(flat[off:off + n].reshape(shp))
        off += n
    return out


def kernel(x, c, positions, norm_g, w_ada, b_ada, w_in, b_f, q_norm_g, w_uq, kv_norm_g, w_ukv, w_out, final_g, loss_target, m_norm_g, m_w_ada, m_b_ada, m_w_in, m_b_f, m_q_norm_g, m_w_uq, m_kv_norm_g, m_w_ukv, m_w_out, m_final_g, v_norm_g, v_w_ada, v_b_ada, v_w_in, v_b_f, v_q_norm_g, v_w_uq, v_kv_norm_g, v_w_ukv, v_w_out, v_final_g):
    S, D = x.shape[1], x.shape[2]
    L = norm_g.shape[0]
    FH = b_f.shape[1]
    QL, KVL = q_norm_g.shape[1], kv_norm_g.shape[1]
    MH = w_ukv.shape[2] * N_CHIPS // (2 * HEAD_DIM)
    FW, MW = FH * HEAD_DIM, MH * HEAD_DIM
    NA = w_ada.shape[2]
    n_in = w_in.shape[2] * N_CHIPS
    cfg = types.SimpleNamespace(S=S, D=D, FW=FW, MW=MW, QL=QL, KVL=KVL, FH=FH, MH=MH)
    cfg.o_fq, cfg.o_fk, cfg.o_fv, cfg.o_fg, cfg.o_mg = 0, FW, 2 * FW, 3 * FW, 4 * FW
    cfg.o_ql = 4 * FW + MW
    cfg.o_kv = cfg.o_ql + QL
    cfg.o_ms = cfg.o_kv + KVL
    cfg.NZ = cfg.o_ms + LANES
    assert FW == MW and FH % 2 == 0 and MH % 2 == 0 and cfg.o_ql % QL == 0 and cfg.o_kv % KVL == 0 and KVL == LANES
    assert n_in == 4 * FW + FH + QL + KVL + ROPE_DIM + MW

    mx, my, mc = _my_pos()
    my_chip = 2 * mx + my
    my_dev = 2 * my_chip + mc

    inv_freq = 1.0 / (ROPE_THETA ** (jnp.arange(0, ROPE_DIM, 2, dtype=F32) / ROPE_DIM))
    ang = positions[0].astype(F32)[:, None] * inv_freq
    cos, sin = jnp.cos(ang), jnp.sin(ang)
    cfg.cos_t = jnp.concatenate([jnp.ones((S, ROPE_LO), F32), cos, cos, jnp.ones((S, HEAD_PAD - ROPE_LO - ROPE_DIM), F32)], axis=1)
    cfg.sin_t = jnp.concatenate([jnp.zeros((S, ROPE_LO), F32), -sin, sin, jnp.zeros((S, HEAD_PAD - ROPE_LO - ROPE_DIM), F32)], axis=1)

    assert L % 4 == 0
    def lane_pad(a):
        return jnp.pad(a, ((0, 0),) * (a.ndim - 1) + ((0, -a.shape[-1] % LANES),))

    shards = (w_in, w_uq, w_ukv, w_out)
    padded = [lane_pad(w.astype(BF16)) for w in shards]

    def kernel_layouts(gathered, own):
        def all_chips(g, w, axis):
            return jnp.concatenate([jnp.where(my_chip == t, w.astype(BF16), g[t][..., :w.shape[-1]])
                                    for t in range(N_CHIPS)], axis=axis)

        n = own[0].shape[0]
        w_in_f = all_chips(gathered[0], own[0], 2)
        w_uq_f = all_chips(gathered[1], own[1], 2)
        w_ukv_f = all_chips(gathered[2], own[2], 2)
        w_out_f = all_chips(gathered[3], own[3], 1)
        sizes = (FW, FW, FW, FH, FW, QL, KVL, ROPE_DIM, MW)
        offs = [0]
        for sz in sizes:
            offs.append(offs[-1] + sz)
        fq_w, fk_w, fv_w, ff_w, fg_w, ql_w, kvl_w, kr_w, mg_w = [w_in_f[:, :, offs[i]:offs[i + 1]] for i in range(len(sizes))]
        zeros = lambda width: jnp.zeros((n, D, width), BF16)
        w_in_p = jnp.concatenate([fq_w, fk_w, fv_w, fg_w, mg_w, ql_w, kvl_w, ff_w, zeros(ROPE_LO - FH), kr_w,
                                  zeros(HEAD_PAD - ROPE_LO - ROPE_DIM)], axis=2)
        w_uq_p = jnp.pad(w_uq_f.reshape(n, QL, MH, HEAD_DIM + ROPE_DIM),
                         ((0, 0), (0, 0), (0, 0), (0, HEAD_PAD - HEAD_DIM - ROPE_DIM))).reshape(n, QL, MH * HEAD_PAD)
        w_ukv4 = w_ukv_f.reshape(n, KVL, MH, 2 * HEAD_DIM)
        w_uk_p = jnp.pad(w_ukv4[..., :HEAD_DIM], ((0, 0), (0, 0), (0, 0), (0, HEAD_PAD - HEAD_DIM))).reshape(n, KVL, MH * HEAD_PAD)
        return types.SimpleNamespace(w_in=w_in_p, w_uq=w_uq_p, w_uk=w_uk_p, w_v=w_ukv4[..., HEAD_DIM:].reshape(n, KVL, MW),
                                     w_out=w_out_f)

    first = kernel_layouts(weights_gather([p[:1] for p in padded], "weights_first"), [w[:1] for w in shards])

    c_all = allgather8(c.reshape(8, D // 8), "gather_c").reshape(N_DEV, D)
    c_pad = jnp.pad(c_all, ((0, 16 - N_DEV), (0, 0)))
    mod_part = ada_forward(c_pad, w_ada)[:, :N_DEV, :]
    mod_all = allgather8(mod_part.reshape(-1, LANES), "gather_mod").reshape(N_CHIPS, 2, L, N_DEV, NA)[:, 0]
    mod_full = mod_all.transpose(1, 2, 0, 3).reshape(L, N_DEV, N_CHIPS * NA) + b_ada[:, None, :]
    mod_mine = lax.dynamic_index_in_dim(mod_full, my_dev, axis=1, keepdims=True)

    step = make_step(cfg)
    bf_pad = jnp.pad(b_f, ((0, 0), (0, LANES - FH)))
    mods = [(mod_mine[l, :, :D], mod_mine[l, :, D:2 * D], mod_mine[l, :, 2 * D:]) for l in range(L)]

    def layer_params(l, ws_, at):
        return types.SimpleNamespace(l=at, norm_g=norm_g[l][None], bf_pad=bf_pad[l][None], gq=q_norm_g[l][None],
                                     gkv=kv_norm_g[l][None], w_in=ws_.w_in, w_uq=ws_.w_uq, w_uk=ws_.w_uk, w_v=ws_.w_v,
                                     w_out=ws_.w_out)

    layers = [layer_params(0, first, 0)]
    xl, sv, later = step.layer_forward(x[0], layers[0], mods[0], [p[1:] for p in padded])
    saved = [sv]
    rest = kernel_layouts(later, [w[1:] for w in shards])
    for l in range(1, L):
        layers.append(layer_params(l, rest, l - 1))
        xl, sv = step.layer_forward(xl, layers[l], mods[l])
        saved.append(sv)
    dx, acc_fin, loss_part = step.final_loss(xl, final_g[None], loss_target[0])
    loss = lax.psum(loss_part[0, 0], ("x", "y", "c"))
    ms = cfg.o_ms
    runs = [(0, 3 * FW, 0), (3 * FW, FH, ms), (3 * FW + FH, FW, cfg.o_fg), (4 * FW + FH, QL + KVL, cfg.o_ql),
            (4 * FW + FH + QL + KVL, ROPE_DIM, ms + ROPE_LO), (4 * FW + FH + QL + KVL + ROPE_DIM, MW, cfg.o_mg)]
    big_names = ["w_in", "w_uq", "w_ukv", "w_out"]

    def shard_columns(dw, t):
        a, b = t * w_in.shape[2], (t + 1) * w_in.shape[2]
        return jnp.concatenate([dw[:, p0 + max(a, r0) - r0:p0 + min(b, r0 + sz) - r0]
                                for r0, sz, p0 in runs if max(a, r0) < min(b, r0 + sz)], axis=1)

    def chip_parts(group, tag):
        n = len(group)
        stk = lambda name: jnp.stack([getattr(g, name) for g in group])
        g_in4 = jnp.stack([jnp.stack([shard_columns(g.w_in, t) for g in group]) for t in range(N_CHIPS)])
        dw_uq_f = stk("w_uq").reshape(n, QL, MH, HEAD_PAD)[..., :HEAD_DIM + ROPE_DIM].reshape(n, QL, -1)
        dw_ukv_f = jnp.concatenate([stk("w_uk").reshape(n, KVL, MH, HEAD_PAD)[..., :HEAD_DIM],
                                    stk("w_v").reshape(n, KVL, MH, HEAD_DIM)], axis=3).reshape(n, KVL, -1)
        gs = [g_in4, dw_uq_f.reshape(n, QL, N_CHIPS, -1).transpose(2, 0, 1, 3),
              dw_ukv_f.reshape(n, KVL, N_CHIPS, -1).transpose(2, 0, 1, 3), stk("w_out").reshape(n, N_CHIPS, -1, D).transpose(1, 0, 2, 3)]
        gs = [lane_pad(g.astype(BF16)) for g in gs]
        theirs = halves_to_sibling(gs, "grads_sibling_" + tag)
        out = []
        for g, o, nm in zip(gs, theirs, big_names):
            keep = lax.dynamic_slice_in_dim(g, mc * (n // 2), n // 2, axis=1)
            merged = (N_CHIPS * (n // 2),) + g.shape[2:]
            out.append(add_cast(keep.reshape(merged), o.reshape(merged), BF16, "grads_chip_sum_%s_%s" % (nm, tag)).reshape(o.shape))
        return out

    def reduced(chip_part, parts, tag):
        where = jnp.stack([my_chip, mc]).astype(jnp.int32)
        red = [sum_chips(p, o, where, "grads_sum_%s_%s" % (nm, tag)) for p, o, nm in zip(parts, chip_part, big_names)]
        return [g[..., :w.shape[-1]] for g, w in zip(halves_gather(red, "grads_back_" + tag), shards)]

    gl = [None] * L
    half_l = L // 2
    for l in range(L - 1, half_l - 1, -1):
        dx, gl[l] = step.layer_backward(dx, saved[l], layers[l], mods[l])
    part_hi = chip_parts(gl[half_l:], "hi")
    dx, gl[half_l - 1], parts_hi = step.layer_backward(dx, saved[half_l - 1], layers[half_l - 1], mods[half_l - 1],
                                                       ExchangeOverIci(part_hi))
    for l in range(half_l - 2, -1, -1):
        dx, gl[l] = step.layer_backward(dx, saved[l], layers[l], mods[l])
    part_lo = chip_parts(gl[:half_l], "lo")
    parts_lo = run_side(ExchangeOverIci(part_lo), "grads_chips_lo")
    g_w_in, g_w_uq, g_w_ukv, g_w_out = [jnp.concatenate([lo, hi]) for lo, hi in
                                        zip(reduced(part_lo, parts_lo, "lo"), reduced(part_hi, parts_hi, "hi"))]
    grad_x = dx[None]

    stack = lambda name: jnp.stack([getattr(g, name) for g in gl])
    small_parts = [stack("norm_g"), stack("dmod"), stack("b_f"), stack("gq"), stack("gkv"), acc_fin[0]]
    small_shapes = [p.shape for p in small_parts]
    small = _pack_rows(small_parts, F32, 8).reshape(-1, LANES)
    small_all = allgather8(small, "gather_small").reshape(N_DEV, -1, LANES)
    small_sum = sum_leading(small_all, "sum_small")
    g_norm_g, g_b_ada, g_b_f, g_q_norm_g, g_kv_norm_g, g_final_g = _unpack(small_sum.reshape(-1), small_shapes)

    n_ng = L * D
    dmod_all = small_all.reshape(N_DEV, -1)[:, n_ng:n_ng + L * 3 * D].reshape(N_DEV, L, 3 * D)
    dmod_cols = lax.dynamic_slice_in_dim(dmod_all, my_chip * NA, NA, axis=2).transpose(1, 0, 2)
    g_w_ada = ada_backward(c_pad, jnp.pad(dmod_cols, ((0, 0), (0, 16 - N_DEV), (0, 0))))


    names = ["norm_g", "w_ada", "b_ada", "w_in", "b_f", "q_norm_g", "w_uq", "kv_norm_g", "w_ukv", "w_out", "final_g"]
    ws = dict(norm_g=norm_g, w_ada=w_ada, b_ada=b_ada, w_in=w_in, b_f=b_f, q_norm_g=q_norm_g, w_uq=w_uq,
              kv_norm_g=kv_norm_g, w_ukv=w_ukv, w_out=w_out, final_g=final_g)
    msd = dict(norm_g=m_norm_g, w_ada=m_w_ada, b_ada=m_b_ada, w_in=m_w_in, b_f=m_b_f, q_norm_g=m_q_norm_g, w_uq=m_w_uq,
               kv_norm_g=m_kv_norm_g, w_ukv=m_w_ukv, w_out=m_w_out, final_g=m_final_g)
    vsd = dict(norm_g=v_norm_g, w_ada=v_w_ada, b_ada=v_b_ada, w_in=v_w_in, b_f=v_b_f, q_norm_g=v_q_norm_g, w_uq=v_w_uq,
               kv_norm_g=v_kv_norm_g, w_ukv=v_w_ukv, w_out=v_w_out, final_g=v_final_g)
    gsd = dict(norm_g=g_norm_g, w_ada=g_w_ada, b_ada=g_b_ada, w_in=g_w_in, b_f=g_b_f, q_norm_g=g_q_norm_g, w_uq=g_w_uq,
               kv_norm_g=g_kv_norm_g, w_ukv=g_w_ukv, w_out=g_w_out, final_g=g_final_g)
    small_names = ["norm_g", "b_ada", "b_f", "q_norm_g", "kv_norm_g", "final_g"]
    sm_shapes = [ws[n].shape for n in small_names]
    pk = lambda d: _pack_rows([d[n] for n in small_names], F32, 8).reshape(1, -1, LANES)
    sm_out = adamw(pk(ws), pk(gsd), pk(msd), pk(vsd), "adamw_small")
    sm_d, sm_m, sm_v = [dict(zip(small_names, _unpack(o.reshape(-1), sm_shapes))) for o in sm_out]
    delta, new_m, new_v = dict(sm_d), dict(sm_m), dict(sm_v)
    for n in ["w_ada", "w_in", "w_uq", "w_ukv", "w_out"]:
        delta[n], new_m[n], new_v[n] = adamw(ws[n], gsd[n], msd[n], vsd[n], "adamw_" + n)

    return (loss, grad_x, *[gsd[n] for n in names], *[delta[n] for n in names],
            *[new_m[n] for n in names], *[new_v[n] for n in names])
```

```python
import types

import jax
import jax.numpy as jnp
from jax import lax
from jax.experimental import pallas as pl
from jax.experimental.pallas import tpu as pltpu

F32 = jnp.float32
BF16 = jnp.bfloat16
MESH = pl.DeviceIdType.MESH

N_CHIPS = 4
N_DEV = 8
HEAD_DIM = 64
ROPE_DIM = 32
ROPE_THETA = 10000.0
HEAD_PAD = 128
ROPE_LO = 64
ROPE_HALF = 16
CHUNK_SHIFT = 6
LANES = 128
EPS = 1e-6
NEG = -1e30
ADAM_LR = 0.001
ADAM_B1 = 0.9
ADAM_B2 = 0.999
ADAM_EPS = 1e-08
ADAM_WD = 0.01
ADAM_STEP = 10
VMEM_LIMIT = 48 * 1024 * 1024
SLAB_COLS = 1024


def _params(sem=None, vmem=VMEM_LIMIT):
    return pltpu.CompilerParams(dimension_semantics=sem, vmem_limit_bytes=vmem)


def _nn(a, b):
    return jnp.dot(a, b, preferred_element_type=F32)


def _nt(a, b):
    return lax.dot_general(a, b, (((1,), (1,)), ((), ())), preferred_element_type=F32)


def _tn(a, b):
    return lax.dot_general(a, b, (((0,), (0,)), ((), ())), preferred_element_type=F32)


def _sigmoid(x):
    return 1.0 / (1.0 + jnp.exp(-x))


def _lane(shape):
    return lax.broadcasted_iota(jnp.int32, shape, len(shape) - 1)


def _pick(n, cands):
    for c in cands:
        if n % c == 0:
            return c
    return n


def _my_pos():
    return lax.axis_index("x"), lax.axis_index("y"), lax.axis_index("c")


def allgather8(xs, name):
    m_per, n = xs.shape

    def body(x_ref, out_ref, send_sems, recv_sems, local_sem):
        x, y, c = _my_pos()
        me, sibling = (x, y, c), (x, y, 1 - c)
        chips = [(1 - x, y), (x, 1 - y), (1 - x, 1 - y)]

        def rows(px, py, pc):
            return out_ref.at[pl.ds((4 * px + 2 * py + pc) * m_per, m_per), :]

        def copy(k, block, to, src=None):
            return pltpu.make_async_remote_copy(
                src_ref=rows(*block) if src is None else src, dst_ref=rows(*block),
                send_sem=send_sems.at[k], recv_sem=recv_sems.at[k], device_id=to, device_id_type=MESH)

        mine = pltpu.make_async_copy(x_ref, rows(*me), local_sem)
        mine.start()
        first = [copy(0, me, sibling, src=x_ref)]
        first += [copy(1 + j, me, (*chip, c), src=x_ref) for j, chip in enumerate(chips)]
        for cp in first:
            cp.start()
        passed = [copy(4 + j, (*chip, c), sibling) for j, chip in enumerate(chips)]
        for j, chip in enumerate(chips):
            copy(1 + j, (*chip, c), me).wait_recv()
            passed[j].start()
        copy(0, sibling, me).wait_recv()
        for j, chip in enumerate(chips):
            copy(4 + j, (*chip, 1 - c), me).wait_recv()
        for cp in first + passed:
            cp.wait_send()
        mine.wait()

    return pl.pallas_call(
        body, name=name,
        out_shape=jax.ShapeDtypeStruct((N_DEV * m_per, n), xs.dtype),
        in_specs=[pl.BlockSpec(memory_space=pltpu.VMEM)],
        out_specs=pl.BlockSpec(memory_space=pltpu.VMEM),
        scratch_shapes=[pltpu.SemaphoreType.DMA((7,)), pltpu.SemaphoreType.DMA((7,)), pltpu.SemaphoreType.DMA],
    )(xs)


def _remote(src, dst, send_sems, recv_sems, k, to):
    return pltpu.make_async_remote_copy(src_ref=src, dst_ref=dst, send_sem=send_sems.at[k], recv_sem=recv_sems.at[k],
                                        device_id=to, device_id_type=MESH)


def _hbm_call(body, name, ins, out_shapes, n_sems, aliases=None):
    hbm = pl.BlockSpec(memory_space=pl.ANY)
    scratch = [pltpu.SemaphoreType.DMA((n_sems,)), pltpu.SemaphoreType.DMA((n_sems,))]
    return pl.pallas_call(body, name=name, out_shape=out_shapes, in_specs=[hbm] * len(ins),
                          out_specs=[hbm] * len(out_shapes), scratch_shapes=scratch,
                          input_output_aliases=aliases or {})(*ins)


def _layers_half(ref, h, axis=0):
    size = ref.shape[axis] // 2
    idx = (slice(None),) * axis + (pl.ds(h * size, size),)
    return ref.at[idx]


class GatherOverIci:
    def __init__(self, ws):
        self.ins = list(ws)
        self.outs = [jax.ShapeDtypeStruct((N_CHIPS,) + w.shape, w.dtype) for w in ws]
        self.aliases = {}
        self.n_sems = 3 * len(ws)

    def _copies(self, x_refs, o_refs, send_sems, recv_sems):
        x, y, c = _my_pos()
        s = 2 * x + y
        out = []
        for a in range(len(x_refs)):
            for k, (tx, ty) in enumerate([(1 - x, y), (x, 1 - y), (1 - x, 1 - y)]):
                mine = _remote(_layers_half(x_refs[a], c, 1), _layers_half(o_refs[a].at[s], c, 1), send_sems, recv_sems,
                               3 * a + k, (tx, ty, c))
                got = _layers_half(o_refs[a].at[2 * tx + ty], c, 1)
                out.append((mine, _remote(got, got, send_sems, recv_sems, 3 * a + k, (tx, ty, c))))
        return out

    def start(self, x_refs, o_refs, send_sems, recv_sems):
        for mine, _ in self._copies(x_refs, o_refs, send_sems, recv_sems):
            mine.start()

    def wait(self, x_refs, o_refs, send_sems, recv_sems):
        for mine, theirs in self._copies(x_refs, o_refs, send_sems, recv_sems):
            theirs.wait_recv()
            mine.wait_send()


class GatherToSibling:
    def __init__(self, gathered):
        self.ins = list(gathered)
        self.outs = [jax.ShapeDtypeStruct(g.shape, g.dtype) for g in gathered]
        self.aliases = {a: a for a in range(len(gathered))}
        self.n_sems = 3 * len(gathered)

    def _copies(self, o_refs, send_sems, recv_sems):
        x, y, c = _my_pos()
        out = []
        for a in range(len(o_refs)):
            for k, (tx, ty) in enumerate([(1 - x, y), (x, 1 - y), (1 - x, 1 - y)]):
                got = _layers_half(o_refs[a].at[2 * tx + ty], c, 1)
                theirs = _layers_half(o_refs[a].at[2 * tx + ty], 1 - c, 1)
                out.append((_remote(got, got, send_sems, recv_sems, 3 * a + k, (x, y, 1 - c)),
                            _remote(theirs, theirs, send_sems, recv_sems, 3 * a + k, (x, y, 1 - c))))
        return out

    def start(self, x_refs, o_refs, send_sems, recv_sems):
        for mine, _ in self._copies(o_refs, send_sems, recv_sems):
            mine.start()

    def wait(self, x_refs, o_refs, send_sems, recv_sems):
        for mine, theirs in self._copies(o_refs, send_sems, recv_sems):
            theirs.wait_recv()
            mine.wait_send()


def run_side(side, name):
    n_in, n_out = len(side.ins), len(side.outs)

    def body(*refs):
        parts = refs[:n_in], refs[n_in:n_in + n_out], refs[n_in + n_out], refs[n_in + n_out + 1]
        side.start(*parts)
        side.wait(*parts)

    return _hbm_call(body, name, side.ins, side.outs, side.n_sems, aliases=side.aliases)


def weights_gather(ws, name):
    return run_side(GatherToSibling(run_side(GatherOverIci(ws), name + "_ici")), name + "_sibling")


def halves_to_sibling(gs, name):
    n = len(gs)

    def body(*refs):
        x_refs, o_refs, send_sems, recv_sems = refs[:n], refs[n:2 * n], refs[2 * n], refs[2 * n + 1]
        x, y, c = _my_pos()
        cps = [_remote(_layers_half(x_refs[a], 1 - c, axis=1), o_refs[a], send_sems, recv_sems, a, (x, y, 1 - c))
               for a in range(n)]
        for cp in cps:
            cp.start()
        for cp in cps:
            cp.wait()

    outs = [jax.ShapeDtypeStruct((g.shape[0], g.shape[1] // 2) + g.shape[2:], g.dtype) for g in gs]
    return _hbm_call(body, name, gs, outs, n)


class ExchangeOverIci:
    def __init__(self, xs):
        self.ins = list(xs)
        self.outs = [jax.ShapeDtypeStruct(v.shape, v.dtype) for v in xs]
        self.aliases = {}
        self.n_sems = 3 * len(xs)

    def _copies(self, x_refs, o_refs, send_sems, recv_sems):
        x, y, c = _my_pos()
        s = 2 * x + y
        out = []
        for a in range(len(x_refs)):
            for k, (tx, ty) in enumerate([(1 - x, y), (x, 1 - y), (1 - x, 1 - y)]):
                got = o_refs[a].at[2 * tx + ty]
                out.append((_remote(x_refs[a].at[2 * tx + ty], o_refs[a].at[s], send_sems, recv_sems, 3 * a + k, (tx, ty, c)),
                            _remote(got, got, send_sems, recv_sems, 3 * a + k, (tx, ty, c))))
        return out

    def start(self, x_refs, o_refs, send_sems, recv_sems):
        for mine, _ in self._copies(x_refs, o_refs, send_sems, recv_sems):
            mine.start()

    def wait(self, x_refs, o_refs, send_sems, recv_sems):
        for mine, theirs in self._copies(x_refs, o_refs, send_sems, recv_sems):
            theirs.wait_recv()
            mine.wait_send()


def halves_gather(xs, name):
    n = len(xs)

    def body(*refs):
        x_refs, o_refs, send_sems, recv_sems = refs[:n], refs[n:2 * n], refs[2 * n], refs[2 * n + 1]
        x, y, c = _my_pos()
        sends = []
        for a in range(n):
            cp = _remote(_layers_half(x_refs[a], c), _layers_half(o_refs[a], c), send_sems, recv_sems, a, (x, y, 1 - c))
            cp.start()
            sends.append(cp)
        for a in range(n):
            theirs = _layers_half(o_refs[a], 1 - c)
            _remote(theirs, theirs, send_sems, recv_sems, a, (x, y, 1 - c)).wait_recv()
        for cp in sends:
            cp.wait_send()

    outs = [jax.ShapeDtypeStruct(v.shape, v.dtype) for v in xs]
    return _hbm_call(body, name, xs, outs, n, aliases={a: a for a in range(n)})


def sum_leading(xs, name):
    n, r, c = xs.shape

    def body(x_ref, o_ref):
        acc = x_ref[0]
        for i in range(1, n):
            acc = acc + x_ref[i]
        o_ref[...] = acc

    return pl.pallas_call(body, name=name, out_shape=jax.ShapeDtypeStruct((r, c), xs.dtype))(xs)


def add_cast(a, b, out_dtype, name):
    n, r, c = a.shape
    tr = _pick(r, (512, 256, 128, 64, 32, 16))

    def body(a_ref, b_ref, o_ref):
        o_ref[...] = (a_ref[...].astype(F32) + b_ref[...].astype(F32)).astype(out_dtype)

    spec = pl.BlockSpec((1, tr, c), lambda i, j: (i, j, 0))
    return pl.pallas_call(body, name=name, grid=(n, r // tr), in_specs=[spec, spec], out_specs=spec,
                          out_shape=jax.ShapeDtypeStruct((n, r, c), out_dtype),
                          compiler_params=_params(("parallel", "parallel")))(a, b)


def sum_chips(parts, own, where, name):
    n, nl, r, c = parts.shape
    tr = _pick(r, (512, 256, 128, 64, 32, 16))

    def body(w_ref, p_ref, o_ref, out_ref):
        s = w_ref[0]
        acc = jnp.zeros(out_ref.shape, F32)
        for t in range(n):
            acc = acc + jnp.where(s == t, o_ref[0], p_ref[t]).astype(F32)
        out_ref[...] = acc

    grid_spec = pltpu.PrefetchScalarGridSpec(
        num_scalar_prefetch=1, grid=(nl, r // tr),
        in_specs=[pl.BlockSpec((n, 1, tr, c), lambda i, j, w: (0, i, j, 0)),
                  pl.BlockSpec((1, 1, tr, c), lambda i, j, w: (w[0], i, j, 0))],
        out_specs=pl.BlockSpec((1, tr, c), lambda i, j, w: (w[1] * nl + i, j, 0)))
    return pl.pallas_call(body, name=name, grid_spec=grid_spec, out_shape=jax.ShapeDtypeStruct((2 * nl, r, c), F32),
                          compiler_params=_params(("parallel", "parallel")))(where, parts, own)


def ada_forward(c_all, w_ada):
    nl, d, n = w_ada.shape
    nb = c_all.shape[0]

    def body(c_ref, w_ref, o_ref):
        cv = c_ref[...]
        ca = (cv * _sigmoid(cv)).astype(BF16)
        o_ref[0] = _nn(ca, w_ref[0].astype(BF16))

    return pl.pallas_call(
        body, name="ada_forward", grid=(nl,),
        in_specs=[pl.BlockSpec((nb, d), lambda l: (0, 0)), pl.BlockSpec((1, d, n), lambda l: (l, 0, 0))],
        out_specs=pl.BlockSpec((1, nb, n), lambda l: (l, 0, 0)),
        out_shape=jax.ShapeDtypeStruct((nl, nb, n), F32), compiler_params=_params(("parallel",)))(c_all, w_ada)


def ada_backward(c_all, dmod):
    nl, nb, n = dmod.shape
    d = c_all.shape[1]

    def body(c_ref, g_ref, o_ref):
        cv = c_ref[...]
        ca = (cv * _sigmoid(cv)).astype(BF16)
        o_ref[0] = _tn(ca, g_ref[0].astype(BF16))

    return pl.pallas_call(
        body, name="ada_backward", grid=(nl,),
        in_specs=[pl.BlockSpec((nb, d), lambda l: (0, 0)), pl.BlockSpec((1, nb, n), lambda l: (l, 0, 0))],
        out_specs=pl.BlockSpec((1, d, n), lambda l: (l, 0, 0)),
        out_shape=jax.ShapeDtypeStruct((nl, d, n), F32), compiler_params=_params(("parallel",)))(c_all, dmod)


def matmul_tn(a, b, name):
    k, m = a.shape
    n = b.shape[1]
    tm, tk = _pick(m, (512, 256, 128)), _pick(k, (512, 256, 128))
    tn = n if n * (tm * 8 + tk * 4) <= VMEM_LIMIT // 2 else _pick(n, (512, 256, 128))

    def body(a_ref, b_ref, o_ref, acc_ref):
        @pl.when(pl.program_id(2) == 0)
        def _():
            acc_ref[...] = jnp.zeros_like(acc_ref)

        acc_ref[...] += _tn(a_ref[...], b_ref[...])

        @pl.when(pl.program_id(2) == k // tk - 1)
        def _():
            o_ref[...] = acc_ref[...].astype(BF16)

    return pl.pallas_call(
        body, name=name, grid=(m // tm, n // tn, k // tk),
        in_specs=[pl.BlockSpec((tk, tm), lambda i, j, kk: (kk, i)), pl.BlockSpec((tk, tn), lambda i, j, kk: (kk, j))],
        out_specs=pl.BlockSpec((tm, tn), lambda i, j, kk: (i, j)),
        out_shape=jax.ShapeDtypeStruct((m, n), BF16), scratch_shapes=[pltpu.VMEM((tm, tn), F32)],
        compiler_params=_params(("parallel", "parallel", "arbitrary")))(a, b)


def adamw(w, g, m, v, name):
    nl, r, c = w.shape
    tr = _pick(r, (512, 256, 128, 64, 32, 16, 8))

    def body(w_ref, g_ref, m_ref, v_ref, d_ref, mo_ref, vo_ref):
        gv = g_ref[...]
        mn = ADAM_B1 * m_ref[...] + (1.0 - ADAM_B1) * gv
        vn = ADAM_B2 * v_ref[...] + (1.0 - ADAM_B2) * (gv * gv)
        m_hat = mn / (1.0 - ADAM_B1 ** ADAM_STEP)
        v_hat = vn / (1.0 - ADAM_B2 ** ADAM_STEP)
        d_ref[...] = -ADAM_LR * (m_hat / (jnp.sqrt(v_hat) + ADAM_EPS) + ADAM_WD * w_ref[...])
        mo_ref[...] = mn
        vo_ref[...] = vn

    spec = pl.BlockSpec((1, tr, c), lambda l, i: (l, i, 0))
    out = jax.ShapeDtypeStruct((nl, r, c), F32)
    return pl.pallas_call(body, name=name, grid=(nl, r // tr), in_specs=[spec] * 4, out_specs=[spec] * 3,
                          out_shape=[out] * 3, compiler_params=_params(("parallel", "parallel")))(w, g, m, v)


def _rope(t, cos_t, sin_t):
    w = t.shape[1]
    lane = _lane(t.shape) & (HEAD_PAD - 1)
    first_half = (lane >= ROPE_LO) & (lane < ROPE_LO + ROPE_HALF)
    partner = jnp.where(first_half, pltpu.roll(t, w - ROPE_HALF, 1), pltpu.roll(t, ROPE_HALF, 1))
    return t * cos_t + partner * sin_t


def _rope_t(dt, cos_t, sin_t):
    w = dt.shape[1]
    lane = _lane(dt.shape) & (HEAD_PAD - 1)
    first_half = (lane >= ROPE_LO) & (lane < ROPE_LO + ROPE_HALF)
    ds = dt * sin_t
    partner = jnp.where(first_half, pltpu.roll(ds, w - ROPE_HALF, 1), pltpu.roll(ds, ROPE_HALF, 1))
    return dt * cos_t + partner


def _rms(xv, g):
    rstd = lax.rsqrt(jnp.mean(xv * xv, axis=-1, keepdims=True) + EPS)
    xh = xv * rstd
    return xh * g, xh, rstd


def _rms_bwd(dy, g, xh, rstd):
    dxh = dy * g
    dx = rstd * (dxh - xh * jnp.mean(dxh * xh, axis=-1, keepdims=True))
    return dx, jnp.sum(dy * xh, axis=0, keepdims=True)


def make_step(cfg):
    S, D, NZ = cfg.S, cfg.D, cfg.NZ
    FW, MW, QL, KVL, FH, MH = cfg.FW, cfg.MW, cfg.QL, cfg.KVL, cfg.FH, cfg.MH
    QW = MH * HEAD_PAD
    TM = _pick(S, (512, 256, 128))
    TQ = TK = _pick(S, (256, 128))
    n_tok = S // TM
    ZO = 3 * FW
    NZR = NZ - ZO
    misc_blk = (cfg.o_ms - ZO) // LANES
    FWD_UNROLL = 8
    FOX_SCALE = HEAD_DIM ** -0.5
    MLA_SCALE = (HEAD_DIM + ROPE_DIM) ** -0.5

    def tok(width, col=0):
        return pl.BlockSpec((TM, width), lambda i: (i, col))

    def const(shape):
        return pl.BlockSpec(shape, lambda i: tuple(0 for _ in shape))

    def layer(l, shape):
        return pl.BlockSpec((None,) + shape, lambda i: (l,) + tuple(0 for _ in shape))


    def ln_inproj(x, g, scale, shift, w_in, l):
        def body(x_ref, g_ref, sc_ref, sh_ref, w_ref, h_ref, z_ref, qkv_ref):
            y, _, _ = _rms(x_ref[...], g_ref[...])
            hb = (y * (1.0 + sc_ref[...]) + sh_ref[...]).astype(BF16)
            h_ref[...] = hb
            z = _nn(hb, w_ref[...])
            z_ref[...] = z[:, ZO:]
            qkv_ref[:, :FW] = (z[:, :FW] * FOX_SCALE).astype(BF16)
            qkv_ref[:, FW:] = z[:, FW:ZO].astype(BF16)

        return pl.pallas_call(
            body, name="ln_inproj", grid=(n_tok,),
            in_specs=[tok(D), const((1, D)), const((1, D)), const((1, D)), layer(l, (D, NZ))],
            out_specs=[tok(D), tok(NZR), tok(ZO)],
            out_shape=[jax.ShapeDtypeStruct((S, D), BF16), jax.ShapeDtypeStruct((S, NZR), F32),
                       jax.ShapeDtypeStruct((S, ZO), BF16)],
            compiler_params=_params(("parallel",)))(x, g, scale, shift, w_in)

    def _log_f_terms(misc, bf):
        lane = _lane(misc.shape)
        a = misc + bf
        e = jnp.exp(-jnp.abs(a))
        logf = jnp.minimum(a, 0.0) - jnp.log(1.0 + e)
        sig_neg = jnp.where(a >= 0, e, 1.0) / (1.0 + e)
        valid = lane < FH
        return jnp.where(valid, logf, 0.0), jnp.where(valid, sig_neg, 0.0)

    def fox_prep(z, bf_pad):
        def body(z_ref, b_ref, o_ref, t_ref, carry):
            @pl.when(pl.program_id(0) == 0)
            def _():
                carry[...] = jnp.zeros_like(carry)

            logf, _ = _log_f_terms(z_ref[...], b_ref[...])
            row = lax.broadcasted_iota(jnp.int32, (TM, TM), 0)
            col = lax.broadcasted_iota(jnp.int32, (TM, TM), 1)
            tri = (col <= row).astype(F32)
            cum = jnp.dot(tri, logf, precision=lax.Precision.HIGHEST, preferred_element_type=F32) + carry[...]
            o_ref[...] = cum
            carry[...] = cum[TM - 1:TM, :]
            for hd in range(FH):
                t_ref[hd // 2, hd % 2] = jnp.broadcast_to(cum[:, hd:hd + 1], (TM, LANES))

        return pl.pallas_call(
            body, name="fox_prep", grid=(n_tok,),
            in_specs=[tok(LANES, misc_blk), const((1, LANES))],
            out_specs=[tok(LANES), pl.BlockSpec((FH // 2, 2, TM, LANES), lambda i: (0, 0, i, 0))],
            out_shape=[jax.ShapeDtypeStruct((S, LANES), F32), jax.ShapeDtypeStruct((FH // 2, 2, S, LANES), F32)],
            scratch_shapes=[pltpu.VMEM((1, LANES), F32)],
            compiler_params=_params(("arbitrary",)))(z, bf_pad)

    def mla_prep(z, gq, gkv, w_uq, w_uk, w_v, l, cos_t, sin_t):
        def body(ql_ref, kvl_ref, ms_ref, gq_ref, gkv_ref, wq_ref, wk_ref, wv_ref, c_ref, s_ref, q_ref, k_ref, v_ref):
            cos1, sin1 = c_ref[...], s_ref[...]
            qn, _, _ = _rms(ql_ref[...], gq_ref[...])
            q = _nn(qn.astype(BF16), wq_ref[...])
            q_ref[...] = (_rope(q, jnp.tile(cos1, (1, MH)), jnp.tile(sin1, (1, MH))) * MLA_SCALE).astype(BF16)
            kvn, _, _ = _rms(kvl_ref[...], gkv_ref[...])
            kvb = kvn.astype(BF16)
            lane = _lane((TM, LANES))
            kr = jnp.where((lane >= ROPE_LO) & (lane < ROPE_LO + ROPE_DIM), ms_ref[...], 0.0)
            kr = _rope(kr, cos1, sin1)
            k_ref[...] = (_nn(kvb, wk_ref[...]) + jnp.tile(kr, (1, MH))).astype(BF16)
            v_ref[...] = _nn(kvb, wv_ref[...]).astype(BF16)

        return pl.pallas_call(
            body, name="mla_prep", grid=(n_tok,),
            in_specs=[tok(QL, (cfg.o_ql - ZO) // QL), tok(KVL, (cfg.o_kv - ZO) // KVL), tok(LANES, misc_blk),
                      const((1, QL)), const((1, KVL)), layer(l, (QL, QW)), layer(l, (KVL, QW)), layer(l, (KVL, MW)),
                      tok(LANES), tok(LANES)],
            out_specs=[tok(QW), tok(QW), tok(MW)],
            out_shape=[jax.ShapeDtypeStruct((S, QW), BF16), jax.ShapeDtypeStruct((S, QW), BF16),
                       jax.ShapeDtypeStruct((S, MW), BF16)],
            compiler_params=_params(("parallel",)))(z, z, z, gq, gkv, w_uq, w_uk, w_v, cos_t, sin_t)

    def _allowed(q0, k0, chunked):
        qi = q0 + lax.broadcasted_iota(jnp.int32, (TQ, TK), 0)
        ki = k0 + lax.broadcasted_iota(jnp.int32, (TQ, TK), 1)
        if chunked:
            return (ki >> CHUNK_SHIFT) <= (qi >> CHUNK_SHIFT)
        return ki <= qi

    def _heads(val, packed):
        if packed:
            lane = _lane(val.shape)
            zero = jnp.zeros_like(val)
            return [jnp.where(lane < HEAD_DIM, val, zero), jnp.where(lane >= HEAD_DIM, val, zero)]
        return [val[:, :HEAD_PAD], val[:, HEAD_PAD:]]

    def _merge(a0, a1):
        return jnp.where(_lane(a0.shape) < HEAD_DIM, a0, a1)

    def attn_fwd(q, k, v, q_blk0, k_blk0, v_blk0, cumt, packed, chunked, name, side=None):
        has_bias = cumt is not None
        n_pairs = (FH if packed else MH) // 2
        wq = LANES if packed else 2 * HEAD_PAD
        assert TQ == TK

        n_main = 4 if has_bias else 3
        n_side_in, n_side_out = (len(side.ins), len(side.outs)) if side else (0, 0)

        def body(*refs):
            if has_bias:
                q_ref, k_ref, v_ref, ct_ref = refs[:n_main]
            else:
                q_ref, k_ref, v_ref = refs[:n_main]
            o_ref, lse_ref = refs[n_main + n_side_in:n_main + n_side_in + 2]
            i = pl.program_id(1)
            if side:
                side_refs = (refs[n_main:n_main + n_side_in], refs[n_main + n_side_in + 2:n_main + n_side_in + 2 + n_side_out],
                             refs[-2], refs[-1])

                @pl.when((pl.program_id(0) == 0) & (i == 0))
                def _():
                    side.start(*side_refs)

            q0 = i * TQ
            qh = _heads(q_ref[...], packed)
            q_both = jnp.concatenate(qh, axis=0) if packed else None

            def scores(kb):
                kblk = k_ref[pl.ds(pl.multiple_of(kb * TK, TK), TK), :]
                if packed:
                    return _nt(kblk, q_both)
                return jnp.concatenate([_nt(kblk[:, :HEAD_PAD], qh[0]), _nt(kblk[:, HEAD_PAD:], qh[1])], axis=1)

            def update(kb, s, carry, masked):
                m, l, acc = carry
                k0 = pl.multiple_of(kb * TK, TK)
                s0, s1 = s[:, :TQ], s[:, TQ:]
                if has_bias:
                    ck = ct_ref[0, :, pl.ds(k0, TK), :]
                    s0, s1 = s0 - jnp.tile(ck[0], (1, TQ // LANES)), s1 - jnp.tile(ck[1], (1, TQ // LANES))
                if masked:
                    allow = jnp.transpose(_allowed(q0, k0, chunked))
                    s0, s1 = jnp.where(allow, s0, NEG), jnp.where(allow, s1, NEG)
                s = jnp.concatenate([s0, s1], axis=1)
                m_new = jnp.maximum(m, jnp.max(s, axis=0, keepdims=True))
                p = jnp.exp(s - m_new)
                alpha = jnp.exp(m - m_new)
                l = alpha * l + jnp.sum(p, axis=0, keepdims=True)
                return m_new, l, alpha * acc + _tn(v_ref[pl.ds(k0, TK), :], p.astype(BF16))

            def several(kb, carry, n, last_masked):
                ss = [scores(kb + u) for u in range(n)]
                for u in range(n):
                    carry = update(kb + u, ss[u], carry, last_masked and u == n - 1)
                return carry

            init = (jnp.full((1, 2 * TQ), NEG, F32), jnp.zeros((1, 2 * TQ), F32), jnp.zeros((LANES, 2 * TQ), F32))
            carry = lax.fori_loop(0, i // FWD_UNROLL, lambda t, cr: several(FWD_UNROLL * t, cr, FWD_UNROLL, False), init)
            m, l, acc = lax.switch(i % FWD_UNROLL,
                                   [lambda cr, r=r: several(i - r, cr, r + 1, True) for r in range(FWD_UNROLL)], carry)
            o = acc / l
            lse = m + jnp.log(l)
            o_ref[...] = jnp.transpose(jnp.concatenate([o[:HEAD_DIM, :TQ], o[HEAD_DIM:, TQ:]], axis=0))
            sub = lax.broadcasted_iota(jnp.int32, (LANES, TQ), 0)
            lse_ref[...] = jnp.transpose(jnp.where(sub == 0, lse[:, :TQ], jnp.where(sub == 1, lse[:, TQ:], 0.0)))
            if side:
                @pl.when((pl.program_id(0) == n_pairs - 1) & (i == S // TQ - 1))
                def _():
                    side.wait(*side_refs)

        in_specs = [pl.BlockSpec((TQ, wq), lambda p, i: (i, q_blk0 + p)),
                    pl.BlockSpec((S, wq), lambda p, i: (0, k_blk0 + p)),
                    pl.BlockSpec((S, LANES), lambda p, i: (0, v_blk0 + p))]
        args = [q, k, v]
        if has_bias:
            in_specs += [pl.BlockSpec((1, 2, S, LANES), lambda p, i: (p, 0, 0, 0))]
            args += [cumt]
        out_specs = [pl.BlockSpec((TQ, LANES), lambda p, i: (i, p)), pl.BlockSpec((TQ, LANES), lambda p, i: (i, p))]
        out_shape = [jax.ShapeDtypeStruct((S, n_pairs * LANES), F32), jax.ShapeDtypeStruct((S, n_pairs * LANES), F32)]
        extra = {}
        if side:
            hbm = pl.BlockSpec(memory_space=pl.ANY)
            in_specs += [hbm] * n_side_in
            args += side.ins
            out_specs += [hbm] * n_side_out
            out_shape += side.outs
            extra = dict(scratch_shapes=[pltpu.SemaphoreType.DMA((side.n_sems,)), pltpu.SemaphoreType.DMA((side.n_sems,))],
                         input_output_aliases={n_main + a: 2 + b for a, b in side.aliases.items()})
        outs = pl.pallas_call(
            body, name=name, grid=(n_pairs, S // TQ), in_specs=in_specs, out_specs=out_specs, out_shape=out_shape,
            compiler_params=_params(("arbitrary", "arbitrary")), **extra)(*args)
        return (outs[0], outs[1], list(outs[2:])) if side else outs

    def gate_outproj(of, om, z, w_out, l, x, gate):
        def body(of_ref, om_ref, fg_ref, mg_ref, w_ref, x_ref, gt_ref, xn_ref, u_ref, y_ref):
            fg, mg = fg_ref[...], mg_ref[...]
            u = jnp.concatenate([of_ref[...] * fg * _sigmoid(fg), om_ref[...] * mg * _sigmoid(mg)], axis=1).astype(BF16)
            y = _nn(u, w_ref[...])
            u_ref[...] = u
            y_ref[...] = y.astype(BF16)
            xn_ref[...] = x_ref[...] + gt_ref[...] * y

        return pl.pallas_call(
            body, name="gate_outproj", grid=(n_tok,),
            in_specs=[tok(FW), tok(MW), tok(FW, (cfg.o_fg - ZO) // FW), tok(MW, (cfg.o_mg - ZO) // MW),
                      layer(l, (FW + MW, D)), tok(D), const((1, D))],
            out_specs=[tok(D), tok(FW + MW), tok(D)],
            out_shape=[jax.ShapeDtypeStruct((S, D), F32), jax.ShapeDtypeStruct((S, FW + MW), BF16),
                       jax.ShapeDtypeStruct((S, D), BF16)],
            compiler_params=_params(("parallel",)))(of, om, z, z, w_out, x, gate)

    def final_loss(x, g, target):
        def body(x_ref, g_ref, t_ref, dx_ref, acc_ref, loss_ref):
            @pl.when(pl.program_id(0) == 0)
            def _():
                acc_ref[...] = jnp.zeros_like(acc_ref)
                loss_ref[...] = jnp.zeros_like(loss_ref)

            gv = g_ref[...]
            y, xh, rstd = _rms(x_ref[...], gv)
            e = y - t_ref[...]
            loss_ref[...] += 0.5 * jnp.sum(jnp.sum(e * e, axis=-1, keepdims=True) / D, axis=0, keepdims=True)
            dx, dg = _rms_bwd(e / D, gv, xh, rstd)
            dx_ref[...] = dx
            acc_ref[0:1, :] += dg

        return pl.pallas_call(
            body, name="final_loss", grid=(n_tok,),
            in_specs=[tok(D), const((1, D)), tok(D)],
            out_specs=[tok(D), const((8, D)), const((1, LANES))],
            out_shape=[jax.ShapeDtypeStruct((S, D), F32), jax.ShapeDtypeStruct((8, D), F32),
                       jax.ShapeDtypeStruct((1, LANES), F32)],
            compiler_params=_params(("arbitrary",)))(x, g, target)


    def bwd_out(dxn, gate, y, w_out, l, of, om, z, lse_f, lse_m):
        def body(dx_ref, gt_ref, y_ref, w_ref, of_ref, om_ref, fg_ref, mg_ref, lf_ref, lm_ref,
                 dof_ref, dom_ref, dfg_ref, dmg_ref, dy_ref, acc_ref, sf_ref, sm_ref):
            @pl.when(pl.program_id(0) == 0)
            def _():
                acc_ref[...] = jnp.zeros_like(acc_ref)

            dxv = dx_ref[...]
            acc_ref[0:1, :] += jnp.sum(dxv * y_ref[...].astype(F32), axis=0, keepdims=True)
            dy = (gt_ref[...] * dxv).astype(BF16)
            dy_ref[...] = dy
            du = _nt(dy, w_ref[...])
            lane = _lane((TM, LANES))
            for lo, width, o_ref, g_ref, do_ref, dg_ref, l_ref, st_ref in (
                    (0, FW, of_ref, fg_ref, dof_ref, dfg_ref, lf_ref, sf_ref),
                    (FW, MW, om_ref, mg_ref, dom_ref, dmg_ref, lm_ref, sm_ref)):
                gv, ov = g_ref[...], o_ref[...]
                sg = _sigmoid(gv)
                dup = du[:, lo:lo + width]
                dob = (dup * gv * sg).astype(BF16)
                do_ref[...] = dob
                dg_ref[...] = (dup * ov * sg * (1.0 + gv * (1.0 - sg))).astype(BF16)
                d = dob.astype(F32) * ov
                for pr in range(width // LANES):
                    cols = slice(pr * LANES, (pr + 1) * LANES)
                    dp = d[:, cols]
                    d0 = jnp.sum(jnp.where(lane < HEAD_DIM, dp, 0.0), axis=-1, keepdims=True)
                    d1 = jnp.sum(jnp.where(lane >= HEAD_DIM, dp, 0.0), axis=-1, keepdims=True)
                    st_ref[:, cols] = jnp.where(lane == 2, d0, jnp.where(lane == 3, d1, l_ref[:, cols]))

        return pl.pallas_call(
            body, name="bwd_out", grid=(n_tok,),
            in_specs=[tok(D), const((1, D)), tok(D), layer(l, (FW + MW, D)), tok(FW), tok(MW),
                      tok(FW, (cfg.o_fg - ZO) // FW), tok(MW, (cfg.o_mg - ZO) // MW), tok(FW), tok(MW)],
            out_specs=[tok(FW), tok(MW), tok(FW), tok(MW), tok(D), const((8, D)), tok(FW), tok(MW)],
            out_shape=[jax.ShapeDtypeStruct((S, FW), BF16), jax.ShapeDtypeStruct((S, MW), BF16),
                       jax.ShapeDtypeStruct((S, FW), BF16), jax.ShapeDtypeStruct((S, MW), BF16),
                       jax.ShapeDtypeStruct((S, D), BF16), jax.ShapeDtypeStruct((8, D), F32),
                       jax.ShapeDtypeStruct((S, FW), F32), jax.ShapeDtypeStruct((S, MW), F32)],
            compiler_params=_params(("arbitrary",)))(dxn, gate, y, w_out, of, om, z, z, lse_f, lse_m)

    def attn_bwd(q, k, v, do, stats, q_blk0, k_blk0, v_blk0, cumt, packed, chunked, name, side=None):
        has_bias = cumt is not None
        n_pairs = (FH if packed else MH) // 2
        wq = LANES if packed else 2 * HEAD_PAD
        n_q = S // TQ
        assert TQ == TK
        n_in, n_out = (6, 5) if has_bias else (5, 3)
        n_side_in, n_side_out = (len(side.ins), len(side.outs)) if side else (0, 0)

        def body(*refs):
            main_out = refs[n_in + n_side_in:n_in + n_side_in + n_out]
            if has_bias:
                q_ref, k_ref, v_ref, do_ref, st_ref, ct_ref = refs[:n_in]
                dq_ref, dk_ref, dv_ref, dc_ref, dr_ref = main_out
            else:
                q_ref, k_ref, v_ref, do_ref, st_ref = refs[:n_in]
                dq_ref, dk_ref, dv_ref = main_out
            jb = pl.program_id(1)
            k0 = jb * TK
            if side:
                side_refs = (refs[n_in:n_in + n_side_in], refs[n_in + n_side_in + n_out:n_in + n_side_in + n_out + n_side_out],
                             refs[-2], refs[-1])

                @pl.when((pl.program_id(0) == 0) & (jb == 0))
                def _():
                    side.start(*side_refs)

            @pl.when(jb == 0)
            def _():
                dq_ref[...] = jnp.zeros_like(dq_ref)
                if has_bias:
                    dr_ref[...] = jnp.zeros_like(dr_ref)

            dk_ref[...] = jnp.zeros_like(dk_ref)
            dv_ref[...] = jnp.zeros_like(dv_ref)
            kh = _heads(k_ref[...], packed)
            k_both = jnp.concatenate(kh, axis=0) if packed else None
            v_both = jnp.concatenate(_heads(v_ref[...], True), axis=0)
            if has_bias:
                ct = jnp.transpose(ct_ref[0, 0])
                ck = jnp.concatenate([ct[:, 0:1], ct[:, 1:2]], axis=0)

            def products(ib):
                rows = pl.ds(pl.multiple_of(ib * TQ, TQ), TQ)
                q2, do2 = q_ref[rows, :], do_ref[rows, :]
                if packed:
                    s = _nt(k_both, q2)
                else:
                    qh = _heads(q2, False)
                    s = jnp.concatenate([_nt(kh[0], qh[0]), _nt(kh[1], qh[1])], axis=0)
                return s, _nt(v_both, do2)

            def update(ib, s, dp, carry, masked):
                q0 = pl.multiple_of(ib * TQ, TQ)
                rows = pl.ds(q0, TQ)
                q2, do2 = q_ref[rows, :], do_ref[rows, :]
                st = jnp.transpose(st_ref[rows, :])
                if not packed:
                    qh = _heads(q2, False)
                if has_bias:
                    s = s - ck
                if masked:
                    allow = jnp.transpose(_allowed(q0, k0, chunked))
                    s = jnp.where(jnp.concatenate([allow, allow], axis=0), s, NEG)
                p = jnp.concatenate([jnp.exp(s[:TK] - st[0:1, :]), jnp.exp(s[TK:] - st[1:2, :])], axis=0)
                dv2 = _nn(p.astype(BF16), do2)
                ds = jnp.concatenate([p[:TK] * (dp[:TK] - st[2:3, :]), p[TK:] * (dp[TK:] - st[3:4, :])], axis=0)
                dsb = ds.astype(BF16)
                dv_ref[...] += _merge(dv2[:TK], dv2[TK:])
                if packed:
                    dk2 = _nn(dsb, q2)
                    dk_ref[...] += _merge(dk2[:TK], dk2[TK:])
                    dq_ref[rows, :] += _tn(dsb, k_both)
                else:
                    dk_ref[...] += jnp.concatenate([_nn(dsb[:TK], qh[0]), _nn(dsb[TK:], qh[1])], axis=1)
                    dq_ref[rows, :] += jnp.concatenate([_tn(dsb[:TK], kh[0]), _tn(dsb[TK:], kh[1])], axis=1)
                if has_bias:
                    sub = lax.broadcasted_iota(jnp.int32, (8, TQ), 0)
                    r0 = jnp.sum(ds[:TK], axis=0, keepdims=True)
                    r1 = jnp.sum(ds[TK:], axis=0, keepdims=True)
                    dr_ref[0, ib] += jnp.where(sub == 0, r0, jnp.where(sub == 1, r1, 0.0))
                    return carry - jnp.sum(ds, axis=1, keepdims=True)
                return carry

            def step(ib, carry, masked):
                s, dp = products(ib)
                return update(ib, s, dp, carry, masked)

            def two(ib, carry):
                s_a, dp_a = products(ib)
                s_b, dp_b = products(ib + 1)
                return update(ib + 1, s_b, dp_b, update(ib, s_a, dp_a, carry, False), False)

            dc = step(jb, jnp.zeros((2 * TK, 1), F32), True)
            n_rest = n_q - 1 - jb
            dc = lax.fori_loop(0, n_rest // 2, lambda t, cr: two(jb + 1 + 2 * t, cr), dc)
            dc = lax.cond(n_rest % 2 == 1, lambda cr: step(n_q - 1, cr, False), lambda cr: cr, dc)
            if has_bias:
                lane = _lane((TK, LANES))
                dc_ref[0] = jnp.where(lane == 0, dc[:TK], jnp.where(lane == 1, dc[TK:], 0.0))
            if side:
                @pl.when((pl.program_id(0) == n_pairs - 1) & (jb == S // TK - 1))
                def _():
                    side.wait(*side_refs)

        in_specs = [pl.BlockSpec((S, wq), lambda p, j: (0, q_blk0 + p)),
                    pl.BlockSpec((TK, wq), lambda p, j: (j, k_blk0 + p)),
                    pl.BlockSpec((TK, LANES), lambda p, j: (j, v_blk0 + p)),
                    pl.BlockSpec((S, LANES), lambda p, j: (0, p)),
                    pl.BlockSpec((S, LANES), lambda p, j: (0, p))]
        args = [q, k, v, do, stats]
        out_specs = [pl.BlockSpec((S, wq), lambda p, j: (0, p)),
                     pl.BlockSpec((TK, wq), lambda p, j: (j, p)),
                     pl.BlockSpec((TK, LANES), lambda p, j: (j, p))]
        out_shape = [jax.ShapeDtypeStruct((S, n_pairs * wq), F32), jax.ShapeDtypeStruct((S, n_pairs * wq), F32),
                     jax.ShapeDtypeStruct((S, n_pairs * LANES), F32)]
        if has_bias:
            in_specs += [pl.BlockSpec((1, 1, 8, TK), lambda p, j: (p, j, 0, 0))]
            args += [cumt]
            out_specs += [pl.BlockSpec((1, TK, LANES), lambda p, j: (p, j, 0)),
                          pl.BlockSpec((1, S // TQ, 8, TQ), lambda p, j: (p, 0, 0, 0))]
            out_shape += [jax.ShapeDtypeStruct((n_pairs, S, LANES), F32),
                          jax.ShapeDtypeStruct((n_pairs, S // TQ, 8, TQ), F32)]
        extra = {}
        if side:
            hbm = pl.BlockSpec(memory_space=pl.ANY)
            in_specs += [hbm] * n_side_in
            args += side.ins
            out_specs += [hbm] * n_side_out
            out_shape += side.outs
            extra = dict(scratch_shapes=[pltpu.SemaphoreType.DMA((side.n_sems,)), pltpu.SemaphoreType.DMA((side.n_sems,))],
                         input_output_aliases={n_in + a: n_out + b for a, b in side.aliases.items()})
        outs = pl.pallas_call(
            body, name=name, grid=(n_pairs, S // TK), in_specs=in_specs, out_specs=out_specs, out_shape=out_shape,
            compiler_params=_params(("arbitrary", "arbitrary")), **extra)(*args)
        return (*outs[:n_out], list(outs[n_out:])) if side else outs

    def fox_post(dcum, z, bf_pad):
        def rev(i):
            return n_tok - 1 - i

        def body(dc_ref, z_ref, b_ref, dff_ref, acc_ref, carry):
            @pl.when(pl.program_id(0) == 0)
            def _():
                carry[...] = jnp.zeros_like(carry)
                acc_ref[...] = jnp.zeros_like(acc_ref)

            _, sig_neg = _log_f_terms(z_ref[...], b_ref[...])
            row = lax.broadcasted_iota(jnp.int32, (TM, TM), 0)
            col = lax.broadcasted_iota(jnp.int32, (TM, TM), 1)
            tri = (col >= row).astype(F32)
            dlog = jnp.dot(tri, dc_ref[...], precision=lax.Precision.HIGHEST, preferred_element_type=F32) + carry[...]
            carry[...] = dlog[0:1, :]
            dff = dlog * sig_neg
            dff_ref[...] = dff
            acc_ref[0:1, :] += jnp.sum(dff, axis=0, keepdims=True)

        return pl.pallas_call(
            body, name="fox_post", grid=(n_tok,),
            in_specs=[pl.BlockSpec((TM, LANES), lambda i: (rev(i), 0)),
                      pl.BlockSpec((TM, LANES), lambda i: (rev(i), misc_blk)), const((1, LANES))],
            out_specs=[pl.BlockSpec((TM, LANES), lambda i: (rev(i), 0)), const((8, LANES))],
            out_shape=[jax.ShapeDtypeStruct((S, LANES), F32), jax.ShapeDtypeStruct((8, LANES), F32)],
            scratch_shapes=[pltpu.VMEM((1, LANES), F32)],
            compiler_params=_params(("arbitrary",)))(dcum, z, bf_pad)

    def mla_post(dq, dk, dv, dff, z, gq, gkv, w_uq, w_uk, w_v, l, cos_t, sin_t):
        def body(dq_ref, dk_ref, dv_ref, dff_ref, ql_ref, kvl_ref, gq_ref, gkv_ref, wq_ref, wk_ref, wv_ref,
                 c_ref, s_ref, zq_ref, zkv_ref, zms_ref, dwq_ref, dwk_ref, dwv_ref, dgq_ref, dgkv_ref):
            @pl.when(pl.program_id(0) == 0)
            def _():
                for r in (dwq_ref, dwk_ref, dwv_ref, dgq_ref, dgkv_ref):
                    r[...] = jnp.zeros_like(r)

            cos1, sin1 = c_ref[...], s_ref[...]
            gqv, gkvv = gq_ref[...], gkv_ref[...]
            qn, qxh, qrstd = _rms(ql_ref[...], gqv)
            dq_pre = _rope_t(dq_ref[...] * MLA_SCALE, jnp.tile(cos1, (1, MH)), jnp.tile(sin1, (1, MH))).astype(BF16)
            dwq_ref[...] += _tn(qn.astype(BF16), dq_pre)
            dql, dgq = _rms_bwd(_nt(dq_pre, wq_ref[...]), gqv, qxh, qrstd)
            zq_ref[...] = dql.astype(BF16)
            dgq_ref[0:1, :] += dgq

            dkv = dk_ref[...]
            lane = _lane(dkv.shape) & (HEAD_PAD - 1)
            dkn = jnp.where(lane < HEAD_DIM, dkv, 0.0).astype(BF16)
            dkr = dkv[:, 0:HEAD_PAD]
            for hd in range(1, MH):
                dkr = dkr + dkv[:, hd * HEAD_PAD:(hd + 1) * HEAD_PAD]
            lane1 = _lane(dkr.shape)
            dkr = jnp.where((lane1 >= ROPE_LO) & (lane1 < ROPE_LO + ROPE_DIM), dkr, 0.0)
            dkr = _rope_t(dkr, cos1, sin1)
            zms_ref[...] = (dkr + dff_ref[...]).astype(BF16)

            kvn, kxh, krstd = _rms(kvl_ref[...], gkvv)
            kvb = kvn.astype(BF16)
            dvb = dv_ref[...].astype(BF16)
            dwk_ref[...] += _tn(kvb, dkn)
            dwv_ref[...] += _tn(kvb, dvb)
            dkvl, dgkv = _rms_bwd(_nt(dkn, wk_ref[...]) + _nt(dvb, wv_ref[...]), gkvv, kxh, krstd)
            zkv_ref[...] = dkvl.astype(BF16)
            dgkv_ref[0:1, :] += dgkv

        return pl.pallas_call(
            body, name="mla_post", grid=(n_tok,),
            in_specs=[tok(QW), tok(QW), tok(MW), tok(LANES), tok(QL, (cfg.o_ql - ZO) // QL), tok(KVL, (cfg.o_kv - ZO) // KVL),
                      const((1, QL)), const((1, KVL)), layer(l, (QL, QW)), layer(l, (KVL, QW)), layer(l, (KVL, MW)),
                      tok(LANES), tok(LANES)],
            out_specs=[tok(QL), tok(KVL), tok(LANES), const((QL, QW)), const((KVL, QW)), const((KVL, MW)),
                       const((8, QL)), const((8, KVL))],
            out_shape=[jax.ShapeDtypeStruct((S, QL), BF16), jax.ShapeDtypeStruct((S, KVL), BF16),
                       jax.ShapeDtypeStruct((S, LANES), BF16), jax.ShapeDtypeStruct((QL, QW), F32),
                       jax.ShapeDtypeStruct((KVL, QW), F32), jax.ShapeDtypeStruct((KVL, MW), F32),
                       jax.ShapeDtypeStruct((8, QL), F32), jax.ShapeDtypeStruct((8, KVL), F32)],
            compiler_params=_params(("arbitrary",)))(dq, dk, dv, dff, z, z, gq, gkv, w_uq, w_uk, w_v, cos_t, sin_t)

    def bwd_in(dz, w_in, l, x, dxn, g, scale):
        def body(dz_ref, w_ref, x_ref, dx_ref, g_ref, sc_ref, o_ref, acc_ref):
            @pl.when(pl.program_id(0) == 0)
            def _():
                acc_ref[...] = jnp.zeros_like(acc_ref)

            dh = _nt(dz_ref[...], w_ref[...])
            gv, mod = g_ref[...], 1.0 + sc_ref[...]
            _, xh, rstd = _rms(x_ref[...], gv)
            t = dh * xh
            acc_ref[0:1, :] += jnp.sum(dh, axis=0, keepdims=True)
            acc_ref[1:2, :] += jnp.sum(t * gv, axis=0, keepdims=True)
            acc_ref[2:3, :] += jnp.sum(t * mod, axis=0, keepdims=True)
            dx, _ = _rms_bwd(dh, gv * mod, xh, rstd)
            o_ref[...] = dx_ref[...] + dx

        return pl.pallas_call(
            body, name="bwd_in", grid=(n_tok,),
            in_specs=[tok(NZ), layer(l, (D, NZ)), tok(D), tok(D), const((1, D)), const((1, D))],
            out_specs=[tok(D), const((8, D))],
            out_shape=[jax.ShapeDtypeStruct((S, D), F32), jax.ShapeDtypeStruct((8, D), F32)],
            compiler_params=_params(("arbitrary",)))(dz, w_in, x, dxn, g, scale)


    def pair_rows(cum):
        n_pairs = FH // 2
        ct = jnp.pad(cum[:, :FH].T.reshape(n_pairs, 2, S), ((0, 0), (0, 6), (0, 0)))
        return ct.reshape(n_pairs, 8, S // TK, TK).transpose(0, 2, 1, 3)

    def bias_grad(dc, dr):
        n_pairs = FH // 2
        d = dr.transpose(0, 2, 1, 3).reshape(n_pairs, 8, S)[:, :2, :].reshape(FH, S).T
        d = d + dc[:, :, :2].transpose(1, 0, 2).reshape(S, FH)
        return jnp.pad(d, ((0, 0), (0, LANES - FH)))

    def layer_forward(x, wl, mod, later_shards=None):
        shift, scale, gate = mod
        h, z, qkv = ln_inproj(x, wl.norm_g, scale, shift, wl.w_in, wl.l)
        cum, tiles = fox_prep(z, wl.bf_pad)
        ct = pair_rows(cum)
        fox = attn_fwd(qkv, qkv, qkv, cfg.o_fq // LANES, cfg.o_fk // LANES, cfg.o_fv // LANES, tiles, True, False,
                       "fox_fwd_gather" if later_shards else "fox_fwd", GatherOverIci(later_shards) if later_shards else None)
        of, lse_f = fox[0], fox[1]
        qp, kp, vp = mla_prep(z, wl.gq, wl.gkv, wl.w_uq, wl.w_uk, wl.w_v, wl.l, cfg.cos_t, cfg.sin_t)
        mla = attn_fwd(qp, kp, vp, 0, 0, 0, None, False, True, "mla_fwd_gather" if later_shards else "mla_fwd",
                       GatherToSibling(fox[2]) if later_shards else None)
        om, lse_m = mla[0], mla[1]
        xn, u, y = gate_outproj(of, om, z, wl.w_out, wl.l, x, gate)
        saved = types.SimpleNamespace(x=x, h=h, z=z, qkv=qkv, ct=ct, of=of, lse_f=lse_f, qp=qp, kp=kp, vp=vp,
                                      om=om, lse_m=lse_m, u=u, y=y)
        return (xn, saved, mla[2]) if later_shards else (xn, saved)

    def layer_backward(dxn, sv, wl, mod, side=None):
        shift, scale, gate = mod
        do_f, do_m, dfg, dmg, dy, acc_o, st_f, st_m = bwd_out(dxn, gate, sv.y, wl.w_out, wl.l, sv.of, sv.om, sv.z,
                                                              sv.lse_f, sv.lse_m)
        dw_out = matmul_tn(sv.u, dy, "dw_out")
        dfq, dfk, dfv, dck, dcr, *rode = attn_bwd(sv.qkv, sv.qkv, sv.qkv, do_f, st_f, cfg.o_fq // LANES, cfg.o_fk // LANES,
                                                  cfg.o_fv // LANES, sv.ct, True, False,
                                                  "fox_bwd_exchange" if side else "fox_bwd", side)
        dff, acc_f = fox_post(bias_grad(dck, dcr), sv.z, wl.bf_pad)
        dqp, dkp, dvp = attn_bwd(sv.qp, sv.kp, sv.vp, do_m, st_m, 0, 0, 0, None, False, True, "mla_bwd")
        zq, zkv, zms, dw_uq, dw_uk, dw_v, dgq, dgkv = mla_post(
            dqp, dkp, dvp, dff, sv.z, wl.gq, wl.gkv, wl.w_uq, wl.w_uk, wl.w_v, wl.l, cfg.cos_t, cfg.sin_t)
        dz = jnp.concatenate([(dfq * FOX_SCALE).astype(BF16), dfk.astype(BF16), dfv.astype(BF16), dfg, dmg, zq, zkv, zms], axis=1)
        dx, acc_i = bwd_in(dz, wl.w_in, wl.l, sv.x, dxn, wl.norm_g, scale)
        dw_in = matmul_tn(sv.h, dz, "dw_in")
        grads = types.SimpleNamespace(
            w_in=dw_in, w_out=dw_out, w_uq=dw_uq, w_uk=dw_uk, w_v=dw_v, gq=dgq[0], gkv=dgkv[0],
            b_f=acc_f[0, :FH], norm_g=acc_i[2], dmod=jnp.concatenate([acc_i[0], acc_i[1], acc_o[0]]))
        return (dx, grads, rode[0]) if side else (dx, grads)

    return types.SimpleNamespace(layer_forward=layer_forward, layer_backward=layer_backward, final_loss=final_loss)


def _pack_rows(parts, dtype, row_multiple):
    flat = jnp.concatenate([p.reshape(-1).astype(dtype) for p in parts])
    per = SLAB_COLS * row_multiple
    total = -(-flat.shape[0] // per) * per
    return jnp.pad(flat, (0, total - flat.shape[0])).reshape(total // SLAB_COLS, SLAB_COLS)


def _unpack(flat, shapes):
    out, off = [], 0
    for shp in shapes:
        n = 1
        for d in shp:
            n *= d
        out.append(flat[off:off + n].reshape(shp))
        off += n
    return out


def kernel(x, c, positions, norm_g, w_ada, b_ada, w_in, b_f, q_norm_g, w_uq, kv_norm_g, w_ukv, w_out, final_g, loss_target, m_norm_g, m_w_ada, m_b_ada, m_w_in, m_b_f, m_q_norm_g, m_w_uq, m_kv_norm_g, m_w_ukv, m_w_out, m_final_g, v_norm_g, v_w_ada, v_b_ada, v_w_in, v_b_f, v_q_norm_g, v_w_uq, v_kv_norm_g, v_w_ukv, v_w_out, v_final_g):
    S, D = x.shape[1], x.shape[2]
    L = norm_g.shape[0]
    FH = b_f.shape[1]
    QL, KVL = q_norm_g.shape[1], kv_norm_g.shape[1]
    MH = w_ukv.shape[2] * N_CHIPS // (2 * HEAD_DIM)
    FW, MW = FH * HEAD_DIM, MH * HEAD_DIM
    NA = w_ada.shape[2]
    n_in = w_in.shape[2] * N_CHIPS
    cfg = types.SimpleNamespace(S=S, D=D, FW=FW, MW=MW, QL=QL, KVL=KVL, FH=FH, MH=MH)
    cfg.o_fq, cfg.o_fk, cfg.o_fv, cfg.o_fg, cfg.o_mg = 0, FW, 2 * FW, 3 * FW, 4 * FW
    cfg.o_ql = 4 * FW + MW
    cfg.o_kv = cfg.o_ql + QL
    cfg.o_ms = cfg.o_kv + KVL
    cfg.NZ = cfg.o_ms + LANES
    assert FW == MW and FH % 2 == 0 and MH % 2 == 0 and cfg.o_ql % QL == 0 and cfg.o_kv % KVL == 0 and KVL == LANES
    assert n_in == 4 * FW + FH + QL + KVL + ROPE_DIM + MW

    mx, my, mc = _my_pos()
    my_chip = 2 * mx + my
    my_dev = 2 * my_chip + mc

    inv_freq = 1.0 / (ROPE_THETA ** (jnp.arange(0, ROPE_DIM, 2, dtype=F32) / ROPE_DIM))
    ang = positions[0].astype(F32)[:, None] * inv_freq
    cos, sin = jnp.cos(ang), jnp.sin(ang)
    cfg.cos_t = jnp.concatenate([jnp.ones((S, ROPE_LO), F32), cos, cos, jnp.ones((S, HEAD_PAD - ROPE_LO - ROPE_DIM), F32)], axis=1)
    cfg.sin_t = jnp.concatenate([jnp.zeros((S, ROPE_LO), F32), -sin, sin, jnp.zeros((S, HEAD_PAD - ROPE_LO - ROPE_DIM), F32)], axis=1)

    assert L % 4 == 0
    def lane_pad(a):
        return jnp.pad(a, ((0, 0),) * (a.ndim - 1) + ((0, -a.shape[-1] % LANES),))

    shards = (w_in, w_uq, w_ukv, w_out)
    padded = [lane_pad(w.astype(BF16)) for w in shards]

    def kernel_layouts(gathered, own):
        def all_chips(g, w, axis):
            return jnp.concatenate([jnp.where(my_chip == t, w.astype(BF16), g[t][..., :w.shape[-1]])
                                    for t in range(N_CHIPS)], axis=axis)

        n = own[0].shape[0]
        w_in_f = all_chips(gathered[0], own[0], 2)
        w_uq_f = all_chips(gathered[1], own[1], 2)
        w_ukv_f = all_chips(gathered[2], own[2], 2)
        w_out_f = all_chips(gathered[3], own[3], 1)
        sizes = (FW, FW, FW, FH, FW, QL, KVL, ROPE_DIM, MW)
        offs = [0]
        for sz in sizes:
            offs.append(offs[-1] + sz)
        fq_w, fk_w, fv_w, ff_w, fg_w, ql_w, kvl_w, kr_w, mg_w = [w_in_f[:, :, offs[i]:offs[i + 1]] for i in range(len(sizes))]
        zeros = lambda width: jnp.zeros((n, D, width), BF16)
        w_in_p = jnp.concatenate([fq_w, fk_w, fv_w, fg_w, mg_w, ql_w, kvl_w, ff_w, zeros(ROPE_LO - FH), kr_w,
                                  zeros(HEAD_PAD - ROPE_LO - ROPE_DIM)], axis=2)
        w_uq_p = jnp.pad(w_uq_f.reshape(n, QL, MH, HEAD_DIM + ROPE_DIM),
                         ((0, 0), (0, 0), (0, 0), (0, HEAD_PAD - HEAD_DIM - ROPE_DIM))).reshape(n, QL, MH * HEAD_PAD)
        w_ukv4 = w_ukv_f.reshape(n, KVL, MH, 2 * HEAD_DIM)
        w_uk_p = jnp.pad(w_ukv4[..., :HEAD_DIM], ((0, 0), (0, 0), (0, 0), (0, HEAD_PAD - HEAD_DIM))).reshape(n, KVL, MH * HEAD_PAD)
        return types.SimpleNamespace(w_in=w_in_p, w_uq=w_uq_p, w_uk=w_uk_p, w_v=w_ukv4[..., HEAD_DIM:].reshape(n, KVL, MW),
                                     w_out=w_out_f)

    first = kernel_layouts(weights_gather([p[:1] for p in padded], "weights_first"), [w[:1] for w in shards])

    c_all = allgather8(c.reshape(8, D // 8), "gather_c").reshape(N_DEV, D)
    c_pad = jnp.pad(c_all, ((0, 16 - N_DEV), (0, 0)))
    mod_part = ada_forward(c_pad, w_ada)[:, :N_DEV, :]
    mod_all = allgather8(mod_part.reshape(-1, LANES), "gather_mod").reshape(N_CHIPS, 2, L, N_DEV, NA)[:, 0]
    mod_full = mod_all.transpose(1, 2, 0, 3).reshape(L, N_DEV, N_CHIPS * NA) + b_ada[:, None, :]
    mod_mine = lax.dynamic_index_in_dim(mod_full, my_dev, axis=1, keepdims=True)

    step = make_step(cfg)
    bf_pad = jnp.pad(b_f, ((0, 0), (0, LANES - FH)))
    mods = [(mod_mine[l, :, :D], mod_mine[l, :, D:2 * D], mod_mine[l, :, 2 * D:]) for l in range(L)]

    def layer_params(l, ws_, at):
        return types.SimpleNamespace(l=at, norm_g=norm_g[l][None], bf_pad=bf_pad[l][None], gq=q_norm_g[l][None],
                                     gkv=kv_norm_g[l][None], w_in=ws_.w_in, w_uq=ws_.w_uq, w_uk=ws_.w_uk, w_v=ws_.w_v,
                                     w_out=ws_.w_out)

    layers = [layer_params(0, first, 0)]
    xl, sv, later = step.layer_forward(x[0], layers[0], mods[0], [p[1:] for p in padded])
    saved = [sv]
    rest = kernel_layouts(later, [w[1:] for w in shards])
    for l in range(1, L):
        layers.append(layer_params(l, rest, l - 1))
        xl, sv = step.layer_forward(xl, layers[l], mods[l])
        saved.append(sv)
    dx, acc_fin, loss_part = step.final_loss(xl, final_g[None], loss_target[0])
    loss = lax.psum(loss_part[0, 0], ("x", "y", "c"))
    ms = cfg.o_ms
    runs = [(0, 3 * FW, 0), (3 * FW, FH, ms), (3 * FW + FH, FW, cfg.o_fg), (4 * FW + FH, QL + KVL, cfg.o_ql),
            (4 * FW + FH + QL + KVL, ROPE_DIM, ms + ROPE_LO), (4 * FW + FH + QL + KVL + ROPE_DIM, MW, cfg.o_mg)]
    big_names = ["w_in", "w_uq", "w_ukv", "w_out"]

    def shard_columns(dw, t):
        a, b = t * w_in.shape[2], (t + 1) * w_in.shape[2]
        return jnp.concatenate([dw[:, p0 + max(a, r0) - r0:p0 + min(b, r0 + sz) - r0]
                                for r0, sz, p0 in runs if max(a, r0) < min(b, r0 + sz)], axis=1)

    def chip_parts(group, tag):
        n = len(group)
        stk = lambda name: jnp.stack([getattr(g, name) for g in group])
        g_in4 = jnp.stack([jnp.stack([shard_columns(g.w_in, t) for g in group]) for t in range(N_CHIPS)])
        dw_uq_f = stk("w_uq").reshape(n, QL, MH, HEAD_PAD)[..., :HEAD_DIM + ROPE_DIM].reshape(n, QL, -1)
        dw_ukv_f = jnp.concatenate([stk("w_uk").reshape(n, KVL, MH, HEAD_PAD)[..., :HEAD_DIM],
                                    stk("w_v").reshape(n, KVL, MH, HEAD_DIM)], axis=3).reshape(n, KVL, -1)
        gs = [g_in4, dw_uq_f.reshape(n, QL, N_CHIPS, -1).transpose(2, 0, 1, 3),
              dw_ukv_f.reshape(n, KVL, N_CHIPS, -1).transpose(2, 0, 1, 3), stk("w_out").reshape(n, N_CHIPS, -1, D).transpose(1, 0, 2, 3)]
        gs = [lane_pad(g.astype(BF16)) for g in gs]
        theirs = halves_to_sibling(gs, "grads_sibling_" + tag)
        out = []
        for g, o, nm in zip(gs, theirs, big_names):
            keep = lax.dynamic_slice_in_dim(g, mc * (n // 2), n // 2, axis=1)
            merged = (N_CHIPS * (n // 2),) + g.shape[2:]
            out.append(add_cast(keep.reshape(merged), o.reshape(merged), BF16, "grads_chip_sum_%s_%s" % (nm, tag)).reshape(o.shape))
        return out

    def reduced(chip_part, parts, tag):
        where = jnp.stack([my_chip, mc]).astype(jnp.int32)
        red = [sum_chips(p, o, where, "grads_sum_%s_%s" % (nm, tag)) for p, o, nm in zip(parts, chip_part, big_names)]
        return [g[..., :w.shape[-1]] for g, w in zip(halves_gather(red, "grads_back_" + tag), shards)]

    gl = [None] * L
    half_l = L // 2
    for l in range(L - 1, half_l - 1, -1):
        dx, gl[l] = step.layer_backward(dx, saved[l], layers[l], mods[l])
    part_hi = chip_parts(gl[half_l:], "hi")
    dx, gl[half_l - 1], parts_hi = step.layer_backward(dx, saved[half_l - 1], layers[half_l - 1], mods[half_l - 1],
                                                       ExchangeOverIci(part_hi))
    for l in range(half_l - 2, -1, -1):
        dx, gl[l] = step.layer_backward(dx, saved[l], layers[l], mods[l])
    part_lo = chip_parts(gl[:half_l], "lo")
    parts_lo = run_side(ExchangeOverIci(part_lo), "grads_chips_lo")
    g_w_in, g_w_uq, g_w_ukv, g_w_out = [jnp.concatenate([lo, hi]) for lo, hi in
                                        zip(reduced(part_lo, parts_lo, "lo"), reduced(part_hi, parts_hi, "hi"))]
    grad_x = dx[None]

    stack = lambda name: jnp.stack([getattr(g, name) for g in gl])
    small_parts = [stack("norm_g"), stack("dmod"), stack("b_f"), stack("gq"), stack("gkv"), acc_fin[0]]
    small_shapes = [p.shape for p in small_parts]
    small = _pack_rows(small_parts, F32, 8).reshape(-1, LANES)
    small_all = allgather8(small, "gather_small").reshape(N_DEV, -1, LANES)
    small_sum = sum_leading(small_all, "sum_small")
    g_norm_g, g_b_ada, g_b_f, g_q_norm_g, g_kv_norm_g, g_final_g = _unpack(small_sum.reshape(-1), small_shapes)

    n_ng = L * D
    dmod_all = small_all.reshape(N_DEV, -1)[:, n_ng:n_ng + L * 3 * D].reshape(N_DEV, L, 3 * D)
    dmod_cols = lax.dynamic_slice_in_dim(dmod_all, my_chip * NA, NA, axis=2).transpose(1, 0, 2)
    g_w_ada = ada_backward(c_pad, jnp.pad(dmod_cols, ((0, 0), (0, 16 - N_DEV), (0, 0))))


    names = ["norm_g", "w_ada", "b_ada", "w_in", "b_f", "q_norm_g", "w_uq", "kv_norm_g", "w_ukv", "w_out", "final_g"]
    ws = dict(norm_g=norm_g, w_ada=w_ada, b_ada=b_ada, w_in=w_in, b_f=b_f, q_norm_g=q_norm_g, w_uq=w_uq,
              kv_norm_g=kv_norm_g, w_ukv=w_ukv, w_out=w_out, final_g=final_g)
    msd = dict(norm_g=m_norm_g, w_ada=m_w_ada, b_ada=m_b_ada, w_in=m_w_in, b_f=m_b_f, q_norm_g=m_q_norm_g, w_uq=m_w_uq,
               kv_norm_g=m_kv_norm_g, w_ukv=m_w_ukv, w_out=m_w_out, final_g=m_final_g)
    vsd = dict(norm_g=v_norm_g, w_ada=v_w_ada, b_ada=v_b_ada, w_in=v_w_in, b_f=v_b_f, q_norm_g=v_q_norm_g, w_uq=v_w_uq,
               kv_norm_g=v_kv_norm_g, w_ukv=v_w_ukv, w_out=v_w_out, final_g=v_final_g)
    gsd = dict(norm_g=g_norm_g, w_ada=g_w_ada, b_ada=g_b_ada, w_in=g_w_in, b_f=g_b_f, q_norm_g=g_q_norm_g, w_uq=g_w_uq,
               kv_norm_g=g_kv_norm_g, w_ukv=g_w_ukv, w_out=g_w_out, final_g=g_final_g)
    small_names = ["norm_g", "b_ada", "b_f", "q_norm_g", "kv_norm_g", "final_g"]
    sm_shapes = [ws[n].shape for n in small_names]
    pk = lambda d: _pack_rows([d[n] for n in small_names], F32, 8).reshape(1, -1, LANES)
    sm_out = adamw(pk(ws), pk(gsd), pk(msd), pk(vsd), "adamw_small")
    sm_d, sm_m, sm_v = [dict(zip(small_names, _unpack(o.reshape(-1), sm_shapes))) for o in sm_out]
    delta, new_m, new_v = dict(sm_d), dict(sm_m), dict(sm_v)
    for n in ["w_ada", "w_in", "w_uq", "w_ukv", "w_out"]:
        delta[n], new_m[n], new_v[n] = adamw(ws[n], gsd[n], msd[n], vsd[n], "adamw_" + n)

    return (loss, grad_x, *[gsd[n] for n in names], *[delta[n] for n in names],
            *[new_m[n] for n in names], *[new_v[n] for n in names])
```

```python
import types

import jax
import jax.numpy as jnp
from jax import lax
from jax.experimental import pallas as pl
from jax.experimental.pallas import tpu as pltpu

F32 = jnp.float32
BF16 = jnp.bfloat16
MESH = pl.DeviceIdType.MESH

N_CHIPS = 4
N_DEV = 8
HEAD_DIM = 64
ROPE_DIM = 32
ROPE_THETA = 10000.0
HEAD_PAD = 128
ROPE_LO = 64
ROPE_HALF = 16
CHUNK_SHIFT = 6
LANES = 128
EPS = 1e-6
NEG = -1e30
ADAM_LR = 0.001
ADAM_B1 = 0.9
ADAM_B2 = 0.999
ADAM_EPS = 1e-08
ADAM_WD = 0.01
ADAM_STEP = 10
VMEM_LIMIT = 48 * 1024 * 1024
SLAB_COLS = 1024


def _params(sem=None, vmem=VMEM_LIMIT):
    return pltpu.CompilerParams(dimension_semantics=sem, vmem_limit_bytes=vmem)


def _nn(a, b):
    return jnp.dot(a, b, preferred_element_type=F32)


def _nt(a, b):
    return lax.dot_general(a, b, (((1,), (1,)), ((), ())), preferred_element_type=F32)


def _tn(a, b):
    return lax.dot_general(a, b, (((0,), (0,)), ((), ())), preferred_element_type=F32)


def _sigmoid(x):
    return 1.0 / (1.0 + jnp.exp(-x))


def _lane(shape):
    return lax.broadcasted_iota(jnp.int32, shape, len(shape) - 1)


def _pick(n, cands):
    for c in cands:
        if n % c == 0:
            return c
    return n


def _my_pos():
    return lax.axis_index("x"), lax.axis_index("y"), lax.axis_index("c")


def allgather8(xs, name):
    m_per, n = xs.shape

    def body(x_ref, out_ref, send_sems, recv_sems, local_sem):
        x, y, c = _my_pos()
        me, sibling = (x, y, c), (x, y, 1 - c)
        chips = [(1 - x, y), (x, 1 - y), (1 - x, 1 - y)]

        def rows(px, py, pc):
            return out_ref.at[pl.ds((4 * px + 2 * py + pc) * m_per, m_per), :]

        def copy(k, block, to, src=None):
            return pltpu.make_async_remote_copy(
                src_ref=rows(*block) if src is None else src, dst_ref=rows(*block),
                send_sem=send_sems.at[k], recv_sem=recv_sems.at[k], device_id=to, device_id_type=MESH)

        mine = pltpu.make_async_copy(x_ref, rows(*me), local_sem)
        mine.start()
        first = [copy(0, me, sibling, src=x_ref)]
        first += [copy(1 + j, me, (*chip, c), src=x_ref) for j, chip in enumerate(chips)]
        for cp in first:
            cp.start()
        passed = [copy(4 + j, (*chip, c), sibling) for j, chip in enumerate(chips)]
        for j, chip in enumerate(chips):
            copy(1 + j, (*chip, c), me).wait_recv()
            passed[j].start()
        copy(0, sibling, me).wait_recv()
        for j, chip in enumerate(chips):
            copy(4 + j, (*chip, 1 - c), me).wait_recv()
        for cp in first + passed:
            cp.wait_send()
        mine.wait()

    return pl.pallas_call(
        body, name=name,
        out_shape=jax.ShapeDtypeStruct((N_DEV * m_per, n), xs.dtype),
        in_specs=[pl.BlockSpec(memory_space=pltpu.VMEM)],
        out_specs=pl.BlockSpec(memory_space=pltpu.VMEM),
        scratch_shapes=[pltpu.SemaphoreType.DMA((7,)), pltpu.SemaphoreType.DMA((7,)), pltpu.SemaphoreType.DMA],
    )(xs)


def _remote(src, dst, send_sems, recv_sems, k, to):
    return pltpu.make_async_remote_copy(src_ref=src, dst_ref=dst, send_sem=send_sems.at[k], recv_sem=recv_sems.at[k],
                                        device_id=to, device_id_type=MESH)


def _hbm_call(body, name, ins, out_shapes, n_sems, aliases=None):
    hbm = pl.BlockSpec(memory_space=pl.ANY)
    scratch = [pltpu.SemaphoreType.DMA((n_sems,)), pltpu.SemaphoreType.DMA((n_sems,))]
    return pl.pallas_call(body, name=name, out_shape=out_shapes, in_specs=[hbm] * len(ins),
                          out_specs=[hbm] * len(out_shapes), scratch_shapes=scratch,
                          input_output_aliases=aliases or {})(*ins)


def _layers_half(ref, h, axis=0):
    size = ref.shape[axis] // 2
    idx = (slice(None),) * axis + (pl.ds(h * size, size),)
    return ref.at[idx]


class GatherOverIci:
    def __init__(self, ws):
        self.ins = list(ws)
        self.outs = [jax.ShapeDtypeStruct((N_CHIPS,) + w.shape, w.dtype) for w in ws]
        self.aliases = {}
        self.n_sems = 3 * len(ws)

    def _copies(self, x_refs, o_refs, send_sems, recv_sems):
        x, y, c = _my_pos()
        s = 2 * x + y
        out = []
        for a in range(len(x_refs)):
            for k, (tx, ty) in enumerate([(1 - x, y), (x, 1 - y), (1 - x, 1 - y)]):
                mine = _remote(_layers_half(x_refs[a], c, 1), _layers_half(o_refs[a].at[s], c, 1), send_sems, recv_sems,
                               3 * a + k, (tx, ty, c))
                got = _layers_half(o_refs[a].at[2 * tx + ty], c, 1)
                out.append((mine, _remote(got, got, send_sems, recv_sems, 3 * a + k, (tx, ty, c))))
        return out

    def start(self, x_refs, o_refs, send_sems, recv_sems):
        for mine, _ in self._copies(x_refs, o_refs, send_sems, recv_sems):
            mine.start()

    def wait(self, x_refs, o_refs, send_sems, recv_sems):
        for mine, theirs in self._copies(x_refs, o_refs, send_sems, recv_sems):
            theirs.wait_recv()
            mine.wait_send()


class GatherToSibling:
    def __init__(self, gathered):
        self.ins = list(gathered)
        self.outs = [jax.ShapeDtypeStruct(g.shape, g.dtype) for g in gathered]
        self.aliases = {a: a for a in range(len(gathered))}
        self.n_sems = 3 * len(gathered)

    def _copies(self, o_refs, send_sems, recv_sems):
        x, y, c = _my_pos()
        out = []
        for a in range(len(o_refs)):
            for k, (tx, ty) in enumerate([(1 - x, y), (x, 1 - y), (1 - x, 1 - y)]):
                got = _layers_half(o_refs[a].at[2 * tx + ty], c, 1)
                theirs = _layers_half(o_refs[a].at[2 * tx + ty], 1 - c, 1)
                out.append((_remote(got, got, send_sems, recv_sems, 3 * a + k, (x, y, 1 - c)),
                            _remote(theirs, theirs, send_sems, recv_sems, 3 * a + k, (x, y, 1 - c))))
        return out

    def start(self, x_refs, o_refs, send_sems, recv_sems):
        for mine, _ in self._copies(o_refs, send_sems, recv_sems):
            mine.start()

    def wait(self, x_refs, o_refs, send_sems, recv_sems):
        for mine, theirs in self._copies(o_refs, send_sems, recv_sems):
            theirs.wait_recv()
            mine.wait_send()


def run_side(side, name):
    n_in, n_out = len(side.ins), len(side.outs)

    def body(*refs):
        parts = refs[:n_in], refs[n_in:n_in + n_out], refs[n_in + n_out], refs[n_in + n_out + 1]
        side.start(*parts)
        side.wait(*parts)

    return _hbm_call(body, name, side.ins, side.outs, side.n_sems, aliases=side.aliases)


def weights_gather(ws, name):
    return run_side(GatherToSibling(run_side(GatherOverIci(ws), name + "_ici")), name + "_sibling")


def halves_to_sibling(gs, name):
    n = len(gs)

    def body(*refs):
        x_refs, o_refs, send_sems, recv_sems = refs[:n], refs[n:2 * n], refs[2 * n], refs[2 * n + 1]
        x, y, c = _my_pos()
        cps = [_remote(_layers_half(x_refs[a], 1 - c, axis=1), o_refs[a], send_sems, recv_sems, a, (x, y, 1 - c))
               for a in range(n)]
        for cp in cps:
            cp.start()
        for cp in cps:
            cp.wait()

    outs = [jax.ShapeDtypeStruct((g.shape[0], g.shape[1] // 2) + g.shape[2:], g.dtype) for g in gs]
    return _hbm_call(body, name, gs, outs, n)


class ExchangeOverIci:
    def __init__(self, xs):
        self.ins = list(xs)
        self.outs = [jax.ShapeDtypeStruct(v.shape, v.dtype) for v in xs]
        self.aliases = {}
        self.n_sems = 3 * len(xs)

    def _copies(self, x_refs, o_refs, send_sems, recv_sems):
        x, y, c = _my_pos()
        s = 2 * x + y
        out = []
        for a in range(len(x_refs)):
            for k, (tx, ty) in enumerate([(1 - x, y), (x, 1 - y), (1 - x, 1 - y)]):
                got = o_refs[a].at[2 * tx + ty]
                out.append((_remote(x_refs[a].at[2 * tx + ty], o_refs[a].at[s], send_sems, recv_sems, 3 * a + k, (tx, ty, c)),
                            _remote(got, got, send_sems, recv_sems, 3 * a + k, (tx, ty, c))))
        return out

    def start(self, x_refs, o_refs, send_sems, recv_sems):
        for mine, _ in self._copies(x_refs, o_refs, send_sems, recv_sems):
            mine.start()

    def wait(self, x_refs, o_refs, send_sems, recv_sems):
        for mine, theirs in self._copies(x_refs, o_refs, send_sems, recv_sems):
            theirs.wait_recv()
            mine.wait_send()


def halves_gather(xs, name):
    n = len(xs)

    def body(*refs):
        x_refs, o_refs, send_sems, recv_sems = refs[:n], refs[n:2 * n], refs[2 * n], refs[2 * n + 1]
        x, y, c = _my_pos()
        sends = []
        for a in range(n):
            cp = _remote(_layers_half(x_refs[a], c), _layers_half(o_refs[a], c), send_sems, recv_sems, a, (x, y, 1 - c))
            cp.start()
            sends.append(cp)
        for a in range(n):
            theirs = _layers_half(o_refs[a], 1 - c)
            _remote(theirs, theirs, send_sems, recv_sems, a, (x, y, 1 - c)).wait_recv()
        for cp in sends:
            cp.wait_send()

    outs = [jax.ShapeDtypeStruct(v.shape, v.dtype) for v in xs]
    return _hbm_call(body, name, xs, outs, n, aliases={a: a for a in range(n)})


def sum_leading(xs, name):
    n, r, c = xs.shape

    def body(x_ref, o_ref):
        acc = x_ref[0]
        for i in range(1, n):
            acc = acc + x_ref[i]
        o_ref[...] = acc

    return pl.pallas_call(body, name=name, out_shape=jax.ShapeDtypeStruct((r, c), xs.dtype))(xs)


def add_cast(a, b, out_dtype, name):
    n, r, c = a.shape
    tr = _pick(r, (512, 256, 128, 64, 32, 16))

    def body(a_ref, b_ref, o_ref):
        o_ref[...] = (a_ref[...].astype(F32) + b_ref[...].astype(F32)).astype(out_dtype)

    spec = pl.BlockSpec((1, tr, c), lambda i, j: (i, j, 0))
    return pl.pallas_call(body, name=name, grid=(n, r // tr), in_specs=[spec, spec], out_specs=spec,
                          out_shape=jax.ShapeDtypeStruct((n, r, c), out_dtype),
                          compiler_params=_params(("parallel", "parallel")))(a, b)


def sum_chips(parts, own, where, name):
    n, nl, r, c = parts.shape
    tr = _pick(r, (512, 256, 128, 64, 32, 16))

    def body(w_ref, p_ref, o_ref, out_ref):
        s = w_ref[0]
        acc = jnp.zeros(out_ref.shape, F32)
        for t in range(n):
            acc = acc + jnp.where(s == t, o_ref[0], p_ref[t]).astype(F32)
        out_ref[...] = acc

    grid_spec = pltpu.PrefetchScalarGridSpec(
        num_scalar_prefetch=1, grid=(nl, r // tr),
        in_specs=[pl.BlockSpec((n, 1, tr, c), lambda i, j, w: (0, i, j, 0)),
                  pl.BlockSpec((1, 1, tr, c), lambda i, j, w: (w[0], i, j, 0))],
        out_specs=pl.BlockSpec((1, tr, c), lambda i, j, w: (w[1] * nl + i, j, 0)))
    return pl.pallas_call(body, name=name, grid_spec=grid_spec, out_shape=jax.ShapeDtypeStruct((2 * nl, r, c), F32),
                          compiler_params=_params(("parallel", "parallel")))(where, parts, own)


def ada_forward(c_all, w_ada):
    nl, d, n = w_ada.shape
    nb = c_all.shape[0]

    def body(c_ref, w_ref, o_ref):
        cv = c_ref[...]
        ca = (cv * _sigmoid(cv)).astype(BF16)
        o_ref[0] = _nn(ca, w_ref[0].astype(BF16))

    return pl.pallas_call(
        body, name="ada_forward", grid=(nl,),
        in_specs=[pl.BlockSpec((nb, d), lambda l: (0, 0)), pl.BlockSpec((1, d, n), lambda l: (l, 0, 0))],
        out_specs=pl.BlockSpec((1, nb, n), lambda l: (l, 0, 0)),
        out_shape=jax.ShapeDtypeStruct((nl, nb, n), F32), compiler_params=_params(("parallel",)))(c_all, w_ada)


def ada_backward(c_all, dmod):
    nl, nb, n = dmod.shape
    d = c_all.shape[1]

    def body(c_ref, g_ref, o_ref):
        cv = c_ref[...]
        ca = (cv * _sigmoid(cv)).astype(BF16)
        o_ref[0] = _tn(ca, g_ref[0].astype(BF16))

    return pl.pallas_call(
        body, name="ada_backward", grid=(nl,),
        in_specs=[pl.BlockSpec((nb, d), lambda l: (0, 0)), pl.BlockSpec((1, nb, n), lambda l: (l, 0, 0))],
        out_specs=pl.BlockSpec((1, d, n), lambda l: (l, 0, 0)),
        out_shape=jax.ShapeDtypeStruct((nl, d, n), F32), compiler_params=_params(("parallel",)))(c_all, dmod)


def matmul_tn(a, b, name):
    k, m = a.shape
    n = b.shape[1]
    tm, tk = _pick(m, (512, 256, 128)), _pick(k, (512, 256, 128))
    tn = n if n * (tm * 8 + tk * 4) <= VMEM_LIMIT // 2 else _pick(n, (512, 256, 128))

    def body(a_ref, b_ref, o_ref, acc_ref):
        @pl.when(pl.program_id(2) == 0)
        def _():
            acc_ref[...] = jnp.zeros_like(acc_ref)

        acc_ref[...] += _tn(a_ref[...], b_ref[...])

        @pl.when(pl.program_id(2) == k // tk - 1)
        def _():
            o_ref[...] = acc_ref[...].astype(BF16)

    return pl.pallas_call(
        body, name=name, grid=(m // tm, n // tn, k // tk),
        in_specs=[pl.BlockSpec((tk, tm), lambda i, j, kk: (kk, i)), pl.BlockSpec((tk, tn), lambda i, j, kk: (kk, j))],
        out_specs=pl.BlockSpec((tm, tn), lambda i, j, kk: (i, j)),
        out_shape=jax.ShapeDtypeStruct((m, n), BF16), scratch_shapes=[pltpu.VMEM((tm, tn), F32)],
        compiler_params=_params(("parallel", "parallel", "arbitrary")))(a, b)


def adamw(w, g, m, v, name):
    nl, r, c = w.shape
    tr = _pick(r, (512, 256, 128, 64, 32, 16, 8))

    def body(w_ref, g_ref, m_ref, v_ref, d_ref, mo_ref, vo_ref):
        gv = g_ref[...]
        mn = ADAM_B1 * m_ref[...] + (1.0 - ADAM_B1) * gv
        vn = ADAM_B2 * v_ref[...] + (1.0 - ADAM_B2) * (gv * gv)
        m_hat = mn / (1.0 - ADAM_B1 ** ADAM_STEP)
        v_hat = vn / (1.0 - ADAM_B2 ** ADAM_STEP)
        d_ref[...] = -ADAM_LR * (m_hat / (jnp.sqrt(v_hat) + ADAM_EPS) + ADAM_WD * w_ref[...])
        mo_ref[...] = mn
        vo_ref[...] = vn

    spec = pl.BlockSpec((1, tr, c), lambda l, i: (l, i, 0))
    out = jax.ShapeDtypeStruct((nl, r, c), F32)
    return pl.pallas_call(body, name=name, grid=(nl, r // tr), in_specs=[spec] * 4, out_specs=[spec] * 3,
                          out_shape=[out] * 3, compiler_params=_params(("parallel", "parallel")))(w, g, m, v)


def _rope(t, cos_t, sin_t):
    w = t.shape[1]
    lane = _lane(t.shape) & (HEAD_PAD - 1)
    first_half = (lane >= ROPE_LO) & (lane < ROPE_LO + ROPE_HALF)
    partner = jnp.where(first_half, pltpu.roll(t, w - ROPE_HALF, 1), pltpu.roll(t, ROPE_HALF, 1))
    return t * cos_t + partner * sin_t


def _rope_t(dt, cos_t, sin_t):
    w = dt.shape[1]
    lane = _lane(dt.shape) & (HEAD_PAD - 1)
    first_half = (lane >= ROPE_LO) & (lane < ROPE_LO + ROPE_HALF)
    ds = dt * sin_t
    partner = jnp.where(first_half, pltpu.roll(ds, w - ROPE_HALF, 1), pltpu.roll(ds, ROPE_HALF, 1))
    return dt * cos_t + partner


def _rms(xv, g):
    rstd = lax.rsqrt(jnp.mean(xv * xv, axis=-1, keepdims=True) + EPS)
    xh = xv * rstd
    return xh * g, xh, rstd


def _rms_bwd(dy, g, xh, rstd):
    dxh = dy * g
    dx = rstd * (dxh - xh * jnp.mean(dxh * xh, axis=-1, keepdims=True))
    return dx, jnp.sum(dy * xh, axis=0, keepdims=True)


def make_step(cfg):
    S, D, NZ = cfg.S, cfg.D, cfg.NZ
    FW, MW, QL, KVL, FH, MH = cfg.FW, cfg.MW, cfg.QL, cfg.KVL, cfg.FH, cfg.MH
    QW = MH * HEAD_PAD
    TM = _pick(S, (512, 256, 128))
    TQ = TK = _pick(S, (256, 128))
    n_tok = S // TM
    ZO = 3 * FW
    NZR = NZ - ZO
    misc_blk = (cfg.o_ms - ZO) // LANES
    FWD_UNROLL = 8
    FOX_SCALE = HEAD_DIM ** -0.5
    MLA_SCALE = (HEAD_DIM + ROPE_DIM) ** -0.5

    def tok(width, col=0):
        return pl.BlockSpec((TM, width), lambda i: (i, col))

    def const(shape):
        return pl.BlockSpec(shape, lambda i: tuple(0 for _ in shape))

    def layer(l, shape):
        return pl.BlockSpec((None,) + shape, lambda i: (l,) + tuple(0 for _ in shape))


    def ln_inproj(x, g, scale, shift, w_in, l):
        def body(x_ref, g_ref, sc_ref, sh_ref, w_ref, h_ref, z_ref, qkv_ref):
            y, _, _ = _rms(x_ref[...], g_ref[...])
            hb = (y * (1.0 + sc_ref[...]) + sh_ref[...]).astype(BF16)
            h_ref[...] = hb
            z = _nn(hb, w_ref[...])
            z_ref[...] = z[:, ZO:]
            qkv_ref[:, :FW] = (z[:, :FW] * FOX_SCALE).astype(BF16)
            qkv_ref[:, FW:] = z[:, FW:ZO].astype(BF16)

        return pl.pallas_call(
            body, name="ln_inproj", grid=(n_tok,),
            in_specs=[tok(D), const((1, D)), const((1, D)), const((1, D)), layer(l, (D, NZ))],
            out_specs=[tok(D), tok(NZR), tok(ZO)],
            out_shape=[jax.ShapeDtypeStruct((S, D), BF16), jax.ShapeDtypeStruct((S, NZR), F32),
                       jax.ShapeDtypeStruct((S, ZO), BF16)],
            compiler_params=_params(("parallel",)))(x, g, scale, shift, w_in)

    def _log_f_terms(misc, bf):
        lane = _lane(misc.shape)
        a = misc + bf
        e = jnp.exp(-jnp.abs(a))
        logf = jnp.minimum(a, 0.0) - jnp.log(1.0 + e)
        sig_neg = jnp.where(a >= 0, e, 1.0) / (1.0 + e)
        valid = lane < FH
        return jnp.where(valid, logf, 0.0), jnp.where(valid, sig_neg, 0.0)

    def fox_prep(z, bf_pad):
        def body(z_ref, b_ref, o_ref, t_ref, carry):
            @pl.when(pl.program_id(0) == 0)
            def _():
                carry[...] = jnp.zeros_like(carry)

            logf, _ = _log_f_terms(z_ref[...], b_ref[...])
            row = lax.broadcasted_iota(jnp.int32, (TM, TM), 0)
            col = lax.broadcasted_iota(jnp.int32, (TM, TM), 1)
            tri = (col <= row).astype(F32)
            cum = jnp.dot(tri, logf, precision=lax.Precision.HIGHEST, preferred_element_type=F32) + carry[...]
            o_ref[...] = cum
            carry[...] = cum[TM - 1:TM, :]
            for hd in range(FH):
                t_ref[hd // 2, hd % 2] = jnp.broadcast_to(cum[:, hd:hd + 1], (TM, LANES))

        return pl.pallas_call(
            body, name="fox_prep", grid=(n_tok,),
            in_specs=[tok(LANES, misc_blk), const((1, LANES))],
            out_specs=[tok(LANES), pl.BlockSpec((FH // 2, 2, TM, LANES), lambda i: (0, 0, i, 0))],
            out_shape=[jax.ShapeDtypeStruct((S, LANES), F32), jax.ShapeDtypeStruct((FH // 2, 2, S, LANES), F32)],
            scratch_shapes=[pltpu.VMEM((1, LANES), F32)],
            compiler_params=_params(("arbitrary",)))(z, bf_pad)

    def mla_prep(z, gq, gkv, w_uq, w_uk, w_v, l, cos_t, sin_t):
        def body(ql_ref, kvl_ref, ms_ref, gq_ref, gkv_ref, wq_ref, wk_ref, wv_ref, c_ref, s_ref, q_ref, k_ref, v_ref):
            cos1, sin1 = c_ref[...], s_ref[...]
            qn, _, _ = _rms(ql_ref[...], gq_ref[...])
            q = _nn(qn.astype(BF16), wq_ref[...])
            q_ref[...] = (_rope(q, jnp.tile(cos1, (1, MH)), jnp.tile(sin1, (1, MH))) * MLA_SCALE).astype(BF16)
            kvn, _, _ = _rms(kvl_ref[...], gkv_ref[...])
            kvb = kvn.astype(BF16)
            lane = _lane((TM, LANES))
            kr = jnp.where((lane >= ROPE_LO) & (lane < ROPE_LO + ROPE_DIM), ms_ref[...], 0.0)
            kr = _rope(kr, cos1, sin1)
            k_ref[...] = (_nn(kvb, wk_ref[...]) + jnp.tile(kr, (1, MH))).astype(BF16)
            v_ref[...] = _nn(kvb, wv_ref[...]).astype(BF16)

        return pl.pallas_call(
            body, name="mla_prep", grid=(n_tok,),
            in_specs=[tok(QL, (cfg.o_ql - ZO) // QL), tok(KVL, (cfg.o_kv - ZO) // KVL), tok(LANES, misc_blk),
                      const((1, QL)), const((1, KVL)), layer(l, (QL, QW)), layer(l, (KVL, QW)), layer(l, (KVL, MW)),
                      tok(LANES), tok(LANES)],
            out_specs=[tok(QW), tok(QW), tok(MW)],
            out_shape=[jax.ShapeDtypeStruct((S, QW), BF16), jax.ShapeDtypeStruct((S, QW), BF16),
                       jax.ShapeDtypeStruct((S, MW), BF16)],
            compiler_params=_params(("parallel",)))(z, z, z, gq, gkv, w_uq, w_uk, w_v, cos_t, sin_t)

    def _allowed(q0, k0, chunked):
        qi = q0 + lax.broadcasted_iota(jnp.int32, (TQ, TK), 0)
        ki = k0 + lax.broadcasted_iota(jnp.int32, (TQ, TK), 1)
        if chunked:
            return (ki >> CHUNK_SHIFT) <= (qi >> CHUNK_SHIFT)
        return ki <= qi

    def _heads(val, packed):
        if packed:
            lane = _lane(val.shape)
            zero = jnp.zeros_like(val)
            return [jnp.where(lane < HEAD_DIM, val, zero), jnp.where(lane >= HEAD_DIM, val, zero)]
        return [val[:, :HEAD_PAD], val[:, HEAD_PAD:]]

    def _merge(a0, a1):
        return jnp.where(_lane(a0.shape) < HEAD_DIM, a0, a1)

    def attn_fwd(q, k, v, q_blk0, k_blk0, v_blk0, cumt, packed, chunked, name, side=None):
        has_bias = cumt is not None
        n_pairs = (FH if packed else MH) // 2
        wq = LANES if packed else 2 * HEAD_PAD
        assert TQ == TK

        n_main = 4 if has_bias else 3
        n_side_in, n_side_out = (len(side.ins), len(side.outs)) if side else (0, 0)

        def body(*refs):
            if has_bias:
                q_ref, k_ref, v_ref, ct_ref = refs[:n_main]
            else:
                q_ref, k_ref, v_ref = refs[:n_main]
            o_ref, lse_ref = refs[n_main + n_side_in:n_main + n_side_in + 2]
            i = pl.program_id(1)
            if side:
                side_refs = (refs[n_main:n_main + n_side_in], refs[n_main + n_side_in + 2:n_main + n_side_in + 2 + n_side_out],
                             refs[-2], refs[-1])

                @pl.when((pl.program_id(0) == 0) & (i == 0))
                def _():
                    side.start(*side_refs)

            q0 = i * TQ
            qh = _heads(q_ref[...], packed)
            q_both = jnp.concatenate(qh, axis=0) if packed else None

            def scores(kb):
                kblk = k_ref[pl.ds(pl.multiple_of(kb * TK, TK), TK), :]
                if packed:
                    return _nt(kblk, q_both)
                return jnp.concatenate([_nt(kblk[:, :HEAD_PAD], qh[0]), _nt(kblk[:, HEAD_PAD:], qh[1])], axis=1)

            def update(kb, s, carry, masked):
                m, l, acc = carry
                k0 = pl.multiple_of(kb * TK, TK)
                s0, s1 = s[:, :TQ], s[:, TQ:]
                if has_bias:
                    ck = ct_ref[0, :, pl.ds(k0, TK), :]
                    s0, s1 = s0 - jnp.tile(ck[0], (1, TQ // LANES)), s1 - jnp.tile(ck[1], (1, TQ // LANES))
                if masked:
                    allow = jnp.transpose(_allowed(q0, k0, chunked))
                    s0, s1 = jnp.where(allow, s0, NEG), jnp.where(allow, s1, NEG)
                s = jnp.concatenate([s0, s1], axis=1)
                m_new = jnp.maximum(m, jnp.max(s, axis=0, keepdims=True))
                p = jnp.exp(s - m_new)
                alpha = jnp.exp(m - m_new)
                l = alpha * l + jnp.sum(p, axis=0, keepdims=True)
                return m_new, l, alpha * acc + _tn(v_ref[pl.ds(k0, TK), :], p.astype(BF16))

            def several(kb, carry, n, last_masked):
                ss = [scores(kb + u) for u in range(n)]
                for u in range(n):
                    carry = update(kb + u, ss[u], carry, last_masked and u == n - 1)
                return carry

            init = (jnp.full((1, 2 * TQ), NEG, F32), jnp.zeros((1, 2 * TQ), F32), jnp.zeros((LANES, 2 * TQ), F32))
            carry = lax.fori_loop(0, i // FWD_UNROLL, lambda t, cr: several(FWD_UNROLL * t, cr, FWD_UNROLL, False), init)
            m, l, acc = lax.switch(i % FWD_UNROLL,
                                   [lambda cr, r=r: several(i - r, cr, r + 1, True) for r in range(FWD_UNROLL)], carry)
            o = acc / l
            lse = m + jnp.log(l)
            o_ref[...] = jnp.transpose(jnp.concatenate([o[:HEAD_DIM, :TQ], o[HEAD_DIM:, TQ:]], axis=0))
            sub = lax.broadcasted_iota(jnp.int32, (LANES, TQ), 0)
            lse_ref[...] = jnp.transpose(jnp.where(sub == 0, lse[:, :TQ], jnp.where(sub == 1, lse[:, TQ:], 0.0)))
            if side:
                @pl.when((pl.program_id(0) == n_pairs - 1) & (i == S // TQ - 1))
                def _():
                    side.wait(*side_refs)

        in_specs = [pl.BlockSpec((TQ, wq), lambda p, i: (i, q_blk0 + p)),
                    pl.BlockSpec((S, wq), lambda p, i: (0, k_blk0 + p)),
                    pl.BlockSpec((S, LANES), lambda p, i: (0, v_blk0 + p))]
        args = [q, k, v]
        if has_bias:
            in_specs += [pl.BlockSpec((1, 2, S, LANES), lambda p, i: (p, 0, 0, 0))]
            args += [cumt]
        out_specs = [pl.BlockSpec((TQ, LANES), lambda p, i: (i, p)), pl.BlockSpec((TQ, LANES), lambda p, i: (i, p))]
        out_shape = [jax.ShapeDtypeStruct((S, n_pairs * LANES), F32), jax.ShapeDtypeStruct((S, n_pairs * LANES), F32)]
        extra = {}
        if side:
            hbm = pl.BlockSpec(memory_space=pl.ANY)
            in_specs += [hbm] * n_side_in
            args += side.ins
            out_specs += [hbm] * n_side_out
            out_shape += side.outs
            extra = dict(scratch_shapes=[pltpu.SemaphoreType.DMA((side.n_sems,)), pltpu.SemaphoreType.DMA((side.n_sems,))],
                         input_output_aliases={n_main + a: 2 + b for a, b in side.aliases.items()})
        outs = pl.pallas_call(
            body, name=name, grid=(n_pairs, S // TQ), in_specs=in_specs, out_specs=out_specs, out_shape=out_shape,
            compiler_params=_params(("arbitrary", "arbitrary")), **extra)(*args)
        return (outs[0], outs[1], list(outs[2:])) if side else outs

    def gate_outproj(of, om, z, w_out, l, x, gate):
        def body(of_ref, om_ref, fg_ref, mg_ref, w_ref, x_ref, gt_ref, xn_ref, u_ref, y_ref):
            fg, mg = fg_ref[...], mg_ref[...]
            u = jnp.concatenate([of_ref[...] * fg * _sigmoid(fg), om_ref[...] * mg * _sigmoid(mg)], axis=1).astype(BF16)
            y = _nn(u, w_ref[...])
            u_ref[...] = u
            y_ref[...] = y.astype(BF16)
            xn_ref[...] = x_ref[...] + gt_ref[...] * y

        return pl.pallas_call(
            body, name="gate_outproj", grid=(n_tok,),
            in_specs=[tok(FW), tok(MW), tok(FW, (cfg.o_fg - ZO) // FW), tok(MW, (cfg.o_mg - ZO) // MW),
                      layer(l, (FW + MW, D)), tok(D), const((1, D))],
            out_specs=[tok(D), tok(FW + MW), tok(D)],
            out_shape=[jax.ShapeDtypeStruct((S, D), F32), jax.ShapeDtypeStruct((S, FW + MW), BF16),
                       jax.ShapeDtypeStruct((S, D), BF16)],
            compiler_params=_params(("parallel",)))(of, om, z, z, w_out, x, gate)

    def final_loss(x, g, target):
        def body(x_ref, g_ref, t_ref, dx_ref, acc_ref, loss_ref):
            @pl.when(pl.program_id(0) == 0)
            def _():
                acc_ref[...] = jnp.zeros_like(acc_ref)
                loss_ref[...] = jnp.zeros_like(loss_ref)

            gv = g_ref[...]
            y, xh, rstd = _rms(x_ref[...], gv)
            e = y - t_ref[...]
            loss_ref[...] += 0.5 * jnp.sum(jnp.sum(e * e, axis=-1, keepdims=True) / D, axis=0, keepdims=True)
            dx, dg = _rms_bwd(e / D, gv, xh, rstd)
            dx_ref[...] = dx
            acc_ref[0:1, :] += dg

        return pl.pallas_call(
            body, name="final_loss", grid=(n_tok,),
            in_specs=[tok(D), const((1, D)), tok(D)],
            out_specs=[tok(D), const((8, D)), const((1, LANES))],
            out_shape=[jax.ShapeDtypeStruct((S, D), F32), jax.ShapeDtypeStruct((8, D), F32),
                       jax.ShapeDtypeStruct((1, LANES), F32)],
            compiler_params=_params(("arbitrary",)))(x, g, target)


    def bwd_out(dxn, gate, y, w_out, l, of, om, z, lse_f, lse_m):
        def body(dx_ref, gt_ref, y_ref, w_ref, of_ref, om_ref, fg_ref, mg_ref, lf_ref, lm_ref,
                 dof_ref, dom_ref, dfg_ref, dmg_ref, dy_ref, acc_ref, sf_ref, sm_ref):
            @pl.when(pl.program_id(0) == 0)
            def _():
                acc_ref[...] = jnp.zeros_like(acc_ref)

            dxv = dx_ref[...]
            acc_ref[0:1, :] += jnp.sum(dxv * y_ref[...].astype(F32), axis=0, keepdims=True)
            dy = (gt_ref[...] * dxv).astype(BF16)
            dy_ref[...] = dy
            du = _nt(dy, w_ref[...])
            lane = _lane((TM, LANES))
            for lo, width, o_ref, g_ref, do_ref, dg_ref, l_ref, st_ref in (
                    (0, FW, of_ref, fg_ref, dof_ref, dfg_ref, lf_ref, sf_ref),
                    (FW, MW, om_ref, mg_ref, dom_ref, dmg_ref, lm_ref, sm_ref)):
                gv, ov = g_ref[...], o_ref[...]
                sg = _sigmoid(gv)
                dup = du[:, lo:lo + width]
                dob = (dup * gv * sg).astype(BF16)
                do_ref[...] = dob
                dg_ref[...] = (dup * ov * sg * (1.0 + gv * (1.0 - sg))).astype(BF16)
                d = dob.astype(F32) * ov
                for pr in range(width // LANES):
                    cols = slice(pr * LANES, (pr + 1) * LANES)
                    dp = d[:, cols]
                    d0 = jnp.sum(jnp.where(lane < HEAD_DIM, dp, 0.0), axis=-1, keepdims=True)
                    d1 = jnp.sum(jnp.where(lane >= HEAD_DIM, dp, 0.0), axis=-1, keepdims=True)
                    st_ref[:, cols] = jnp.where(lane == 2, d0, jnp.where(lane == 3, d1, l_ref[:, cols]))

        return pl.pallas_call(
            body, name="bwd_out", grid=(n_tok,),
            in_specs=[tok(D), const((1, D)), tok(D), layer(l, (FW + MW, D)), tok(FW), tok(MW),
                      tok(FW, (cfg.o_fg - ZO) // FW), tok(MW, (cfg.o_mg - ZO) // MW), tok(FW), tok(MW)],
            out_specs=[tok(FW), tok(MW), tok(FW), tok(MW), tok(D), const((8, D)), tok(FW), tok(MW)],
            out_shape=[jax.ShapeDtypeStruct((S, FW), BF16), jax.ShapeDtypeStruct((S, MW), BF16),
                       jax.ShapeDtypeStruct((S, FW), BF16), jax.ShapeDtypeStruct((S, MW), BF16),
                       jax.ShapeDtypeStruct((S, D), BF16), jax.ShapeDtypeStruct((8, D), F32),
                       jax.ShapeDtypeStruct((S, FW), F32), jax.ShapeDtypeStruct((S, MW), F32)],
            compiler_params=_params(("arbitrary",)))(dxn, gate, y, w_out, of, om, z, z, lse_f, lse_m)

    def attn_bwd(q, k, v, do, stats, q_blk0, k_blk0, v_blk0, cumt, packed, chunked, name, side=None):
        has_bias = cumt is not None
        n_pairs = (FH if packed else MH) // 2
        wq = LANES if packed else 2 * HEAD_PAD
        n_q = S // TQ
        assert TQ == TK
        n_in, n_out = (6, 5) if has_bias else (5, 3)
        n_side_in, n_side_out = (len(side.ins), len(side.outs)) if side else (0, 0)

        def body(*refs):
            main_out = refs[n_in + n_side_in:n_in + n_side_in + n_out]
            if has_bias:
                q_ref, k_ref, v_ref, do_ref, st_ref, ct_ref = refs[:n_in]
                dq_ref, dk_ref, dv_ref, dc_ref, dr_ref = main_out
            else:
                q_ref, k_ref, v_ref, do_ref, st_ref = refs[:n_in]
                dq_ref, dk_ref, dv_ref = main_out
            jb = pl.program_id(1)
            k0 = jb * TK
            if side:
                side_refs = (refs[n_in:n_in + n_side_in], refs[n_in + n_side_in + n_out:n_in + n_side_in + n_out + n_side_out],
                             refs[-2], refs[-1])

                @pl.when((pl.program_id(0) == 0) & (jb == 0))
                def _():
                    side.start(*side_refs)

            @pl.when(jb == 0)
            def _():
                dq_ref[...] = jnp.zeros_like(dq_ref)
                if has_bias:
                    dr_ref[...] = jnp.zeros_like(dr_ref)

            dk_ref[...] = jnp.zeros_like(dk_ref)
            dv_ref[...] = jnp.zeros_like(dv_ref)
            kh = _heads(k_ref[...], packed)
            k_both = jnp.concatenate(kh, axis=0) if packed else None
            v_both = jnp.concatenate(_heads(v_ref[...], True), axis=0)
            if has_bias:
                ct = jnp.transpose(ct_ref[0, 0])
                ck = jnp.concatenate([ct[:, 0:1], ct[:, 1:2]], axis=0)

            def products(ib):
                rows = pl.ds(pl.multiple_of(ib * TQ, TQ), TQ)
                q2, do2 = q_ref[rows, :], do_ref[rows, :]
                if packed:
                    s = _nt(k_both, q2)
                else:
                    qh = _heads(q2, False)
                    s = jnp.concatenate([_nt(kh[0], qh[0]), _nt(kh[1], qh[1])], axis=0)
                return s, _nt(v_both, do2)

            def update(ib, s, dp, carry, masked):
                q0 = pl.multiple_of(ib * TQ, TQ)
                rows = pl.ds(q0, TQ)
                q2, do2 = q_ref[rows, :], do_ref[rows, :]
                st = jnp.transpose(st_ref[rows, :])
                if not packed:
                    qh = _heads(q2, False)
                if has_bias:
                    s = s - ck
                if masked:
                    allow = jnp.transpose(_allowed(q0, k0, chunked))
                    s = jnp.where(jnp.concatenate([allow, allow], axis=0), s, NEG)
                p = jnp.concatenate([jnp.exp(s[:TK] - st[0:1, :]), jnp.exp(s[TK:] - st[1:2, :])], axis=0)
                dv2 = _nn(p.astype(BF16), do2)
                ds = jnp.concatenate([p[:TK] * (dp[:TK] - st[2:3, :]), p[TK:] * (dp[TK:] - st[3:4, :])], axis=0)
                dsb = ds.astype(BF16)
                dv_ref[...] += _merge(dv2[:TK], dv2[TK:])
                if packed:
                    dk2 = _nn(dsb, q2)
                    dk_ref[...] += _merge(dk2[:TK], dk2[TK:])
                    dq_ref[rows, :] += _tn(dsb, k_both)
                else:
                    dk_ref[...] += jnp.concatenate([_nn(dsb[:TK], qh[0]), _nn(dsb[TK:], qh[1])], axis=1)
                    dq_ref[rows, :] += jnp.concatenate([_tn(dsb[:TK], kh[0]), _tn(dsb[TK:], kh[1])], axis=1)
                if has_bias:
                    sub = lax.broadcasted_iota(jnp.int32, (8, TQ), 0)
                    r0 = jnp.sum(ds[:TK], axis=0, keepdims=True)
                    r1 = jnp.sum(ds[TK:], axis=0, keepdims=True)
                    dr_ref[0, ib] += jnp.where(sub == 0, r0, jnp.where(sub == 1, r1, 0.0))
                    return carry - jnp.sum(ds, axis=1, keepdims=True)
                return carry

            def step(ib, carry, masked):
                s, dp = products(ib)
                return update(ib, s, dp, carry, masked)

            def two(ib, carry, first_masked):
                s_a, dp_a = products(ib)
                s_b, dp_b = products(ib + 1)
                return update(ib + 1, s_b, dp_b, update(ib, s_a, dp_a, carry, first_masked), False)

            n_blocks = n_q - jb
            dc = lax.cond(n_blocks >= 2, lambda cr: two(jb, cr, True), lambda cr: step(jb, cr, True),
                          jnp.zeros((2 * TK, 1), F32))
            n_rest = jnp.maximum(n_blocks - 2, 0)
            dc = lax.fori_loop(0, n_rest // 2, lambda t, cr: two(jb + 2 + 2 * t, cr, False), dc)
            dc = lax.cond(n_rest % 2 == 1, lambda cr: step(n_q - 1, cr, False), lambda cr: cr, dc)
            if has_bias:
                lane = _lane((TK, LANES))
                dc_ref[0] = jnp.where(lane == 0, dc[:TK], jnp.where(lane == 1, dc[TK:], 0.0))
            if side:
                @pl.when((pl.program_id(0) == n_pairs - 1) & (jb == S // TK - 1))
                def _():
                    side.wait(*side_refs)

        in_specs = [pl.BlockSpec((S, wq), lambda p, j: (0, q_blk0 + p)),
                    pl.BlockSpec((TK, wq), lambda p, j: (j, k_blk0 + p)),
                    pl.BlockSpec((TK, LANES), lambda p, j: (j, v_blk0 + p)),
                    pl.BlockSpec((S, LANES), lambda p, j: (0, p)),
                    pl.BlockSpec((S, LANES), lambda p, j: (0, p))]
        args = [q, k, v, do, stats]
        out_specs = [pl.BlockSpec((S, wq), lambda p, j: (0, p)),
                     pl.BlockSpec((TK, wq), lambda p, j: (j, p)),
                     pl.BlockSpec((TK, LANES), lambda p, j: (j, p))]
        out_shape = [jax.ShapeDtypeStruct((S, n_pairs * wq), F32), jax.ShapeDtypeStruct((S, n_pairs * wq), F32),
                     jax.ShapeDtypeStruct((S, n_pairs * LANES), F32)]
        if has_bias:
            in_specs += [pl.BlockSpec((1, 1, 8, TK), lambda p, j: (p, j, 0, 0))]
            args += [cumt]
            out_specs += [pl.BlockSpec((1, TK, LANES), lambda p, j: (p, j, 0)),
                          pl.BlockSpec((1, S // TQ, 8, TQ), lambda p, j: (p, 0, 0, 0))]
            out_shape += [jax.ShapeDtypeStruct((n_pairs, S, LANES), F32),
                          jax.ShapeDtypeStruct((n_pairs, S // TQ, 8, TQ), F32)]
        extra = {}
        if side:
            hbm = pl.BlockSpec(memory_space=pl.ANY)
            in_specs += [hbm] * n_side_in
            args += side.ins
            out_specs += [hbm] * n_side_out
            out_shape += side.outs
            extra = dict(scratch_shapes=[pltpu.SemaphoreType.DMA((side.n_sems,)), pltpu.SemaphoreType.DMA((side.n_sems,))],
                         input_output_aliases={n_in + a: n_out + b for a, b in side.aliases.items()})
        outs = pl.pallas_call(
            body, name=name, grid=(n_pairs, S // TK), in_specs=in_specs, out_specs=out_specs, out_shape=out_shape,
            compiler_params=_params(("arbitrary", "arbitrary")), **extra)(*args)
        return (*outs[:n_out], list(outs[n_out:])) if side else outs

    def fox_post(dcum, z, bf_pad):
        def rev(i):
            return n_tok - 1 - i

        def body(dc_ref, z_ref, b_ref, dff_ref, acc_ref, carry):
            @pl.when(pl.program_id(0) == 0)
            def _():
                carry[...] = jnp.zeros_like(carry)
                acc_ref[...] = jnp.zeros_like(acc_ref)

            _, sig_neg = _log_f_terms(z_ref[...], b_ref[...])
            row = lax.broadcasted_iota(jnp.int32, (TM, TM), 0)
            col = lax.broadcasted_iota(jnp.int32, (TM, TM), 1)
            tri = (col >= row).astype(F32)
            dlog = jnp.dot(tri, dc_ref[...], precision=lax.Precision.HIGHEST, preferred_element_type=F32) + carry[...]
            carry[...] = dlog[0:1, :]
            dff = dlog * sig_neg
            dff_ref[...] = dff
            acc_ref[0:1, :] += jnp.sum(dff, axis=0, keepdims=True)

        return pl.pallas_call(
            body, name="fox_post", grid=(n_tok,),
            in_specs=[pl.BlockSpec((TM, LANES), lambda i: (rev(i), 0)),
                      pl.BlockSpec((TM, LANES), lambda i: (rev(i), misc_blk)), const((1, LANES))],
            out_specs=[pl.BlockSpec((TM, LANES), lambda i: (rev(i), 0)), const((8, LANES))],
            out_shape=[jax.ShapeDtypeStruct((S, LANES), F32), jax.ShapeDtypeStruct((8, LANES), F32)],
            scratch_shapes=[pltpu.VMEM((1, LANES), F32)],
            compiler_params=_params(("arbitrary",)))(dcum, z, bf_pad)

    def mla_post(dq, dk, dv, dff, z, gq, gkv, w_uq, w_uk, w_v, l, cos_t, sin_t):
        def body(dq_ref, dk_ref, dv_ref, dff_ref, ql_ref, kvl_ref, gq_ref, gkv_ref, wq_ref, wk_ref, wv_ref,
                 c_ref, s_ref, zq_ref, zkv_ref, zms_ref, dwq_ref, dwk_ref, dwv_ref, dgq_ref, dgkv_ref):
            @pl.when(pl.program_id(0) == 0)
            def _():
                for r in (dwq_ref, dwk_ref, dwv_ref, dgq_ref, dgkv_ref):
                    r[...] = jnp.zeros_like(r)

            cos1, sin1 = c_ref[...], s_ref[...]
            gqv, gkvv = gq_ref[...], gkv_ref[...]
            qn, qxh, qrstd = _rms(ql_ref[...], gqv)
            dq_pre = _rope_t(dq_ref[...] * MLA_SCALE, jnp.tile(cos1, (1, MH)), jnp.tile(sin1, (1, MH))).astype(BF16)
            dwq_ref[...] += _tn(qn.astype(BF16), dq_pre)
            dql, dgq = _rms_bwd(_nt(dq_pre, wq_ref[...]), gqv, qxh, qrstd)
            zq_ref[...] = dql.astype(BF16)
            dgq_ref[0:1, :] += dgq

            dkv = dk_ref[...]
            lane = _lane(dkv.shape) & (HEAD_PAD - 1)
            dkn = jnp.where(lane < HEAD_DIM, dkv, 0.0).astype(BF16)
            dkr = dkv[:, 0:HEAD_PAD]
            for hd in range(1, MH):
                dkr = dkr + dkv[:, hd * HEAD_PAD:(hd + 1) * HEAD_PAD]
            lane1 = _lane(dkr.shape)
            dkr = jnp.where((lane1 >= ROPE_LO) & (lane1 < ROPE_LO + ROPE_DIM), dkr, 0.0)
            dkr = _rope_t(dkr, cos1, sin1)
            zms_ref[...] = (dkr + dff_ref[...]).astype(BF16)

            kvn, kxh, krstd = _rms(kvl_ref[...], gkvv)
            kvb = kvn.astype(BF16)
            dvb = dv_ref[...].astype(BF16)
            dwk_ref[...] += _tn(kvb, dkn)
            dwv_ref[...] += _tn(kvb, dvb)
            dkvl, dgkv = _rms_bwd(_nt(dkn, wk_ref[...]) + _nt(dvb, wv_ref[...]), gkvv, kxh, krstd)
            zkv_ref[...] = dkvl.astype(BF16)
            dgkv_ref[0:1, :] += dgkv

        return pl.pallas_call(
            body, name="mla_post", grid=(n_tok,),
            in_specs=[tok(QW), tok(QW), tok(MW), tok(LANES), tok(QL, (cfg.o_ql - ZO) // QL), tok(KVL, (cfg.o_kv - ZO) // KVL),
                      const((1, QL)), const((1, KVL)), layer(l, (QL, QW)), layer(l, (KVL, QW)), layer(l, (KVL, MW)),
                      tok(LANES), tok(LANES)],
            out_specs=[tok(QL), tok(KVL), tok(LANES), const((QL, QW)), const((KVL, QW)), const((KVL, MW)),
                       const((8, QL)), const((8, KVL))],
            out_shape=[jax.ShapeDtypeStruct((S, QL), BF16), jax.ShapeDtypeStruct((S, KVL), BF16),
                       jax.ShapeDtypeStruct((S, LANES), BF16), jax.ShapeDtypeStruct((QL, QW), F32),
                       jax.ShapeDtypeStruct((KVL, QW), F32), jax.ShapeDtypeStruct((KVL, MW), F32),
                       jax.ShapeDtypeStruct((8, QL), F32), jax.ShapeDtypeStruct((8, KVL), F32)],
            compiler_params=_params(("arbitrary",)))(dq, dk, dv, dff, z, z, gq, gkv, w_uq, w_uk, w_v, cos_t, sin_t)

    def bwd_in(dz, w_in, l, x, dxn, g, scale):
        def body(dz_ref, w_ref, x_ref, dx_ref, g_ref, sc_ref, o_ref, acc_ref):
            @pl.when(pl.program_id(0) == 0)
            def _():
                acc_ref[...] = jnp.zeros_like(acc_ref)

            dh = _nt(dz_ref[...], w_ref[...])
            gv, mod = g_ref[...], 1.0 + sc_ref[...]
            _, xh, rstd = _rms(x_ref[...], gv)
            t = dh * xh
            acc_ref[0:1, :] += jnp.sum(dh, axis=0, keepdims=True)
            acc_ref[1:2, :] += jnp.sum(t * gv, axis=0, keepdims=True)
            acc_ref[2:3, :] += jnp.sum(t * mod, axis=0, keepdims=True)
            dx, _ = _rms_bwd(dh, gv * mod, xh, rstd)
            o_ref[...] = dx_ref[...] + dx

        return pl.pallas_call(
            body, name="bwd_in", grid=(n_tok,),
            in_specs=[tok(NZ), layer(l, (D, NZ)), tok(D), tok(D), const((1, D)), const((1, D))],
            out_specs=[tok(D), const((8, D))],
            out_shape=[jax.ShapeDtypeStruct((S, D), F32), jax.ShapeDtypeStruct((8, D), F32)],
            compiler_params=_params(("arbitrary",)))(dz, w_in, x, dxn, g, scale)


    def pair_rows(cum):
        n_pairs = FH // 2
        ct = jnp.pad(cum[:, :FH].T.reshape(n_pairs, 2, S), ((0, 0), (0, 6), (0, 0)))
        return ct.reshape(n_pairs, 8, S // TK, TK).transpose(0, 2, 1, 3)

    def bias_grad(dc, dr):
        n_pairs = FH // 2
        d = dr.transpose(0, 2, 1, 3).reshape(n_pairs, 8, S)[:, :2, :].reshape(FH, S).T
        d = d + dc[:, :, :2].transpose(1, 0, 2).reshape(S, FH)
        return jnp.pad(d, ((0, 0), (0, LANES - FH)))

    def layer_forward(x, wl, mod, later_shards=None):
        shift, scale, gate = mod
        h, z, qkv = ln_inproj(x, wl.norm_g, scale, shift, wl.w_in, wl.l)
        cum, tiles = fox_prep(z, wl.bf_pad)
        ct = pair_rows(cum)
        fox = attn_fwd(qkv, qkv, qkv, cfg.o_fq // LANES, cfg.o_fk // LANES, cfg.o_fv // LANES, tiles, True, False,
                       "fox_fwd_gather" if later_shards else "fox_fwd", GatherOverIci(later_shards) if later_shards else None)
        of, lse_f = fox[0], fox[1]
        qp, kp, vp = mla_prep(z, wl.gq, wl.gkv, wl.w_uq, wl.w_uk, wl.w_v, wl.l, cfg.cos_t, cfg.sin_t)
        mla = attn_fwd(qp, kp, vp, 0, 0, 0, None, False, True, "mla_fwd_gather" if later_shards else "mla_fwd",
                       GatherToSibling(fox[2]) if later_shards else None)
        om, lse_m = mla[0], mla[1]
        xn, u, y = gate_outproj(of, om, z, wl.w_out, wl.l, x, gate)
        saved = types.SimpleNamespace(x=x, h=h, z=z, qkv=qkv, ct=ct, of=of, lse_f=lse_f, qp=qp, kp=kp, vp=vp,
                                      om=om, lse_m=lse_m, u=u, y=y)
        return (xn, saved, mla[2]) if later_shards else (xn, saved)

    def layer_backward(dxn, sv, wl, mod, side=None):
        shift, scale, gate = mod
        do_f, do_m, dfg, dmg, dy, acc_o, st_f, st_m = bwd_out(dxn, gate, sv.y, wl.w_out, wl.l, sv.of, sv.om, sv.z,
                                                              sv.lse_f, sv.lse_m)
        dw_out = matmul_tn(sv.u, dy, "dw_out")
        dfq, dfk, dfv, dck, dcr, *rode = attn_bwd(sv.qkv, sv.qkv, sv.qkv, do_f, st_f, cfg.o_fq // LANES, cfg.o_fk // LANES,
                                                  cfg.o_fv // LANES, sv.ct, True, False,
                                                  "fox_bwd_exchange" if side else "fox_bwd", side)
        dff, acc_f = fox_post(bias_grad(dck, dcr), sv.z, wl.bf_pad)
        dqp, dkp, dvp = attn_bwd(sv.qp, sv.kp, sv.vp, do_m, st_m, 0, 0, 0, None, False, True, "mla_bwd")
        zq, zkv, zms, dw_uq, dw_uk, dw_v, dgq, dgkv = mla_post(
            dqp, dkp, dvp, dff, sv.z, wl.gq, wl.gkv, wl.w_uq, wl.w_uk, wl.w_v, wl.l, cfg.cos_t, cfg.sin_t)
        dz = jnp.concatenate([(dfq * FOX_SCALE).astype(BF16), dfk.astype(BF16), dfv.astype(BF16), dfg, dmg, zq, zkv, zms], axis=1)
        dx, acc_i = bwd_in(dz, wl.w_in, wl.l, sv.x, dxn, wl.norm_g, scale)
        dw_in = matmul_tn(sv.h, dz, "dw_in")
        grads = types.SimpleNamespace(
            w_in=dw_in, w_out=dw_out, w_uq=dw_uq, w_uk=dw_uk, w_v=dw_v, gq=dgq[0], gkv=dgkv[0],
            b_f=acc_f[0, :FH], norm_g=acc_i[2], dmod=jnp.concatenate([acc_i[0], acc_i[1], acc_o[0]]))
        return (dx, grads, rode[0]) if side else (dx, grads)

    return types.SimpleNamespace(layer_forward=layer_forward, layer_backward=layer_backward, final_loss=final_loss)


def _pack_rows(parts, dtype, row_multiple):
    flat = jnp.concatenate([p.reshape(-1).astype(dtype) for p in parts])
    per = SLAB_COLS * row_multiple
    total = -(-flat.shape[0] // per) * per
    return jnp.pad(flat, (0, total - flat.shape[0])).reshape(total // SLAB_COLS, SLAB_COLS)


def _unpack(flat, shapes):
    out, off = [], 0
    for shp in shapes:
        n = 1
        for d in shp:
            n *= d
        out.append(flat[off:off + n].reshape(shp))
        off += n
    return out


def kernel(x, c, positions, norm_g, w_ada, b_ada, w_in, b_f, q_norm_g, w_uq, kv_norm_g, w_ukv, w_out, final_g, loss_target, m_norm_g, m_w_ada, m_b_ada, m_w_in, m_b_f, m_q_norm_g, m_w_uq, m_kv_norm_g, m_w_ukv, m_w_out, m_final_g, v_norm_g, v_w_ada, v_b_ada, v_w_in, v_b_f, v_q_norm_g, v_w_uq, v_kv_norm_g, v_w_ukv, v_w_out, v_final_g):
    S, D = x.shape[1], x.shape[2]
    L = norm_g.shape[0]
    FH = b_f.shape[1]
    QL, KVL = q_norm_g.shape[1], kv_norm_g.shape[1]
    MH = w_ukv.shape[2] * N_CHIPS // (2 * HEAD_DIM)
    FW, MW = FH * HEAD_DIM, MH * HEAD_DIM
    NA = w_ada.shape[2]
    n_in = w_in.shape[2] * N_CHIPS
    cfg = types.SimpleNamespace(S=S, D=D, FW=FW, MW=MW, QL=QL, KVL=KVL, FH=FH, MH=MH)
    cfg.o_fq, cfg.o_fk, cfg.o_fv, cfg.o_fg, cfg.o_mg = 0, FW, 2 * FW, 3 * FW, 4 * FW
    cfg.o_ql = 4 * FW + MW
    cfg.o_kv = cfg.o_ql + QL
    cfg.o_ms = cfg.o_kv + KVL
    cfg.NZ = cfg.o_ms + LANES
    assert FW == MW and FH % 2 == 0 and MH % 2 == 0 and cfg.o_ql % QL == 0 and cfg.o_kv % KVL == 0 and KVL == LANES
    assert n_in == 4 * FW + FH + QL + KVL + ROPE_DIM + MW

    mx, my, mc = _my_pos()
    my_chip = 2 * mx + my
    my_dev = 2 * my_chip + mc

    inv_freq = 1.0 / (ROPE_THETA ** (jnp.arange(0, ROPE_DIM, 2, dtype=F32) / ROPE_DIM))
    ang = positions[0].astype(F32)[:, None] * inv_freq
    cos, sin = jnp.cos(ang), jnp.sin(ang)
    cfg.cos_t = jnp.concatenate([jnp.ones((S, ROPE_LO), F32), cos, cos, jnp.ones((S, HEAD_PAD - ROPE_LO - ROPE_DIM), F32)], axis=1)
    cfg.sin_t = jnp.concatenate([jnp.zeros((S, ROPE_LO), F32), -sin, sin, jnp.zeros((S, HEAD_PAD - ROPE_LO - ROPE_DIM), F32)], axis=1)

    assert L % 4 == 0
    def lane_pad(a):
        return jnp.pad(a, ((0, 0),) * (a.ndim - 1) + ((0, -a.shape[-1] % LANES),))

    shards = (w_in, w_uq, w_ukv, w_out)
    padded = [lane_pad(w.astype(BF16)) for w in shards]

    def kernel_layouts(gathered, own):
        def all_chips(g, w, axis):
            return jnp.concatenate([jnp.where(my_chip == t, w.astype(BF16), g[t][..., :w.shape[-1]])
                                    for t in range(N_CHIPS)], axis=axis)

        n = own[0].shape[0]
        w_in_f = all_chips(gathered[0], own[0], 2)
        w_uq_f = all_chips(gathered[1], own[1], 2)
        w_ukv_f = all_chips(gathered[2], own[2], 2)
        w_out_f = all_chips(gathered[3], own[3], 1)
        sizes = (FW, FW, FW, FH, FW, QL, KVL, ROPE_DIM, MW)
        offs = [0]
        for sz in sizes:
            offs.append(offs[-1] + sz)
        fq_w, fk_w, fv_w, ff_w, fg_w, ql_w, kvl_w, kr_w, mg_w = [w_in_f[:, :, offs[i]:offs[i + 1]] for i in range(len(sizes))]
        zeros = lambda width: jnp.zeros((n, D, width), BF16)
        w_in_p = jnp.concatenate([fq_w, fk_w, fv_w, fg_w, mg_w, ql_w, kvl_w, ff_w, zeros(ROPE_LO - FH), kr_w,
                                  zeros(HEAD_PAD - ROPE_LO - ROPE_DIM)], axis=2)
        w_uq_p = jnp.pad(w_uq_f.reshape(n, QL, MH, HEAD_DIM + ROPE_DIM),
                         ((0, 0), (0, 0), (0, 0), (0, HEAD_PAD - HEAD_DIM - ROPE_DIM))).reshape(n, QL, MH * HEAD_PAD)
        w_ukv4 = w_ukv_f.reshape(n, KVL, MH, 2 * HEAD_DIM)
        w_uk_p = jnp.pad(w_ukv4[..., :HEAD_DIM], ((0, 0), (0, 0), (0, 0), (0, HEAD_PAD - HEAD_DIM))).reshape(n, KVL, MH * HEAD_PAD)
        return types.SimpleNamespace(w_in=w_in_p, w_uq=w_uq_p, w_uk=w_uk_p, w_v=w_ukv4[..., HEAD_DIM:].reshape(n, KVL, MW),
                                     w_out=w_out_f)

    first = kernel_layouts(weights_gather([p[:1] for p in padded], "weights_first"), [w[:1] for w in shards])

    c_all = allgather8(c.reshape(8, D // 8), "gather_c").reshape(N_DEV, D)
    c_pad = jnp.pad(c_all, ((0, 16 - N_DEV), (0, 0)))
    mod_part = ada_forward(c_pad, w_ada)[:, :N_DEV, :]
    mod_all = allgather8(mod_part.reshape(-1, LANES), "gather_mod").reshape(N_CHIPS, 2, L, N_DEV, NA)[:, 0]
    mod_full = mod_all.transpose(1, 2, 0, 3).reshape(L, N_DEV, N_CHIPS * NA) + b_ada[:, None, :]
    mod_mine = lax.dynamic_index_in_dim(mod_full, my_dev, axis=1, keepdims=True)

    step = make_step(cfg)
    bf_pad = jnp.pad(b_f, ((0, 0), (0, LANES - FH)))
    mods = [(mod_mine[l, :, :D], mod_mine[l, :, D:2 * D], mod_mine[l, :, 2 * D:]) for l in range(L)]

    def layer_params(l, ws_, at):
        return types.SimpleNamespace(l=at, norm_g=norm_g[l][None], bf_pad=bf_pad[l][None], gq=q_norm_g[l][None],
                                     gkv=kv_norm_g[l][None], w_in=ws_.w_in, w_uq=ws_.w_uq, w_uk=ws_.w_uk, w_v=ws_.w_v,
                                     w_out=ws_.w_out)

    layers = [layer_params(0, first, 0)]
    xl, sv, later = step.layer_forward(x[0], layers[0], mods[0], [p[1:] for p in padded])
    saved = [sv]
    rest = kernel_layouts(later, [w[1:] for w in shards])
    for l in range(1, L):
        layers.append(layer_params(l, rest, l - 1))
        xl, sv = step.layer_forward(xl, layers[l], mods[l])
        saved.append(sv)
    dx, acc_fin, loss_part = step.final_loss(xl, final_g[None], loss_target[0])
    loss = lax.psum(loss_part[0, 0], ("x", "y", "c"))
    ms = cfg.o_ms
    runs = [(0, 3 * FW, 0), (3 * FW, FH, ms), (3 * FW + FH, FW, cfg.o_fg), (4 * FW + FH, QL + KVL, cfg.o_ql),
            (4 * FW + FH + QL + KVL, ROPE_DIM, ms + ROPE_LO), (4 * FW + FH + QL + KVL + ROPE_DIM, MW, cfg.o_mg)]
    big_names = ["w_in", "w_uq", "w_ukv", "w_out"]

    def shard_columns(dw, t):
        a, b = t * w_in.shape[2], (t + 1) * w_in.shape[2]
        return jnp.concatenate([dw[:, p0 + max(a, r0) - r0:p0 + min(b, r0 + sz) - r0]
                                for r0, sz, p0 in runs if max(a, r0) < min(b, r0 + sz)], axis=1)

    def chip_parts(group, tag):
        n = len(group)
        stk = lambda name: jnp.stack([getattr(g, name) for g in group])
        g_in4 = jnp.stack([jnp.stack([shard_columns(g.w_in, t) for g in group]) for t in range(N_CHIPS)])
        dw_uq_f = stk("w_uq").reshape(n, QL, MH, HEAD_PAD)[..., :HEAD_DIM + ROPE_DIM].reshape(n, QL, -1)
        dw_ukv_f = jnp.concatenate([stk("w_uk").reshape(n, KVL, MH, HEAD_PAD)[..., :HEAD_DIM],
                                    stk("w_v").reshape(n, KVL, MH, HEAD_DIM)], axis=3).reshape(n, KVL, -1)
        gs = [g_in4, dw_uq_f.reshape(n, QL, N_CHIPS, -1).transpose(2, 0, 1, 3),
              dw_ukv_f.reshape(n, KVL, N_CHIPS, -1).transpose(2, 0, 1, 3), stk("w_out").reshape(n, N_CHIPS, -1, D).transpose(1, 0, 2, 3)]
        gs = [lane_pad(g.astype(BF16)) for g in gs]
        theirs = halves_to_sibling(gs, "grads_sibling_" + tag)
        out = []
        for g, o, nm in zip(gs, theirs, big_names):
            keep = lax.dynamic_slice_in_dim(g, mc * (n // 2), n // 2, axis=1)
            merged = (N_CHIPS * (n // 2),) + g.shape[2:]
            out.append(add_cast(keep.reshape(merged), o.reshape(merged), BF16, "grads_chip_sum_%s_%s" % (nm, tag)).reshape(o.shape))
        return out

    def reduced(chip_part, parts, tag):
        where = jnp.stack([my_chip, mc]).astype(jnp.int32)
        red = [sum_chips(p, o, where, "grads_sum_%s_%s" % (nm, tag)) for p, o, nm in zip(parts, chip_part, big_names)]
        return [g[..., :w.shape[-1]] for g, w in zip(halves_gather(red, "grads_back_" + tag), shards)]

    gl = [None] * L
    half_l = L // 2
    for l in range(L - 1, half_l - 1, -1):
        dx, gl[l] = step.layer_backward(dx, saved[l], layers[l], mods[l])
    part_hi = chip_parts(gl[half_l:], "hi")
    dx, gl[half_l - 1], parts_hi = step.layer_backward(dx, saved[half_l - 1], layers[half_l - 1], mods[half_l - 1],
                                                       ExchangeOverIci(part_hi))
    for l in range(half_l - 2, -1, -1):
        dx, gl[l] = step.layer_backward(dx, saved[l], layers[l], mods[l])
    part_lo = chip_parts(gl[:half_l], "lo")
    parts_lo = run_side(ExchangeOverIci(part_lo), "grads_chips_lo")
    g_w_in, g_w_uq, g_w_ukv, g_w_out = [jnp.concatenate([lo, hi]) for lo, hi in
                                        zip(reduced(part_lo, parts_lo, "lo"), reduced(part_hi, parts_hi, "hi"))]
    grad_x = dx[None]

    stack = lambda name: jnp.stack([getattr(g, name) for g in gl])
    small_parts = [stack("norm_g"), stack("dmod"), stack("b_f"), stack("gq"), stack("gkv"), acc_fin[0]]
    small_shapes = [p.shape for p in small_parts]
    small = _pack_rows(small_parts, F32, 8).reshape(-1, LANES)
    small_all = allgather8(small, "gather_small").reshape(N_DEV, -1, LANES)
    small_sum = sum_leading(small_all, "sum_small")
    g_norm_g, g_b_ada, g_b_f, g_q_norm_g, g_kv_norm_g, g_final_g = _unpack(small_sum.reshape(-1), small_shapes)

    n_ng = L * D
    dmod_all = small_all.reshape(N_DEV, -1)[:, n_ng:n_ng + L * 3 * D].reshape(N_DEV, L, 3 * D)
    dmod_cols = lax.dynamic_slice_in_dim(dmod_all, my_chip * NA, NA, axis=2).transpose(1, 0, 2)
    g_w_ada = ada_backward(c_pad, jnp.pad(dmod_cols, ((0, 0), (0, 16 - N_DEV), (0, 0))))


    names = ["norm_g", "w_ada", "b_ada", "w_in", "b_f", "q_norm_g", "w_uq", "kv_norm_g", "w_ukv", "w_out", "final_g"]
    ws = dict(norm_g=norm_g, w_ada=w_ada, b_ada=b_ada, w_in=w_in, b_f=b_f, q_norm_g=q_norm_g, w_uq=w_uq,
              kv_norm_g=kv_norm_g, w_ukv=w_ukv, w_out=w_out, final_g=final_g)
    msd = dict(norm_g=m_norm_g, w_ada=m_w_ada, b_ada=m_b_ada, w_in=m_w_in, b_f=m_b_f, q_norm_g=m_q_norm_g, w_uq=m_w_uq,
               kv_norm_g=m_kv_norm_g, w_ukv=m_w_ukv, w_out=m_w_out, final_g=m_final_g)
    vsd = dict(norm_g=v_norm_g, w_ada=v_w_ada, b_ada=v_b_ada, w_in=v_w_in, b_f=v_b_f, q_norm_g=v_q_norm_g, w_uq=v_w_uq,
               kv_norm_g=v_kv_norm_g, w_ukv=v_w_ukv, w_out=v_w_out, final_g=v_final_g)
    gsd = dict(norm_g=g_norm_g, w_ada=g_w_ada, b_ada=g_b_ada, w_in=g_w_in, b_f=g_b_f, q_norm_g=g_q_norm_g, w_uq=g_w_uq,
               kv_norm_g=g_kv_norm_g, w_ukv=g_w_ukv, w_out=g_w_out, final_g=g_final_g)
    small_names = ["norm_g", "b_ada", "b_f", "q_norm_g", "kv_norm_g", "final_g"]
    sm_shapes = [ws[n].shape for n in small_names]
    pk = lambda d: _pack_rows([d[n] for n in small_names], F32, 8).reshape(1, -1, LANES)
    sm_out = adamw(pk(ws), pk(gsd), pk(msd), pk(vsd), "adamw_small")
    sm_d, sm_m, sm_v = [dict(zip(small_names, _unpack(o.reshape(-1), sm_shapes))) for o in sm_out]
    delta, new_m, new_v = dict(sm_d), dict(sm_m), dict(sm_v)
    for n in ["w_ada", "w_in", "w_uq", "w_ukv", "w_out"]:
        delta[n], new_m[n], new_v[n] = adamw(ws[n], gsd[n], msd[n], vsd[n], "adamw_" + n)

    return (loss, grad_x, *[gsd[n] for n in names], *[delta[n] for n in names],
            *[new_m[n] for n in names], *[new_v[n] for n in names])
```

```python
import types

import jax
import jax.numpy as jnp
from jax import lax
from jax.experimental import pallas as pl
from jax.experimental.pallas import tpu as pltpu

F32 = jnp.float32
BF16 = jnp.bfloat16
MESH = pl.DeviceIdType.MESH

N_CHIPS = 4
N_DEV = 8
HEAD_DIM = 64
ROPE_DIM = 32
ROPE_THETA = 10000.0
HEAD_PAD = 128
ROPE_LO = 64
ROPE_HALF = 16
CHUNK_SHIFT = 6
LANES = 128
EPS = 1e-6
NEG = -1e30
ADAM_LR = 0.001
ADAM_B1 = 0.9
ADAM_B2 = 0.999
ADAM_EPS = 1e-08
ADAM_WD = 0.01
ADAM_STEP = 10
VMEM_LIMIT = 48 * 1024 * 1024
SLAB_COLS = 1024


def _params(sem=None, vmem=VMEM_LIMIT):
    return pltpu.CompilerParams(dimension_semantics=sem, vmem_limit_bytes=vmem)


def _nn(a, b):
    return jnp.dot(a, b, preferred_element_type=F32)


def _nt(a, b):
    return lax.dot_general(a, b, (((1,), (1,)), ((), ())), preferred_element_type=F32)


def _tn(a, b):
    return lax.dot_general(a, b, (((0,), (0,)), ((), ())), preferred_element_type=F32)


def _sigmoid(x):
    return 1.0 / (1.0 + jnp.exp(-x))


def _lane(shape):
    return lax.broadcasted_iota(jnp.int32, shape, len(shape) - 1)


def _pick(n, cands):
    for c in cands:
        if n % c == 0:
            return c
    return n


def _my_pos():
    return lax.axis_index("x"), lax.axis_index("y"), lax.axis_index("c")


def allgather8(xs, name):
    m_per, n = xs.shape

    def body(x_ref, out_ref, send_sems, recv_sems, local_sem):
        x, y, c = _my_pos()
        me, sibling = (x, y, c), (x, y, 1 - c)
        chips = [(1 - x, y), (x, 1 - y), (1 - x, 1 - y)]

        def rows(px, py, pc):
            return out_ref.at[pl.ds((4 * px + 2 * py + pc) * m_per, m_per), :]

        def copy(k, block, to, src=None):
            return pltpu.make_async_remote_copy(
                src_ref=rows(*block) if src is None else src, dst_ref=rows(*block),
                send_sem=send_sems.at[k], recv_sem=recv_sems.at[k], device_id=to, device_id_type=MESH)

        mine = pltpu.make_async_copy(x_ref, rows(*me), local_sem)
        mine.start()
        first = [copy(0, me, sibling, src=x_ref)]
        first += [copy(1 + j, me, (*chip, c), src=x_ref) for j, chip in enumerate(chips)]
        for cp in first:
            cp.start()
        passed = [copy(4 + j, (*chip, c), sibling) for j, chip in enumerate(chips)]
        for j, chip in enumerate(chips):
            copy(1 + j, (*chip, c), me).wait_recv()
            passed[j].start()
        copy(0, sibling, me).wait_recv()
        for j, chip in enumerate(chips):
            copy(4 + j, (*chip, 1 - c), me).wait_recv()
        for cp in first + passed:
            cp.wait_send()
        mine.wait()

    return pl.pallas_call(
        body, name=name,
        out_shape=jax.ShapeDtypeStruct((N_DEV * m_per, n), xs.dtype),
        in_specs=[pl.BlockSpec(memory_space=pltpu.VMEM)],
        out_specs=pl.BlockSpec(memory_space=pltpu.VMEM),
        scratch_shapes=[pltpu.SemaphoreType.DMA((7,)), pltpu.SemaphoreType.DMA((7,)), pltpu.SemaphoreType.DMA],
    )(xs)


def _remote(src, dst, send_sems, recv_sems, k, to):
    return pltpu.make_async_remote_copy(src_ref=src, dst_ref=dst, send_sem=send_sems.at[k], recv_sem=recv_sems.at[k],
                                        device_id=to, device_id_type=MESH)


def _hbm_call(body, name, ins, out_shapes, n_sems, aliases=None):
    hbm = pl.BlockSpec(memory_space=pl.ANY)
    scratch = [pltpu.SemaphoreType.DMA((n_sems,)), pltpu.SemaphoreType.DMA((n_sems,))]
    return pl.pallas_call(body, name=name, out_shape=out_shapes, in_specs=[hbm] * len(ins),
                          out_specs=[hbm] * len(out_shapes), scratch_shapes=scratch,
                          input_output_aliases=aliases or {})(*ins)


def _layers_half(ref, h, axis=0):
    size = ref.shape[axis] // 2
    idx = (slice(None),) * axis + (pl.ds(h * size, size),)
    return ref.at[idx]


class GatherOverIci:
    def __init__(self, ws):
        self.ins = list(ws)
        self.outs = [jax.ShapeDtypeStruct((N_CHIPS,) + w.shape, w.dtype) for w in ws]
        self.aliases = {}
        self.n_sems = 3 * len(ws)

    def _copies(self, x_refs, o_refs, send_sems, recv_sems):
        x, y, c = _my_pos()
        s = 2 * x + y
        out = []
        for a in range(len(x_refs)):
            for k, (tx, ty) in enumerate([(1 - x, y), (x, 1 - y), (1 - x, 1 - y)]):
                mine = _remote(_layers_half(x_refs[a], c, 1), _layers_half(o_refs[a].at[s], c, 1), send_sems, recv_sems,
                               3 * a + k, (tx, ty, c))
                got = _layers_half(o_refs[a].at[2 * tx + ty], c, 1)
                out.append((mine, _remote(got, got, send_sems, recv_sems, 3 * a + k, (tx, ty, c))))
        return out

    def start(self, x_refs, o_refs, send_sems, recv_sems):
        for mine, _ in self._copies(x_refs, o_refs, send_sems, recv_sems):
            mine.start()

    def wait(self, x_refs, o_refs, send_sems, recv_sems):
        for mine, theirs in self._copies(x_refs, o_refs, send_sems, recv_sems):
            theirs.wait_recv()
            mine.wait_send()


class GatherToSibling:
    def __init__(self, gathered):
        self.ins = list(gathered)
        self.outs = [jax.ShapeDtypeStruct(g.shape, g.dtype) for g in gathered]
        self.aliases = {a: a for a in range(len(gathered))}
        self.n_sems = 3 * len(gathered)

    def _copies(self, o_refs, send_sems, recv_sems):
        x, y, c = _my_pos()
        out = []
        for a in range(len(o_refs)):
            for k, (tx, ty) in enumerate([(1 - x, y), (x, 1 - y), (1 - x, 1 - y)]):
                got = _layers_half(o_refs[a].at[2 * tx + ty], c, 1)
                theirs = _layers_half(o_refs[a].at[2 * tx + ty], 1 - c, 1)
                out.append((_remote(got, got, send_sems, recv_sems, 3 * a + k, (x, y, 1 - c)),
                            _remote(theirs, theirs, send_sems, recv_sems, 3 * a + k, (x, y, 1 - c))))
        return out

    def start(self, x_refs, o_refs, send_sems, recv_sems):
        for mine, _ in self._copies(o_refs, send_sems, recv_sems):
            mine.start()

    def wait(self, x_refs, o_refs, send_sems, recv_sems):
        for mine, theirs in self._copies(o_refs, send_sems, recv_sems):
            theirs.wait_recv()
            mine.wait_send()


def run_side(side, name):
    n_in, n_out = len(side.ins), len(side.outs)

    def body(*refs):
        parts = refs[:n_in], refs[n_in:n_in + n_out], refs[n_in + n_out], refs[n_in + n_out + 1]
        side.start(*parts)
        side.wait(*parts)

    return _hbm_call(body, name, side.ins, side.outs, side.n_sems, aliases=side.aliases)


def weights_gather(ws, name):
    return run_side(GatherToSibling(run_side(GatherOverIci(ws), name + "_ici")), name + "_sibling")


def halves_to_sibling(gs, name):
    n = len(gs)

    def body(*refs):
        x_refs, o_refs, send_sems, recv_sems = refs[:n], refs[n:2 * n], refs[2 * n], refs[2 * n + 1]
        x, y, c = _my_pos()
        cps = [_remote(_layers_half(x_refs[a], 1 - c, axis=1), o_refs[a], send_sems, recv_sems, a, (x, y, 1 - c))
               for a in range(n)]
        for cp in cps:
            cp.start()
        for cp in cps:
            cp.wait()

    outs = [jax.ShapeDtypeStruct((g.shape[0], g.shape[1] // 2) + g.shape[2:], g.dtype) for g in gs]
    return _hbm_call(body, name, gs, outs, n)


class ExchangeOverIci:
    def __init__(self, xs):
        self.ins = list(xs)
        self.outs = [jax.ShapeDtypeStruct(v.shape, v.dtype) for v in xs]
        self.aliases = {}
        self.n_sems = 3 * len(xs)

    def _copies(self, x_refs, o_refs, send_sems, recv_sems):
        x, y, c = _my_pos()
        s = 2 * x + y
        out = []
        for a in range(len(x_refs)):
            for k, (tx, ty) in enumerate([(1 - x, y), (x, 1 - y), (1 - x, 1 - y)]):
                got = o_refs[a].at[2 * tx + ty]
                out.append((_remote(x_refs[a].at[2 * tx + ty], o_refs[a].at[s], send_sems, recv_sems, 3 * a + k, (tx, ty, c)),
                            _remote(got, got, send_sems, recv_sems, 3 * a + k, (tx, ty, c))))
        return out

    def start(self, x_refs, o_refs, send_sems, recv_sems):
        for mine, _ in self._copies(x_refs, o_refs, send_sems, recv_sems):
            mine.start()

    def wait(self, x_refs, o_refs, send_sems, recv_sems):
        for mine, theirs in self._copies(x_refs, o_refs, send_sems, recv_sems):
            theirs.wait_recv()
            mine.wait_send()


def halves_gather(xs, name):
    n = len(xs)

    def body(*refs):
        x_refs, o_refs, send_sems, recv_sems = refs[:n], refs[n:2 * n], refs[2 * n], refs[2 * n + 1]
        x, y, c = _my_pos()
        sends = []
        for a in range(n):
            cp = _remote(_layers_half(x_refs[a], c), _layers_half(o_refs[a], c), send_sems, recv_sems, a, (x, y, 1 - c))
            cp.start()
            sends.append(cp)
        for a in range(n):
            theirs = _layers_half(o_refs[a], 1 - c)
            _remote(theirs, theirs, send_sems, recv_sems, a, (x, y, 1 - c)).wait_recv()
        for cp in sends:
            cp.wait_send()

    outs = [jax.ShapeDtypeStruct(v.shape, v.dtype) for v in xs]
    return _hbm_call(body, name, xs, outs, n, aliases={a: a for a in range(n)})


def sum_leading(xs, name):
    n, r, c = xs.shape

    def body(x_ref, o_ref):
        acc = x_ref[0]
        for i in range(1, n):
            acc = acc + x_ref[i]
        o_ref[...] = acc

    return pl.pallas_call(body, name=name, out_shape=jax.ShapeDtypeStruct((r, c), xs.dtype))(xs)


def add_cast(a, b, out_dtype, name):
    n, r, c = a.shape
    tr = _pick(r, (512, 256, 128, 64, 32, 16))

    def body(a_ref, b_ref, o_ref):
        o_ref[...] = (a_ref[...].astype(F32) + b_ref[...].astype(F32)).astype(out_dtype)

    spec = pl.BlockSpec((1, tr, c), lambda i, j: (i, j, 0))
    return pl.pallas_call(body, name=name, grid=(n, r // tr), in_specs=[spec, spec], out_specs=spec,
                          out_shape=jax.ShapeDtypeStruct((n, r, c), out_dtype),
                          compiler_params=_params(("parallel", "parallel")))(a, b)


def sum_chips(parts, own, where, name):
    n, nl, r, c = parts.shape
    tr = _pick(r, (512, 256, 128, 64, 32, 16))

    def body(w_ref, p_ref, o_ref, out_ref):
        s = w_ref[0]
        acc = jnp.zeros(out_ref.shape, F32)
        for t in range(n):
            acc = acc + jnp.where(s == t, o_ref[0], p_ref[t]).astype(F32)
        out_ref[...] = acc

    grid_spec = pltpu.PrefetchScalarGridSpec(
        num_scalar_prefetch=1, grid=(nl, r // tr),
        in_specs=[pl.BlockSpec((n, 1, tr, c), lambda i, j, w: (0, i, j, 0)),
                  pl.BlockSpec((1, 1, tr, c), lambda i, j, w: (w[0], i, j, 0))],
        out_specs=pl.BlockSpec((1, tr, c), lambda i, j, w: (w[1] * nl + i, j, 0)))
    return pl.pallas_call(body, name=name, grid_spec=grid_spec, out_shape=jax.ShapeDtypeStruct((2 * nl, r, c), F32),
                          compiler_params=_params(("parallel", "parallel")))(where, parts, own)


def ada_forward(c_all, w_ada):
    nl, d, n = w_ada.shape
    nb = c_all.shape[0]

    def body(c_ref, w_ref, o_ref):
        cv = c_ref[...]
        ca = (cv * _sigmoid(cv)).astype(BF16)
        o_ref[0] = _nn(ca, w_ref[0].astype(BF16))

    return pl.pallas_call(
        body, name="ada_forward", grid=(nl,),
        in_specs=[pl.BlockSpec((nb, d), lambda l: (0, 0)), pl.BlockSpec((1, d, n), lambda l: (l, 0, 0))],
        out_specs=pl.BlockSpec((1, nb, n), lambda l: (l, 0, 0)),
        out_shape=jax.ShapeDtypeStruct((nl, nb, n), F32), compiler_params=_params(("parallel",)))(c_all, w_ada)


def ada_backward(c_all, dmod):
    nl, nb, n = dmod.shape
    d = c_all.shape[1]

    def body(c_ref, g_ref, o_ref):
        cv = c_ref[...]
        ca = (cv * _sigmoid(cv)).astype(BF16)
        o_ref[0] = _tn(ca, g_ref[0].astype(BF16))

    return pl.pallas_call(
        body, name="ada_backward", grid=(nl,),
        in_specs=[pl.BlockSpec((nb, d), lambda l: (0, 0)), pl.BlockSpec((1, nb, n), lambda l: (l, 0, 0))],
        out_specs=pl.BlockSpec((1, d, n), lambda l: (l, 0, 0)),
        out_shape=jax.ShapeDtypeStruct((nl, d, n), F32), compiler_params=_params(("parallel",)))(c_all, dmod)


def matmul_tn(a, b, name):
    k, m = a.shape
    n = b.shape[1]
    tm, tk = _pick(m, (512, 256, 128)), _pick(k, (512, 256, 128))
    tn = n if n * (tm * 8 + tk * 4) <= VMEM_LIMIT // 2 else _pick(n, (512, 256, 128))

    def body(a_ref, b_ref, o_ref, acc_ref):
        @pl.when(pl.program_id(2) == 0)
        def _():
            acc_ref[...] = jnp.zeros_like(acc_ref)

        acc_ref[...] += _tn(a_ref[...], b_ref[...])

        @pl.when(pl.program_id(2) == k // tk - 1)
        def _():
            o_ref[...] = acc_ref[...].astype(BF16)

    return pl.pallas_call(
        body, name=name, grid=(m // tm, n // tn, k // tk),
        in_specs=[pl.BlockSpec((tk, tm), lambda i, j, kk: (kk, i)), pl.BlockSpec((tk, tn), lambda i, j, kk: (kk, j))],
        out_specs=pl.BlockSpec((tm, tn), lambda i, j, kk: (i, j)),
        out_shape=jax.ShapeDtypeStruct((m, n), BF16), scratch_shapes=[pltpu.VMEM((tm, tn), F32)],
        compiler_params=_params(("parallel", "parallel", "arbitrary")))(a, b)


def adamw(w, g, m, v, name):
    nl, r, c = w.shape
    tr = _pick(r, (512, 256, 128, 64, 32, 16, 8))

    def body(w_ref, g_ref, m_ref, v_ref, d_ref, mo_ref, vo_ref):
        gv = g_ref[...]
        mn = ADAM_B1 * m_ref[...] + (1.0 - ADAM_B1) * gv
        vn = ADAM_B2 * v_ref[...] + (1.0 - ADAM_B2) * (gv * gv)
        m_hat = mn / (1.0 - ADAM_B1 ** ADAM_STEP)
        v_hat = vn / (1.0 - ADAM_B2 ** ADAM_STEP)
        d_ref[...] = -ADAM_LR * (m_hat / (jnp.sqrt(v_hat) + ADAM_EPS) + ADAM_WD * w_ref[...])
        mo_ref[...] = mn
        vo_ref[...] = vn

    spec = pl.BlockSpec((1, tr, c), lambda l, i: (l, i, 0))
    out = jax.ShapeDtypeStruct((nl, r, c), F32)
    return pl.pallas_call(body, name=name, grid=(nl, r // tr), in_specs=[spec] * 4, out_specs=[spec] * 3,
                          out_shape=[out] * 3, compiler_params=_params(("parallel", "parallel")))(w, g, m, v)


def _rope(t, cos_t, sin_t):
    w = t.shape[1]
    lane = _lane(t.shape) & (HEAD_PAD - 1)
    first_half = (lane >= ROPE_LO) & (lane < ROPE_LO + ROPE_HALF)
    partner = jnp.where(first_half, pltpu.roll(t, w - ROPE_HALF, 1), pltpu.roll(t, ROPE_HALF, 1))
    return t * cos_t + partner * sin_t


def _rope_t(dt, cos_t, sin_t):
    w = dt.shape[1]
    lane = _lane(dt.shape) & (HEAD_PAD - 1)
    first_half = (lane >= ROPE_LO) & (lane < ROPE_LO + ROPE_HALF)
    ds = dt * sin_t
    partner = jnp.where(first_half, pltpu.roll(ds, w - ROPE_HALF, 1), pltpu.roll(ds, ROPE_HALF, 1))
    return dt * cos_t + partner


def _rms(xv, g):
    rstd = lax.rsqrt(jnp.mean(xv * xv, axis=-1, keepdims=True) + EPS)
    xh = xv * rstd
    return xh * g, xh, rstd


def _rms_bwd(dy, g, xh, rstd):
    dxh = dy * g
    dx = rstd * (dxh - xh * jnp.mean(dxh * xh, axis=-1, keepdims=True))
    return dx, jnp.sum(dy * xh, axis=0, keepdims=True)


def make_step(cfg):
    S, D, NZ = cfg.S, cfg.D, cfg.NZ
    FW, MW, QL, KVL, FH, MH = cfg.FW, cfg.MW, cfg.QL, cfg.KVL, cfg.FH, cfg.MH
    QW = MH * HEAD_PAD
    TM = _pick(S, (512, 256, 128))
    TQ = TK = _pick(S, (256, 128))
    n_tok = S // TM
    ZO = 3 * FW
    NZR = NZ - ZO
    misc_blk = (cfg.o_ms - ZO) // LANES
    FWD_UNROLL = 8
    FOX_SCALE = HEAD_DIM ** -0.5
    MLA_SCALE = (HEAD_DIM + ROPE_DIM) ** -0.5

    def tok(width, col=0):
        return pl.BlockSpec((TM, width), lambda i: (i, col))

    def const(shape):
        return pl.BlockSpec(shape, lambda i: tuple(0 for _ in shape))

    def layer(l, shape):
        return pl.BlockSpec((None,) + shape, lambda i: (l,) + tuple(0 for _ in shape))


    def ln_inproj(x, g, scale, shift, w_in, l):
        def body(x_ref, g_ref, sc_ref, sh_ref, w_ref, h_ref, z_ref, qkv_ref):
            y, _, _ = _rms(x_ref[...], g_ref[...])
            hb = (y * (1.0 + sc_ref[...]) + sh_ref[...]).astype(BF16)
            h_ref[...] = hb
            z = _nn(hb, w_ref[...])
            z_ref[...] = z[:, ZO:]
            qkv_ref[:, :FW] = (z[:, :FW] * FOX_SCALE).astype(BF16)
            qkv_ref[:, FW:] = z[:, FW:ZO].astype(BF16)

        return pl.pallas_call(
            body, name="ln_inproj", grid=(n_tok,),
            in_specs=[tok(D), const((1, D)), const((1, D)), const((1, D)), layer(l, (D, NZ))],
            out_specs=[tok(D), tok(NZR), tok(ZO)],
            out_shape=[jax.ShapeDtypeStruct((S, D), BF16), jax.ShapeDtypeStruct((S, NZR), F32),
                       jax.ShapeDtypeStruct((S, ZO), BF16)],
            compiler_params=_params(("parallel",)))(x, g, scale, shift, w_in)

    def _log_f_terms(misc, bf):
        lane = _lane(misc.shape)
        a = misc + bf
        e = jnp.exp(-jnp.abs(a))
        logf = jnp.minimum(a, 0.0) - jnp.log(1.0 + e)
        sig_neg = jnp.where(a >= 0, e, 1.0) / (1.0 + e)
        valid = lane < FH
        return jnp.where(valid, logf, 0.0), jnp.where(valid, sig_neg, 0.0)

    def fox_prep(z, bf_pad):
        def body(z_ref, b_ref, o_ref, t_ref, carry):
            @pl.when(pl.program_id(0) == 0)
            def _():
                carry[...] = jnp.zeros_like(carry)

            logf, _ = _log_f_terms(z_ref[...], b_ref[...])
            row = lax.broadcasted_iota(jnp.int32, (TM, TM), 0)
            col = lax.broadcasted_iota(jnp.int32, (TM, TM), 1)
            tri = (col <= row).astype(F32)
            cum = jnp.dot(tri, logf, precision=lax.Precision.HIGHEST, preferred_element_type=F32) + carry[...]
            o_ref[...] = cum
            carry[...] = cum[TM - 1:TM, :]
            for hd in range(FH):
                t_ref[hd // 2, hd % 2] = jnp.broadcast_to(cum[:, hd:hd + 1], (TM, LANES))

        return pl.pallas_call(
            body, name="fox_prep", grid=(n_tok,),
            in_specs=[tok(LANES, misc_blk), const((1, LANES))],
            out_specs=[tok(LANES), pl.BlockSpec((FH // 2, 2, TM, LANES), lambda i: (0, 0, i, 0))],
            out_shape=[jax.ShapeDtypeStruct((S, LANES), F32), jax.ShapeDtypeStruct((FH // 2, 2, S, LANES), F32)],
            scratch_shapes=[pltpu.VMEM((1, LANES), F32)],
            compiler_params=_params(("arbitrary",)))(z, bf_pad)

    def mla_prep(z, gq, gkv, w_uq, w_uk, w_v, l, cos_t, sin_t):
        def body(ql_ref, kvl_ref, ms_ref, gq_ref, gkv_ref, wq_ref, wk_ref, wv_ref, c_ref, s_ref, q_ref, k_ref, v_ref):
            cos1, sin1 = c_ref[...], s_ref[...]
            qn, _, _ = _rms(ql_ref[...], gq_ref[...])
            q = _nn(qn.astype(BF16), wq_ref[...])
            q_ref[...] = (_rope(q, jnp.tile(cos1, (1, MH)), jnp.tile(sin1, (1, MH))) * MLA_SCALE).astype(BF16)
            kvn, _, _ = _rms(kvl_ref[...], gkv_ref[...])
            kvb = kvn.astype(BF16)
            lane = _lane((TM, LANES))
            kr = jnp.where((lane >= ROPE_LO) & (lane < ROPE_LO + ROPE_DIM), ms_ref[...], 0.0)
            kr = _rope(kr, cos1, sin1)
            k_ref[...] = (_nn(kvb, wk_ref[...]) + jnp.tile(kr, (1, MH))).astype(BF16)
            v_ref[...] = _nn(kvb, wv_ref[...]).astype(BF16)

        return pl.pallas_call(
            body, name="mla_prep", grid=(n_tok,),
            in_specs=[tok(QL, (cfg.o_ql - ZO) // QL), tok(KVL, (cfg.o_kv - ZO) // KVL), tok(LANES, misc_blk),
                      const((1, QL)), const((1, KVL)), layer(l, (QL, QW)), layer(l, (KVL, QW)), layer(l, (KVL, MW)),
                      tok(LANES), tok(LANES)],
            out_specs=[tok(QW), tok(QW), tok(MW)],
            out_shape=[jax.ShapeDtypeStruct((S, QW), BF16), jax.ShapeDtypeStruct((S, QW), BF16),
                       jax.ShapeDtypeStruct((S, MW), BF16)],
            compiler_params=_params(("parallel",)))(z, z, z, gq, gkv, w_uq, w_uk, w_v, cos_t, sin_t)

    def _allowed(q0, k0, chunked):
        qi = q0 + lax.broadcasted_iota(jnp.int32, (TQ, TK), 0)
        ki = k0 + lax.broadcasted_iota(jnp.int32, (TQ, TK), 1)
        if chunked:
            return (ki >> CHUNK_SHIFT) <= (qi >> CHUNK_SHIFT)
        return ki <= qi

    def _heads(val, packed):
        if packed:
            lane = _lane(val.shape)
            zero = jnp.zeros_like(val)
            return [jnp.where(lane < HEAD_DIM, val, zero), jnp.where(lane >= HEAD_DIM, val, zero)]
        return [val[:, :HEAD_PAD], val[:, HEAD_PAD:]]

    def _merge(a0, a1):
        return jnp.where(_lane(a0.shape) < HEAD_DIM, a0, a1)

    def attn_fwd(q, k, v, q_blk0, k_blk0, v_blk0, cumt, packed, chunked, name, side=None):
        has_bias = cumt is not None
        n_pairs = (FH if packed else MH) // 2
        wq = LANES if packed else 2 * HEAD_PAD
        assert TQ == TK

        n_main = 4 if has_bias else 3
        n_side_in, n_side_out = (len(side.ins), len(side.outs)) if side else (0, 0)

        def body(*refs):
            if has_bias:
                q_ref, k_ref, v_ref, ct_ref = refs[:n_main]
            else:
                q_ref, k_ref, v_ref = refs[:n_main]
            o_ref, lse_ref = refs[n_main + n_side_in:n_main + n_side_in + 2]
            i = pl.program_id(1)
            if side:
                side_refs = (refs[n_main:n_main + n_side_in], refs[n_main + n_side_in + 2:n_main + n_side_in + 2 + n_side_out],
                             refs[-2], refs[-1])

                @pl.when((pl.program_id(0) == 0) & (i == 0))
                def _():
                    side.start(*side_refs)

            q0 = i * TQ
            qh = _heads(q_ref[...], packed)
            q_both = jnp.concatenate(qh, axis=0) if packed else None

            def scores(kb):
                kblk = k_ref[pl.ds(pl.multiple_of(kb * TK, TK), TK), :]
                if packed:
                    return _nt(kblk, q_both)
                return jnp.concatenate([_nt(kblk[:, :HEAD_PAD], qh[0]), _nt(kblk[:, HEAD_PAD:], qh[1])], axis=1)

            def update(kb, s, carry, masked):
                m, l, acc = carry
                k0 = pl.multiple_of(kb * TK, TK)
                s0, s1 = s[:, :TQ], s[:, TQ:]
                if has_bias:
                    ck = ct_ref[0, :, pl.ds(k0, TK), :]
                    s0, s1 = s0 - jnp.tile(ck[0], (1, TQ // LANES)), s1 - jnp.tile(ck[1], (1, TQ // LANES))
                if masked:
                    allow = jnp.transpose(_allowed(q0, k0, chunked))
                    s0, s1 = jnp.where(allow, s0, NEG), jnp.where(allow, s1, NEG)
                s = jnp.concatenate([s0, s1], axis=1)
                m_new = jnp.maximum(m, jnp.max(s, axis=0, keepdims=True))
                p = jnp.exp(s - m_new)
                alpha = jnp.exp(m - m_new)
                l = alpha * l + jnp.sum(p, axis=0, keepdims=True)
                return m_new, l, alpha * acc + _tn(v_ref[pl.ds(k0, TK), :], p.astype(BF16))

            def several(kb, carry, n, last_masked):
                ss = [scores(kb + u) for u in range(n)]
                for u in range(n):
                    carry = update(kb + u, ss[u], carry, last_masked and u == n - 1)
                return carry

            init = (jnp.full((1, 2 * TQ), NEG, F32), jnp.zeros((1, 2 * TQ), F32), jnp.zeros((LANES, 2 * TQ), F32))
            carry = lax.fori_loop(0, i // FWD_UNROLL, lambda t, cr: several(FWD_UNROLL * t, cr, FWD_UNROLL, False), init)
            m, l, acc = lax.switch(i % FWD_UNROLL,
                                   [lambda cr, r=r: several(i - r, cr, r + 1, True) for r in range(FWD_UNROLL)], carry)
            o = acc / l
            lse = m + jnp.log(l)
            o_ref[...] = jnp.transpose(jnp.concatenate([o[:HEAD_DIM, :TQ], o[HEAD_DIM:, TQ:]], axis=0))
            sub = lax.broadcasted_iota(jnp.int32, (LANES, TQ), 0)
            lse_ref[...] = jnp.transpose(jnp.where(sub == 0, lse[:, :TQ], jnp.where(sub == 1, lse[:, TQ:], 0.0)))
            if side:
                @pl.when((pl.program_id(0) == n_pairs - 1) & (i == S // TQ - 1))
                def _():
                    side.wait(*side_refs)

        in_specs = [pl.BlockSpec((TQ, wq), lambda p, i: (i, q_blk0 + p)),
                    pl.BlockSpec((S, wq), lambda p, i: (0, k_blk0 + p)),
                    pl.BlockSpec((S, LANES), lambda p, i: (0, v_blk0 + p))]
        args = [q, k, v]
        if has_bias:
            in_specs += [pl.BlockSpec((1, 2, S, LANES), lambda p, i: (p, 0, 0, 0))]
            args += [cumt]
        out_specs = [pl.BlockSpec((TQ, LANES), lambda p, i: (i, p)), pl.BlockSpec((TQ, LANES), lambda p, i: (i, p))]
        out_shape = [jax.ShapeDtypeStruct((S, n_pairs * LANES), F32), jax.ShapeDtypeStruct((S, n_pairs * LANES), F32)]
        extra = {}
        if side:
            hbm = pl.BlockSpec(memory_space=pl.ANY)
            in_specs += [hbm] * n_side_in
            args += side.ins
            out_specs += [hbm] * n_side_out
            out_shape += side.outs
            extra = dict(scratch_shapes=[pltpu.SemaphoreType.DMA((side.n_sems,)), pltpu.SemaphoreType.DMA((side.n_sems,))],
                         input_output_aliases={n_main + a: 2 + b for a, b in side.aliases.items()})
        outs = pl.pallas_call(
            body, name=name, grid=(n_pairs, S // TQ), in_specs=in_specs, out_specs=out_specs, out_shape=out_shape,
            compiler_params=_params(("arbitrary", "arbitrary")), **extra)(*args)
        return (outs[0], outs[1], list(outs[2:])) if side else outs

    def gate_outproj(of, om, z, w_out, l, x, gate):
        def body(of_ref, om_ref, fg_ref, mg_ref, w_ref, x_ref, gt_ref, xn_ref, u_ref, y_ref):
            fg, mg = fg_ref[...], mg_ref[...]
            u = jnp.concatenate([of_ref[...] * fg * _sigmoid(fg), om_ref[...] * mg * _sigmoid(mg)], axis=1).astype(BF16)
            y = _nn(u, w_ref[...])
            u_ref[...] = u
            y_ref[...] = y.astype(BF16)
            xn_ref[...] = x_ref[...] + gt_ref[...] * y

        return pl.pallas_call(
            body, name="gate_outproj", grid=(n_tok,),
            in_specs=[tok(FW), tok(MW), tok(FW, (cfg.o_fg - ZO) // FW), tok(MW, (cfg.o_mg - ZO) // MW),
                      layer(l, (FW + MW, D)), tok(D), const((1, D))],
            out_specs=[tok(D), tok(FW + MW), tok(D)],
            out_shape=[jax.ShapeDtypeStruct((S, D), F32), jax.ShapeDtypeStruct((S, FW + MW), BF16),
                       jax.ShapeDtypeStruct((S, D), BF16)],
            compiler_params=_params(("parallel",)))(of, om, z, z, w_out, x, gate)

    def final_loss(x, g, target):
        def body(x_ref, g_ref, t_ref, dx_ref, acc_ref, loss_ref):
            @pl.when(pl.program_id(0) == 0)
            def _():
                acc_ref[...] = jnp.zeros_like(acc_ref)
                loss_ref[...] = jnp.zeros_like(loss_ref)

            gv = g_ref[...]
            y, xh, rstd = _rms(x_ref[...], gv)
            e = y - t_ref[...]
            loss_ref[...] += 0.5 * jnp.sum(jnp.sum(e * e, axis=-1, keepdims=True) / D, axis=0, keepdims=True)
            dx, dg = _rms_bwd(e / D, gv, xh, rstd)
            dx_ref[...] = dx
            acc_ref[0:1, :] += dg

        return pl.pallas_call(
            body, name="final_loss", grid=(n_tok,),
            in_specs=[tok(D), const((1, D)), tok(D)],
            out_specs=[tok(D), const((8, D)), const((1, LANES))],
            out_shape=[jax.ShapeDtypeStruct((S, D), F32), jax.ShapeDtypeStruct((8, D), F32),
                       jax.ShapeDtypeStruct((1, LANES), F32)],
            compiler_params=_params(("arbitrary",)))(x, g, target)


    def bwd_out(dxn, gate, y, w_out, l, of, om, z, lse_f, lse_m, u):
        def body(dx_ref, gt_ref, y_ref, w_ref, of_ref, om_ref, fg_ref, mg_ref, lf_ref, lm_ref, u_ref,
                 dof_ref, dom_ref, dfg_ref, dmg_ref, dwo_ref, acc_ref, sf_ref, sm_ref):
            @pl.when(pl.program_id(0) == 0)
            def _():
                acc_ref[...] = jnp.zeros_like(acc_ref)
                dwo_ref[...] = jnp.zeros_like(dwo_ref)

            dxv = dx_ref[...]
            acc_ref[0:1, :] += jnp.sum(dxv * y_ref[...].astype(F32), axis=0, keepdims=True)
            dy = (gt_ref[...] * dxv).astype(BF16)
            dwo_ref[...] += _tn(u_ref[...], dy)
            du = _nt(dy, w_ref[...])
            lane = _lane((TM, LANES))
            for lo, width, o_ref, g_ref, do_ref, dg_ref, l_ref, st_ref in (
                    (0, FW, of_ref, fg_ref, dof_ref, dfg_ref, lf_ref, sf_ref),
                    (FW, MW, om_ref, mg_ref, dom_ref, dmg_ref, lm_ref, sm_ref)):
                gv, ov = g_ref[...], o_ref[...]
                sg = _sigmoid(gv)
                dup = du[:, lo:lo + width]
                dob = (dup * gv * sg).astype(BF16)
                do_ref[...] = dob
                dg_ref[...] = (dup * ov * sg * (1.0 + gv * (1.0 - sg))).astype(BF16)
                d = dob.astype(F32) * ov
                for pr in range(width // LANES):
                    cols = slice(pr * LANES, (pr + 1) * LANES)
                    dp = d[:, cols]
                    d0 = jnp.sum(jnp.where(lane < HEAD_DIM, dp, 0.0), axis=-1, keepdims=True)
                    d1 = jnp.sum(jnp.where(lane >= HEAD_DIM, dp, 0.0), axis=-1, keepdims=True)
                    st_ref[:, cols] = jnp.where(lane == 2, d0, jnp.where(lane == 3, d1, l_ref[:, cols]))

        return pl.pallas_call(
            body, name="bwd_out", grid=(n_tok,),
            in_specs=[tok(D), const((1, D)), tok(D), layer(l, (FW + MW, D)), tok(FW), tok(MW),
                      tok(FW, (cfg.o_fg - ZO) // FW), tok(MW, (cfg.o_mg - ZO) // MW), tok(FW), tok(MW), tok(FW + MW)],
            out_specs=[tok(FW), tok(MW), tok(FW), tok(MW), const((FW + MW, D)), const((8, D)), tok(FW), tok(MW)],
            out_shape=[jax.ShapeDtypeStruct((S, FW), BF16), jax.ShapeDtypeStruct((S, MW), BF16),
                       jax.ShapeDtypeStruct((S, FW), BF16), jax.ShapeDtypeStruct((S, MW), BF16),
                       jax.ShapeDtypeStruct((FW + MW, D), F32), jax.ShapeDtypeStruct((8, D), F32),
                       jax.ShapeDtypeStruct((S, FW), F32), jax.ShapeDtypeStruct((S, MW), F32)],
            compiler_params=_params(("arbitrary",)))(dxn, gate, y, w_out, of, om, z, z, lse_f, lse_m, u)

    def attn_bwd(q, k, v, do, stats, q_blk0, k_blk0, v_blk0, cumt, packed, chunked, name, side=None):
        has_bias = cumt is not None
        n_pairs = (FH if packed else MH) // 2
        wq = LANES if packed else 2 * HEAD_PAD
        n_q = S // TQ
        assert TQ == TK
        n_in, n_out = (6, 5) if has_bias else (5, 3)
        n_side_in, n_side_out = (len(side.ins), len(side.outs)) if side else (0, 0)

        def body(*refs):
            main_out = refs[n_in + n_side_in:n_in + n_side_in + n_out]
            if has_bias:
                q_ref, k_ref, v_ref, do_ref, st_ref, ct_ref = refs[:n_in]
                dq_ref, dk_ref, dv_ref, dc_ref, dr_ref = main_out
            else:
                q_ref, k_ref, v_ref, do_ref, st_ref = refs[:n_in]
                dq_ref, dk_ref, dv_ref = main_out
            jb = pl.program_id(1)
            k0 = jb * TK
            if side:
                side_refs = (refs[n_in:n_in + n_side_in], refs[n_in + n_side_in + n_out:n_in + n_side_in + n_out + n_side_out],
                             refs[-2], refs[-1])

                @pl.when((pl.program_id(0) == 0) & (jb == 0))
                def _():
                    side.start(*side_refs)

            @pl.when(jb == 0)
            def _():
                dq_ref[...] = jnp.zeros_like(dq_ref)
                if has_bias:
                    dr_ref[...] = jnp.zeros_like(dr_ref)

            dk_ref[...] = jnp.zeros_like(dk_ref)
            dv_ref[...] = jnp.zeros_like(dv_ref)
            kh = _heads(k_ref[...], packed)
            k_both = jnp.concatenate(kh, axis=0) if packed else None
            v_both = jnp.concatenate(_heads(v_ref[...], True), axis=0)
            if has_bias:
                ct = jnp.transpose(ct_ref[0, 0])
                ck = jnp.concatenate([ct[:, 0:1], ct[:, 1:2]], axis=0)

            def products(ib):
                rows = pl.ds(pl.multiple_of(ib * TQ, TQ), TQ)
                q2, do2 = q_ref[rows, :], do_ref[rows, :]
                if packed:
                    s = _nt(k_both, q2)
                else:
                    qh = _heads(q2, False)
                    s = jnp.concatenate([_nt(kh[0], qh[0]), _nt(kh[1], qh[1])], axis=0)
                return s, _nt(v_both, do2)

            def update(ib, s, dp, carry, masked):
                q0 = pl.multiple_of(ib * TQ, TQ)
                rows = pl.ds(q0, TQ)
                q2, do2 = q_ref[rows, :], do_ref[rows, :]
                st = jnp.transpose(st_ref[rows, :])
                if not packed:
                    qh = _heads(q2, False)
                if has_bias:
                    s = s - ck
                if masked:
                    allow = jnp.transpose(_allowed(q0, k0, chunked))
                    s = jnp.where(jnp.concatenate([allow, allow], axis=0), s, NEG)
                p = jnp.concatenate([jnp.exp(s[:TK] - st[0:1, :]), jnp.exp(s[TK:] - st[1:2, :])], axis=0)
                dv2 = _nn(p.astype(BF16), do2)
                ds = jnp.concatenate([p[:TK] * (dp[:TK] - st[2:3, :]), p[TK:] * (dp[TK:] - st[3:4, :])], axis=0)
                dsb = ds.astype(BF16)
                dv_ref[...] += _merge(dv2[:TK], dv2[TK:])
                if packed:
                    dk2 = _nn(dsb, q2)
                    dk_ref[...] += _merge(dk2[:TK], dk2[TK:])
                    dq_ref[rows, :] += _tn(dsb, k_both)
                else:
                    dk_ref[...] += jnp.concatenate([_nn(dsb[:TK], qh[0]), _nn(dsb[TK:], qh[1])], axis=1)
                    dq_ref[rows, :] += jnp.concatenate([_tn(dsb[:TK], kh[0]), _tn(dsb[TK:], kh[1])], axis=1)
                if has_bias:
                    sub = lax.broadcasted_iota(jnp.int32, (8, TQ), 0)
                    r0 = jnp.sum(ds[:TK], axis=0, keepdims=True)
                    r1 = jnp.sum(ds[TK:], axis=0, keepdims=True)
                    dr_ref[0, ib] += jnp.where(sub == 0, r0, jnp.where(sub == 1, r1, 0.0))
                    return carry - jnp.sum(ds, axis=1, keepdims=True)
                return carry

            def step(ib, carry, masked):
                s, dp = products(ib)
                return update(ib, s, dp, carry, masked)

            def two(ib, carry, first_masked):
                s_a, dp_a = products(ib)
                s_b, dp_b = products(ib + 1)
                return update(ib + 1, s_b, dp_b, update(ib, s_a, dp_a, carry, first_masked), False)

            n_blocks = n_q - jb
            dc = lax.cond(n_blocks >= 2, lambda cr: two(jb, cr, True), lambda cr: step(jb, cr, True),
                          jnp.zeros((2 * TK, 1), F32))
            n_rest = jnp.maximum(n_blocks - 2, 0)
            dc = lax.fori_loop(0, n_rest // 2, lambda t, cr: two(jb + 2 + 2 * t, cr, False), dc)
            dc = lax.cond(n_rest % 2 == 1, lambda cr: step(n_q - 1, cr, False), lambda cr: cr, dc)
            if has_bias:
                lane = _lane((TK, LANES))
                dc_ref[0] = jnp.where(lane == 0, dc[:TK], jnp.where(lane == 1, dc[TK:], 0.0))
            if side:
                @pl.when((pl.program_id(0) == n_pairs - 1) & (jb == S // TK - 1))
                def _():
                    side.wait(*side_refs)

        in_specs = [pl.BlockSpec((S, wq), lambda p, j: (0, q_blk0 + p)),
                    pl.BlockSpec((TK, wq), lambda p, j: (j, k_blk0 + p)),
                    pl.BlockSpec((TK, LANES), lambda p, j: (j, v_blk0 + p)),
                    pl.BlockSpec((S, LANES), lambda p, j: (0, p)),
                    pl.BlockSpec((S, LANES), lambda p, j: (0, p))]
        args = [q, k, v, do, stats]
        out_specs = [pl.BlockSpec((S, wq), lambda p, j: (0, p)),
                     pl.BlockSpec((TK, wq), lambda p, j: (j, p)),
                     pl.BlockSpec((TK, LANES), lambda p, j: (j, p))]
        out_shape = [jax.ShapeDtypeStruct((S, n_pairs * wq), F32), jax.ShapeDtypeStruct((S, n_pairs * wq), F32),
                     jax.ShapeDtypeStruct((S, n_pairs * LANES), F32)]
        if has_bias:
            in_specs += [pl.BlockSpec((1, 1, 8, TK), lambda p, j: (p, j, 0, 0))]
            args += [cumt]
            out_specs += [pl.BlockSpec((1, TK, LANES), lambda p, j: (p, j, 0)),
                          pl.BlockSpec((1, S // TQ, 8, TQ), lambda p, j: (p, 0, 0, 0))]
            out_shape += [jax.ShapeDtypeStruct((n_pairs, S, LANES), F32),
                          jax.ShapeDtypeStruct((n_pairs, S // TQ, 8, TQ), F32)]
        extra = {}
        if side:
            hbm = pl.BlockSpec(memory_space=pl.ANY)
            in_specs += [hbm] * n_side_in
            args += side.ins
            out_specs += [hbm] * n_side_out
            out_shape += side.outs
            extra = dict(scratch_shapes=[pltpu.SemaphoreType.DMA((side.n_sems,)), pltpu.SemaphoreType.DMA((side.n_sems,))],
                         input_output_aliases={n_in + a: n_out + b for a, b in side.aliases.items()})
        outs = pl.pallas_call(
            body, name=name, grid=(n_pairs, S // TK), in_specs=in_specs, out_specs=out_specs, out_shape=out_shape,
            compiler_params=_params(("arbitrary", "arbitrary")), **extra)(*args)
        return (*outs[:n_out], list(outs[n_out:])) if side else outs

    def fox_post(dcum, z, bf_pad):
        def rev(i):
            return n_tok - 1 - i

        def body(dc_ref, z_ref, b_ref, dff_ref, acc_ref, carry):
            @pl.when(pl.program_id(0) == 0)
            def _():
                carry[...] = jnp.zeros_like(carry)
                acc_ref[...] = jnp.zeros_like(acc_ref)

            _, sig_neg = _log_f_terms(z_ref[...], b_ref[...])
            row = lax.broadcasted_iota(jnp.int32, (TM, TM), 0)
            col = lax.broadcasted_iota(jnp.int32, (TM, TM), 1)
            tri = (col >= row).astype(F32)
            dlog = jnp.dot(tri, dc_ref[...], precision=lax.Precision.HIGHEST, preferred_element_type=F32) + carry[...]
            carry[...] = dlog[0:1, :]
            dff = dlog * sig_neg
            dff_ref[...] = dff
            acc_ref[0:1, :] += jnp.sum(dff, axis=0, keepdims=True)

        return pl.pallas_call(
            body, name="fox_post", grid=(n_tok,),
            in_specs=[pl.BlockSpec((TM, LANES), lambda i: (rev(i), 0)),
                      pl.BlockSpec((TM, LANES), lambda i: (rev(i), misc_blk)), const((1, LANES))],
            out_specs=[pl.BlockSpec((TM, LANES), lambda i: (rev(i), 0)), const((8, LANES))],
            out_shape=[jax.ShapeDtypeStruct((S, LANES), F32), jax.ShapeDtypeStruct((8, LANES), F32)],
            scratch_shapes=[pltpu.VMEM((1, LANES), F32)],
            compiler_params=_params(("arbitrary",)))(dcum, z, bf_pad)

    def mla_post(dq, dk, dv, dff, z, gq, gkv, w_uq, w_uk, w_v, l, cos_t, sin_t):
        def body(dq_ref, dk_ref, dv_ref, dff_ref, ql_ref, kvl_ref, gq_ref, gkv_ref, wq_ref, wk_ref, wv_ref,
                 c_ref, s_ref, zq_ref, zkv_ref, zms_ref, dwq_ref, dwk_ref, dwv_ref, dgq_ref, dgkv_ref):
            @pl.when(pl.program_id(0) == 0)
            def _():
                for r in (dwq_ref, dwk_ref, dwv_ref, dgq_ref, dgkv_ref):
                    r[...] = jnp.zeros_like(r)

            cos1, sin1 = c_ref[...], s_ref[...]
            gqv, gkvv = gq_ref[...], gkv_ref[...]
            qn, qxh, qrstd = _rms(ql_ref[...], gqv)
            dq_pre = _rope_t(dq_ref[...] * MLA_SCALE, jnp.tile(cos1, (1, MH)), jnp.tile(sin1, (1, MH))).astype(BF16)
            dwq_ref[...] += _tn(qn.astype(BF16), dq_pre)
            dql, dgq = _rms_bwd(_nt(dq_pre, wq_ref[...]), gqv, qxh, qrstd)
            zq_ref[...] = dql.astype(BF16)
            dgq_ref[0:1, :] += dgq

            dkv = dk_ref[...]
            lane = _lane(dkv.shape) & (HEAD_PAD - 1)
            dkn = jnp.where(lane < HEAD_DIM, dkv, 0.0).astype(BF16)
            dkr = dkv[:, 0:HEAD_PAD]
            for hd in range(1, MH):
                dkr = dkr + dkv[:, hd * HEAD_PAD:(hd + 1) * HEAD_PAD]
            lane1 = _lane(dkr.shape)
            dkr = jnp.where((lane1 >= ROPE_LO) & (lane1 < ROPE_LO + ROPE_DIM), dkr, 0.0)
            dkr = _rope_t(dkr, cos1, sin1)
            zms_ref[...] = (dkr + dff_ref[...]).astype(BF16)

            kvn, kxh, krstd = _rms(kvl_ref[...], gkvv)
            kvb = kvn.astype(BF16)
            dvb = dv_ref[...].astype(BF16)
            dwk_ref[...] += _tn(kvb, dkn)
            dwv_ref[...] += _tn(kvb, dvb)
            dkvl, dgkv = _rms_bwd(_nt(dkn, wk_ref[...]) + _nt(dvb, wv_ref[...]), gkvv, kxh, krstd)
            zkv_ref[...] = dkvl.astype(BF16)
            dgkv_ref[0:1, :] += dgkv

        return pl.pallas_call(
            body, name="mla_post", grid=(n_tok,),
            in_specs=[tok(QW), tok(QW), tok(MW), tok(LANES), tok(QL, (cfg.o_ql - ZO) // QL), tok(KVL, (cfg.o_kv - ZO) // KVL),
                      const((1, QL)), const((1, KVL)), layer(l, (QL, QW)), layer(l, (KVL, QW)), layer(l, (KVL, MW)),
                      tok(LANES), tok(LANES)],
            out_specs=[tok(QL), tok(KVL), tok(LANES), const((QL, QW)), const((KVL, QW)), const((KVL, MW)),
                       const((8, QL)), const((8, KVL))],
            out_shape=[jax.ShapeDtypeStruct((S, QL), BF16), jax.ShapeDtypeStruct((S, KVL), BF16),
                       jax.ShapeDtypeStruct((S, LANES), BF16), jax.ShapeDtypeStruct((QL, QW), F32),
                       jax.ShapeDtypeStruct((KVL, QW), F32), jax.ShapeDtypeStruct((KVL, MW), F32),
                       jax.ShapeDtypeStruct((8, QL), F32), jax.ShapeDtypeStruct((8, KVL), F32)],
            compiler_params=_params(("arbitrary",)))(dq, dk, dv, dff, z, z, gq, gkv, w_uq, w_uk, w_v, cos_t, sin_t)

    def bwd_in(dz, w_in, l, x, dxn, g, scale):
        def body(dz_ref, w_ref, x_ref, dx_ref, g_ref, sc_ref, o_ref, acc_ref):
            @pl.when(pl.program_id(0) == 0)
            def _():
                acc_ref[...] = jnp.zeros_like(acc_ref)

            dh = _nt(dz_ref[...], w_ref[...])
            gv, mod = g_ref[...], 1.0 + sc_ref[...]
            _, xh, rstd = _rms(x_ref[...], gv)
            t = dh * xh
            acc_ref[0:1, :] += jnp.sum(dh, axis=0, keepdims=True)
            acc_ref[1:2, :] += jnp.sum(t * gv, axis=0, keepdims=True)
            acc_ref[2:3, :] += jnp.sum(t * mod, axis=0, keepdims=True)
            dx, _ = _rms_bwd(dh, gv * mod, xh, rstd)
            o_ref[...] = dx_ref[...] + dx

        return pl.pallas_call(
            body, name="bwd_in", grid=(n_tok,),
            in_specs=[tok(NZ), layer(l, (D, NZ)), tok(D), tok(D), const((1, D)), const((1, D))],
            out_specs=[tok(D), const((8, D))],
            out_shape=[jax.ShapeDtypeStruct((S, D), F32), jax.ShapeDtypeStruct((8, D), F32)],
            compiler_params=_params(("arbitrary",)))(dz, w_in, x, dxn, g, scale)


    def pair_rows(cum):
        n_pairs = FH // 2
        ct = jnp.pad(cum[:, :FH].T.reshape(n_pairs, 2, S), ((0, 0), (0, 6), (0, 0)))
        return ct.reshape(n_pairs, 8, S // TK, TK).transpose(0, 2, 1, 3)

    def bias_grad(dc, dr):
        n_pairs = FH // 2
        d = dr.transpose(0, 2, 1, 3).reshape(n_pairs, 8, S)[:, :2, :].reshape(FH, S).T
        d = d + dc[:, :, :2].transpose(1, 0, 2).reshape(S, FH)
        return jnp.pad(d, ((0, 0), (0, LANES - FH)))

    def layer_forward(x, wl, mod, later_shards=None):
        shift, scale, gate = mod
        h, z, qkv = ln_inproj(x, wl.norm_g, scale, shift, wl.w_in, wl.l)
        cum, tiles = fox_prep(z, wl.bf_pad)
        ct = pair_rows(cum)
        fox = attn_fwd(qkv, qkv, qkv, cfg.o_fq // LANES, cfg.o_fk // LANES, cfg.o_fv // LANES, tiles, True, False,
                       "fox_fwd_gather" if later_shards else "fox_fwd", GatherOverIci(later_shards) if later_shards else None)
        of, lse_f = fox[0], fox[1]
        qp, kp, vp = mla_prep(z, wl.gq, wl.gkv, wl.w_uq, wl.w_uk, wl.w_v, wl.l, cfg.cos_t, cfg.sin_t)
        mla = attn_fwd(qp, kp, vp, 0, 0, 0, None, False, True, "mla_fwd_gather" if later_shards else "mla_fwd",
                       GatherToSibling(fox[2]) if later_shards else None)
        om, lse_m = mla[0], mla[1]
        xn, u, y = gate_outproj(of, om, z, wl.w_out, wl.l, x, gate)
        saved = types.SimpleNamespace(x=x, h=h, z=z, qkv=qkv, ct=ct, of=of, lse_f=lse_f, qp=qp, kp=kp, vp=vp,
                                      om=om, lse_m=lse_m, u=u, y=y)
        return (xn, saved, mla[2]) if later_shards else (xn, saved)

    def layer_backward(dxn, sv, wl, mod, side=None):
        shift, scale, gate = mod
        do_f, do_m, dfg, dmg, dw_out, acc_o, st_f, st_m = bwd_out(dxn, gate, sv.y, wl.w_out, wl.l, sv.of, sv.om, sv.z,
                                                                  sv.lse_f, sv.lse_m, sv.u)
        dfq, dfk, dfv, dck, dcr, *rode = attn_bwd(sv.qkv, sv.qkv, sv.qkv, do_f, st_f, cfg.o_fq // LANES, cfg.o_fk // LANES,
                                                  cfg.o_fv // LANES, sv.ct, True, False,
                                                  "fox_bwd_exchange" if side else "fox_bwd", side)
        dff, acc_f = fox_post(bias_grad(dck, dcr), sv.z, wl.bf_pad)
        dqp, dkp, dvp = attn_bwd(sv.qp, sv.kp, sv.vp, do_m, st_m, 0, 0, 0, None, False, True, "mla_bwd")
        zq, zkv, zms, dw_uq, dw_uk, dw_v, dgq, dgkv = mla_post(
            dqp, dkp, dvp, dff, sv.z, wl.gq, wl.gkv, wl.w_uq, wl.w_uk, wl.w_v, wl.l, cfg.cos_t, cfg.sin_t)
        dz = jnp.concatenate([(dfq * FOX_SCALE).astype(BF16), dfk.astype(BF16), dfv.astype(BF16), dfg, dmg, zq, zkv, zms], axis=1)
        dx, acc_i = bwd_in(dz, wl.w_in, wl.l, sv.x, dxn, wl.norm_g, scale)
        dw_in = matmul_tn(sv.h, dz, "dw_in")
        grads = types.SimpleNamespace(
            w_in=dw_in, w_out=dw_out, w_uq=dw_uq, w_uk=dw_uk, w_v=dw_v, gq=dgq[0], gkv=dgkv[0],
            b_f=acc_f[0, :FH], norm_g=acc_i[2], dmod=jnp.concatenate([acc_i[0], acc_i[1], acc_o[0]]))
        return (dx, grads, rode[0]) if side else (dx, grads)

    return types.SimpleNamespace(layer_forward=layer_forward, layer_backward=layer_backward, final_loss=final_loss)


def _pack_rows(parts, dtype, row_multiple):
    flat = jnp.concatenate([p.reshape(-1).astype(dtype) for p in parts])
    per = SLAB_COLS * row_multiple
    total = -(-flat.shape[0] // per) * per
    return jnp.pad(flat, (0, total - flat.shape[0])).reshape(total // SLAB_COLS, SLAB_COLS)


def _unpack(flat, shapes):
    out, off = [], 0
    for shp in shapes:
        n = 1
        for d in shp:
            n *= d
        out.append(flat[off:off + n].reshape(shp))
        off += n
    return out


def kernel(x, c, positions, norm_g, w_ada, b_ada, w_in, b_f, q_norm_g, w_uq, kv_norm_g, w_ukv, w_out, final_g, loss_target, m_norm_g, m_w_ada, m_b_ada, m_w_in, m_b_f, m_q_norm_g, m_w_uq, m_kv_norm_g, m_w_ukv, m_w_out, m_final_g, v_norm_g, v_w_ada, v_b_ada, v_w_in, v_b_f, v_q_norm_g, v_w_uq, v_kv_norm_g, v_w_ukv, v_w_out, v_final_g):
    S, D = x.shape[1], x.shape[2]
    L = norm_g.shape[0]
    FH = b_f.shape[1]
    QL, KVL = q_norm_g.shape[1], kv_norm_g.shape[1]
    MH = w_ukv.shape[2] * N_CHIPS // (2 * HEAD_DIM)
    FW, MW = FH * HEAD_DIM, MH * HEAD_DIM
    NA = w_ada.shape[2]
    n_in = w_in.shape[2] * N_CHIPS
    cfg = types.SimpleNamespace(S=S, D=D, FW=FW, MW=MW, QL=QL, KVL=KVL, FH=FH, MH=MH)
    cfg.o_fq, cfg.o_fk, cfg.o_fv, cfg.o_fg, cfg.o_mg = 0, FW, 2 * FW, 3 * FW, 4 * FW
    cfg.o_ql = 4 * FW + MW
    cfg.o_kv = cfg.o_ql + QL
    cfg.o_ms = cfg.o_kv + KVL
    cfg.NZ = cfg.o_ms + LANES
    assert FW == MW and FH % 2 == 0 and MH % 2 == 0 and cfg.o_ql % QL == 0 and cfg.o_kv % KVL == 0 and KVL == LANES
    assert n_in == 4 * FW + FH + QL + KVL + ROPE_DIM + MW

    mx, my, mc = _my_pos()
    my_chip = 2 * mx + my
    my_dev = 2 * my_chip + mc

    inv_freq = 1.0 / (ROPE_THETA ** (jnp.arange(0, ROPE_DIM, 2, dtype=F32) / ROPE_DIM))
    ang = positions[0].astype(F32)[:, None] * inv_freq
    cos, sin = jnp.cos(ang), jnp.sin(ang)
    cfg.cos_t = jnp.concatenate([jnp.ones((S, ROPE_LO), F32), cos, cos, jnp.ones((S, HEAD_PAD - ROPE_LO - ROPE_DIM), F32)], axis=1)
    cfg.sin_t = jnp.concatenate([jnp.zeros((S, ROPE_LO), F32), -sin, sin, jnp.zeros((S, HEAD_PAD - ROPE_LO - ROPE_DIM), F32)], axis=1)

    assert L % 4 == 0
    def lane_pad(a):
        return jnp.pad(a, ((0, 0),) * (a.ndim - 1) + ((0, -a.shape[-1] % LANES),))

    shards = (w_in, w_uq, w_ukv, w_out)
    padded = [lane_pad(w.astype(BF16)) for w in shards]

    def kernel_layouts(gathered, own):
        def all_chips(g, w, axis):
            return jnp.concatenate([jnp.where(my_chip == t, w.astype(BF16), g[t][..., :w.shape[-1]])
                                    for t in range(N_CHIPS)], axis=axis)

        n = own[0].shape[0]
        w_in_f = all_chips(gathered[0], own[0], 2)
        w_uq_f = all_chips(gathered[1], own[1], 2)
        w_ukv_f = all_chips(gathered[2], own[2], 2)
        w_out_f = all_chips(gathered[3], own[3], 1)
        sizes = (FW, FW, FW, FH, FW, QL, KVL, ROPE_DIM, MW)
        offs = [0]
        for sz in sizes:
            offs.append(offs[-1] + sz)
        fq_w, fk_w, fv_w, ff_w, fg_w, ql_w, kvl_w, kr_w, mg_w = [w_in_f[:, :, offs[i]:offs[i + 1]] for i in range(len(sizes))]
        zeros = lambda width: jnp.zeros((n, D, width), BF16)
        w_in_p = jnp.concatenate([fq_w, fk_w, fv_w, fg_w, mg_w, ql_w, kvl_w, ff_w, zeros(ROPE_LO - FH), kr_w,
                                  zeros(HEAD_PAD - ROPE_LO - ROPE_DIM)], axis=2)
        w_uq_p = jnp.pad(w_uq_f.reshape(n, QL, MH, HEAD_DIM + ROPE_DIM),
                         ((0, 0), (0, 0), (0, 0), (0, HEAD_PAD - HEAD_DIM - ROPE_DIM))).reshape(n, QL, MH * HEAD_PAD)
        w_ukv4 = w_ukv_f.reshape(n, KVL, MH, 2 * HEAD_DIM)
        w_uk_p = jnp.pad(w_ukv4[..., :HEAD_DIM], ((0, 0), (0, 0), (0, 0), (0, HEAD_PAD - HEAD_DIM))).reshape(n, KVL, MH * HEAD_PAD)
        return types.SimpleNamespace(w_in=w_in_p, w_uq=w_uq_p, w_uk=w_uk_p, w_v=w_ukv4[..., HEAD_DIM:].reshape(n, KVL, MW),
                                     w_out=w_out_f)

    first = kernel_layouts(weights_gather([p[:1] for p in padded], "weights_first"), [w[:1] for w in shards])

    c_all = allgather8(c.reshape(8, D // 8), "gather_c").reshape(N_DEV, D)
    c_pad = jnp.pad(c_all, ((0, 16 - N_DEV), (0, 0)))
    mod_part = ada_forward(c_pad, w_ada)[:, :N_DEV, :]
    mod_all = allgather8(mod_part.reshape(-1, LANES), "gather_mod").reshape(N_CHIPS, 2, L, N_DEV, NA)[:, 0]
    mod_full = mod_all.transpose(1, 2, 0, 3).reshape(L, N_DEV, N_CHIPS * NA) + b_ada[:, None, :]
    mod_mine = lax.dynamic_index_in_dim(mod_full, my_dev, axis=1, keepdims=True)

    step = make_step(cfg)
    bf_pad = jnp.pad(b_f, ((0, 0), (0, LANES - FH)))
    mods = [(mod_mine[l, :, :D], mod_mine[l, :, D:2 * D], mod_mine[l, :, 2 * D:]) for l in range(L)]

    def layer_params(l, ws_, at):
        return types.SimpleNamespace(l=at, norm_g=norm_g[l][None], bf_pad=bf_pad[l][None], gq=q_norm_g[l][None],
                                     gkv=kv_norm_g[l][None], w_in=ws_.w_in, w_uq=ws_.w_uq, w_uk=ws_.w_uk, w_v=ws_.w_v,
                                     w_out=ws_.w_out)

    layers = [layer_params(0, first, 0)]
    xl, sv, later = step.layer_forward(x[0], layers[0], mods[0], [p[1:] for p in padded])
    saved = [sv]
    rest = kernel_layouts(later, [w[1:] for w in shards])
    for l in range(1, L):
        layers.append(layer_params(l, rest, l - 1))
        xl, sv = step.layer_forward(xl, layers[l], mods[l])
        saved.append(sv)
    dx, acc_fin, loss_part = step.final_loss(xl, final_g[None], loss_target[0])
    loss = lax.psum(loss_part[0, 0], ("x", "y", "c"))
    ms = cfg.o_ms
    runs = [(0, 3 * FW, 0), (3 * FW, FH, ms), (3 * FW + FH, FW, cfg.o_fg), (4 * FW + FH, QL + KVL, cfg.o_ql),
            (4 * FW + FH + QL + KVL, ROPE_DIM, ms + ROPE_LO), (4 * FW + FH + QL + KVL + ROPE_DIM, MW, cfg.o_mg)]
    big_names = ["w_in", "w_uq", "w_ukv", "w_out"]

    def shard_columns(dw, t):
        a, b = t * w_in.shape[2], (t + 1) * w_in.shape[2]
        return jnp.concatenate([dw[:, p0 + max(a, r0) - r0:p0 + min(b, r0 + sz) - r0]
                                for r0, sz, p0 in runs if max(a, r0) < min(b, r0 + sz)], axis=1)

    def chip_parts(group, tag):
        n = len(group)
        stk = lambda name: jnp.stack([getattr(g, name) for g in group])
        g_in4 = jnp.stack([jnp.stack([shard_columns(g.w_in, t) for g in group]) for t in range(N_CHIPS)])
        dw_uq_f = stk("w_uq").reshape(n, QL, MH, HEAD_PAD)[..., :HEAD_DIM + ROPE_DIM].reshape(n, QL, -1)
        dw_ukv_f = jnp.concatenate([stk("w_uk").reshape(n, KVL, MH, HEAD_PAD)[..., :HEAD_DIM],
                                    stk("w_v").reshape(n, KVL, MH, HEAD_DIM)], axis=3).reshape(n, KVL, -1)
        gs = [g_in4, dw_uq_f.reshape(n, QL, N_CHIPS, -1).transpose(2, 0, 1, 3),
              dw_ukv_f.reshape(n, KVL, N_CHIPS, -1).transpose(2, 0, 1, 3), stk("w_out").reshape(n, N_CHIPS, -1, D).transpose(1, 0, 2, 3)]
        gs = [lane_pad(g.astype(BF16)) for g in gs]
        theirs = halves_to_sibling(gs, "grads_sibling_" + tag)
        out = []
        for g, o, nm in zip(gs, theirs, big_names):
            keep = lax.dynamic_slice_in_dim(g, mc * (n // 2), n // 2, axis=1)
            merged = (N_CHIPS * (n // 2),) + g.shape[2:]
            out.append(add_cast(keep.reshape(merged), o.reshape(merged), BF16, "grads_chip_sum_%s_%s" % (nm, tag)).reshape(o.shape))
        return out

    def reduced(chip_part, parts, tag):
        where = jnp.stack([my_chip, mc]).astype(jnp.int32)
        red = [sum_chips(p, o, where, "grads_sum_%s_%s" % (nm, tag)) for p, o, nm in zip(parts, chip_part, big_names)]
        return [g[..., :w.shape[-1]] for g, w in zip(halves_gather(red, "grads_back_" + tag), shards)]

    gl = [None] * L
    half_l = L // 2
    for l in range(L - 1, half_l - 1, -1):
        dx, gl[l] = step.layer_backward(dx, saved[l], layers[l], mods[l])
    part_hi = chip_parts(gl[half_l:], "hi")
    dx, gl[half_l - 1], parts_hi = step.layer_backward(dx, saved[half_l - 1], layers[half_l - 1], mods[half_l - 1],
                                                       ExchangeOverIci(part_hi))
    for l in range(half_l - 2, -1, -1):
        dx, gl[l] = step.layer_backward(dx, saved[l], layers[l], mods[l])
    part_lo = chip_parts(gl[:half_l], "lo")
    parts_lo = run_side(ExchangeOverIci(part_lo), "grads_chips_lo")
    g_w_in, g_w_uq, g_w_ukv, g_w_out = [jnp.concatenate([lo, hi]) for lo, hi in
                                        zip(reduced(part_lo, parts_lo, "lo"), reduced(part_hi, parts_hi, "hi"))]
    grad_x = dx[None]

    stack = lambda name: jnp.stack([getattr(g, name) for g in gl])
    small_parts = [stack("norm_g"), stack("dmod"), stack("b_f"), stack("gq"), stack("gkv"), acc_fin[0]]
    small_shapes = [p.shape for p in small_parts]
    small = _pack_rows(small_parts, F32, 8).reshape(-1, LANES)
    small_all = allgather8(small, "gather_small").reshape(N_DEV, -1, LANES)
    small_sum = sum_leading(small_all, "sum_small")
    g_norm_g, g_b_ada, g_b_f, g_q_norm_g, g_kv_norm_g, g_final_g = _unpack(small_sum.reshape(-1), small_shapes)

    n_ng = L * D
    dmod_all = small_all.reshape(N_DEV, -1)[:, n_ng:n_ng + L * 3 * D].reshape(N_DEV, L, 3 * D)
    dmod_cols = lax.dynamic_slice_in_dim(dmod_all, my_chip * NA, NA, axis=2).transpose(1, 0, 2)
    g_w_ada = ada_backward(c_pad, jnp.pad(dmod_cols, ((0, 0), (0, 16 - N_DEV), (0, 0))))


    names = ["norm_g", "w_ada", "b_ada", "w_in", "b_f", "q_norm_g", "w_uq", "kv_norm_g", "w_ukv", "w_out", "final_g"]
    ws = dict(norm_g=norm_g, w_ada=w_ada, b_ada=b_ada, w_in=w_in, b_f=b_f, q_norm_g=q_norm_g, w_uq=w_uq,
              kv_norm_g=kv_norm_g, w_ukv=w_ukv, w_out=w_out, final_g=final_g)
    msd = dict(norm_g=m_norm_g, w_ada=m_w_ada, b_ada=m_b_ada, w_in=m_w_in, b_f=m_b_f, q_norm_g=m_q_norm_g, w_uq=m_w_uq,
               kv_norm_g=m_kv_norm_g, w_ukv=m_w_ukv, w_out=m_w_out, final_g=m_final_g)
    vsd = dict(norm_g=v_norm_g, w_ada=v_w_ada, b_ada=v_b_ada, w_in=v_w_in, b_f=v_b_f, q_norm_g=v_q_norm_g, w_uq=v_w_uq,
               kv_norm_g=v_kv_norm_g, w_ukv=v_w_ukv, w_out=v_w_out, final_g=v_final_g)
    gsd = dict(norm_g=g_norm_g, w_ada=g_w_ada, b_ada=g_b_ada, w_in=g_w_in, b_f=g_b_f, q_norm_g=g_q_norm_g, w_uq=g_w_uq,
               kv_norm_g=g_kv_norm_g, w_ukv=g_w_ukv, w_out=g_w_out, final_g=g_final_g)
    small_names = ["norm_g", "b_ada", "b_f", "q_norm_g", "kv_norm_g", "final_g"]
    sm_shapes = [ws[n].shape for n in small_names]
    pk = lambda d: _pack_rows([d[n] for n in small_names], F32, 8).reshape(1, -1, LANES)
    sm_out = adamw(pk(ws), pk(gsd), pk(msd), pk(vsd), "adamw_small")
    sm_d, sm_m, sm_v = [dict(zip(small_names, _unpack(o.reshape(-1), sm_shapes))) for o in sm_out]
    delta, new_m, new_v = dict(sm_d), dict(sm_m), dict(sm_v)
    for n in ["w_ada", "w_in", "w_uq", "w_ukv", "w_out"]:
        delta[n], new_m[n], new_v[n] = adamw(ws[n], gsd[n], msd[n], vsd[n], "adamw_" + n)

    return (loss, grad_x, *[gsd[n] for n in names], *[delta[n] for n in names],
            *[new_m[n] for n in names], *[new_v[n] for n in names])
```

```python
import types

import jax
import jax.numpy as jnp
from jax import lax
from jax.experimental import pallas as pl
from jax.experimental.pallas import tpu as pltpu

F32 = jnp.float32
BF16 = jnp.bfloat16
MESH = pl.DeviceIdType.MESH

N_CHIPS = 4
N_DEV = 8
HEAD_DIM = 64
ROPE_DIM = 32
ROPE_THETA = 10000.0
HEAD_PAD = 128
ROPE_LO = 64
ROPE_HALF = 16
CHUNK_SHIFT = 6
LANES = 128
EPS = 1e-6
NEG = -1e30
ADAM_LR = 0.001
ADAM_B1 = 0.9
ADAM_B2 = 0.999
ADAM_EPS = 1e-08
ADAM_WD = 0.01
ADAM_STEP = 10
VMEM_LIMIT = 48 * 1024 * 1024
SLAB_COLS = 1024


def _params(sem=None, vmem=VMEM_LIMIT):
    return pltpu.CompilerParams(dimension_semantics=sem, vmem_limit_bytes=vmem)


def _nn(a, b):
    return jnp.dot(a, b, preferred_element_type=F32)


def _nt(a, b):
    return lax.dot_general(a, b, (((1,), (1,)), ((), ())), preferred_element_type=F32)


def _tn(a, b):
    return lax.dot_general(a, b, (((0,), (0,)), ((), ())), preferred_element_type=F32)


def _sigmoid(x):
    return 1.0 / (1.0 + jnp.exp(-x))


def _lane(shape):
    return lax.broadcasted_iota(jnp.int32, shape, len(shape) - 1)


def _pick(n, cands):
    for c in cands:
        if n % c == 0:
            return c
    return n


def _my_pos():
    return lax.axis_index("x"), lax.axis_index("y"), lax.axis_index("c")


def allgather8(xs, name):
    m_per, n = xs.shape

    def body(x_ref, out_ref, send_sems, recv_sems, local_sem):
        x, y, c = _my_pos()
        me, sibling = (x, y, c), (x, y, 1 - c)
        chips = [(1 - x, y), (x, 1 - y), (1 - x, 1 - y)]

        def rows(px, py, pc):
            return out_ref.at[pl.ds((4 * px + 2 * py + pc) * m_per, m_per), :]

        def copy(k, block, to, src=None):
            return pltpu.make_async_remote_copy(
                src_ref=rows(*block) if src is None else src, dst_ref=rows(*block),
                send_sem=send_sems.at[k], recv_sem=recv_sems.at[k], device_id=to, device_id_type=MESH)

        mine = pltpu.make_async_copy(x_ref, rows(*me), local_sem)
        mine.start()
        first = [copy(0, me, sibling, src=x_ref)]
        first += [copy(1 + j, me, (*chip, c), src=x_ref) for j, chip in enumerate(chips)]
        for cp in first:
            cp.start()
        passed = [copy(4 + j, (*chip, c), sibling) for j, chip in enumerate(chips)]
        for j, chip in enumerate(chips):
            copy(1 + j, (*chip, c), me).wait_recv()
            passed[j].start()
        copy(0, sibling, me).wait_recv()
        for j, chip in enumerate(chips):
            copy(4 + j, (*chip, 1 - c), me).wait_recv()
        for cp in first + passed:
            cp.wait_send()
        mine.wait()

    return pl.pallas_call(
        body, name=name,
        out_shape=jax.ShapeDtypeStruct((N_DEV * m_per, n), xs.dtype),
        in_specs=[pl.BlockSpec(memory_space=pltpu.VMEM)],
        out_specs=pl.BlockSpec(memory_space=pltpu.VMEM),
        scratch_shapes=[pltpu.SemaphoreType.DMA((7,)), pltpu.SemaphoreType.DMA((7,)), pltpu.SemaphoreType.DMA],
    )(xs)


def _remote(src, dst, send_sems, recv_sems, k, to):
    return pltpu.make_async_remote_copy(src_ref=src, dst_ref=dst, send_sem=send_sems.at[k], recv_sem=recv_sems.at[k],
                                        device_id=to, device_id_type=MESH)


def _hbm_call(body, name, ins, out_shapes, n_sems, aliases=None):
    hbm = pl.BlockSpec(memory_space=pl.ANY)
    scratch = [pltpu.SemaphoreType.DMA((n_sems,)), pltpu.SemaphoreType.DMA((n_sems,))]
    return pl.pallas_call(body, name=name, out_shape=out_shapes, in_specs=[hbm] * len(ins),
                          out_specs=[hbm] * len(out_shapes), scratch_shapes=scratch,
                          input_output_aliases=aliases or {})(*ins)


def _layers_half(ref, h, axis=0):
    size = ref.shape[axis] // 2
    idx = (slice(None),) * axis + (pl.ds(h * size, size),)
    return ref.at[idx]


class GatherOverIci:
    def __init__(self, ws):
        self.ins = list(ws)
        self.outs = [jax.ShapeDtypeStruct((N_CHIPS,) + w.shape, w.dtype) for w in ws]
        self.aliases = {}
        self.n_sems = 3 * len(ws)

    def _copies(self, x_refs, o_refs, send_sems, recv_sems):
        x, y, c = _my_pos()
        s = 2 * x + y
        out = []
        for a in range(len(x_refs)):
            for k, (tx, ty) in enumerate([(1 - x, y), (x, 1 - y), (1 - x, 1 - y)]):
                mine = _remote(_layers_half(x_refs[a], c, 1), _layers_half(o_refs[a].at[s], c, 1), send_sems, recv_sems,
                               3 * a + k, (tx, ty, c))
                got = _layers_half(o_refs[a].at[2 * tx + ty], c, 1)
                out.append((mine, _remote(got, got, send_sems, recv_sems, 3 * a + k, (tx, ty, c))))
        return out

    def start(self, x_refs, o_refs, send_sems, recv_sems):
        for mine, _ in self._copies(x_refs, o_refs, send_sems, recv_sems):
            mine.start()

    def wait(self, x_refs, o_refs, send_sems, recv_sems):
        for mine, theirs in self._copies(x_refs, o_refs, send_sems, recv_sems):
            theirs.wait_recv()
            mine.wait_send()


class GatherToSibling:
    def __init__(self, gathered):
        self.ins = list(gathered)
        self.outs = [jax.ShapeDtypeStruct(g.shape, g.dtype) for g in gathered]
        self.aliases = {a: a for a in range(len(gathered))}
        self.n_sems = 3 * len(gathered)

    def _copies(self, o_refs, send_sems, recv_sems):
        x, y, c = _my_pos()
        out = []
        for a in range(len(o_refs)):
            for k, (tx, ty) in enumerate([(1 - x, y), (x, 1 - y), (1 - x, 1 - y)]):
                got = _layers_half(o_refs[a].at[2 * tx + ty], c, 1)
                theirs = _layers_half(o_refs[a].at[2 * tx + ty], 1 - c, 1)
                out.append((_remote(got, got, send_sems, recv_sems, 3 * a + k, (x, y, 1 - c)),
                            _remote(theirs, theirs, send_sems, recv_sems, 3 * a + k, (x, y, 1 - c))))
        return out

    def start(self, x_refs, o_refs, send_sems, recv_sems):
        for mine, _ in self._copies(o_refs, send_sems, recv_sems):
            mine.start()

    def wait(self, x_refs, o_refs, send_sems, recv_sems):
        for mine, theirs in self._copies(o_refs, send_sems, recv_sems):
            theirs.wait_recv()
            mine.wait_send()


def run_side(side, name):
    n_in, n_out = len(side.ins), len(side.outs)

    def body(*refs):
        parts = refs[:n_in], refs[n_in:n_in + n_out], refs[n_in + n_out], refs[n_in + n_out + 1]
        side.start(*parts)
        side.wait(*parts)

    return _hbm_call(body, name, side.ins, side.outs, side.n_sems, aliases=side.aliases)


def weights_gather(ws, name):
    return run_side(GatherToSibling(run_side(GatherOverIci(ws), name + "_ici")), name + "_sibling")


def halves_to_sibling(gs, name):
    n = len(gs)

    def body(*refs):
        x_refs, o_refs, send_sems, recv_sems = refs[:n], refs[n:2 * n], refs[2 * n], refs[2 * n + 1]
        x, y, c = _my_pos()
        cps = [_remote(_layers_half(x_refs[a], 1 - c, axis=1), o_refs[a], send_sems, recv_sems, a, (x, y, 1 - c))
               for a in range(n)]
        for cp in cps:
            cp.start()
        for cp in cps:
            cp.wait()

    outs = [jax.ShapeDtypeStruct((g.shape[0], g.shape[1] // 2) + g.shape[2:], g.dtype) for g in gs]
    return _hbm_call(body, name, gs, outs, n)


class ExchangeOverIci:
    def __init__(self, xs):
        self.ins = list(xs)
        self.outs = [jax.ShapeDtypeStruct(v.shape, v.dtype) for v in xs]
        self.aliases = {}
        self.n_sems = 3 * len(xs)

    def _copies(self, x_refs, o_refs, send_sems, recv_sems):
        x, y, c = _my_pos()
        s = 2 * x + y
        out = []
        for a in range(len(x_refs)):
            for k, (tx, ty) in enumerate([(1 - x, y), (x, 1 - y), (1 - x, 1 - y)]):
                got = o_refs[a].at[2 * tx + ty]
                out.append((_remote(x_refs[a].at[2 * tx + ty], o_refs[a].at[s], send_sems, recv_sems, 3 * a + k, (tx, ty, c)),
                            _remote(got, got, send_sems, recv_sems, 3 * a + k, (tx, ty, c))))
        return out

    def start(self, x_refs, o_refs, send_sems, recv_sems):
        for mine, _ in self._copies(x_refs, o_refs, send_sems, recv_sems):
            mine.start()

    def wait(self, x_refs, o_refs, send_sems, recv_sems):
        for mine, theirs in self._copies(x_refs, o_refs, send_sems, recv_sems):
            theirs.wait_recv()
            mine.wait_send()


def halves_gather(xs, name):
    n = len(xs)

    def body(*refs):
        x_refs, o_refs, send_sems, recv_sems = refs[:n], refs[n:2 * n], refs[2 * n], refs[2 * n + 1]
        x, y, c = _my_pos()
        sends = []
        for a in range(n):
            cp = _remote(_layers_half(x_refs[a], c), _layers_half(o_refs[a], c), send_sems, recv_sems, a, (x, y, 1 - c))
            cp.start()
            sends.append(cp)
        for a in range(n):
            theirs = _layers_half(o_refs[a], 1 - c)
            _remote(theirs, theirs, send_sems, recv_sems, a, (x, y, 1 - c)).wait_recv()
        for cp in sends:
            cp.wait_send()

    outs = [jax.ShapeDtypeStruct(v.shape, v.dtype) for v in xs]
    return _hbm_call(body, name, xs, outs, n, aliases={a: a for a in range(n)})


def sum_leading(xs, name):
    n, r, c = xs.shape

    def body(x_ref, o_ref):
        acc = x_ref[0]
        for i in range(1, n):
            acc = acc + x_ref[i]
        o_ref[...] = acc

    return pl.pallas_call(body, name=name, out_shape=jax.ShapeDtypeStruct((r, c), xs.dtype))(xs)


def add_cast(a, b, out_dtype, name):
    n, r, c = a.shape
    tr = _pick(r, (512, 256, 128, 64, 32, 16))

    def body(a_ref, b_ref, o_ref):
        o_ref[...] = (a_ref[...].astype(F32) + b_ref[...].astype(F32)).astype(out_dtype)

    spec = pl.BlockSpec((1, tr, c), lambda i, j: (i, j, 0))
    return pl.pallas_call(body, name=name, grid=(n, r // tr), in_specs=[spec, spec], out_specs=spec,
                          out_shape=jax.ShapeDtypeStruct((n, r, c), out_dtype),
                          compiler_params=_params(("parallel", "parallel")))(a, b)


def sum_chips(parts, own, where, name):
    n, nl, r, c = parts.shape
    tr = _pick(r, (512, 256, 128, 64, 32, 16))

    def body(w_ref, p_ref, o_ref, out_ref):
        s = w_ref[0]
        acc = jnp.zeros(out_ref.shape, F32)
        for t in range(n):
            acc = acc + jnp.where(s == t, o_ref[0], p_ref[t]).astype(F32)
        out_ref[...] = acc

    grid_spec = pltpu.PrefetchScalarGridSpec(
        num_scalar_prefetch=1, grid=(nl, r // tr),
        in_specs=[pl.BlockSpec((n, 1, tr, c), lambda i, j, w: (0, i, j, 0)),
                  pl.BlockSpec((1, 1, tr, c), lambda i, j, w: (w[0], i, j, 0))],
        out_specs=pl.BlockSpec((1, tr, c), lambda i, j, w: (w[1] * nl + i, j, 0)))
    return pl.pallas_call(body, name=name, grid_spec=grid_spec, out_shape=jax.ShapeDtypeStruct((2 * nl, r, c), F32),
                          compiler_params=_params(("parallel", "parallel")))(where, parts, own)


def ada_forward(c_all, w_ada):
    nl, d, n = w_ada.shape
    nb = c_all.shape[0]

    def body(c_ref, w_ref, o_ref):
        cv = c_ref[...]
        ca = (cv * _sigmoid(cv)).astype(BF16)
        o_ref[0] = _nn(ca, w_ref[0].astype(BF16))

    return pl.pallas_call(
        body, name="ada_forward", grid=(nl,),
        in_specs=[pl.BlockSpec((nb, d), lambda l: (0, 0)), pl.BlockSpec((1, d, n), lambda l: (l, 0, 0))],
        out_specs=pl.BlockSpec((1, nb, n), lambda l: (l, 0, 0)),
        out_shape=jax.ShapeDtypeStruct((nl, nb, n), F32), compiler_params=_params(("parallel",)))(c_all, w_ada)


def ada_backward(c_all, dmod):
    nl, nb, n = dmod.shape
    d = c_all.shape[1]

    def body(c_ref, g_ref, o_ref):
        cv = c_ref[...]
        ca = (cv * _sigmoid(cv)).astype(BF16)
        o_ref[0] = _tn(ca, g_ref[0].astype(BF16))

    return pl.pallas_call(
        body, name="ada_backward", grid=(nl,),
        in_specs=[pl.BlockSpec((nb, d), lambda l: (0, 0)), pl.BlockSpec((1, nb, n), lambda l: (l, 0, 0))],
        out_specs=pl.BlockSpec((1, d, n), lambda l: (l, 0, 0)),
        out_shape=jax.ShapeDtypeStruct((nl, d, n), F32), compiler_params=_params(("parallel",)))(c_all, dmod)


def matmul_tn(a, b, name):
    k, m = a.shape
    n = b.shape[1]
    tm, tk = _pick(m, (512, 256, 128)), _pick(k, (512, 256, 128))
    tn = n if n * (tm * 8 + tk * 4) <= VMEM_LIMIT // 2 else _pick(n, (512, 256, 128))

    def body(a_ref, b_ref, o_ref, acc_ref):
        @pl.when(pl.program_id(2) == 0)
        def _():
            acc_ref[...] = jnp.zeros_like(acc_ref)

        acc_ref[...] += _tn(a_ref[...], b_ref[...])

        @pl.when(pl.program_id(2) == k // tk - 1)
        def _():
            o_ref[...] = acc_ref[...].astype(BF16)

    return pl.pallas_call(
        body, name=name, grid=(m // tm, n // tn, k // tk),
        in_specs=[pl.BlockSpec((tk, tm), lambda i, j, kk: (kk, i)), pl.BlockSpec((tk, tn), lambda i, j, kk: (kk, j))],
        out_specs=pl.BlockSpec((tm, tn), lambda i, j, kk: (i, j)),
        out_shape=jax.ShapeDtypeStruct((m, n), BF16), scratch_shapes=[pltpu.VMEM((tm, tn), F32)],
        compiler_params=_params(("parallel", "parallel", "arbitrary")))(a, b)


def adamw(w, g, m, v, name):
    nl, r, c = w.shape
    tr = _pick(r, (512, 256, 128, 64, 32, 16, 8))

    def body(w_ref, g_ref, m_ref, v_ref, d_ref, mo_ref, vo_ref):
        gv = g_ref[...]
        mn = ADAM_B1 * m_ref[...] + (1.0 - ADAM_B1) * gv
        vn = ADAM_B2 * v_ref[...] + (1.0 - ADAM_B2) * (gv * gv)
        m_hat = mn / (1.0 - ADAM_B1 ** ADAM_STEP)
        v_hat = vn / (1.0 - ADAM_B2 ** ADAM_STEP)
        d_ref[...] = -ADAM_LR * (m_hat / (jnp.sqrt(v_hat) + ADAM_EPS) + ADAM_WD * w_ref[...])
        mo_ref[...] = mn
        vo_ref[...] = vn

    spec = pl.BlockSpec((1, tr, c), lambda l, i: (l, i, 0))
    out = jax.ShapeDtypeStruct((nl, r, c), F32)
    return pl.pallas_call(body, name=name, grid=(nl, r // tr), in_specs=[spec] * 4, out_specs=[spec] * 3,
                          out_shape=[out] * 3, compiler_params=_params(("parallel", "parallel")))(w, g, m, v)


def _rope(t, cos_t, sin_t):
    w = t.shape[1]
    lane = _lane(t.shape) & (HEAD_PAD - 1)
    first_half = (lane >= ROPE_LO) & (lane < ROPE_LO + ROPE_HALF)
    partner = jnp.where(first_half, pltpu.roll(t, w - ROPE_HALF, 1), pltpu.roll(t, ROPE_HALF, 1))
    return t * cos_t + partner * sin_t


def _rope_t(dt, cos_t, sin_t):
    w = dt.shape[1]
    lane = _lane(dt.shape) & (HEAD_PAD - 1)
    first_half = (lane >= ROPE_LO) & (lane < ROPE_LO + ROPE_HALF)
    ds = dt * sin_t
    partner = jnp.where(first_half, pltpu.roll(ds, w - ROPE_HALF, 1), pltpu.roll(ds, ROPE_HALF, 1))
    return dt * cos_t + partner


def _rms(xv, g):
    rstd = lax.rsqrt(jnp.mean(xv * xv, axis=-1, keepdims=True) + EPS)
    xh = xv * rstd
    return xh * g, xh, rstd


def _rms_bwd(dy, g, xh, rstd):
    dxh = dy * g
    dx = rstd * (dxh - xh * jnp.mean(dxh * xh, axis=-1, keepdims=True))
    return dx, jnp.sum(dy * xh, axis=0, keepdims=True)


def make_step(cfg):
    S, D, NZ = cfg.S, cfg.D, cfg.NZ
    FW, MW, QL, KVL, FH, MH = cfg.FW, cfg.MW, cfg.QL, cfg.KVL, cfg.FH, cfg.MH
    QW = MH * HEAD_PAD
    TM = _pick(S, (512, 256, 128))
    TQ = TK = _pick(S, (256, 128))
    n_tok = S // TM
    ZO = 3 * FW
    NZR = NZ - ZO
    misc_blk = (cfg.o_ms - ZO) // LANES
    FWD_UNROLL = 8
    FOX_SCALE = HEAD_DIM ** -0.5
    MLA_SCALE = (HEAD_DIM + ROPE_DIM) ** -0.5

    def tok(width, col=0):
        return pl.BlockSpec((TM, width), lambda i: (i, col))

    def const(shape):
        return pl.BlockSpec(shape, lambda i: tuple(0 for _ in shape))

    def layer(l, shape):
        return pl.BlockSpec((None,) + shape, lambda i: (l,) + tuple(0 for _ in shape))


    def ln_inproj(x, g, scale, shift, w_in, l):
        def body(x_ref, g_ref, sc_ref, sh_ref, w_ref, h_ref, z_ref, qkv_ref):
            y, _, _ = _rms(x_ref[...], g_ref[...])
            hb = (y * (1.0 + sc_ref[...]) + sh_ref[...]).astype(BF16)
            h_ref[...] = hb
            z = _nn(hb, w_ref[...])
            z_ref[...] = z[:, ZO:]
            qkv_ref[:, :FW] = (z[:, :FW] * FOX_SCALE).astype(BF16)
            qkv_ref[:, FW:] = z[:, FW:ZO].astype(BF16)

        return pl.pallas_call(
            body, name="ln_inproj", grid=(n_tok,),
            in_specs=[tok(D), const((1, D)), const((1, D)), const((1, D)), layer(l, (D, NZ))],
            out_specs=[tok(D), tok(NZR), tok(ZO)],
            out_shape=[jax.ShapeDtypeStruct((S, D), BF16), jax.ShapeDtypeStruct((S, NZR), F32),
                       jax.ShapeDtypeStruct((S, ZO), BF16)],
            compiler_params=_params(("parallel",)))(x, g, scale, shift, w_in)

    def _log_f_terms(misc, bf):
        lane = _lane(misc.shape)
        a = misc + bf
        e = jnp.exp(-jnp.abs(a))
        logf = jnp.minimum(a, 0.0) - jnp.log(1.0 + e)
        sig_neg = jnp.where(a >= 0, e, 1.0) / (1.0 + e)
        valid = lane < FH
        return jnp.where(valid, logf, 0.0), jnp.where(valid, sig_neg, 0.0)

    def fox_prep(z, bf_pad):
        def body(z_ref, b_ref, o_ref, t_ref, carry):
            @pl.when(pl.program_id(0) == 0)
            def _():
                carry[...] = jnp.zeros_like(carry)

            logf, _ = _log_f_terms(z_ref[...], b_ref[...])
            row = lax.broadcasted_iota(jnp.int32, (TM, TM), 0)
            col = lax.broadcasted_iota(jnp.int32, (TM, TM), 1)
            tri = (col <= row).astype(F32)
            cum = jnp.dot(tri, logf, precision=lax.Precision.HIGHEST, preferred_element_type=F32) + carry[...]
            o_ref[...] = cum
            carry[...] = cum[TM - 1:TM, :]
            for hd in range(FH):
                t_ref[hd // 2, hd % 2] = jnp.broadcast_to(cum[:, hd:hd + 1], (TM, LANES))

        return pl.pallas_call(
            body, name="fox_prep", grid=(n_tok,),
            in_specs=[tok(LANES, misc_blk), const((1, LANES))],
            out_specs=[tok(LANES), pl.BlockSpec((FH // 2, 2, TM, LANES), lambda i: (0, 0, i, 0))],
            out_shape=[jax.ShapeDtypeStruct((S, LANES), F32), jax.ShapeDtypeStruct((FH // 2, 2, S, LANES), F32)],
            scratch_shapes=[pltpu.VMEM((1, LANES), F32)],
            compiler_params=_params(("arbitrary",)))(z, bf_pad)

    def mla_prep(z, gq, gkv, w_uq, w_uk, w_v, l, cos_t, sin_t):
        def body(ql_ref, kvl_ref, ms_ref, gq_ref, gkv_ref, wq_ref, wk_ref, wv_ref, c_ref, s_ref, q_ref, k_ref, v_ref):
            cos1, sin1 = c_ref[...], s_ref[...]
            qn, _, _ = _rms(ql_ref[...], gq_ref[...])
            q = _nn(qn.astype(BF16), wq_ref[...])
            q_ref[...] = (_rope(q, jnp.tile(cos1, (1, MH)), jnp.tile(sin1, (1, MH))) * MLA_SCALE).astype(BF16)
            kvn, _, _ = _rms(kvl_ref[...], gkv_ref[...])
            kvb = kvn.astype(BF16)
            lane = _lane((TM, LANES))
            kr = jnp.where((lane >= ROPE_LO) & (lane < ROPE_LO + ROPE_DIM), ms_ref[...], 0.0)
            kr = _rope(kr, cos1, sin1)
            k_ref[...] = (_nn(kvb, wk_ref[...]) + jnp.tile(kr, (1, MH))).astype(BF16)
            v_ref[...] = _nn(kvb, wv_ref[...]).astype(BF16)

        return pl.pallas_call(
            body, name="mla_prep", grid=(n_tok,),
            in_specs=[tok(QL, (cfg.o_ql - ZO) // QL), tok(KVL, (cfg.o_kv - ZO) // KVL), tok(LANES, misc_blk),
                      const((1, QL)), const((1, KVL)), layer(l, (QL, QW)), layer(l, (KVL, QW)), layer(l, (KVL, MW)),
                      tok(LANES), tok(LANES)],
            out_specs=[tok(QW), tok(QW), tok(MW)],
            out_shape=[jax.ShapeDtypeStruct((S, QW), BF16), jax.ShapeDtypeStruct((S, QW), BF16),
                       jax.ShapeDtypeStruct((S, MW), BF16)],
            compiler_params=_params(("parallel",)))(z, z, z, gq, gkv, w_uq, w_uk, w_v, cos_t, sin_t)

    def _allowed(q0, k0, chunked):
        qi = q0 + lax.broadcasted_iota(jnp.int32, (TQ, TK), 0)
        ki = k0 + lax.broadcasted_iota(jnp.int32, (TQ, TK), 1)
        if chunked:
            return (ki >> CHUNK_SHIFT) <= (qi >> CHUNK_SHIFT)
        return ki <= qi

    def _heads(val, packed):
        if packed:
            lane = _lane(val.shape)
            zero = jnp.zeros_like(val)
            return [jnp.where(lane < HEAD_DIM, val, zero), jnp.where(lane >= HEAD_DIM, val, zero)]
        return [val[:, :HEAD_PAD], val[:, HEAD_PAD:]]

    def _merge(a0, a1):
        return jnp.where(_lane(a0.shape) < HEAD_DIM, a0, a1)

    def attn_fwd(q, k, v, q_blk0, k_blk0, v_blk0, cumt, packed, chunked, name, side=None):
        has_bias = cumt is not None
        n_pairs = (FH if packed else MH) // 2
        wq = LANES if packed else 2 * HEAD_PAD
        assert TQ == TK

        n_main = 4 if has_bias else 3
        n_side_in, n_side_out = (len(side.ins), len(side.outs)) if side else (0, 0)

        def body(*refs):
            if has_bias:
                q_ref, k_ref, v_ref, ct_ref = refs[:n_main]
            else:
                q_ref, k_ref, v_ref = refs[:n_main]
            o_ref, lse_ref = refs[n_main + n_side_in:n_main + n_side_in + 2]
            i = pl.program_id(1)
            if side:
                side_refs = (refs[n_main:n_main + n_side_in], refs[n_main + n_side_in + 2:n_main + n_side_in + 2 + n_side_out],
                             refs[-2], refs[-1])

                @pl.when((pl.program_id(0) == 0) & (i == 0))
                def _():
                    side.start(*side_refs)

            q0 = i * TQ
            qh = _heads(q_ref[...], packed)
            q_both = jnp.concatenate(qh, axis=0) if packed else None

            def scores(kb):
                kblk = k_ref[pl.ds(pl.multiple_of(kb * TK, TK), TK), :]
                if packed:
                    return _nt(kblk, q_both)
                return jnp.concatenate([_nt(kblk[:, :HEAD_PAD], qh[0]), _nt(kblk[:, HEAD_PAD:], qh[1])], axis=1)

            def update(kb, s, carry, masked):
                m, l, acc = carry
                k0 = pl.multiple_of(kb * TK, TK)
                s0, s1 = s[:, :TQ], s[:, TQ:]
                if has_bias:
                    ck = ct_ref[0, :, pl.ds(k0, TK), :]
                    s0, s1 = s0 - jnp.tile(ck[0], (1, TQ // LANES)), s1 - jnp.tile(ck[1], (1, TQ // LANES))
                if masked:
                    allow = jnp.transpose(_allowed(q0, k0, chunked))
                    s0, s1 = jnp.where(allow, s0, NEG), jnp.where(allow, s1, NEG)
                s = jnp.concatenate([s0, s1], axis=1)
                m_new = jnp.maximum(m, jnp.max(s, axis=0, keepdims=True))
                p = jnp.exp(s - m_new)
                alpha = jnp.exp(m - m_new)
                l = alpha * l + jnp.sum(p, axis=0, keepdims=True)
                return m_new, l, alpha * acc + _tn(v_ref[pl.ds(k0, TK), :], p.astype(BF16))

            def several(kb, carry, n, last_masked):
                ss = [scores(kb + u) for u in range(n)]
                for u in range(n):
                    carry = update(kb + u, ss[u], carry, last_masked and u == n - 1)
                return carry

            init = (jnp.full((1, 2 * TQ), NEG, F32), jnp.zeros((1, 2 * TQ), F32), jnp.zeros((LANES, 2 * TQ), F32))
            carry = lax.fori_loop(0, i // FWD_UNROLL, lambda t, cr: several(FWD_UNROLL * t, cr, FWD_UNROLL, False), init)
            m, l, acc = lax.switch(i % FWD_UNROLL,
                                   [lambda cr, r=r: several(i - r, cr, r + 1, True) for r in range(FWD_UNROLL)], carry)
            o = acc / l
            lse = m + jnp.log(l)
            o_ref[...] = jnp.transpose(jnp.concatenate([o[:HEAD_DIM, :TQ], o[HEAD_DIM:, TQ:]], axis=0))
            sub = lax.broadcasted_iota(jnp.int32, (LANES, TQ), 0)
            lse_ref[...] = jnp.transpose(jnp.where(sub == 0, lse[:, :TQ], jnp.where(sub == 1, lse[:, TQ:], 0.0)))
            if side:
                @pl.when((pl.program_id(0) == n_pairs - 1) & (i == S // TQ - 1))
                def _():
                    side.wait(*side_refs)

        in_specs = [pl.BlockSpec((TQ, wq), lambda p, i: (i, q_blk0 + p)),
                    pl.BlockSpec((S, wq), lambda p, i: (0, k_blk0 + p)),
                    pl.BlockSpec((S, LANES), lambda p, i: (0, v_blk0 + p))]
        args = [q, k, v]
        if has_bias:
            in_specs += [pl.BlockSpec((1, 2, S, LANES), lambda p, i: (p, 0, 0, 0))]
            args += [cumt]
        out_specs = [pl.BlockSpec((TQ, LANES), lambda p, i: (i, p)), pl.BlockSpec((TQ, LANES), lambda p, i: (i, p))]
        out_shape = [jax.ShapeDtypeStruct((S, n_pairs * LANES), F32), jax.ShapeDtypeStruct((S, n_pairs * LANES), F32)]
        extra = {}
        if side:
            hbm = pl.BlockSpec(memory_space=pl.ANY)
            in_specs += [hbm] * n_side_in
            args += side.ins
            out_specs += [hbm] * n_side_out
            out_shape += side.outs
            extra = dict(scratch_shapes=[pltpu.SemaphoreType.DMA((side.n_sems,)), pltpu.SemaphoreType.DMA((side.n_sems,))],
                         input_output_aliases={n_main + a: 2 + b for a, b in side.aliases.items()})
        outs = pl.pallas_call(
            body, name=name, grid=(n_pairs, S // TQ), in_specs=in_specs, out_specs=out_specs, out_shape=out_shape,
            compiler_params=_params(("arbitrary", "arbitrary")), **extra)(*args)
        return (outs[0], outs[1], list(outs[2:])) if side else outs

    def gate_outproj(of, om, z, w_out, l, x, gate):
        def body(of_ref, om_ref, fg_ref, mg_ref, w_ref, x_ref, gt_ref, xn_ref, u_ref, y_ref):
            fg, mg = fg_ref[...], mg_ref[...]
            u = jnp.concatenate([of_ref[...] * fg * _sigmoid(fg), om_ref[...] * mg * _sigmoid(mg)], axis=1).astype(BF16)
            y = _nn(u, w_ref[...])
            u_ref[...] = u
            y_ref[...] = y.astype(BF16)
            xn_ref[...] = x_ref[...] + gt_ref[...] * y

        return pl.pallas_call(
            body, name="gate_outproj", grid=(n_tok,),
            in_specs=[tok(FW), tok(MW), tok(FW, (cfg.o_fg - ZO) // FW), tok(MW, (cfg.o_mg - ZO) // MW),
                      layer(l, (FW + MW, D)), tok(D), const((1, D))],
            out_specs=[tok(D), tok(FW + MW), tok(D)],
            out_shape=[jax.ShapeDtypeStruct((S, D), F32), jax.ShapeDtypeStruct((S, FW + MW), BF16),
                       jax.ShapeDtypeStruct((S, D), BF16)],
            compiler_params=_params(("parallel",)))(of, om, z, z, w_out, x, gate)

    def final_loss(x, g, target):
        def body(x_ref, g_ref, t_ref, dx_ref, acc_ref, loss_ref):
            @pl.when(pl.program_id(0) == 0)
            def _():
                acc_ref[...] = jnp.zeros_like(acc_ref)
                loss_ref[...] = jnp.zeros_like(loss_ref)

            gv = g_ref[...]
            y, xh, rstd = _rms(x_ref[...], gv)
            e = y - t_ref[...]
            loss_ref[...] += 0.5 * jnp.sum(jnp.sum(e * e, axis=-1, keepdims=True) / D, axis=0, keepdims=True)
            dx, dg = _rms_bwd(e / D, gv, xh, rstd)
            dx_ref[...] = dx
            acc_ref[0:1, :] += dg

        return pl.pallas_call(
            body, name="final_loss", grid=(n_tok,),
            in_specs=[tok(D), const((1, D)), tok(D)],
            out_specs=[tok(D), const((8, D)), const((1, LANES))],
            out_shape=[jax.ShapeDtypeStruct((S, D), F32), jax.ShapeDtypeStruct((8, D), F32),
                       jax.ShapeDtypeStruct((1, LANES), F32)],
            compiler_params=_params(("arbitrary",)))(x, g, target)


    def bwd_out(dxn, gate, y, w_out, l, of, om, z, lse_f, lse_m, u):
        def body(dx_ref, gt_ref, y_ref, w_ref, of_ref, om_ref, fg_ref, mg_ref, lf_ref, lm_ref, u_ref,
                 dof_ref, dom_ref, dfg_ref, dmg_ref, dwo_ref, acc_ref, sf_ref, sm_ref):
            @pl.when(pl.program_id(0) == 0)
            def _():
                acc_ref[...] = jnp.zeros_like(acc_ref)
                dwo_ref[...] = jnp.zeros_like(dwo_ref)

            dxv = dx_ref[...]
            acc_ref[0:1, :] += jnp.sum(dxv * y_ref[...].astype(F32), axis=0, keepdims=True)
            dy = (gt_ref[...] * dxv).astype(BF16)
            dwo_ref[...] += _tn(u_ref[...], dy)
            du = _nt(dy, w_ref[...])
            lane = _lane((TM, LANES))
            for lo, width, o_ref, g_ref, do_ref, dg_ref, l_ref, st_ref in (
                    (0, FW, of_ref, fg_ref, dof_ref, dfg_ref, lf_ref, sf_ref),
                    (FW, MW, om_ref, mg_ref, dom_ref, dmg_ref, lm_ref, sm_ref)):
                gv, ov = g_ref[...], o_ref[...]
                sg = _sigmoid(gv)
                dup = du[:, lo:lo + width]
                dob = (dup * gv * sg).astype(BF16)
                do_ref[...] = dob
                dg_ref[...] = (dup * ov * sg * (1.0 + gv * (1.0 - sg))).astype(BF16)
                d = dob.astype(F32) * ov
                for pr in range(width // LANES):
                    cols = slice(pr * LANES, (pr + 1) * LANES)
                    dp = d[:, cols]
                    d0 = jnp.sum(jnp.where(lane < HEAD_DIM, dp, 0.0), axis=-1, keepdims=True)
                    d1 = jnp.sum(jnp.where(lane >= HEAD_DIM, dp, 0.0), axis=-1, keepdims=True)
                    st_ref[:, cols] = jnp.where(lane == 2, d0, jnp.where(lane == 3, d1, l_ref[:, cols]))

        return pl.pallas_call(
            body, name="bwd_out", grid=(n_tok,),
            in_specs=[tok(D), const((1, D)), tok(D), layer(l, (FW + MW, D)), tok(FW), tok(MW),
                      tok(FW, (cfg.o_fg - ZO) // FW), tok(MW, (cfg.o_mg - ZO) // MW), tok(FW), tok(MW), tok(FW + MW)],
            out_specs=[tok(FW), tok(MW), tok(FW), tok(MW), const((FW + MW, D)), const((8, D)), tok(FW), tok(MW)],
            out_shape=[jax.ShapeDtypeStruct((S, FW), BF16), jax.ShapeDtypeStruct((S, MW), BF16),
                       jax.ShapeDtypeStruct((S, FW), BF16), jax.ShapeDtypeStruct((S, MW), BF16),
                       jax.ShapeDtypeStruct((FW + MW, D), F32), jax.ShapeDtypeStruct((8, D), F32),
                       jax.ShapeDtypeStruct((S, FW), F32), jax.ShapeDtypeStruct((S, MW), F32)],
            compiler_params=_params(("arbitrary",)))(dxn, gate, y, w_out, of, om, z, z, lse_f, lse_m, u)

    def attn_bwd(q, k, v, do, stats, q_blk0, k_blk0, v_blk0, cumt, packed, chunked, name, side=None):
        has_bias = cumt is not None
        n_pairs = (FH if packed else MH) // 2
        wq = LANES if packed else 2 * HEAD_PAD
        n_q = S // TQ
        assert TQ == TK
        n_in, n_out = (6, 5) if has_bias else (5, 3)
        n_side_in, n_side_out = (len(side.ins), len(side.outs)) if side else (0, 0)

        def body(*refs):
            main_out = refs[n_in + n_side_in:n_in + n_side_in + n_out]
            if has_bias:
                q_ref, k_ref, v_ref, do_ref, st_ref, ct_ref = refs[:n_in]
                dq_ref, dk_ref, dv_ref, dc_ref, dr_ref = main_out
            else:
                q_ref, k_ref, v_ref, do_ref, st_ref = refs[:n_in]
                dq_ref, dk_ref, dv_ref = main_out
            jb = pl.program_id(1)
            k0 = jb * TK
            if side:
                side_refs = (refs[n_in:n_in + n_side_in], refs[n_in + n_side_in + n_out:n_in + n_side_in + n_out + n_side_out],
                             refs[-2], refs[-1])

                @pl.when((pl.program_id(0) == 0) & (jb == 0))
                def _():
                    side.start(*side_refs)

            @pl.when(jb == 0)
            def _():
                dq_ref[...] = jnp.zeros_like(dq_ref)
                if has_bias:
                    dr_ref[...] = jnp.zeros_like(dr_ref)

            dk_ref[...] = jnp.zeros_like(dk_ref)
            dv_ref[...] = jnp.zeros_like(dv_ref)
            kh = _heads(k_ref[...], packed)
            k_both = jnp.concatenate(kh, axis=0) if packed else None
            v_both = jnp.concatenate(_heads(v_ref[...], True), axis=0)
            if has_bias:
                ct = jnp.transpose(ct_ref[0, 0])
                ck = jnp.concatenate([ct[:, 0:1], ct[:, 1:2]], axis=0)

            def products(ib):
                rows = pl.ds(pl.multiple_of(ib * TQ, TQ), TQ)
                q2, do2 = q_ref[rows, :], do_ref[rows, :]
                if packed:
                    s = _nt(k_both, q2)
                else:
                    qh = _heads(q2, False)
                    s = jnp.concatenate([_nt(kh[0], qh[0]), _nt(kh[1], qh[1])], axis=0)
                return s, _nt(v_both, do2)

            def update(ib, s, dp, carry, masked):
                q0 = pl.multiple_of(ib * TQ, TQ)
                rows = pl.ds(q0, TQ)
                q2, do2 = q_ref[rows, :], do_ref[rows, :]
                st = jnp.transpose(st_ref[rows, :])
                if not packed:
                    qh = _heads(q2, False)
                if has_bias:
                    s = s - ck
                if masked:
                    allow = jnp.transpose(_allowed(q0, k0, chunked))
                    s = jnp.where(jnp.concatenate([allow, allow], axis=0), s, NEG)
                p = jnp.concatenate([jnp.exp(s[:TK] - st[0:1, :]), jnp.exp(s[TK:] - st[1:2, :])], axis=0)
                dv2 = _nn(p.astype(BF16), do2)
                ds = jnp.concatenate([p[:TK] * (dp[:TK] - st[2:3, :]), p[TK:] * (dp[TK:] - st[3:4, :])], axis=0)
                dsb = ds.astype(BF16)
                dv_ref[...] += _merge(dv2[:TK], dv2[TK:])
                if packed:
                    dk2 = _nn(dsb, q2)
                    dk_ref[...] += _merge(dk2[:TK], dk2[TK:])
                    dq_ref[rows, :] += _tn(dsb, k_both)
                else:
                    dk_ref[...] += jnp.concatenate([_nn(dsb[:TK], qh[0]), _nn(dsb[TK:], qh[1])], axis=1)
                    dq_ref[rows, :] += jnp.concatenate([_tn(dsb[:TK], kh[0]), _tn(dsb[TK:], kh[1])], axis=1)
                if has_bias:
                    sub = lax.broadcasted_iota(jnp.int32, (8, TQ), 0)
                    r0 = jnp.sum(ds[:TK], axis=0, keepdims=True)
                    r1 = jnp.sum(ds[TK:], axis=0, keepdims=True)
                    dr_ref[0, ib] += jnp.where(sub == 0, r0, jnp.where(sub == 1, r1, 0.0))
                    return carry - jnp.sum(ds, axis=1, keepdims=True)
                return carry

            def step(ib, carry, masked):
                s, dp = products(ib)
                return update(ib, s, dp, carry, masked)

            def two(ib, carry, first_masked):
                s_a, dp_a = products(ib)
                s_b, dp_b = products(ib + 1)
                return update(ib + 1, s_b, dp_b, update(ib, s_a, dp_a, carry, first_masked), False)

            def three(ib, carry):
                prods = [products(ib + u) for u in range(3)]
                for u in range(3):
                    carry = update(ib + u, prods[u][0], prods[u][1], carry, u == 0)
                return carry

            n_blocks = n_q - jb
            taken = jnp.where(n_blocks == 1, 1, 2 + n_blocks % 2)
            dc = lax.switch(taken - 1, [lambda cr: step(jb, cr, True), lambda cr: two(jb, cr, True), lambda cr: three(jb, cr)],
                            jnp.zeros((2 * TK, 1), F32))
            dc = lax.fori_loop(0, (n_blocks - taken) // 2, lambda t, cr: two(jb + taken + 2 * t, cr, False), dc)
            if has_bias:
                lane = _lane((TK, LANES))
                dc_ref[0] = jnp.where(lane == 0, dc[:TK], jnp.where(lane == 1, dc[TK:], 0.0))
            if side:
                @pl.when((pl.program_id(0) == n_pairs - 1) & (jb == S // TK - 1))
                def _():
                    side.wait(*side_refs)

        in_specs = [pl.BlockSpec((S, wq), lambda p, j: (0, q_blk0 + p)),
                    pl.BlockSpec((TK, wq), lambda p, j: (j, k_blk0 + p)),
                    pl.BlockSpec((TK, LANES), lambda p, j: (j, v_blk0 + p)),
                    pl.BlockSpec((S, LANES), lambda p, j: (0, p)),
                    pl.BlockSpec((S, LANES), lambda p, j: (0, p))]
        args = [q, k, v, do, stats]
        out_specs = [pl.BlockSpec((S, wq), lambda p, j: (0, p)),
                     pl.BlockSpec((TK, wq), lambda p, j: (j, p)),
                     pl.BlockSpec((TK, LANES), lambda p, j: (j, p))]
        out_shape = [jax.ShapeDtypeStruct((S, n_pairs * wq), F32), jax.ShapeDtypeStruct((S, n_pairs * wq), F32),
                     jax.ShapeDtypeStruct((S, n_pairs * LANES), F32)]
        if has_bias:
            in_specs += [pl.BlockSpec((1, 1, 8, TK), lambda p, j: (p, j, 0, 0))]
            args += [cumt]
            out_specs += [pl.BlockSpec((1, TK, LANES), lambda p, j: (p, j, 0)),
                          pl.BlockSpec((1, S // TQ, 8, TQ), lambda p, j: (p, 0, 0, 0))]
            out_shape += [jax.ShapeDtypeStruct((n_pairs, S, LANES), F32),
                          jax.ShapeDtypeStruct((n_pairs, S // TQ, 8, TQ), F32)]
        extra = {}
        if side:
            hbm = pl.BlockSpec(memory_space=pl.ANY)
            in_specs += [hbm] * n_side_in
            args += side.ins
            out_specs += [hbm] * n_side_out
            out_shape += side.outs
            extra = dict(scratch_shapes=[pltpu.SemaphoreType.DMA((side.n_sems,)), pltpu.SemaphoreType.DMA((side.n_sems,))],
                         input_output_aliases={n_in + a: n_out + b for a, b in side.aliases.items()})
        outs = pl.pallas_call(
            body, name=name, grid=(n_pairs, S // TK), in_specs=in_specs, out_specs=out_specs, out_shape=out_shape,
            compiler_params=_params(("arbitrary", "arbitrary")), **extra)(*args)
        return (*outs[:n_out], list(outs[n_out:])) if side else outs

    def fox_post(dcum, z, bf_pad):
        def rev(i):
            return n_tok - 1 - i

        def body(dc_ref, z_ref, b_ref, dff_ref, acc_ref, carry):
            @pl.when(pl.program_id(0) == 0)
            def _():
                carry[...] = jnp.zeros_like(carry)
                acc_ref[...] = jnp.zeros_like(acc_ref)

            _, sig_neg = _log_f_terms(z_ref[...], b_ref[...])
            row = lax.broadcasted_iota(jnp.int32, (TM, TM), 0)
            col = lax.broadcasted_iota(jnp.int32, (TM, TM), 1)
            tri = (col >= row).astype(F32)
            dlog = jnp.dot(tri, dc_ref[...], precision=lax.Precision.HIGHEST, preferred_element_type=F32) + carry[...]
            carry[...] = dlog[0:1, :]
            dff = dlog * sig_neg
            dff_ref[...] = dff
            acc_ref[0:1, :] += jnp.sum(dff, axis=0, keepdims=True)

        return pl.pallas_call(
            body, name="fox_post", grid=(n_tok,),
            in_specs=[pl.BlockSpec((TM, LANES), lambda i: (rev(i), 0)),
                      pl.BlockSpec((TM, LANES), lambda i: (rev(i), misc_blk)), const((1, LANES))],
            out_specs=[pl.BlockSpec((TM, LANES), lambda i: (rev(i), 0)), const((8, LANES))],
            out_shape=[jax.ShapeDtypeStruct((S, LANES), F32), jax.ShapeDtypeStruct((8, LANES), F32)],
            scratch_shapes=[pltpu.VMEM((1, LANES), F32)],
            compiler_params=_params(("arbitrary",)))(dcum, z, bf_pad)

    def mla_post(dq, dk, dv, dff, z, gq, gkv, w_uq, w_uk, w_v, l, cos_t, sin_t):
        def body(dq_ref, dk_ref, dv_ref, dff_ref, ql_ref, kvl_ref, gq_ref, gkv_ref, wq_ref, wk_ref, wv_ref,
                 c_ref, s_ref, zq_ref, zkv_ref, zms_ref, dwq_ref, dwk_ref, dwv_ref, dgq_ref, dgkv_ref):
            @pl.when(pl.program_id(0) == 0)
            def _():
                for r in (dwq_ref, dwk_ref, dwv_ref, dgq_ref, dgkv_ref):
                    r[...] = jnp.zeros_like(r)

            cos1, sin1 = c_ref[...], s_ref[...]
            gqv, gkvv = gq_ref[...], gkv_ref[...]
            qn, qxh, qrstd = _rms(ql_ref[...], gqv)
            dq_pre = _rope_t(dq_ref[...] * MLA_SCALE, jnp.tile(cos1, (1, MH)), jnp.tile(sin1, (1, MH))).astype(BF16)
            dwq_ref[...] += _tn(qn.astype(BF16), dq_pre)
            dql, dgq = _rms_bwd(_nt(dq_pre, wq_ref[...]), gqv, qxh, qrstd)
            zq_ref[...] = dql.astype(BF16)
            dgq_ref[0:1, :] += dgq

            dkv = dk_ref[...]
            lane = _lane(dkv.shape) & (HEAD_PAD - 1)
            dkn = jnp.where(lane < HEAD_DIM, dkv, 0.0).astype(BF16)
            dkr = dkv[:, 0:HEAD_PAD]
            for hd in range(1, MH):
                dkr = dkr + dkv[:, hd * HEAD_PAD:(hd + 1) * HEAD_PAD]
            lane1 = _lane(dkr.shape)
            dkr = jnp.where((lane1 >= ROPE_LO) & (lane1 < ROPE_LO + ROPE_DIM), dkr, 0.0)
            dkr = _rope_t(dkr, cos1, sin1)
            zms_ref[...] = (dkr + dff_ref[...]).astype(BF16)

            kvn, kxh, krstd = _rms(kvl_ref[...], gkvv)
            kvb = kvn.astype(BF16)
            dvb = dv_ref[...].astype(BF16)
            dwk_ref[...] += _tn(kvb, dkn)
            dwv_ref[...] += _tn(kvb, dvb)
            dkvl, dgkv = _rms_bwd(_nt(dkn, wk_ref[...]) + _nt(dvb, wv_ref[...]), gkvv, kxh, krstd)
            zkv_ref[...] = dkvl.astype(BF16)
            dgkv_ref[0:1, :] += dgkv

        return pl.pallas_call(
            body, name="mla_post", grid=(n_tok,),
            in_specs=[tok(QW), tok(QW), tok(MW), tok(LANES), tok(QL, (cfg.o_ql - ZO) // QL), tok(KVL, (cfg.o_kv - ZO) // KVL),
                      const((1, QL)), const((1, KVL)), layer(l, (QL, QW)), layer(l, (KVL, QW)), layer(l, (KVL, MW)),
                      tok(LANES), tok(LANES)],
            out_specs=[tok(QL), tok(KVL), tok(LANES), const((QL, QW)), const((KVL, QW)), const((KVL, MW)),
                       const((8, QL)), const((8, KVL))],
            out_shape=[jax.ShapeDtypeStruct((S, QL), BF16), jax.ShapeDtypeStruct((S, KVL), BF16),
                       jax.ShapeDtypeStruct((S, LANES), BF16), jax.ShapeDtypeStruct((QL, QW), F32),
                       jax.ShapeDtypeStruct((KVL, QW), F32), jax.ShapeDtypeStruct((KVL, MW), F32),
                       jax.ShapeDtypeStruct((8, QL), F32), jax.ShapeDtypeStruct((8, KVL), F32)],
            compiler_params=_params(("arbitrary",)))(dq, dk, dv, dff, z, z, gq, gkv, w_uq, w_uk, w_v, cos_t, sin_t)

    def bwd_in(dz, w_in, l, x, dxn, g, scale):
        def body(dz_ref, w_ref, x_ref, dx_ref, g_ref, sc_ref, o_ref, acc_ref):
            @pl.when(pl.program_id(0) == 0)
            def _():
                acc_ref[...] = jnp.zeros_like(acc_ref)

            dh = _nt(dz_ref[...], w_ref[...])
            gv, mod = g_ref[...], 1.0 + sc_ref[...]
            _, xh, rstd = _rms(x_ref[...], gv)
            t = dh * xh
            acc_ref[0:1, :] += jnp.sum(dh, axis=0, keepdims=True)
            acc_ref[1:2, :] += jnp.sum(t * gv, axis=0, keepdims=True)
            acc_ref[2:3, :] += jnp.sum(t * mod, axis=0, keepdims=True)
            dx, _ = _rms_bwd(dh, gv * mod, xh, rstd)
            o_ref[...] = dx_ref[...] + dx

        return pl.pallas_call(
            body, name="bwd_in", grid=(n_tok,),
            in_specs=[tok(NZ), layer(l, (D, NZ)), tok(D), tok(D), const((1, D)), const((1, D))],
            out_specs=[tok(D), const((8, D))],
            out_shape=[jax.ShapeDtypeStruct((S, D), F32), jax.ShapeDtypeStruct((8, D), F32)],
            compiler_params=_params(("arbitrary",)))(dz, w_in, x, dxn, g, scale)


    def pair_rows(cum):
        n_pairs = FH // 2
        ct = jnp.pad(cum[:, :FH].T.reshape(n_pairs, 2, S), ((0, 0), (0, 6), (0, 0)))
        return ct.reshape(n_pairs, 8, S // TK, TK).transpose(0, 2, 1, 3)

    def bias_grad(dc, dr):
        n_pairs = FH // 2
        d = dr.transpose(0, 2, 1, 3).reshape(n_pairs, 8, S)[:, :2, :].reshape(FH, S).T
        d = d + dc[:, :, :2].transpose(1, 0, 2).reshape(S, FH)
        return jnp.pad(d, ((0, 0), (0, LANES - FH)))

    def layer_forward(x, wl, mod, later_shards=None):
        shift, scale, gate = mod
        h, z, qkv = ln_inproj(x, wl.norm_g, scale, shift, wl.w_in, wl.l)
        cum, tiles = fox_prep(z, wl.bf_pad)
        ct = pair_rows(cum)
        fox = attn_fwd(qkv, qkv, qkv, cfg.o_fq // LANES, cfg.o_fk // LANES, cfg.o_fv // LANES, tiles, True, False,
                       "fox_fwd_gather" if later_shards else "fox_fwd", GatherOverIci(later_shards) if later_shards else None)
        of, lse_f = fox[0], fox[1]
        qp, kp, vp = mla_prep(z, wl.gq, wl.gkv, wl.w_uq, wl.w_uk, wl.w_v, wl.l, cfg.cos_t, cfg.sin_t)
        mla = attn_fwd(qp, kp, vp, 0, 0, 0, None, False, True, "mla_fwd_gather" if later_shards else "mla_fwd",
                       GatherToSibling(fox[2]) if later_shards else None)
        om, lse_m = mla[0], mla[1]
        xn, u, y = gate_outproj(of, om, z, wl.w_out, wl.l, x, gate)
        saved = types.SimpleNamespace(x=x, h=h, z=z, qkv=qkv, ct=ct, of=of, lse_f=lse_f, qp=qp, kp=kp, vp=vp,
                                      om=om, lse_m=lse_m, u=u, y=y)
        return (xn, saved, mla[2]) if later_shards else (xn, saved)

    def layer_backward(dxn, sv, wl, mod, side=None):
        shift, scale, gate = mod
        do_f, do_m, dfg, dmg, dw_out, acc_o, st_f, st_m = bwd_out(dxn, gate, sv.y, wl.w_out, wl.l, sv.of, sv.om, sv.z,
                                                                  sv.lse_f, sv.lse_m, sv.u)
        dfq, dfk, dfv, dck, dcr, *rode = attn_bwd(sv.qkv, sv.qkv, sv.qkv, do_f, st_f, cfg.o_fq // LANES, cfg.o_fk // LANES,
                                                  cfg.o_fv // LANES, sv.ct, True, False,
                                                  "fox_bwd_exchange" if side else "fox_bwd", side)
        dff, acc_f = fox_post(bias_grad(dck, dcr), sv.z, wl.bf_pad)
        dqp, dkp, dvp = attn_bwd(sv.qp, sv.kp, sv.vp, do_m, st_m, 0, 0, 0, None, False, True, "mla_bwd")
        zq, zkv, zms, dw_uq, dw_uk, dw_v, dgq, dgkv = mla_post(
            dqp, dkp, dvp, dff, sv.z, wl.gq, wl.gkv, wl.w_uq, wl.w_uk, wl.w_v, wl.l, cfg.cos_t, cfg.sin_t)
        dz = jnp.concatenate([(dfq * FOX_SCALE).astype(BF16), dfk.astype(BF16), dfv.astype(BF16), dfg, dmg, zq, zkv, zms], axis=1)
        dx, acc_i = bwd_in(dz, wl.w_in, wl.l, sv.x, dxn, wl.norm_g, scale)
        dw_in = matmul_tn(sv.h, dz, "dw_in")
        grads = types.SimpleNamespace(
            w_in=dw_in, w_out=dw_out, w_uq=dw_uq, w_uk=dw_uk, w_v=dw_v, gq=dgq[0], gkv=dgkv[0],
            b_f=acc_f[0, :FH], norm_g=acc_i[2], dmod=jnp.concatenate([acc_i[0], acc_i[1], acc_o[0]]))
        return (dx, grads, rode[0]) if side else (dx, grads)

    return types.SimpleNamespace(layer_forward=layer_forward, layer_backward=layer_backward, final_loss=final_loss)


def _pack_rows(parts, dtype, row_multiple):
    flat = jnp.concatenate([p.reshape(-1).astype(dtype) for p in parts])
    per = SLAB_COLS * row_multiple
    total = -(-flat.shape[0] // per) * per
    return jnp.pad(flat, (0, total - flat.shape[0])).reshape(total // SLAB_COLS, SLAB_COLS)


def _unpack(flat, shapes):
    out, off = [], 0
    for shp in shapes:
        n = 1
        for d in shp:
            n *= d
        out.append(flat[off:off + n].reshape(shp))
        off += n
    return out


def kernel(x, c, positions, norm_g, w_ada, b_ada, w_in, b_f, q_norm_g, w_uq, kv_norm_g, w_ukv, w_out, final_g, loss_target, m_norm_g, m_w_ada, m_b_ada, m_w_in, m_b_f, m_q_norm_g, m_w_uq, m_kv_norm_g, m_w_ukv, m_w_out, m_final_g, v_norm_g, v_w_ada, v_b_ada, v_w_in, v_b_f, v_q_norm_g, v_w_uq, v_kv_norm_g, v_w_ukv, v_w_out, v_final_g):
    S, D = x.shape[1], x.shape[2]
    L = norm_g.shape[0]
    FH = b_f.shape[1]
    QL, KVL = q_norm_g.shape[1], kv_norm_g.shape[1]
    MH = w_ukv.shape[2] * N_CHIPS // (2 * HEAD_DIM)
    FW, MW = FH * HEAD_DIM, MH * HEAD_DIM
    NA = w_ada.shape[2]
    n_in = w_in.shape[2] * N_CHIPS
    cfg = types.SimpleNamespace(S=S, D=D, FW=FW, MW=MW, QL=QL, KVL=KVL, FH=FH, MH=MH)
    cfg.o_fq, cfg.o_fk, cfg.o_fv, cfg.o_fg, cfg.o_mg = 0, FW, 2 * FW, 3 * FW, 4 * FW
    cfg.o_ql = 4 * FW + MW
    cfg.o_kv = cfg.o_ql + QL
    cfg.o_ms = cfg.o_kv + KVL
    cfg.NZ = cfg.o_ms + LANES
    assert FW == MW and FH % 2 == 0 and MH % 2 == 0 and cfg.o_ql % QL == 0 and cfg.o_kv % KVL == 0 and KVL == LANES
    assert n_in == 4 * FW + FH + QL + KVL + ROPE_DIM + MW

    mx, my, mc = _my_pos()
    my_chip = 2 * mx + my
    my_dev = 2 * my_chip + mc

    inv_freq = 1.0 / (ROPE_THETA ** (jnp.arange(0, ROPE_DIM, 2, dtype=F32) / ROPE_DIM))
    ang = positions[0].astype(F32)[:, None] * inv_freq
    cos, sin = jnp.cos(ang), jnp.sin(ang)
    cfg.cos_t = jnp.concatenate([jnp.ones((S, ROPE_LO), F32), cos, cos, jnp.ones((S, HEAD_PAD - ROPE_LO - ROPE_DIM), F32)], axis=1)
    cfg.sin_t = jnp.concatenate([jnp.zeros((S, ROPE_LO), F32), -sin, sin, jnp.zeros((S, HEAD_PAD - ROPE_LO - ROPE_DIM), F32)], axis=1)

    assert L % 4 == 0
    def lane_pad(a):
        return jnp.pad(a, ((0, 0),) * (a.ndim - 1) + ((0, -a.shape[-1] % LANES),))

    shards = (w_in, w_uq, w_ukv, w_out)
    padded = [lane_pad(w.astype(BF16)) for w in shards]

    def kernel_layouts(gathered, own):
        def all_chips(g, w, axis):
            return jnp.concatenate([jnp.where(my_chip == t, w.astype(BF16), g[t][..., :w.shape[-1]])
                                    for t in range(N_CHIPS)], axis=axis)

        n = own[0].shape[0]
        w_in_f = all_chips(gathered[0], own[0], 2)
        w_uq_f = all_chips(gathered[1], own[1], 2)
        w_ukv_f = all_chips(gathered[2], own[2], 2)
        w_out_f = all_chips(gathered[3], own[3], 1)
        sizes = (FW, FW, FW, FH, FW, QL, KVL, ROPE_DIM, MW)
        offs = [0]
        for sz in sizes:
            offs.append(offs[-1] + sz)
        fq_w, fk_w, fv_w, ff_w, fg_w, ql_w, kvl_w, kr_w, mg_w = [w_in_f[:, :, offs[i]:offs[i + 1]] for i in range(len(sizes))]
        zeros = lambda width: jnp.zeros((n, D, width), BF16)
        w_in_p = jnp.concatenate([fq_w, fk_w, fv_w, fg_w, mg_w, ql_w, kvl_w, ff_w, zeros(ROPE_LO - FH), kr_w,
                                  zeros(HEAD_PAD - ROPE_LO - ROPE_DIM)], axis=2)
        w_uq_p = jnp.pad(w_uq_f.reshape(n, QL, MH, HEAD_DIM + ROPE_DIM),
                         ((0, 0), (0, 0), (0, 0), (0, HEAD_PAD - HEAD_DIM - ROPE_DIM))).reshape(n, QL, MH * HEAD_PAD)
        w_ukv4 = w_ukv_f.reshape(n, KVL, MH, 2 * HEAD_DIM)
        w_uk_p = jnp.pad(w_ukv4[..., :HEAD_DIM], ((0, 0), (0, 0), (0, 0), (0, HEAD_PAD - HEAD_DIM))).reshape(n, KVL, MH * HEAD_PAD)
        return types.SimpleNamespace(w_in=w_in_p, w_uq=w_uq_p, w_uk=w_uk_p, w_v=w_ukv4[..., HEAD_DIM:].reshape(n, KVL, MW),
                                     w_out=w_out_f)

    first = kernel_layouts(weights_gather([p[:1] for p in padded], "weights_first"), [w[:1] for w in shards])

    c_all = allgather8(c.reshape(8, D // 8), "gather_c").reshape(N_DEV, D)
    c_pad = jnp.pad(c_all, ((0, 16 - N_DEV), (0, 0)))
    mod_part = ada_forward(c_pad, w_ada)[:, :N_DEV, :]
    mod_all = allgather8(mod_part.reshape(-1, LANES), "gather_mod").reshape(N_CHIPS, 2, L, N_DEV, NA)[:, 0]
    mod_full = mod_all.transpose(1, 2, 0, 3).reshape(L, N_DEV, N_CHIPS * NA) + b_ada[:, None, :]
    mod_mine = lax.dynamic_index_in_dim(mod_full, my_dev, axis=1, keepdims=True)

    step = make_step(cfg)
    bf_pad = jnp.pad(b_f, ((0, 0), (0, LANES - FH)))
    mods = [(mod_mine[l, :, :D], mod_mine[l, :, D:2 * D], mod_mine[l, :, 2 * D:]) for l in range(L)]

    def layer_params(l, ws_, at):
        return types.SimpleNamespace(l=at, norm_g=norm_g[l][None], bf_pad=bf_pad[l][None], gq=q_norm_g[l][None],
                                     gkv=kv_norm_g[l][None], w_in=ws_.w_in, w_uq=ws_.w_uq, w_uk=ws_.w_uk, w_v=ws_.w_v,
                                     w_out=ws_.w_out)

    layers = [layer_params(0, first, 0)]
    xl, sv, later = step.layer_forward(x[0], layers[0], mods[0], [p[1:] for p in padded])
    saved = [sv]
    rest = kernel_layouts(later, [w[1:] for w in shards])
    for l in range(1, L):
        layers.append(layer_params(l, rest, l - 1))
        xl, sv = step.layer_forward(xl, layers[l], mods[l])
        saved.append(sv)
    dx, acc_fin, loss_part = step.final_loss(xl, final_g[None], loss_target[0])
    loss = lax.psum(loss_part[0, 0], ("x", "y", "c"))
    ms = cfg.o_ms
    runs = [(0, 3 * FW, 0), (3 * FW, FH, ms), (3 * FW + FH, FW, cfg.o_fg), (4 * FW + FH, QL + KVL, cfg.o_ql),
            (4 * FW + FH + QL + KVL, ROPE_DIM, ms + ROPE_LO), (4 * FW + FH + QL + KVL + ROPE_DIM, MW, cfg.o_mg)]
    big_names = ["w_in", "w_uq", "w_ukv", "w_out"]

    def shard_columns(dw, t):
        a, b = t * w_in.shape[2], (t + 1) * w_in.shape[2]
        return jnp.concatenate([dw[:, p0 + max(a, r0) - r0:p0 + min(b, r0 + sz) - r0]
                                for r0, sz, p0 in runs if max(a, r0) < min(b, r0 + sz)], axis=1)

    def chip_parts(group, tag):
        n = len(group)
        stk = lambda name: jnp.stack([getattr(g, name) for g in group])
        g_in4 = jnp.stack([jnp.stack([shard_columns(g.w_in, t) for g in group]) for t in range(N_CHIPS)])
        dw_uq_f = stk("w_uq").reshape(n, QL, MH, HEAD_PAD)[..., :HEAD_DIM + ROPE_DIM].reshape(n, QL, -1)
        dw_ukv_f = jnp.concatenate([stk("w_uk").reshape(n, KVL, MH, HEAD_PAD)[..., :HEAD_DIM],
                                    stk("w_v").reshape(n, KVL, MH, HEAD_DIM)], axis=3).reshape(n, KVL, -1)
        gs = [g_in4, dw_uq_f.reshape(n, QL, N_CHIPS, -1).transpose(2, 0, 1, 3),
              dw_ukv_f.reshape(n, KVL, N_CHIPS, -1).transpose(2, 0, 1, 3), stk("w_out").reshape(n, N_CHIPS, -1, D).transpose(1, 0, 2, 3)]
        gs = [lane_pad(g.astype(BF16)) for g in gs]
        theirs = halves_to_sibling(gs, "grads_sibling_" + tag)
        out = []
        for g, o, nm in zip(gs, theirs, big_names):
            keep = lax.dynamic_slice_in_dim(g, mc * (n // 2), n // 2, axis=1)
            merged = (N_CHIPS * (n // 2),) + g.shape[2:]
            out.append(add_cast(keep.reshape(merged), o.reshape(merged), BF16, "grads_chip_sum_%s_%s" % (nm, tag)).reshape(o.shape))
        return out

    def reduced(chip_part, parts, tag):
        where = jnp.stack([my_chip, mc]).astype(jnp.int32)
        red = [sum_chips(p, o, where, "grads_sum_%s_%s" % (nm, tag)) for p, o, nm in zip(parts, chip_part, big_names)]
        return [g[..., :w.shape[-1]] for g, w in zip(halves_gather(red, "grads_back_" + tag), shards)]

    gl = [None] * L
    half_l = L // 2
    for l in range(L - 1, half_l - 1, -1):
        dx, gl[l] = step.layer_backward(dx, saved[l], layers[l], mods[l])
    part_hi = chip_parts(gl[half_l:], "hi")
    dx, gl[half_l - 1], parts_hi = step.layer_backward(dx, saved[half_l - 1], layers[half_l - 1], mods[half_l - 1],
                                                       ExchangeOverIci(part_hi))
    for l in range(half_l - 2, -1, -1):
        dx, gl[l] = step.layer_backward(dx, saved[l], layers[l], mods[l])
    part_lo = chip_parts(gl[:half_l], "lo")
    parts_lo = run_side(ExchangeOverIci(part_lo), "grads_chips_lo")
    g_w_in, g_w_uq, g_w_ukv, g_w_out = [jnp.concatenate([lo, hi]) for lo, hi in
                                        zip(reduced(part_lo, parts_lo, "lo"), reduced(part_hi, parts_hi, "hi"))]
    grad_x = dx[None]

    stack = lambda name: jnp.stack([getattr(g, name) for g in gl])
    small_parts = [stack("norm_g"), stack("dmod"), stack("b_f"), stack("gq"), stack("gkv"), acc_fin[0]]
    small_shapes = [p.shape for p in small_parts]
    small = _pack_rows(small_parts, F32, 8).reshape(-1, LANES)
    small_all = allgather8(small, "gather_small").reshape(N_DEV, -1, LANES)
    small_sum = sum_leading(small_all, "sum_small")
    g_norm_g, g_b_ada, g_b_f, g_q_norm_g, g_kv_norm_g, g_final_g = _unpack(small_sum.reshape(-1), small_shapes)

    n_ng = L * D
    dmod_all = small_all.reshape(N_DEV, -1)[:, n_ng:n_ng + L * 3 * D].reshape(N_DEV, L, 3 * D)
    dmod_cols = lax.dynamic_slice_in_dim(dmod_all, my_chip * NA, NA, axis=2).transpose(1, 0, 2)
    g_w_ada = ada_backward(c_pad, jnp.pad(dmod_cols, ((0, 0), (0, 16 - N_DEV), (0, 0))))


    names = ["norm_g", "w_ada", "b_ada", "w_in", "b_f", "q_norm_g", "w_uq", "kv_norm_g", "w_ukv", "w_out", "final_g"]
    ws = dict(norm_g=norm_g, w_ada=w_ada, b_ada=b_ada, w_in=w_in, b_f=b_f, q_norm_g=q_norm_g, w_uq=w_uq,
              kv_norm_g=kv_norm_g, w_ukv=w_ukv, w_out=w_out, final_g=final_g)
    msd = dict(norm_g=m_norm_g, w_ada=m_w_ada, b_ada=m_b_ada, w_in=m_w_in, b_f=m_b_f, q_norm_g=m_q_norm_g, w_uq=m_w_uq,
               kv_norm_g=m_kv_norm_g, w_ukv=m_w_ukv, w_out=m_w_out, final_g=m_final_g)
    vsd = dict(norm_g=v_norm_g, w_ada=v_w_ada, b_ada=v_b_ada, w_in=v_w_in, b_f=v_b_f, q_norm_g=v_q_norm_g, w_uq=v_w_uq,
               kv_norm_g=v_kv_norm_g, w_ukv=v_w_ukv, w_out=v_w_out, final_g=v_final_g)
    gsd = dict(norm_g=g_norm_g, w_ada=g_w_ada, b_ada=g_b_ada, w_in=g_w_in, b_f=g_b_f, q_norm_g=g_q_norm_g, w_uq=g_w_uq,
               kv_norm_g=g_kv_norm_g, w_ukv=g_w_ukv, w_out=g_w_out, final_g=g_final_g)
    small_names = ["norm_g", "b_ada", "b_f", "q_norm_g", "kv_norm_g", "final_g"]
    sm_shapes = [ws[n].shape for n in small_names]
    pk = lambda d: _pack_rows([d[n] for n in small_names], F32, 8).reshape(1, -1, LANES)
    sm_out = adamw(pk(ws), pk(gsd), pk(msd), pk(vsd), "adamw_small")
    sm_d, sm_m, sm_v = [dict(zip(small_names, _unpack(o.reshape(-1), sm_shapes))) for o in sm_out]
    delta, new_m, new_v = dict(sm_d), dict(sm_m), dict(sm_v)
    for n in ["w_ada", "w_in", "w_uq", "w_ukv", "w_out"]:
        delta[n], new_m[n], new_v[n] = adamw(ws[n], gsd[n], msd[n], vsd[n], "adamw_" + n)

    return (loss, grad_x, *[gsd[n] for n in names], *[delta[n] for n in names],
            *[new_m[n] for n in names], *[new_v[n] for n in names])
```
